```python
import math
import jax, jax.numpy as jnp
from jax import lax
import numpy as np

D_MODEL = 1024
BATCH = 8
SEQ = 2048
DEPTH = 2

CTX_LEN = 256
GRID_W = 64
D_S5 = D_MODEL // 2
S5_GROUP = 16
S5_GROUPS = D_S5 // S5_GROUP
S5_STATE = 64
D_SC = D_MODEL // 4
SC_WIDTH = 3
D_CF = D_MODEL // 4
CF_WIDTH = 31
SPLITS = (D_S5, D_S5 + D_SC, D_S5 + 2 * D_SC, D_S5 + 3 * D_SC, D_S5 + 3 * D_SC + D_CF)
D_IN = D_S5 + 3 * D_SC + 2 * D_CF
N_GROUPS = 4
EXP_PER_GROUP = 4
N_EXPERTS = N_GROUPS * EXP_PER_GROUP
D_EXPERT = D_MODEL // 4
TOP_K = 2
DN_ALPHA = (2 * DEPTH) ** 0.25
DN_BETA = (8 * DEPTH) ** -0.25
LN_EPS = 1e-5

kernel_name = 'hybrid_s5_conv_conformer_hmoe_diffusion'


def layer_norm(x, g, b):
    xf = x.astype(jnp.float32)
    mu = jnp.mean(xf, -1, keepdims=True)
    var = jnp.mean(jnp.square(xf - mu), -1, keepdims=True)
    return ((xf - mu) * lax.rsqrt(var + LN_EPS) * g.astype(jnp.float32) + b.astype(jnp.float32)).astype(x.dtype)


def modulate(x, shift, scale):
    return x * (1 + scale) + shift


def dwconv_seq(x, w):
    k = w.shape[0]
    p = (k - 1) // 2
    return lax.conv_general_dilated(x, w[:, None, :].astype(x.dtype), (1,), [(p, p)],
                                    dimension_numbers=('NWC', 'WIO', 'NWC'),
                                    feature_group_count=x.shape[-1])


def dwconv_grid(x, w):
    kh, kw = w.shape[0], w.shape[1]
    return lax.conv_general_dilated(x, w[:, :, None, :].astype(x.dtype), (1, 1),
                                    [((kh - 1) // 2, (kh - 1) // 2), ((kw - 1) // 2, (kw - 1) // 2)],
                                    dimension_numbers=('NHWC', 'HWIO', 'NHWC'),
                                    feature_group_count=x.shape[-1])


def _linear_combine(e1, e2):
    a1, b1 = e1
    a2, b2 = e2
    return a2 * a1, a2 * b1 + b2


def linear_scan(lam_bar, bu):
    a = jnp.broadcast_to(lam_bar, (1,) + bu.shape[1:])
    _, h = lax.associative_scan(_linear_combine, (a, bu), axis=1)
    return h


def zoh(a_re, a_im, log_dt, b_re, b_im):
    f32 = jnp.float32
    lam = lax.complex(a_re.astype(f32), a_im.astype(f32))
    dt = jnp.exp(log_dt.astype(f32))[:, None]
    lam_bar = jnp.exp(lam * dt)
    b = lax.complex(b_re.astype(f32), b_im.astype(f32))
    b_bar = ((lam_bar - 1.0) / lam)[..., None] * b
    return lam_bar, b_bar


def s5_mixer(u, uc, a_re, a_im, log_dt, b_re, b_im, c_re, c_im, d_skip, w_glu, b_glu, ctx_out):
    f32 = jnp.float32
    bsz, seq, _ = u.shape
    n_ctx = uc.shape[1]
    ug = u.astype(f32).reshape(bsz, seq, S5_GROUPS, S5_GROUP)
    ucg = uc.astype(f32).reshape(bsz, n_ctx, S5_GROUPS, S5_GROUP)
    d = d_skip.astype(f32)
    y = d * u.astype(f32)
    yc = d * uc.astype(f32) if ctx_out else None
    for direction in range(2):
        lam_bar, b_bar = zoh(a_re[direction], a_im[direction], log_dt[direction],
                             b_re[direction], b_im[direction])
        c_mat = lax.complex(c_re[direction].astype(f32), c_im[direction].astype(f32))
        bu_c = jnp.einsum('bsgn,gpn->bsgp', ucg, b_bar)
        bu = jnp.einsum('bsgn,gpn->bsgp', ug, b_bar)
        if direction == 1:
            bu_c = jnp.flip(bu_c, 1)
            bu = jnp.flip(bu, 1)
        h_c = linear_scan(lam_bar, bu_c)
        bu = bu.at[:, 0].add(lam_bar * h_c[:, -1])
        h = linear_scan(lam_bar, bu)
        if direction == 1:
            h = jnp.flip(h, 1)
        y = y + jnp.real(jnp.einsum('bsgp,gnp->bsgn', h, c_mat)).reshape(bsz, seq, D_S5)
        if ctx_out:
            if direction == 1:
                h_c = jnp.flip(h_c, 1)
            yc = yc + jnp.real(jnp.einsum('bsgp,gnp->bsgn', h_c, c_mat)).reshape(bsz, n_ctx, D_S5)
    wg = w_glu.astype(f32)
    bg = b_glu.astype(f32)

    def glu(t):
        t = jax.nn.gelu(t)
        return t * jax.nn.sigmoid(t @ wg + bg)

    y = glu(y).astype(u.dtype)
    yc = glu(yc).astype(uc.dtype) if ctx_out else None
    return y, yc


def mixer_sublayer(h, hc, rows, w_in, s5_a_re, s5_a_im, s5_log_dt, s5_b_re, s5_b_im, s5_c_re, s5_c_im,
                   s5_d, w_glu, b_glu, w_sc, w_dw, b_dw, ln_cf_g, ln_cf_b, w_o, ctx_out):
    bsz, seq, _ = h.shape
    z = h @ w_in
    u, b_g, c_g, v, g_a, g_b = jnp.split(z, list(SPLITS), axis=-1)
    zc = hc @ (w_in if ctx_out else w_in[:, :D_S5])
    y_s5, yc_s5 = s5_mixer(u, zc[..., :D_S5], s5_a_re, s5_a_im, s5_log_dt, s5_b_re, s5_b_im,
                           s5_c_re, s5_c_im, s5_d, w_glu, b_glu, ctx_out)
    y_sc = b_g * dwconv_grid((c_g * v).reshape(bsz, rows, GRID_W, D_SC), w_sc[None]).reshape(bsz, seq, D_SC)
    t = dwconv_grid((g_a * jax.nn.sigmoid(g_b)).reshape(bsz, rows, GRID_W, D_CF),
                    w_dw[:, None]).reshape(bsz, seq, D_CF) + b_dw
    y_cf = jax.nn.silu(layer_norm(t, ln_cf_g, ln_cf_b))
    y = jnp.concatenate([y_s5, y_sc, y_cf], axis=-1) @ w_o
    if not ctx_out:
        return y, None
    _, bc_g, cc_g, vc, gc_a, gc_b = jnp.split(zc, list(SPLITS), axis=-1)
    yc_sc = bc_g * dwconv_seq(cc_g * vc, w_sc)
    tc = dwconv_seq(gc_a * jax.nn.sigmoid(gc_b), w_dw) + b_dw
    yc_cf = jax.nn.silu(layer_norm(tc, ln_cf_g, ln_cf_b))
    yc = jnp.concatenate([yc_s5, yc_sc, yc_cf], axis=-1) @ w_o
    return y, yc


def hier_moe(h, w_rg, b_rg, w_rexp, b_rexp, w_gate, w_up, w_down):
    f32 = jnp.float32
    hf = h.astype(f32)
    g_logits = hf @ w_rg.astype(f32) + b_rg.astype(f32)
    g_idx = jnp.argmax(g_logits, axis=-1)
    g_w = jnp.take_along_axis(jax.nn.softmax(g_logits, axis=-1), g_idx[..., None], axis=-1)
    e_logits = (hf @ w_rexp.astype(f32) + b_rexp.astype(f32)).reshape(
        h.shape[:-1] + (N_GROUPS, EXP_PER_GROUP))
    e_logits = jnp.take_along_axis(e_logits, g_idx[..., None, None], axis=-2)[..., 0, :]
    top_v, top_i = lax.top_k(e_logits, TOP_K)
    comb = g_w * jax.nn.softmax(top_v, axis=-1)
    expert = g_idx[..., None] * EXP_PER_GROUP + top_i
    weights = jnp.sum(jax.nn.one_hot(expert, N_EXPERTS, dtype=f32) * comb[..., None], axis=-2)
    gate = jnp.einsum('bsd,edf->bsef', h, w_gate)
    up = jnp.einsum('bsd,edf->bsef', h, w_up)
    act = jax.nn.silu(gate) * up * weights[..., None].astype(h.dtype)
    return jnp.einsum('bsef,efd->bsd', act, w_down)


def setup_inputs(seed: int = 0) -> dict:
    key = jax.random.key(seed)
    keys = jax.random.split(key, 40)
    counter = iter(range(40))
    f32 = jnp.float32

    def nrm(shape, scale):
        return scale * jax.random.normal(keys[next(counter)], shape, f32)

    L, D = DEPTH, D_MODEL
    G, P, N = S5_GROUPS, S5_STATE, S5_GROUP
    a_im0 = jnp.pi * jnp.arange(P, dtype=f32)
    return {
        'x': nrm((BATCH, SEQ, D), 1.0),
        'c': nrm((BATCH, D), 1.0),
        'ctx': nrm((BATCH, CTX_LEN, D), 1.0),
        'c_ctx': nrm((D,), 1.0),
        'w_mod': nrm((L, D, 6 * D), 0.5 * D ** -0.5),
        'b_mod': nrm((L, 6 * D), 0.02),
        'w_in': nrm((L, D, D_IN), D ** -0.5),
        's5_a_re': -0.5 * (1.0 + nrm((L, 2, G, P), 0.02)),
        's5_a_im': a_im0 + nrm((L, 2, G, P), 0.01),
        's5_log_dt': jax.random.uniform(keys[next(counter)], (L, 2, G), f32,
                                        math.log(1e-3), math.log(1e-1)),
        's5_b_re': nrm((L, 2, G, P, N), (2 * N) ** -0.5),
        's5_b_im': nrm((L, 2, G, P, N), (2 * N) ** -0.5),
        's5_c_re': nrm((L, 2, G, N, P), 0.5),
        's5_c_im': nrm((L, 2, G, N, P), 0.5),
        's5_d': nrm((L, D_S5), 1.0),
        'w_glu': nrm((L, D_S5, D_S5), D_S5 ** -0.5),
        'b_glu': nrm((L, D_S5), 0.02),
        'w_sc': nrm((L, SC_WIDTH, D_SC), SC_WIDTH ** -0.5),
        'w_dw': nrm((L, CF_WIDTH, D_CF), CF_WIDTH ** -0.5),
        'b_dw': nrm((L, D_CF), 0.02),
        'ln_cf_g': 1.0 + nrm((L, D_CF), 0.02),
        'ln_cf_b': nrm((L, D_CF), 0.02),
        'w_o': nrm((L, D, D), DN_BETA * D ** -0.5),
        'ln1_g': 1.0 + nrm((L, D), 0.02),
        'ln1_b': nrm((L, D), 0.02),
        'w_rg': nrm((L, D, N_GROUPS), D ** -0.5),
        'b_rg': nrm((L, N_GROUPS), 0.01),
        'w_rexp': nrm((L, D, N_EXPERTS), D ** -0.5),
        'b_rexp': nrm((L, N_EXPERTS), 0.01),
        'w_gate': nrm((L, N_EXPERTS, D, D_EXPERT), D ** -0.5),
        'w_up': nrm((L, N_EXPERTS, D, D_EXPERT), D ** -0.5),
        'w_down': nrm((L, N_EXPERTS, D_EXPERT, D), DN_BETA * D_EXPERT ** -0.5),
        'ln2_g': 1.0 + nrm((L, D), 0.02),
        'ln2_b': nrm((L, D), 0.02),
    }


def reference(x, c, ctx, c_ctx, w_mod, b_mod, w_in, s5_a_re, s5_a_im, s5_log_dt, s5_b_re, s5_b_im,
              s5_c_re, s5_c_im, s5_d, w_glu, b_glu, w_sc, w_dw, b_dw, ln_cf_g, ln_cf_b, w_o,
              ln1_g, ln1_b, w_rg, b_rg, w_rexp, b_rexp, w_gate, w_up, w_down, ln2_g, ln2_b):
    rows = x.shape[1] // GRID_W
    x_lat, x_ctx = x, ctx
    for l in range(DEPTH):
        last = l == DEPTH - 1
        mod = (jax.nn.silu(c) @ w_mod[l] + b_mod[l])[:, None, :]
        mod_c = jax.nn.silu(c_ctx) @ w_mod[l] + b_mod[l]
        sh1, sc1, g1, sh2, sc2, g2 = jnp.split(mod, 6, axis=-1)
        sh1c, sc1c, g1c, sh2c, sc2c, g2c = jnp.split(mod_c, 6, axis=-1)
        y, yc = mixer_sublayer(modulate(x_lat, sh1, sc1), modulate(x_ctx, sh1c, sc1c), rows, w_in[l],
                               s5_a_re[l], s5_a_im[l], s5_log_dt[l], s5_b_re[l], s5_b_im[l],
                               s5_c_re[l], s5_c_im[l], s5_d[l], w_glu[l], b_glu[l], w_sc[l], w_dw[l],
                               b_dw[l], ln_cf_g[l], ln_cf_b[l], w_o[l], not last)
        x_lat = layer_norm(DN_ALPHA * x_lat + g1 * y, ln1_g[l], ln1_b[l])
        if not last:
            x_ctx = layer_norm(DN_ALPHA * x_ctx + g1c * yc, ln1_g[l], ln1_b[l])
            f_ctx = hier_moe(modulate(x_ctx, sh2c, sc2c), w_rg[l], b_rg[l], w_rexp[l], b_rexp[l],
                             w_gate[l], w_up[l], w_down[l])
            x_ctx = layer_norm(DN_ALPHA * x_ctx + g2c * f_ctx, ln2_g[l], ln2_b[l])
        f = hier_moe(modulate(x_lat, sh2, sc2), w_rg[l], b_rg[l], w_rexp[l], b_rexp[l],
                     w_gate[l], w_up[l], w_down[l])
        x_lat = layer_norm(DN_ALPHA * x_lat + g2 * f, ln2_g[l], ln2_b[l])
    return x_lat
```

```python
import functools
import math

import jax
import jax.numpy as jnp
from jax import lax
from jax.experimental import pallas as pl
from jax.experimental.pallas import tpu as pltpu

F32 = jnp.float32
BF16 = jnp.bfloat16

D_MODEL = 1024
DEPTH = 2
GRID_W = 64
D_S5 = 512
S5_GROUP = 16
S5_GROUPS = 32
S5_STATE = 64
D_SC = 256
D_CF = 256
CF_WIDTH = 31
CF_HALF = 15
D_IN = 1792
N_GROUPS = 4
EXP_PER_GROUP = 4
N_EXPERTS = 16
D_EXPERT = 256
DN_ALPHA = (2 * DEPTH) ** 0.25
LN_EPS = 1e-5

CHUNK = 16
TILE_CHUNKS = 16
TILE_TOK = CHUNK * TILE_CHUNKS
LANE_BLK = 128
GRP_PER_BLK = LANE_BLK // S5_GROUP
ROUTER_LANES = 128
VMEM_LIMIT = 56 * 1024 * 1024


def _cparams(sem):
    return pltpu.CompilerParams(dimension_semantics=sem, vmem_limit_bytes=VMEM_LIMIT)


def _split_bf16(a):
    hi = a.astype(BF16)
    lo = (a - hi.astype(F32)).astype(BF16)
    return hi, lo


def _dot(a, b):
    return jnp.dot(a, b, preferred_element_type=F32)


def _dot3(a, b):
    ah, al = _split_bf16(a)
    bh, bl = _split_bf16(b)
    return _dot(ah, bh) + (_dot(al, bh) + _dot(ah, bl))


def _sigmoid(x):
    return 1.0 / (1.0 + jnp.exp(-x))


def _layer_norm(x, g, b):
    mu = jnp.mean(x, axis=-1, keepdims=True)
    xc = x - mu
    var = jnp.mean(xc * xc, axis=-1, keepdims=True)
    return xc * lax.rsqrt(var + LN_EPS) * g + b


def _mod_kernel(c_ref, w_ref, b_ref, o_ref):
    c = c_ref[...]
    s = c * _sigmoid(c)
    o_ref[0] = _dot3(s, w_ref[0]) + b_ref[0]


def _mod_call(c_all, w_mod, b_mod):
    n_layers, d, n_out = w_mod.shape
    tn = 1536
    rows = c_all.shape[0]
    return pl.pallas_call(
        _mod_kernel,
        grid=(n_layers, n_out // tn),
        in_specs=[
            pl.BlockSpec((rows, d), lambda l, j: (0, 0)),
            pl.BlockSpec((1, d, tn), lambda l, j: (l, 0, j)),
            pl.BlockSpec((1, 1, tn), lambda l, j: (l, 0, j)),
        ],
        out_specs=pl.BlockSpec((1, rows, tn), lambda l, j: (l, 0, j)),
        out_shape=jax.ShapeDtypeStruct((n_layers, rows, n_out), F32),
        compiler_params=_cparams(("parallel", "parallel")),
        name="mod",
    )(c_all, w_mod, b_mod.reshape(n_layers, 1, n_out))


def _in_kernel(x_ref, mod_ref, w_ref, u_ref, bg_ref, cv_ref, gl_ref):
    x = x_ref[0]
    sh = mod_ref[0, 0:1, :]
    sc = mod_ref[0, 1:2, :]
    h = (x * (1.0 + sc) + sh).astype(BF16)
    z = _dot(h, w_ref[...])
    u_ref[0] = z[:, 0:512]
    bg_ref[0] = z[:, 512:768]
    cv_ref[0] = z[:, 768:1024] * z[:, 1024:1280]
    gl_ref[0] = z[:, 1280:1536] * _sigmoid(z[:, 1536:1792])


def _in_u_kernel(x_ref, mod_ref, w_ref, u_ref):
    x = x_ref[0]
    sh = mod_ref[0, 0:1, :]
    sc = mod_ref[0, 1:2, :]
    h = (x * (1.0 + sc) + sh).astype(BF16)
    u_ref[0] = _dot(h, w_ref[...])


def _in_call(x, mod, w_in_bf, tm, u_only):
    b, s, d = x.shape
    grid = (b, s // tm)
    row_spec = lambda n: pl.BlockSpec((1, tm, n), lambda i, j: (i, j, 0))
    in_specs = [
        row_spec(d),
        pl.BlockSpec((1, 6, d), lambda i, j: (i, 0, 0)),
    ]
    if u_only:
        in_specs.append(pl.BlockSpec((d, D_S5), lambda i, j: (0, 0)))
        return pl.pallas_call(
            _in_u_kernel, grid=grid, in_specs=in_specs,
            out_specs=row_spec(D_S5),
            out_shape=jax.ShapeDtypeStruct((b, s, D_S5), F32),
            compiler_params=_cparams(("parallel", "parallel")),
            name="in_proj_u",
        )(x, mod, w_in_bf)
    in_specs.append(pl.BlockSpec((d, D_IN), lambda i, j: (0, 0)))
    return pl.pallas_call(
        _in_kernel, grid=grid, in_specs=in_specs,
        out_specs=[row_spec(D_S5), row_spec(D_SC), row_spec(D_SC), row_spec(D_CF)],
        out_shape=[jax.ShapeDtypeStruct((b, s, D_S5), F32),
                   jax.ShapeDtypeStruct((b, s, D_SC), F32),
                   jax.ShapeDtypeStruct((b, s, D_SC), F32),
                   jax.ShapeDtypeStruct((b, s, D_CF), F32)],
        compiler_params=_cparams(("parallel", "parallel")),
        name="in_proj",
    )(x, mod, w_in_bf)


def _conv_tail(t, bdw_ref, lng_ref, lnb_ref):
    t = t + bdw_ref[...]
    t = _layer_norm(t, lng_ref[...], lnb_ref[...])
    return t * _sigmoid(t)


def _conv_grid_kernel(bg_ref, cv_ref, gl_ref, wsc_ref, wdw_ref, bdw_ref, lng_ref, lnb_ref,
                      ysc_ref, ycf_ref, pad_ref, t_ref):
    s = cv_ref.shape[1]
    rows = s // GRID_W
    cv = cv_ref[0]
    col = lax.broadcasted_iota(jnp.int32, (s, D_SC), 0) % GRID_W
    prev = jnp.where(col == 0, 0.0, pltpu.roll(cv, 1, axis=0))
    nxt = jnp.where(col == GRID_W - 1, 0.0, pltpu.roll(cv, s - 1, axis=0))
    conv = prev * wsc_ref[0:1, :] + cv * wsc_ref[1:2, :] + nxt * wsc_ref[2:3, :]
    ysc_ref[0] = (bg_ref[0] * conv).astype(ysc_ref.dtype)

    zero = jnp.zeros((CF_HALF, GRID_W, D_CF), F32)
    pad_ref[0:CF_HALF] = zero
    pad_ref[CF_HALF + rows:CF_HALF + rows + CF_HALF] = zero
    pad_ref[CF_HALF:CF_HALF + rows] = gl_ref[0].reshape(rows, GRID_W, D_CF)

    def body(i, carry):
        w0 = pl.multiple_of(i * 8, 8)
        for half in range(D_CF // 128):
            lanes = slice(half * 128, (half + 1) * 128)
            acc = jnp.zeros((rows, 8, 128), F32)
            for k in range(CF_WIDTH):
                acc = acc + pad_ref[k:k + rows, pl.ds(w0, 8), lanes] * wdw_ref[k:k + 1, lanes]
            t_ref[:, pl.ds(w0, 8), lanes] = acc
        return carry

    lax.fori_loop(0, GRID_W // 8, body, 0)
    t = t_ref[...].reshape(s, D_CF)
    ycf_ref[0] = _conv_tail(t, bdw_ref, lng_ref, lnb_ref).astype(ycf_ref.dtype)


def _conv_seq_kernel(bg_ref, cv_ref, gl_ref, wsc_ref, wdw_ref, bdw_ref, lng_ref, lnb_ref,
                     ysc_ref, ycf_ref, pad_ref):
    s = cv_ref.shape[1]
    cv = cv_ref[0]
    pos = lax.broadcasted_iota(jnp.int32, (s, D_SC), 0)
    prev = jnp.where(pos == 0, 0.0, pltpu.roll(cv, 1, axis=0))
    nxt = jnp.where(pos == s - 1, 0.0, pltpu.roll(cv, s - 1, axis=0))
    conv = prev * wsc_ref[0:1, :] + cv * wsc_ref[1:2, :] + nxt * wsc_ref[2:3, :]
    ysc_ref[0] = (bg_ref[0] * conv).astype(ysc_ref.dtype)

    off = 16
    pad_ref[0:off] = jnp.zeros((off, D_CF), F32)
    pad_ref[off + s:off + s + 16] = jnp.zeros((16, D_CF), F32)
    pad_ref[off:off + s] = gl_ref[0]
    acc = jnp.zeros((s, D_CF), F32)
    for k in range(CF_WIDTH):
        acc = acc + pad_ref[pl.ds(off - CF_HALF + k, s), :] * wdw_ref[k:k + 1, :]
    ycf_ref[0] = _conv_tail(acc, bdw_ref, lng_ref, lnb_ref).astype(ycf_ref.dtype)


def _conv_call(bg, cv, gl, w_sc, w_dw, b_dw, ln_g, ln_b, grid_mode):
    b, s, _ = bg.shape
    row_spec = pl.BlockSpec((1, s, D_SC), lambda i: (i, 0, 0))
    full = lambda shape: pl.BlockSpec(shape, lambda i: (0,) * len(shape))
    if grid_mode:
        rows = s // GRID_W
        kern = _conv_grid_kernel
        scratch = [pltpu.VMEM((rows + 2 * CF_HALF, GRID_W, D_CF), F32),
                   pltpu.VMEM((rows, GRID_W, D_CF), F32)]
        name = "conv_grid"
    else:
        kern = _conv_seq_kernel
        scratch = [pltpu.VMEM((s + 32, D_CF), F32)]
        name = "conv_seq"
    return pl.pallas_call(
        kern, grid=(b,),
        in_specs=[row_spec, row_spec, row_spec, full((3, D_SC)), full((CF_WIDTH, D_CF)),
                  full((1, D_CF)), full((1, D_CF)), full((1, D_CF))],
        out_specs=[row_spec, row_spec],
        out_shape=[jax.ShapeDtypeStruct((b, s, D_SC), BF16), jax.ShapeDtypeStruct((b, s, D_CF), BF16)],
        scratch_shapes=scratch,
        compiler_params=_cparams(("parallel",)),
        name=name,
    )(bg, cv, gl, w_sc, w_dw, b_dw.reshape(1, -1), ln_g.reshape(1, -1), ln_b.reshape(1, -1))


def _s5_prep_kernel(are_r, aim_r, are_c, aim_c, ldt, bre_ref, bim_ref, cre_ref, cim_ref,
                    kt_ref, pbre_ref, pbim_ref, pcre_ref, pcnim_ref, lre_ref, lim_ref, lnim_ref):
    dt = jnp.exp(ldt[0, 0])
    n_pow = CHUNK + 1
    jr = lax.broadcasted_iota(jnp.int32, (24, S5_STATE), 0).astype(F32)
    mag_r = are_r[0, 0] * dt
    th_r = aim_r[0, 0] * dt
    e_r = jnp.exp(jr * mag_r)
    pr = e_r * jnp.cos(jr * th_r)
    pi = e_r * jnp.sin(jr * th_r)
    cre = cre_ref[0, 0]
    cim = cim_ref[0, 0]
    pc_re = []
    pc_im = []
    for j in range(n_pow):
        re = cre * pr[j:j + 1] - cim * pi[j:j + 1]
        im = cre * pi[j:j + 1] + cim * pr[j:j + 1]
        pcre_ref[0, 0, j] = re
        pcnim_ref[0, 0, j] = -im
        pc_re.append(re)
        pc_im.append(im)
    lre_ref[0, 0] = pr[CHUNK:CHUNK + 1]
    lim_ref[0, 0] = pi[CHUNK:CHUNK + 1]
    lnim_ref[0, 0] = -pi[CHUNK:CHUNK + 1]

    jc = lax.broadcasted_iota(jnp.int32, (S5_STATE, CHUNK), 1).astype(F32)
    a_re = are_c[0, 0]
    a_im = aim_c[0, 0]
    e_c = jnp.exp(jc * (a_re * dt))
    qr = e_c * jnp.cos(jc * (a_im * dt))
    qi = e_c * jnp.sin(jc * (a_im * dt))
    nr = qr[:, 1:2] - 1.0
    ni = qi[:, 1:2]
    den = a_re * a_re + a_im * a_im
    fre = (nr * a_re + ni * a_im) / den
    fim = (ni * a_re - nr * a_im) / den
    bre = bre_ref[0, 0]
    bim = bim_ref[0, 0]
    bbre = fre * bre - fim * bim
    bbim = fre * bim + fim * bre
    for j in range(CHUNK):
        pbre_ref[0, 0, j] = bbre * qr[:, j:j + 1] - bbim * qi[:, j:j + 1]
        pbim_ref[0, 0, j] = bbre * qi[:, j:j + 1] + bbim * qr[:, j:j + 1]
    stack_re = jnp.concatenate(pc_re[:CHUNK], axis=0)
    stack_im = jnp.concatenate(pc_im[:CHUNK], axis=0)
    kt_ref[0, 0] = _dot3(stack_re, bbre) - _dot3(stack_im, bbim)


def _s5_prep_call(a_re, a_im, log_dt, b_re, b_im, c_re, c_im):
    g, p, n = S5_GROUPS, S5_STATE, S5_GROUP
    spec = lambda shape: pl.BlockSpec((1, 1) + shape, lambda d, i: (d, i) + (0,) * len(shape))
    sds = lambda shape: jax.ShapeDtypeStruct((2, g) + shape, F32)
    return pl.pallas_call(
        _s5_prep_kernel, grid=(2, g),
        in_specs=[spec((1, p)), spec((1, p)), spec((p, 1)), spec((p, 1)), spec((1, 1)),
                  spec((p, n)), spec((p, n)), spec((n, p)), spec((n, p))],
        out_specs=[spec((CHUNK * n, n)), spec((CHUNK, p, n)), spec((CHUNK, p, n)),
                   spec((CHUNK + 1, n, p)), spec((CHUNK + 1, n, p)),
                   spec((1, p)), spec((1, p)), spec((1, p))],
        out_shape=[sds((CHUNK * n, n)), sds((CHUNK, p, n)), sds((CHUNK, p, n)),
                   sds((CHUNK + 1, n, p)), sds((CHUNK + 1, n, p)),
                   sds((1, p)), sds((1, p)), sds((1, p))],
        compiler_params=_cparams(("parallel", "parallel")),
        name="s5_prep",
    )(a_re.reshape(2, g, 1, p), a_im.reshape(2, g, 1, p), a_re.reshape(2, g, p, 1),
      a_im.reshape(2, g, p, 1), log_dt.reshape(2, g, 1, 1), b_re, b_im, c_re, c_im)


def _s5_operators(a_re, a_im, log_dt, b_re, b_im, c_re, c_im):
    g, p, n, t = S5_GROUPS, S5_STATE, S5_GROUP, CHUNK
    kt, pbre, pbim, pcre, pcnim, lre, lim, lnim = _s5_prep_call(a_re, a_im, log_dt, b_re, b_im, c_re, c_im)
    k = kt.reshape(2, g, t, n, n)
    s_idx = jnp.arange(t)[:, None]
    t_idx = jnp.arange(t)[None, :]
    kf = k[0][:, jnp.clip(t_idx - s_idx, 0, t - 1)]
    kb = k[1][:, jnp.clip(s_idx - t_idx, 0, t - 1)]
    mf = jnp.where((t_idx >= s_idx)[None, :, :, None, None], kf, 0.0)
    mb = jnp.where((s_idx >= t_idx)[None, :, :, None, None], kb, 0.0)
    m = (mf + mb).transpose(0, 1, 4, 2, 3).reshape(g, t * n, t * n)

    def state_in(pb_dir):
        return pb_dir.transpose(0, 1, 3, 2).reshape(g, t * n, p)

    ff = jnp.concatenate([state_in(pbre[0][:, ::-1]), state_in(pbim[0][:, ::-1])], axis=2)
    fb = jnp.concatenate([state_in(pbre[1]), state_in(pbim[1])], axis=2)
    w1 = jnp.concatenate([m, ff, fb], axis=2).astype(BF16)

    def state_out(pc_dir):
        return pc_dir.transpose(0, 3, 1, 2).reshape(g, p, t * n)

    e = jnp.concatenate([
        state_out(pcre[0][:, 1:t + 1]), state_out(pcnim[0][:, 1:t + 1]),
        state_out(pcre[1][:, ::-1][:, 0:t]), state_out(pcnim[1][:, ::-1][:, 0:t])], axis=1).astype(BF16)
    a1 = jnp.concatenate([lre, lre], axis=3)
    a2 = jnp.concatenate([lnim, lim], axis=3)
    decay = jnp.concatenate([a1[0], a2[0], a1[1], a2[1]], axis=1)
    return w1, e, decay


def _block_transpose8(ps):
    ps = list(ps)
    blk = lax.broadcasted_iota(jnp.int32, ps[0].shape, 1) // S5_GROUP
    for k in range(3):
        step = 1 << k
        shift = S5_GROUP * step
        keep = ((blk >> k) & 1) == 0
        for a in range(8):
            if a & step:
                continue
            pa, pb = ps[a], ps[a + step]
            ps[a] = jnp.where(keep, pa, pltpu.roll(pb, shift, axis=1))
            ps[a + step] = jnp.where(keep, pltpu.roll(pa, 128 - shift, axis=1), pb)
    return ps


def _s5a_kernel(u_ref, w1_ref, yin_ref, gf_ref, gb_ref):
    nb = u_ref.shape[0]
    xs = []
    for s in range(CHUNK):
        parts = [u_ref[b, pl.ds(s, TILE_CHUNKS, stride=CHUNK), :] for b in range(nb)]
        xs.append(jnp.concatenate(parts, axis=0))
    lo = _block_transpose8(xs[:8])
    hi = _block_transpose8(xs[8:])
    for j in range(GRP_PER_BLK):
        og = jnp.concatenate([lo[j], hi[j]], axis=1)
        r = _dot(og.astype(BF16), w1_ref[j])
        yin_ref[j] = r[:, 0:256]
        gf_ref[j] = r[:, 256:384]
        gb_ref[j] = r[:, 384:512]


def _s5a_call(u, w1):
    b, s, _ = u.shape
    nt = s // TILE_TOK
    rows = b * TILE_CHUNKS
    n_blk = D_S5 // LANE_BLK
    out_spec = lambda n: pl.BlockSpec((GRP_PER_BLK, rows, n), lambda j, l: (l, j, 0))
    return pl.pallas_call(
        _s5a_kernel, grid=(nt, n_blk),
        in_specs=[pl.BlockSpec((b, TILE_TOK, LANE_BLK), lambda j, l: (0, j, l)),
                  pl.BlockSpec((GRP_PER_BLK, 256, 512), lambda j, l: (l, 0, 0))],
        out_specs=[out_spec(256), out_spec(128), out_spec(128)],
        out_shape=[jax.ShapeDtypeStruct((S5_GROUPS, nt * rows, 256), F32),
                   jax.ShapeDtypeStruct((S5_GROUPS, nt * rows, 128), F32),
                   jax.ShapeDtypeStruct((S5_GROUPS, nt * rows, 128), F32)],
        compiler_params=_cparams(("parallel", "parallel")),
        name="s5_chunk_in",
    )(u, w1)


def _s5b_kernel(nb, a_ref, gfc_ref, gfl_ref, gbc_ref, gbl_ref, hfc_ref, hfl_ref, hbc_ref, hbl_ref):
    gb = a_ref.shape[0]
    rows = nb * TILE_CHUNKS
    n_lat = gfl_ref.shape[1] // rows
    a1f = [jnp.broadcast_to(a_ref[g, 0:1, :], (nb, 128)) for g in range(gb)]
    a2f = [jnp.broadcast_to(a_ref[g, 1:2, :], (nb, 128)) for g in range(gb)]
    a1b = [jnp.broadcast_to(a_ref[g, 2:3, :], (nb, 128)) for g in range(gb)]
    a2b = [jnp.broadcast_to(a_ref[g, 3:4, :], (nb, 128)) for g in range(gb)]

    def step(h, a1, a2, g_ref, h_ref, g, row):
        h_ref[g, row, :] = h
        return a1 * h + a2 * pltpu.roll(h, 64, axis=1) + g_ref[g, row, :]

    hf = [jnp.zeros((nb, 128), F32) for _ in range(gb)]
    hb = [jnp.zeros((nb, 128), F32) for _ in range(gb)]
    for ci in range(TILE_CHUNKS):
        rf = pl.ds(ci, nb, stride=TILE_CHUNKS)
        rb = pl.ds(TILE_CHUNKS - 1 - ci, nb, stride=TILE_CHUNKS)
        for g in range(gb):
            hf[g] = step(hf[g], a1f[g], a2f[g], gfc_ref, hfc_ref, g, rf)
            hb[g] = step(hb[g], a1b[g], a2b[g], gbc_ref, hbc_ref, g, rb)

    def body(j, carry):
        hf, hb = carry
        hf = list(hf)
        hb = list(hb)
        base_f = j * rows
        base_b = (n_lat - 1 - j) * rows
        for ci in range(TILE_CHUNKS):
            rf = pl.ds(base_f + ci, nb, stride=TILE_CHUNKS)
            rb = pl.ds(base_b + (TILE_CHUNKS - 1 - ci), nb, stride=TILE_CHUNKS)
            for g in range(gb):
                hf[g] = step(hf[g], a1f[g], a2f[g], gfl_ref, hfl_ref, g, rf)
                hb[g] = step(hb[g], a1b[g], a2b[g], gbl_ref, hbl_ref, g, rb)
        return tuple(hf), tuple(hb)

    lax.fori_loop(0, n_lat, body, (tuple(hf), tuple(hb)))


def _s5b_call(decay, gf_c, gf_l, gb_c, gb_l, nb):
    gblk = 4
    spec = lambda a: pl.BlockSpec((gblk, a.shape[1], 128), lambda i: (i, 0, 0))
    sds = lambda a: jax.ShapeDtypeStruct(a.shape, F32)
    return pl.pallas_call(
        functools.partial(_s5b_kernel, nb), grid=(S5_GROUPS // gblk,),
        in_specs=[pl.BlockSpec((gblk, 4, 128), lambda i: (i, 0, 0)),
                  spec(gf_c), spec(gf_l), spec(gb_c), spec(gb_l)],
        out_specs=[spec(gf_c), spec(gf_l), spec(gb_c), spec(gb_l)],
        out_shape=[sds(gf_c), sds(gf_l), sds(gb_c), sds(gb_l)],
        compiler_params=_cparams(("parallel",)),
        name="s5_state_scan",
    )(decay, gf_c, gf_l, gb_c, gb_l)


def _s5c_kernel(yin_ref, hf_ref, hb_ref, e_ref, u_ref, d_ref, y_ref):
    nb = u_ref.shape[0]
    ys = []
    for j in range(GRP_PER_BLK):
        h = jnp.concatenate([hf_ref[j], hb_ref[j]], axis=1).astype(BF16)
        ys.append(yin_ref[j] + _dot(h, e_ref[j]))
    at = (_block_transpose8([y[:, :128] for y in ys])
          + _block_transpose8([y[:, 128:] for y in ys]))
    d = d_ref[...]
    for t in range(CHUNK):
        for b in range(nb):
            rows = pl.ds(t, TILE_CHUNKS, stride=CHUNK)
            y_ref[b, rows, :] = at[t][b * TILE_CHUNKS:(b + 1) * TILE_CHUNKS] + d * u_ref[b, rows, :]


def _s5c_call(yin, hf, hb, e, u, d_skip):
    b, s, _ = u.shape
    nt = s // TILE_TOK
    rows = b * TILE_CHUNKS
    n_blk = D_S5 // LANE_BLK
    gspec = lambda n: pl.BlockSpec((GRP_PER_BLK, rows, n), lambda j, l: (l, j, 0))
    tok_spec = pl.BlockSpec((b, TILE_TOK, LANE_BLK), lambda j, l: (0, j, l))
    return pl.pallas_call(
        _s5c_kernel, grid=(nt, n_blk),
        in_specs=[gspec(256), gspec(128), gspec(128),
                  pl.BlockSpec((GRP_PER_BLK, 256, 256), lambda j, l: (l, 0, 0)),
                  tok_spec,
                  pl.BlockSpec((1, LANE_BLK), lambda j, l: (0, l))],
        out_specs=tok_spec,
        out_shape=jax.ShapeDtypeStruct((b, s, D_S5), F32),
        compiler_params=_cparams(("parallel", "parallel")),
        name="s5_chunk_out",
    )(yin, hf, hb, e, u, d_skip.reshape(1, D_S5))


def _gelu_tanh(x):
    return 0.5 * x * (1.0 + jnp.tanh(math.sqrt(2.0 / math.pi) * (x + 0.044715 * (x * x * x))))


def _route(logits):
    lane = lax.broadcasted_iota(jnp.int32, logits.shape, 1).astype(F32)
    neg = jnp.float32(-1e30)
    big = jnp.float32(1e9)
    gl = jnp.where(lane < N_GROUPS, logits, neg)
    gmax = jnp.max(gl, axis=1, keepdims=True)
    gidx = jnp.min(jnp.where(gl == gmax, lane, big), axis=1, keepdims=True)
    gsum = jnp.sum(jnp.exp(gl - gmax), axis=1, keepdims=True)
    gw = 1.0 / gsum
    lo = N_GROUPS + EXP_PER_GROUP * gidx
    el = jnp.where((lane >= lo) & (lane < lo + EXP_PER_GROUP), logits, neg)
    v1 = jnp.max(el, axis=1, keepdims=True)
    i1 = jnp.min(jnp.where(el == v1, lane, big), axis=1, keepdims=True)
    el2 = jnp.where(lane == i1, neg, el)
    v2 = jnp.max(el2, axis=1, keepdims=True)
    i2 = jnp.min(jnp.where(el2 == v2, lane, big), axis=1, keepdims=True)
    ex = jnp.exp(v2 - v1)
    p1 = 1.0 / (1.0 + ex)
    p2 = ex * p1
    return jnp.where(lane == i1, gw * p1, 0.0) + jnp.where(lane == i2, gw * p2, 0.0)


def _out_kernel(ypre_ref, ysc_ref, ycf_ref, x_ref, mod_ref, wglu_ref, bglu_ref, wo_ref,
                lng_ref, lnb_ref, wr_ref, br_ref, x1_ref, h2_ref, wts_ref):
    t = _gelu_tanh(ypre_ref[0])
    gate = _sigmoid(_dot(t.astype(BF16), wglu_ref[...]) + bglu_ref[...])
    ys5 = (t * gate).astype(BF16)
    y = (_dot(ys5, wo_ref[0:D_S5, :]) + _dot(ysc_ref[0], wo_ref[D_S5:D_S5 + D_SC, :])
         + _dot(ycf_ref[0], wo_ref[D_S5 + D_SC:D_MODEL, :]))
    g1 = mod_ref[0, 2:3, :]
    x1 = _layer_norm(DN_ALPHA * x_ref[0] + g1 * y, lng_ref[...], lnb_ref[...])
    x1_ref[0] = x1
    h2 = x1 * (1.0 + mod_ref[0, 4:5, :]) + mod_ref[0, 3:4, :]
    h2_ref[0] = h2.astype(BF16)
    wts_ref[0] = _route(_dot3(h2, wr_ref[...]) + br_ref[...])


def _out_call(ypre, ysc, ycf, x, mod, wglu_bf, b_glu, wo_bf, ln_g, ln_b, w_router, b_router, tm):
    b, s, d = x.shape
    row_spec = lambda n: pl.BlockSpec((1, tm, n), lambda i, j: (i, j, 0))
    full = lambda shape: pl.BlockSpec(shape, lambda i, j: (0,) * len(shape))
    return pl.pallas_call(
        _out_kernel, grid=(b, s // tm),
        in_specs=[row_spec(D_S5), row_spec(D_SC), row_spec(D_CF), row_spec(d),
                  pl.BlockSpec((1, 6, d), lambda i, j: (i, 0, 0)),
                  full((D_S5, D_S5)), full((1, D_S5)), full((d, d)),
                  full((1, d)), full((1, d)), full((d, ROUTER_LANES)), full((1, ROUTER_LANES))],
        out_specs=[row_spec(d), row_spec(d), row_spec(ROUTER_LANES)],
        out_shape=[jax.ShapeDtypeStruct((b, s, d), F32), jax.ShapeDtypeStruct((b, s, d), BF16),
                   jax.ShapeDtypeStruct((b, s, ROUTER_LANES), F32)],
        compiler_params=_cparams(("parallel", "parallel")),
        name="out_proj",
    )(ypre, ysc, ycf, x, mod, wglu_bf, b_glu.reshape(1, -1), wo_bf, ln_g.reshape(1, -1),
      ln_b.reshape(1, -1), w_router, b_router)


def _moe_kernel(h2_ref, wts_ref, x1_ref, mod_ref, wg_ref, wu_ref, wd_ref, lng_ref, lnb_ref, o_ref, acc_ref):
    e = pl.program_id(2)

    @pl.when(e == 0)
    def _():
        acc_ref[...] = jnp.zeros_like(acc_ref)

    h = h2_ref[0]
    gate = _dot(h, wg_ref[0])
    up = _dot(h, wu_ref[0])
    wts = wts_ref[0]
    lane = lax.broadcasted_iota(jnp.int32, wts.shape, 1)
    w_e = jnp.sum(jnp.where(lane == e + N_GROUPS, wts, 0.0), axis=1, keepdims=True)
    act = gate * _sigmoid(gate) * up * w_e
    acc_ref[...] += _dot(act.astype(BF16), wd_ref[0])

    @pl.when(e == pl.num_programs(2) - 1)
    def _():
        g2 = mod_ref[0, 5:6, :]
        o_ref[0] = _layer_norm(DN_ALPHA * x1_ref[0] + g2 * acc_ref[...], lng_ref[...], lnb_ref[...])


def _moe_call(h2, wts, x1, mod, wg_bf, wu_bf, wd_bf, ln_g, ln_b, tm):
    b, s, d = x1.shape
    row_spec = lambda n: pl.BlockSpec((1, tm, n), lambda i, j, e: (i, j, 0))
    full = lambda shape: pl.BlockSpec(shape, lambda i, j, e: (0,) * len(shape))
    return pl.pallas_call(
        _moe_kernel, grid=(b, s // tm, N_EXPERTS),
        in_specs=[row_spec(d), row_spec(ROUTER_LANES), row_spec(d),
                  pl.BlockSpec((1, 6, d), lambda i, j, e: (i, 0, 0)),
                  pl.BlockSpec((1, d, D_EXPERT), lambda i, j, e: (e, 0, 0)),
                  pl.BlockSpec((1, d, D_EXPERT), lambda i, j, e: (e, 0, 0)),
                  pl.BlockSpec((1, D_EXPERT, d), lambda i, j, e: (e, 0, 0)),
                  full((1, d)), full((1, d))],
        out_specs=row_spec(d),
        out_shape=jax.ShapeDtypeStruct((b, s, d), F32),
        scratch_shapes=[pltpu.VMEM((tm, d), F32)],
        compiler_params=_cparams(("parallel", "parallel", "arbitrary")),
        name="moe",
    )(h2, wts, x1, mod, wg_bf, wu_bf, wd_bf, ln_g.reshape(1, -1), ln_b.reshape(1, -1))


def kernel(x, c, ctx, c_ctx, w_mod, b_mod, w_in, s5_a_re, s5_a_im, s5_log_dt, s5_b_re, s5_b_im, s5_c_re, s5_c_im, s5_d, w_glu, b_glu, w_sc, w_dw, b_dw, ln_cf_g, ln_cf_b, w_o, ln1_g, ln1_b, w_rg, b_rg, w_rexp, b_rexp, w_gate, w_up, w_down, ln2_g, ln2_b):
    nb, seq, d = x.shape
    n_ctx = ctx.shape[1]
    n_layers = w_mod.shape[0]
    assert seq % TILE_TOK == 0 and n_ctx % TILE_TOK == 0 and seq % GRID_W == 0

    mod_rows = 16
    assert nb + 1 <= mod_rows
    c_all = jnp.concatenate([c, c_ctx[None, :], jnp.zeros((mod_rows - nb - 1, d), F32)], axis=0)
    mod_all = _mod_call(c_all, w_mod, b_mod)

    pad_r = ROUTER_LANES - N_GROUPS - N_EXPERTS
    x_lat, x_ctx = x, ctx
    for l in range(n_layers):
        last = l == n_layers - 1
        mod_lat = mod_all[l, :nb].reshape(nb, 6, d)
        mod_ctx = jnp.broadcast_to(mod_all[l, nb].reshape(1, 6, d), (nb, 6, d))
        w_in_bf = w_in[l].astype(BF16)
        wglu_bf = w_glu[l].astype(BF16)
        wo_bf = w_o[l].astype(BF16)
        wg_bf = w_gate[l].astype(BF16)
        wu_bf = w_up[l].astype(BF16)
        wd_bf = w_down[l].astype(BF16)
        w_router = jnp.concatenate([w_rg[l], w_rexp[l], jnp.zeros((d, pad_r), F32)], axis=1)
        b_router = jnp.concatenate([b_rg[l], b_rexp[l], jnp.zeros((pad_r,), F32)]).reshape(1, -1)
        w1, e_op, decay = _s5_operators(s5_a_re[l], s5_a_im[l], s5_log_dt[l], s5_b_re[l], s5_b_im[l],
                                        s5_c_re[l], s5_c_im[l])

        u_l, bg_l, cv_l, gl_l = _in_call(x_lat, mod_lat, w_in_bf, 512, False)
        if last:
            u_c = _in_call(x_ctx, mod_ctx, w_in_bf[:, :D_S5], TILE_TOK, True)
        else:
            u_c, bg_c, cv_c, gl_c = _in_call(x_ctx, mod_ctx, w_in_bf, TILE_TOK, False)

        yin_l, gf_l, gb_l = _s5a_call(u_l, w1)
        yin_c, gf_c, gb_c = _s5a_call(u_c, w1)
        hf_c, hf_l, hb_c, hb_l = _s5b_call(decay, gf_c, gf_l, gb_c, gb_l, nb)
        ypre_l = _s5c_call(yin_l, hf_l, hb_l, e_op, u_l, s5_d[l])

        ysc_l, ycf_l = _conv_call(bg_l, cv_l, gl_l, w_sc[l], w_dw[l], b_dw[l], ln_cf_g[l], ln_cf_b[l], True)
        x1_l, h2_l, wts_l = _out_call(ypre_l, ysc_l, ycf_l, x_lat, mod_lat, wglu_bf, b_glu[l], wo_bf,
                                      ln1_g[l], ln1_b[l], w_router, b_router, 512)
        if not last:
            ypre_c = _s5c_call(yin_c, hf_c, hb_c, e_op, u_c, s5_d[l])
            ysc_c, ycf_c = _conv_call(bg_c, cv_c, gl_c, w_sc[l], w_dw[l], b_dw[l], ln_cf_g[l], ln_cf_b[l], False)
            x1_c, h2_c, wts_c = _out_call(ypre_c, ysc_c, ycf_c, x_ctx, mod_ctx, wglu_bf, b_glu[l], wo_bf,
                                          ln1_g[l], ln1_b[l], w_router, b_router, TILE_TOK)
            x_ctx = _moe_call(h2_c, wts_c, x1_c, mod_ctx, wg_bf, wu_bf, wd_bf, ln2_g[l], ln2_b[l], TILE_TOK)
        x_lat = _moe_call(h2_l, wts_l, x1_l, mod_lat, wg_bf, wu_bf, wd_bf, ln2_g[l], ln2_b[l], min(1024, seq))
    return x_lat
```

```python
import functools
import math

import jax
import jax.numpy as jnp
from jax import lax
from jax.experimental import pallas as pl
from jax.experimental.pallas import tpu as pltpu

F32 = jnp.float32
BF16 = jnp.bfloat16

D_MODEL = 1024
DEPTH = 2
GRID_W = 64
D_S5 = 512
S5_GROUP = 16
S5_GROUPS = 32
S5_STATE = 64
D_SC = 256
D_CF = 256
CF_WIDTH = 31
CF_HALF = 15
D_IN = 1792
N_GROUPS = 4
EXP_PER_GROUP = 4
N_EXPERTS = 16
D_EXPERT = 256
DN_ALPHA = (2 * DEPTH) ** 0.25
LN_EPS = 1e-5

CHUNK = 16
TILE_CHUNKS = 16
TILE_TOK = CHUNK * TILE_CHUNKS
LANE_BLK = 128
GRP_PER_BLK = LANE_BLK // S5_GROUP
ROUTER_LANES = 128
VMEM_LIMIT = 56 * 1024 * 1024


def _cparams(sem):
    return pltpu.CompilerParams(dimension_semantics=sem, vmem_limit_bytes=VMEM_LIMIT)


def _split_bf16(a):
    hi = a.astype(BF16)
    lo = (a - hi.astype(F32)).astype(BF16)
    return hi, lo


def _dot(a, b):
    return jnp.dot(a, b, preferred_element_type=F32)


def _dot3(a, b):
    ah, al = _split_bf16(a)
    bh, bl = _split_bf16(b)
    return _dot(ah, bh) + (_dot(al, bh) + _dot(ah, bl))


def _sigmoid(x):
    return 1.0 / (1.0 + jnp.exp(-x))


def _layer_norm(x, g, b):
    mu = jnp.mean(x, axis=-1, keepdims=True)
    xc = x - mu
    var = jnp.mean(xc * xc, axis=-1, keepdims=True)
    return xc * lax.rsqrt(var + LN_EPS) * g + b


def _mod_kernel(c_ref, w_ref, b_ref, o_ref):
    c = c_ref[...]
    s = c * _sigmoid(c)
    o_ref[0] = _dot3(s, w_ref[0]) + b_ref[0]


def _mod_call(c_all, w_mod, b_mod):
    n_layers, d, n_out = w_mod.shape
    tn = 1536
    rows = c_all.shape[0]
    return pl.pallas_call(
        _mod_kernel,
        grid=(n_layers, n_out // tn),
        in_specs=[
            pl.BlockSpec((rows, d), lambda l, j: (0, 0)),
            pl.BlockSpec((1, d, tn), lambda l, j: (l, 0, j)),
            pl.BlockSpec((1, 1, tn), lambda l, j: (l, 0, j)),
        ],
        out_specs=pl.BlockSpec((1, rows, tn), lambda l, j: (l, 0, j)),
        out_shape=jax.ShapeDtypeStruct((n_layers, rows, n_out), F32),
        compiler_params=_cparams(("parallel", "parallel")),
        name="mod",
    )(c_all, w_mod, b_mod.reshape(n_layers, 1, n_out))


def _in_kernel(x_ref, mod_ref, w_ref, u_ref, bg_ref, cv_ref, gl_ref):
    x = x_ref[0]
    sh = mod_ref[0, 0:1, :]
    sc = mod_ref[0, 1:2, :]
    h = (x * (1.0 + sc) + sh).astype(BF16)
    z = _dot(h, w_ref[...])
    u_ref[0] = z[:, 0:512]
    bg_ref[0] = z[:, 512:768]
    cv_ref[0] = z[:, 768:1024] * z[:, 1024:1280]
    gl_ref[0] = z[:, 1280:1536] * _sigmoid(z[:, 1536:1792])


def _in_u_kernel(x_ref, mod_ref, w_ref, u_ref):
    x = x_ref[0]
    sh = mod_ref[0, 0:1, :]
    sc = mod_ref[0, 1:2, :]
    h = (x * (1.0 + sc) + sh).astype(BF16)
    u_ref[0] = _dot(h, w_ref[...])


def _in_call(x, mod, w_in_bf, tm, u_only):
    b, s, d = x.shape
    grid = (b, s // tm)
    row_spec = lambda n: pl.BlockSpec((1, tm, n), lambda i, j: (i, j, 0))
    in_specs = [
        row_spec(d),
        pl.BlockSpec((1, 6, d), lambda i, j: (i, 0, 0)),
    ]
    if u_only:
        in_specs.append(pl.BlockSpec((d, D_S5), lambda i, j: (0, 0)))
        return pl.pallas_call(
            _in_u_kernel, grid=grid, in_specs=in_specs,
            out_specs=row_spec(D_S5),
            out_shape=jax.ShapeDtypeStruct((b, s, D_S5), F32),
            compiler_params=_cparams(("parallel", "parallel")),
            name="in_proj_u",
        )(x, mod, w_in_bf)
    in_specs.append(pl.BlockSpec((d, D_IN), lambda i, j: (0, 0)))
    return pl.pallas_call(
        _in_kernel, grid=grid, in_specs=in_specs,
        out_specs=[row_spec(D_S5), row_spec(D_SC), row_spec(D_SC), row_spec(D_CF)],
        out_shape=[jax.ShapeDtypeStruct((b, s, D_S5), F32),
                   jax.ShapeDtypeStruct((b, s, D_SC), F32),
                   jax.ShapeDtypeStruct((b, s, D_SC), F32),
                   jax.ShapeDtypeStruct((b, s, D_CF), F32)],
        compiler_params=_cparams(("parallel", "parallel")),
        name="in_proj",
    )(x, mod, w_in_bf)


def _conv_tail(t, bdw_ref, lng_ref, lnb_ref):
    t = t + bdw_ref[...]
    t = _layer_norm(t, lng_ref[...], lnb_ref[...])
    return t * _sigmoid(t)


def _conv_grid_kernel(bg_ref, cv_ref, gl_ref, wsc_ref, wdw_ref, bdw_ref, lng_ref, lnb_ref,
                      ysc_ref, ycf_ref, pad_ref, t_ref):
    s = cv_ref.shape[1]
    rows = s // GRID_W
    cv = cv_ref[0]
    col = lax.broadcasted_iota(jnp.int32, (s, D_SC), 0) % GRID_W
    prev = jnp.where(col == 0, 0.0, pltpu.roll(cv, 1, axis=0))
    nxt = jnp.where(col == GRID_W - 1, 0.0, pltpu.roll(cv, s - 1, axis=0))
    conv = prev * wsc_ref[0:1, :] + cv * wsc_ref[1:2, :] + nxt * wsc_ref[2:3, :]
    ysc_ref[0] = (bg_ref[0] * conv).astype(ysc_ref.dtype)

    zero = jnp.zeros((CF_HALF, GRID_W, D_CF), F32)
    pad_ref[0:CF_HALF] = zero
    pad_ref[CF_HALF + rows:CF_HALF + rows + CF_HALF] = zero
    pad_ref[CF_HALF:CF_HALF + rows] = gl_ref[0].reshape(rows, GRID_W, D_CF)

    def body(i, carry):
        w0 = pl.multiple_of(i * 8, 8)
        for half in range(D_CF // 128):
            lanes = slice(half * 128, (half + 1) * 128)
            acc = jnp.zeros((rows, 8, 128), F32)
            for k in range(CF_WIDTH):
                acc = acc + pad_ref[k:k + rows, pl.ds(w0, 8), lanes] * wdw_ref[k:k + 1, lanes]
            t_ref[:, pl.ds(w0, 8), lanes] = acc
        return carry

    lax.fori_loop(0, GRID_W // 8, body, 0)
    t = t_ref[...].reshape(s, D_CF)
    ycf_ref[0] = _conv_tail(t, bdw_ref, lng_ref, lnb_ref).astype(ycf_ref.dtype)


def _conv_seq_kernel(bg_ref, cv_ref, gl_ref, wsc_ref, wdw_ref, bdw_ref, lng_ref, lnb_ref,
                     ysc_ref, ycf_ref, pad_ref):
    s = cv_ref.shape[1]
    cv = cv_ref[0]
    pos = lax.broadcasted_iota(jnp.int32, (s, D_SC), 0)
    prev = jnp.where(pos == 0, 0.0, pltpu.roll(cv, 1, axis=0))
    nxt = jnp.where(pos == s - 1, 0.0, pltpu.roll(cv, s - 1, axis=0))
    conv = prev * wsc_ref[0:1, :] + cv * wsc_ref[1:2, :] + nxt * wsc_ref[2:3, :]
    ysc_ref[0] = (bg_ref[0] * conv).astype(ysc_ref.dtype)

    off = 16
    pad_ref[0:off] = jnp.zeros((off, D_CF), F32)
    pad_ref[off + s:off + s + 16] = jnp.zeros((16, D_CF), F32)
    pad_ref[off:off + s] = gl_ref[0]
    acc = jnp.zeros((s, D_CF), F32)
    for k in range(CF_WIDTH):
        acc = acc + pad_ref[pl.ds(off - CF_HALF + k, s), :] * wdw_ref[k:k + 1, :]
    ycf_ref[0] = _conv_tail(acc, bdw_ref, lng_ref, lnb_ref).astype(ycf_ref.dtype)


def _conv_call(bg, cv, gl, w_sc, w_dw, b_dw, ln_g, ln_b, grid_mode):
    b, s, _ = bg.shape
    row_spec = pl.BlockSpec((1, s, D_SC), lambda i: (i, 0, 0))
    full = lambda shape: pl.BlockSpec(shape, lambda i: (0,) * len(shape))
    if grid_mode:
        rows = s // GRID_W
        kern = _conv_grid_kernel
        scratch = [pltpu.VMEM((rows + 2 * CF_HALF, GRID_W, D_CF), F32),
                   pltpu.VMEM((rows, GRID_W, D_CF), F32)]
        name = "conv_grid"
    else:
        kern = _conv_seq_kernel
        scratch = [pltpu.VMEM((s + 32, D_CF), F32)]
        name = "conv_seq"
    return pl.pallas_call(
        kern, grid=(b,),
        in_specs=[row_spec, row_spec, row_spec, full((3, D_SC)), full((CF_WIDTH, D_CF)),
                  full((1, D_CF)), full((1, D_CF)), full((1, D_CF))],
        out_specs=[row_spec, row_spec],
        out_shape=[jax.ShapeDtypeStruct((b, s, D_SC), BF16), jax.ShapeDtypeStruct((b, s, D_CF), BF16)],
        scratch_shapes=scratch,
        compiler_params=_cparams(("parallel",)),
        name=name,
    )(bg, cv, gl, w_sc, w_dw, b_dw.reshape(1, -1), ln_g.reshape(1, -1), ln_b.reshape(1, -1))


def _split3(a):
    hi = a.astype(BF16)
    r = a - hi.astype(F32)
    mid = r.astype(BF16)
    lo = (r - mid.astype(F32)).astype(BF16)
    return hi, mid, lo


def _select_cols(a, sel):
    hi, mid, lo = _split3(a)
    return _dot(hi, sel) + (_dot(mid, sel) + _dot(lo, sel))


def _select_rows(sel, a):
    hi, mid, lo = _split3(a)
    return _dot(sel, hi) + (_dot(sel, mid) + _dot(sel, lo))


def _cmul(ar, ai, br, bi):
    return ar * br - ai * bi, ar * bi + ai * br


def _s5_prep_kernel(are_r, aim_r, are_c, aim_c, ldt, bre_ref, bim_ref, cre_ref, cim_ref,
                    w1_ref, e_ref, dec_ref):
    t, n, p = CHUNK, S5_GROUP, S5_STATE
    width = t * n
    lane_tok = lax.broadcasted_iota(jnp.int32, (128, width), 1) // n
    pow_id = lax.broadcasted_iota(jnp.int32, (128, width), 0)
    onehot = lambda cond: jnp.where(cond, 1.0, 0.0).astype(BF16)
    sel_fwd = onehot(pow_id == lane_tok)
    sel_rev = onehot(pow_id == t - 1 - lane_tok)
    sel_out = onehot(pow_id == t - lane_tok)
    row_tok = lax.broadcasted_iota(jnp.int32, (width, 32), 0) // n
    row_pow = lax.broadcasted_iota(jnp.int32, (width, 32), 1)
    rsel_rev = onehot(row_pow == t - 1 - row_tok)
    rsel_fwd = onehot(row_pow == row_tok)
    lane = lax.broadcasted_iota(jnp.int32, (n, width), 1)

    jc = jnp.minimum(lax.broadcasted_iota(jnp.int32, (p, 128), 1), t).astype(F32)
    jr = jnp.minimum(lax.broadcasted_iota(jnp.int32, (32, p), 0), t).astype(F32)

    strips = []
    f_parts = []
    e_parts = []
    for d in range(2):
        dt = jnp.exp(ldt[d, 0])
        mag_c = are_c[d, 0] * dt
        th_c = aim_c[d, 0] * dt
        ec = jnp.exp(jc * mag_c)
        qr = ec * jnp.cos(jc * th_c)
        qi = ec * jnp.sin(jc * th_c)
        a_re = are_r[d, 0]
        a_im = aim_r[d, 0]
        er = jnp.exp(jr * (a_re * dt))
        pr = er * jnp.cos(jr * (a_im * dt))
        pi = er * jnp.sin(jr * (a_im * dt))
        nr = pr[1:2] - 1.0
        ni = pi[1:2]
        den = a_re * a_re + a_im * a_im
        fre = (nr * a_re + ni * a_im) / den
        fim = (ni * a_re - nr * a_im) / den
        bt_re = bre_ref[d, 0].T
        bt_im = bim_ref[d, 0].T
        bb_re, bb_im = _cmul(fre, fim, bt_re, bt_im)
        ct_re = jnp.concatenate([cre_ref[d, 0].T] * t, axis=1)
        ct_im = jnp.concatenate([cim_ref[d, 0].T] * t, axis=1)
        sel = sel_fwd if d == 0 else sel_rev
        w_re, w_im = _cmul(ct_re, ct_im, _select_cols(qr, sel), _select_cols(qi, sel))
        strips.append(_dot3(bb_re, w_re) - _dot3(bb_im, w_im))
        if d == 0:
            o_re, o_im = _cmul(w_re, w_im, qr[:, 1:2], qi[:, 1:2])
        else:
            o_re, o_im = _cmul(ct_re, ct_im, _select_cols(qr, sel_out), _select_cols(qi, sel_out))
        e_parts += [o_re, -o_im]
        rsel = rsel_rev if d == 0 else rsel_fwd
        f_re, f_im = _cmul(jnp.concatenate([bb_re] * t, axis=0), jnp.concatenate([bb_im] * t, axis=0),
                           _select_rows(rsel, pr), _select_rows(rsel, pi))
        f_parts += [f_re, f_im]
        dec_ref[0, 2 * d:2 * d + 1, :] = jnp.concatenate([pr[t:t + 1], pr[t:t + 1]], axis=1)
        dec_ref[0, 2 * d + 1:2 * d + 2, :] = jnp.concatenate([-pi[t:t + 1], pi[t:t + 1]], axis=1)

    blocks = []
    for s in range(t):
        fwd = strips[0] if s == 0 else jnp.where(lane >= n * s, pltpu.roll(strips[0], n * s, axis=1), 0.0)
        back = t - 1 - s
        bwd = strips[1] if back == 0 else jnp.where(lane < width - n * back,
                                                     pltpu.roll(strips[1], width - n * back, axis=1), 0.0)
        blocks.append(fwd + bwd)
    m = jnp.concatenate(blocks, axis=0)
    w1_ref[0] = jnp.concatenate([m] + f_parts, axis=1).astype(BF16)
    e_ref[0] = jnp.concatenate(e_parts, axis=0).astype(BF16)


def _s5_operators(a_re, a_im, log_dt, b_re, b_im, c_re, c_im):
    g, p, n, t = S5_GROUPS, S5_STATE, S5_GROUP, CHUNK
    spec = lambda shape: pl.BlockSpec((2, 1) + shape, lambda i: (0, i) + (0,) * len(shape))
    return pl.pallas_call(
        _s5_prep_kernel, grid=(g,),
        in_specs=[spec((1, p)), spec((1, p)), spec((p, 1)), spec((p, 1)), spec((1, 1)),
                  spec((p, n)), spec((p, n)), spec((n, p)), spec((n, p))],
        out_specs=[pl.BlockSpec((1, t * n, 2 * t * n), lambda i: (i, 0, 0)),
                   pl.BlockSpec((1, 4 * p, t * n), lambda i: (i, 0, 0)),
                   pl.BlockSpec((1, 4, 2 * p), lambda i: (i, 0, 0))],
        out_shape=[jax.ShapeDtypeStruct((g, t * n, 2 * t * n), BF16),
                   jax.ShapeDtypeStruct((g, 4 * p, t * n), BF16),
                   jax.ShapeDtypeStruct((g, 4, 2 * p), F32)],
        compiler_params=_cparams(("parallel",)),
        name="s5_prep",
    )(a_re.reshape(2, g, 1, p), a_im.reshape(2, g, 1, p), a_re.reshape(2, g, p, 1),
      a_im.reshape(2, g, p, 1), log_dt.reshape(2, g, 1, 1), b_re, b_im, c_re, c_im)


def _block_transpose8(ps):
    ps = list(ps)
    blk = lax.broadcasted_iota(jnp.int32, ps[0].shape, 1) // S5_GROUP
    for k in range(3):
        step = 1 << k
        shift = S5_GROUP * step
        keep = ((blk >> k) & 1) == 0
        for a in range(8):
            if a & step:
                continue
            pa, pb = ps[a], ps[a + step]
            ps[a] = jnp.where(keep, pa, pltpu.roll(pb, shift, axis=1))
            ps[a + step] = jnp.where(keep, pltpu.roll(pa, 128 - shift, axis=1), pb)
    return ps


def _s5a_kernel(u_ref, w1_ref, yin_ref, gf_ref, gb_ref):
    nb = u_ref.shape[0]
    xs = []
    for s in range(CHUNK):
        parts = [u_ref[b, pl.ds(s, TILE_CHUNKS, stride=CHUNK), :] for b in range(nb)]
        xs.append(jnp.concatenate(parts, axis=0))
    lo = _block_transpose8(xs[:8])
    hi = _block_transpose8(xs[8:])
    for j in range(GRP_PER_BLK):
        og = jnp.concatenate([lo[j], hi[j]], axis=1)
        r = _dot(og.astype(BF16), w1_ref[j])
        yin_ref[j] = r[:, 0:256]
        gf_ref[j] = r[:, 256:384]
        gb_ref[j] = r[:, 384:512]


def _s5a_call(u, w1):
    b, s, _ = u.shape
    nt = s // TILE_TOK
    rows = b * TILE_CHUNKS
    n_blk = D_S5 // LANE_BLK
    out_spec = lambda n: pl.BlockSpec((GRP_PER_BLK, rows, n), lambda j, l: (l, j, 0))
    return pl.pallas_call(
        _s5a_kernel, grid=(nt, n_blk),
        in_specs=[pl.BlockSpec((b, TILE_TOK, LANE_BLK), lambda j, l: (0, j, l)),
                  pl.BlockSpec((GRP_PER_BLK, 256, 512), lambda j, l: (l, 0, 0))],
        out_specs=[out_spec(256), out_spec(128), out_spec(128)],
        out_shape=[jax.ShapeDtypeStruct((S5_GROUPS, nt * rows, 256), F32),
                   jax.ShapeDtypeStruct((S5_GROUPS, nt * rows, 128), F32),
                   jax.ShapeDtypeStruct((S5_GROUPS, nt * rows, 128), F32)],
        compiler_params=_cparams(("parallel", "parallel")),
        name="s5_chunk_in",
    )(u, w1)


def _s5b_kernel(nb, a_ref, gfc_ref, gfl_ref, gbc_ref, gbl_ref, hfc_ref, hfl_ref, hbc_ref, hbl_ref):
    gb = a_ref.shape[0]
    rows = nb * TILE_CHUNKS
    n_lat = gfl_ref.shape[1] // rows
    a1f = [jnp.broadcast_to(a_ref[g, 0:1, :], (nb, 128)) for g in range(gb)]
    a2f = [jnp.broadcast_to(a_ref[g, 1:2, :], (nb, 128)) for g in range(gb)]
    a1b = [jnp.broadcast_to(a_ref[g, 2:3, :], (nb, 128)) for g in range(gb)]
    a2b = [jnp.broadcast_to(a_ref[g, 3:4, :], (nb, 128)) for g in range(gb)]

    def step(h, a1, a2, g_ref, h_ref, g, row):
        h_ref[g, row, :] = h
        return a1 * h + a2 * pltpu.roll(h, 64, axis=1) + g_ref[g, row, :]

    hf = [jnp.zeros((nb, 128), F32) for _ in range(gb)]
    hb = [jnp.zeros((nb, 128), F32) for _ in range(gb)]
    for ci in range(TILE_CHUNKS):
        rf = pl.ds(ci, nb, stride=TILE_CHUNKS)
        rb = pl.ds(TILE_CHUNKS - 1 - ci, nb, stride=TILE_CHUNKS)
        for g in range(gb):
            hf[g] = step(hf[g], a1f[g], a2f[g], gfc_ref, hfc_ref, g, rf)
            hb[g] = step(hb[g], a1b[g], a2b[g], gbc_ref, hbc_ref, g, rb)

    def body(j, carry):
        hf, hb = carry
        hf = list(hf)
        hb = list(hb)
        base_f = j * rows
        base_b = (n_lat - 1 - j) * rows
        for ci in range(TILE_CHUNKS):
            rf = pl.ds(base_f + ci, nb, stride=TILE_CHUNKS)
            rb = pl.ds(base_b + (TILE_CHUNKS - 1 - ci), nb, stride=TILE_CHUNKS)
            for g in range(gb):
                hf[g] = step(hf[g], a1f[g], a2f[g], gfl_ref, hfl_ref, g, rf)
                hb[g] = step(hb[g], a1b[g], a2b[g], gbl_ref, hbl_ref, g, rb)
        return tuple(hf), tuple(hb)

    lax.fori_loop(0, n_lat, body, (tuple(hf), tuple(hb)))


def _s5b_call(decay, gf_c, gf_l, gb_c, gb_l, nb):
    gblk = 4
    spec = lambda a: pl.BlockSpec((gblk, a.shape[1], 128), lambda i: (i, 0, 0))
    sds = lambda a: jax.ShapeDtypeStruct(a.shape, F32)
    return pl.pallas_call(
        functools.partial(_s5b_kernel, nb), grid=(S5_GROUPS // gblk,),
        in_specs=[pl.BlockSpec((gblk, 4, 128), lambda i: (i, 0, 0)),
                  spec(gf_c), spec(gf_l), spec(gb_c), spec(gb_l)],
        out_specs=[spec(gf_c), spec(gf_l), spec(gb_c), spec(gb_l)],
        out_shape=[sds(gf_c), sds(gf_l), sds(gb_c), sds(gb_l)],
        compiler_params=_cparams(("parallel",)),
        name="s5_state_scan",
    )(decay, gf_c, gf_l, gb_c, gb_l)


def _s5c_kernel(yin_ref, hf_ref, hb_ref, e_ref, u_ref, d_ref, y_ref):
    nb = u_ref.shape[0]
    ys = []
    for j in range(GRP_PER_BLK):
        h = jnp.concatenate([hf_ref[j], hb_ref[j]], axis=1).astype(BF16)
        ys.append(yin_ref[j] + _dot(h, e_ref[j]))
    at = (_block_transpose8([y[:, :128] for y in ys])
          + _block_transpose8([y[:, 128:] for y in ys]))
    d = d_ref[...]
    for t in range(CHUNK):
        for b in range(nb):
            rows = pl.ds(t, TILE_CHUNKS, stride=CHUNK)
            y_ref[b, rows, :] = at[t][b * TILE_CHUNKS:(b + 1) * TILE_CHUNKS] + d * u_ref[b, rows, :]


def _s5c_call(yin, hf, hb, e, u, d_skip):
    b, s, _ = u.shape
    nt = s // TILE_TOK
    rows = b * TILE_CHUNKS
    n_blk = D_S5 // LANE_BLK
    gspec = lambda n: pl.BlockSpec((GRP_PER_BLK, rows, n), lambda j, l: (l, j, 0))
    tok_spec = pl.BlockSpec((b, TILE_TOK, LANE_BLK), lambda j, l: (0, j, l))
    return pl.pallas_call(
        _s5c_kernel, grid=(nt, n_blk),
        in_specs=[gspec(256), gspec(128), gspec(128),
                  pl.BlockSpec((GRP_PER_BLK, 256, 256), lambda j, l: (l, 0, 0)),
                  tok_spec,
                  pl.BlockSpec((1, LANE_BLK), lambda j, l: (0, l))],
        out_specs=tok_spec,
        out_shape=jax.ShapeDtypeStruct((b, s, D_S5), F32),
        compiler_params=_cparams(("parallel", "parallel")),
        name="s5_chunk_out",
    )(yin, hf, hb, e, u, d_skip.reshape(1, D_S5))


def _gelu_tanh(x):
    return 0.5 * x * (1.0 + jnp.tanh(math.sqrt(2.0 / math.pi) * (x + 0.044715 * (x * x * x))))


def _route(logits):
    lane = lax.broadcasted_iota(jnp.int32, logits.shape, 1).astype(F32)
    neg = jnp.float32(-1e30)
    big = jnp.float32(1e9)
    gl = jnp.where(lane < N_GROUPS, logits, neg)
    gmax = jnp.max(gl, axis=1, keepdims=True)
    gidx = jnp.min(jnp.where(gl == gmax, lane, big), axis=1, keepdims=True)
    gsum = jnp.sum(jnp.exp(gl - gmax), axis=1, keepdims=True)
    gw = 1.0 / gsum
    lo = N_GROUPS + EXP_PER_GROUP * gidx
    el = jnp.where((lane >= lo) & (lane < lo + EXP_PER_GROUP), logits, neg)
    v1 = jnp.max(el, axis=1, keepdims=True)
    i1 = jnp.min(jnp.where(el == v1, lane, big), axis=1, keepdims=True)
    el2 = jnp.where(lane == i1, neg, el)
    v2 = jnp.max(el2, axis=1, keepdims=True)
    i2 = jnp.min(jnp.where(el2 == v2, lane, big), axis=1, keepdims=True)
    ex = jnp.exp(v2 - v1)
    p1 = 1.0 / (1.0 + ex)
    p2 = ex * p1
    return jnp.where(lane == i1, gw * p1, 0.0) + jnp.where(lane == i2, gw * p2, 0.0)


def _out_kernel(ypre_ref, ysc_ref, ycf_ref, x_ref, mod_ref, wglu_ref, bglu_ref, wo_ref,
                lng_ref, lnb_ref, wr_ref, br_ref, x1_ref, h2_ref, wts_ref):
    t = _gelu_tanh(ypre_ref[0])
    gate = _sigmoid(_dot(t.astype(BF16), wglu_ref[...]) + bglu_ref[...])
    ys5 = (t * gate).astype(BF16)
    y = (_dot(ys5, wo_ref[0:D_S5, :]) + _dot(ysc_ref[0], wo_ref[D_S5:D_S5 + D_SC, :])
         + _dot(ycf_ref[0], wo_ref[D_S5 + D_SC:D_MODEL, :]))
    g1 = mod_ref[0, 2:3, :]
    x1 = _layer_norm(DN_ALPHA * x_ref[0] + g1 * y, lng_ref[...], lnb_ref[...])
    x1_ref[0] = x1
    h2 = x1 * (1.0 + mod_ref[0, 4:5, :]) + mod_ref[0, 3:4, :]
    h2_ref[0] = h2.astype(BF16)
    wts_ref[0] = _route(_dot3(h2, wr_ref[...]) + br_ref[...])


def _out_call(ypre, ysc, ycf, x, mod, wglu_bf, b_glu, wo_bf, ln_g, ln_b, w_router, b_router, tm):
    b, s, d = x.shape
    row_spec = lambda n: pl.BlockSpec((1, tm, n), lambda i, j: (i, j, 0))
    full = lambda shape: pl.BlockSpec(shape, lambda i, j: (0,) * len(shape))
    return pl.pallas_call(
        _out_kernel, grid=(b, s // tm),
        in_specs=[row_spec(D_S5), row_spec(D_SC), row_spec(D_CF), row_spec(d),
                  pl.BlockSpec((1, 6, d), lambda i, j: (i, 0, 0)),
                  full((D_S5, D_S5)), full((1, D_S5)), full((d, d)),
                  full((1, d)), full((1, d)), full((d, ROUTER_LANES)), full((1, ROUTER_LANES))],
        out_specs=[row_spec(d), row_spec(d), row_spec(ROUTER_LANES)],
        out_shape=[jax.ShapeDtypeStruct((b, s, d), F32), jax.ShapeDtypeStruct((b, s, d), BF16),
                   jax.ShapeDtypeStruct((b, s, ROUTER_LANES), F32)],
        compiler_params=_cparams(("parallel", "parallel")),
        name="out_proj",
    )(ypre, ysc, ycf, x, mod, wglu_bf, b_glu.reshape(1, -1), wo_bf, ln_g.reshape(1, -1),
      ln_b.reshape(1, -1), w_router, b_router)


def _moe_kernel(h2_ref, wts_ref, x1_ref, mod_ref, wg_ref, wu_ref, wd_ref, lng_ref, lnb_ref, o_ref, acc_ref):
    e = pl.program_id(2)

    @pl.when(e == 0)
    def _():
        acc_ref[...] = jnp.zeros_like(acc_ref)

    h = h2_ref[0]
    gate = _dot(h, wg_ref[0])
    up = _dot(h, wu_ref[0])
    wts = wts_ref[0]
    lane = lax.broadcasted_iota(jnp.int32, wts.shape, 1)
    w_e = jnp.sum(jnp.where(lane == e + N_GROUPS, wts, 0.0), axis=1, keepdims=True)
    act = gate * _sigmoid(gate) * up * w_e
    acc_ref[...] += _dot(act.astype(BF16), wd_ref[0])

    @pl.when(e == pl.num_programs(2) - 1)
    def _():
        g2 = mod_ref[0, 5:6, :]
        o_ref[0] = _layer_norm(DN_ALPHA * x1_ref[0] + g2 * acc_ref[...], lng_ref[...], lnb_ref[...])


def _moe_call(h2, wts, x1, mod, wg_bf, wu_bf, wd_bf, ln_g, ln_b, tm):
    b, s, d = x1.shape
    row_spec = lambda n: pl.BlockSpec((1, tm, n), lambda i, j, e: (i, j, 0))
    full = lambda shape: pl.BlockSpec(shape, lambda i, j, e: (0,) * len(shape))
    return pl.pallas_call(
        _moe_kernel, grid=(b, s // tm, N_EXPERTS),
        in_specs=[row_spec(d), row_spec(ROUTER_LANES), row_spec(d),
                  pl.BlockSpec((1, 6, d), lambda i, j, e: (i, 0, 0)),
                  pl.BlockSpec((1, d, D_EXPERT), lambda i, j, e: (e, 0, 0)),
                  pl.BlockSpec((1, d, D_EXPERT), lambda i, j, e: (e, 0, 0)),
                  pl.BlockSpec((1, D_EXPERT, d), lambda i, j, e: (e, 0, 0)),
                  full((1, d)), full((1, d))],
        out_specs=row_spec(d),
        out_shape=jax.ShapeDtypeStruct((b, s, d), F32),
        scratch_shapes=[pltpu.VMEM((tm, d), F32)],
        compiler_params=_cparams(("parallel", "parallel", "arbitrary")),
        name="moe",
    )(h2, wts, x1, mod, wg_bf, wu_bf, wd_bf, ln_g.reshape(1, -1), ln_b.reshape(1, -1))


def kernel(x, c, ctx, c_ctx, w_mod, b_mod, w_in, s5_a_re, s5_a_im, s5_log_dt, s5_b_re, s5_b_im, s5_c_re, s5_c_im, s5_d, w_glu, b_glu, w_sc, w_dw, b_dw, ln_cf_g, ln_cf_b, w_o, ln1_g, ln1_b, w_rg, b_rg, w_rexp, b_rexp, w_gate, w_up, w_down, ln2_g, ln2_b):
    nb, seq, d = x.shape
    n_ctx = ctx.shape[1]
    n_layers = w_mod.shape[0]
    assert seq % TILE_TOK == 0 and n_ctx % TILE_TOK == 0 and seq % GRID_W == 0

    mod_rows = 16
    assert nb + 1 <= mod_rows
    c_all = jnp.concatenate([c, c_ctx[None, :], jnp.zeros((mod_rows - nb - 1, d), F32)], axis=0)
    mod_all = _mod_call(c_all, w_mod, b_mod)

    pad_r = ROUTER_LANES - N_GROUPS - N_EXPERTS
    x_lat, x_ctx = x, ctx
    for l in range(n_layers):
        last = l == n_layers - 1
        mod_lat = mod_all[l, :nb].reshape(nb, 6, d)
        mod_ctx = jnp.broadcast_to(mod_all[l, nb].reshape(1, 6, d), (nb, 6, d))
        w_in_bf = w_in[l].astype(BF16)
        wglu_bf = w_glu[l].astype(BF16)
        wo_bf = w_o[l].astype(BF16)
        wg_bf = w_gate[l].astype(BF16)
        wu_bf = w_up[l].astype(BF16)
        wd_bf = w_down[l].astype(BF16)
        w_router = jnp.concatenate([w_rg[l], w_rexp[l], jnp.zeros((d, pad_r), F32)], axis=1)
        b_router = jnp.concatenate([b_rg[l], b_rexp[l], jnp.zeros((pad_r,), F32)]).reshape(1, -1)
        w1, e_op, decay = _s5_operators(s5_a_re[l], s5_a_im[l], s5_log_dt[l], s5_b_re[l], s5_b_im[l],
                                        s5_c_re[l], s5_c_im[l])

        u_l, bg_l, cv_l, gl_l = _in_call(x_lat, mod_lat, w_in_bf, 512, False)
        if last:
            u_c = _in_call(x_ctx, mod_ctx, w_in_bf[:, :D_S5], TILE_TOK, True)
        else:
            u_c, bg_c, cv_c, gl_c = _in_call(x_ctx, mod_ctx, w_in_bf, TILE_TOK, False)

        yin_l, gf_l, gb_l = _s5a_call(u_l, w1)
        yin_c, gf_c, gb_c = _s5a_call(u_c, w1)
        hf_c, hf_l, hb_c, hb_l = _s5b_call(decay, gf_c, gf_l, gb_c, gb_l, nb)
        ypre_l = _s5c_call(yin_l, hf_l, hb_l, e_op, u_l, s5_d[l])

        ysc_l, ycf_l = _conv_call(bg_l, cv_l, gl_l, w_sc[l], w_dw[l], b_dw[l], ln_cf_g[l], ln_cf_b[l], True)
        x1_l, h2_l, wts_l = _out_call(ypre_l, ysc_l, ycf_l, x_lat, mod_lat, wglu_bf, b_glu[l], wo_bf,
                                      ln1_g[l], ln1_b[l], w_router, b_router, 512)
        if not last:
            ypre_c = _s5c_call(yin_c, hf_c, hb_c, e_op, u_c, s5_d[l])
            ysc_c, ycf_c = _conv_call(bg_c, cv_c, gl_c, w_sc[l], w_dw[l], b_dw[l], ln_cf_g[l], ln_cf_b[l], False)
            x1_c, h2_c, wts_c = _out_call(ypre_c, ysc_c, ycf_c, x_ctx, mod_ctx, wglu_bf, b_glu[l], wo_bf,
                                          ln1_g[l], ln1_b[l], w_router, b_router, TILE_TOK)
            x_ctx = _moe_call(h2_c, wts_c, x1_c, mod_ctx, wg_bf, wu_bf, wd_bf, ln2_g[l], ln2_b[l], TILE_TOK)
        x_lat = _moe_call(h2_l, wts_l, x1_l, mod_lat, wg_bf, wu_bf, wd_bf, ln2_g[l], ln2_b[l], min(1024, seq))
    return x_lat
```

```python
import functools
import math

import jax
import jax.numpy as jnp
from jax import lax
from jax.experimental import pallas as pl
from jax.experimental.pallas import tpu as pltpu

F32 = jnp.float32
BF16 = jnp.bfloat16

D_MODEL = 1024
DEPTH = 2
GRID_W = 64
D_S5 = 512
S5_GROUP = 16
S5_GROUPS = 32
S5_STATE = 64
D_SC = 256
D_CF = 256
CF_WIDTH = 31
CF_HALF = 15
D_IN = 1792
N_GROUPS = 4
EXP_PER_GROUP = 4
N_EXPERTS = 16
D_EXPERT = 256
DN_ALPHA = (2 * DEPTH) ** 0.25
LN_EPS = 1e-5

CHUNK = 16
TILE_CHUNKS = 16
TILE_TOK = CHUNK * TILE_CHUNKS
LANE_BLK = 128
GRP_PER_BLK = LANE_BLK // S5_GROUP
ROUTER_LANES = 128
HX_LANES = D_MODEL + ROUTER_LANES
META_WA, META_WB, META_CLS, META_RANK = 0, 1, 2, 3
N_CLASSES = N_GROUPS * 6
MOE_TM = 256
VMEM_LIMIT = 56 * 1024 * 1024


def _cparams(sem):
    return pltpu.CompilerParams(dimension_semantics=sem, vmem_limit_bytes=VMEM_LIMIT)


def _split_bf16(a):
    hi = a.astype(BF16)
    lo = (a - hi.astype(F32)).astype(BF16)
    return hi, lo


def _dot(a, b):
    return jnp.dot(a, b, preferred_element_type=F32)


def _dot3(a, b):
    ah, al = _split_bf16(a)
    bh, bl = _split_bf16(b)
    return _dot(ah, bh) + (_dot(al, bh) + _dot(ah, bl))


def _sigmoid(x):
    return 1.0 / (1.0 + jnp.exp(-x))


def _layer_norm(x, g, b):
    mu = jnp.mean(x, axis=-1, keepdims=True)
    xc = x - mu
    var = jnp.mean(xc * xc, axis=-1, keepdims=True)
    return xc * lax.rsqrt(var + LN_EPS) * g + b


def _mod_kernel(c_ref, w_ref, b_ref, o_ref):
    c = c_ref[...]
    s = c * _sigmoid(c)
    o_ref[0] = _dot3(s, w_ref[0]) + b_ref[0]


def _mod_call(c_all, w_mod, b_mod):
    n_layers, d, n_out = w_mod.shape
    tn = 1536
    rows = c_all.shape[0]
    return pl.pallas_call(
        _mod_kernel,
        grid=(n_layers, n_out // tn),
        in_specs=[
            pl.BlockSpec((rows, d), lambda l, j: (0, 0)),
            pl.BlockSpec((1, d, tn), lambda l, j: (l, 0, j)),
            pl.BlockSpec((1, 1, tn), lambda l, j: (l, 0, j)),
        ],
        out_specs=pl.BlockSpec((1, rows, tn), lambda l, j: (l, 0, j)),
        out_shape=jax.ShapeDtypeStruct((n_layers, rows, n_out), F32),
        compiler_params=_cparams(("parallel", "parallel")),
        name="mod",
    )(c_all, w_mod, b_mod.reshape(n_layers, 1, n_out))


def _in_kernel(x_ref, mod_ref, w_ref, u_ref, bg_ref, cv_ref, gl_ref):
    x = x_ref[0]
    sh = mod_ref[0, 0:1, :]
    sc = mod_ref[0, 1:2, :]
    h = (x * (1.0 + sc) + sh).astype(BF16)
    z = _dot(h, w_ref[...])
    u_ref[0] = z[:, 0:512]
    bg_ref[0] = z[:, 512:768]
    cv_ref[0] = z[:, 768:1024] * z[:, 1024:1280]
    gl_ref[0] = z[:, 1280:1536] * _sigmoid(z[:, 1536:1792])


def _in_u_kernel(x_ref, mod_ref, w_ref, u_ref):
    x = x_ref[0]
    sh = mod_ref[0, 0:1, :]
    sc = mod_ref[0, 1:2, :]
    h = (x * (1.0 + sc) + sh).astype(BF16)
    u_ref[0] = _dot(h, w_ref[...])


def _in_call(x, mod, w_in_bf, tm, u_only):
    b, s, d = x.shape
    grid = (b, s // tm)
    row_spec = lambda n: pl.BlockSpec((1, tm, n), lambda i, j: (i, j, 0))
    in_specs = [
        row_spec(d),
        pl.BlockSpec((1, 6, d), lambda i, j: (i, 0, 0)),
    ]
    if u_only:
        in_specs.append(pl.BlockSpec((d, D_S5), lambda i, j: (0, 0)))
        return pl.pallas_call(
            _in_u_kernel, grid=grid, in_specs=in_specs,
            out_specs=row_spec(D_S5),
            out_shape=jax.ShapeDtypeStruct((b, s, D_S5), F32),
            compiler_params=_cparams(("parallel", "parallel")),
            name="in_proj_u",
        )(x, mod, w_in_bf)
    in_specs.append(pl.BlockSpec((d, D_IN), lambda i, j: (0, 0)))
    return pl.pallas_call(
        _in_kernel, grid=grid, in_specs=in_specs,
        out_specs=[row_spec(D_S5), row_spec(D_SC), row_spec(D_SC), row_spec(D_CF)],
        out_shape=[jax.ShapeDtypeStruct((b, s, D_S5), F32),
                   jax.ShapeDtypeStruct((b, s, D_SC), F32),
                   jax.ShapeDtypeStruct((b, s, D_SC), F32),
                   jax.ShapeDtypeStruct((b, s, D_CF), F32)],
        compiler_params=_cparams(("parallel", "parallel")),
        name="in_proj",
    )(x, mod, w_in_bf)


def _conv_tail(t, bdw_ref, lng_ref, lnb_ref):
    t = t + bdw_ref[...]
    t = _layer_norm(t, lng_ref[...], lnb_ref[...])
    return t * _sigmoid(t)


def _conv_grid_kernel(bg_ref, cv_ref, gl_ref, wsc_ref, wdw_ref, bdw_ref, lng_ref, lnb_ref,
                      ysc_ref, ycf_ref, pad_ref, t_ref):
    s = cv_ref.shape[1]
    rows = s // GRID_W
    cv = cv_ref[0]
    col = lax.broadcasted_iota(jnp.int32, (s, D_SC), 0) % GRID_W
    prev = jnp.where(col == 0, 0.0, pltpu.roll(cv, 1, axis=0))
    nxt = jnp.where(col == GRID_W - 1, 0.0, pltpu.roll(cv, s - 1, axis=0))
    conv = prev * wsc_ref[0:1, :] + cv * wsc_ref[1:2, :] + nxt * wsc_ref[2:3, :]
    ysc_ref[0] = (bg_ref[0] * conv).astype(ysc_ref.dtype)

    zero = jnp.zeros((CF_HALF, GRID_W, D_CF), F32)
    pad_ref[0:CF_HALF] = zero
    pad_ref[CF_HALF + rows:CF_HALF + rows + CF_HALF] = zero
    pad_ref[CF_HALF:CF_HALF + rows] = gl_ref[0].reshape(rows, GRID_W, D_CF)

    def body(i, carry):
        w0 = pl.multiple_of(i * 8, 8)
        for half in range(D_CF // 128):
            lanes = slice(half * 128, (half + 1) * 128)
            acc = jnp.zeros((rows, 8, 128), F32)
            for k in range(CF_WIDTH):
                acc = acc + pad_ref[k:k + rows, pl.ds(w0, 8), lanes] * wdw_ref[k:k + 1, lanes]
            t_ref[:, pl.ds(w0, 8), lanes] = acc
        return carry

    lax.fori_loop(0, GRID_W // 8, body, 0)
    t = t_ref[...].reshape(s, D_CF)
    ycf_ref[0] = _conv_tail(t, bdw_ref, lng_ref, lnb_ref).astype(ycf_ref.dtype)


def _conv_seq_kernel(bg_ref, cv_ref, gl_ref, wsc_ref, wdw_ref, bdw_ref, lng_ref, lnb_ref,
                     ysc_ref, ycf_ref, pad_ref):
    s = cv_ref.shape[1]
    cv = cv_ref[0]
    pos = lax.broadcasted_iota(jnp.int32, (s, D_SC), 0)
    prev = jnp.where(pos == 0, 0.0, pltpu.roll(cv, 1, axis=0))
    nxt = jnp.where(pos == s - 1, 0.0, pltpu.roll(cv, s - 1, axis=0))
    conv = prev * wsc_ref[0:1, :] + cv * wsc_ref[1:2, :] + nxt * wsc_ref[2:3, :]
    ysc_ref[0] = (bg_ref[0] * conv).astype(ysc_ref.dtype)

    off = 16
    pad_ref[0:off] = jnp.zeros((off, D_CF), F32)
    pad_ref[off + s:off + s + 16] = jnp.zeros((16, D_CF), F32)
    pad_ref[off:off + s] = gl_ref[0]
    acc = jnp.zeros((s, D_CF), F32)
    for k in range(CF_WIDTH):
        acc = acc + pad_ref[pl.ds(off - CF_HALF + k, s), :] * wdw_ref[k:k + 1, :]
    ycf_ref[0] = _conv_tail(acc, bdw_ref, lng_ref, lnb_ref).astype(ycf_ref.dtype)


def _conv_call(bg, cv, gl, w_sc, w_dw, b_dw, ln_g, ln_b, grid_mode):
    b, s, _ = bg.shape
    row_spec = pl.BlockSpec((1, s, D_SC), lambda i: (i, 0, 0))
    full = lambda shape: pl.BlockSpec(shape, lambda i: (0,) * len(shape))
    if grid_mode:
        rows = s // GRID_W
        kern = _conv_grid_kernel
        scratch = [pltpu.VMEM((rows + 2 * CF_HALF, GRID_W, D_CF), F32),
                   pltpu.VMEM((rows, GRID_W, D_CF), F32)]
        name = "conv_grid"
    else:
        kern = _conv_seq_kernel
        scratch = [pltpu.VMEM((s + 32, D_CF), F32)]
        name = "conv_seq"
    return pl.pallas_call(
        kern, grid=(b,),
        in_specs=[row_spec, row_spec, row_spec, full((3, D_SC)), full((CF_WIDTH, D_CF)),
                  full((1, D_CF)), full((1, D_CF)), full((1, D_CF))],
        out_specs=[row_spec, row_spec],
        out_shape=[jax.ShapeDtypeStruct((b, s, D_SC), BF16), jax.ShapeDtypeStruct((b, s, D_CF), BF16)],
        scratch_shapes=scratch,
        compiler_params=_cparams(("parallel",)),
        name=name,
    )(bg, cv, gl, w_sc, w_dw, b_dw.reshape(1, -1), ln_g.reshape(1, -1), ln_b.reshape(1, -1))


def _split3(a):
    hi = a.astype(BF16)
    r = a - hi.astype(F32)
    mid = r.astype(BF16)
    lo = (r - mid.astype(F32)).astype(BF16)
    return hi, mid, lo


def _select_cols(a, sel):
    hi, mid, lo = _split3(a)
    return _dot(hi, sel) + (_dot(mid, sel) + _dot(lo, sel))


def _select_rows(sel, a):
    hi, mid, lo = _split3(a)
    return _dot(sel, hi) + (_dot(sel, mid) + _dot(sel, lo))


def _cmul(ar, ai, br, bi):
    return ar * br - ai * bi, ar * bi + ai * br


def _s5_prep_kernel(are_r, aim_r, are_c, aim_c, ldt, bre_ref, bim_ref, cre_ref, cim_ref,
                    w1_ref, e_ref, dec_ref):
    t, n, p = CHUNK, S5_GROUP, S5_STATE
    width = t * n
    lane_tok = lax.broadcasted_iota(jnp.int32, (128, width), 1) // n
    pow_id = lax.broadcasted_iota(jnp.int32, (128, width), 0)
    onehot = lambda cond: jnp.where(cond, 1.0, 0.0).astype(BF16)
    sel_fwd = onehot(pow_id == lane_tok)
    sel_rev = onehot(pow_id == t - 1 - lane_tok)
    sel_out = onehot(pow_id == t - lane_tok)
    row_tok = lax.broadcasted_iota(jnp.int32, (width, 32), 0) // n
    row_pow = lax.broadcasted_iota(jnp.int32, (width, 32), 1)
    rsel_rev = onehot(row_pow == t - 1 - row_tok)
    rsel_fwd = onehot(row_pow == row_tok)
    lane = lax.broadcasted_iota(jnp.int32, (n, width), 1)

    jc = jnp.minimum(lax.broadcasted_iota(jnp.int32, (p, 128), 1), t).astype(F32)
    jr = jnp.minimum(lax.broadcasted_iota(jnp.int32, (32, p), 0), t).astype(F32)

    strips = []
    f_parts = []
    e_parts = []
    for d in range(2):
        dt = jnp.exp(ldt[d, 0])
        mag_c = are_c[d, 0] * dt
        th_c = aim_c[d, 0] * dt
        ec = jnp.exp(jc * mag_c)
        qr = ec * jnp.cos(jc * th_c)
        qi = ec * jnp.sin(jc * th_c)
        a_re = are_r[d, 0]
        a_im = aim_r[d, 0]
        er = jnp.exp(jr * (a_re * dt))
        pr = er * jnp.cos(jr * (a_im * dt))
        pi = er * jnp.sin(jr * (a_im * dt))
        nr = pr[1:2] - 1.0
        ni = pi[1:2]
        den = a_re * a_re + a_im * a_im
        fre = (nr * a_re + ni * a_im) / den
        fim = (ni * a_re - nr * a_im) / den
        bt_re = bre_ref[d, 0].T
        bt_im = bim_ref[d, 0].T
        bb_re, bb_im = _cmul(fre, fim, bt_re, bt_im)
        ct_re = jnp.concatenate([cre_ref[d, 0].T] * t, axis=1)
        ct_im = jnp.concatenate([cim_ref[d, 0].T] * t, axis=1)
        sel = sel_fwd if d == 0 else sel_rev
        w_re, w_im = _cmul(ct_re, ct_im, _select_cols(qr, sel), _select_cols(qi, sel))
        strips.append(_dot3(bb_re, w_re) - _dot3(bb_im, w_im))
        if d == 0:
            o_re, o_im = _cmul(w_re, w_im, qr[:, 1:2], qi[:, 1:2])
        else:
            o_re, o_im = _cmul(ct_re, ct_im, _select_cols(qr, sel_out), _select_cols(qi, sel_out))
        e_parts += [o_re, -o_im]
        rsel = rsel_rev if d == 0 else rsel_fwd
        f_re, f_im = _cmul(jnp.concatenate([bb_re] * t, axis=0), jnp.concatenate([bb_im] * t, axis=0),
                           _select_rows(rsel, pr), _select_rows(rsel, pi))
        f_parts += [f_re, f_im]
        dec_ref[0, 2 * d:2 * d + 1, :] = jnp.concatenate([pr[t:t + 1], pr[t:t + 1]], axis=1)
        dec_ref[0, 2 * d + 1:2 * d + 2, :] = jnp.concatenate([-pi[t:t + 1], pi[t:t + 1]], axis=1)

    blocks = []
    for s in range(t):
        fwd = strips[0] if s == 0 else jnp.where(lane >= n * s, pltpu.roll(strips[0], n * s, axis=1), 0.0)
        back = t - 1 - s
        bwd = strips[1] if back == 0 else jnp.where(lane < width - n * back,
                                                     pltpu.roll(strips[1], width - n * back, axis=1), 0.0)
        blocks.append(fwd + bwd)
    m = jnp.concatenate(blocks, axis=0)
    w1_ref[0] = jnp.concatenate([m] + f_parts, axis=1).astype(BF16)
    e_ref[0] = jnp.concatenate(e_parts, axis=0).astype(BF16)


def _s5_operators(a_re, a_im, log_dt, b_re, b_im, c_re, c_im):
    g, p, n, t = S5_GROUPS, S5_STATE, S5_GROUP, CHUNK
    spec = lambda shape: pl.BlockSpec((2, 1) + shape, lambda i: (0, i) + (0,) * len(shape))
    return pl.pallas_call(
        _s5_prep_kernel, grid=(g,),
        in_specs=[spec((1, p)), spec((1, p)), spec((p, 1)), spec((p, 1)), spec((1, 1)),
                  spec((p, n)), spec((p, n)), spec((n, p)), spec((n, p))],
        out_specs=[pl.BlockSpec((1, t * n, 2 * t * n), lambda i: (i, 0, 0)),
                   pl.BlockSpec((1, 4 * p, t * n), lambda i: (i, 0, 0)),
                   pl.BlockSpec((1, 4, 2 * p), lambda i: (i, 0, 0))],
        out_shape=[jax.ShapeDtypeStruct((g, t * n, 2 * t * n), BF16),
                   jax.ShapeDtypeStruct((g, 4 * p, t * n), BF16),
                   jax.ShapeDtypeStruct((g, 4, 2 * p), F32)],
        compiler_params=_cparams(("parallel",)),
        name="s5_prep",
    )(a_re.reshape(2, g, 1, p), a_im.reshape(2, g, 1, p), a_re.reshape(2, g, p, 1),
      a_im.reshape(2, g, p, 1), log_dt.reshape(2, g, 1, 1), b_re, b_im, c_re, c_im)


def _block_transpose8(ps):
    ps = list(ps)
    blk = lax.broadcasted_iota(jnp.int32, ps[0].shape, 1) // S5_GROUP
    for k in range(3):
        step = 1 << k
        shift = S5_GROUP * step
        keep = ((blk >> k) & 1) == 0
        for a in range(8):
            if a & step:
                continue
            pa, pb = ps[a], ps[a + step]
            ps[a] = jnp.where(keep, pa, pltpu.roll(pb, shift, axis=1))
            ps[a + step] = jnp.where(keep, pltpu.roll(pa, 128 - shift, axis=1), pb)
    return ps


def _s5a_kernel(u_ref, w1_ref, yin_ref, gf_ref, gb_ref):
    nb = u_ref.shape[0]
    xs = []
    for s in range(CHUNK):
        parts = [u_ref[b, pl.ds(s, TILE_CHUNKS, stride=CHUNK), :] for b in range(nb)]
        xs.append(jnp.concatenate(parts, axis=0))
    lo = _block_transpose8(xs[:8])
    hi = _block_transpose8(xs[8:])
    for j in range(GRP_PER_BLK):
        og = jnp.concatenate([lo[j], hi[j]], axis=1)
        r = _dot(og.astype(BF16), w1_ref[j])
        yin_ref[j] = r[:, 0:256]
        gf_ref[j] = r[:, 256:384]
        gb_ref[j] = r[:, 384:512]


def _s5a_call(u, w1):
    b, s, _ = u.shape
    nt = s // TILE_TOK
    rows = b * TILE_CHUNKS
    n_blk = D_S5 // LANE_BLK
    out_spec = lambda n: pl.BlockSpec((GRP_PER_BLK, rows, n), lambda j, l: (l, j, 0))
    return pl.pallas_call(
        _s5a_kernel, grid=(nt, n_blk),
        in_specs=[pl.BlockSpec((b, TILE_TOK, LANE_BLK), lambda j, l: (0, j, l)),
                  pl.BlockSpec((GRP_PER_BLK, 256, 512), lambda j, l: (l, 0, 0))],
        out_specs=[out_spec(256), out_spec(128), out_spec(128)],
        out_shape=[jax.ShapeDtypeStruct((S5_GROUPS, nt * rows, 256), F32),
                   jax.ShapeDtypeStruct((S5_GROUPS, nt * rows, 128), F32),
                   jax.ShapeDtypeStruct((S5_GROUPS, nt * rows, 128), F32)],
        compiler_params=_cparams(("parallel", "parallel")),
        name="s5_chunk_in",
    )(u, w1)


def _s5b_kernel(nb, a_ref, gfc_ref, gfl_ref, gbc_ref, gbl_ref, hfc_ref, hfl_ref, hbc_ref, hbl_ref):
    gb = a_ref.shape[0]
    rows = nb * TILE_CHUNKS
    n_lat = gfl_ref.shape[1] // rows
    a1f = [jnp.broadcast_to(a_ref[g, 0:1, :], (nb, 128)) for g in range(gb)]
    a2f = [jnp.broadcast_to(a_ref[g, 1:2, :], (nb, 128)) for g in range(gb)]
    a1b = [jnp.broadcast_to(a_ref[g, 2:3, :], (nb, 128)) for g in range(gb)]
    a2b = [jnp.broadcast_to(a_ref[g, 3:4, :], (nb, 128)) for g in range(gb)]

    def step(h, a1, a2, g_ref, h_ref, g, row):
        h_ref[g, row, :] = h
        return a1 * h + a2 * pltpu.roll(h, 64, axis=1) + g_ref[g, row, :]

    hf = [jnp.zeros((nb, 128), F32) for _ in range(gb)]
    hb = [jnp.zeros((nb, 128), F32) for _ in range(gb)]
    for ci in range(TILE_CHUNKS):
        rf = pl.ds(ci, nb, stride=TILE_CHUNKS)
        rb = pl.ds(TILE_CHUNKS - 1 - ci, nb, stride=TILE_CHUNKS)
        for g in range(gb):
            hf[g] = step(hf[g], a1f[g], a2f[g], gfc_ref, hfc_ref, g, rf)
            hb[g] = step(hb[g], a1b[g], a2b[g], gbc_ref, hbc_ref, g, rb)

    def body(j, carry):
        hf, hb = carry
        hf = list(hf)
        hb = list(hb)
        base_f = j * rows
        base_b = (n_lat - 1 - j) * rows
        for ci in range(TILE_CHUNKS):
            rf = pl.ds(base_f + ci, nb, stride=TILE_CHUNKS)
            rb = pl.ds(base_b + (TILE_CHUNKS - 1 - ci), nb, stride=TILE_CHUNKS)
            for g in range(gb):
                hf[g] = step(hf[g], a1f[g], a2f[g], gfl_ref, hfl_ref, g, rf)
                hb[g] = step(hb[g], a1b[g], a2b[g], gbl_ref, hbl_ref, g, rb)
        return tuple(hf), tuple(hb)

    lax.fori_loop(0, n_lat, body, (tuple(hf), tuple(hb)))


def _s5b_call(decay, gf_c, gf_l, gb_c, gb_l, nb):
    gblk = 4
    spec = lambda a: pl.BlockSpec((gblk, a.shape[1], 128), lambda i: (i, 0, 0))
    sds = lambda a: jax.ShapeDtypeStruct(a.shape, F32)
    return pl.pallas_call(
        functools.partial(_s5b_kernel, nb), grid=(S5_GROUPS // gblk,),
        in_specs=[pl.BlockSpec((gblk, 4, 128), lambda i: (i, 0, 0)),
                  spec(gf_c), spec(gf_l), spec(gb_c), spec(gb_l)],
        out_specs=[spec(gf_c), spec(gf_l), spec(gb_c), spec(gb_l)],
        out_shape=[sds(gf_c), sds(gf_l), sds(gb_c), sds(gb_l)],
        compiler_params=_cparams(("parallel",)),
        name="s5_state_scan",
    )(decay, gf_c, gf_l, gb_c, gb_l)


def _s5c_kernel(yin_ref, hf_ref, hb_ref, e_ref, u_ref, d_ref, y_ref):
    nb = u_ref.shape[0]
    ys = []
    for j in range(GRP_PER_BLK):
        h = jnp.concatenate([hf_ref[j], hb_ref[j]], axis=1).astype(BF16)
        ys.append(yin_ref[j] + _dot(h, e_ref[j]))
    at = (_block_transpose8([y[:, :128] for y in ys])
          + _block_transpose8([y[:, 128:] for y in ys]))
    d = d_ref[...]
    for t in range(CHUNK):
        for b in range(nb):
            rows = pl.ds(t, TILE_CHUNKS, stride=CHUNK)
            y_ref[b, rows, :] = at[t][b * TILE_CHUNKS:(b + 1) * TILE_CHUNKS] + d * u_ref[b, rows, :]


def _s5c_call(yin, hf, hb, e, u, d_skip):
    b, s, _ = u.shape
    nt = s // TILE_TOK
    rows = b * TILE_CHUNKS
    n_blk = D_S5 // LANE_BLK
    gspec = lambda n: pl.BlockSpec((GRP_PER_BLK, rows, n), lambda j, l: (l, j, 0))
    tok_spec = pl.BlockSpec((b, TILE_TOK, LANE_BLK), lambda j, l: (0, j, l))
    return pl.pallas_call(
        _s5c_kernel, grid=(nt, n_blk),
        in_specs=[gspec(256), gspec(128), gspec(128),
                  pl.BlockSpec((GRP_PER_BLK, 256, 256), lambda j, l: (l, 0, 0)),
                  tok_spec,
                  pl.BlockSpec((1, LANE_BLK), lambda j, l: (0, l))],
        out_specs=tok_spec,
        out_shape=jax.ShapeDtypeStruct((b, s, D_S5), F32),
        compiler_params=_cparams(("parallel", "parallel")),
        name="s5_chunk_out",
    )(yin, hf, hb, e, u, d_skip.reshape(1, D_S5))


def _gelu_tanh(x):
    return 0.5 * x * (1.0 + jnp.tanh(math.sqrt(2.0 / math.pi) * (x + 0.044715 * (x * x * x))))


def _route(logits):
    lane = lax.broadcasted_iota(jnp.int32, logits.shape, 1).astype(F32)
    neg = jnp.float32(-1e30)
    big = jnp.float32(1e9)
    gl = jnp.where(lane < N_GROUPS, logits, neg)
    gmax = jnp.max(gl, axis=1, keepdims=True)
    gidx = jnp.min(jnp.where(gl == gmax, lane, big), axis=1, keepdims=True)
    gsum = jnp.sum(jnp.exp(gl - gmax), axis=1, keepdims=True)
    gw = 1.0 / gsum
    lo = N_GROUPS + EXP_PER_GROUP * gidx
    el = jnp.where((lane >= lo) & (lane < lo + EXP_PER_GROUP), logits, neg)
    v1 = jnp.max(el, axis=1, keepdims=True)
    i1 = jnp.min(jnp.where(el == v1, lane, big), axis=1, keepdims=True)
    el2 = jnp.where(lane == i1, neg, el)
    v2 = jnp.max(el2, axis=1, keepdims=True)
    i2 = jnp.min(jnp.where(el2 == v2, lane, big), axis=1, keepdims=True)
    ex = jnp.exp(v2 - v1)
    p1 = 1.0 / (1.0 + ex)
    p2 = ex * p1
    e1 = i1 - lo
    e2 = i2 - lo
    first = e1 < e2
    ea = jnp.where(first, e1, e2)
    eb = jnp.where(first, e2, e1)
    wa = gw * jnp.where(first, p1, p2)
    wb = gw * jnp.where(first, p2, p1)
    pair = ea * (7.0 - ea) * 0.5 + (eb - ea - 1.0)
    return wa, wb, 6.0 * gidx + pair


def _out_kernel(ypre_ref, ysc_ref, ycf_ref, x_ref, mod_ref, wglu_ref, bglu_ref, wo_ref,
                lng_ref, lnb_ref, wr_ref, br_ref, x1_ref, hx_ref, meta_ref, counts_ref, cnt_ref):
    @pl.when((pl.program_id(0) == 0) & (pl.program_id(1) == 0))
    def _():
        cnt_ref[...] = jnp.zeros_like(cnt_ref)

    t = _gelu_tanh(ypre_ref[0])
    gate = _sigmoid(_dot(t.astype(BF16), wglu_ref[...]) + bglu_ref[...])
    ys5 = (t * gate).astype(BF16)
    y = (_dot(ys5, wo_ref[0:D_S5, :]) + _dot(ysc_ref[0], wo_ref[D_S5:D_S5 + D_SC, :])
         + _dot(ycf_ref[0], wo_ref[D_S5 + D_SC:D_MODEL, :]))
    g1 = mod_ref[0, 2:3, :]
    x1 = _layer_norm(DN_ALPHA * x_ref[0] + g1 * y, lng_ref[...], lnb_ref[...])
    x1_ref[0] = x1
    h2 = x1 * (1.0 + mod_ref[0, 4:5, :]) + mod_ref[0, 3:4, :]
    wa, wb, cls = _route(_dot3(h2, wr_ref[...]) + br_ref[...])

    tm = h2.shape[0]
    lane = lax.broadcasted_iota(jnp.int32, (tm, ROUTER_LANES), 1).astype(F32)
    onehot = jnp.where(lane == cls, 1.0, 0.0)
    row_i = lax.broadcasted_iota(jnp.int32, (tm, tm), 0)
    col_i = lax.broadcasted_iota(jnp.int32, (tm, tm), 1)
    earlier = jnp.where(col_i < row_i, 1.0, 0.0).astype(BF16)
    before = _dot(earlier, onehot.astype(BF16)) + cnt_ref[...]
    rank = jnp.sum(before * onehot, axis=1, keepdims=True)
    cnt_ref[...] += jnp.sum(onehot, axis=0, keepdims=True)
    counts_ref[...] = cnt_ref[...]

    meta = (jnp.where(lane == META_WA, wa, 0.0) + jnp.where(lane == META_WB, wb, 0.0)
            + jnp.where(lane == META_CLS, cls, 0.0) + jnp.where(lane == META_RANK, rank, 0.0))
    meta_ref[0] = meta
    hx_ref[0, :, 0:D_MODEL] = h2
    hx_ref[0, :, D_MODEL:HX_LANES] = meta


def _out_call(ypre, ysc, ycf, x, mod, wglu_bf, b_glu, wo_bf, ln_g, ln_b, w_router, b_router, tm):
    b, s, d = x.shape
    row_spec = lambda n: pl.BlockSpec((1, tm, n), lambda i, j: (i, j, 0))
    full = lambda shape: pl.BlockSpec(shape, lambda i, j: (0,) * len(shape))
    return pl.pallas_call(
        _out_kernel, grid=(b, s // tm),
        in_specs=[row_spec(D_S5), row_spec(D_SC), row_spec(D_CF), row_spec(d),
                  pl.BlockSpec((1, 6, d), lambda i, j: (i, 0, 0)),
                  full((D_S5, D_S5)), full((1, D_S5)), full((d, d)),
                  full((1, d)), full((1, d)), full((d, ROUTER_LANES)), full((1, ROUTER_LANES))],
        out_specs=[row_spec(d), row_spec(HX_LANES), row_spec(ROUTER_LANES), full((1, ROUTER_LANES))],
        out_shape=[jax.ShapeDtypeStruct((b, s, d), F32), jax.ShapeDtypeStruct((b, s, HX_LANES), F32),
                   jax.ShapeDtypeStruct((b, s, ROUTER_LANES), F32),
                   jax.ShapeDtypeStruct((1, ROUTER_LANES), F32)],
        scratch_shapes=[pltpu.VMEM((1, ROUTER_LANES), F32)],
        compiler_params=_cparams(("arbitrary", "arbitrary")),
        name="out_proj",
    )(ypre, ysc, ycf, x, mod, wglu_bf, b_glu.reshape(1, -1), wo_bf, ln_g.reshape(1, -1),
      ln_b.reshape(1, -1), w_router, b_router)


def _moe_plan(meta, counts, n_tok):
    cls = meta[..., META_CLS].reshape(-1).astype(jnp.int32)
    rank = meta[..., META_RANK].reshape(-1).astype(jnp.int32)
    cnt = counts[0, :N_CLASSES].astype(jnp.int32)
    n_tiles = (cnt + (MOE_TM - 1)) // MOE_TM
    ends = jnp.cumsum(n_tiles)
    starts = ends - n_tiles
    slot = starts[cls] * MOE_TM + rank
    t_max = n_tok // MOE_TM + N_CLASSES
    n_used = ends[N_CLASSES - 1]
    tile = jnp.minimum(jnp.arange(t_max, dtype=jnp.int32), n_used - 1)
    tile_cls = jnp.sum((tile[:, None] >= ends[None, :]).astype(jnp.int32), axis=1)
    group = tile_cls // 6
    pair = tile_cls % 6
    first = jnp.array([0, 0, 0, 1, 1, 2], jnp.int32)[pair] + EXP_PER_GROUP * group
    second = jnp.array([1, 2, 3, 2, 3, 3], jnp.int32)[pair] + EXP_PER_GROUP * group
    return slot, tile, first, second, n_used.reshape(1)


def _dispatch_kernel(slot_ref, hx_ref, xs_init_ref, xs_ref, sem):
    del xs_init_ref
    tm = hx_ref.shape[1]
    base = (pl.program_id(0) * pl.num_programs(1) + pl.program_id(1)) * tm

    def body(r, carry):
        dst = slot_ref[base + r]
        pltpu.make_async_copy(hx_ref.at[0, pl.ds(r, 1), :], xs_ref.at[pl.ds(dst, 1), :], sem).start()
        return carry

    lax.fori_loop(0, tm, body, 0)
    pltpu.make_async_copy(hx_ref.at[0], xs_ref.at[pl.ds(0, tm), :], sem).wait()


def _dispatch_call(slot, hx, n_rows, tm):
    b, s, w = hx.shape
    xs_init = jnp.zeros((n_rows, w), F32)
    grid_spec = pltpu.PrefetchScalarGridSpec(
        num_scalar_prefetch=1, grid=(b, s // tm),
        in_specs=[pl.BlockSpec((1, tm, w), lambda i, j, slot: (i, j, 0)),
                  pl.BlockSpec(memory_space=pl.ANY)],
        out_specs=pl.BlockSpec(memory_space=pl.ANY),
        scratch_shapes=[pltpu.SemaphoreType.DMA(())])
    return pl.pallas_call(
        _dispatch_kernel, grid_spec=grid_spec,
        out_shape=jax.ShapeDtypeStruct((n_rows, w), F32),
        input_output_aliases={2: 0},
        compiler_params=_cparams(("arbitrary", "arbitrary")),
        name="moe_dispatch",
    )(slot, hx, xs_init)


def _moe_kernel(tile_ref, first_ref, second_ref, nused_ref, xs_ref, wga_ref, wgb_ref, wua_ref, wub_ref,
                wda_ref, wdb_ref, ys_ref):
    del tile_ref, first_ref, second_ref

    @pl.when(pl.program_id(0) < nused_ref[0])
    def _():
        x = xs_ref[...]
        xb = x[:, 0:D_MODEL].astype(BF16)

        def expert(wg_ref, wu_ref, wd_ref, w):
            gate = _dot(xb, wg_ref[0])
            up = _dot(xb, wu_ref[0])
            act = gate * _sigmoid(gate) * up * w
            return _dot(act.astype(BF16), wd_ref[0])

        wa = x[:, D_MODEL + META_WA:D_MODEL + META_WA + 1]
        wb = x[:, D_MODEL + META_WB:D_MODEL + META_WB + 1]
        ys_ref[...] = expert(wga_ref, wua_ref, wda_ref, wa) + expert(wgb_ref, wub_ref, wdb_ref, wb)

    @pl.when(pl.program_id(0) >= nused_ref[0])
    def _():
        ys_ref[...] = jnp.zeros_like(ys_ref)


def _moe_call(tile, first, second, n_used, xs, wg_bf, wu_bf, wd_bf):
    n_rows, w = xs.shape
    d = D_MODEL
    t_max = tile.shape[0]
    up_spec = lambda sel: pl.BlockSpec((1, d, D_EXPERT), lambda t, tl, fi, se, nu: ((fi, se)[sel][t], 0, 0))
    down_spec = lambda sel: pl.BlockSpec((1, D_EXPERT, d), lambda t, tl, fi, se, nu: ((fi, se)[sel][t], 0, 0))
    grid_spec = pltpu.PrefetchScalarGridSpec(
        num_scalar_prefetch=4, grid=(t_max,),
        in_specs=[pl.BlockSpec((MOE_TM, w), lambda t, tl, fi, se, nu: (tl[t], 0)),
                  up_spec(0), up_spec(1), up_spec(0), up_spec(1), down_spec(0), down_spec(1)],
        out_specs=pl.BlockSpec((MOE_TM, d), lambda t, tl, fi, se, nu: (t, 0)))
    return pl.pallas_call(
        _moe_kernel, grid_spec=grid_spec,
        out_shape=jax.ShapeDtypeStruct((n_rows, d), F32),
        compiler_params=_cparams(("arbitrary",)),
        name="moe_experts",
    )(tile, first, second, n_used, xs, wg_bf, wg_bf, wu_bf, wu_bf, wd_bf, wd_bf)


def _combine_kernel(slot_ref, x1_ref, mod_ref, lng_ref, lnb_ref, ys_ref, o_ref, f_ref, sem):
    tm = x1_ref.shape[1]
    base = (pl.program_id(0) * pl.num_programs(1) + pl.program_id(1)) * tm

    def body(r, carry):
        src = slot_ref[base + r]
        pltpu.make_async_copy(ys_ref.at[pl.ds(src, 1), :], f_ref.at[pl.ds(r, 1), :], sem).start()
        return carry

    lax.fori_loop(0, tm, body, 0)
    pltpu.make_async_copy(ys_ref.at[pl.ds(0, tm), :], f_ref, sem).wait()
    g2 = mod_ref[0, 5:6, :]
    o_ref[0] = _layer_norm(DN_ALPHA * x1_ref[0] + g2 * f_ref[...], lng_ref[...], lnb_ref[...])


def _combine_call(slot, x1, mod, ln_g, ln_b, ys, tm):
    b, s, d = x1.shape
    grid_spec = pltpu.PrefetchScalarGridSpec(
        num_scalar_prefetch=1, grid=(b, s // tm),
        in_specs=[pl.BlockSpec((1, tm, d), lambda i, j, slot: (i, j, 0)),
                  pl.BlockSpec((1, 6, d), lambda i, j, slot: (i, 0, 0)),
                  pl.BlockSpec((1, d), lambda i, j, slot: (0, 0)),
                  pl.BlockSpec((1, d), lambda i, j, slot: (0, 0)),
                  pl.BlockSpec(memory_space=pl.ANY)],
        out_specs=pl.BlockSpec((1, tm, d), lambda i, j, slot: (i, j, 0)),
        scratch_shapes=[pltpu.VMEM((tm, d), F32), pltpu.SemaphoreType.DMA(())])
    return pl.pallas_call(
        _combine_kernel, grid_spec=grid_spec,
        out_shape=jax.ShapeDtypeStruct((b, s, d), F32),
        compiler_params=_cparams(("arbitrary", "arbitrary")),
        name="moe_combine",
    )(slot, x1, mod, ln_g.reshape(1, -1), ln_b.reshape(1, -1), ys)


def _moe_sublayer(hx, meta, counts, x1, mod, wg_bf, wu_bf, wd_bf, ln_g, ln_b, tm):
    b, s, _ = x1.shape
    n_tok = b * s
    slot, tile, first, second, n_used = _moe_plan(meta, counts, n_tok)
    n_rows = n_tok + N_CLASSES * MOE_TM
    xs = _dispatch_call(slot, hx, n_rows, tm)
    ys = _moe_call(tile, first, second, n_used, xs, wg_bf, wu_bf, wd_bf)
    return _combine_call(slot, x1, mod, ln_g, ln_b, ys, tm)


def kernel(x, c, ctx, c_ctx, w_mod, b_mod, w_in, s5_a_re, s5_a_im, s5_log_dt, s5_b_re, s5_b_im, s5_c_re, s5_c_im, s5_d, w_glu, b_glu, w_sc, w_dw, b_dw, ln_cf_g, ln_cf_b, w_o, ln1_g, ln1_b, w_rg, b_rg, w_rexp, b_rexp, w_gate, w_up, w_down, ln2_g, ln2_b):
    nb, seq, d = x.shape
    n_ctx = ctx.shape[1]
    n_layers = w_mod.shape[0]
    assert seq % TILE_TOK == 0 and n_ctx % TILE_TOK == 0 and seq % GRID_W == 0

    mod_rows = 16
    assert nb + 1 <= mod_rows
    c_all = jnp.concatenate([c, c_ctx[None, :], jnp.zeros((mod_rows - nb - 1, d), F32)], axis=0)
    mod_all = _mod_call(c_all, w_mod, b_mod)

    pad_r = ROUTER_LANES - N_GROUPS - N_EXPERTS
    x_lat, x_ctx = x, ctx
    for l in range(n_layers):
        last = l == n_layers - 1
        mod_lat = mod_all[l, :nb].reshape(nb, 6, d)
        mod_ctx = jnp.broadcast_to(mod_all[l, nb].reshape(1, 6, d), (nb, 6, d))
        w_in_bf = w_in[l].astype(BF16)
        wglu_bf = w_glu[l].astype(BF16)
        wo_bf = w_o[l].astype(BF16)
        wg_bf = w_gate[l].astype(BF16)
        wu_bf = w_up[l].astype(BF16)
        wd_bf = w_down[l].astype(BF16)
        w_router = jnp.concatenate([w_rg[l], w_rexp[l], jnp.zeros((d, pad_r), F32)], axis=1)
        b_router = jnp.concatenate([b_rg[l], b_rexp[l], jnp.zeros((pad_r,), F32)]).reshape(1, -1)
        w1, e_op, decay = _s5_operators(s5_a_re[l], s5_a_im[l], s5_log_dt[l], s5_b_re[l], s5_b_im[l],
                                        s5_c_re[l], s5_c_im[l])

        u_l, bg_l, cv_l, gl_l = _in_call(x_lat, mod_lat, w_in_bf, 512, False)
        if last:
            u_c = _in_call(x_ctx, mod_ctx, w_in_bf[:, :D_S5], TILE_TOK, True)
        else:
            u_c, bg_c, cv_c, gl_c = _in_call(x_ctx, mod_ctx, w_in_bf, TILE_TOK, False)

        yin_l, gf_l, gb_l = _s5a_call(u_l, w1)
        yin_c, gf_c, gb_c = _s5a_call(u_c, w1)
        hf_c, hf_l, hb_c, hb_l = _s5b_call(decay, gf_c, gf_l, gb_c, gb_l, nb)
        ypre_l = _s5c_call(yin_l, hf_l, hb_l, e_op, u_l, s5_d[l])

        ysc_l, ycf_l = _conv_call(bg_l, cv_l, gl_l, w_sc[l], w_dw[l], b_dw[l], ln_cf_g[l], ln_cf_b[l], True)
        x1_l, hx_l, meta_l, cnt_l = _out_call(ypre_l, ysc_l, ycf_l, x_lat, mod_lat, wglu_bf, b_glu[l], wo_bf,
                                              ln1_g[l], ln1_b[l], w_router, b_router, 512)
        if not last:
            ypre_c = _s5c_call(yin_c, hf_c, hb_c, e_op, u_c, s5_d[l])
            ysc_c, ycf_c = _conv_call(bg_c, cv_c, gl_c, w_sc[l], w_dw[l], b_dw[l], ln_cf_g[l], ln_cf_b[l], False)
            x1_c, hx_c, meta_c, cnt_c = _out_call(ypre_c, ysc_c, ycf_c, x_ctx, mod_ctx, wglu_bf, b_glu[l], wo_bf,
                                                  ln1_g[l], ln1_b[l], w_router, b_router, TILE_TOK)
            x_ctx = _moe_sublayer(hx_c, meta_c, cnt_c, x1_c, mod_ctx, wg_bf, wu_bf, wd_bf,
                                  ln2_g[l], ln2_b[l], TILE_TOK)
        x_lat = _moe_sublayer(hx_l, meta_l, cnt_l, x1_l, mod_lat, wg_bf, wu_bf, wd_bf,
                              ln2_g[l], ln2_b[l], 512)
    return x_lat
```

```python
import functools
import math

import jax
import jax.numpy as jnp
from jax import lax
from jax.experimental import pallas as pl
from jax.experimental.pallas import tpu as pltpu

F32 = jnp.float32
BF16 = jnp.bfloat16

D_MODEL = 1024
DEPTH = 2
GRID_W = 64
D_S5 = 512
S5_GROUP = 16
S5_GROUPS = 32
S5_STATE = 64
D_SC = 256
D_CF = 256
CF_WIDTH = 31
CF_HALF = 15
D_IN = 1792
N_GROUPS = 4
EXP_PER_GROUP = 4
N_EXPERTS = 16
D_EXPERT = 256
DN_ALPHA = (2 * DEPTH) ** 0.25
LN_EPS = 1e-5

CHUNK = 16
TILE_CHUNKS = 16
TILE_TOK = CHUNK * TILE_CHUNKS
LANE_BLK = 128
GRP_PER_BLK = LANE_BLK // S5_GROUP
ROUTER_LANES = 128
HX_LANES = D_MODEL + ROUTER_LANES
META_WA, META_WB, META_CLS, META_RANK = 0, 1, 2, 3
N_CLASSES = N_GROUPS * 6
MOE_TM = 256
DMA_UNROLL = 8
VMEM_LIMIT = 56 * 1024 * 1024


def _cparams(sem):
    return pltpu.CompilerParams(dimension_semantics=sem, vmem_limit_bytes=VMEM_LIMIT)


def _split_bf16(a):
    hi = a.astype(BF16)
    lo = (a - hi.astype(F32)).astype(BF16)
    return hi, lo


def _dot(a, b):
    return jnp.dot(a, b, preferred_element_type=F32)


def _dot3(a, b):
    ah, al = _split_bf16(a)
    bh, bl = _split_bf16(b)
    return _dot(ah, bh) + (_dot(al, bh) + _dot(ah, bl))


def _sigmoid(x):
    return 1.0 / (1.0 + jnp.exp(-x))


def _layer_norm(x, g, b):
    mu = jnp.mean(x, axis=-1, keepdims=True)
    xc = x - mu
    var = jnp.mean(xc * xc, axis=-1, keepdims=True)
    return xc * lax.rsqrt(var + LN_EPS) * g + b


def _mod_kernel(c_ref, w_ref, b_ref, o_ref):
    c = c_ref[...]
    s = c * _sigmoid(c)
    o_ref[0] = _dot3(s, w_ref[0]) + b_ref[0]


def _mod_call(c_all, w_mod, b_mod):
    n_layers, d, n_out = w_mod.shape
    tn = 1536
    rows = c_all.shape[0]
    return pl.pallas_call(
        _mod_kernel,
        grid=(n_layers, n_out // tn),
        in_specs=[
            pl.BlockSpec((rows, d), lambda l, j: (0, 0)),
            pl.BlockSpec((1, d, tn), lambda l, j: (l, 0, j)),
            pl.BlockSpec((1, 1, tn), lambda l, j: (l, 0, j)),
        ],
        out_specs=pl.BlockSpec((1, rows, tn), lambda l, j: (l, 0, j)),
        out_shape=jax.ShapeDtypeStruct((n_layers, rows, n_out), F32),
        compiler_params=_cparams(("parallel", "parallel")),
        name="mod",
    )(c_all, w_mod, b_mod.reshape(n_layers, 1, n_out))


def _in_kernel(x_ref, mod_ref, w_ref, u_ref, bg_ref, cv_ref, gl_ref):
    x = x_ref[0]
    sh = mod_ref[0, 0:1, :]
    sc = mod_ref[0, 1:2, :]
    h = (x * (1.0 + sc) + sh).astype(BF16)
    z = _dot(h, w_ref[...])
    u_ref[0] = z[:, 0:512]
    bg_ref[0] = z[:, 512:768]
    cv_ref[0] = z[:, 768:1024] * z[:, 1024:1280]
    gl_ref[0] = z[:, 1280:1536] * _sigmoid(z[:, 1536:1792])


def _in_u_kernel(x_ref, mod_ref, w_ref, u_ref):
    x = x_ref[0]
    sh = mod_ref[0, 0:1, :]
    sc = mod_ref[0, 1:2, :]
    h = (x * (1.0 + sc) + sh).astype(BF16)
    u_ref[0] = _dot(h, w_ref[...])


def _in_call(x, mod, w_in_bf, tm, u_only):
    b, s, d = x.shape
    grid = (b, s // tm)
    row_spec = lambda n: pl.BlockSpec((1, tm, n), lambda i, j: (i, j, 0))
    in_specs = [
        row_spec(d),
        pl.BlockSpec((1, 6, d), lambda i, j: (i, 0, 0)),
    ]
    if u_only:
        in_specs.append(pl.BlockSpec((d, D_S5), lambda i, j: (0, 0)))
        return pl.pallas_call(
            _in_u_kernel, grid=grid, in_specs=in_specs,
            out_specs=row_spec(D_S5),
            out_shape=jax.ShapeDtypeStruct((b, s, D_S5), F32),
            compiler_params=_cparams(("parallel", "parallel")),
            name="in_proj_u",
        )(x, mod, w_in_bf)
    in_specs.append(pl.BlockSpec((d, D_IN), lambda i, j: (0, 0)))
    return pl.pallas_call(
        _in_kernel, grid=grid, in_specs=in_specs,
        out_specs=[row_spec(D_S5), row_spec(D_SC), row_spec(D_SC), row_spec(D_CF)],
        out_shape=[jax.ShapeDtypeStruct((b, s, D_S5), F32),
                   jax.ShapeDtypeStruct((b, s, D_SC), F32),
                   jax.ShapeDtypeStruct((b, s, D_SC), F32),
                   jax.ShapeDtypeStruct((b, s, D_CF), F32)],
        compiler_params=_cparams(("parallel", "parallel")),
        name="in_proj",
    )(x, mod, w_in_bf)


def _conv_tail(t, bdw_ref, lng_ref, lnb_ref):
    t = t + bdw_ref[...]
    t = _layer_norm(t, lng_ref[...], lnb_ref[...])
    return t * _sigmoid(t)


def _conv_grid_kernel(bg_ref, cv_ref, gl_ref, wsc_ref, wdw_ref, bdw_ref, lng_ref, lnb_ref,
                      ysc_ref, ycf_ref, pad_ref, t_ref):
    s = cv_ref.shape[1]
    rows = s // GRID_W
    cv = cv_ref[0]
    col = lax.broadcasted_iota(jnp.int32, (s, D_SC), 0) % GRID_W
    prev = jnp.where(col == 0, 0.0, pltpu.roll(cv, 1, axis=0))
    nxt = jnp.where(col == GRID_W - 1, 0.0, pltpu.roll(cv, s - 1, axis=0))
    conv = prev * wsc_ref[0:1, :] + cv * wsc_ref[1:2, :] + nxt * wsc_ref[2:3, :]
    ysc_ref[0] = (bg_ref[0] * conv).astype(ysc_ref.dtype)

    zero = jnp.zeros((CF_HALF, GRID_W, D_CF), F32)
    pad_ref[0:CF_HALF] = zero
    pad_ref[CF_HALF + rows:CF_HALF + rows + CF_HALF] = zero
    pad_ref[CF_HALF:CF_HALF + rows] = gl_ref[0].reshape(rows, GRID_W, D_CF)

    def body(i, carry):
        w0 = pl.multiple_of(i * 8, 8)
        for half in range(D_CF // 128):
            lanes = slice(half * 128, (half + 1) * 128)
            acc = jnp.zeros((rows, 8, 128), F32)
            for k in range(CF_WIDTH):
                acc = acc + pad_ref[k:k + rows, pl.ds(w0, 8), lanes] * wdw_ref[k:k + 1, lanes]
            t_ref[:, pl.ds(w0, 8), lanes] = acc
        return carry

    lax.fori_loop(0, GRID_W // 8, body, 0)
    t = t_ref[...].reshape(s, D_CF)
    ycf_ref[0] = _conv_tail(t, bdw_ref, lng_ref, lnb_ref).astype(ycf_ref.dtype)


def _conv_seq_kernel(bg_ref, cv_ref, gl_ref, wsc_ref, wdw_ref, bdw_ref, lng_ref, lnb_ref,
                     ysc_ref, ycf_ref, pad_ref):
    s = cv_ref.shape[1]
    cv = cv_ref[0]
    pos = lax.broadcasted_iota(jnp.int32, (s, D_SC), 0)
    prev = jnp.where(pos == 0, 0.0, pltpu.roll(cv, 1, axis=0))
    nxt = jnp.where(pos == s - 1, 0.0, pltpu.roll(cv, s - 1, axis=0))
    conv = prev * wsc_ref[0:1, :] + cv * wsc_ref[1:2, :] + nxt * wsc_ref[2:3, :]
    ysc_ref[0] = (bg_ref[0] * conv).astype(ysc_ref.dtype)

    off = 16
    pad_ref[0:off] = jnp.zeros((off, D_CF), F32)
    pad_ref[off + s:off + s + 16] = jnp.zeros((16, D_CF), F32)
    pad_ref[off:off + s] = gl_ref[0]
    acc = jnp.zeros((s, D_CF), F32)
    for k in range(CF_WIDTH):
        acc = acc + pad_ref[pl.ds(off - CF_HALF + k, s), :] * wdw_ref[k:k + 1, :]
    ycf_ref[0] = _conv_tail(acc, bdw_ref, lng_ref, lnb_ref).astype(ycf_ref.dtype)


def _conv_call(bg, cv, gl, w_sc, w_dw, b_dw, ln_g, ln_b, grid_mode):
    b, s, _ = bg.shape
    row_spec = pl.BlockSpec((1, s, D_SC), lambda i: (i, 0, 0))
    full = lambda shape: pl.BlockSpec(shape, lambda i: (0,) * len(shape))
    if grid_mode:
        rows = s // GRID_W
        kern = _conv_grid_kernel
        scratch = [pltpu.VMEM((rows + 2 * CF_HALF, GRID_W, D_CF), F32),
                   pltpu.VMEM((rows, GRID_W, D_CF), F32)]
        name = "conv_grid"
    else:
        kern = _conv_seq_kernel
        scratch = [pltpu.VMEM((s + 32, D_CF), F32)]
        name = "conv_seq"
    return pl.pallas_call(
        kern, grid=(b,),
        in_specs=[row_spec, row_spec, row_spec, full((3, D_SC)), full((CF_WIDTH, D_CF)),
                  full((1, D_CF)), full((1, D_CF)), full((1, D_CF))],
        out_specs=[row_spec, row_spec],
        out_shape=[jax.ShapeDtypeStruct((b, s, D_SC), BF16), jax.ShapeDtypeStruct((b, s, D_CF), BF16)],
        scratch_shapes=scratch,
        compiler_params=_cparams(("parallel",)),
        name=name,
    )(bg, cv, gl, w_sc, w_dw, b_dw.reshape(1, -1), ln_g.reshape(1, -1), ln_b.reshape(1, -1))


def _split3(a):
    hi = a.astype(BF16)
    r = a - hi.astype(F32)
    mid = r.astype(BF16)
    lo = (r - mid.astype(F32)).astype(BF16)
    return hi, mid, lo


def _select_cols(a, sel):
    hi, mid, lo = _split3(a)
    return _dot(hi, sel) + (_dot(mid, sel) + _dot(lo, sel))


def _select_rows(sel, a):
    hi, mid, lo = _split3(a)
    return _dot(sel, hi) + (_dot(sel, mid) + _dot(sel, lo))


def _cmul(ar, ai, br, bi):
    return ar * br - ai * bi, ar * bi + ai * br


def _s5_prep_kernel(are_r, aim_r, are_c, aim_c, ldt, bre_ref, bim_ref, cre_ref, cim_ref,
                    w1_ref, e_ref, dec_ref):
    t, n, p = CHUNK, S5_GROUP, S5_STATE
    width = t * n
    lane_tok = lax.broadcasted_iota(jnp.int32, (128, width), 1) // n
    pow_id = lax.broadcasted_iota(jnp.int32, (128, width), 0)
    onehot = lambda cond: jnp.where(cond, 1.0, 0.0).astype(BF16)
    sel_fwd = onehot(pow_id == lane_tok)
    sel_rev = onehot(pow_id == t - 1 - lane_tok)
    sel_out = onehot(pow_id == t - lane_tok)
    row_tok = lax.broadcasted_iota(jnp.int32, (width, 32), 0) // n
    row_pow = lax.broadcasted_iota(jnp.int32, (width, 32), 1)
    rsel_rev = onehot(row_pow == t - 1 - row_tok)
    rsel_fwd = onehot(row_pow == row_tok)
    lane = lax.broadcasted_iota(jnp.int32, (n, width), 1)

    jc = jnp.minimum(lax.broadcasted_iota(jnp.int32, (p, 128), 1), t).astype(F32)
    jr = jnp.minimum(lax.broadcasted_iota(jnp.int32, (32, p), 0), t).astype(F32)

    strips = []
    f_parts = []
    e_parts = []
    for d in range(2):
        dt = jnp.exp(ldt[d, 0])
        mag_c = are_c[d, 0] * dt
        th_c = aim_c[d, 0] * dt
        ec = jnp.exp(jc * mag_c)
        qr = ec * jnp.cos(jc * th_c)
        qi = ec * jnp.sin(jc * th_c)
        a_re = are_r[d, 0]
        a_im = aim_r[d, 0]
        er = jnp.exp(jr * (a_re * dt))
        pr = er * jnp.cos(jr * (a_im * dt))
        pi = er * jnp.sin(jr * (a_im * dt))
        nr = pr[1:2] - 1.0
        ni = pi[1:2]
        den = a_re * a_re + a_im * a_im
        fre = (nr * a_re + ni * a_im) / den
        fim = (ni * a_re - nr * a_im) / den
        bt_re = bre_ref[d, 0].T
        bt_im = bim_ref[d, 0].T
        bb_re, bb_im = _cmul(fre, fim, bt_re, bt_im)
        ct_re = jnp.concatenate([cre_ref[d, 0].T] * t, axis=1)
        ct_im = jnp.concatenate([cim_ref[d, 0].T] * t, axis=1)
        sel = sel_fwd if d == 0 else sel_rev
        w_re, w_im = _cmul(ct_re, ct_im, _select_cols(qr, sel), _select_cols(qi, sel))
        strips.append(_dot3(bb_re, w_re) - _dot3(bb_im, w_im))
        if d == 0:
            o_re, o_im = _cmul(w_re, w_im, qr[:, 1:2], qi[:, 1:2])
        else:
            o_re, o_im = _cmul(ct_re, ct_im, _select_cols(qr, sel_out), _select_cols(qi, sel_out))
        e_parts += [o_re, -o_im]
        rsel = rsel_rev if d == 0 else rsel_fwd
        f_re, f_im = _cmul(jnp.concatenate([bb_re] * t, axis=0), jnp.concatenate([bb_im] * t, axis=0),
                           _select_rows(rsel, pr), _select_rows(rsel, pi))
        f_parts += [f_re, f_im]
        dec_ref[0, 2 * d:2 * d + 1, :] = jnp.concatenate([pr[t:t + 1], pr[t:t + 1]], axis=1)
        dec_ref[0, 2 * d + 1:2 * d + 2, :] = jnp.concatenate([-pi[t:t + 1], pi[t:t + 1]], axis=1)

    blocks = []
    for s in range(t):
        fwd = strips[0] if s == 0 else jnp.where(lane >= n * s, pltpu.roll(strips[0], n * s, axis=1), 0.0)
        back = t - 1 - s
        bwd = strips[1] if back == 0 else jnp.where(lane < width - n * back,
                                                     pltpu.roll(strips[1], width - n * back, axis=1), 0.0)
        blocks.append(fwd + bwd)
    m = jnp.concatenate(blocks, axis=0)
    w1_ref[0] = jnp.concatenate([m] + f_parts, axis=1).astype(BF16)
    e_ref[0] = jnp.concatenate(e_parts, axis=0).astype(BF16)


def _s5_operators(a_re, a_im, log_dt, b_re, b_im, c_re, c_im):
    g, p, n, t = S5_GROUPS, S5_STATE, S5_GROUP, CHUNK
    spec = lambda shape: pl.BlockSpec((2, 1) + shape, lambda i: (0, i) + (0,) * len(shape))
    return pl.pallas_call(
        _s5_prep_kernel, grid=(g,),
        in_specs=[spec((1, p)), spec((1, p)), spec((p, 1)), spec((p, 1)), spec((1, 1)),
                  spec((p, n)), spec((p, n)), spec((n, p)), spec((n, p))],
        out_specs=[pl.BlockSpec((1, t * n, 2 * t * n), lambda i: (i, 0, 0)),
                   pl.BlockSpec((1, 4 * p, t * n), lambda i: (i, 0, 0)),
                   pl.BlockSpec((1, 4, 2 * p), lambda i: (i, 0, 0))],
        out_shape=[jax.ShapeDtypeStruct((g, t * n, 2 * t * n), BF16),
                   jax.ShapeDtypeStruct((g, 4 * p, t * n), BF16),
                   jax.ShapeDtypeStruct((g, 4, 2 * p), F32)],
        compiler_params=_cparams(("parallel",)),
        name="s5_prep",
    )(a_re.reshape(2, g, 1, p), a_im.reshape(2, g, 1, p), a_re.reshape(2, g, p, 1),
      a_im.reshape(2, g, p, 1), log_dt.reshape(2, g, 1, 1), b_re, b_im, c_re, c_im)


def _block_transpose8(ps):
    ps = list(ps)
    blk = lax.broadcasted_iota(jnp.int32, ps[0].shape, 1) // S5_GROUP
    for k in range(3):
        step = 1 << k
        shift = S5_GROUP * step
        keep = ((blk >> k) & 1) == 0
        for a in range(8):
            if a & step:
                continue
            pa, pb = ps[a], ps[a + step]
            ps[a] = jnp.where(keep, pa, pltpu.roll(pb, shift, axis=1))
            ps[a + step] = jnp.where(keep, pltpu.roll(pa, 128 - shift, axis=1), pb)
    return ps


def _s5a_kernel(u_ref, w1_ref, yin_ref, gf_ref, gb_ref):
    nb = u_ref.shape[0]
    xs = []
    for s in range(CHUNK):
        parts = [u_ref[b, pl.ds(s, TILE_CHUNKS, stride=CHUNK), :] for b in range(nb)]
        xs.append(jnp.concatenate(parts, axis=0))
    lo = _block_transpose8(xs[:8])
    hi = _block_transpose8(xs[8:])
    for j in range(GRP_PER_BLK):
        og = jnp.concatenate([lo[j], hi[j]], axis=1)
        r = _dot(og.astype(BF16), w1_ref[j])
        yin_ref[j] = r[:, 0:256]
        gf_ref[j] = r[:, 256:384]
        gb_ref[j] = r[:, 384:512]


def _s5a_call(u, w1):
    b, s, _ = u.shape
    nt = s // TILE_TOK
    rows = b * TILE_CHUNKS
    n_blk = D_S5 // LANE_BLK
    out_spec = lambda n: pl.BlockSpec((GRP_PER_BLK, rows, n), lambda j, l: (l, j, 0))
    return pl.pallas_call(
        _s5a_kernel, grid=(nt, n_blk),
        in_specs=[pl.BlockSpec((b, TILE_TOK, LANE_BLK), lambda j, l: (0, j, l)),
                  pl.BlockSpec((GRP_PER_BLK, 256, 512), lambda j, l: (l, 0, 0))],
        out_specs=[out_spec(256), out_spec(128), out_spec(128)],
        out_shape=[jax.ShapeDtypeStruct((S5_GROUPS, nt * rows, 256), F32),
                   jax.ShapeDtypeStruct((S5_GROUPS, nt * rows, 128), F32),
                   jax.ShapeDtypeStruct((S5_GROUPS, nt * rows, 128), F32)],
        compiler_params=_cparams(("parallel", "parallel")),
        name="s5_chunk_in",
    )(u, w1)


def _s5b_kernel(nb, a_ref, gfc_ref, gfl_ref, gbc_ref, gbl_ref, hfc_ref, hfl_ref, hbc_ref, hbl_ref):
    gb = a_ref.shape[0]
    rows = nb * TILE_CHUNKS
    n_lat = gfl_ref.shape[1] // rows
    a1f = [jnp.broadcast_to(a_ref[g, 0:1, :], (nb, 128)) for g in range(gb)]
    a2f = [jnp.broadcast_to(a_ref[g, 1:2, :], (nb, 128)) for g in range(gb)]
    a1b = [jnp.broadcast_to(a_ref[g, 2:3, :], (nb, 128)) for g in range(gb)]
    a2b = [jnp.broadcast_to(a_ref[g, 3:4, :], (nb, 128)) for g in range(gb)]

    def step(h, a1, a2, g_ref, h_ref, g, row):
        h_ref[g, row, :] = h
        return a1 * h + a2 * pltpu.roll(h, 64, axis=1) + g_ref[g, row, :]

    hf = [jnp.zeros((nb, 128), F32) for _ in range(gb)]
    hb = [jnp.zeros((nb, 128), F32) for _ in range(gb)]
    for ci in range(TILE_CHUNKS):
        rf = pl.ds(ci, nb, stride=TILE_CHUNKS)
        rb = pl.ds(TILE_CHUNKS - 1 - ci, nb, stride=TILE_CHUNKS)
        for g in range(gb):
            hf[g] = step(hf[g], a1f[g], a2f[g], gfc_ref, hfc_ref, g, rf)
            hb[g] = step(hb[g], a1b[g], a2b[g], gbc_ref, hbc_ref, g, rb)

    def body(j, carry):
        hf, hb = carry
        hf = list(hf)
        hb = list(hb)
        base_f = j * rows
        base_b = (n_lat - 1 - j) * rows
        for ci in range(TILE_CHUNKS):
            rf = pl.ds(base_f + ci, nb, stride=TILE_CHUNKS)
            rb = pl.ds(base_b + (TILE_CHUNKS - 1 - ci), nb, stride=TILE_CHUNKS)
            for g in range(gb):
                hf[g] = step(hf[g], a1f[g], a2f[g], gfl_ref, hfl_ref, g, rf)
                hb[g] = step(hb[g], a1b[g], a2b[g], gbl_ref, hbl_ref, g, rb)
        return tuple(hf), tuple(hb)

    lax.fori_loop(0, n_lat, body, (tuple(hf), tuple(hb)))


def _s5b_call(decay, gf_c, gf_l, gb_c, gb_l, nb):
    gblk = 4
    spec = lambda a: pl.BlockSpec((gblk, a.shape[1], 128), lambda i: (i, 0, 0))
    sds = lambda a: jax.ShapeDtypeStruct(a.shape, F32)
    return pl.pallas_call(
        functools.partial(_s5b_kernel, nb), grid=(S5_GROUPS // gblk,),
        in_specs=[pl.BlockSpec((gblk, 4, 128), lambda i: (i, 0, 0)),
                  spec(gf_c), spec(gf_l), spec(gb_c), spec(gb_l)],
        out_specs=[spec(gf_c), spec(gf_l), spec(gb_c), spec(gb_l)],
        out_shape=[sds(gf_c), sds(gf_l), sds(gb_c), sds(gb_l)],
        compiler_params=_cparams(("parallel",)),
        name="s5_state_scan",
    )(decay, gf_c, gf_l, gb_c, gb_l)


def _s5c_kernel(yin_ref, hf_ref, hb_ref, e_ref, u_ref, d_ref, y_ref):
    nb = u_ref.shape[0]
    ys = []
    for j in range(GRP_PER_BLK):
        h = jnp.concatenate([hf_ref[j], hb_ref[j]], axis=1).astype(BF16)
        ys.append(yin_ref[j] + _dot(h, e_ref[j]))
    at = (_block_transpose8([y[:, :128] for y in ys])
          + _block_transpose8([y[:, 128:] for y in ys]))
    d = d_ref[...]
    for t in range(CHUNK):
        for b in range(nb):
            rows = pl.ds(t, TILE_CHUNKS, stride=CHUNK)
            y_ref[b, rows, :] = at[t][b * TILE_CHUNKS:(b + 1) * TILE_CHUNKS] + d * u_ref[b, rows, :]


def _s5c_call(yin, hf, hb, e, u, d_skip):
    b, s, _ = u.shape
    nt = s // TILE_TOK
    rows = b * TILE_CHUNKS
    n_blk = D_S5 // LANE_BLK
    gspec = lambda n: pl.BlockSpec((GRP_PER_BLK, rows, n), lambda j, l: (l, j, 0))
    tok_spec = pl.BlockSpec((b, TILE_TOK, LANE_BLK), lambda j, l: (0, j, l))
    return pl.pallas_call(
        _s5c_kernel, grid=(nt, n_blk),
        in_specs=[gspec(256), gspec(128), gspec(128),
                  pl.BlockSpec((GRP_PER_BLK, 256, 256), lambda j, l: (l, 0, 0)),
                  tok_spec,
                  pl.BlockSpec((1, LANE_BLK), lambda j, l: (0, l))],
        out_specs=tok_spec,
        out_shape=jax.ShapeDtypeStruct((b, s, D_S5), F32),
        compiler_params=_cparams(("parallel", "parallel")),
        name="s5_chunk_out",
    )(yin, hf, hb, e, u, d_skip.reshape(1, D_S5))


def _gelu_tanh(x):
    return 0.5 * x * (1.0 + jnp.tanh(math.sqrt(2.0 / math.pi) * (x + 0.044715 * (x * x * x))))


def _route(logits):
    lane = lax.broadcasted_iota(jnp.int32, logits.shape, 1).astype(F32)
    neg = jnp.float32(-1e30)
    big = jnp.float32(1e9)
    gl = jnp.where(lane < N_GROUPS, logits, neg)
    gmax = jnp.max(gl, axis=1, keepdims=True)
    gidx = jnp.min(jnp.where(gl == gmax, lane, big), axis=1, keepdims=True)
    gsum = jnp.sum(jnp.exp(gl - gmax), axis=1, keepdims=True)
    gw = 1.0 / gsum
    lo = N_GROUPS + EXP_PER_GROUP * gidx
    el = jnp.where((lane >= lo) & (lane < lo + EXP_PER_GROUP), logits, neg)
    v1 = jnp.max(el, axis=1, keepdims=True)
    i1 = jnp.min(jnp.where(el == v1, lane, big), axis=1, keepdims=True)
    el2 = jnp.where(lane == i1, neg, el)
    v2 = jnp.max(el2, axis=1, keepdims=True)
    i2 = jnp.min(jnp.where(el2 == v2, lane, big), axis=1, keepdims=True)
    ex = jnp.exp(v2 - v1)
    p1 = 1.0 / (1.0 + ex)
    p2 = ex * p1
    e1 = i1 - lo
    e2 = i2 - lo
    first = e1 < e2
    ea = jnp.where(first, e1, e2)
    eb = jnp.where(first, e2, e1)
    wa = gw * jnp.where(first, p1, p2)
    wb = gw * jnp.where(first, p2, p1)
    pair = ea * (7.0 - ea) * 0.5 + (eb - ea - 1.0)
    return wa, wb, 6.0 * gidx + pair


def _out_kernel(ypre_ref, ysc_ref, ycf_ref, x_ref, mod_ref, wglu_ref, bglu_ref, wo_ref,
                lng_ref, lnb_ref, wr_ref, br_ref, x1_ref, hx_ref, meta_ref, counts_ref, cnt_ref):
    @pl.when((pl.program_id(0) == 0) & (pl.program_id(1) == 0))
    def _():
        cnt_ref[...] = jnp.zeros_like(cnt_ref)

    t = _gelu_tanh(ypre_ref[0])
    gate = _sigmoid(_dot(t.astype(BF16), wglu_ref[...]) + bglu_ref[...])
    ys5 = (t * gate).astype(BF16)
    y = (_dot(ys5, wo_ref[0:D_S5, :]) + _dot(ysc_ref[0], wo_ref[D_S5:D_S5 + D_SC, :])
         + _dot(ycf_ref[0], wo_ref[D_S5 + D_SC:D_MODEL, :]))
    g1 = mod_ref[0, 2:3, :]
    x1 = _layer_norm(DN_ALPHA * x_ref[0] + g1 * y, lng_ref[...], lnb_ref[...])
    x1_ref[0] = x1
    h2 = x1 * (1.0 + mod_ref[0, 4:5, :]) + mod_ref[0, 3:4, :]
    wa, wb, cls = _route(_dot3(h2, wr_ref[...]) + br_ref[...])

    tm = h2.shape[0]
    lane = lax.broadcasted_iota(jnp.int32, (tm, ROUTER_LANES), 1).astype(F32)
    onehot = jnp.where(lane == cls, 1.0, 0.0)
    row_i = lax.broadcasted_iota(jnp.int32, (tm, tm), 0)
    col_i = lax.broadcasted_iota(jnp.int32, (tm, tm), 1)
    earlier = jnp.where(col_i < row_i, 1.0, 0.0).astype(BF16)
    before = _dot(earlier, onehot.astype(BF16)) + cnt_ref[...]
    rank = jnp.sum(before * onehot, axis=1, keepdims=True)
    cnt_ref[...] += jnp.sum(onehot, axis=0, keepdims=True)
    counts_ref[...] = cnt_ref[...]

    meta = (jnp.where(lane == META_WA, wa, 0.0) + jnp.where(lane == META_WB, wb, 0.0)
            + jnp.where(lane == META_CLS, cls, 0.0) + jnp.where(lane == META_RANK, rank, 0.0))
    meta_ref[0] = meta
    hx_ref[0, :, 0:D_MODEL] = h2
    hx_ref[0, :, D_MODEL:HX_LANES] = meta


def _out_call(ypre, ysc, ycf, x, mod, wglu_bf, b_glu, wo_bf, ln_g, ln_b, w_router, b_router, tm):
    b, s, d = x.shape
    row_spec = lambda n: pl.BlockSpec((1, tm, n), lambda i, j: (i, j, 0))
    full = lambda shape: pl.BlockSpec(shape, lambda i, j: (0,) * len(shape))
    return pl.pallas_call(
        _out_kernel, grid=(b, s // tm),
        in_specs=[row_spec(D_S5), row_spec(D_SC), row_spec(D_CF), row_spec(d),
                  pl.BlockSpec((1, 6, d), lambda i, j: (i, 0, 0)),
                  full((D_S5, D_S5)), full((1, D_S5)), full((d, d)),
                  full((1, d)), full((1, d)), full((d, ROUTER_LANES)), full((1, ROUTER_LANES))],
        out_specs=[row_spec(d), row_spec(HX_LANES), row_spec(ROUTER_LANES), full((1, ROUTER_LANES))],
        out_shape=[jax.ShapeDtypeStruct((b, s, d), F32), jax.ShapeDtypeStruct((b, s, HX_LANES), F32),
                   jax.ShapeDtypeStruct((b, s, ROUTER_LANES), F32),
                   jax.ShapeDtypeStruct((1, ROUTER_LANES), F32)],
        scratch_shapes=[pltpu.VMEM((1, ROUTER_LANES), F32)],
        compiler_params=_cparams(("arbitrary", "arbitrary")),
        name="out_proj",
    )(ypre, ysc, ycf, x, mod, wglu_bf, b_glu.reshape(1, -1), wo_bf, ln_g.reshape(1, -1),
      ln_b.reshape(1, -1), w_router, b_router)


def _moe_plan(meta, counts, n_tok):
    cls = meta[..., META_CLS].reshape(-1).astype(jnp.int32)
    rank = meta[..., META_RANK].reshape(-1).astype(jnp.int32)
    cnt = counts[0, :N_CLASSES].astype(jnp.int32)
    n_tiles = (cnt + (MOE_TM - 1)) // MOE_TM
    ends = jnp.cumsum(n_tiles)
    starts = ends - n_tiles
    slot = starts[cls] * MOE_TM + rank
    t_max = n_tok // MOE_TM + N_CLASSES
    n_used = ends[N_CLASSES - 1]
    tile = jnp.minimum(jnp.arange(t_max, dtype=jnp.int32), n_used - 1)
    tile_cls = jnp.sum((tile[:, None] >= ends[None, :]).astype(jnp.int32), axis=1)
    group = tile_cls // 6
    pair = tile_cls % 6
    first = jnp.array([0, 0, 0, 1, 1, 2], jnp.int32)[pair] + EXP_PER_GROUP * group
    second = jnp.array([1, 2, 3, 2, 3, 3], jnp.int32)[pair] + EXP_PER_GROUP * group
    return slot, tile, first, second, n_used.reshape(1)


def _dispatch_kernel(slot_ref, hx_ref, xs_init_ref, xs_ref, sem):
    del xs_init_ref
    tm = hx_ref.shape[1]
    base = (pl.program_id(0) * pl.num_programs(1) + pl.program_id(1)) * tm

    def body(r, carry):
        dst = slot_ref[base + r]
        pltpu.make_async_copy(hx_ref.at[0, pl.ds(r, 1), :], xs_ref.at[pl.ds(dst, 1), :], sem).start()
        return carry

    lax.fori_loop(0, tm, body, 0, unroll=DMA_UNROLL)
    pltpu.make_async_copy(hx_ref.at[0], xs_ref.at[pl.ds(0, tm), :], sem).wait()


def _dispatch_call(slot, hx, n_rows, tm):
    b, s, w = hx.shape
    xs_init = jnp.zeros((n_rows, w), F32)
    grid_spec = pltpu.PrefetchScalarGridSpec(
        num_scalar_prefetch=1, grid=(b, s // tm),
        in_specs=[pl.BlockSpec((1, tm, w), lambda i, j, slot: (i, j, 0)),
                  pl.BlockSpec(memory_space=pl.ANY)],
        out_specs=pl.BlockSpec(memory_space=pl.ANY),
        scratch_shapes=[pltpu.SemaphoreType.DMA(())])
    return pl.pallas_call(
        _dispatch_kernel, grid_spec=grid_spec,
        out_shape=jax.ShapeDtypeStruct((n_rows, w), F32),
        input_output_aliases={2: 0},
        compiler_params=_cparams(("arbitrary", "arbitrary")),
        name="moe_dispatch",
    )(slot, hx, xs_init)


def _moe_kernel(tile_ref, first_ref, second_ref, nused_ref, xs_ref, wga_ref, wgb_ref, wua_ref, wub_ref,
                wda_ref, wdb_ref, ys_ref):
    del tile_ref, first_ref, second_ref

    @pl.when(pl.program_id(0) < nused_ref[0])
    def _():
        x = xs_ref[...]
        xb = x[:, 0:D_MODEL].astype(BF16)

        def expert(wg_ref, wu_ref, wd_ref, w):
            gate = _dot(xb, wg_ref[0])
            up = _dot(xb, wu_ref[0])
            act = gate * _sigmoid(gate) * up * w
            return _dot(act.astype(BF16), wd_ref[0])

        wa = x[:, D_MODEL + META_WA:D_MODEL + META_WA + 1]
        wb = x[:, D_MODEL + META_WB:D_MODEL + META_WB + 1]
        ys_ref[...] = expert(wga_ref, wua_ref, wda_ref, wa) + expert(wgb_ref, wub_ref, wdb_ref, wb)

    @pl.when(pl.program_id(0) >= nused_ref[0])
    def _():
        ys_ref[...] = jnp.zeros_like(ys_ref)


def _moe_call(tile, first, second, n_used, xs, wg_bf, wu_bf, wd_bf):
    n_rows, w = xs.shape
    d = D_MODEL
    t_max = tile.shape[0]
    up_spec = lambda sel: pl.BlockSpec((1, d, D_EXPERT), lambda t, tl, fi, se, nu: ((fi, se)[sel][t], 0, 0))
    down_spec = lambda sel: pl.BlockSpec((1, D_EXPERT, d), lambda t, tl, fi, se, nu: ((fi, se)[sel][t], 0, 0))
    grid_spec = pltpu.PrefetchScalarGridSpec(
        num_scalar_prefetch=4, grid=(t_max,),
        in_specs=[pl.BlockSpec((MOE_TM, w), lambda t, tl, fi, se, nu: (tl[t], 0)),
                  up_spec(0), up_spec(1), up_spec(0), up_spec(1), down_spec(0), down_spec(1)],
        out_specs=pl.BlockSpec((MOE_TM, d), lambda t, tl, fi, se, nu: (t, 0)))
    return pl.pallas_call(
        _moe_kernel, grid_spec=grid_spec,
        out_shape=jax.ShapeDtypeStruct((n_rows, d), F32),
        compiler_params=_cparams(("arbitrary",)),
        name="moe_experts",
    )(tile, first, second, n_used, xs, wg_bf, wg_bf, wu_bf, wu_bf, wd_bf, wd_bf)


def _combine_kernel(slot_ref, x1_ref, mod_ref, lng_ref, lnb_ref, ys_ref, o_ref, f_ref, sem):
    tm = x1_ref.shape[1]
    base = (pl.program_id(0) * pl.num_programs(1) + pl.program_id(1)) * tm

    def body(r, carry):
        src = slot_ref[base + r]
        pltpu.make_async_copy(ys_ref.at[pl.ds(src, 1), :], f_ref.at[pl.ds(r, 1), :], sem).start()
        return carry

    lax.fori_loop(0, tm, body, 0, unroll=DMA_UNROLL)
    pltpu.make_async_copy(ys_ref.at[pl.ds(0, tm), :], f_ref, sem).wait()
    g2 = mod_ref[0, 5:6, :]
    o_ref[0] = _layer_norm(DN_ALPHA * x1_ref[0] + g2 * f_ref[...], lng_ref[...], lnb_ref[...])


def _combine_call(slot, x1, mod, ln_g, ln_b, ys, tm):
    b, s, d = x1.shape
    grid_spec = pltpu.PrefetchScalarGridSpec(
        num_scalar_prefetch=1, grid=(b, s // tm),
        in_specs=[pl.BlockSpec((1, tm, d), lambda i, j, slot: (i, j, 0)),
                  pl.BlockSpec((1, 6, d), lambda i, j, slot: (i, 0, 0)),
                  pl.BlockSpec((1, d), lambda i, j, slot: (0, 0)),
                  pl.BlockSpec((1, d), lambda i, j, slot: (0, 0)),
                  pl.BlockSpec(memory_space=pl.ANY)],
        out_specs=pl.BlockSpec((1, tm, d), lambda i, j, slot: (i, j, 0)),
        scratch_shapes=[pltpu.VMEM((tm, d), F32), pltpu.SemaphoreType.DMA(())])
    return pl.pallas_call(
        _combine_kernel, grid_spec=grid_spec,
        out_shape=jax.ShapeDtypeStruct((b, s, d), F32),
        compiler_params=_cparams(("arbitrary", "arbitrary")),
        name="moe_combine",
    )(slot, x1, mod, ln_g.reshape(1, -1), ln_b.reshape(1, -1), ys)


def _moe_sublayer(hx, meta, counts, x1, mod, wg_bf, wu_bf, wd_bf, ln_g, ln_b, tm):
    b, s, _ = x1.shape
    n_tok = b * s
    slot, tile, first, second, n_used = _moe_plan(meta, counts, n_tok)
    n_rows = n_tok + N_CLASSES * MOE_TM
    xs = _dispatch_call(slot, hx, n_rows, tm)
    ys = _moe_call(tile, first, second, n_used, xs, wg_bf, wu_bf, wd_bf)
    return _combine_call(slot, x1, mod, ln_g, ln_b, ys, tm)


def kernel(x, c, ctx, c_ctx, w_mod, b_mod, w_in, s5_a_re, s5_a_im, s5_log_dt, s5_b_re, s5_b_im, s5_c_re, s5_c_im, s5_d, w_glu, b_glu, w_sc, w_dw, b_dw, ln_cf_g, ln_cf_b, w_o, ln1_g, ln1_b, w_rg, b_rg, w_rexp, b_rexp, w_gate, w_up, w_down, ln2_g, ln2_b):
    nb, seq, d = x.shape
    n_ctx = ctx.shape[1]
    n_layers = w_mod.shape[0]
    assert seq % TILE_TOK == 0 and n_ctx % TILE_TOK == 0 and seq % GRID_W == 0

    mod_rows = 16
    assert nb + 1 <= mod_rows
    c_all = jnp.concatenate([c, c_ctx[None, :], jnp.zeros((mod_rows - nb - 1, d), F32)], axis=0)
    mod_all = _mod_call(c_all, w_mod, b_mod)

    pad_r = ROUTER_LANES - N_GROUPS - N_EXPERTS
    x_lat, x_ctx = x, ctx
    for l in range(n_layers):
        last = l == n_layers - 1
        mod_lat = mod_all[l, :nb].reshape(nb, 6, d)
        mod_ctx = jnp.broadcast_to(mod_all[l, nb].reshape(1, 6, d), (nb, 6, d))
        w_in_bf = w_in[l].astype(BF16)
        wglu_bf = w_glu[l].astype(BF16)
        wo_bf = w_o[l].astype(BF16)
        wg_bf = w_gate[l].astype(BF16)
        wu_bf = w_up[l].astype(BF16)
        wd_bf = w_down[l].astype(BF16)
        w_router = jnp.concatenate([w_rg[l], w_rexp[l], jnp.zeros((d, pad_r), F32)], axis=1)
        b_router = jnp.concatenate([b_rg[l], b_rexp[l], jnp.zeros((pad_r,), F32)]).reshape(1, -1)
        w1, e_op, decay = _s5_operators(s5_a_re[l], s5_a_im[l], s5_log_dt[l], s5_b_re[l], s5_b_im[l],
                                        s5_c_re[l], s5_c_im[l])

        u_l, bg_l, cv_l, gl_l = _in_call(x_lat, mod_lat, w_in_bf, 512, False)
        if last:
            u_c = _in_call(x_ctx, mod_ctx, w_in_bf[:, :D_S5], TILE_TOK, True)
        else:
            u_c, bg_c, cv_c, gl_c = _in_call(x_ctx, mod_ctx, w_in_bf, TILE_TOK, False)

        yin_l, gf_l, gb_l = _s5a_call(u_l, w1)
        yin_c, gf_c, gb_c = _s5a_call(u_c, w1)
        hf_c, hf_l, hb_c, hb_l = _s5b_call(decay, gf_c, gf_l, gb_c, gb_l, nb)
        ypre_l = _s5c_call(yin_l, hf_l, hb_l, e_op, u_l, s5_d[l])

        ysc_l, ycf_l = _conv_call(bg_l, cv_l, gl_l, w_sc[l], w_dw[l], b_dw[l], ln_cf_g[l], ln_cf_b[l], True)
        x1_l, hx_l, meta_l, cnt_l = _out_call(ypre_l, ysc_l, ycf_l, x_lat, mod_lat, wglu_bf, b_glu[l], wo_bf,
                                              ln1_g[l], ln1_b[l], w_router, b_router, 512)
        if not last:
            ypre_c = _s5c_call(yin_c, hf_c, hb_c, e_op, u_c, s5_d[l])
            ysc_c, ycf_c = _conv_call(bg_c, cv_c, gl_c, w_sc[l], w_dw[l], b_dw[l], ln_cf_g[l], ln_cf_b[l], False)
            x1_c, hx_c, meta_c, cnt_c = _out_call(ypre_c, ysc_c, ycf_c, x_ctx, mod_ctx, wglu_bf, b_glu[l], wo_bf,
                                                  ln1_g[l], ln1_b[l], w_router, b_router, TILE_TOK)
            x_ctx = _moe_sublayer(hx_c, meta_c, cnt_c, x1_c, mod_ctx, wg_bf, wu_bf, wd_bf,
                                  ln2_g[l], ln2_b[l], TILE_TOK)
        x_lat = _moe_sublayer(hx_l, meta_l, cnt_l, x1_l, mod_lat, wg_bf, wu_bf, wd_bf,
                              ln2_g[l], ln2_b[l], 512)
    return x_lat
```

```python
import functools
import math

import jax
import jax.numpy as jnp
from jax import lax
from jax.experimental import pallas as pl
from jax.experimental.pallas import tpu as pltpu

F32 = jnp.float32
BF16 = jnp.bfloat16

D_MODEL = 1024
DEPTH = 2
GRID_W = 64
D_S5 = 512
S5_GROUP = 16
S5_GROUPS = 32
S5_STATE = 64
D_SC = 256
D_CF = 256
CF_WIDTH = 31
CF_HALF = 15
D_IN = 1792
N_GROUPS = 4
EXP_PER_GROUP = 4
N_EXPERTS = 16
D_EXPERT = 256
DN_ALPHA = (2 * DEPTH) ** 0.25
LN_EPS = 1e-5

CHUNK = 16
TILE_CHUNKS = 16
TILE_TOK = CHUNK * TILE_CHUNKS
LANE_BLK = 128
GRP_PER_BLK = LANE_BLK // S5_GROUP
ROUTER_LANES = 128
HX_LANES = D_MODEL + ROUTER_LANES
META_WA, META_WB, META_CLS, META_RANK = 0, 1, 2, 3
N_CLASSES = N_GROUPS * 6
MOE_TM = 256
SUBLANES = 8
VMEM_LIMIT = 56 * 1024 * 1024


def _cparams(sem):
    return pltpu.CompilerParams(dimension_semantics=sem, vmem_limit_bytes=VMEM_LIMIT)


def _split_bf16(a):
    hi = a.astype(BF16)
    lo = (a - hi.astype(F32)).astype(BF16)
    return hi, lo


def _dot(a, b):
    return jnp.dot(a, b, preferred_element_type=F32)


def _dot3(a, b):
    ah, al = _split_bf16(a)
    bh, bl = _split_bf16(b)
    return _dot(ah, bh) + (_dot(al, bh) + _dot(ah, bl))


def _sigmoid(x):
    return 1.0 / (1.0 + jnp.exp(-x))


def _layer_norm(x, g, b):
    mu = jnp.mean(x, axis=-1, keepdims=True)
    xc = x - mu
    var = jnp.mean(xc * xc, axis=-1, keepdims=True)
    return xc * lax.rsqrt(var + LN_EPS) * g + b


def _mod_kernel(c_ref, w_ref, b_ref, o_ref):
    c = c_ref[...]
    s = c * _sigmoid(c)
    o_ref[0] = _dot3(s, w_ref[0]) + b_ref[0]


def _mod_call(c_all, w_mod, b_mod):
    n_layers, d, n_out = w_mod.shape
    tn = 1536
    rows = c_all.shape[0]
    return pl.pallas_call(
        _mod_kernel,
        grid=(n_layers, n_out // tn),
        in_specs=[
            pl.BlockSpec((rows, d), lambda l, j: (0, 0)),
            pl.BlockSpec((1, d, tn), lambda l, j: (l, 0, j)),
            pl.BlockSpec((1, 1, tn), lambda l, j: (l, 0, j)),
        ],
        out_specs=pl.BlockSpec((1, rows, tn), lambda l, j: (l, 0, j)),
        out_shape=jax.ShapeDtypeStruct((n_layers, rows, n_out), F32),
        compiler_params=_cparams(("parallel", "parallel")),
        name="mod",
    )(c_all, w_mod, b_mod.reshape(n_layers, 1, n_out))


def _in_kernel(x_ref, mod_ref, w_ref, u_ref, bg_ref, cv_ref, gl_ref):
    x = x_ref[0]
    sh = mod_ref[0, 0:1, :]
    sc = mod_ref[0, 1:2, :]
    h = (x * (1.0 + sc) + sh).astype(BF16)
    z = _dot(h, w_ref[...])
    _store_lane_blocks(u_ref, z[:, 0:512])
    bg_ref[0] = z[:, 512:768]
    cv_ref[0] = z[:, 768:1024] * z[:, 1024:1280]
    gl_ref[0] = z[:, 1280:1536] * _sigmoid(z[:, 1536:1792])


def _in_u_kernel(x_ref, mod_ref, w_ref, u_ref):
    x = x_ref[0]
    sh = mod_ref[0, 0:1, :]
    sc = mod_ref[0, 1:2, :]
    h = (x * (1.0 + sc) + sh).astype(BF16)
    _store_lane_blocks(u_ref, _dot(h, w_ref[...]))


def _store_lane_blocks(ref, val):
    for blk in range(ref.shape[0]):
        ref[blk, 0] = val[:, blk * LANE_BLK:(blk + 1) * LANE_BLK]


def _lane_block_spec(tm):
    return pl.BlockSpec((D_S5 // LANE_BLK, 1, tm, LANE_BLK), lambda i, j: (0, i, j, 0))


def _lane_block_shape(b, s):
    return jax.ShapeDtypeStruct((D_S5 // LANE_BLK, b, s, LANE_BLK), F32)


def _in_call(x, mod, w_in_bf, tm, u_only):
    b, s, d = x.shape
    grid = (b, s // tm)
    row_spec = lambda n: pl.BlockSpec((1, tm, n), lambda i, j: (i, j, 0))
    in_specs = [
        row_spec(d),
        pl.BlockSpec((1, 6, d), lambda i, j: (i, 0, 0)),
    ]
    if u_only:
        in_specs.append(pl.BlockSpec((d, D_S5), lambda i, j: (0, 0)))
        return pl.pallas_call(
            _in_u_kernel, grid=grid, in_specs=in_specs,
            out_specs=_lane_block_spec(tm),
            out_shape=_lane_block_shape(b, s),
            compiler_params=_cparams(("parallel", "parallel")),
            name="in_proj_u",
        )(x, mod, w_in_bf)
    in_specs.append(pl.BlockSpec((d, D_IN), lambda i, j: (0, 0)))
    return pl.pallas_call(
        _in_kernel, grid=grid, in_specs=in_specs,
        out_specs=[_lane_block_spec(tm), row_spec(D_SC), row_spec(D_SC), row_spec(D_CF)],
        out_shape=[_lane_block_shape(b, s),
                   jax.ShapeDtypeStruct((b, s, D_SC), F32),
                   jax.ShapeDtypeStruct((b, s, D_SC), F32),
                   jax.ShapeDtypeStruct((b, s, D_CF), F32)],
        compiler_params=_cparams(("parallel", "parallel")),
        name="in_proj",
    )(x, mod, w_in_bf)


def _conv_tail(t, bdw_ref, lng_ref, lnb_ref):
    t = t + bdw_ref[...]
    t = _layer_norm(t, lng_ref[...], lnb_ref[...])
    return t * _sigmoid(t)


def _conv_grid_kernel(bg_ref, cv_ref, gl_ref, wsc_ref, wdw_ref, bdw_ref, lng_ref, lnb_ref,
                      ysc_ref, ycf_ref, pad_ref, t_ref):
    s = cv_ref.shape[1]
    rows = s // GRID_W
    cv = cv_ref[0]
    col = lax.broadcasted_iota(jnp.int32, (s, D_SC), 0) % GRID_W
    prev = jnp.where(col == 0, 0.0, pltpu.roll(cv, 1, axis=0))
    nxt = jnp.where(col == GRID_W - 1, 0.0, pltpu.roll(cv, s - 1, axis=0))
    conv = prev * wsc_ref[0:1, :] + cv * wsc_ref[1:2, :] + nxt * wsc_ref[2:3, :]
    ysc_ref[0] = (bg_ref[0] * conv).astype(ysc_ref.dtype)

    zero = jnp.zeros((CF_HALF, GRID_W, D_CF), F32)
    pad_ref[0:CF_HALF] = zero
    pad_ref[CF_HALF + rows:CF_HALF + rows + CF_HALF] = zero
    pad_ref[CF_HALF:CF_HALF + rows] = gl_ref[0].reshape(rows, GRID_W, D_CF)

    def body(i, carry):
        w0 = pl.multiple_of(i * 8, 8)
        for half in range(D_CF // 128):
            lanes = slice(half * 128, (half + 1) * 128)
            acc = jnp.zeros((rows, 8, 128), F32)
            for k in range(CF_WIDTH):
                acc = acc + pad_ref[k:k + rows, pl.ds(w0, 8), lanes] * wdw_ref[k:k + 1, lanes]
            t_ref[:, pl.ds(w0, 8), lanes] = acc
        return carry

    lax.fori_loop(0, GRID_W // 8, body, 0)
    t = t_ref[...].reshape(s, D_CF)
    ycf_ref[0] = _conv_tail(t, bdw_ref, lng_ref, lnb_ref).astype(ycf_ref.dtype)


def _conv_seq_kernel(bg_ref, cv_ref, gl_ref, wsc_ref, wdw_ref, bdw_ref, lng_ref, lnb_ref,
                     ysc_ref, ycf_ref, pad_ref):
    s = cv_ref.shape[1]
    cv = cv_ref[0]
    pos = lax.broadcasted_iota(jnp.int32, (s, D_SC), 0)
    prev = jnp.where(pos == 0, 0.0, pltpu.roll(cv, 1, axis=0))
    nxt = jnp.where(pos == s - 1, 0.0, pltpu.roll(cv, s - 1, axis=0))
    conv = prev * wsc_ref[0:1, :] + cv * wsc_ref[1:2, :] + nxt * wsc_ref[2:3, :]
    ysc_ref[0] = (bg_ref[0] * conv).astype(ysc_ref.dtype)

    off = 16
    pad_ref[0:off] = jnp.zeros((off, D_CF), F32)
    pad_ref[off + s:off + s + 16] = jnp.zeros((16, D_CF), F32)
    pad_ref[off:off + s] = gl_ref[0]
    acc = jnp.zeros((s, D_CF), F32)
    for k in range(CF_WIDTH):
        acc = acc + pad_ref[pl.ds(off - CF_HALF + k, s), :] * wdw_ref[k:k + 1, :]
    ycf_ref[0] = _conv_tail(acc, bdw_ref, lng_ref, lnb_ref).astype(ycf_ref.dtype)


def _conv_call(bg, cv, gl, w_sc, w_dw, b_dw, ln_g, ln_b, grid_mode):
    b, s, _ = bg.shape
    row_spec = pl.BlockSpec((1, s, D_SC), lambda i: (i, 0, 0))
    full = lambda shape: pl.BlockSpec(shape, lambda i: (0,) * len(shape))
    if grid_mode:
        rows = s // GRID_W
        kern = _conv_grid_kernel
        scratch = [pltpu.VMEM((rows + 2 * CF_HALF, GRID_W, D_CF), F32),
                   pltpu.VMEM((rows, GRID_W, D_CF), F32)]
        name = "conv_grid"
    else:
        kern = _conv_seq_kernel
        scratch = [pltpu.VMEM((s + 32, D_CF), F32)]
        name = "conv_seq"
    return pl.pallas_call(
        kern, grid=(b,),
        in_specs=[row_spec, row_spec, row_spec, full((3, D_SC)), full((CF_WIDTH, D_CF)),
                  full((1, D_CF)), full((1, D_CF)), full((1, D_CF))],
        out_specs=[row_spec, row_spec],
        out_shape=[jax.ShapeDtypeStruct((b, s, D_SC), BF16), jax.ShapeDtypeStruct((b, s, D_CF), BF16)],
        scratch_shapes=scratch,
        compiler_params=_cparams(("parallel",)),
        name=name,
    )(bg, cv, gl, w_sc, w_dw, b_dw.reshape(1, -1), ln_g.reshape(1, -1), ln_b.reshape(1, -1))


def _split3(a):
    hi = a.astype(BF16)
    r = a - hi.astype(F32)
    mid = r.astype(BF16)
    lo = (r - mid.astype(F32)).astype(BF16)
    return hi, mid, lo


def _select_cols(a, sel):
    hi, mid, lo = _split3(a)
    return _dot(hi, sel) + (_dot(mid, sel) + _dot(lo, sel))


def _select_rows(sel, a):
    hi, mid, lo = _split3(a)
    return _dot(sel, hi) + (_dot(sel, mid) + _dot(sel, lo))


def _cmul(ar, ai, br, bi):
    return ar * br - ai * bi, ar * bi + ai * br


def _s5_prep_kernel(are_r, aim_r, are_c, aim_c, ldt, bre_ref, bim_ref, cre_ref, cim_ref,
                    w1_ref, e_ref, dec_ref):
    t, n, p = CHUNK, S5_GROUP, S5_STATE
    width = t * n
    lane_tok = lax.broadcasted_iota(jnp.int32, (128, width), 1) // n
    pow_id = lax.broadcasted_iota(jnp.int32, (128, width), 0)
    onehot = lambda cond: jnp.where(cond, 1.0, 0.0).astype(BF16)
    sel_fwd = onehot(pow_id == lane_tok)
    sel_rev = onehot(pow_id == t - 1 - lane_tok)
    sel_out = onehot(pow_id == t - lane_tok)
    row_tok = lax.broadcasted_iota(jnp.int32, (width, 32), 0) // n
    row_pow = lax.broadcasted_iota(jnp.int32, (width, 32), 1)
    rsel_rev = onehot(row_pow == t - 1 - row_tok)
    rsel_fwd = onehot(row_pow == row_tok)
    lane = lax.broadcasted_iota(jnp.int32, (n, width), 1)

    jc = jnp.minimum(lax.broadcasted_iota(jnp.int32, (p, 128), 1), t).astype(F32)
    jr = jnp.minimum(lax.broadcasted_iota(jnp.int32, (32, p), 0), t).astype(F32)

    strips = []
    f_parts = []
    e_parts = []
    for d in range(2):
        dt = jnp.exp(ldt[d, 0])
        mag_c = are_c[d, 0] * dt
        th_c = aim_c[d, 0] * dt
        ec = jnp.exp(jc * mag_c)
        qr = ec * jnp.cos(jc * th_c)
        qi = ec * jnp.sin(jc * th_c)
        a_re = are_r[d, 0]
        a_im = aim_r[d, 0]
        er = jnp.exp(jr * (a_re * dt))
        pr = er * jnp.cos(jr * (a_im * dt))
        pi = er * jnp.sin(jr * (a_im * dt))
        nr = pr[1:2] - 1.0
        ni = pi[1:2]
        den = a_re * a_re + a_im * a_im
        fre = (nr * a_re + ni * a_im) / den
        fim = (ni * a_re - nr * a_im) / den
        bt_re = bre_ref[d, 0].T
        bt_im = bim_ref[d, 0].T
        bb_re, bb_im = _cmul(fre, fim, bt_re, bt_im)
        ct_re = jnp.concatenate([cre_ref[d, 0].T] * t, axis=1)
        ct_im = jnp.concatenate([cim_ref[d, 0].T] * t, axis=1)
        sel = sel_fwd if d == 0 else sel_rev
        w_re, w_im = _cmul(ct_re, ct_im, _select_cols(qr, sel), _select_cols(qi, sel))
        strips.append(_dot3(bb_re, w_re) - _dot3(bb_im, w_im))
        if d == 0:
            o_re, o_im = _cmul(w_re, w_im, qr[:, 1:2], qi[:, 1:2])
        else:
            o_re, o_im = _cmul(ct_re, ct_im, _select_cols(qr, sel_out), _select_cols(qi, sel_out))
        e_parts += [o_re, -o_im]
        rsel = rsel_rev if d == 0 else rsel_fwd
        f_re, f_im = _cmul(jnp.concatenate([bb_re] * t, axis=0), jnp.concatenate([bb_im] * t, axis=0),
                           _select_rows(rsel, pr), _select_rows(rsel, pi))
        f_parts += [f_re, f_im]
        dec_ref[0, 2 * d:2 * d + 1, :] = jnp.concatenate([pr[t:t + 1], pr[t:t + 1]], axis=1)
        dec_ref[0, 2 * d + 1:2 * d + 2, :] = jnp.concatenate([-pi[t:t + 1], pi[t:t + 1]], axis=1)

    blocks = []
    for s in range(t):
        fwd = strips[0] if s == 0 else jnp.where(lane >= n * s, pltpu.roll(strips[0], n * s, axis=1), 0.0)
        back = t - 1 - s
        bwd = strips[1] if back == 0 else jnp.where(lane < width - n * back,
                                                     pltpu.roll(strips[1], width - n * back, axis=1), 0.0)
        blocks.append(fwd + bwd)
    m = jnp.concatenate(blocks, axis=0)
    w1_ref[0] = jnp.concatenate([m] + f_parts, axis=1).astype(BF16)
    e_ref[0] = jnp.concatenate(e_parts, axis=0).astype(BF16)


def _s5_operators(a_re, a_im, log_dt, b_re, b_im, c_re, c_im):
    g, p, n, t = S5_GROUPS, S5_STATE, S5_GROUP, CHUNK
    spec = lambda shape: pl.BlockSpec((2, 1) + shape, lambda i: (0, i) + (0,) * len(shape))
    return pl.pallas_call(
        _s5_prep_kernel, grid=(g,),
        in_specs=[spec((1, p)), spec((1, p)), spec((p, 1)), spec((p, 1)), spec((1, 1)),
                  spec((p, n)), spec((p, n)), spec((n, p)), spec((n, p))],
        out_specs=[pl.BlockSpec((1, t * n, 2 * t * n), lambda i: (i, 0, 0)),
                   pl.BlockSpec((1, 4 * p, t * n), lambda i: (i, 0, 0)),
                   pl.BlockSpec((1, 4, 2 * p), lambda i: (i, 0, 0))],
        out_shape=[jax.ShapeDtypeStruct((g, t * n, 2 * t * n), BF16),
                   jax.ShapeDtypeStruct((g, 4 * p, t * n), BF16),
                   jax.ShapeDtypeStruct((g, 4, 2 * p), F32)],
        compiler_params=_cparams(("parallel",)),
        name="s5_prep",
    )(a_re.reshape(2, g, 1, p), a_im.reshape(2, g, 1, p), a_re.reshape(2, g, p, 1),
      a_im.reshape(2, g, p, 1), log_dt.reshape(2, g, 1, 1), b_re, b_im, c_re, c_im)


def _block_transpose8(ps):
    ps = list(ps)
    blk = lax.broadcasted_iota(jnp.int32, ps[0].shape, 1) // S5_GROUP
    for k in range(3):
        step = 1 << k
        shift = S5_GROUP * step
        keep = ((blk >> k) & 1) == 0
        for a in range(8):
            if a & step:
                continue
            pa, pb = ps[a], ps[a + step]
            ps[a] = jnp.where(keep, pa, pltpu.roll(pb, shift, axis=1))
            ps[a + step] = jnp.where(keep, pltpu.roll(pa, 128 - shift, axis=1), pb)
    return ps


def _s5a_kernel(u_ref, w1_ref, yin_ref, gf_ref, gb_ref):
    nb = u_ref.shape[1]
    xs = []
    for s in range(CHUNK):
        parts = [u_ref[0, b, pl.ds(s, TILE_CHUNKS, stride=CHUNK), :] for b in range(nb)]
        xs.append(jnp.concatenate(parts, axis=0))
    lo = _block_transpose8(xs[:8])
    hi = _block_transpose8(xs[8:])
    for j in range(GRP_PER_BLK):
        og = jnp.concatenate([lo[j], hi[j]], axis=1)
        r = _dot(og.astype(BF16), w1_ref[j])
        yin_ref[j] = r[:, 0:256]
        gf_ref[j] = r[:, 256:384]
        gb_ref[j] = r[:, 384:512]


def _s5a_call(u, w1):
    n_blk, b, s, _ = u.shape
    nt = s // TILE_TOK
    rows = b * TILE_CHUNKS
    out_spec = lambda n: pl.BlockSpec((GRP_PER_BLK, rows, n), lambda j, l: (l, j, 0))
    return pl.pallas_call(
        _s5a_kernel, grid=(nt, n_blk),
        in_specs=[pl.BlockSpec((1, b, TILE_TOK, LANE_BLK), lambda j, l: (l, 0, j, 0)),
                  pl.BlockSpec((GRP_PER_BLK, 256, 512), lambda j, l: (l, 0, 0))],
        out_specs=[out_spec(256), out_spec(128), out_spec(128)],
        out_shape=[jax.ShapeDtypeStruct((S5_GROUPS, nt * rows, 256), F32),
                   jax.ShapeDtypeStruct((S5_GROUPS, nt * rows, 128), F32),
                   jax.ShapeDtypeStruct((S5_GROUPS, nt * rows, 128), F32)],
        compiler_params=_cparams(("parallel", "parallel")),
        name="s5_chunk_in",
    )(u, w1)


def _s5b_kernel(nb, a_ref, gfc_ref, gfl_ref, gbc_ref, gbl_ref, hfc_ref, hfl_ref, hbc_ref, hbl_ref):
    gb = a_ref.shape[0]
    rows = nb * TILE_CHUNKS
    n_lat = gfl_ref.shape[1] // rows
    a1f = [jnp.broadcast_to(a_ref[g, 0:1, :], (nb, 128)) for g in range(gb)]
    a2f = [jnp.broadcast_to(a_ref[g, 1:2, :], (nb, 128)) for g in range(gb)]
    a1b = [jnp.broadcast_to(a_ref[g, 2:3, :], (nb, 128)) for g in range(gb)]
    a2b = [jnp.broadcast_to(a_ref[g, 3:4, :], (nb, 128)) for g in range(gb)]

    def step(state, a1, a2, g_ref, h_ref, g, row):
        h, hs = state
        h_ref[g, row, :] = h
        inp = g_ref[g, row, :]
        return a1 * h + a2 * hs + inp, a1 * hs - a2 * h + pltpu.roll(inp, 64, axis=1)

    zero = jnp.zeros((nb, 128), F32)
    hf = [(zero, zero) for _ in range(gb)]
    hb = [(zero, zero) for _ in range(gb)]
    for ci in range(TILE_CHUNKS):
        rf = pl.ds(ci, nb, stride=TILE_CHUNKS)
        rb = pl.ds(TILE_CHUNKS - 1 - ci, nb, stride=TILE_CHUNKS)
        for g in range(gb):
            hf[g] = step(hf[g], a1f[g], a2f[g], gfc_ref, hfc_ref, g, rf)
            hb[g] = step(hb[g], a1b[g], a2b[g], gbc_ref, hbc_ref, g, rb)

    def body(j, carry):
        hf, hb = carry
        hf = list(hf)
        hb = list(hb)
        base_f = j * rows
        base_b = (n_lat - 1 - j) * rows
        for ci in range(TILE_CHUNKS):
            rf = pl.ds(base_f + ci, nb, stride=TILE_CHUNKS)
            rb = pl.ds(base_b + (TILE_CHUNKS - 1 - ci), nb, stride=TILE_CHUNKS)
            for g in range(gb):
                hf[g] = step(hf[g], a1f[g], a2f[g], gfl_ref, hfl_ref, g, rf)
                hb[g] = step(hb[g], a1b[g], a2b[g], gbl_ref, hbl_ref, g, rb)
        return tuple(hf), tuple(hb)

    lax.fori_loop(0, n_lat, body, (tuple(hf), tuple(hb)))


def _s5b_call(decay, gf_c, gf_l, gb_c, gb_l, nb):
    gblk = 4
    spec = lambda a: pl.BlockSpec((gblk, a.shape[1], 128), lambda i: (i, 0, 0))
    sds = lambda a: jax.ShapeDtypeStruct(a.shape, F32)
    return pl.pallas_call(
        functools.partial(_s5b_kernel, nb), grid=(S5_GROUPS // gblk,),
        in_specs=[pl.BlockSpec((gblk, 4, 128), lambda i: (i, 0, 0)),
                  spec(gf_c), spec(gf_l), spec(gb_c), spec(gb_l)],
        out_specs=[spec(gf_c), spec(gf_l), spec(gb_c), spec(gb_l)],
        out_shape=[sds(gf_c), sds(gf_l), sds(gb_c), sds(gb_l)],
        compiler_params=_cparams(("parallel",)),
        name="s5_state_scan",
    )(decay, gf_c, gf_l, gb_c, gb_l)


def _s5c_kernel(yin_ref, hf_ref, hb_ref, e_ref, u_ref, d_ref, y_ref):
    nb = u_ref.shape[1]
    ys = []
    for j in range(GRP_PER_BLK):
        h = jnp.concatenate([hf_ref[j], hb_ref[j]], axis=1).astype(BF16)
        ys.append(yin_ref[j] + _dot(h, e_ref[j]))
    at = (_block_transpose8([y[:, :128] for y in ys])
          + _block_transpose8([y[:, 128:] for y in ys]))
    d = d_ref[...]
    for t in range(CHUNK):
        for b in range(nb):
            rows = pl.ds(t, TILE_CHUNKS, stride=CHUNK)
            y_ref[0, b, rows, :] = at[t][b * TILE_CHUNKS:(b + 1) * TILE_CHUNKS] + d * u_ref[0, b, rows, :]


def _s5c_call(yin, hf, hb, e, u, d_skip):
    n_blk, b, s, _ = u.shape
    nt = s // TILE_TOK
    rows = b * TILE_CHUNKS
    gspec = lambda n: pl.BlockSpec((GRP_PER_BLK, rows, n), lambda j, l: (l, j, 0))
    tok_spec = pl.BlockSpec((1, b, TILE_TOK, LANE_BLK), lambda j, l: (l, 0, j, 0))
    return pl.pallas_call(
        _s5c_kernel, grid=(nt, n_blk),
        in_specs=[gspec(256), gspec(128), gspec(128),
                  pl.BlockSpec((GRP_PER_BLK, 256, 256), lambda j, l: (l, 0, 0)),
                  tok_spec,
                  pl.BlockSpec((1, LANE_BLK), lambda j, l: (0, l))],
        out_specs=tok_spec,
        out_shape=_lane_block_shape(b, s),
        compiler_params=_cparams(("parallel", "parallel")),
        name="s5_chunk_out",
    )(yin, hf, hb, e, u, d_skip.reshape(1, D_S5))


def _gelu_tanh(x):
    return 0.5 * x * (1.0 + jnp.tanh(math.sqrt(2.0 / math.pi) * (x + 0.044715 * (x * x * x))))


def _route(logits):
    lane = lax.broadcasted_iota(jnp.int32, logits.shape, 1).astype(F32)
    neg = jnp.float32(-1e30)
    big = jnp.float32(1e9)
    gl = jnp.where(lane < N_GROUPS, logits, neg)
    gmax = jnp.max(gl, axis=1, keepdims=True)
    gidx = jnp.min(jnp.where(gl == gmax, lane, big), axis=1, keepdims=True)
    gsum = jnp.sum(jnp.exp(gl - gmax), axis=1, keepdims=True)
    gw = 1.0 / gsum
    lo = N_GROUPS + EXP_PER_GROUP * gidx
    el = jnp.where((lane >= lo) & (lane < lo + EXP_PER_GROUP), logits, neg)
    v1 = jnp.max(el, axis=1, keepdims=True)
    i1 = jnp.min(jnp.where(el == v1, lane, big), axis=1, keepdims=True)
    el2 = jnp.where(lane == i1, neg, el)
    v2 = jnp.max(el2, axis=1, keepdims=True)
    i2 = jnp.min(jnp.where(el2 == v2, lane, big), axis=1, keepdims=True)
    ex = jnp.exp(v2 - v1)
    p1 = 1.0 / (1.0 + ex)
    p2 = ex * p1
    e1 = i1 - lo
    e2 = i2 - lo
    first = e1 < e2
    ea = jnp.where(first, e1, e2)
    eb = jnp.where(first, e2, e1)
    wa = gw * jnp.where(first, p1, p2)
    wb = gw * jnp.where(first, p2, p1)
    pair = ea * (7.0 - ea) * 0.5 + (eb - ea - 1.0)
    return wa, wb, 6.0 * gidx + pair


def _out_kernel(ypre_ref, ysc_ref, ycf_ref, x_ref, mod_ref, wglu_ref, bglu_ref, wo_ref,
                lng_ref, lnb_ref, wr_ref, br_ref, x1_ref, hx_ref, meta_ref, counts_ref, cnt_ref):
    @pl.when((pl.program_id(0) == 0) & (pl.program_id(1) == 0))
    def _():
        cnt_ref[...] = jnp.zeros_like(cnt_ref)

    ypre = jnp.concatenate([ypre_ref[blk, 0] for blk in range(ypre_ref.shape[0])], axis=1)
    t = _gelu_tanh(ypre)
    gate = _sigmoid(_dot(t.astype(BF16), wglu_ref[...]) + bglu_ref[...])
    ys5 = (t * gate).astype(BF16)
    y = (_dot(ys5, wo_ref[0:D_S5, :]) + _dot(ysc_ref[0], wo_ref[D_S5:D_S5 + D_SC, :])
         + _dot(ycf_ref[0], wo_ref[D_S5 + D_SC:D_MODEL, :]))
    g1 = mod_ref[0, 2:3, :]
    x1 = _layer_norm(DN_ALPHA * x_ref[0] + g1 * y, lng_ref[...], lnb_ref[...])
    x1_ref[0] = x1
    h2 = x1 * (1.0 + mod_ref[0, 4:5, :]) + mod_ref[0, 3:4, :]
    wa, wb, cls = _route(_dot(h2.astype(BF16), wr_ref[...]) + br_ref[...])

    tm = h2.shape[0]
    lane = lax.broadcasted_iota(jnp.int32, (tm, ROUTER_LANES), 1).astype(F32)
    onehot = jnp.where(lane == cls, 1.0, 0.0)
    row_i = lax.broadcasted_iota(jnp.int32, (tm, tm), 0)
    col_i = lax.broadcasted_iota(jnp.int32, (tm, tm), 1)
    earlier = jnp.where(col_i < row_i, 1.0, 0.0).astype(BF16)
    before = _dot(earlier, onehot.astype(BF16)) + cnt_ref[...]
    rank = jnp.sum(before * onehot, axis=1, keepdims=True)
    cnt_ref[...] += jnp.sum(onehot, axis=0, keepdims=True)
    counts_ref[...] = cnt_ref[...]

    meta = (jnp.where(lane == META_WA, wa, 0.0) + jnp.where(lane == META_WB, wb, 0.0)
            + jnp.where(lane == META_CLS, cls, 0.0) + jnp.where(lane == META_RANK, rank, 0.0))
    meta_ref[0] = meta
    hx_ref[0, :, 0:D_MODEL] = h2
    hx_ref[0, :, D_MODEL:HX_LANES] = meta


def _out_call(ypre, ysc, ycf, x, mod, wglu_bf, b_glu, wo_bf, ln_g, ln_b, w_router, b_router, tm):
    b, s, d = x.shape
    row_spec = lambda n: pl.BlockSpec((1, tm, n), lambda i, j: (i, j, 0))
    full = lambda shape: pl.BlockSpec(shape, lambda i, j: (0,) * len(shape))
    return pl.pallas_call(
        _out_kernel, grid=(b, s // tm),
        in_specs=[_lane_block_spec(tm), row_spec(D_SC), row_spec(D_CF), row_spec(d),
                  pl.BlockSpec((1, 6, d), lambda i, j: (i, 0, 0)),
                  full((D_S5, D_S5)), full((1, D_S5)), full((d, d)),
                  full((1, d)), full((1, d)), full((d, ROUTER_LANES)), full((1, ROUTER_LANES))],
        out_specs=[row_spec(d), row_spec(HX_LANES), row_spec(ROUTER_LANES), full((1, ROUTER_LANES))],
        out_shape=[jax.ShapeDtypeStruct((b, s, d), F32), jax.ShapeDtypeStruct((b, s, HX_LANES), F32),
                   jax.ShapeDtypeStruct((b, s, ROUTER_LANES), F32),
                   jax.ShapeDtypeStruct((1, ROUTER_LANES), F32)],
        scratch_shapes=[pltpu.VMEM((1, ROUTER_LANES), F32)],
        compiler_params=_cparams(("arbitrary", "arbitrary")),
        name="out_proj",
    )(ypre, ysc, ycf, x, mod, wglu_bf, b_glu.reshape(1, -1), wo_bf, ln_g.reshape(1, -1),
      ln_b.reshape(1, -1), w_router, b_router)


def _moe_plan(meta, counts, n_tok):
    cls = meta[..., META_CLS].reshape(-1).astype(jnp.int32)
    rank = meta[..., META_RANK].reshape(-1).astype(jnp.int32)
    cnt = counts[0, :N_CLASSES].astype(jnp.int32)
    n_tiles = (cnt + (MOE_TM - 1)) // MOE_TM
    ends = jnp.cumsum(n_tiles)
    starts = ends - n_tiles
    slot = starts[cls] * MOE_TM + rank
    t_max = n_tok // MOE_TM + N_CLASSES
    n_used = ends[N_CLASSES - 1]
    tile = jnp.minimum(jnp.arange(t_max, dtype=jnp.int32), n_used - 1)
    tile_cls = jnp.sum((tile[:, None] >= ends[None, :]).astype(jnp.int32), axis=1)
    group = tile_cls // 6
    pair = tile_cls % 6
    first = jnp.array([0, 0, 0, 1, 1, 2], jnp.int32)[pair] + EXP_PER_GROUP * group
    second = jnp.array([1, 2, 3, 2, 3, 3], jnp.int32)[pair] + EXP_PER_GROUP * group
    return slot, tile, first, second, n_used.reshape(1)


def _dispatch_kernel(slot_ref, hx_ref, xs_init_ref, xs_ref, sem):
    del xs_init_ref
    n_oct = hx_ref.shape[1]
    base = (pl.program_id(0) * pl.num_programs(1) + pl.program_id(1)) * (n_oct * SUBLANES)

    def body(i, carry):
        for k in range(SUBLANES):
            oct_id, sub = _split_row(slot_ref[base + i * SUBLANES + k])
            pltpu.make_async_copy(hx_ref.at[0, i, pl.ds(k, 1), :], xs_ref.at[oct_id, pl.ds(sub, 1), :], sem).start()
        return carry

    lax.fori_loop(0, n_oct, body, 0)
    pltpu.make_async_copy(hx_ref.at[0], xs_ref.at[pl.ds(0, n_oct)], sem).wait()


def _split_row(row):
    return lax.shift_right_logical(row, 3), lax.bitwise_and(row, SUBLANES - 1)


def _dispatch_call(slot, hx, n_rows, tm):
    b, s, w = hx.shape
    xs_init = jnp.zeros((n_rows // SUBLANES, SUBLANES, w), F32)
    grid_spec = pltpu.PrefetchScalarGridSpec(
        num_scalar_prefetch=1, grid=(b, s // tm),
        in_specs=[pl.BlockSpec((1, tm // SUBLANES, SUBLANES, w), lambda i, j, slot: (i, j, 0, 0)),
                  pl.BlockSpec(memory_space=pl.ANY)],
        out_specs=pl.BlockSpec(memory_space=pl.ANY),
        scratch_shapes=[pltpu.SemaphoreType.DMA(())])
    xs = pl.pallas_call(
        _dispatch_kernel, grid_spec=grid_spec,
        out_shape=jax.ShapeDtypeStruct(xs_init.shape, F32),
        input_output_aliases={2: 0},
        compiler_params=_cparams(("arbitrary", "arbitrary")),
        name="moe_dispatch",
    )(slot, hx.reshape(b, s // SUBLANES, SUBLANES, w), xs_init)
    return xs.reshape(n_rows, w)


def _moe_kernel(tile_ref, first_ref, second_ref, nused_ref, xs_ref, wga_ref, wgb_ref, wua_ref, wub_ref,
                wda_ref, wdb_ref, ys_ref):
    del tile_ref, first_ref, second_ref

    @pl.when(pl.program_id(0) < nused_ref[0])
    def _():
        x = xs_ref[...]
        xb = x[:, 0:D_MODEL].astype(BF16)

        def expert(wg_ref, wu_ref, wd_ref, w):
            gate = _dot(xb, wg_ref[0])
            up = _dot(xb, wu_ref[0])
            act = gate * _sigmoid(gate) * up * w
            return _dot(act.astype(BF16), wd_ref[0])

        wa = x[:, D_MODEL + META_WA:D_MODEL + META_WA + 1]
        wb = x[:, D_MODEL + META_WB:D_MODEL + META_WB + 1]
        ys_ref[...] = expert(wga_ref, wua_ref, wda_ref, wa) + expert(wgb_ref, wub_ref, wdb_ref, wb)

    @pl.when(pl.program_id(0) >= nused_ref[0])
    def _():
        ys_ref[...] = jnp.zeros_like(ys_ref)


def _moe_call(tile, first, second, n_used, xs, wg_bf, wu_bf, wd_bf):
    n_rows, w = xs.shape
    d = D_MODEL
    t_max = tile.shape[0]
    up_spec = lambda sel: pl.BlockSpec((1, d, D_EXPERT), lambda t, tl, fi, se, nu: ((fi, se)[sel][t], 0, 0))
    down_spec = lambda sel: pl.BlockSpec((1, D_EXPERT, d), lambda t, tl, fi, se, nu: ((fi, se)[sel][t], 0, 0))
    grid_spec = pltpu.PrefetchScalarGridSpec(
        num_scalar_prefetch=4, grid=(t_max,),
        in_specs=[pl.BlockSpec((MOE_TM, w), lambda t, tl, fi, se, nu: (tl[t], 0)),
                  up_spec(0), up_spec(1), up_spec(0), up_spec(1), down_spec(0), down_spec(1)],
        out_specs=pl.BlockSpec((MOE_TM, d), lambda t, tl, fi, se, nu: (t, 0)))
    return pl.pallas_call(
        _moe_kernel, grid_spec=grid_spec,
        out_shape=jax.ShapeDtypeStruct((n_rows, d), F32),
        compiler_params=_cparams(("arbitrary",)),
        name="moe_experts",
    )(tile, first, second, n_used, xs, wg_bf, wg_bf, wu_bf, wu_bf, wd_bf, wd_bf)


def _combine_kernel(slot_ref, x1_ref, mod_ref, lng_ref, lnb_ref, ys_ref, o_ref, f_ref, sem):
    n_oct = f_ref.shape[1]
    tm = n_oct * SUBLANES
    step = pl.program_id(0) * pl.num_programs(1) + pl.program_id(1)
    n_steps = pl.num_programs(0) * pl.num_programs(1)

    def request(which, buf):
        base = which * tm

        def body(i, carry):
            for k in range(SUBLANES):
                oct_id, sub = _split_row(slot_ref[base + i * SUBLANES + k])
                pltpu.make_async_copy(ys_ref.at[oct_id, pl.ds(sub, 1), :], f_ref.at[buf, i, pl.ds(k, 1), :],
                                      sem.at[buf]).start()
            return carry

        lax.fori_loop(0, n_oct, body, 0)

    @pl.when(step == 0)
    def _():
        request(0, 0)

    @pl.when(step + 1 < n_steps)
    def _():
        request(step + 1, (step + 1) % 2)

    buf = step % 2
    pltpu.make_async_copy(ys_ref.at[pl.ds(0, n_oct)], f_ref.at[buf], sem.at[buf]).wait()
    f = f_ref[buf].reshape(tm, f_ref.shape[3])
    g2 = mod_ref[0, 5:6, :]
    o_ref[0] = _layer_norm(DN_ALPHA * x1_ref[0] + g2 * f, lng_ref[...], lnb_ref[...])


def _combine_call(slot, x1, mod, ln_g, ln_b, ys, tm):
    b, s, d = x1.shape
    ys = ys.reshape(ys.shape[0] // SUBLANES, SUBLANES, d)
    grid_spec = pltpu.PrefetchScalarGridSpec(
        num_scalar_prefetch=1, grid=(b, s // tm),
        in_specs=[pl.BlockSpec((1, tm, d), lambda i, j, slot: (i, j, 0)),
                  pl.BlockSpec((1, 6, d), lambda i, j, slot: (i, 0, 0)),
                  pl.BlockSpec((1, d), lambda i, j, slot: (0, 0)),
                  pl.BlockSpec((1, d), lambda i, j, slot: (0, 0)),
                  pl.BlockSpec(memory_space=pl.ANY)],
        out_specs=pl.BlockSpec((1, tm, d), lambda i, j, slot: (i, j, 0)),
        scratch_shapes=[pltpu.VMEM((2, tm // SUBLANES, SUBLANES, d), F32), pltpu.SemaphoreType.DMA((2,))])
    return pl.pallas_call(
        _combine_kernel, grid_spec=grid_spec,
        out_shape=jax.ShapeDtypeStruct((b, s, d), F32),
        compiler_params=_cparams(("arbitrary", "arbitrary")),
        name="moe_combine",
    )(slot, x1, mod, ln_g.reshape(1, -1), ln_b.reshape(1, -1), ys)


def _moe_sublayer(hx, meta, counts, x1, mod, wg_bf, wu_bf, wd_bf, ln_g, ln_b, tm):
    b, s, _ = x1.shape
    n_tok = b * s
    slot, tile, first, second, n_used = _moe_plan(meta, counts, n_tok)
    n_rows = n_tok + N_CLASSES * MOE_TM
    xs = _dispatch_call(slot, hx, n_rows, tm)
    ys = _moe_call(tile, first, second, n_used, xs, wg_bf, wu_bf, wd_bf)
    return _combine_call(slot, x1, mod, ln_g, ln_b, ys, tm)


def kernel(x, c, ctx, c_ctx, w_mod, b_mod, w_in, s5_a_re, s5_a_im, s5_log_dt, s5_b_re, s5_b_im, s5_c_re, s5_c_im, s5_d, w_glu, b_glu, w_sc, w_dw, b_dw, ln_cf_g, ln_cf_b, w_o, ln1_g, ln1_b, w_rg, b_rg, w_rexp, b_rexp, w_gate, w_up, w_down, ln2_g, ln2_b):
    nb, seq, d = x.shape
    n_ctx = ctx.shape[1]
    n_layers = w_mod.shape[0]
    assert seq % TILE_TOK == 0 and n_ctx % TILE_TOK == 0 and seq % GRID_W == 0

    mod_rows = 16
    assert nb + 1 <= mod_rows
    c_all = jnp.concatenate([c, c_ctx[None, :], jnp.zeros((mod_rows - nb - 1, d), F32)], axis=0)
    mod_all = _mod_call(c_all, w_mod, b_mod)

    pad_r = ROUTER_LANES - N_GROUPS - N_EXPERTS
    x_lat, x_ctx = x, ctx
    for l in range(n_layers):
        last = l == n_layers - 1
        mod_lat = mod_all[l, :nb].reshape(nb, 6, d)
        mod_ctx = jnp.broadcast_to(mod_all[l, nb].reshape(1, 6, d), (nb, 6, d))
        w_in_bf = w_in[l].astype(BF16)
        wglu_bf = w_glu[l].astype(BF16)
        wo_bf = w_o[l].astype(BF16)
        wg_bf = w_gate[l].astype(BF16)
        wu_bf = w_up[l].astype(BF16)
        wd_bf = w_down[l].astype(BF16)
        w_router = jnp.concatenate([w_rg[l], w_rexp[l], jnp.zeros((d, pad_r), F32)], axis=1).astype(BF16)
        b_router = jnp.concatenate([b_rg[l], b_rexp[l], jnp.zeros((pad_r,), F32)]).reshape(1, -1)
        w1, e_op, decay = _s5_operators(s5_a_re[l], s5_a_im[l], s5_log_dt[l], s5_b_re[l], s5_b_im[l],
                                        s5_c_re[l], s5_c_im[l])

        u_l, bg_l, cv_l, gl_l = _in_call(x_lat, mod_lat, w_in_bf, 512, False)
        if last:
            u_c = _in_call(x_ctx, mod_ctx, w_in_bf[:, :D_S5], TILE_TOK, True)
        else:
            u_c, bg_c, cv_c, gl_c = _in_call(x_ctx, mod_ctx, w_in_bf, TILE_TOK, False)

        yin_l, gf_l, gb_l = _s5a_call(u_l, w1)
        yin_c, gf_c, gb_c = _s5a_call(u_c, w1)
        hf_c, hf_l, hb_c, hb_l = _s5b_call(decay, gf_c, gf_l, gb_c, gb_l, nb)
        ypre_l = _s5c_call(yin_l, hf_l, hb_l, e_op, u_l, s5_d[l])

        ysc_l, ycf_l = _conv_call(bg_l, cv_l, gl_l, w_sc[l], w_dw[l], b_dw[l], ln_cf_g[l], ln_cf_b[l], True)
        x1_l, hx_l, meta_l, cnt_l = _out_call(ypre_l, ysc_l, ycf_l, x_lat, mod_lat, wglu_bf, b_glu[l], wo_bf,
                                              ln1_g[l], ln1_b[l], w_router, b_router, 512)
        if not last:
            ypre_c = _s5c_call(yin_c, hf_c, hb_c, e_op, u_c, s5_d[l])
            ysc_c, ycf_c = _conv_call(bg_c, cv_c, gl_c, w_sc[l], w_dw[l], b_dw[l], ln_cf_g[l], ln_cf_b[l], False)
            x1_c, hx_c, meta_c, cnt_c = _out_call(ypre_c, ysc_c, ycf_c, x_ctx, mod_ctx, wglu_bf, b_glu[l], wo_bf,
                                                  ln1_g[l], ln1_b[l], w_router, b_router, TILE_TOK)
            x_ctx = _moe_sublayer(hx_c, meta_c, cnt_c, x1_c, mod_ctx, wg_bf, wu_bf, wd_bf,
                                  ln2_g[l], ln2_b[l], TILE_TOK)
        x_lat = _moe_sublayer(hx_l, meta_l, cnt_l, x1_l, mod_lat, wg_bf, wu_bf, wd_bf,
                              ln2_g[l], ln2_b[l], 512)
    return x_lat
```

```python
import functools
import math

import jax
import jax.numpy as jnp
from jax import lax
from jax.experimental import pallas as pl
from jax.experimental.pallas import tpu as pltpu

F32 = jnp.float32
BF16 = jnp.bfloat16

D_MODEL = 1024
DEPTH = 2
GRID_W = 64
D_S5 = 512
S5_GROUP = 16
S5_GROUPS = 32
S5_STATE = 64
D_SC = 256
D_CF = 256
CF_WIDTH = 31
CF_HALF = 15
D_IN = 1792
N_GROUPS = 4
EXP_PER_GROUP = 4
N_EXPERTS = 16
D_EXPERT = 256
DN_ALPHA = (2 * DEPTH) ** 0.25
LN_EPS = 1e-5

CHUNK = 16
TILE_CHUNKS = 16
TILE_TOK = CHUNK * TILE_CHUNKS
LANE_BLK = 128
GRP_PER_BLK = LANE_BLK // S5_GROUP
ROUTER_LANES = 128
HX_LANES = D_MODEL + ROUTER_LANES
META_WA, META_WB, META_CLS, META_RANK = 0, 1, 2, 3
N_CLASSES = N_GROUPS * 6
MOE_TM = 256
SUBLANES = 8
VMEM_LIMIT = 56 * 1024 * 1024


def _cparams(sem):
    return pltpu.CompilerParams(dimension_semantics=sem, vmem_limit_bytes=VMEM_LIMIT)


def _split_bf16(a):
    hi = a.astype(BF16)
    lo = (a - hi.astype(F32)).astype(BF16)
    return hi, lo


def _dot(a, b):
    return jnp.dot(a, b, preferred_element_type=F32)


def _dot3(a, b):
    ah, al = _split_bf16(a)
    bh, bl = _split_bf16(b)
    return _dot(ah, bh) + (_dot(al, bh) + _dot(ah, bl))


def _sigmoid(x):
    return 1.0 / (1.0 + jnp.exp(-x))


def _layer_norm(x, g, b):
    mu = jnp.mean(x, axis=-1, keepdims=True)
    xc = x - mu
    var = jnp.mean(xc * xc, axis=-1, keepdims=True)
    return xc * lax.rsqrt(var + LN_EPS) * g + b


def _mod_kernel(c_ref, w_ref, b_ref, o_ref):
    c = c_ref[...]
    s = c * _sigmoid(c)
    o_ref[0] = _dot3(s, w_ref[0]) + b_ref[0]


def _mod_call(c_all, w_mod, b_mod):
    n_layers, d, n_out = w_mod.shape
    tn = 1536
    rows = c_all.shape[0]
    return pl.pallas_call(
        _mod_kernel,
        grid=(n_layers, n_out // tn),
        in_specs=[
            pl.BlockSpec((rows, d), lambda l, j: (0, 0)),
            pl.BlockSpec((1, d, tn), lambda l, j: (l, 0, j)),
            pl.BlockSpec((1, 1, tn), lambda l, j: (l, 0, j)),
        ],
        out_specs=pl.BlockSpec((1, rows, tn), lambda l, j: (l, 0, j)),
        out_shape=jax.ShapeDtypeStruct((n_layers, rows, n_out), F32),
        compiler_params=_cparams(("parallel", "parallel")),
        name="mod",
    )(c_all, w_mod, b_mod.reshape(n_layers, 1, n_out))


def _in_kernel(x_ref, mod_ref, w_ref, u_ref, bg_ref, cv_ref, gl_ref):
    x = x_ref[0]
    sh = mod_ref[0, 0:1, :]
    sc = mod_ref[0, 1:2, :]
    h = (x * (1.0 + sc) + sh).astype(BF16)
    z = _dot(h, w_ref[...])
    _store_lane_blocks(u_ref, z[:, 0:512])
    bg_ref[0] = z[:, 512:768]
    cv_ref[0] = z[:, 768:1024] * z[:, 1024:1280]
    gl_ref[0] = z[:, 1280:1536] * _sigmoid(z[:, 1536:1792])


def _in_u_kernel(x_ref, mod_ref, w_ref, u_ref):
    x = x_ref[0]
    sh = mod_ref[0, 0:1, :]
    sc = mod_ref[0, 1:2, :]
    h = (x * (1.0 + sc) + sh).astype(BF16)
    _store_lane_blocks(u_ref, _dot(h, w_ref[...]))


def _store_lane_blocks(ref, val):
    for blk in range(ref.shape[0]):
        ref[blk, 0] = val[:, blk * LANE_BLK:(blk + 1) * LANE_BLK]


def _lane_block_spec(tm):
    return pl.BlockSpec((D_S5 // LANE_BLK, 1, tm, LANE_BLK), lambda i, j: (0, i, j, 0))


def _lane_block_shape(b, s):
    return jax.ShapeDtypeStruct((D_S5 // LANE_BLK, b, s, LANE_BLK), F32)


def _in_call(x, mod, w_in_bf, tm, u_only):
    b, s, d = x.shape
    grid = (b, s // tm)
    row_spec = lambda n: pl.BlockSpec((1, tm, n), lambda i, j: (i, j, 0))
    in_specs = [
        row_spec(d),
        pl.BlockSpec((1, 6, d), lambda i, j: (i, 0, 0)),
    ]
    if u_only:
        in_specs.append(pl.BlockSpec((d, D_S5), lambda i, j: (0, 0)))
        return pl.pallas_call(
            _in_u_kernel, grid=grid, in_specs=in_specs,
            out_specs=_lane_block_spec(tm),
            out_shape=_lane_block_shape(b, s),
            compiler_params=_cparams(("parallel", "parallel")),
            name="in_proj_u",
        )(x, mod, w_in_bf)
    in_specs.append(pl.BlockSpec((d, D_IN), lambda i, j: (0, 0)))
    return pl.pallas_call(
        _in_kernel, grid=grid, in_specs=in_specs,
        out_specs=[_lane_block_spec(tm), row_spec(D_SC), row_spec(D_SC), row_spec(D_CF)],
        out_shape=[_lane_block_shape(b, s),
                   jax.ShapeDtypeStruct((b, s, D_SC), F32),
                   jax.ShapeDtypeStruct((b, s, D_SC), F32),
                   jax.ShapeDtypeStruct((b, s, D_CF), F32)],
        compiler_params=_cparams(("parallel", "parallel")),
        name="in_proj",
    )(x, mod, w_in_bf)


def _conv_tail(t, bdw_ref, lng_ref, lnb_ref):
    t = t + bdw_ref[...]
    t = _layer_norm(t, lng_ref[...], lnb_ref[...])
    return t * _sigmoid(t)


def _conv_grid_kernel(bg_ref, cv_ref, gl_ref, wsc_ref, wdw_ref, bdw_ref, lng_ref, lnb_ref,
                      ysc_ref, ycf_ref, pad_ref, t_ref):
    s = cv_ref.shape[1]
    rows = s // GRID_W
    cv = cv_ref[0]
    col = lax.broadcasted_iota(jnp.int32, (s, D_SC), 0) % GRID_W
    prev = jnp.where(col == 0, 0.0, pltpu.roll(cv, 1, axis=0))
    nxt = jnp.where(col == GRID_W - 1, 0.0, pltpu.roll(cv, s - 1, axis=0))
    conv = prev * wsc_ref[0:1, :] + cv * wsc_ref[1:2, :] + nxt * wsc_ref[2:3, :]
    ysc_ref[0] = (bg_ref[0] * conv).astype(ysc_ref.dtype)

    zero = jnp.zeros((CF_HALF, GRID_W, D_CF), F32)
    pad_ref[0:CF_HALF] = zero
    pad_ref[CF_HALF + rows:CF_HALF + rows + CF_HALF] = zero
    pad_ref[CF_HALF:CF_HALF + rows] = gl_ref[0].reshape(rows, GRID_W, D_CF)

    def body(i, carry):
        w0 = pl.multiple_of(i * 8, 8)
        for half in range(D_CF // 128):
            lanes = slice(half * 128, (half + 1) * 128)
            acc = jnp.zeros((rows, 8, 128), F32)
            for k in range(CF_WIDTH):
                acc = acc + pad_ref[k:k + rows, pl.ds(w0, 8), lanes] * wdw_ref[k:k + 1, lanes]
            t_ref[:, pl.ds(w0, 8), lanes] = acc
        return carry

    lax.fori_loop(0, GRID_W // 8, body, 0)
    t = t_ref[...].reshape(s, D_CF)
    ycf_ref[0] = _conv_tail(t, bdw_ref, lng_ref, lnb_ref).astype(ycf_ref.dtype)


def _conv_seq_kernel(bg_ref, cv_ref, gl_ref, wsc_ref, wdw_ref, bdw_ref, lng_ref, lnb_ref,
                     ysc_ref, ycf_ref, pad_ref):
    s = cv_ref.shape[1]
    cv = cv_ref[0]
    pos = lax.broadcasted_iota(jnp.int32, (s, D_SC), 0)
    prev = jnp.where(pos == 0, 0.0, pltpu.roll(cv, 1, axis=0))
    nxt = jnp.where(pos == s - 1, 0.0, pltpu.roll(cv, s - 1, axis=0))
    conv = prev * wsc_ref[0:1, :] + cv * wsc_ref[1:2, :] + nxt * wsc_ref[2:3, :]
    ysc_ref[0] = (bg_ref[0] * conv).astype(ysc_ref.dtype)

    off = 16
    pad_ref[0:off] = jnp.zeros((off, D_CF), F32)
    pad_ref[off + s:off + s + 16] = jnp.zeros((16, D_CF), F32)
    pad_ref[off:off + s] = gl_ref[0]
    acc = jnp.zeros((s, D_CF), F32)
    for k in range(CF_WIDTH):
        acc = acc + pad_ref[pl.ds(off - CF_HALF + k, s), :] * wdw_ref[k:k + 1, :]
    ycf_ref[0] = _conv_tail(acc, bdw_ref, lng_ref, lnb_ref).astype(ycf_ref.dtype)


def _conv_call(bg, cv, gl, w_sc, w_dw, b_dw, ln_g, ln_b, grid_mode):
    b, s, _ = bg.shape
    row_spec = pl.BlockSpec((1, s, D_SC), lambda i: (i, 0, 0))
    full = lambda shape: pl.BlockSpec(shape, lambda i: (0,) * len(shape))
    if grid_mode:
        rows = s // GRID_W
        kern = _conv_grid_kernel
        scratch = [pltpu.VMEM((rows + 2 * CF_HALF, GRID_W, D_CF), F32),
                   pltpu.VMEM((rows, GRID_W, D_CF), F32)]
        name = "conv_grid"
    else:
        kern = _conv_seq_kernel
        scratch = [pltpu.VMEM((s + 32, D_CF), F32)]
        name = "conv_seq"
    return pl.pallas_call(
        kern, grid=(b,),
        in_specs=[row_spec, row_spec, row_spec, full((3, D_SC)), full((CF_WIDTH, D_CF)),
                  full((1, D_CF)), full((1, D_CF)), full((1, D_CF))],
        out_specs=[row_spec, row_spec],
        out_shape=[jax.ShapeDtypeStruct((b, s, D_SC), BF16), jax.ShapeDtypeStruct((b, s, D_CF), BF16)],
        scratch_shapes=scratch,
        compiler_params=_cparams(("parallel",)),
        name=name,
    )(bg, cv, gl, w_sc, w_dw, b_dw.reshape(1, -1), ln_g.reshape(1, -1), ln_b.reshape(1, -1))


def _split3(a):
    hi = a.astype(BF16)
    r = a - hi.astype(F32)
    mid = r.astype(BF16)
    lo = (r - mid.astype(F32)).astype(BF16)
    return hi, mid, lo


def _select_cols(a, sel):
    hi, mid, lo = _split3(a)
    return _dot(hi, sel) + (_dot(mid, sel) + _dot(lo, sel))


def _select_rows(sel, a):
    hi, mid, lo = _split3(a)
    return _dot(sel, hi) + (_dot(sel, mid) + _dot(sel, lo))


def _cmul(ar, ai, br, bi):
    return ar * br - ai * bi, ar * bi + ai * br


def _s5_prep_kernel(are_r, aim_r, are_c, aim_c, ldt, bre_ref, bim_ref, cre_ref, cim_ref,
                    w1_ref, e_ref, dec_ref):
    t, n, p = CHUNK, S5_GROUP, S5_STATE
    width = t * n
    lane_tok = lax.broadcasted_iota(jnp.int32, (128, width), 1) // n
    pow_id = lax.broadcasted_iota(jnp.int32, (128, width), 0)
    onehot = lambda cond: jnp.where(cond, 1.0, 0.0).astype(BF16)
    sel_fwd = onehot(pow_id == lane_tok)
    sel_rev = onehot(pow_id == t - 1 - lane_tok)
    sel_out = onehot(pow_id == t - lane_tok)
    row_tok = lax.broadcasted_iota(jnp.int32, (width, 32), 0) // n
    row_pow = lax.broadcasted_iota(jnp.int32, (width, 32), 1)
    rsel_rev = onehot(row_pow == t - 1 - row_tok)
    rsel_fwd = onehot(row_pow == row_tok)
    lane = lax.broadcasted_iota(jnp.int32, (n, width), 1)

    jc = jnp.minimum(lax.broadcasted_iota(jnp.int32, (p, 128), 1), t).astype(F32)
    jr = jnp.minimum(lax.broadcasted_iota(jnp.int32, (32, p), 0), t).astype(F32)

    strips = []
    f_parts = []
    e_parts = []
    for d in range(2):
        dt = jnp.exp(ldt[d, 0])
        mag_c = are_c[d, 0] * dt
        th_c = aim_c[d, 0] * dt
        ec = jnp.exp(jc * mag_c)
        qr = ec * jnp.cos(jc * th_c)
        qi = ec * jnp.sin(jc * th_c)
        a_re = are_r[d, 0]
        a_im = aim_r[d, 0]
        er = jnp.exp(jr * (a_re * dt))
        pr = er * jnp.cos(jr * (a_im * dt))
        pi = er * jnp.sin(jr * (a_im * dt))
        nr = pr[1:2] - 1.0
        ni = pi[1:2]
        den = a_re * a_re + a_im * a_im
        fre = (nr * a_re + ni * a_im) / den
        fim = (ni * a_re - nr * a_im) / den
        bt_re = bre_ref[d, 0].T
        bt_im = bim_ref[d, 0].T
        bb_re, bb_im = _cmul(fre, fim, bt_re, bt_im)
        ct_re = jnp.concatenate([cre_ref[d, 0].T] * t, axis=1)
        ct_im = jnp.concatenate([cim_ref[d, 0].T] * t, axis=1)
        sel = sel_fwd if d == 0 else sel_rev
        w_re, w_im = _cmul(ct_re, ct_im, _select_cols(qr, sel), _select_cols(qi, sel))
        strips.append(_dot3(bb_re, w_re) - _dot3(bb_im, w_im))
        if d == 0:
            o_re, o_im = _cmul(w_re, w_im, qr[:, 1:2], qi[:, 1:2])
        else:
            o_re, o_im = _cmul(ct_re, ct_im, _select_cols(qr, sel_out), _select_cols(qi, sel_out))
        e_parts += [o_re, -o_im]
        rsel = rsel_rev if d == 0 else rsel_fwd
        f_re, f_im = _cmul(jnp.concatenate([bb_re] * t, axis=0), jnp.concatenate([bb_im] * t, axis=0),
                           _select_rows(rsel, pr), _select_rows(rsel, pi))
        f_parts += [f_re, f_im]
        dec_ref[0, 2 * d:2 * d + 1, :] = jnp.concatenate([pr[t:t + 1], pr[t:t + 1]], axis=1)
        dec_ref[0, 2 * d + 1:2 * d + 2, :] = jnp.concatenate([-pi[t:t + 1], pi[t:t + 1]], axis=1)

    blocks = []
    for s in range(t):
        fwd = strips[0] if s == 0 else jnp.where(lane >= n * s, pltpu.roll(strips[0], n * s, axis=1), 0.0)
        back = t - 1 - s
        bwd = strips[1] if back == 0 else jnp.where(lane < width - n * back,
                                                     pltpu.roll(strips[1], width - n * back, axis=1), 0.0)
        blocks.append(fwd + bwd)
    m = jnp.concatenate(blocks, axis=0)
    w1_ref[0] = jnp.concatenate([m] + f_parts, axis=1).astype(BF16)
    e_ref[0] = jnp.concatenate(e_parts, axis=0).astype(BF16)


def _s5_operators(a_re, a_im, log_dt, b_re, b_im, c_re, c_im):
    g, p, n, t = S5_GROUPS, S5_STATE, S5_GROUP, CHUNK
    spec = lambda shape: pl.BlockSpec((2, 1) + shape, lambda i: (0, i) + (0,) * len(shape))
    return pl.pallas_call(
        _s5_prep_kernel, grid=(g,),
        in_specs=[spec((1, p)), spec((1, p)), spec((p, 1)), spec((p, 1)), spec((1, 1)),
                  spec((p, n)), spec((p, n)), spec((n, p)), spec((n, p))],
        out_specs=[pl.BlockSpec((1, t * n, 2 * t * n), lambda i: (i, 0, 0)),
                   pl.BlockSpec((1, 4 * p, t * n), lambda i: (i, 0, 0)),
                   pl.BlockSpec((1, 4, 2 * p), lambda i: (i, 0, 0))],
        out_shape=[jax.ShapeDtypeStruct((g, t * n, 2 * t * n), BF16),
                   jax.ShapeDtypeStruct((g, 4 * p, t * n), BF16),
                   jax.ShapeDtypeStruct((g, 4, 2 * p), F32)],
        compiler_params=_cparams(("parallel",)),
        name="s5_prep",
    )(a_re.reshape(2, g, 1, p), a_im.reshape(2, g, 1, p), a_re.reshape(2, g, p, 1),
      a_im.reshape(2, g, p, 1), log_dt.reshape(2, g, 1, 1), b_re, b_im, c_re, c_im)


def _block_transpose8(ps):
    ps = list(ps)
    blk = lax.broadcasted_iota(jnp.int32, ps[0].shape, 1) // S5_GROUP
    for k in range(3):
        step = 1 << k
        shift = S5_GROUP * step
        keep = ((blk >> k) & 1) == 0
        for a in range(8):
            if a & step:
                continue
            pa, pb = ps[a], ps[a + step]
            ps[a] = jnp.where(keep, pa, pltpu.roll(pb, shift, axis=1))
            ps[a + step] = jnp.where(keep, pltpu.roll(pa, 128 - shift, axis=1), pb)
    return ps


def _s5a_kernel(u_ref, w1_ref, yin_ref, gf_ref, gb_ref):
    nb = u_ref.shape[1]
    xs = []
    for s in range(CHUNK):
        parts = [u_ref[0, b, pl.ds(s, TILE_CHUNKS, stride=CHUNK), :] for b in range(nb)]
        xs.append(jnp.concatenate(parts, axis=0))
    lo = _block_transpose8(xs[:8])
    hi = _block_transpose8(xs[8:])
    for j in range(GRP_PER_BLK):
        og = jnp.concatenate([lo[j], hi[j]], axis=1)
        r = _dot(og.astype(BF16), w1_ref[j])
        yin_ref[j] = r[:, 0:256]
        gf_ref[j] = r[:, 256:384]
        gb_ref[j] = r[:, 384:512]


def _s5a_call(u, w1):
    n_blk, b, s, _ = u.shape
    nt = s // TILE_TOK
    rows = b * TILE_CHUNKS
    out_spec = lambda n: pl.BlockSpec((GRP_PER_BLK, rows, n), lambda l, j: (l, j, 0))
    return pl.pallas_call(
        _s5a_kernel, grid=(n_blk, nt),
        in_specs=[pl.BlockSpec((1, b, TILE_TOK, LANE_BLK), lambda l, j: (l, 0, j, 0)),
                  pl.BlockSpec((GRP_PER_BLK, 256, 512), lambda l, j: (l, 0, 0))],
        out_specs=[out_spec(256), out_spec(128), out_spec(128)],
        out_shape=[jax.ShapeDtypeStruct((S5_GROUPS, nt * rows, 256), F32),
                   jax.ShapeDtypeStruct((S5_GROUPS, nt * rows, 128), F32),
                   jax.ShapeDtypeStruct((S5_GROUPS, nt * rows, 128), F32)],
        compiler_params=_cparams(("parallel", "parallel")),
        name="s5_chunk_in",
    )(u, w1)


def _s5b_kernel(nb, a_ref, gfc_ref, gfl_ref, gbc_ref, gbl_ref, hfc_ref, hfl_ref, hbc_ref, hbl_ref):
    gb = a_ref.shape[0]
    rows = nb * TILE_CHUNKS
    n_lat = gfl_ref.shape[1] // rows
    a1f = [jnp.broadcast_to(a_ref[g, 0:1, :], (nb, 128)) for g in range(gb)]
    a2f = [jnp.broadcast_to(a_ref[g, 1:2, :], (nb, 128)) for g in range(gb)]
    a1b = [jnp.broadcast_to(a_ref[g, 2:3, :], (nb, 128)) for g in range(gb)]
    a2b = [jnp.broadcast_to(a_ref[g, 3:4, :], (nb, 128)) for g in range(gb)]

    def step(state, a1, a2, g_ref, h_ref, g, row):
        h, hs = state
        h_ref[g, row, :] = h
        inp = g_ref[g, row, :]
        return a1 * h + a2 * hs + inp, a1 * hs - a2 * h + pltpu.roll(inp, 64, axis=1)

    zero = jnp.zeros((nb, 128), F32)
    hf = [(zero, zero) for _ in range(gb)]
    hb = [(zero, zero) for _ in range(gb)]
    for ci in range(TILE_CHUNKS):
        rf = pl.ds(ci, nb, stride=TILE_CHUNKS)
        rb = pl.ds(TILE_CHUNKS - 1 - ci, nb, stride=TILE_CHUNKS)
        for g in range(gb):
            hf[g] = step(hf[g], a1f[g], a2f[g], gfc_ref, hfc_ref, g, rf)
            hb[g] = step(hb[g], a1b[g], a2b[g], gbc_ref, hbc_ref, g, rb)

    def body(j, carry):
        hf, hb = carry
        hf = list(hf)
        hb = list(hb)
        base_f = j * rows
        base_b = (n_lat - 1 - j) * rows
        for ci in range(TILE_CHUNKS):
            rf = pl.ds(base_f + ci, nb, stride=TILE_CHUNKS)
            rb = pl.ds(base_b + (TILE_CHUNKS - 1 - ci), nb, stride=TILE_CHUNKS)
            for g in range(gb):
                hf[g] = step(hf[g], a1f[g], a2f[g], gfl_ref, hfl_ref, g, rf)
                hb[g] = step(hb[g], a1b[g], a2b[g], gbl_ref, hbl_ref, g, rb)
        return tuple(hf), tuple(hb)

    lax.fori_loop(0, n_lat, body, (tuple(hf), tuple(hb)))


def _s5b_call(decay, gf_c, gf_l, gb_c, gb_l, nb):
    gblk = 4
    spec = lambda a: pl.BlockSpec((gblk, a.shape[1], 128), lambda i: (i, 0, 0))
    sds = lambda a: jax.ShapeDtypeStruct(a.shape, F32)
    return pl.pallas_call(
        functools.partial(_s5b_kernel, nb), grid=(S5_GROUPS // gblk,),
        in_specs=[pl.BlockSpec((gblk, 4, 128), lambda i: (i, 0, 0)),
                  spec(gf_c), spec(gf_l), spec(gb_c), spec(gb_l)],
        out_specs=[spec(gf_c), spec(gf_l), spec(gb_c), spec(gb_l)],
        out_shape=[sds(gf_c), sds(gf_l), sds(gb_c), sds(gb_l)],
        compiler_params=_cparams(("parallel",)),
        name="s5_state_scan",
    )(decay, gf_c, gf_l, gb_c, gb_l)


def _s5c_kernel(yin_ref, hf_ref, hb_ref, e_ref, u_ref, d_ref, y_ref):
    nb = u_ref.shape[1]
    ys = []
    for j in range(GRP_PER_BLK):
        h = jnp.concatenate([hf_ref[j], hb_ref[j]], axis=1).astype(BF16)
        ys.append(yin_ref[j] + _dot(h, e_ref[j]))
    at = (_block_transpose8([y[:, :128] for y in ys])
          + _block_transpose8([y[:, 128:] for y in ys]))
    d = d_ref[...]
    for t in range(CHUNK):
        for b in range(nb):
            rows = pl.ds(t, TILE_CHUNKS, stride=CHUNK)
            y_ref[0, b, rows, :] = at[t][b * TILE_CHUNKS:(b + 1) * TILE_CHUNKS] + d * u_ref[0, b, rows, :]


def _s5c_call(yin, hf, hb, e, u, d_skip):
    n_blk, b, s, _ = u.shape
    nt = s // TILE_TOK
    rows = b * TILE_CHUNKS
    gspec = lambda n: pl.BlockSpec((GRP_PER_BLK, rows, n), lambda l, j: (l, j, 0))
    tok_spec = pl.BlockSpec((1, b, TILE_TOK, LANE_BLK), lambda l, j: (l, 0, j, 0))
    return pl.pallas_call(
        _s5c_kernel, grid=(n_blk, nt),
        in_specs=[gspec(256), gspec(128), gspec(128),
                  pl.BlockSpec((GRP_PER_BLK, 256, 256), lambda l, j: (l, 0, 0)),
                  tok_spec,
                  pl.BlockSpec((1, LANE_BLK), lambda l, j: (0, l))],
        out_specs=tok_spec,
        out_shape=_lane_block_shape(b, s),
        compiler_params=_cparams(("parallel", "parallel")),
        name="s5_chunk_out",
    )(yin, hf, hb, e, u, d_skip.reshape(1, D_S5))


def _gelu_tanh(x):
    return 0.5 * x * (1.0 + jnp.tanh(math.sqrt(2.0 / math.pi) * (x + 0.044715 * (x * x * x))))


def _route(logits):
    lane = lax.broadcasted_iota(jnp.int32, logits.shape, 1).astype(F32)
    neg = jnp.float32(-1e30)
    big = jnp.float32(1e9)
    gl = jnp.where(lane < N_GROUPS, logits, neg)
    gmax = jnp.max(gl, axis=1, keepdims=True)
    gidx = jnp.min(jnp.where(gl == gmax, lane, big), axis=1, keepdims=True)
    gsum = jnp.sum(jnp.exp(gl - gmax), axis=1, keepdims=True)
    gw = 1.0 / gsum
    lo = N_GROUPS + EXP_PER_GROUP * gidx
    el = jnp.where((lane >= lo) & (lane < lo + EXP_PER_GROUP), logits, neg)
    v1 = jnp.max(el, axis=1, keepdims=True)
    i1 = jnp.min(jnp.where(el == v1, lane, big), axis=1, keepdims=True)
    el2 = jnp.where(lane == i1, neg, el)
    v2 = jnp.max(el2, axis=1, keepdims=True)
    i2 = jnp.min(jnp.where(el2 == v2, lane, big), axis=1, keepdims=True)
    ex = jnp.exp(v2 - v1)
    p1 = 1.0 / (1.0 + ex)
    p2 = ex * p1
    e1 = i1 - lo
    e2 = i2 - lo
    first = e1 < e2
    ea = jnp.where(first, e1, e2)
    eb = jnp.where(first, e2, e1)
    wa = gw * jnp.where(first, p1, p2)
    wb = gw * jnp.where(first, p2, p1)
    pair = ea * (7.0 - ea) * 0.5 + (eb - ea - 1.0)
    return wa, wb, 6.0 * gidx + pair


def _out_kernel(ypre_ref, ysc_ref, ycf_ref, x_ref, mod_ref, wglu_ref, bglu_ref, wo_ref,
                lng_ref, lnb_ref, wr_ref, br_ref, x1_ref, hx_ref, meta_ref, counts_ref, cnt_ref):
    @pl.when((pl.program_id(0) == 0) & (pl.program_id(1) == 0))
    def _():
        cnt_ref[...] = jnp.zeros_like(cnt_ref)

    ypre = jnp.concatenate([ypre_ref[blk, 0] for blk in range(ypre_ref.shape[0])], axis=1)
    t = _gelu_tanh(ypre)
    gate = _sigmoid(_dot(t.astype(BF16), wglu_ref[...]) + bglu_ref[...])
    ys5 = (t * gate).astype(BF16)
    y = (_dot(ys5, wo_ref[0:D_S5, :]) + _dot(ysc_ref[0], wo_ref[D_S5:D_S5 + D_SC, :])
         + _dot(ycf_ref[0], wo_ref[D_S5 + D_SC:D_MODEL, :]))
    g1 = mod_ref[0, 2:3, :]
    x1 = _layer_norm(DN_ALPHA * x_ref[0] + g1 * y, lng_ref[...], lnb_ref[...])
    x1_ref[0] = x1
    h2 = x1 * (1.0 + mod_ref[0, 4:5, :]) + mod_ref[0, 3:4, :]
    wa, wb, cls = _route(_dot(h2.astype(BF16), wr_ref[...]) + br_ref[...])

    tm = h2.shape[0]
    lane = lax.broadcasted_iota(jnp.int32, (tm, ROUTER_LANES), 1).astype(F32)
    onehot = jnp.where(lane == cls, 1.0, 0.0)
    row_i = lax.broadcasted_iota(jnp.int32, (tm, tm), 0)
    col_i = lax.broadcasted_iota(jnp.int32, (tm, tm), 1)
    earlier = jnp.where(col_i < row_i, 1.0, 0.0).astype(BF16)
    before = _dot(earlier, onehot.astype(BF16)) + cnt_ref[...]
    rank = jnp.sum(before * onehot, axis=1, keepdims=True)
    cnt_ref[...] += jnp.sum(onehot, axis=0, keepdims=True)
    counts_ref[...] = cnt_ref[...]

    meta = (jnp.where(lane == META_WA, wa, 0.0) + jnp.where(lane == META_WB, wb, 0.0)
            + jnp.where(lane == META_CLS, cls, 0.0) + jnp.where(lane == META_RANK, rank, 0.0))
    meta_ref[0] = meta
    hx_ref[0, :, 0:D_MODEL] = h2
    hx_ref[0, :, D_MODEL:HX_LANES] = meta


def _out_call(ypre, ysc, ycf, x, mod, wglu_bf, b_glu, wo_bf, ln_g, ln_b, w_router, b_router, tm):
    b, s, d = x.shape
    row_spec = lambda n: pl.BlockSpec((1, tm, n), lambda i, j: (i, j, 0))
    full = lambda shape: pl.BlockSpec(shape, lambda i, j: (0,) * len(shape))
    return pl.pallas_call(
        _out_kernel, grid=(b, s // tm),
        in_specs=[_lane_block_spec(tm), row_spec(D_SC), row_spec(D_CF), row_spec(d),
                  pl.BlockSpec((1, 6, d), lambda i, j: (i, 0, 0)),
                  full((D_S5, D_S5)), full((1, D_S5)), full((d, d)),
                  full((1, d)), full((1, d)), full((d, ROUTER_LANES)), full((1, ROUTER_LANES))],
        out_specs=[row_spec(d), row_spec(HX_LANES), row_spec(ROUTER_LANES), full((1, ROUTER_LANES))],
        out_shape=[jax.ShapeDtypeStruct((b, s, d), F32), jax.ShapeDtypeStruct((b, s, HX_LANES), F32),
                   jax.ShapeDtypeStruct((b, s, ROUTER_LANES), F32),
                   jax.ShapeDtypeStruct((1, ROUTER_LANES), F32)],
        scratch_shapes=[pltpu.VMEM((1, ROUTER_LANES), F32)],
        compiler_params=_cparams(("arbitrary", "arbitrary")),
        name="out_proj",
    )(ypre, ysc, ycf, x, mod, wglu_bf, b_glu.reshape(1, -1), wo_bf, ln_g.reshape(1, -1),
      ln_b.reshape(1, -1), w_router, b_router)


def _moe_plan(meta, counts, n_tok):
    cls = meta[..., META_CLS].reshape(-1).astype(jnp.int32)
    rank = meta[..., META_RANK].reshape(-1).astype(jnp.int32)
    cnt = counts[0, :N_CLASSES].astype(jnp.int32)
    n_tiles = (cnt + (MOE_TM - 1)) // MOE_TM
    ends = jnp.cumsum(n_tiles)
    starts = ends - n_tiles
    slot = starts[cls] * MOE_TM + rank
    t_max = n_tok // MOE_TM + N_CLASSES
    n_used = ends[N_CLASSES - 1]
    tile = jnp.minimum(jnp.arange(t_max, dtype=jnp.int32), n_used - 1)
    tile_cls = jnp.sum((tile[:, None] >= ends[None, :]).astype(jnp.int32), axis=1)
    group = tile_cls // 6
    pair = tile_cls % 6
    first = jnp.array([0, 0, 0, 1, 1, 2], jnp.int32)[pair] + EXP_PER_GROUP * group
    second = jnp.array([1, 2, 3, 2, 3, 3], jnp.int32)[pair] + EXP_PER_GROUP * group
    return slot, tile, first, second, n_used.reshape(1)


def _dispatch_kernel(slot_ref, hx_ref, xs_init_ref, xs_ref, sem):
    del xs_init_ref
    n_oct = hx_ref.shape[1]
    base = (pl.program_id(0) * pl.num_programs(1) + pl.program_id(1)) * (n_oct * SUBLANES)

    def body(i, carry):
        for k in range(SUBLANES):
            oct_id, sub = _split_row(slot_ref[base + i * SUBLANES + k])
            pltpu.make_async_copy(hx_ref.at[0, i, pl.ds(k, 1), :], xs_ref.at[oct_id, pl.ds(sub, 1), :], sem).start()
        return carry

    lax.fori_loop(0, n_oct, body, 0)
    pltpu.make_async_copy(hx_ref.at[0], xs_ref.at[pl.ds(0, n_oct)], sem).wait()


def _split_row(row):
    return lax.shift_right_logical(row, 3), lax.bitwise_and(row, SUBLANES - 1)


def _dispatch_call(slot, hx, n_rows, tm):
    b, s, w = hx.shape
    xs_init = jnp.zeros((n_rows // SUBLANES, SUBLANES, w), F32)
    grid_spec = pltpu.PrefetchScalarGridSpec(
        num_scalar_prefetch=1, grid=(b, s // tm),
        in_specs=[pl.BlockSpec((1, tm // SUBLANES, SUBLANES, w), lambda i, j, slot: (i, j, 0, 0)),
                  pl.BlockSpec(memory_space=pl.ANY)],
        out_specs=pl.BlockSpec(memory_space=pl.ANY),
        scratch_shapes=[pltpu.SemaphoreType.DMA(())])
    xs = pl.pallas_call(
        _dispatch_kernel, grid_spec=grid_spec,
        out_shape=jax.ShapeDtypeStruct(xs_init.shape, F32),
        input_output_aliases={2: 0},
        compiler_params=_cparams(("arbitrary", "arbitrary")),
        name="moe_dispatch",
    )(slot, hx.reshape(b, s // SUBLANES, SUBLANES, w), xs_init)
    return xs.reshape(n_rows, w)


def _moe_kernel(tile_ref, first_ref, second_ref, nused_ref, xs_ref, wga_ref, wgb_ref, wua_ref, wub_ref,
                wda_ref, wdb_ref, ys_ref):
    del tile_ref, first_ref, second_ref

    @pl.when(pl.program_id(0) < nused_ref[0])
    def _():
        x = xs_ref[...]
        xb = x[:, 0:D_MODEL].astype(BF16)

        def expert(wg_ref, wu_ref, wd_ref, w):
            gate = _dot(xb, wg_ref[0])
            up = _dot(xb, wu_ref[0])
            act = gate * _sigmoid(gate) * up * w
            return _dot(act.astype(BF16), wd_ref[0])

        wa = x[:, D_MODEL + META_WA:D_MODEL + META_WA + 1]
        wb = x[:, D_MODEL + META_WB:D_MODEL + META_WB + 1]
        ys_ref[...] = expert(wga_ref, wua_ref, wda_ref, wa) + expert(wgb_ref, wub_ref, wdb_ref, wb)

    @pl.when(pl.program_id(0) >= nused_ref[0])
    def _():
        ys_ref[...] = jnp.zeros_like(ys_ref)


def _moe_call(tile, first, second, n_used, xs, wg_bf, wu_bf, wd_bf):
    n_rows, w = xs.shape
    d = D_MODEL
    t_max = tile.shape[0]
    up_spec = lambda sel: pl.BlockSpec((1, d, D_EXPERT), lambda t, tl, fi, se, nu: ((fi, se)[sel][t], 0, 0))
    down_spec = lambda sel: pl.BlockSpec((1, D_EXPERT, d), lambda t, tl, fi, se, nu: ((fi, se)[sel][t], 0, 0))
    grid_spec = pltpu.PrefetchScalarGridSpec(
        num_scalar_prefetch=4, grid=(t_max,),
        in_specs=[pl.BlockSpec((MOE_TM, w), lambda t, tl, fi, se, nu: (tl[t], 0)),
                  up_spec(0), up_spec(1), up_spec(0), up_spec(1), down_spec(0), down_spec(1)],
        out_specs=pl.BlockSpec((MOE_TM, d), lambda t, tl, fi, se, nu: (t, 0)))
    return pl.pallas_call(
        _moe_kernel, grid_spec=grid_spec,
        out_shape=jax.ShapeDtypeStruct((n_rows, d), F32),
        compiler_params=_cparams(("arbitrary",)),
        name="moe_experts",
    )(tile, first, second, n_used, xs, wg_bf, wg_bf, wu_bf, wu_bf, wd_bf, wd_bf)


def _combine_kernel(slot_ref, x1_ref, mod_ref, lng_ref, lnb_ref, ys_ref, o_ref, f_ref, sem):
    n_oct = f_ref.shape[1]
    tm = n_oct * SUBLANES
    step = pl.program_id(0) * pl.num_programs(1) + pl.program_id(1)
    n_steps = pl.num_programs(0) * pl.num_programs(1)

    def request(which, buf):
        base = which * tm

        def body(i, carry):
            for k in range(SUBLANES):
                oct_id, sub = _split_row(slot_ref[base + i * SUBLANES + k])
                pltpu.make_async_copy(ys_ref.at[oct_id, pl.ds(sub, 1), :], f_ref.at[buf, i, pl.ds(k, 1), :],
                                      sem.at[buf]).start()
            return carry

        lax.fori_loop(0, n_oct, body, 0)

    @pl.when(step == 0)
    def _():
        request(0, 0)

    @pl.when(step + 1 < n_steps)
    def _():
        request(step + 1, (step + 1) % 2)

    buf = step % 2
    pltpu.make_async_copy(ys_ref.at[pl.ds(0, n_oct)], f_ref.at[buf], sem.at[buf]).wait()
    f = f_ref[buf].reshape(tm, f_ref.shape[3])
    g2 = mod_ref[0, 5:6, :]
    o_ref[0] = _layer_norm(DN_ALPHA * x1_ref[0] + g2 * f, lng_ref[...], lnb_ref[...])


def _combine_call(slot, x1, mod, ln_g, ln_b, ys, tm):
    b, s, d = x1.shape
    ys = ys.reshape(ys.shape[0] // SUBLANES, SUBLANES, d)
    grid_spec = pltpu.PrefetchScalarGridSpec(
        num_scalar_prefetch=1, grid=(b, s // tm),
        in_specs=[pl.BlockSpec((1, tm, d), lambda i, j, slot: (i, j, 0)),
                  pl.BlockSpec((1, 6, d), lambda i, j, slot: (i, 0, 0)),
                  pl.BlockSpec((1, d), lambda i, j, slot: (0, 0)),
                  pl.BlockSpec((1, d), lambda i, j, slot: (0, 0)),
                  pl.BlockSpec(memory_space=pl.ANY)],
        out_specs=pl.BlockSpec((1, tm, d), lambda i, j, slot: (i, j, 0)),
        scratch_shapes=[pltpu.VMEM((2, tm // SUBLANES, SUBLANES, d), F32), pltpu.SemaphoreType.DMA((2,))])
    return pl.pallas_call(
        _combine_kernel, grid_spec=grid_spec,
        out_shape=jax.ShapeDtypeStruct((b, s, d), F32),
        compiler_params=_cparams(("arbitrary", "arbitrary")),
        name="moe_combine",
    )(slot, x1, mod, ln_g.reshape(1, -1), ln_b.reshape(1, -1), ys)


def _moe_sublayer(hx, meta, counts, x1, mod, wg_bf, wu_bf, wd_bf, ln_g, ln_b, tm):
    b, s, _ = x1.shape
    n_tok = b * s
    slot, tile, first, second, n_used = _moe_plan(meta, counts, n_tok)
    n_rows = n_tok + N_CLASSES * MOE_TM
    xs = _dispatch_call(slot, hx, n_rows, tm)
    ys = _moe_call(tile, first, second, n_used, xs, wg_bf, wu_bf, wd_bf)
    return _combine_call(slot, x1, mod, ln_g, ln_b, ys, tm)


def kernel(x, c, ctx, c_ctx, w_mod, b_mod, w_in, s5_a_re, s5_a_im, s5_log_dt, s5_b_re, s5_b_im, s5_c_re, s5_c_im, s5_d, w_glu, b_glu, w_sc, w_dw, b_dw, ln_cf_g, ln_cf_b, w_o, ln1_g, ln1_b, w_rg, b_rg, w_rexp, b_rexp, w_gate, w_up, w_down, ln2_g, ln2_b):
    nb, seq, d = x.shape
    n_ctx = ctx.shape[1]
    n_layers = w_mod.shape[0]
    assert seq % TILE_TOK == 0 and n_ctx % TILE_TOK == 0 and seq % GRID_W == 0

    mod_rows = 16
    assert nb + 1 <= mod_rows
    c_all = jnp.concatenate([c, c_ctx[None, :], jnp.zeros((mod_rows - nb - 1, d), F32)], axis=0)
    mod_all = _mod_call(c_all, w_mod, b_mod)

    pad_r = ROUTER_LANES - N_GROUPS - N_EXPERTS
    x_lat, x_ctx = x, ctx
    for l in range(n_layers):
        last = l == n_layers - 1
        mod_lat = mod_all[l, :nb].reshape(nb, 6, d)
        mod_ctx = jnp.broadcast_to(mod_all[l, nb].reshape(1, 6, d), (nb, 6, d))
        w_in_bf = w_in[l].astype(BF16)
        wglu_bf = w_glu[l].astype(BF16)
        wo_bf = w_o[l].astype(BF16)
        wg_bf = w_gate[l].astype(BF16)
        wu_bf = w_up[l].astype(BF16)
        wd_bf = w_down[l].astype(BF16)
        w_router = jnp.concatenate([w_rg[l], w_rexp[l], jnp.zeros((d, pad_r), F32)], axis=1).astype(BF16)
        b_router = jnp.concatenate([b_rg[l], b_rexp[l], jnp.zeros((pad_r,), F32)]).reshape(1, -1)
        w1, e_op, decay = _s5_operators(s5_a_re[l], s5_a_im[l], s5_log_dt[l], s5_b_re[l], s5_b_im[l],
                                        s5_c_re[l], s5_c_im[l])

        u_l, bg_l, cv_l, gl_l = _in_call(x_lat, mod_lat, w_in_bf, 512, False)
        if last:
            u_c = _in_call(x_ctx, mod_ctx, w_in_bf[:, :D_S5], TILE_TOK, True)
        else:
            u_c, bg_c, cv_c, gl_c = _in_call(x_ctx, mod_ctx, w_in_bf, TILE_TOK, False)

        yin_l, gf_l, gb_l = _s5a_call(u_l, w1)
        yin_c, gf_c, gb_c = _s5a_call(u_c, w1)
        hf_c, hf_l, hb_c, hb_l = _s5b_call(decay, gf_c, gf_l, gb_c, gb_l, nb)
        ypre_l = _s5c_call(yin_l, hf_l, hb_l, e_op, u_l, s5_d[l])

        ysc_l, ycf_l = _conv_call(bg_l, cv_l, gl_l, w_sc[l], w_dw[l], b_dw[l], ln_cf_g[l], ln_cf_b[l], True)
        x1_l, hx_l, meta_l, cnt_l = _out_call(ypre_l, ysc_l, ycf_l, x_lat, mod_lat, wglu_bf, b_glu[l], wo_bf,
                                              ln1_g[l], ln1_b[l], w_router, b_router, 512)
        if not last:
            ypre_c = _s5c_call(yin_c, hf_c, hb_c, e_op, u_c, s5_d[l])
            ysc_c, ycf_c = _conv_call(bg_c, cv_c, gl_c, w_sc[l], w_dw[l], b_dw[l], ln_cf_g[l], ln_cf_b[l], False)
            x1_c, hx_c, meta_c, cnt_c = _out_call(ypre_c, ysc_c, ycf_c, x_ctx, mod_ctx, wglu_bf, b_glu[l], wo_bf,
                                                  ln1_g[l], ln1_b[l], w_router, b_router, TILE_TOK)
            x_ctx = _moe_sublayer(hx_c, meta_c, cnt_c, x1_c, mod_ctx, wg_bf, wu_bf, wd_bf,
                                  ln2_g[l], ln2_b[l], TILE_TOK)
        x_lat = _moe_sublayer(hx_l, meta_l, cnt_l, x1_l, mod_lat, wg_bf, wu_bf, wd_bf,
                              ln2_g[l], ln2_b[l], 512)
    return x_lat
```

```python
import functools
import math

import jax
import jax.numpy as jnp
from jax import lax
from jax.experimental import pallas as pl
from jax.experimental.pallas import tpu as pltpu

F32 = jnp.float32
BF16 = jnp.bfloat16

D_MODEL = 1024
DEPTH = 2
GRID_W = 64
D_S5 = 512
S5_GROUP = 16
S5_GROUPS = 32
S5_STATE = 64
D_SC = 256
D_CF = 256
CF_WIDTH = 31
CF_HALF = 15
D_IN = 1792
N_GROUPS = 4
EXP_PER_GROUP = 4
N_EXPERTS = 16
D_EXPERT = 256
DN_ALPHA = (2 * DEPTH) ** 0.25
LN_EPS = 1e-5

CHUNK = 16
TILE_CHUNKS = 16
TILE_TOK = CHUNK * TILE_CHUNKS
LANE_BLK = 128
GRP_PER_BLK = LANE_BLK // S5_GROUP
ROUTER_LANES = 128
HX_LANES = D_MODEL + ROUTER_LANES
META_WA, META_WB, META_CLS, META_RANK = 0, 1, 2, 3
N_CLASSES = N_GROUPS * 6
MOE_TM = 256
SUBLANES = 8
VMEM_LIMIT = 56 * 1024 * 1024


def _cparams(sem):
    return pltpu.CompilerParams(dimension_semantics=sem, vmem_limit_bytes=VMEM_LIMIT)


def _split_bf16(a):
    hi = a.astype(BF16)
    lo = (a - hi.astype(F32)).astype(BF16)
    return hi, lo


def _dot(a, b):
    return jnp.dot(a, b, preferred_element_type=F32)


def _dot3(a, b):
    ah, al = _split_bf16(a)
    bh, bl = _split_bf16(b)
    return _dot(ah, bh) + (_dot(al, bh) + _dot(ah, bl))


def _sigmoid(x):
    return 1.0 / (1.0 + jnp.exp(-x))


def _layer_norm(x, g, b):
    mu = jnp.mean(x, axis=-1, keepdims=True)
    xc = x - mu
    var = jnp.mean(xc * xc, axis=-1, keepdims=True)
    return xc * lax.rsqrt(var + LN_EPS) * g + b


def _mod_kernel(c_ref, w_ref, b_ref, o_ref):
    c = c_ref[...]
    s = c * _sigmoid(c)
    o_ref[0] = _dot3(s, w_ref[0]) + b_ref[0]


def _mod_call(c_all, w_mod, b_mod):
    n_layers, d, n_out = w_mod.shape
    tn = 1536
    rows = c_all.shape[0]
    return pl.pallas_call(
        _mod_kernel,
        grid=(n_layers, n_out // tn),
        in_specs=[
            pl.BlockSpec((rows, d), lambda l, j: (0, 0)),
            pl.BlockSpec((1, d, tn), lambda l, j: (l, 0, j)),
            pl.BlockSpec((1, 1, tn), lambda l, j: (l, 0, j)),
        ],
        out_specs=pl.BlockSpec((1, rows, tn), lambda l, j: (l, 0, j)),
        out_shape=jax.ShapeDtypeStruct((n_layers, rows, n_out), F32),
        compiler_params=_cparams(("parallel", "parallel")),
        name="mod",
    )(c_all, w_mod, b_mod.reshape(n_layers, 1, n_out))


def _in_kernel(x_ref, mod_ref, w_ref, u_ref, bg_ref, cv_ref, gl_ref):
    x = x_ref[0]
    sh = mod_ref[0, 0:1, :]
    sc = mod_ref[0, 1:2, :]
    h = (x * (1.0 + sc) + sh).astype(BF16)
    z = _dot(h, w_ref[...])
    _store_lane_blocks(u_ref, z[:, 0:512])
    bg_ref[0] = z[:, 512:768]
    cv_ref[0] = z[:, 768:1024] * z[:, 1024:1280]
    gl_ref[0] = z[:, 1280:1536] * _sigmoid(z[:, 1536:1792])


def _in_u_kernel(x_ref, mod_ref, w_ref, u_ref):
    x = x_ref[0]
    sh = mod_ref[0, 0:1, :]
    sc = mod_ref[0, 1:2, :]
    h = (x * (1.0 + sc) + sh).astype(BF16)
    _store_lane_blocks(u_ref, _dot(h, w_ref[...]))


def _store_lane_blocks(ref, val):
    for blk in range(ref.shape[0]):
        ref[blk, 0] = val[:, blk * LANE_BLK:(blk + 1) * LANE_BLK]


def _lane_block_spec(tm):
    return pl.BlockSpec((D_S5 // LANE_BLK, 1, tm, LANE_BLK), lambda i, j: (0, i, j, 0))


def _lane_block_shape(b, s):
    return jax.ShapeDtypeStruct((D_S5 // LANE_BLK, b, s, LANE_BLK), F32)


def _in_call(x, mod, w_in_bf, tm, u_only):
    b, s, d = x.shape
    grid = (b, s // tm)
    row_spec = lambda n: pl.BlockSpec((1, tm, n), lambda i, j: (i, j, 0))
    in_specs = [
        row_spec(d),
        pl.BlockSpec((1, 6, d), lambda i, j: (i, 0, 0)),
    ]
    if u_only:
        in_specs.append(pl.BlockSpec((d, D_S5), lambda i, j: (0, 0)))
        return pl.pallas_call(
            _in_u_kernel, grid=grid, in_specs=in_specs,
            out_specs=_lane_block_spec(tm),
            out_shape=_lane_block_shape(b, s),
            compiler_params=_cparams(("parallel", "parallel")),
            name="in_proj_u",
        )(x, mod, w_in_bf)
    in_specs.append(pl.BlockSpec((d, D_IN), lambda i, j: (0, 0)))
    return pl.pallas_call(
        _in_kernel, grid=grid, in_specs=in_specs,
        out_specs=[_lane_block_spec(tm), row_spec(D_SC), row_spec(D_SC), row_spec(D_CF)],
        out_shape=[_lane_block_shape(b, s),
                   jax.ShapeDtypeStruct((b, s, D_SC), F32),
                   jax.ShapeDtypeStruct((b, s, D_SC), F32),
                   jax.ShapeDtypeStruct((b, s, D_CF), F32)],
        compiler_params=_cparams(("parallel", "parallel")),
        name="in_proj",
    )(x, mod, w_in_bf)


def _conv_tail(t, bdw_ref, lng_ref, lnb_ref):
    t = t + bdw_ref[...]
    t = _layer_norm(t, lng_ref[...], lnb_ref[...])
    return t * _sigmoid(t)


def _conv_grid_kernel(bg_ref, cv_ref, gl_ref, wsc_ref, wdw_ref, bdw_ref, lng_ref, lnb_ref,
                      ysc_ref, ycf_ref, pad_ref, t_ref):
    s = cv_ref.shape[1]
    rows = s // GRID_W
    cv = cv_ref[0]
    col = lax.broadcasted_iota(jnp.int32, (s, D_SC), 0) % GRID_W
    prev = jnp.where(col == 0, 0.0, pltpu.roll(cv, 1, axis=0))
    nxt = jnp.where(col == GRID_W - 1, 0.0, pltpu.roll(cv, s - 1, axis=0))
    conv = prev * wsc_ref[0:1, :] + cv * wsc_ref[1:2, :] + nxt * wsc_ref[2:3, :]
    ysc_ref[0] = (bg_ref[0] * conv).astype(ysc_ref.dtype)

    zero = jnp.zeros((CF_HALF, GRID_W, D_CF), F32)
    pad_ref[0:CF_HALF] = zero
    pad_ref[CF_HALF + rows:CF_HALF + rows + CF_HALF] = zero
    pad_ref[CF_HALF:CF_HALF + rows] = gl_ref[0].reshape(rows, GRID_W, D_CF)

    def body(i, carry):
        w0 = pl.multiple_of(i * 8, 8)
        for half in range(D_CF // 128):
            lanes = slice(half * 128, (half + 1) * 128)
            acc = jnp.zeros((rows, 8, 128), F32)
            for k in range(CF_WIDTH):
                acc = acc + pad_ref[k:k + rows, pl.ds(w0, 8), lanes] * wdw_ref[k:k + 1, lanes]
            t_ref[:, pl.ds(w0, 8), lanes] = acc
        return carry

    lax.fori_loop(0, GRID_W // 8, body, 0)
    t = t_ref[...].reshape(s, D_CF)
    ycf_ref[0] = _conv_tail(t, bdw_ref, lng_ref, lnb_ref).astype(ycf_ref.dtype)


def _conv_seq_kernel(bg_ref, cv_ref, gl_ref, wsc_ref, wdw_ref, bdw_ref, lng_ref, lnb_ref,
                     ysc_ref, ycf_ref, pad_ref):
    s = cv_ref.shape[1]
    cv = cv_ref[0]
    pos = lax.broadcasted_iota(jnp.int32, (s, D_SC), 0)
    prev = jnp.where(pos == 0, 0.0, pltpu.roll(cv, 1, axis=0))
    nxt = jnp.where(pos == s - 1, 0.0, pltpu.roll(cv, s - 1, axis=0))
    conv = prev * wsc_ref[0:1, :] + cv * wsc_ref[1:2, :] + nxt * wsc_ref[2:3, :]
    ysc_ref[0] = (bg_ref[0] * conv).astype(ysc_ref.dtype)

    off = 16
    pad_ref[0:off] = jnp.zeros((off, D_CF), F32)
    pad_ref[off + s:off + s + 16] = jnp.zeros((16, D_CF), F32)
    pad_ref[off:off + s] = gl_ref[0]
    acc = jnp.zeros((s, D_CF), F32)
    for k in range(CF_WIDTH):
        acc = acc + pad_ref[pl.ds(off - CF_HALF + k, s), :] * wdw_ref[k:k + 1, :]
    ycf_ref[0] = _conv_tail(acc, bdw_ref, lng_ref, lnb_ref).astype(ycf_ref.dtype)


def _conv_call(bg, cv, gl, w_sc, w_dw, b_dw, ln_g, ln_b, grid_mode):
    b, s, _ = bg.shape
    row_spec = pl.BlockSpec((1, s, D_SC), lambda i: (i, 0, 0))
    full = lambda shape: pl.BlockSpec(shape, lambda i: (0,) * len(shape))
    if grid_mode:
        rows = s // GRID_W
        kern = _conv_grid_kernel
        scratch = [pltpu.VMEM((rows + 2 * CF_HALF, GRID_W, D_CF), F32),
                   pltpu.VMEM((rows, GRID_W, D_CF), F32)]
        name = "conv_grid"
    else:
        kern = _conv_seq_kernel
        scratch = [pltpu.VMEM((s + 32, D_CF), F32)]
        name = "conv_seq"
    return pl.pallas_call(
        kern, grid=(b,),
        in_specs=[row_spec, row_spec, row_spec, full((3, D_SC)), full((CF_WIDTH, D_CF)),
                  full((1, D_CF)), full((1, D_CF)), full((1, D_CF))],
        out_specs=[row_spec, row_spec],
        out_shape=[jax.ShapeDtypeStruct((b, s, D_SC), BF16), jax.ShapeDtypeStruct((b, s, D_CF), BF16)],
        scratch_shapes=scratch,
        compiler_params=_cparams(("parallel",)),
        name=name,
    )(bg, cv, gl, w_sc, w_dw, b_dw.reshape(1, -1), ln_g.reshape(1, -1), ln_b.reshape(1, -1))


def _split3(a):
    hi = a.astype(BF16)
    r = a - hi.astype(F32)
    mid = r.astype(BF16)
    lo = (r - mid.astype(F32)).astype(BF16)
    return hi, mid, lo


def _select_cols(a, sel):
    hi, mid, lo = _split3(a)
    return _dot(hi, sel) + (_dot(mid, sel) + _dot(lo, sel))


def _select_rows(sel, a):
    hi, mid, lo = _split3(a)
    return _dot(sel, hi) + (_dot(sel, mid) + _dot(sel, lo))


def _cmul(ar, ai, br, bi):
    return ar * br - ai * bi, ar * bi + ai * br


def _s5_prep_kernel(are_r, aim_r, are_c, aim_c, ldt, bre_ref, bim_ref, cre_ref, cim_ref,
                    w1_ref, e_ref, dec_ref):
    t, n, p = CHUNK, S5_GROUP, S5_STATE
    width = t * n
    lane_tok = lax.broadcasted_iota(jnp.int32, (128, width), 1) // n
    pow_id = lax.broadcasted_iota(jnp.int32, (128, width), 0)
    onehot = lambda cond: jnp.where(cond, 1.0, 0.0).astype(BF16)
    sel_fwd = onehot(pow_id == lane_tok)
    sel_rev = onehot(pow_id == t - 1 - lane_tok)
    sel_out = onehot(pow_id == t - lane_tok)
    row_tok = lax.broadcasted_iota(jnp.int32, (width, 32), 0) // n
    row_pow = lax.broadcasted_iota(jnp.int32, (width, 32), 1)
    rsel_rev = onehot(row_pow == t - 1 - row_tok)
    rsel_fwd = onehot(row_pow == row_tok)
    lane = lax.broadcasted_iota(jnp.int32, (n, width), 1)

    jc = jnp.minimum(lax.broadcasted_iota(jnp.int32, (p, 128), 1), t).astype(F32)
    jr = jnp.minimum(lax.broadcasted_iota(jnp.int32, (32, p), 0), t).astype(F32)

    strips = []
    f_parts = []
    e_parts = []
    for d in range(2):
        dt = jnp.exp(ldt[d, 0])
        mag_c = are_c[d, 0] * dt
        th_c = aim_c[d, 0] * dt
        ec = jnp.exp(jc * mag_c)
        qr = ec * jnp.cos(jc * th_c)
        qi = ec * jnp.sin(jc * th_c)
        a_re = are_r[d, 0]
        a_im = aim_r[d, 0]
        er = jnp.exp(jr * (a_re * dt))
        pr = er * jnp.cos(jr * (a_im * dt))
        pi = er * jnp.sin(jr * (a_im * dt))
        nr = pr[1:2] - 1.0
        ni = pi[1:2]
        den = a_re * a_re + a_im * a_im
        fre = (nr * a_re + ni * a_im) / den
        fim = (ni * a_re - nr * a_im) / den
        bt_re = bre_ref[d, 0].T
        bt_im = bim_ref[d, 0].T
        bb_re, bb_im = _cmul(fre, fim, bt_re, bt_im)
        ct_re = jnp.concatenate([cre_ref[d, 0].T] * t, axis=1)
        ct_im = jnp.concatenate([cim_ref[d, 0].T] * t, axis=1)
        sel = sel_fwd if d == 0 else sel_rev
        w_re, w_im = _cmul(ct_re, ct_im, _select_cols(qr, sel), _select_cols(qi, sel))
        strips.append(_dot3(bb_re, w_re) - _dot3(bb_im, w_im))
        if d == 0:
            o_re, o_im = _cmul(w_re, w_im, qr[:, 1:2], qi[:, 1:2])
        else:
            o_re, o_im = _cmul(ct_re, ct_im, _select_cols(qr, sel_out), _select_cols(qi, sel_out))
        e_parts += [o_re, -o_im]
        rsel = rsel_rev if d == 0 else rsel_fwd
        f_re, f_im = _cmul(jnp.concatenate([bb_re] * t, axis=0), jnp.concatenate([bb_im] * t, axis=0),
                           _select_rows(rsel, pr), _select_rows(rsel, pi))
        f_parts += [f_re, f_im]
        dec_ref[0, 2 * d:2 * d + 1, :] = jnp.concatenate([pr[t:t + 1], pr[t:t + 1]], axis=1)
        dec_ref[0, 2 * d + 1:2 * d + 2, :] = jnp.concatenate([-pi[t:t + 1], pi[t:t + 1]], axis=1)

    blocks = []
    for s in range(t):
        fwd = strips[0] if s == 0 else jnp.where(lane >= n * s, pltpu.roll(strips[0], n * s, axis=1), 0.0)
        back = t - 1 - s
        bwd = strips[1] if back == 0 else jnp.where(lane < width - n * back,
                                                     pltpu.roll(strips[1], width - n * back, axis=1), 0.0)
        blocks.append(fwd + bwd)
    m = jnp.concatenate(blocks, axis=0)
    w1_ref[0] = jnp.concatenate([m] + f_parts, axis=1).astype(BF16)
    e_ref[0] = jnp.concatenate(e_parts, axis=0).astype(BF16)


def _s5_operators(a_re, a_im, log_dt, b_re, b_im, c_re, c_im):
    g, p, n, t = S5_GROUPS, S5_STATE, S5_GROUP, CHUNK
    spec = lambda shape: pl.BlockSpec((2, 1) + shape, lambda i: (0, i) + (0,) * len(shape))
    return pl.pallas_call(
        _s5_prep_kernel, grid=(g,),
        in_specs=[spec((1, p)), spec((1, p)), spec((p, 1)), spec((p, 1)), spec((1, 1)),
                  spec((p, n)), spec((p, n)), spec((n, p)), spec((n, p))],
        out_specs=[pl.BlockSpec((1, t * n, 2 * t * n), lambda i: (i, 0, 0)),
                   pl.BlockSpec((1, 4 * p, t * n), lambda i: (i, 0, 0)),
                   pl.BlockSpec((1, 4, 2 * p), lambda i: (i, 0, 0))],
        out_shape=[jax.ShapeDtypeStruct((g, t * n, 2 * t * n), BF16),
                   jax.ShapeDtypeStruct((g, 4 * p, t * n), BF16),
                   jax.ShapeDtypeStruct((g, 4, 2 * p), F32)],
        compiler_params=_cparams(("parallel",)),
        name="s5_prep",
    )(a_re.reshape(2, g, 1, p), a_im.reshape(2, g, 1, p), a_re.reshape(2, g, p, 1),
      a_im.reshape(2, g, p, 1), log_dt.reshape(2, g, 1, 1), b_re, b_im, c_re, c_im)


def _block_transpose8(ps):
    ps = list(ps)
    blk = lax.broadcasted_iota(jnp.int32, ps[0].shape, 1) // S5_GROUP
    for k in range(3):
        step = 1 << k
        shift = S5_GROUP * step
        keep = ((blk >> k) & 1) == 0
        for a in range(8):
            if a & step:
                continue
            pa, pb = ps[a], ps[a + step]
            ps[a] = jnp.where(keep, pa, pltpu.roll(pb, shift, axis=1))
            ps[a + step] = jnp.where(keep, pltpu.roll(pa, 128 - shift, axis=1), pb)
    return ps


def _s5a_kernel(u_ref, w1_ref, yin_ref, gf_ref, gb_ref):
    nb = u_ref.shape[1]
    xs = []
    for s in range(CHUNK):
        parts = [u_ref[0, b, pl.ds(s, TILE_CHUNKS, stride=CHUNK), :] for b in range(nb)]
        xs.append(jnp.concatenate(parts, axis=0))
    lo = _block_transpose8(xs[:8])
    hi = _block_transpose8(xs[8:])
    for j in range(GRP_PER_BLK):
        og = jnp.concatenate([lo[j], hi[j]], axis=1)
        r = _dot(og.astype(BF16), w1_ref[j])
        yin_ref[j] = r[:, 0:256]
        gf_ref[j] = r[:, 256:384]
        gb_ref[j] = r[:, 384:512]


def _s5a_call(u, w1):
    n_blk, b, s, _ = u.shape
    nt = s // TILE_TOK
    rows = b * TILE_CHUNKS
    out_spec = lambda n: pl.BlockSpec((GRP_PER_BLK, rows, n), lambda l, j: (l, j, 0))
    return pl.pallas_call(
        _s5a_kernel, grid=(n_blk, nt),
        in_specs=[pl.BlockSpec((1, b, TILE_TOK, LANE_BLK), lambda l, j: (l, 0, j, 0)),
                  pl.BlockSpec((GRP_PER_BLK, 256, 512), lambda l, j: (l, 0, 0))],
        out_specs=[out_spec(256), out_spec(128), out_spec(128)],
        out_shape=[jax.ShapeDtypeStruct((S5_GROUPS, nt * rows, 256), F32),
                   jax.ShapeDtypeStruct((S5_GROUPS, nt * rows, 128), F32),
                   jax.ShapeDtypeStruct((S5_GROUPS, nt * rows, 128), F32)],
        compiler_params=_cparams(("parallel", "parallel")),
        name="s5_chunk_in",
    )(u, w1)


def _s5b_kernel(nb, a_ref, gfc_ref, gfl_ref, gbc_ref, gbl_ref, hfc_ref, hfl_ref, hbc_ref, hbl_ref):
    gb = a_ref.shape[0]
    rows = nb * TILE_CHUNKS
    n_lat = gfl_ref.shape[1] // rows
    a1f = [jnp.broadcast_to(a_ref[g, 0:1, :], (nb, 128)) for g in range(gb)]
    a2f = [jnp.broadcast_to(a_ref[g, 1:2, :], (nb, 128)) for g in range(gb)]
    a1b = [jnp.broadcast_to(a_ref[g, 2:3, :], (nb, 128)) for g in range(gb)]
    a2b = [jnp.broadcast_to(a_ref[g, 3:4, :], (nb, 128)) for g in range(gb)]

    def step(state, a1, a2, g_ref, h_ref, g, row):
        h, hs = state
        h_ref[g, row, :] = h
        inp = g_ref[g, row, :]
        return a1 * h + a2 * hs + inp, a1 * hs - a2 * h + pltpu.roll(inp, 64, axis=1)

    zero = jnp.zeros((nb, 128), F32)
    hf = [(zero, zero) for _ in range(gb)]
    hb = [(zero, zero) for _ in range(gb)]
    for ci in range(TILE_CHUNKS):
        rf = pl.ds(ci, nb, stride=TILE_CHUNKS)
        rb = pl.ds(TILE_CHUNKS - 1 - ci, nb, stride=TILE_CHUNKS)
        for g in range(gb):
            hf[g] = step(hf[g], a1f[g], a2f[g], gfc_ref, hfc_ref, g, rf)
            hb[g] = step(hb[g], a1b[g], a2b[g], gbc_ref, hbc_ref, g, rb)

    def body(j, carry):
        hf, hb = carry
        hf = list(hf)
        hb = list(hb)
        base_f = j * rows
        base_b = (n_lat - 1 - j) * rows
        for ci in range(TILE_CHUNKS):
            rf = pl.ds(base_f + ci, nb, stride=TILE_CHUNKS)
            rb = pl.ds(base_b + (TILE_CHUNKS - 1 - ci), nb, stride=TILE_CHUNKS)
            for g in range(gb):
                hf[g] = step(hf[g], a1f[g], a2f[g], gfl_ref, hfl_ref, g, rf)
                hb[g] = step(hb[g], a1b[g], a2b[g], gbl_ref, hbl_ref, g, rb)
        return tuple(hf), tuple(hb)

    lax.fori_loop(0, n_lat, body, (tuple(hf), tuple(hb)))


def _s5b_call(decay, gf_c, gf_l, gb_c, gb_l, nb):
    gblk = 4
    spec = lambda a: pl.BlockSpec((gblk, a.shape[1], 128), lambda i: (i, 0, 0))
    sds = lambda a: jax.ShapeDtypeStruct(a.shape, F32)
    return pl.pallas_call(
        functools.partial(_s5b_kernel, nb), grid=(S5_GROUPS // gblk,),
        in_specs=[pl.BlockSpec((gblk, 4, 128), lambda i: (i, 0, 0)),
                  spec(gf_c), spec(gf_l), spec(gb_c), spec(gb_l)],
        out_specs=[spec(gf_c), spec(gf_l), spec(gb_c), spec(gb_l)],
        out_shape=[sds(gf_c), sds(gf_l), sds(gb_c), sds(gb_l)],
        compiler_params=_cparams(("parallel",)),
        name="s5_state_scan",
    )(decay, gf_c, gf_l, gb_c, gb_l)


def _s5c_kernel(yin_ref, hf_ref, hb_ref, e_ref, u_ref, d_ref, y_ref):
    nb = u_ref.shape[1]
    ys = []
    for j in range(GRP_PER_BLK):
        h = jnp.concatenate([hf_ref[j], hb_ref[j]], axis=1).astype(BF16)
        ys.append(yin_ref[j] + _dot(h, e_ref[j]))
    at = (_block_transpose8([y[:, :128] for y in ys])
          + _block_transpose8([y[:, 128:] for y in ys]))
    d = d_ref[...]
    for t in range(CHUNK):
        for b in range(nb):
            rows = pl.ds(t, TILE_CHUNKS, stride=CHUNK)
            y_ref[0, b, rows, :] = at[t][b * TILE_CHUNKS:(b + 1) * TILE_CHUNKS] + d * u_ref[0, b, rows, :]


def _s5c_call(yin, hf, hb, e, u, d_skip):
    n_blk, b, s, _ = u.shape
    nt = s // TILE_TOK
    rows = b * TILE_CHUNKS
    gspec = lambda n: pl.BlockSpec((GRP_PER_BLK, rows, n), lambda l, j: (l, j, 0))
    tok_spec = pl.BlockSpec((1, b, TILE_TOK, LANE_BLK), lambda l, j: (l, 0, j, 0))
    return pl.pallas_call(
        _s5c_kernel, grid=(n_blk, nt),
        in_specs=[gspec(256), gspec(128), gspec(128),
                  pl.BlockSpec((GRP_PER_BLK, 256, 256), lambda l, j: (l, 0, 0)),
                  tok_spec,
                  pl.BlockSpec((1, LANE_BLK), lambda l, j: (0, l))],
        out_specs=tok_spec,
        out_shape=_lane_block_shape(b, s),
        compiler_params=_cparams(("parallel", "parallel")),
        name="s5_chunk_out",
    )(yin, hf, hb, e, u, d_skip.reshape(1, D_S5))


def _gelu_tanh(x):
    return 0.5 * x * (1.0 + jnp.tanh(math.sqrt(2.0 / math.pi) * (x + 0.044715 * (x * x * x))))


def _route(logits):
    lane = lax.broadcasted_iota(jnp.int32, logits.shape, 1).astype(F32)
    neg = jnp.float32(-1e30)
    big = jnp.float32(1e9)
    gl = jnp.where(lane < N_GROUPS, logits, neg)
    gmax = jnp.max(gl, axis=1, keepdims=True)
    gidx = jnp.min(jnp.where(gl == gmax, lane, big), axis=1, keepdims=True)
    gsum = jnp.sum(jnp.exp(gl - gmax), axis=1, keepdims=True)
    gw = 1.0 / gsum
    lo = N_GROUPS + EXP_PER_GROUP * gidx
    el = jnp.where((lane >= lo) & (lane < lo + EXP_PER_GROUP), logits, neg)
    v1 = jnp.max(el, axis=1, keepdims=True)
    i1 = jnp.min(jnp.where(el == v1, lane, big), axis=1, keepdims=True)
    el2 = jnp.where(lane == i1, neg, el)
    v2 = jnp.max(el2, axis=1, keepdims=True)
    i2 = jnp.min(jnp.where(el2 == v2, lane, big), axis=1, keepdims=True)
    ex = jnp.exp(v2 - v1)
    p1 = 1.0 / (1.0 + ex)
    p2 = ex * p1
    e1 = i1 - lo
    e2 = i2 - lo
    first = e1 < e2
    ea = jnp.where(first, e1, e2)
    eb = jnp.where(first, e2, e1)
    wa = gw * jnp.where(first, p1, p2)
    wb = gw * jnp.where(first, p2, p1)
    pair = ea * (7.0 - ea) * 0.5 + (eb - ea - 1.0)
    return wa, wb, 6.0 * gidx + pair


def _out_kernel(ypre_ref, ysc_ref, ycf_ref, x_ref, mod_ref, wglu_ref, bglu_ref, wo_ref,
                lng_ref, lnb_ref, wr_ref, br_ref, x1_ref, hx_ref, meta_ref, counts_ref, cnt_ref):
    @pl.when((pl.program_id(0) == 0) & (pl.program_id(1) == 0))
    def _():
        cnt_ref[...] = jnp.zeros_like(cnt_ref)

    ypre = jnp.concatenate([ypre_ref[blk, 0] for blk in range(ypre_ref.shape[0])], axis=1)
    t = _gelu_tanh(ypre)
    gate = _sigmoid(_dot(t.astype(BF16), wglu_ref[...]) + bglu_ref[...])
    ys5 = (t * gate).astype(BF16)
    y = (_dot(ys5, wo_ref[0:D_S5, :]) + _dot(ysc_ref[0], wo_ref[D_S5:D_S5 + D_SC, :])
         + _dot(ycf_ref[0], wo_ref[D_S5 + D_SC:D_MODEL, :]))
    g1 = mod_ref[0, 2:3, :]
    x1 = _layer_norm(DN_ALPHA * x_ref[0] + g1 * y, lng_ref[...], lnb_ref[...])
    x1_ref[0] = x1
    h2 = x1 * (1.0 + mod_ref[0, 4:5, :]) + mod_ref[0, 3:4, :]
    wa, wb, cls = _route(_dot(h2.astype(BF16), wr_ref[...]) + br_ref[...])

    tm = h2.shape[0]
    lane = lax.broadcasted_iota(jnp.int32, (tm, ROUTER_LANES), 1).astype(F32)
    onehot = jnp.where(lane == cls, 1.0, 0.0)
    row_i = lax.broadcasted_iota(jnp.int32, (tm, tm), 0)
    col_i = lax.broadcasted_iota(jnp.int32, (tm, tm), 1)
    earlier = jnp.where(col_i < row_i, 1.0, 0.0).astype(BF16)
    before = _dot(earlier, onehot.astype(BF16)) + cnt_ref[...]
    rank = jnp.sum(before * onehot, axis=1, keepdims=True)
    cnt_ref[...] += jnp.sum(onehot, axis=0, keepdims=True)
    counts_ref[...] = cnt_ref[...]

    meta = (jnp.where(lane == META_WA, wa, 0.0) + jnp.where(lane == META_WB, wb, 0.0)
            + jnp.where(lane == META_CLS, cls, 0.0) + jnp.where(lane == META_RANK, rank, 0.0))
    meta_ref[0] = meta
    hx_ref[0, :, 0:D_MODEL] = h2
    hx_ref[0, :, D_MODEL:HX_LANES] = meta


def _out_call(ypre, ysc, ycf, x, mod, wglu_bf, b_glu, wo_bf, ln_g, ln_b, w_router, b_router, tm):
    b, s, d = x.shape
    row_spec = lambda n: pl.BlockSpec((1, tm, n), lambda i, j: (i, j, 0))
    full = lambda shape: pl.BlockSpec(shape, lambda i, j: (0,) * len(shape))
    return pl.pallas_call(
        _out_kernel, grid=(b, s // tm),
        in_specs=[_lane_block_spec(tm), row_spec(D_SC), row_spec(D_CF), row_spec(d),
                  pl.BlockSpec((1, 6, d), lambda i, j: (i, 0, 0)),
                  full((D_S5, D_S5)), full((1, D_S5)), full((d, d)),
                  full((1, d)), full((1, d)), full((d, ROUTER_LANES)), full((1, ROUTER_LANES))],
        out_specs=[row_spec(d), row_spec(HX_LANES), row_spec(ROUTER_LANES), full((1, ROUTER_LANES))],
        out_shape=[jax.ShapeDtypeStruct((b, s, d), F32), jax.ShapeDtypeStruct((b, s, HX_LANES), F32),
                   jax.ShapeDtypeStruct((b, s, ROUTER_LANES), F32),
                   jax.ShapeDtypeStruct((1, ROUTER_LANES), F32)],
        scratch_shapes=[pltpu.VMEM((1, ROUTER_LANES), F32)],
        compiler_params=_cparams(("arbitrary", "arbitrary")),
        name="out_proj",
    )(ypre, ysc, ycf, x, mod, wglu_bf, b_glu.reshape(1, -1), wo_bf, ln_g.reshape(1, -1),
      ln_b.reshape(1, -1), w_router, b_router)


def _moe_plan(meta, counts, n_tok):
    cls = meta[..., META_CLS].reshape(-1).astype(jnp.int32)
    rank = meta[..., META_RANK].reshape(-1).astype(jnp.int32)
    cnt = counts[0, :N_CLASSES].astype(jnp.int32)
    n_tiles = (cnt + (MOE_TM - 1)) // MOE_TM
    ends = jnp.cumsum(n_tiles)
    starts = ends - n_tiles
    slot = starts[cls] * MOE_TM + rank
    t_max = n_tok // MOE_TM + N_CLASSES
    src = jnp.zeros((t_max * MOE_TM,), jnp.int32).at[slot].set(jnp.arange(n_tok, dtype=jnp.int32))
    n_used = ends[N_CLASSES - 1]
    tile = jnp.minimum(jnp.arange(t_max, dtype=jnp.int32), n_used - 1)
    tile_cls = jnp.sum((tile[:, None] >= ends[None, :]).astype(jnp.int32), axis=1)
    group = tile_cls // 6
    pair = tile_cls % 6
    first = jnp.array([0, 0, 0, 1, 1, 2], jnp.int32)[pair] + EXP_PER_GROUP * group
    second = jnp.array([1, 2, 3, 2, 3, 3], jnp.int32)[pair] + EXP_PER_GROUP * group
    return slot, src, first, second, n_used.reshape(1)


def _split_row(row):
    return lax.shift_right_logical(row, 3), lax.bitwise_and(row, SUBLANES - 1)


def _moe_kernel(src_ref, first_ref, second_ref, nused_ref, hx_ref, wga_ref, wgb_ref, wua_ref, wub_ref,
                wda_ref, wdb_ref, ys_ref, x_ref, sem):
    del first_ref, second_ref
    n_oct = x_ref.shape[1]
    t = pl.program_id(0)
    n_used = nused_ref[0]

    def request(which, buf):
        base = which * (n_oct * SUBLANES)

        def body(i, carry):
            for k in range(SUBLANES):
                oct_id, sub = _split_row(src_ref[base + i * SUBLANES + k])
                pltpu.make_async_copy(hx_ref.at[oct_id, pl.ds(sub, 1), :], x_ref.at[buf, i, pl.ds(k, 1), :],
                                      sem.at[buf]).start()
            return carry

        lax.fori_loop(0, n_oct, body, 0)

    @pl.when(t == 0)
    def _():
        request(0, 0)

    @pl.when(t + 1 < n_used)
    def _():
        request(t + 1, (t + 1) % 2)

    @pl.when(t < n_used)
    def _():
        buf = t % 2
        pltpu.make_async_copy(hx_ref.at[pl.ds(0, n_oct)], x_ref.at[buf], sem.at[buf]).wait()
        x = x_ref[buf].reshape(n_oct * SUBLANES, x_ref.shape[3])
        xb = x[:, 0:D_MODEL].astype(BF16)

        def expert(wg_ref, wu_ref, wd_ref, w):
            gate = _dot(xb, wg_ref[0])
            up = _dot(xb, wu_ref[0])
            act = gate * _sigmoid(gate) * up * w
            return _dot(act.astype(BF16), wd_ref[0])

        wa = x[:, D_MODEL + META_WA:D_MODEL + META_WA + 1]
        wb = x[:, D_MODEL + META_WB:D_MODEL + META_WB + 1]
        ys_ref[...] = expert(wga_ref, wua_ref, wda_ref, wa) + expert(wgb_ref, wub_ref, wdb_ref, wb)

    @pl.when(t >= n_used)
    def _():
        ys_ref[...] = jnp.zeros_like(ys_ref)


def _moe_call(src, first, second, n_used, hx, wg_bf, wu_bf, wd_bf):
    b, s, w = hx.shape
    d = D_MODEL
    t_max = first.shape[0]
    up_spec = lambda sel: pl.BlockSpec((1, d, D_EXPERT), lambda t, sr, fi, se, nu: ((fi, se)[sel][t], 0, 0))
    down_spec = lambda sel: pl.BlockSpec((1, D_EXPERT, d), lambda t, sr, fi, se, nu: ((fi, se)[sel][t], 0, 0))
    grid_spec = pltpu.PrefetchScalarGridSpec(
        num_scalar_prefetch=4, grid=(t_max,),
        in_specs=[pl.BlockSpec(memory_space=pl.ANY),
                  up_spec(0), up_spec(1), up_spec(0), up_spec(1), down_spec(0), down_spec(1)],
        out_specs=pl.BlockSpec((MOE_TM, d), lambda t, sr, fi, se, nu: (t, 0)),
        scratch_shapes=[pltpu.VMEM((2, MOE_TM // SUBLANES, SUBLANES, w), F32), pltpu.SemaphoreType.DMA((2,))])
    return pl.pallas_call(
        _moe_kernel, grid_spec=grid_spec,
        out_shape=jax.ShapeDtypeStruct((t_max * MOE_TM, d), F32),
        compiler_params=_cparams(("arbitrary",)),
        name="moe_experts",
    )(src, first, second, n_used, hx.reshape(b * s // SUBLANES, SUBLANES, w), wg_bf, wg_bf, wu_bf, wu_bf,
      wd_bf, wd_bf)


def _combine_kernel(slot_ref, x1_ref, mod_ref, lng_ref, lnb_ref, ys_ref, o_ref, f_ref, sem):
    n_oct = f_ref.shape[1]
    tm = n_oct * SUBLANES
    step = pl.program_id(0) * pl.num_programs(1) + pl.program_id(1)
    n_steps = pl.num_programs(0) * pl.num_programs(1)

    def request(which, buf):
        base = which * tm

        def body(i, carry):
            for k in range(SUBLANES):
                oct_id, sub = _split_row(slot_ref[base + i * SUBLANES + k])
                pltpu.make_async_copy(ys_ref.at[oct_id, pl.ds(sub, 1), :], f_ref.at[buf, i, pl.ds(k, 1), :],
                                      sem.at[buf]).start()
            return carry

        lax.fori_loop(0, n_oct, body, 0)

    @pl.when(step == 0)
    def _():
        request(0, 0)

    @pl.when(step + 1 < n_steps)
    def _():
        request(step + 1, (step + 1) % 2)

    buf = step % 2
    pltpu.make_async_copy(ys_ref.at[pl.ds(0, n_oct)], f_ref.at[buf], sem.at[buf]).wait()
    f = f_ref[buf].reshape(tm, f_ref.shape[3])
    g2 = mod_ref[0, 5:6, :]
    o_ref[0] = _layer_norm(DN_ALPHA * x1_ref[0] + g2 * f, lng_ref[...], lnb_ref[...])


def _combine_call(slot, x1, mod, ln_g, ln_b, ys, tm):
    b, s, d = x1.shape
    ys = ys.reshape(ys.shape[0] // SUBLANES, SUBLANES, d)
    grid_spec = pltpu.PrefetchScalarGridSpec(
        num_scalar_prefetch=1, grid=(b, s // tm),
        in_specs=[pl.BlockSpec((1, tm, d), lambda i, j, slot: (i, j, 0)),
                  pl.BlockSpec((1, 6, d), lambda i, j, slot: (i, 0, 0)),
                  pl.BlockSpec((1, d), lambda i, j, slot: (0, 0)),
                  pl.BlockSpec((1, d), lambda i, j, slot: (0, 0)),
                  pl.BlockSpec(memory_space=pl.ANY)],
        out_specs=pl.BlockSpec((1, tm, d), lambda i, j, slot: (i, j, 0)),
        scratch_shapes=[pltpu.VMEM((2, tm // SUBLANES, SUBLANES, d), F32), pltpu.SemaphoreType.DMA((2,))])
    return pl.pallas_call(
        _combine_kernel, grid_spec=grid_spec,
        out_shape=jax.ShapeDtypeStruct((b, s, d), F32),
        compiler_params=_cparams(("arbitrary", "arbitrary")),
        name="moe_combine",
    )(slot, x1, mod, ln_g.reshape(1, -1), ln_b.reshape(1, -1), ys)


def _moe_sublayer(hx, meta, counts, x1, mod, wg_bf, wu_bf, wd_bf, ln_g, ln_b, tm):
    b, s, _ = x1.shape
    n_tok = b * s
    slot, src, first, second, n_used = _moe_plan(meta, counts, n_tok)
    ys = _moe_call(src, first, second, n_used, hx, wg_bf, wu_bf, wd_bf)
    return _combine_call(slot, x1, mod, ln_g, ln_b, ys, tm)


def kernel(x, c, ctx, c_ctx, w_mod, b_mod, w_in, s5_a_re, s5_a_im, s5_log_dt, s5_b_re, s5_b_im, s5_c_re, s5_c_im, s5_d, w_glu, b_glu, w_sc, w_dw, b_dw, ln_cf_g, ln_cf_b, w_o, ln1_g, ln1_b, w_rg, b_rg, w_rexp, b_rexp, w_gate, w_up, w_down, ln2_g, ln2_b):
    nb, seq, d = x.shape
    n_ctx = ctx.shape[1]
    n_layers = w_mod.shape[0]
    assert seq % TILE_TOK == 0 and n_ctx % TILE_TOK == 0 and seq % GRID_W == 0

    mod_rows = 16
    assert nb + 1 <= mod_rows
    c_all = jnp.concatenate([c, c_ctx[None, :], jnp.zeros((mod_rows - nb - 1, d), F32)], axis=0)
    mod_all = _mod_call(c_all, w_mod, b_mod)

    pad_r = ROUTER_LANES - N_GROUPS - N_EXPERTS
    x_lat, x_ctx = x, ctx
    for l in range(n_layers):
        last = l == n_layers - 1
        mod_lat = mod_all[l, :nb].reshape(nb, 6, d)
        mod_ctx = jnp.broadcast_to(mod_all[l, nb].reshape(1, 6, d), (nb, 6, d))
        w_in_bf = w_in[l].astype(BF16)
        wglu_bf = w_glu[l].astype(BF16)
        wo_bf = w_o[l].astype(BF16)
        wg_bf = w_gate[l].astype(BF16)
        wu_bf = w_up[l].astype(BF16)
        wd_bf = w_down[l].astype(BF16)
        w_router = jnp.concatenate([w_rg[l], w_rexp[l], jnp.zeros((d, pad_r), F32)], axis=1).astype(BF16)
        b_router = jnp.concatenate([b_rg[l], b_rexp[l], jnp.zeros((pad_r,), F32)]).reshape(1, -1)
        w1, e_op, decay = _s5_operators(s5_a_re[l], s5_a_im[l], s5_log_dt[l], s5_b_re[l], s5_b_im[l],
                                        s5_c_re[l], s5_c_im[l])

        u_l, bg_l, cv_l, gl_l = _in_call(x_lat, mod_lat, w_in_bf, 512, False)
        if last:
            u_c = _in_call(x_ctx, mod_ctx, w_in_bf[:, :D_S5], TILE_TOK, True)
        else:
            u_c, bg_c, cv_c, gl_c = _in_call(x_ctx, mod_ctx, w_in_bf, TILE_TOK, False)

        yin_l, gf_l, gb_l = _s5a_call(u_l, w1)
        yin_c, gf_c, gb_c = _s5a_call(u_c, w1)
        hf_c, hf_l, hb_c, hb_l = _s5b_call(decay, gf_c, gf_l, gb_c, gb_l, nb)
        ypre_l = _s5c_call(yin_l, hf_l, hb_l, e_op, u_l, s5_d[l])

        ysc_l, ycf_l = _conv_call(bg_l, cv_l, gl_l, w_sc[l], w_dw[l], b_dw[l], ln_cf_g[l], ln_cf_b[l], True)
        x1_l, hx_l, meta_l, cnt_l = _out_call(ypre_l, ysc_l, ycf_l, x_lat, mod_lat, wglu_bf, b_glu[l], wo_bf,
                                              ln1_g[l], ln1_b[l], w_router, b_router, 512)
        if not last:
            ypre_c = _s5c_call(yin_c, hf_c, hb_c, e_op, u_c, s5_d[l])
            ysc_c, ycf_c = _conv_call(bg_c, cv_c, gl_c, w_sc[l], w_dw[l], b_dw[l], ln_cf_g[l], ln_cf_b[l], False)
            x1_c, hx_c, meta_c, cnt_c = _out_call(ypre_c, ysc_c, ycf_c, x_ctx, mod_ctx, wglu_bf, b_glu[l], wo_bf,
                                                  ln1_g[l], ln1_b[l], w_router, b_router, TILE_TOK)
            x_ctx = _moe_sublayer(hx_c, meta_c, cnt_c, x1_c, mod_ctx, wg_bf, wu_bf, wd_bf,
                                  ln2_g[l], ln2_b[l], TILE_TOK)
        x_lat = _moe_sublayer(hx_l, meta_l, cnt_l, x1_l, mod_lat, wg_bf, wu_bf, wd_bf,
                              ln2_g[l], ln2_b[l], 512)
    return x_lat
```

```python
import functools
import math

import jax
import jax.numpy as jnp
from jax import lax
from jax.experimental import pallas as pl
from jax.experimental.pallas import tpu as pltpu

F32 = jnp.float32
BF16 = jnp.bfloat16

D_MODEL = 1024
DEPTH = 2
GRID_W = 64
D_S5 = 512
S5_GROUP = 16
S5_GROUPS = 32
S5_STATE = 64
D_SC = 256
D_CF = 256
CF_WIDTH = 31
CF_HALF = 15
D_IN = 1792
N_GROUPS = 4
EXP_PER_GROUP = 4
N_EXPERTS = 16
D_EXPERT = 256
DN_ALPHA = (2 * DEPTH) ** 0.25
LN_EPS = 1e-5

CHUNK = 16
TILE_CHUNKS = 16
TILE_TOK = CHUNK * TILE_CHUNKS
LANE_BLK = 128
GRP_PER_BLK = LANE_BLK // S5_GROUP
ROUTER_LANES = 128
HX_LANES = D_MODEL + ROUTER_LANES
META_WA, META_WB, META_CLS, META_RANK = 0, 1, 2, 3
N_CLASSES = N_GROUPS * 6
MOE_TM = 256
SUBLANES = 8
ROW_TILE = D_MODEL // 128
VMEM_LIMIT = 56 * 1024 * 1024


def _cparams(sem):
    return pltpu.CompilerParams(dimension_semantics=sem, vmem_limit_bytes=VMEM_LIMIT)


def _split_bf16(a):
    hi = a.astype(BF16)
    lo = (a - hi.astype(F32)).astype(BF16)
    return hi, lo


def _dot(a, b):
    return jnp.dot(a, b, preferred_element_type=F32)


def _dot3(a, b):
    ah, al = _split_bf16(a)
    bh, bl = _split_bf16(b)
    return _dot(ah, bh) + (_dot(al, bh) + _dot(ah, bl))


def _sigmoid(x):
    return 1.0 / (1.0 + jnp.exp(-x))


def _layer_norm(x, g, b):
    mu = jnp.mean(x, axis=-1, keepdims=True)
    xc = x - mu
    var = jnp.mean(xc * xc, axis=-1, keepdims=True)
    return xc * lax.rsqrt(var + LN_EPS) * g + b


def _mod_kernel(c_ref, w_ref, b_ref, o_ref):
    c = c_ref[...]
    s = c * _sigmoid(c)
    o_ref[0] = _dot3(s, w_ref[0]) + b_ref[0]


def _mod_call(c_all, w_mod, b_mod):
    n_layers, d, n_out = w_mod.shape
    tn = 1536
    rows = c_all.shape[0]
    return pl.pallas_call(
        _mod_kernel,
        grid=(n_layers, n_out // tn),
        in_specs=[
            pl.BlockSpec((rows, d), lambda l, j: (0, 0)),
            pl.BlockSpec((1, d, tn), lambda l, j: (l, 0, j)),
            pl.BlockSpec((1, 1, tn), lambda l, j: (l, 0, j)),
        ],
        out_specs=pl.BlockSpec((1, rows, tn), lambda l, j: (l, 0, j)),
        out_shape=jax.ShapeDtypeStruct((n_layers, rows, n_out), F32),
        compiler_params=_cparams(("parallel", "parallel")),
        name="mod",
    )(c_all, w_mod, b_mod.reshape(n_layers, 1, n_out))


def _in_kernel(x_ref, mod_ref, w_ref, u_ref, bg_ref, cv_ref, gl_ref):
    x = x_ref[0]
    sh = mod_ref[0, 0:1, :]
    sc = mod_ref[0, 1:2, :]
    h = (x * (1.0 + sc) + sh).astype(BF16)
    z = _dot(h, w_ref[...])
    _store_lane_blocks(u_ref, z[:, 0:512])
    bg_ref[0] = z[:, 512:768]
    cv_ref[0] = z[:, 768:1024] * z[:, 1024:1280]
    gl_ref[0] = z[:, 1280:1536] * _sigmoid(z[:, 1536:1792])


def _in_u_kernel(x_ref, mod_ref, w_ref, u_ref):
    x = x_ref[0]
    sh = mod_ref[0, 0:1, :]
    sc = mod_ref[0, 1:2, :]
    h = (x * (1.0 + sc) + sh).astype(BF16)
    _store_lane_blocks(u_ref, _dot(h, w_ref[...]))


def _store_lane_blocks(ref, val):
    for blk in range(ref.shape[0]):
        ref[blk, 0] = val[:, blk * LANE_BLK:(blk + 1) * LANE_BLK]


def _lane_block_spec(tm):
    return pl.BlockSpec((D_S5 // LANE_BLK, 1, tm, LANE_BLK), lambda i, j: (0, i, j, 0))


def _lane_block_shape(b, s):
    return jax.ShapeDtypeStruct((D_S5 // LANE_BLK, b, s, LANE_BLK), F32)


def _in_call(x, mod, w_in_bf, tm, u_only):
    b, s, d = x.shape
    grid = (b, s // tm)
    row_spec = lambda n: pl.BlockSpec((1, tm, n), lambda i, j: (i, j, 0))
    in_specs = [
        row_spec(d),
        pl.BlockSpec((1, 6, d), lambda i, j: (i, 0, 0)),
    ]
    if u_only:
        in_specs.append(pl.BlockSpec((d, D_S5), lambda i, j: (0, 0)))
        return pl.pallas_call(
            _in_u_kernel, grid=grid, in_specs=in_specs,
            out_specs=_lane_block_spec(tm),
            out_shape=_lane_block_shape(b, s),
            compiler_params=_cparams(("parallel", "parallel")),
            name="in_proj_u",
        )(x, mod, w_in_bf)
    in_specs.append(pl.BlockSpec((d, D_IN), lambda i, j: (0, 0)))
    return pl.pallas_call(
        _in_kernel, grid=grid, in_specs=in_specs,
        out_specs=[_lane_block_spec(tm), row_spec(D_SC), row_spec(D_SC), row_spec(D_CF)],
        out_shape=[_lane_block_shape(b, s),
                   jax.ShapeDtypeStruct((b, s, D_SC), F32),
                   jax.ShapeDtypeStruct((b, s, D_SC), F32),
                   jax.ShapeDtypeStruct((b, s, D_CF), F32)],
        compiler_params=_cparams(("parallel", "parallel")),
        name="in_proj",
    )(x, mod, w_in_bf)


def _conv_tail(t, bdw_ref, lng_ref, lnb_ref):
    t = t + bdw_ref[...]
    t = _layer_norm(t, lng_ref[...], lnb_ref[...])
    return t * _sigmoid(t)


def _conv_grid_kernel(bg_ref, cv_ref, gl_ref, wsc_ref, wdw_ref, bdw_ref, lng_ref, lnb_ref,
                      ysc_ref, ycf_ref, pad_ref, t_ref):
    s = cv_ref.shape[1]
    rows = s // GRID_W
    cv = cv_ref[0]
    col = lax.broadcasted_iota(jnp.int32, (s, D_SC), 0) % GRID_W
    prev = jnp.where(col == 0, 0.0, pltpu.roll(cv, 1, axis=0))
    nxt = jnp.where(col == GRID_W - 1, 0.0, pltpu.roll(cv, s - 1, axis=0))
    conv = prev * wsc_ref[0:1, :] + cv * wsc_ref[1:2, :] + nxt * wsc_ref[2:3, :]
    ysc_ref[0] = (bg_ref[0] * conv).astype(ysc_ref.dtype)

    zero = jnp.zeros((CF_HALF, GRID_W, D_CF), F32)
    pad_ref[0:CF_HALF] = zero
    pad_ref[CF_HALF + rows:CF_HALF + rows + CF_HALF] = zero
    pad_ref[CF_HALF:CF_HALF + rows] = gl_ref[0].reshape(rows, GRID_W, D_CF)

    def body(i, carry):
        w0 = pl.multiple_of(i * 8, 8)
        for half in range(D_CF // 128):
            lanes = slice(half * 128, (half + 1) * 128)
            acc = jnp.zeros((rows, 8, 128), F32)
            for k in range(CF_WIDTH):
                acc = acc + pad_ref[k:k + rows, pl.ds(w0, 8), lanes] * wdw_ref[k:k + 1, lanes]
            t_ref[:, pl.ds(w0, 8), lanes] = acc
        return carry

    lax.fori_loop(0, GRID_W // 8, body, 0)
    t = t_ref[...].reshape(s, D_CF)
    ycf_ref[0] = _conv_tail(t, bdw_ref, lng_ref, lnb_ref).astype(ycf_ref.dtype)


def _conv_seq_kernel(bg_ref, cv_ref, gl_ref, wsc_ref, wdw_ref, bdw_ref, lng_ref, lnb_ref,
                     ysc_ref, ycf_ref, pad_ref):
    s = cv_ref.shape[1]
    cv = cv_ref[0]
    pos = lax.broadcasted_iota(jnp.int32, (s, D_SC), 0)
    prev = jnp.where(pos == 0, 0.0, pltpu.roll(cv, 1, axis=0))
    nxt = jnp.where(pos == s - 1, 0.0, pltpu.roll(cv, s - 1, axis=0))
    conv = prev * wsc_ref[0:1, :] + cv * wsc_ref[1:2, :] + nxt * wsc_ref[2:3, :]
    ysc_ref[0] = (bg_ref[0] * conv).astype(ysc_ref.dtype)

    off = 16
    pad_ref[0:off] = jnp.zeros((off, D_CF), F32)
    pad_ref[off + s:off + s + 16] = jnp.zeros((16, D_CF), F32)
    pad_ref[off:off + s] = gl_ref[0]
    acc = jnp.zeros((s, D_CF), F32)
    for k in range(CF_WIDTH):
        acc = acc + pad_ref[pl.ds(off - CF_HALF + k, s), :] * wdw_ref[k:k + 1, :]
    ycf_ref[0] = _conv_tail(acc, bdw_ref, lng_ref, lnb_ref).astype(ycf_ref.dtype)


def _conv_call(bg, cv, gl, w_sc, w_dw, b_dw, ln_g, ln_b, grid_mode):
    b, s, _ = bg.shape
    row_spec = pl.BlockSpec((1, s, D_SC), lambda i: (i, 0, 0))
    full = lambda shape: pl.BlockSpec(shape, lambda i: (0,) * len(shape))
    if grid_mode:
        rows = s // GRID_W
        kern = _conv_grid_kernel
        scratch = [pltpu.VMEM((rows + 2 * CF_HALF, GRID_W, D_CF), F32),
                   pltpu.VMEM((rows, GRID_W, D_CF), F32)]
        name = "conv_grid"
    else:
        kern = _conv_seq_kernel
        scratch = [pltpu.VMEM((s + 32, D_CF), F32)]
        name = "conv_seq"
    return pl.pallas_call(
        kern, grid=(b,),
        in_specs=[row_spec, row_spec, row_spec, full((3, D_SC)), full((CF_WIDTH, D_CF)),
                  full((1, D_CF)), full((1, D_CF)), full((1, D_CF))],
        out_specs=[row_spec, row_spec],
        out_shape=[jax.ShapeDtypeStruct((b, s, D_SC), BF16), jax.ShapeDtypeStruct((b, s, D_CF), BF16)],
        scratch_shapes=scratch,
        compiler_params=_cparams(("parallel",)),
        name=name,
    )(bg, cv, gl, w_sc, w_dw, b_dw.reshape(1, -1), ln_g.reshape(1, -1), ln_b.reshape(1, -1))


def _split3(a):
    hi = a.astype(BF16)
    r = a - hi.astype(F32)
    mid = r.astype(BF16)
    lo = (r - mid.astype(F32)).astype(BF16)
    return hi, mid, lo


def _select_cols(a, sel):
    hi, mid, lo = _split3(a)
    return _dot(hi, sel) + (_dot(mid, sel) + _dot(lo, sel))


def _select_rows(sel, a):
    hi, mid, lo = _split3(a)
    return _dot(sel, hi) + (_dot(sel, mid) + _dot(sel, lo))


def _cmul(ar, ai, br, bi):
    return ar * br - ai * bi, ar * bi + ai * br


def _s5_prep_kernel(are_r, aim_r, are_c, aim_c, ldt, bre_ref, bim_ref, cre_ref, cim_ref,
                    w1_ref, e_ref, dec_ref):
    t, n, p = CHUNK, S5_GROUP, S5_STATE
    width = t * n
    lane_tok = lax.broadcasted_iota(jnp.int32, (128, width), 1) // n
    pow_id = lax.broadcasted_iota(jnp.int32, (128, width), 0)
    onehot = lambda cond: jnp.where(cond, 1.0, 0.0).astype(BF16)
    sel_fwd = onehot(pow_id == lane_tok)
    sel_rev = onehot(pow_id == t - 1 - lane_tok)
    sel_out = onehot(pow_id == t - lane_tok)
    row_tok = lax.broadcasted_iota(jnp.int32, (width, 32), 0) // n
    row_pow = lax.broadcasted_iota(jnp.int32, (width, 32), 1)
    rsel_rev = onehot(row_pow == t - 1 - row_tok)
    rsel_fwd = onehot(row_pow == row_tok)
    lane = lax.broadcasted_iota(jnp.int32, (n, width), 1)

    jc = jnp.minimum(lax.broadcasted_iota(jnp.int32, (p, 128), 1), t).astype(F32)
    jr = jnp.minimum(lax.broadcasted_iota(jnp.int32, (32, p), 0), t).astype(F32)

    strips = []
    f_parts = []
    e_parts = []
    for d in range(2):
        dt = jnp.exp(ldt[d, 0])
        mag_c = are_c[d, 0] * dt
        th_c = aim_c[d, 0] * dt
        ec = jnp.exp(jc * mag_c)
        qr = ec * jnp.cos(jc * th_c)
        qi = ec * jnp.sin(jc * th_c)
        a_re = are_r[d, 0]
        a_im = aim_r[d, 0]
        er = jnp.exp(jr * (a_re * dt))
        pr = er * jnp.cos(jr * (a_im * dt))
        pi = er * jnp.sin(jr * (a_im * dt))
        nr = pr[1:2] - 1.0
        ni = pi[1:2]
        den = a_re * a_re + a_im * a_im
        fre = (nr * a_re + ni * a_im) / den
        fim = (ni * a_re - nr * a_im) / den
        bt_re = bre_ref[d, 0].T
        bt_im = bim_ref[d, 0].T
        bb_re, bb_im = _cmul(fre, fim, bt_re, bt_im)
        ct_re = jnp.concatenate([cre_ref[d, 0].T] * t, axis=1)
        ct_im = jnp.concatenate([cim_ref[d, 0].T] * t, axis=1)
        sel = sel_fwd if d == 0 else sel_rev
        w_re, w_im = _cmul(ct_re, ct_im, _select_cols(qr, sel), _select_cols(qi, sel))
        strips.append(_dot3(bb_re, w_re) - _dot3(bb_im, w_im))
        if d == 0:
            o_re, o_im = _cmul(w_re, w_im, qr[:, 1:2], qi[:, 1:2])
        else:
            o_re, o_im = _cmul(ct_re, ct_im, _select_cols(qr, sel_out), _select_cols(qi, sel_out))
        e_parts += [o_re, -o_im]
        rsel = rsel_rev if d == 0 else rsel_fwd
        f_re, f_im = _cmul(jnp.concatenate([bb_re] * t, axis=0), jnp.concatenate([bb_im] * t, axis=0),
                           _select_rows(rsel, pr), _select_rows(rsel, pi))
        f_parts += [f_re, f_im]
        dec_ref[0, 2 * d:2 * d + 1, :] = jnp.concatenate([pr[t:t + 1], pr[t:t + 1]], axis=1)
        dec_ref[0, 2 * d + 1:2 * d + 2, :] = jnp.concatenate([-pi[t:t + 1], pi[t:t + 1]], axis=1)

    blocks = []
    for s in range(t):
        fwd = strips[0] if s == 0 else jnp.where(lane >= n * s, pltpu.roll(strips[0], n * s, axis=1), 0.0)
        back = t - 1 - s
        bwd = strips[1] if back == 0 else jnp.where(lane < width - n * back,
                                                     pltpu.roll(strips[1], width - n * back, axis=1), 0.0)
        blocks.append(fwd + bwd)
    m = jnp.concatenate(blocks, axis=0)
    w1_ref[0] = jnp.concatenate([m] + f_parts, axis=1).astype(BF16)
    e_ref[0] = jnp.concatenate(e_parts, axis=0).astype(BF16)


def _s5_operators(a_re, a_im, log_dt, b_re, b_im, c_re, c_im):
    g, p, n, t = S5_GROUPS, S5_STATE, S5_GROUP, CHUNK
    spec = lambda shape: pl.BlockSpec((2, 1) + shape, lambda i: (0, i) + (0,) * len(shape))
    return pl.pallas_call(
        _s5_prep_kernel, grid=(g,),
        in_specs=[spec((1, p)), spec((1, p)), spec((p, 1)), spec((p, 1)), spec((1, 1)),
                  spec((p, n)), spec((p, n)), spec((n, p)), spec((n, p))],
        out_specs=[pl.BlockSpec((1, t * n, 2 * t * n), lambda i: (i, 0, 0)),
                   pl.BlockSpec((1, 4 * p, t * n), lambda i: (i, 0, 0)),
                   pl.BlockSpec((1, 4, 2 * p), lambda i: (i, 0, 0))],
        out_shape=[jax.ShapeDtypeStruct((g, t * n, 2 * t * n), BF16),
                   jax.ShapeDtypeStruct((g, 4 * p, t * n), BF16),
                   jax.ShapeDtypeStruct((g, 4, 2 * p), F32)],
        compiler_params=_cparams(("parallel",)),
        name="s5_prep",
    )(a_re.reshape(2, g, 1, p), a_im.reshape(2, g, 1, p), a_re.reshape(2, g, p, 1),
      a_im.reshape(2, g, p, 1), log_dt.reshape(2, g, 1, 1), b_re, b_im, c_re, c_im)


def _block_transpose8(ps):
    ps = list(ps)
    blk = lax.broadcasted_iota(jnp.int32, ps[0].shape, 1) // S5_GROUP
    for k in range(3):
        step = 1 << k
        shift = S5_GROUP * step
        keep = ((blk >> k) & 1) == 0
        for a in range(8):
            if a & step:
                continue
            pa, pb = ps[a], ps[a + step]
            ps[a] = jnp.where(keep, pa, pltpu.roll(pb, shift, axis=1))
            ps[a + step] = jnp.where(keep, pltpu.roll(pa, 128 - shift, axis=1), pb)
    return ps


def _s5a_kernel(u_ref, w1_ref, yin_ref, gf_ref, gb_ref):
    nb = u_ref.shape[1]
    xs = []
    for s in range(CHUNK):
        parts = [u_ref[0, b, pl.ds(s, TILE_CHUNKS, stride=CHUNK), :] for b in range(nb)]
        xs.append(jnp.concatenate(parts, axis=0))
    lo = _block_transpose8(xs[:8])
    hi = _block_transpose8(xs[8:])
    for j in range(GRP_PER_BLK):
        og = jnp.concatenate([lo[j], hi[j]], axis=1)
        r = _dot(og.astype(BF16), w1_ref[j])
        yin_ref[j] = r[:, 0:256]
        gf_ref[j] = r[:, 256:384]
        gb_ref[j] = r[:, 384:512]


def _s5a_call(u, w1):
    n_blk, b, s, _ = u.shape
    nt = s // TILE_TOK
    rows = b * TILE_CHUNKS
    out_spec = lambda n: pl.BlockSpec((GRP_PER_BLK, rows, n), lambda l, j: (l, j, 0))
    return pl.pallas_call(
        _s5a_kernel, grid=(n_blk, nt),
        in_specs=[pl.BlockSpec((1, b, TILE_TOK, LANE_BLK), lambda l, j: (l, 0, j, 0)),
                  pl.BlockSpec((GRP_PER_BLK, 256, 512), lambda l, j: (l, 0, 0))],
        out_specs=[out_spec(256), out_spec(128), out_spec(128)],
        out_shape=[jax.ShapeDtypeStruct((S5_GROUPS, nt * rows, 256), F32),
                   jax.ShapeDtypeStruct((S5_GROUPS, nt * rows, 128), F32),
                   jax.ShapeDtypeStruct((S5_GROUPS, nt * rows, 128), F32)],
        compiler_params=_cparams(("parallel", "parallel")),
        name="s5_chunk_in",
    )(u, w1)


def _s5b_kernel(nb, a_ref, gfc_ref, gfl_ref, gbc_ref, gbl_ref, hfc_ref, hfl_ref, hbc_ref, hbl_ref):
    gb = a_ref.shape[0]
    rows = nb * TILE_CHUNKS
    n_lat = gfl_ref.shape[1] // rows
    a1f = [jnp.broadcast_to(a_ref[g, 0:1, :], (nb, 128)) for g in range(gb)]
    a2f = [jnp.broadcast_to(a_ref[g, 1:2, :], (nb, 128)) for g in range(gb)]
    a1b = [jnp.broadcast_to(a_ref[g, 2:3, :], (nb, 128)) for g in range(gb)]
    a2b = [jnp.broadcast_to(a_ref[g, 3:4, :], (nb, 128)) for g in range(gb)]

    def step(state, a1, a2, g_ref, h_ref, g, row):
        h, hs = state
        h_ref[g, row, :] = h
        inp = g_ref[g, row, :]
        return a1 * h + a2 * hs + inp, a1 * hs - a2 * h + pltpu.roll(inp, 64, axis=1)

    zero = jnp.zeros((nb, 128), F32)
    hf = [(zero, zero) for _ in range(gb)]
    hb = [(zero, zero) for _ in range(gb)]
    for ci in range(TILE_CHUNKS):
        rf = pl.ds(ci, nb, stride=TILE_CHUNKS)
        rb = pl.ds(TILE_CHUNKS - 1 - ci, nb, stride=TILE_CHUNKS)
        for g in range(gb):
            hf[g] = step(hf[g], a1f[g], a2f[g], gfc_ref, hfc_ref, g, rf)
            hb[g] = step(hb[g], a1b[g], a2b[g], gbc_ref, hbc_ref, g, rb)

    def body(j, carry):
        hf, hb = carry
        hf = list(hf)
        hb = list(hb)
        base_f = j * rows
        base_b = (n_lat - 1 - j) * rows
        for ci in range(TILE_CHUNKS):
            rf = pl.ds(base_f + ci, nb, stride=TILE_CHUNKS)
            rb = pl.ds(base_b + (TILE_CHUNKS - 1 - ci), nb, stride=TILE_CHUNKS)
            for g in range(gb):
                hf[g] = step(hf[g], a1f[g], a2f[g], gfl_ref, hfl_ref, g, rf)
                hb[g] = step(hb[g], a1b[g], a2b[g], gbl_ref, hbl_ref, g, rb)
        return tuple(hf), tuple(hb)

    lax.fori_loop(0, n_lat, body, (tuple(hf), tuple(hb)))


def _s5b_call(decay, gf_c, gf_l, gb_c, gb_l, nb):
    gblk = 4
    spec = lambda a: pl.BlockSpec((gblk, a.shape[1], 128), lambda i: (i, 0, 0))
    sds = lambda a: jax.ShapeDtypeStruct(a.shape, F32)
    return pl.pallas_call(
        functools.partial(_s5b_kernel, nb), grid=(S5_GROUPS // gblk,),
        in_specs=[pl.BlockSpec((gblk, 4, 128), lambda i: (i, 0, 0)),
                  spec(gf_c), spec(gf_l), spec(gb_c), spec(gb_l)],
        out_specs=[spec(gf_c), spec(gf_l), spec(gb_c), spec(gb_l)],
        out_shape=[sds(gf_c), sds(gf_l), sds(gb_c), sds(gb_l)],
        compiler_params=_cparams(("parallel",)),
        name="s5_state_scan",
    )(decay, gf_c, gf_l, gb_c, gb_l)


def _s5c_kernel(yin_ref, hf_ref, hb_ref, e_ref, u_ref, d_ref, y_ref):
    nb = u_ref.shape[1]
    ys = []
    for j in range(GRP_PER_BLK):
        h = jnp.concatenate([hf_ref[j], hb_ref[j]], axis=1).astype(BF16)
        ys.append(yin_ref[j] + _dot(h, e_ref[j]))
    at = (_block_transpose8([y[:, :128] for y in ys])
          + _block_transpose8([y[:, 128:] for y in ys]))
    d = d_ref[...]
    for t in range(CHUNK):
        for b in range(nb):
            rows = pl.ds(t, TILE_CHUNKS, stride=CHUNK)
            y_ref[0, b, rows, :] = at[t][b * TILE_CHUNKS:(b + 1) * TILE_CHUNKS] + d * u_ref[0, b, rows, :]


def _s5c_call(yin, hf, hb, e, u, d_skip):
    n_blk, b, s, _ = u.shape
    nt = s // TILE_TOK
    rows = b * TILE_CHUNKS
    gspec = lambda n: pl.BlockSpec((GRP_PER_BLK, rows, n), lambda l, j: (l, j, 0))
    tok_spec = pl.BlockSpec((1, b, TILE_TOK, LANE_BLK), lambda l, j: (l, 0, j, 0))
    return pl.pallas_call(
        _s5c_kernel, grid=(n_blk, nt),
        in_specs=[gspec(256), gspec(128), gspec(128),
                  pl.BlockSpec((GRP_PER_BLK, 256, 256), lambda l, j: (l, 0, 0)),
                  tok_spec,
                  pl.BlockSpec((1, LANE_BLK), lambda l, j: (0, l))],
        out_specs=tok_spec,
        out_shape=_lane_block_shape(b, s),
        compiler_params=_cparams(("parallel", "parallel")),
        name="s5_chunk_out",
    )(yin, hf, hb, e, u, d_skip.reshape(1, D_S5))


def _gelu_tanh(x):
    return 0.5 * x * (1.0 + jnp.tanh(math.sqrt(2.0 / math.pi) * (x + 0.044715 * (x * x * x))))


def _route(logits):
    lane = lax.broadcasted_iota(jnp.int32, logits.shape, 1).astype(F32)
    neg = jnp.float32(-1e30)
    big = jnp.float32(1e9)
    gl = jnp.where(lane < N_GROUPS, logits, neg)
    gmax = jnp.max(gl, axis=1, keepdims=True)
    gidx = jnp.min(jnp.where(gl == gmax, lane, big), axis=1, keepdims=True)
    gsum = jnp.sum(jnp.exp(gl - gmax), axis=1, keepdims=True)
    gw = 1.0 / gsum
    lo = N_GROUPS + EXP_PER_GROUP * gidx
    el = jnp.where((lane >= lo) & (lane < lo + EXP_PER_GROUP), logits, neg)
    v1 = jnp.max(el, axis=1, keepdims=True)
    i1 = jnp.min(jnp.where(el == v1, lane, big), axis=1, keepdims=True)
    el2 = jnp.where(lane == i1, neg, el)
    v2 = jnp.max(el2, axis=1, keepdims=True)
    i2 = jnp.min(jnp.where(el2 == v2, lane, big), axis=1, keepdims=True)
    ex = jnp.exp(v2 - v1)
    p1 = 1.0 / (1.0 + ex)
    p2 = ex * p1
    e1 = i1 - lo
    e2 = i2 - lo
    first = e1 < e2
    ea = jnp.where(first, e1, e2)
    eb = jnp.where(first, e2, e1)
    wa = gw * jnp.where(first, p1, p2)
    wb = gw * jnp.where(first, p2, p1)
    pair = ea * (7.0 - ea) * 0.5 + (eb - ea - 1.0)
    return wa, wb, 6.0 * gidx + pair


def _out_kernel(ypre_ref, ysc_ref, ycf_ref, x_ref, mod_ref, wglu_ref, bglu_ref, wo_ref,
                lng_ref, lnb_ref, wr_ref, br_ref, x1_ref, hx_ref, meta_ref, counts_ref, cnt_ref):
    @pl.when((pl.program_id(0) == 0) & (pl.program_id(1) == 0))
    def _():
        cnt_ref[...] = jnp.zeros_like(cnt_ref)

    ypre = jnp.concatenate([ypre_ref[blk, 0] for blk in range(ypre_ref.shape[0])], axis=1)
    t = _gelu_tanh(ypre)
    gate = _sigmoid(_dot(t.astype(BF16), wglu_ref[...]) + bglu_ref[...])
    ys5 = (t * gate).astype(BF16)
    y = (_dot(ys5, wo_ref[0:D_S5, :]) + _dot(ysc_ref[0], wo_ref[D_S5:D_S5 + D_SC, :])
         + _dot(ycf_ref[0], wo_ref[D_S5 + D_SC:D_MODEL, :]))
    g1 = mod_ref[0, 2:3, :]
    x1 = _layer_norm(DN_ALPHA * x_ref[0] + g1 * y, lng_ref[...], lnb_ref[...])
    x1_ref[0] = x1
    h2 = x1 * (1.0 + mod_ref[0, 4:5, :]) + mod_ref[0, 3:4, :]
    wa, wb, cls = _route(_dot(h2.astype(BF16), wr_ref[...]) + br_ref[...])

    tm = h2.shape[0]
    lane = lax.broadcasted_iota(jnp.int32, (tm, ROUTER_LANES), 1).astype(F32)
    onehot = jnp.where(lane == cls, 1.0, 0.0)
    row_i = lax.broadcasted_iota(jnp.int32, (tm, tm), 0)
    col_i = lax.broadcasted_iota(jnp.int32, (tm, tm), 1)
    earlier = jnp.where(col_i < row_i, 1.0, 0.0).astype(BF16)
    before = _dot(earlier, onehot.astype(BF16)) + cnt_ref[...]
    rank = jnp.sum(before * onehot, axis=1, keepdims=True)
    cnt_ref[...] += jnp.sum(onehot, axis=0, keepdims=True)
    counts_ref[...] = cnt_ref[...]

    meta = (jnp.where(lane == META_WA, wa, 0.0) + jnp.where(lane == META_WB, wb, 0.0)
            + jnp.where(lane == META_CLS, cls, 0.0) + jnp.where(lane == META_RANK, rank, 0.0))
    meta_ref[0] = meta
    hx_ref[0, :, 0:D_MODEL] = h2
    hx_ref[0, :, D_MODEL:HX_LANES] = meta


def _out_call(ypre, ysc, ycf, x, mod, wglu_bf, b_glu, wo_bf, ln_g, ln_b, w_router, b_router, tm):
    b, s, d = x.shape
    row_spec = lambda n: pl.BlockSpec((1, tm, n), lambda i, j: (i, j, 0))
    full = lambda shape: pl.BlockSpec(shape, lambda i, j: (0,) * len(shape))
    return pl.pallas_call(
        _out_kernel, grid=(b, s // tm),
        in_specs=[_lane_block_spec(tm), row_spec(D_SC), row_spec(D_CF), row_spec(d),
                  pl.BlockSpec((1, 6, d), lambda i, j: (i, 0, 0)),
                  full((D_S5, D_S5)), full((1, D_S5)), full((d, d)),
                  full((1, d)), full((1, d)), full((d, ROUTER_LANES)), full((1, ROUTER_LANES))],
        out_specs=[row_spec(d), row_spec(HX_LANES), row_spec(ROUTER_LANES), full((1, ROUTER_LANES))],
        out_shape=[jax.ShapeDtypeStruct((b, s, d), F32), jax.ShapeDtypeStruct((b, s, HX_LANES), F32),
                   jax.ShapeDtypeStruct((b, s, ROUTER_LANES), F32),
                   jax.ShapeDtypeStruct((1, ROUTER_LANES), F32)],
        scratch_shapes=[pltpu.VMEM((1, ROUTER_LANES), F32)],
        compiler_params=_cparams(("arbitrary", "arbitrary")),
        name="out_proj",
    )(ypre, ysc, ycf, x, mod, wglu_bf, b_glu.reshape(1, -1), wo_bf, ln_g.reshape(1, -1),
      ln_b.reshape(1, -1), w_router, b_router)


def _moe_plan(meta, counts, n_tok):
    cls = meta[..., META_CLS].reshape(-1).astype(jnp.int32)
    rank = meta[..., META_RANK].reshape(-1).astype(jnp.int32)
    cnt = counts[0, :N_CLASSES].astype(jnp.int32)
    n_tiles = (cnt + (MOE_TM - 1)) // MOE_TM
    ends = jnp.cumsum(n_tiles)
    starts = ends - n_tiles
    slot = starts[cls] * MOE_TM + rank
    t_max = n_tok // MOE_TM + N_CLASSES
    n_used = ends[N_CLASSES - 1]
    tile = jnp.minimum(jnp.arange(t_max, dtype=jnp.int32), n_used - 1)
    tile_cls = jnp.sum((tile[:, None] >= ends[None, :]).astype(jnp.int32), axis=1)
    group = tile_cls // 6
    pair = tile_cls % 6
    first = jnp.array([0, 0, 0, 1, 1, 2], jnp.int32)[pair] + EXP_PER_GROUP * group
    second = jnp.array([1, 2, 3, 2, 3, 3], jnp.int32)[pair] + EXP_PER_GROUP * group
    return slot, tile, first, second, n_used.reshape(1)


def _split_row(row):
    return lax.shift_right_logical(row, 3), lax.bitwise_and(row, SUBLANES - 1)


def _dispatch_kernel(slot_ref, hx_ref, xs_init_ref, xs_ref, sem):
    del xs_init_ref
    n_oct = hx_ref.shape[1]
    base = (pl.program_id(0) * pl.num_programs(1) + pl.program_id(1)) * (n_oct * SUBLANES)

    def body(i, carry):
        for k in range(SUBLANES):
            oct_id, sub = _split_row(slot_ref[base + i * SUBLANES + k])
            pltpu.make_async_copy(hx_ref.at[0, i, pl.ds(k, 1), :], xs_ref.at[oct_id, pl.ds(sub, 1), :], sem).start()
        return carry

    lax.fori_loop(0, n_oct, body, 0)
    pltpu.make_async_copy(hx_ref.at[0], xs_ref.at[pl.ds(0, n_oct)], sem).wait()


def _dispatch_call(slot, hx, n_rows, tm):
    b, s, w = hx.shape
    xs_init = jnp.zeros((n_rows // SUBLANES, SUBLANES, w), F32)
    grid_spec = pltpu.PrefetchScalarGridSpec(
        num_scalar_prefetch=1, grid=(b, s // tm),
        in_specs=[pl.BlockSpec((1, tm // SUBLANES, SUBLANES, w), lambda i, j, slot: (i, j, 0, 0)),
                  pl.BlockSpec(memory_space=pl.ANY)],
        out_specs=pl.BlockSpec(memory_space=pl.ANY),
        scratch_shapes=[pltpu.SemaphoreType.DMA(())])
    xs = pl.pallas_call(
        _dispatch_kernel, grid_spec=grid_spec,
        out_shape=jax.ShapeDtypeStruct(xs_init.shape, F32),
        input_output_aliases={2: 0},
        compiler_params=_cparams(("arbitrary", "arbitrary")),
        name="moe_dispatch",
    )(slot, hx.reshape(b, s // SUBLANES, SUBLANES, w), xs_init)
    return xs.reshape(n_rows, w)


def _moe_kernel(tile_ref, first_ref, second_ref, nused_ref, xs_ref, wga_ref, wgb_ref, wua_ref, wub_ref,
                wda_ref, wdb_ref, ys_ref):
    del tile_ref, first_ref, second_ref
    t = pl.program_id(0)
    n_used = nused_ref[0]

    @pl.when(t < n_used)
    def _():
        x = xs_ref[...]
        xb = x[:, 0:D_MODEL].astype(BF16)

        def expert(wg_ref, wu_ref, wd_ref, w):
            gate = _dot(xb, wg_ref[0])
            up = _dot(xb, wu_ref[0])
            act = gate * _sigmoid(gate) * up * w
            return _dot(act.astype(BF16), wd_ref[0])

        wa = x[:, D_MODEL + META_WA:D_MODEL + META_WA + 1]
        wb = x[:, D_MODEL + META_WB:D_MODEL + META_WB + 1]
        y = expert(wga_ref, wua_ref, wda_ref, wa) + expert(wgb_ref, wub_ref, wdb_ref, wb)
        for q in range(ROW_TILE):
            ys_ref[:, q, :] = y[:, q * 128:(q + 1) * 128]

    @pl.when(t >= n_used)
    def _():
        ys_ref[...] = jnp.zeros_like(ys_ref)


def _moe_call(tile, first, second, n_used, xs, wg_bf, wu_bf, wd_bf):
    n_rows, w = xs.shape
    d = D_MODEL
    t_max = tile.shape[0]
    up_spec = lambda sel: pl.BlockSpec((1, d, D_EXPERT), lambda t, tl, fi, se, nu: ((fi, se)[sel][t], 0, 0))
    down_spec = lambda sel: pl.BlockSpec((1, D_EXPERT, d), lambda t, tl, fi, se, nu: ((fi, se)[sel][t], 0, 0))
    grid_spec = pltpu.PrefetchScalarGridSpec(
        num_scalar_prefetch=4, grid=(t_max,),
        in_specs=[pl.BlockSpec((MOE_TM, w), lambda t, tl, fi, se, nu: (tl[t], 0)),
                  up_spec(0), up_spec(1), up_spec(0), up_spec(1), down_spec(0), down_spec(1)],
        out_specs=pl.BlockSpec((MOE_TM, ROW_TILE, 128), lambda t, tl, fi, se, nu: (t, 0, 0)))
    return pl.pallas_call(
        _moe_kernel, grid_spec=grid_spec,
        out_shape=jax.ShapeDtypeStruct((n_rows, ROW_TILE, 128), F32),
        compiler_params=_cparams(("arbitrary",)),
        name="moe_experts",
    )(tile, first, second, n_used, xs, wg_bf, wg_bf, wu_bf, wu_bf, wd_bf, wd_bf)


def _combine_kernel(slot_ref, x1_ref, mod_ref, lng_ref, lnb_ref, ys_ref, o_ref, f_ref, sem):
    tm = f_ref.shape[1]
    step = pl.program_id(0) * pl.num_programs(1) + pl.program_id(1)
    n_steps = pl.num_programs(0) * pl.num_programs(1)

    def request(which, buf):
        base = which * tm

        def body(i, carry):
            for k in range(SUBLANES):
                r = i * SUBLANES + k
                pltpu.make_async_copy(ys_ref.at[slot_ref[base + r]], f_ref.at[buf, r], sem.at[buf]).start()
            return carry

        lax.fori_loop(0, tm // SUBLANES, body, 0)

    @pl.when(step == 0)
    def _():
        request(0, 0)

    @pl.when(step + 1 < n_steps)
    def _():
        request(step + 1, (step + 1) % 2)

    buf = step % 2
    pltpu.make_async_copy(ys_ref.at[pl.ds(0, tm)], f_ref.at[buf], sem.at[buf]).wait()
    f = jnp.concatenate([f_ref[buf, :, q, :] for q in range(ROW_TILE)], axis=1)
    g2 = mod_ref[0, 5:6, :]
    o_ref[0] = _layer_norm(DN_ALPHA * x1_ref[0] + g2 * f, lng_ref[...], lnb_ref[...])


def _combine_call(slot, x1, mod, ln_g, ln_b, ys, tm):
    b, s, d = x1.shape
    grid_spec = pltpu.PrefetchScalarGridSpec(
        num_scalar_prefetch=1, grid=(b, s // tm),
        in_specs=[pl.BlockSpec((1, tm, d), lambda i, j, slot: (i, j, 0)),
                  pl.BlockSpec((1, 6, d), lambda i, j, slot: (i, 0, 0)),
                  pl.BlockSpec((1, d), lambda i, j, slot: (0, 0)),
                  pl.BlockSpec((1, d), lambda i, j, slot: (0, 0)),
                  pl.BlockSpec(memory_space=pl.ANY)],
        out_specs=pl.BlockSpec((1, tm, d), lambda i, j, slot: (i, j, 0)),
        scratch_shapes=[pltpu.VMEM((2, tm, ROW_TILE, 128), F32), pltpu.SemaphoreType.DMA((2,))])
    return pl.pallas_call(
        _combine_kernel, grid_spec=grid_spec,
        out_shape=jax.ShapeDtypeStruct((b, s, d), F32),
        compiler_params=_cparams(("arbitrary", "arbitrary")),
        name="moe_combine",
    )(slot, x1, mod, ln_g.reshape(1, -1), ln_b.reshape(1, -1), ys)


def _moe_sublayer(hx, meta, counts, x1, mod, wg_bf, wu_bf, wd_bf, ln_g, ln_b, tm):
    b, s, _ = x1.shape
    n_tok = b * s
    slot, tile, first, second, n_used = _moe_plan(meta, counts, n_tok)
    n_rows = n_tok + N_CLASSES * MOE_TM
    xs = _dispatch_call(slot, hx, n_rows, tm)
    ys = _moe_call(tile, first, second, n_used, xs, wg_bf, wu_bf, wd_bf)
    return _combine_call(slot, x1, mod, ln_g, ln_b, ys, tm)


def kernel(x, c, ctx, c_ctx, w_mod, b_mod, w_in, s5_a_re, s5_a_im, s5_log_dt, s5_b_re, s5_b_im, s5_c_re, s5_c_im, s5_d, w_glu, b_glu, w_sc, w_dw, b_dw, ln_cf_g, ln_cf_b, w_o, ln1_g, ln1_b, w_rg, b_rg, w_rexp, b_rexp, w_gate, w_up, w_down, ln2_g, ln2_b):
    nb, seq, d = x.shape
    n_ctx = ctx.shape[1]
    n_layers = w_mod.shape[0]
    assert seq % TILE_TOK == 0 and n_ctx % TILE_TOK == 0 and seq % GRID_W == 0

    mod_rows = 16
    assert nb + 1 <= mod_rows
    c_all = jnp.concatenate([c, c_ctx[None, :], jnp.zeros((mod_rows - nb - 1, d), F32)], axis=0)
    mod_all = _mod_call(c_all, w_mod, b_mod)

    pad_r = ROUTER_LANES - N_GROUPS - N_EXPERTS
    x_lat, x_ctx = x, ctx
    for l in range(n_layers):
        last = l == n_layers - 1
        mod_lat = mod_all[l, :nb].reshape(nb, 6, d)
        mod_ctx = jnp.broadcast_to(mod_all[l, nb].reshape(1, 6, d), (nb, 6, d))
        w_in_bf = w_in[l].astype(BF16)
        wglu_bf = w_glu[l].astype(BF16)
        wo_bf = w_o[l].astype(BF16)
        wg_bf = w_gate[l].astype(BF16)
        wu_bf = w_up[l].astype(BF16)
        wd_bf = w_down[l].astype(BF16)
        w_router = jnp.concatenate([w_rg[l], w_rexp[l], jnp.zeros((d, pad_r), F32)], axis=1).astype(BF16)
        b_router = jnp.concatenate([b_rg[l], b_rexp[l], jnp.zeros((pad_r,), F32)]).reshape(1, -1)
        w1, e_op, decay = _s5_operators(s5_a_re[l], s5_a_im[l], s5_log_dt[l], s5_b_re[l], s5_b_im[l],
                                        s5_c_re[l], s5_c_im[l])

        u_l, bg_l, cv_l, gl_l = _in_call(x_lat, mod_lat, w_in_bf, 512, False)
        if last:
            u_c = _in_call(x_ctx, mod_ctx, w_in_bf[:, :D_S5], TILE_TOK, True)
        else:
            u_c, bg_c, cv_c, gl_c = _in_call(x_ctx, mod_ctx, w_in_bf, TILE_TOK, False)

        yin_l, gf_l, gb_l = _s5a_call(u_l, w1)
        yin_c, gf_c, gb_c = _s5a_call(u_c, w1)
        hf_c, hf_l, hb_c, hb_l = _s5b_call(decay, gf_c, gf_l, gb_c, gb_l, nb)
        ypre_l = _s5c_call(yin_l, hf_l, hb_l, e_op, u_l, s5_d[l])

        ysc_l, ycf_l = _conv_call(bg_l, cv_l, gl_l, w_sc[l], w_dw[l], b_dw[l], ln_cf_g[l], ln_cf_b[l], True)
        x1_l, hx_l, meta_l, cnt_l = _out_call(ypre_l, ysc_l, ycf_l, x_lat, mod_lat, wglu_bf, b_glu[l], wo_bf,
                                              ln1_g[l], ln1_b[l], w_router, b_router, 512)
        if not last:
            ypre_c = _s5c_call(yin_c, hf_c, hb_c, e_op, u_c, s5_d[l])
            ysc_c, ycf_c = _conv_call(bg_c, cv_c, gl_c, w_sc[l], w_dw[l], b_dw[l], ln_cf_g[l], ln_cf_b[l], False)
            x1_c, hx_c, meta_c, cnt_c = _out_call(ypre_c, ysc_c, ycf_c, x_ctx, mod_ctx, wglu_bf, b_glu[l], wo_bf,
                                                  ln1_g[l], ln1_b[l], w_router, b_router, TILE_TOK)
            x_ctx = _moe_sublayer(hx_c, meta_c, cnt_c, x1_c, mod_ctx, wg_bf, wu_bf, wd_bf,
                                  ln2_g[l], ln2_b[l], TILE_TOK)
        x_lat = _moe_sublayer(hx_l, meta_l, cnt_l, x1_l, mod_lat, wg_bf, wu_bf, wd_bf,
                              ln2_g[l], ln2_b[l], 512)
    return x_lat
```

```python
import functools
import math

import jax
import jax.numpy as jnp
from jax import lax
from jax.experimental import pallas as pl
from jax.experimental.pallas import tpu as pltpu

F32 = jnp.float32
BF16 = jnp.bfloat16

D_MODEL = 1024
DEPTH = 2
GRID_W = 64
D_S5 = 512
S5_GROUP = 16
S5_GROUPS = 32
S5_STATE = 64
D_SC = 256
D_CF = 256
CF_WIDTH = 31
CF_HALF = 15
D_IN = 1792
N_GROUPS = 4
EXP_PER_GROUP = 4
N_EXPERTS = 16
D_EXPERT = 256
DN_ALPHA = (2 * DEPTH) ** 0.25
LN_EPS = 1e-5

CHUNK = 16
TILE_CHUNKS = 16
TILE_TOK = CHUNK * TILE_CHUNKS
LANE_BLK = 128
GRP_PER_BLK = LANE_BLK // S5_GROUP
ROUTER_LANES = 128
HX_LANES = D_MODEL + ROUTER_LANES
META_WA, META_WB, META_CLS, META_RANK = 0, 1, 2, 3
N_CLASSES = N_GROUPS * 6
MOE_TM = 256
SUBLANES = 8
VMEM_LIMIT = 56 * 1024 * 1024


def _cparams(sem):
    return pltpu.CompilerParams(dimension_semantics=sem, vmem_limit_bytes=VMEM_LIMIT)


def _split_bf16(a):
    hi = a.astype(BF16)
    lo = (a - hi.astype(F32)).astype(BF16)
    return hi, lo


def _dot(a, b):
    return jnp.dot(a, b, preferred_element_type=F32)


def _dot3(a, b):
    ah, al = _split_bf16(a)
    bh, bl = _split_bf16(b)
    return _dot(ah, bh) + (_dot(al, bh) + _dot(ah, bl))


def _sigmoid(x):
    return 1.0 / (1.0 + jnp.exp(-x))


def _layer_norm(x, g, b):
    mu = jnp.mean(x, axis=-1, keepdims=True)
    xc = x - mu
    var = jnp.mean(xc * xc, axis=-1, keepdims=True)
    return xc * lax.rsqrt(var + LN_EPS) * g + b


def _mod_kernel(c_ref, w_ref, b_ref, o_ref):
    c = c_ref[...]
    s = c * _sigmoid(c)
    o_ref[0] = _dot3(s, w_ref[0]) + b_ref[0]


def _mod_call(c_all, w_mod, b_mod):
    n_layers, d, n_out = w_mod.shape
    tn = 1536
    rows = c_all.shape[0]
    return pl.pallas_call(
        _mod_kernel,
        grid=(n_layers, n_out // tn),
        in_specs=[
            pl.BlockSpec((rows, d), lambda l, j: (0, 0)),
            pl.BlockSpec((1, d, tn), lambda l, j: (l, 0, j)),
            pl.BlockSpec((1, 1, tn), lambda l, j: (l, 0, j)),
        ],
        out_specs=pl.BlockSpec((1, rows, tn), lambda l, j: (l, 0, j)),
        out_shape=jax.ShapeDtypeStruct((n_layers, rows, n_out), F32),
        compiler_params=_cparams(("parallel", "parallel")),
        name="mod",
    )(c_all, w_mod, b_mod.reshape(n_layers, 1, n_out))


def _in_kernel(x_ref, mod_ref, w_ref, u_ref, bg_ref, cv_ref, gl_ref):
    x = x_ref[0]
    sh = mod_ref[0, 0:1, :]
    sc = mod_ref[0, 1:2, :]
    h = (x * (1.0 + sc) + sh).astype(BF16)
    z = _dot(h, w_ref[...])
    _store_lane_blocks(u_ref, z[:, 0:512])
    bg_ref[0] = z[:, 512:768]
    cv_ref[0] = z[:, 768:1024] * z[:, 1024:1280]
    gl_ref[0] = z[:, 1280:1536] * _sigmoid(z[:, 1536:1792])


def _in_u_kernel(x_ref, mod_ref, w_ref, u_ref):
    x = x_ref[0]
    sh = mod_ref[0, 0:1, :]
    sc = mod_ref[0, 1:2, :]
    h = (x * (1.0 + sc) + sh).astype(BF16)
    _store_lane_blocks(u_ref, _dot(h, w_ref[...]))


def _store_lane_blocks(ref, val):
    for blk in range(ref.shape[0]):
        ref[blk, 0] = val[:, blk * LANE_BLK:(blk + 1) * LANE_BLK]


def _lane_block_spec(tm):
    return pl.BlockSpec((D_S5 // LANE_BLK, 1, tm, LANE_BLK), lambda i, j: (0, i, j, 0))


def _lane_block_shape(b, s):
    return jax.ShapeDtypeStruct((D_S5 // LANE_BLK, b, s, LANE_BLK), F32)


def _in_call(x, mod, w_in_bf, tm, u_only):
    b, s, d = x.shape
    grid = (b, s // tm)
    row_spec = lambda n: pl.BlockSpec((1, tm, n), lambda i, j: (i, j, 0))
    in_specs = [
        row_spec(d),
        pl.BlockSpec((1, 6, d), lambda i, j: (i, 0, 0)),
    ]
    if u_only:
        in_specs.append(pl.BlockSpec((d, D_S5), lambda i, j: (0, 0)))
        return pl.pallas_call(
            _in_u_kernel, grid=grid, in_specs=in_specs,
            out_specs=_lane_block_spec(tm),
            out_shape=_lane_block_shape(b, s),
            compiler_params=_cparams(("parallel", "parallel")),
            name="in_proj_u",
        )(x, mod, w_in_bf)
    in_specs.append(pl.BlockSpec((d, D_IN), lambda i, j: (0, 0)))
    return pl.pallas_call(
        _in_kernel, grid=grid, in_specs=in_specs,
        out_specs=[_lane_block_spec(tm), row_spec(D_SC), row_spec(D_SC), row_spec(D_CF)],
        out_shape=[_lane_block_shape(b, s),
                   jax.ShapeDtypeStruct((b, s, D_SC), F32),
                   jax.ShapeDtypeStruct((b, s, D_SC), F32),
                   jax.ShapeDtypeStruct((b, s, D_CF), F32)],
        compiler_params=_cparams(("parallel", "parallel")),
        name="in_proj",
    )(x, mod, w_in_bf)


def _conv_tail(t, bdw_ref, lng_ref, lnb_ref):
    t = t + bdw_ref[...]
    t = _layer_norm(t, lng_ref[...], lnb_ref[...])
    return t * _sigmoid(t)


def _conv_grid_kernel(bg_ref, cv_ref, gl_ref, wsc_ref, wdw_ref, bdw_ref, lng_ref, lnb_ref,
                      ysc_ref, ycf_ref, pad_ref, t_ref):
    s = cv_ref.shape[1]
    rows = s // GRID_W
    cv = cv_ref[0]
    col = lax.broadcasted_iota(jnp.int32, (s, D_SC), 0) % GRID_W
    prev = jnp.where(col == 0, 0.0, pltpu.roll(cv, 1, axis=0))
    nxt = jnp.where(col == GRID_W - 1, 0.0, pltpu.roll(cv, s - 1, axis=0))
    conv = prev * wsc_ref[0:1, :] + cv * wsc_ref[1:2, :] + nxt * wsc_ref[2:3, :]
    ysc_ref[0] = (bg_ref[0] * conv).astype(ysc_ref.dtype)

    zero = jnp.zeros((CF_HALF, GRID_W, D_CF), F32)
    pad_ref[0:CF_HALF] = zero
    pad_ref[CF_HALF + rows:CF_HALF + rows + CF_HALF] = zero
    pad_ref[CF_HALF:CF_HALF + rows] = gl_ref[0].reshape(rows, GRID_W, D_CF)

    def body(i, carry):
        w0 = pl.multiple_of(i * 8, 8)
        for half in range(D_CF // 128):
            lanes = slice(half * 128, (half + 1) * 128)
            acc = jnp.zeros((rows, 8, 128), F32)
            for k in range(CF_WIDTH):
                acc = acc + pad_ref[k:k + rows, pl.ds(w0, 8), lanes] * wdw_ref[k:k + 1, lanes]
            t_ref[:, pl.ds(w0, 8), lanes] = acc
        return carry

    lax.fori_loop(0, GRID_W // 8, body, 0)
    t = t_ref[...].reshape(s, D_CF)
    ycf_ref[0] = _conv_tail(t, bdw_ref, lng_ref, lnb_ref).astype(ycf_ref.dtype)


def _conv_seq_kernel(bg_ref, cv_ref, gl_ref, wsc_ref, wdw_ref, bdw_ref, lng_ref, lnb_ref,
                     ysc_ref, ycf_ref, pad_ref):
    s = cv_ref.shape[1]
    cv = cv_ref[0]
    pos = lax.broadcasted_iota(jnp.int32, (s, D_SC), 0)
    prev = jnp.where(pos == 0, 0.0, pltpu.roll(cv, 1, axis=0))
    nxt = jnp.where(pos == s - 1, 0.0, pltpu.roll(cv, s - 1, axis=0))
    conv = prev * wsc_ref[0:1, :] + cv * wsc_ref[1:2, :] + nxt * wsc_ref[2:3, :]
    ysc_ref[0] = (bg_ref[0] * conv).astype(ysc_ref.dtype)

    off = 16
    pad_ref[0:off] = jnp.zeros((off, D_CF), F32)
    pad_ref[off + s:off + s + 16] = jnp.zeros((16, D_CF), F32)
    pad_ref[off:off + s] = gl_ref[0]
    acc = jnp.zeros((s, D_CF), F32)
    for k in range(CF_WIDTH):
        acc = acc + pad_ref[pl.ds(off - CF_HALF + k, s), :] * wdw_ref[k:k + 1, :]
    ycf_ref[0] = _conv_tail(acc, bdw_ref, lng_ref, lnb_ref).astype(ycf_ref.dtype)


def _conv_call(bg, cv, gl, w_sc, w_dw, b_dw, ln_g, ln_b, grid_mode):
    b, s, _ = bg.shape
    row_spec = pl.BlockSpec((1, s, D_SC), lambda i: (i, 0, 0))
    full = lambda shape: pl.BlockSpec(shape, lambda i: (0,) * len(shape))
    if grid_mode:
        rows = s // GRID_W
        kern = _conv_grid_kernel
        scratch = [pltpu.VMEM((rows + 2 * CF_HALF, GRID_W, D_CF), F32),
                   pltpu.VMEM((rows, GRID_W, D_CF), F32)]
        name = "conv_grid"
    else:
        kern = _conv_seq_kernel
        scratch = [pltpu.VMEM((s + 32, D_CF), F32)]
        name = "conv_seq"
    return pl.pallas_call(
        kern, grid=(b,),
        in_specs=[row_spec, row_spec, row_spec, full((3, D_SC)), full((CF_WIDTH, D_CF)),
                  full((1, D_CF)), full((1, D_CF)), full((1, D_CF))],
        out_specs=[row_spec, row_spec],
        out_shape=[jax.ShapeDtypeStruct((b, s, D_SC), BF16), jax.ShapeDtypeStruct((b, s, D_CF), BF16)],
        scratch_shapes=scratch,
        compiler_params=_cparams(("parallel",)),
        name=name,
    )(bg, cv, gl, w_sc, w_dw, b_dw.reshape(1, -1), ln_g.reshape(1, -1), ln_b.reshape(1, -1))


def _split3(a):
    hi = a.astype(BF16)
    r = a - hi.astype(F32)
    mid = r.astype(BF16)
    lo = (r - mid.astype(F32)).astype(BF16)
    return hi, mid, lo


def _select_cols(a, sel):
    hi, mid, lo = _split3(a)
    return _dot(hi, sel) + (_dot(mid, sel) + _dot(lo, sel))


def _select_rows(sel, a):
    hi, mid, lo = _split3(a)
    return _dot(sel, hi) + (_dot(sel, mid) + _dot(sel, lo))


def _cmul(ar, ai, br, bi):
    return ar * br - ai * bi, ar * bi + ai * br


def _s5_prep_kernel(are_r, aim_r, are_c, aim_c, ldt, bre_ref, bim_ref, cre_ref, cim_ref,
                    w1_ref, e_ref, dec_ref):
    t, n, p = CHUNK, S5_GROUP, S5_STATE
    width = t * n
    lane_tok = lax.broadcasted_iota(jnp.int32, (128, width), 1) // n
    pow_id = lax.broadcasted_iota(jnp.int32, (128, width), 0)
    onehot = lambda cond: jnp.where(cond, 1.0, 0.0).astype(BF16)
    sel_fwd = onehot(pow_id == lane_tok)
    sel_rev = onehot(pow_id == t - 1 - lane_tok)
    sel_out = onehot(pow_id == t - lane_tok)
    row_tok = lax.broadcasted_iota(jnp.int32, (width, 32), 0) // n
    row_pow = lax.broadcasted_iota(jnp.int32, (width, 32), 1)
    rsel_rev = onehot(row_pow == t - 1 - row_tok)
    rsel_fwd = onehot(row_pow == row_tok)
    lane = lax.broadcasted_iota(jnp.int32, (n, width), 1)

    jc = jnp.minimum(lax.broadcasted_iota(jnp.int32, (p, 128), 1), t).astype(F32)
    jr = jnp.minimum(lax.broadcasted_iota(jnp.int32, (32, p), 0), t).astype(F32)

    strips = []
    f_parts = []
    e_parts = []
    for d in range(2):
        dt = jnp.exp(ldt[d, 0])
        mag_c = are_c[d, 0] * dt
        th_c = aim_c[d, 0] * dt
        ec = jnp.exp(jc * mag_c)
        qr = ec * jnp.cos(jc * th_c)
        qi = ec * jnp.sin(jc * th_c)
        a_re = are_r[d, 0]
        a_im = aim_r[d, 0]
        er = jnp.exp(jr * (a_re * dt))
        pr = er * jnp.cos(jr * (a_im * dt))
        pi = er * jnp.sin(jr * (a_im * dt))
        nr = pr[1:2] - 1.0
        ni = pi[1:2]
        den = a_re * a_re + a_im * a_im
        fre = (nr * a_re + ni * a_im) / den
        fim = (ni * a_re - nr * a_im) / den
        bt_re = bre_ref[d, 0].T
        bt_im = bim_ref[d, 0].T
        bb_re, bb_im = _cmul(fre, fim, bt_re, bt_im)
        ct_re = jnp.concatenate([cre_ref[d, 0].T] * t, axis=1)
        ct_im = jnp.concatenate([cim_ref[d, 0].T] * t, axis=1)
        sel = sel_fwd if d == 0 else sel_rev
        w_re, w_im = _cmul(ct_re, ct_im, _select_cols(qr, sel), _select_cols(qi, sel))
        strips.append(_dot3(bb_re, w_re) - _dot3(bb_im, w_im))
        if d == 0:
            o_re, o_im = _cmul(w_re, w_im, qr[:, 1:2], qi[:, 1:2])
        else:
            o_re, o_im = _cmul(ct_re, ct_im, _select_cols(qr, sel_out), _select_cols(qi, sel_out))
        e_parts += [o_re, -o_im]
        rsel = rsel_rev if d == 0 else rsel_fwd
        f_re, f_im = _cmul(jnp.concatenate([bb_re] * t, axis=0), jnp.concatenate([bb_im] * t, axis=0),
                           _select_rows(rsel, pr), _select_rows(rsel, pi))
        f_parts += [f_re, f_im]
        dec_ref[0, 2 * d:2 * d + 1, :] = jnp.concatenate([pr[t:t + 1], pr[t:t + 1]], axis=1)
        dec_ref[0, 2 * d + 1:2 * d + 2, :] = jnp.concatenate([-pi[t:t + 1], pi[t:t + 1]], axis=1)

    blocks = []
    for s in range(t):
        fwd = strips[0] if s == 0 else jnp.where(lane >= n * s, pltpu.roll(strips[0], n * s, axis=1), 0.0)
        back = t - 1 - s
        bwd = strips[1] if back == 0 else jnp.where(lane < width - n * back,
                                                     pltpu.roll(strips[1], width - n * back, axis=1), 0.0)
        blocks.append(fwd + bwd)
    m = jnp.concatenate(blocks, axis=0)
    w1_ref[0] = jnp.concatenate([m] + f_parts, axis=1).astype(BF16)
    e_ref[0] = jnp.concatenate(e_parts, axis=0).astype(BF16)


def _s5_operators(a_re, a_im, log_dt, b_re, b_im, c_re, c_im):
    g, p, n, t = S5_GROUPS, S5_STATE, S5_GROUP, CHUNK
    spec = lambda shape: pl.BlockSpec((2, 1) + shape, lambda i: (0, i) + (0,) * len(shape))
    return pl.pallas_call(
        _s5_prep_kernel, grid=(g,),
        in_specs=[spec((1, p)), spec((1, p)), spec((p, 1)), spec((p, 1)), spec((1, 1)),
                  spec((p, n)), spec((p, n)), spec((n, p)), spec((n, p))],
        out_specs=[pl.BlockSpec((1, t * n, 2 * t * n), lambda i: (i, 0, 0)),
                   pl.BlockSpec((1, 4 * p, t * n), lambda i: (i, 0, 0)),
                   pl.BlockSpec((1, 4, 2 * p), lambda i: (i, 0, 0))],
        out_shape=[jax.ShapeDtypeStruct((g, t * n, 2 * t * n), BF16),
                   jax.ShapeDtypeStruct((g, 4 * p, t * n), BF16),
                   jax.ShapeDtypeStruct((g, 4, 2 * p), F32)],
        compiler_params=_cparams(("parallel",)),
        name="s5_prep",
    )(a_re.reshape(2, g, 1, p), a_im.reshape(2, g, 1, p), a_re.reshape(2, g, p, 1),
      a_im.reshape(2, g, p, 1), log_dt.reshape(2, g, 1, 1), b_re, b_im, c_re, c_im)


def _block_transpose8(ps):
    ps = list(ps)
    blk = lax.broadcasted_iota(jnp.int32, ps[0].shape, 1) // S5_GROUP
    for k in range(3):
        step = 1 << k
        shift = S5_GROUP * step
        keep = ((blk >> k) & 1) == 0
        for a in range(8):
            if a & step:
                continue
            pa, pb = ps[a], ps[a + step]
            ps[a] = jnp.where(keep, pa, pltpu.roll(pb, shift, axis=1))
            ps[a + step] = jnp.where(keep, pltpu.roll(pa, 128 - shift, axis=1), pb)
    return ps


def _s5a_kernel(u_ref, w1_ref, yin_ref, gf_ref, gb_ref):
    nb = u_ref.shape[1]
    xs = []
    for s in range(CHUNK):
        parts = [u_ref[0, b, pl.ds(s, TILE_CHUNKS, stride=CHUNK), :] for b in range(nb)]
        xs.append(jnp.concatenate(parts, axis=0))
    lo = _block_transpose8(xs[:8])
    hi = _block_transpose8(xs[8:])
    for j in range(GRP_PER_BLK):
        og = jnp.concatenate([lo[j], hi[j]], axis=1)
        r = _dot(og.astype(BF16), w1_ref[j])
        yin_ref[j] = r[:, 0:256]
        gf_ref[j] = r[:, 256:384]
        gb_ref[j] = r[:, 384:512]


def _s5a_call(u, w1):
    n_blk, b, s, _ = u.shape
    nt = s // TILE_TOK
    rows = b * TILE_CHUNKS
    out_spec = lambda n: pl.BlockSpec((GRP_PER_BLK, rows, n), lambda l, j: (l, j, 0))
    return pl.pallas_call(
        _s5a_kernel, grid=(n_blk, nt),
        in_specs=[pl.BlockSpec((1, b, TILE_TOK, LANE_BLK), lambda l, j: (l, 0, j, 0)),
                  pl.BlockSpec((GRP_PER_BLK, 256, 512), lambda l, j: (l, 0, 0))],
        out_specs=[out_spec(256), out_spec(128), out_spec(128)],
        out_shape=[jax.ShapeDtypeStruct((S5_GROUPS, nt * rows, 256), F32),
                   jax.ShapeDtypeStruct((S5_GROUPS, nt * rows, 128), F32),
                   jax.ShapeDtypeStruct((S5_GROUPS, nt * rows, 128), F32)],
        compiler_params=_cparams(("parallel", "parallel")),
        name="s5_chunk_in",
    )(u, w1)


def _s5b_kernel(nb, a_ref, gfc_ref, gfl_ref, gbc_ref, gbl_ref, hfc_ref, hfl_ref, hbc_ref, hbl_ref):
    gb = a_ref.shape[0]
    rows = nb * TILE_CHUNKS
    n_lat = gfl_ref.shape[1] // rows
    a1f = [jnp.broadcast_to(a_ref[g, 0:1, :], (nb, 128)) for g in range(gb)]
    a2f = [jnp.broadcast_to(a_ref[g, 1:2, :], (nb, 128)) for g in range(gb)]
    a1b = [jnp.broadcast_to(a_ref[g, 2:3, :], (nb, 128)) for g in range(gb)]
    a2b = [jnp.broadcast_to(a_ref[g, 3:4, :], (nb, 128)) for g in range(gb)]

    def step(state, a1, a2, g_ref, h_ref, g, row):
        h, hs = state
        h_ref[g, row, :] = h
        inp = g_ref[g, row, :]
        return a1 * h + a2 * hs + inp, a1 * hs - a2 * h + pltpu.roll(inp, 64, axis=1)

    zero = jnp.zeros((nb, 128), F32)
    hf = [(zero, zero) for _ in range(gb)]
    hb = [(zero, zero) for _ in range(gb)]
    for ci in range(TILE_CHUNKS):
        rf = pl.ds(ci, nb, stride=TILE_CHUNKS)
        rb = pl.ds(TILE_CHUNKS - 1 - ci, nb, stride=TILE_CHUNKS)
        for g in range(gb):
            hf[g] = step(hf[g], a1f[g], a2f[g], gfc_ref, hfc_ref, g, rf)
            hb[g] = step(hb[g], a1b[g], a2b[g], gbc_ref, hbc_ref, g, rb)

    def body(j, carry):
        hf, hb = carry
        hf = list(hf)
        hb = list(hb)
        base_f = j * rows
        base_b = (n_lat - 1 - j) * rows
        for ci in range(TILE_CHUNKS):
            rf = pl.ds(base_f + ci, nb, stride=TILE_CHUNKS)
            rb = pl.ds(base_b + (TILE_CHUNKS - 1 - ci), nb, stride=TILE_CHUNKS)
            for g in range(gb):
                hf[g] = step(hf[g], a1f[g], a2f[g], gfl_ref, hfl_ref, g, rf)
                hb[g] = step(hb[g], a1b[g], a2b[g], gbl_ref, hbl_ref, g, rb)
        return tuple(hf), tuple(hb)

    lax.fori_loop(0, n_lat, body, (tuple(hf), tuple(hb)))


def _s5b_call(decay, gf_c, gf_l, gb_c, gb_l, nb):
    gblk = 4
    spec = lambda a: pl.BlockSpec((gblk, a.shape[1], 128), lambda i: (i, 0, 0))
    sds = lambda a: jax.ShapeDtypeStruct(a.shape, F32)
    return pl.pallas_call(
        functools.partial(_s5b_kernel, nb), grid=(S5_GROUPS // gblk,),
        in_specs=[pl.BlockSpec((gblk, 4, 128), lambda i: (i, 0, 0)),
                  spec(gf_c), spec(gf_l), spec(gb_c), spec(gb_l)],
        out_specs=[spec(gf_c), spec(gf_l), spec(gb_c), spec(gb_l)],
        out_shape=[sds(gf_c), sds(gf_l), sds(gb_c), sds(gb_l)],
        compiler_params=_cparams(("parallel",)),
        name="s5_state_scan",
    )(decay, gf_c, gf_l, gb_c, gb_l)


def _s5c_kernel(yin_ref, hf_ref, hb_ref, e_ref, u_ref, d_ref, y_ref):
    nb = u_ref.shape[1]
    ys = []
    for j in range(GRP_PER_BLK):
        h = jnp.concatenate([hf_ref[j], hb_ref[j]], axis=1).astype(BF16)
        ys.append(yin_ref[j] + _dot(h, e_ref[j]))
    at = (_block_transpose8([y[:, :128] for y in ys])
          + _block_transpose8([y[:, 128:] for y in ys]))
    d = d_ref[...]
    for t in range(CHUNK):
        for b in range(nb):
            rows = pl.ds(t, TILE_CHUNKS, stride=CHUNK)
            y_ref[0, b, rows, :] = at[t][b * TILE_CHUNKS:(b + 1) * TILE_CHUNKS] + d * u_ref[0, b, rows, :]


def _s5c_call(yin, hf, hb, e, u, d_skip):
    n_blk, b, s, _ = u.shape
    nt = s // TILE_TOK
    rows = b * TILE_CHUNKS
    gspec = lambda n: pl.BlockSpec((GRP_PER_BLK, rows, n), lambda l, j: (l, j, 0))
    tok_spec = pl.BlockSpec((1, b, TILE_TOK, LANE_BLK), lambda l, j: (l, 0, j, 0))
    return pl.pallas_call(
        _s5c_kernel, grid=(n_blk, nt),
        in_specs=[gspec(256), gspec(128), gspec(128),
                  pl.BlockSpec((GRP_PER_BLK, 256, 256), lambda l, j: (l, 0, 0)),
                  tok_spec,
                  pl.BlockSpec((1, LANE_BLK), lambda l, j: (0, l))],
        out_specs=tok_spec,
        out_shape=_lane_block_shape(b, s),
        compiler_params=_cparams(("parallel", "parallel")),
        name="s5_chunk_out",
    )(yin, hf, hb, e, u, d_skip.reshape(1, D_S5))


def _gelu_tanh(x):
    return 0.5 * x * (1.0 + jnp.tanh(math.sqrt(2.0 / math.pi) * (x + 0.044715 * (x * x * x))))


def _route(logits):
    lane = lax.broadcasted_iota(jnp.int32, logits.shape, 1).astype(F32)
    neg = jnp.float32(-1e30)
    big = jnp.float32(1e9)
    gl = jnp.where(lane < N_GROUPS, logits, neg)
    gmax = jnp.max(gl, axis=1, keepdims=True)
    gidx = jnp.min(jnp.where(gl == gmax, lane, big), axis=1, keepdims=True)
    gsum = jnp.sum(jnp.exp(gl - gmax), axis=1, keepdims=True)
    gw = 1.0 / gsum
    lo = N_GROUPS + EXP_PER_GROUP * gidx
    el = jnp.where((lane >= lo) & (lane < lo + EXP_PER_GROUP), logits, neg)
    v1 = jnp.max(el, axis=1, keepdims=True)
    i1 = jnp.min(jnp.where(el == v1, lane, big), axis=1, keepdims=True)
    el2 = jnp.where(lane == i1, neg, el)
    v2 = jnp.max(el2, axis=1, keepdims=True)
    i2 = jnp.min(jnp.where(el2 == v2, lane, big), axis=1, keepdims=True)
    ex = jnp.exp(v2 - v1)
    p1 = 1.0 / (1.0 + ex)
    p2 = ex * p1
    e1 = i1 - lo
    e2 = i2 - lo
    first = e1 < e2
    ea = jnp.where(first, e1, e2)
    eb = jnp.where(first, e2, e1)
    wa = gw * jnp.where(first, p1, p2)
    wb = gw * jnp.where(first, p2, p1)
    pair = ea * (7.0 - ea) * 0.5 + (eb - ea - 1.0)
    return wa, wb, 6.0 * gidx + pair


def _out_kernel(ypre_ref, ysc_ref, ycf_ref, x_ref, mod_ref, wglu_ref, bglu_ref, wo_ref,
                lng_ref, lnb_ref, wr_ref, br_ref, x1_ref, hx_ref, meta_ref, counts_ref, cnt_ref):
    @pl.when((pl.program_id(0) == 0) & (pl.program_id(1) == 0))
    def _():
        cnt_ref[...] = jnp.zeros_like(cnt_ref)

    ypre = jnp.concatenate([ypre_ref[blk, 0] for blk in range(ypre_ref.shape[0])], axis=1)
    t = _gelu_tanh(ypre)
    gate = _sigmoid(_dot(t.astype(BF16), wglu_ref[...]) + bglu_ref[...])
    ys5 = (t * gate).astype(BF16)
    y = (_dot(ys5, wo_ref[0:D_S5, :]) + _dot(ysc_ref[0], wo_ref[D_S5:D_S5 + D_SC, :])
         + _dot(ycf_ref[0], wo_ref[D_S5 + D_SC:D_MODEL, :]))
    g1 = mod_ref[0, 2:3, :]
    x1 = _layer_norm(DN_ALPHA * x_ref[0] + g1 * y, lng_ref[...], lnb_ref[...])
    x1_ref[0] = x1
    h2 = x1 * (1.0 + mod_ref[0, 4:5, :]) + mod_ref[0, 3:4, :]
    wa, wb, cls = _route(_dot(h2.astype(BF16), wr_ref[...]) + br_ref[...])

    tm = h2.shape[0]
    lane = lax.broadcasted_iota(jnp.int32, (tm, ROUTER_LANES), 1).astype(F32)
    onehot = jnp.where(lane == cls, 1.0, 0.0)
    row_i = lax.broadcasted_iota(jnp.int32, (tm, tm), 0)
    col_i = lax.broadcasted_iota(jnp.int32, (tm, tm), 1)
    earlier = jnp.where(col_i < row_i, 1.0, 0.0).astype(BF16)
    before = _dot(earlier, onehot.astype(BF16)) + cnt_ref[...]
    rank = jnp.sum(before * onehot, axis=1, keepdims=True)
    cnt_ref[...] += jnp.sum(onehot, axis=0, keepdims=True)
    counts_ref[...] = cnt_ref[...]

    meta = (jnp.where(lane == META_WA, wa, 0.0) + jnp.where(lane == META_WB, wb, 0.0)
            + jnp.where(lane == META_CLS, cls, 0.0) + jnp.where(lane == META_RANK, rank, 0.0))
    meta_ref[0] = meta
    hx_ref[0, :, 0:D_MODEL] = h2
    hx_ref[0, :, D_MODEL:HX_LANES] = meta


def _out_call(ypre, ysc, ycf, x, mod, wglu_bf, b_glu, wo_bf, ln_g, ln_b, w_router, b_router, tm):
    b, s, d = x.shape
    row_spec = lambda n: pl.BlockSpec((1, tm, n), lambda i, j: (i, j, 0))
    full = lambda shape: pl.BlockSpec(shape, lambda i, j: (0,) * len(shape))
    return pl.pallas_call(
        _out_kernel, grid=(b, s // tm),
        in_specs=[_lane_block_spec(tm), row_spec(D_SC), row_spec(D_CF), row_spec(d),
                  pl.BlockSpec((1, 6, d), lambda i, j: (i, 0, 0)),
                  full((D_S5, D_S5)), full((1, D_S5)), full((d, d)),
                  full((1, d)), full((1, d)), full((d, ROUTER_LANES)), full((1, ROUTER_LANES))],
        out_specs=[row_spec(d), row_spec(HX_LANES), row_spec(ROUTER_LANES), full((1, ROUTER_LANES))],
        out_shape=[jax.ShapeDtypeStruct((b, s, d), F32), jax.ShapeDtypeStruct((b, s, HX_LANES), F32),
                   jax.ShapeDtypeStruct((b, s, ROUTER_LANES), F32),
                   jax.ShapeDtypeStruct((1, ROUTER_LANES), F32)],
        scratch_shapes=[pltpu.VMEM((1, ROUTER_LANES), F32)],
        compiler_params=_cparams(("arbitrary", "arbitrary")),
        name="out_proj",
    )(ypre, ysc, ycf, x, mod, wglu_bf, b_glu.reshape(1, -1), wo_bf, ln_g.reshape(1, -1),
      ln_b.reshape(1, -1), w_router, b_router)


def _moe_plan(meta, counts, n_tok):
    cls = meta[..., META_CLS].reshape(-1).astype(jnp.int32)
    rank = meta[..., META_RANK].reshape(-1).astype(jnp.int32)
    cnt = counts[0, :N_CLASSES].astype(jnp.int32)
    n_tiles = (cnt + (MOE_TM - 1)) // MOE_TM
    ends = jnp.cumsum(n_tiles)
    starts = ends - n_tiles
    slot = starts[cls] * MOE_TM + rank
    t_max = n_tok // MOE_TM + N_CLASSES
    n_used = ends[N_CLASSES - 1]
    tile = jnp.minimum(jnp.arange(t_max, dtype=jnp.int32), n_used - 1)
    tile_cls = jnp.sum((tile[:, None] >= ends[None, :]).astype(jnp.int32), axis=1)
    group = tile_cls // 6
    pair = tile_cls % 6
    first = jnp.array([0, 0, 0, 1, 1, 2], jnp.int32)[pair] + EXP_PER_GROUP * group
    second = jnp.array([1, 2, 3, 2, 3, 3], jnp.int32)[pair] + EXP_PER_GROUP * group
    return slot, tile, first, second, n_used.reshape(1)


def _split_row(row):
    return lax.shift_right_logical(row, 3), lax.bitwise_and(row, SUBLANES - 1)


def _dispatch_kernel(slot_ref, hx_ref, xs_init_ref, xs_ref, sem):
    del xs_init_ref
    n_oct = hx_ref.shape[1]
    base = (pl.program_id(0) * pl.num_programs(1) + pl.program_id(1)) * (n_oct * SUBLANES)

    def body(i, carry):
        for k in range(SUBLANES):
            oct_id, sub = _split_row(slot_ref[base + i * SUBLANES + k])
            pltpu.make_async_copy(hx_ref.at[0, i, pl.ds(k, 1), :], xs_ref.at[oct_id, pl.ds(sub, 1), :],
                                  sem).start(priority=k % 2)
        return carry

    lax.fori_loop(0, n_oct, body, 0)
    pltpu.make_async_copy(hx_ref.at[0], xs_ref.at[pl.ds(0, n_oct)], sem).wait()


def _dispatch_call(slot, hx, n_rows, tm):
    b, s, w = hx.shape
    xs_init = jnp.zeros((n_rows // SUBLANES, SUBLANES, w), F32)
    grid_spec = pltpu.PrefetchScalarGridSpec(
        num_scalar_prefetch=1, grid=(b, s // tm),
        in_specs=[pl.BlockSpec((1, tm // SUBLANES, SUBLANES, w), lambda i, j, slot: (i, j, 0, 0)),
                  pl.BlockSpec(memory_space=pl.ANY)],
        out_specs=pl.BlockSpec(memory_space=pl.ANY),
        scratch_shapes=[pltpu.SemaphoreType.DMA(())])
    xs = pl.pallas_call(
        _dispatch_kernel, grid_spec=grid_spec,
        out_shape=jax.ShapeDtypeStruct(xs_init.shape, F32),
        input_output_aliases={2: 0},
        compiler_params=_cparams(("arbitrary", "arbitrary")),
        name="moe_dispatch",
    )(slot, hx.reshape(b, s // SUBLANES, SUBLANES, w), xs_init)
    return xs.reshape(n_rows, w)


def _moe_kernel(tile_ref, first_ref, second_ref, nused_ref, xs_ref, wga_ref, wgb_ref, wua_ref, wub_ref,
                wda_ref, wdb_ref, ys_ref):
    del tile_ref, first_ref, second_ref
    t = pl.program_id(0)
    n_used = nused_ref[0]

    @pl.when(t < n_used)
    def _():
        x = xs_ref[...]
        xb = x[:, 0:D_MODEL].astype(BF16)

        def expert(wg_ref, wu_ref, wd_ref, w):
            gate = _dot(xb, wg_ref[0])
            up = _dot(xb, wu_ref[0])
            act = gate * _sigmoid(gate) * up * w
            return _dot(act.astype(BF16), wd_ref[0])

        wa = x[:, D_MODEL + META_WA:D_MODEL + META_WA + 1]
        wb = x[:, D_MODEL + META_WB:D_MODEL + META_WB + 1]
        ys_ref[...] = expert(wga_ref, wua_ref, wda_ref, wa) + expert(wgb_ref, wub_ref, wdb_ref, wb)

    @pl.when(t >= n_used)
    def _():
        ys_ref[...] = jnp.zeros_like(ys_ref)


def _moe_call(tile, first, second, n_used, xs, wg_bf, wu_bf, wd_bf):
    n_rows, w = xs.shape
    d = D_MODEL
    t_max = tile.shape[0]
    up_spec = lambda sel: pl.BlockSpec((1, d, D_EXPERT), lambda t, tl, fi, se, nu: ((fi, se)[sel][t], 0, 0))
    down_spec = lambda sel: pl.BlockSpec((1, D_EXPERT, d), lambda t, tl, fi, se, nu: ((fi, se)[sel][t], 0, 0))
    grid_spec = pltpu.PrefetchScalarGridSpec(
        num_scalar_prefetch=4, grid=(t_max,),
        in_specs=[pl.BlockSpec((MOE_TM, w), lambda t, tl, fi, se, nu: (tl[t], 0)),
                  up_spec(0), up_spec(1), up_spec(0), up_spec(1), down_spec(0), down_spec(1)],
        out_specs=pl.BlockSpec((MOE_TM, d), lambda t, tl, fi, se, nu: (t, 0)))
    return pl.pallas_call(
        _moe_kernel, grid_spec=grid_spec,
        out_shape=jax.ShapeDtypeStruct((n_rows, d), F32),
        compiler_params=_cparams(("arbitrary",)),
        name="moe_experts",
    )(tile, first, second, n_used, xs, wg_bf, wg_bf, wu_bf, wu_bf, wd_bf, wd_bf)


def _combine_kernel(slot_ref, x1_ref, mod_ref, lng_ref, lnb_ref, ys_ref, o_ref, f_ref, sem):
    n_oct = f_ref.shape[1]
    tm = n_oct * SUBLANES
    step = pl.program_id(0) * pl.num_programs(1) + pl.program_id(1)
    n_steps = pl.num_programs(0) * pl.num_programs(1)

    def request(which, buf):
        base = which * tm

        def body(i, carry):
            for k in range(SUBLANES):
                oct_id, sub = _split_row(slot_ref[base + i * SUBLANES + k])
                pltpu.make_async_copy(ys_ref.at[oct_id, pl.ds(sub, 1), :], f_ref.at[buf, i, pl.ds(k, 1), :],
                                      sem.at[buf]).start(priority=k % 2)
            return carry

        lax.fori_loop(0, n_oct, body, 0)

    @pl.when(step == 0)
    def _():
        request(0, 0)

    @pl.when(step + 1 < n_steps)
    def _():
        request(step + 1, (step + 1) % 2)

    buf = step % 2
    pltpu.make_async_copy(ys_ref.at[pl.ds(0, n_oct)], f_ref.at[buf], sem.at[buf]).wait()
    f = f_ref[buf].reshape(tm, f_ref.shape[3])
    g2 = mod_ref[0, 5:6, :]
    o_ref[0] = _layer_norm(DN_ALPHA * x1_ref[0] + g2 * f, lng_ref[...], lnb_ref[...])


def _combine_call(slot, x1, mod, ln_g, ln_b, ys, tm):
    b, s, d = x1.shape
    ys = ys.reshape(ys.shape[0] // SUBLANES, SUBLANES, d)
    grid_spec = pltpu.PrefetchScalarGridSpec(
        num_scalar_prefetch=1, grid=(b, s // tm),
        in_specs=[pl.BlockSpec((1, tm, d), lambda i, j, slot: (i, j, 0)),
                  pl.BlockSpec((1, 6, d), lambda i, j, slot: (i, 0, 0)),
                  pl.BlockSpec((1, d), lambda i, j, slot: (0, 0)),
                  pl.BlockSpec((1, d), lambda i, j, slot: (0, 0)),
                  pl.BlockSpec(memory_space=pl.ANY)],
        out_specs=pl.BlockSpec((1, tm, d), lambda i, j, slot: (i, j, 0)),
        scratch_shapes=[pltpu.VMEM((2, tm // SUBLANES, SUBLANES, d), F32), pltpu.SemaphoreType.DMA((2,))])
    return pl.pallas_call(
        _combine_kernel, grid_spec=grid_spec,
        out_shape=jax.ShapeDtypeStruct((b, s, d), F32),
        compiler_params=_cparams(("arbitrary", "arbitrary")),
        name="moe_combine",
    )(slot, x1, mod, ln_g.reshape(1, -1), ln_b.reshape(1, -1), ys)


def _moe_sublayer(hx, meta, counts, x1, mod, wg_bf, wu_bf, wd_bf, ln_g, ln_b, tm):
    b, s, _ = x1.shape
    n_tok = b * s
    slot, tile, first, second, n_used = _moe_plan(meta, counts, n_tok)
    n_rows = n_tok + N_CLASSES * MOE_TM
    xs = _dispatch_call(slot, hx, n_rows, tm)
    ys = _moe_call(tile, first, second, n_used, xs, wg_bf, wu_bf, wd_bf)
    return _combine_call(slot, x1, mod, ln_g, ln_b, ys, tm)


def kernel(x, c, ctx, c_ctx, w_mod, b_mod, w_in, s5_a_re, s5_a_im, s5_log_dt, s5_b_re, s5_b_im, s5_c_re, s5_c_im, s5_d, w_glu, b_glu, w_sc, w_dw, b_dw, ln_cf_g, ln_cf_b, w_o, ln1_g, ln1_b, w_rg, b_rg, w_rexp, b_rexp, w_gate, w_up, w_down, ln2_g, ln2_b):
    nb, seq, d = x.shape
    n_ctx = ctx.shape[1]
    n_layers = w_mod.shape[0]
    assert seq % TILE_TOK == 0 and n_ctx % TILE_TOK == 0 and seq % GRID_W == 0

    mod_rows = 16
    assert nb + 1 <= mod_rows
    c_all = jnp.concatenate([c, c_ctx[None, :], jnp.zeros((mod_rows - nb - 1, d), F32)], axis=0)
    mod_all = _mod_call(c_all, w_mod, b_mod)

    pad_r = ROUTER_LANES - N_GROUPS - N_EXPERTS
    x_lat, x_ctx = x, ctx
    for l in range(n_layers):
        last = l == n_layers - 1
        mod_lat = mod_all[l, :nb].reshape(nb, 6, d)
        mod_ctx = jnp.broadcast_to(mod_all[l, nb].reshape(1, 6, d), (nb, 6, d))
        w_in_bf = w_in[l].astype(BF16)
        wglu_bf = w_glu[l].astype(BF16)
        wo_bf = w_o[l].astype(BF16)
        wg_bf = w_gate[l].astype(BF16)
        wu_bf = w_up[l].astype(BF16)
        wd_bf = w_down[l].astype(BF16)
        w_router = jnp.concatenate([w_rg[l], w_rexp[l], jnp.zeros((d, pad_r), F32)], axis=1).astype(BF16)
        b_router = jnp.concatenate([b_rg[l], b_rexp[l], jnp.zeros((pad_r,), F32)]).reshape(1, -1)
        w1, e_op, decay = _s5_operators(s5_a_re[l], s5_a_im[l], s5_log_dt[l], s5_b_re[l], s5_b_im[l],
                                        s5_c_re[l], s5_c_im[l])

        u_l, bg_l, cv_l, gl_l = _in_call(x_lat, mod_lat, w_in_bf, 512, False)
        if last:
            u_c = _in_call(x_ctx, mod_ctx, w_in_bf[:, :D_S5], TILE_TOK, True)
        else:
            u_c, bg_c, cv_c, gl_c = _in_call(x_ctx, mod_ctx, w_in_bf, TILE_TOK, False)

        yin_l, gf_l, gb_l = _s5a_call(u_l, w1)
        yin_c, gf_c, gb_c = _s5a_call(u_c, w1)
        hf_c, hf_l, hb_c, hb_l = _s5b_call(decay, gf_c, gf_l, gb_c, gb_l, nb)
        ypre_l = _s5c_call(yin_l, hf_l, hb_l, e_op, u_l, s5_d[l])

        ysc_l, ycf_l = _conv_call(bg_l, cv_l, gl_l, w_sc[l], w_dw[l], b_dw[l], ln_cf_g[l], ln_cf_b[l], True)
        x1_l, hx_l, meta_l, cnt_l = _out_call(ypre_l, ysc_l, ycf_l, x_lat, mod_lat, wglu_bf, b_glu[l], wo_bf,
                                              ln1_g[l], ln1_b[l], w_router, b_router, 512)
        if not last:
            ypre_c = _s5c_call(yin_c, hf_c, hb_c, e_op, u_c, s5_d[l])
            ysc_c, ycf_c = _conv_call(bg_c, cv_c, gl_c, w_sc[l], w_dw[l], b_dw[l], ln_cf_g[l], ln_cf_b[l], False)
            x1_c, hx_c, meta_c, cnt_c = _out_call(ypre_c, ysc_c, ycf_c, x_ctx, mod_ctx, wglu_bf, b_glu[l], wo_bf,
                                                  ln1_g[l], ln1_b[l], w_router, b_router, TILE_TOK)
            x_ctx = _moe_sublayer(hx_c, meta_c, cnt_c, x1_c, mod_ctx, wg_bf, wu_bf, wd_bf,
                                  ln2_g[l], ln2_b[l], TILE_TOK)
        x_lat = _moe_sublayer(hx_l, meta_l, cnt_l, x1_l, mod_lat, wg_bf, wu_bf, wd_bf,
                              ln2_g[l], ln2_b[l], 512)
    return x_lat
```

```python
import functools
import math

import jax
import jax.numpy as jnp
from jax import lax
from jax.experimental import pallas as pl
from jax.experimental.pallas import tpu as pltpu

F32 = jnp.float32
BF16 = jnp.bfloat16

D_MODEL = 1024
DEPTH = 2
GRID_W = 64
D_S5 = 512
S5_GROUP = 16
S5_GROUPS = 32
S5_STATE = 64
D_SC = 256
D_CF = 256
CF_WIDTH = 31
CF_HALF = 15
D_IN = 1792
N_GROUPS = 4
EXP_PER_GROUP = 4
N_EXPERTS = 16
D_EXPERT = 256
DN_ALPHA = (2 * DEPTH) ** 0.25
LN_EPS = 1e-5

CHUNK = 16
TILE_CHUNKS = 16
TILE_TOK = CHUNK * TILE_CHUNKS
LANE_BLK = 128
GRP_PER_BLK = LANE_BLK // S5_GROUP
ROUTER_LANES = 128
HX_LANES = D_MODEL + ROUTER_LANES
META_WA, META_WB, META_CLS, META_RANK = 0, 1, 2, 3
N_CLASSES = N_GROUPS * 6
MOE_TM = 256
SUBLANES = 8
VMEM_LIMIT = 56 * 1024 * 1024


def _cparams(sem):
    return pltpu.CompilerParams(dimension_semantics=sem, vmem_limit_bytes=VMEM_LIMIT)


def _split_bf16(a):
    hi = a.astype(BF16)
    lo = (a - hi.astype(F32)).astype(BF16)
    return hi, lo


def _dot(a, b):
    return jnp.dot(a, b, preferred_element_type=F32)


def _dot3(a, b):
    ah, al = _split_bf16(a)
    bh, bl = _split_bf16(b)
    return _dot(ah, bh) + (_dot(al, bh) + _dot(ah, bl))


def _sigmoid(x):
    return 1.0 / (1.0 + jnp.exp(-x))


def _layer_norm(x, g, b):
    mu = jnp.mean(x, axis=-1, keepdims=True)
    xc = x - mu
    var = jnp.mean(xc * xc, axis=-1, keepdims=True)
    return xc * lax.rsqrt(var + LN_EPS) * g + b


def _mod_kernel(c_ref, w_ref, b_ref, o_ref):
    c = c_ref[...]
    s = c * _sigmoid(c)
    o_ref[0] = _dot3(s, w_ref[0]) + b_ref[0]


def _mod_call(c_all, w_mod, b_mod):
    n_layers, d, n_out = w_mod.shape
    tn = 1536
    rows = c_all.shape[0]
    return pl.pallas_call(
        _mod_kernel,
        grid=(n_layers, n_out // tn),
        in_specs=[
            pl.BlockSpec((rows, d), lambda l, j: (0, 0)),
            pl.BlockSpec((1, d, tn), lambda l, j: (l, 0, j)),
            pl.BlockSpec((1, 1, tn), lambda l, j: (l, 0, j)),
        ],
        out_specs=pl.BlockSpec((1, rows, tn), lambda l, j: (l, 0, j)),
        out_shape=jax.ShapeDtypeStruct((n_layers, rows, n_out), F32),
        compiler_params=_cparams(("parallel", "parallel")),
        name="mod",
    )(c_all, w_mod, b_mod.reshape(n_layers, 1, n_out))


def _in_kernel(x_ref, mod_ref, w_ref, u_ref, bg_ref, cv_ref, gl_ref):
    x = x_ref[0]
    sh = mod_ref[0, 0:1, :]
    sc = mod_ref[0, 1:2, :]
    h = (x * (1.0 + sc) + sh).astype(BF16)
    z = _dot(h, w_ref[...])
    _store_lane_blocks(u_ref, z[:, 0:512])
    bg_ref[0] = z[:, 512:768]
    cv_ref[0] = z[:, 768:1024] * z[:, 1024:1280]
    gl_ref[0] = z[:, 1280:1536] * _sigmoid(z[:, 1536:1792])


def _in_u_kernel(x_ref, mod_ref, w_ref, u_ref):
    x = x_ref[0]
    sh = mod_ref[0, 0:1, :]
    sc = mod_ref[0, 1:2, :]
    h = (x * (1.0 + sc) + sh).astype(BF16)
    _store_lane_blocks(u_ref, _dot(h, w_ref[...]))


def _store_lane_blocks(ref, val):
    for blk in range(ref.shape[0]):
        ref[blk, 0] = val[:, blk * LANE_BLK:(blk + 1) * LANE_BLK]


def _lane_block_spec(tm):
    return pl.BlockSpec((D_S5 // LANE_BLK, 1, tm, LANE_BLK), lambda i, j: (0, i, j, 0))


def _lane_block_shape(b, s):
    return jax.ShapeDtypeStruct((D_S5 // LANE_BLK, b, s, LANE_BLK), F32)


def _in_call(x, mod, w_in_bf, tm, u_only):
    b, s, d = x.shape
    grid = (b, s // tm)
    row_spec = lambda n: pl.BlockSpec((1, tm, n), lambda i, j: (i, j, 0))
    in_specs = [
        row_spec(d),
        pl.BlockSpec((1, 6, d), lambda i, j: (i, 0, 0)),
    ]
    if u_only:
        in_specs.append(pl.BlockSpec((d, D_S5), lambda i, j: (0, 0)))
        return pl.pallas_call(
            _in_u_kernel, grid=grid, in_specs=in_specs,
            out_specs=_lane_block_spec(tm),
            out_shape=_lane_block_shape(b, s),
            compiler_params=_cparams(("parallel", "parallel")),
            name="in_proj_u",
        )(x, mod, w_in_bf)
    in_specs.append(pl.BlockSpec((d, D_IN), lambda i, j: (0, 0)))
    return pl.pallas_call(
        _in_kernel, grid=grid, in_specs=in_specs,
        out_specs=[_lane_block_spec(tm), row_spec(D_SC), row_spec(D_SC), row_spec(D_CF)],
        out_shape=[_lane_block_shape(b, s),
                   jax.ShapeDtypeStruct((b, s, D_SC), F32),
                   jax.ShapeDtypeStruct((b, s, D_SC), F32),
                   jax.ShapeDtypeStruct((b, s, D_CF), F32)],
        compiler_params=_cparams(("parallel", "parallel")),
        name="in_proj",
    )(x, mod, w_in_bf)


def _conv_tail(t, bdw_ref, lng_ref, lnb_ref):
    t = t + bdw_ref[...]
    t = _layer_norm(t, lng_ref[...], lnb_ref[...])
    return t * _sigmoid(t)


def _conv_grid_kernel(bg_ref, cv_ref, gl_ref, wsc_ref, wdw_ref, bdw_ref, lng_ref, lnb_ref,
                      ysc_ref, ycf_ref, pad_ref, t_ref):
    s = cv_ref.shape[1]
    rows = s // GRID_W
    cv = cv_ref[0]
    col = lax.broadcasted_iota(jnp.int32, (s, D_SC), 0) % GRID_W
    prev = jnp.where(col == 0, 0.0, pltpu.roll(cv, 1, axis=0))
    nxt = jnp.where(col == GRID_W - 1, 0.0, pltpu.roll(cv, s - 1, axis=0))
    conv = prev * wsc_ref[0:1, :] + cv * wsc_ref[1:2, :] + nxt * wsc_ref[2:3, :]
    ysc_ref[0] = (bg_ref[0] * conv).astype(ysc_ref.dtype)

    zero = jnp.zeros((CF_HALF, GRID_W, D_CF), F32)
    pad_ref[0:CF_HALF] = zero
    pad_ref[CF_HALF + rows:CF_HALF + rows + CF_HALF] = zero
    pad_ref[CF_HALF:CF_HALF + rows] = gl_ref[0].reshape(rows, GRID_W, D_CF)

    def body(i, carry):
        w0 = pl.multiple_of(i * 8, 8)
        for half in range(D_CF // 128):
            lanes = slice(half * 128, (half + 1) * 128)
            acc = jnp.zeros((rows, 8, 128), F32)
            for k in range(CF_WIDTH):
                acc = acc + pad_ref[k:k + rows, pl.ds(w0, 8), lanes] * wdw_ref[k:k + 1, lanes]
            t_ref[:, pl.ds(w0, 8), lanes] = acc
        return carry

    lax.fori_loop(0, GRID_W // 8, body, 0)
    t = t_ref[...].reshape(s, D_CF)
    ycf_ref[0] = _conv_tail(t, bdw_ref, lng_ref, lnb_ref).astype(ycf_ref.dtype)


def _conv_seq_kernel(bg_ref, cv_ref, gl_ref, wsc_ref, wdw_ref, bdw_ref, lng_ref, lnb_ref,
                     ysc_ref, ycf_ref, pad_ref):
    s = cv_ref.shape[1]
    cv = cv_ref[0]
    pos = lax.broadcasted_iota(jnp.int32, (s, D_SC), 0)
    prev = jnp.where(pos == 0, 0.0, pltpu.roll(cv, 1, axis=0))
    nxt = jnp.where(pos == s - 1, 0.0, pltpu.roll(cv, s - 1, axis=0))
    conv = prev * wsc_ref[0:1, :] + cv * wsc_ref[1:2, :] + nxt * wsc_ref[2:3, :]
    ysc_ref[0] = (bg_ref[0] * conv).astype(ysc_ref.dtype)

    off = 16
    pad_ref[0:off] = jnp.zeros((off, D_CF), F32)
    pad_ref[off + s:off + s + 16] = jnp.zeros((16, D_CF), F32)
    pad_ref[off:off + s] = gl_ref[0]
    acc = jnp.zeros((s, D_CF), F32)
    for k in range(CF_WIDTH):
        acc = acc + pad_ref[pl.ds(off - CF_HALF + k, s), :] * wdw_ref[k:k + 1, :]
    ycf_ref[0] = _conv_tail(acc, bdw_ref, lng_ref, lnb_ref).astype(ycf_ref.dtype)


def _conv_call(bg, cv, gl, w_sc, w_dw, b_dw, ln_g, ln_b, grid_mode):
    b, s, _ = bg.shape
    row_spec = pl.BlockSpec((1, s, D_SC), lambda i: (i, 0, 0))
    full = lambda shape: pl.BlockSpec(shape, lambda i: (0,) * len(shape))
    if grid_mode:
        rows = s // GRID_W
        kern = _conv_grid_kernel
        scratch = [pltpu.VMEM((rows + 2 * CF_HALF, GRID_W, D_CF), F32),
                   pltpu.VMEM((rows, GRID_W, D_CF), F32)]
        name = "conv_grid"
    else:
        kern = _conv_seq_kernel
        scratch = [pltpu.VMEM((s + 32, D_CF), F32)]
        name = "conv_seq"
    return pl.pallas_call(
        kern, grid=(b,),
        in_specs=[row_spec, row_spec, row_spec, full((3, D_SC)), full((CF_WIDTH, D_CF)),
                  full((1, D_CF)), full((1, D_CF)), full((1, D_CF))],
        out_specs=[row_spec, row_spec],
        out_shape=[jax.ShapeDtypeStruct((b, s, D_SC), BF16), jax.ShapeDtypeStruct((b, s, D_CF), BF16)],
        scratch_shapes=scratch,
        compiler_params=_cparams(("parallel",)),
        name=name,
    )(bg, cv, gl, w_sc, w_dw, b_dw.reshape(1, -1), ln_g.reshape(1, -1), ln_b.reshape(1, -1))


def _split3(a):
    hi = a.astype(BF16)
    r = a - hi.astype(F32)
    mid = r.astype(BF16)
    lo = (r - mid.astype(F32)).astype(BF16)
    return hi, mid, lo


def _select_cols(a, sel):
    hi, mid, lo = _split3(a)
    return _dot(hi, sel) + (_dot(mid, sel) + _dot(lo, sel))


def _select_rows(sel, a):
    hi, mid, lo = _split3(a)
    return _dot(sel, hi) + (_dot(sel, mid) + _dot(sel, lo))


def _cmul(ar, ai, br, bi):
    return ar * br - ai * bi, ar * bi + ai * br


def _s5_prep_kernel(are_r, aim_r, are_c, aim_c, ldt, bre_ref, bim_ref, cre_ref, cim_ref,
                    w1_ref, e_ref, dec_ref):
    t, n, p = CHUNK, S5_GROUP, S5_STATE
    width = t * n
    lane_tok = lax.broadcasted_iota(jnp.int32, (128, width), 1) // n
    pow_id = lax.broadcasted_iota(jnp.int32, (128, width), 0)
    onehot = lambda cond: jnp.where(cond, 1.0, 0.0).astype(BF16)
    sel_fwd = onehot(pow_id == lane_tok)
    sel_rev = onehot(pow_id == t - 1 - lane_tok)
    sel_out = onehot(pow_id == t - lane_tok)
    row_tok = lax.broadcasted_iota(jnp.int32, (width, 32), 0) // n
    row_pow = lax.broadcasted_iota(jnp.int32, (width, 32), 1)
    rsel_rev = onehot(row_pow == t - 1 - row_tok)
    rsel_fwd = onehot(row_pow == row_tok)
    lane = lax.broadcasted_iota(jnp.int32, (n, width), 1)

    jc = jnp.minimum(lax.broadcasted_iota(jnp.int32, (p, 128), 1), t).astype(F32)
    jr = jnp.minimum(lax.broadcasted_iota(jnp.int32, (32, p), 0), t).astype(F32)

    strips = []
    f_parts = []
    e_parts = []
    for d in range(2):
        dt = jnp.exp(ldt[d, 0])
        mag_c = are_c[d, 0] * dt
        th_c = aim_c[d, 0] * dt
        ec = jnp.exp(jc * mag_c)
        qr = ec * jnp.cos(jc * th_c)
        qi = ec * jnp.sin(jc * th_c)
        a_re = are_r[d, 0]
        a_im = aim_r[d, 0]
        er = jnp.exp(jr * (a_re * dt))
        pr = er * jnp.cos(jr * (a_im * dt))
        pi = er * jnp.sin(jr * (a_im * dt))
        nr = pr[1:2] - 1.0
        ni = pi[1:2]
        den = a_re * a_re + a_im * a_im
        fre = (nr * a_re + ni * a_im) / den
        fim = (ni * a_re - nr * a_im) / den
        bt_re = bre_ref[d, 0].T
        bt_im = bim_ref[d, 0].T
        bb_re, bb_im = _cmul(fre, fim, bt_re, bt_im)
        ct_re = jnp.concatenate([cre_ref[d, 0].T] * t, axis=1)
        ct_im = jnp.concatenate([cim_ref[d, 0].T] * t, axis=1)
        sel = sel_fwd if d == 0 else sel_rev
        w_re, w_im = _cmul(ct_re, ct_im, _select_cols(qr, sel), _select_cols(qi, sel))
        strips.append(_dot3(bb_re, w_re) - _dot3(bb_im, w_im))
        if d == 0:
            o_re, o_im = _cmul(w_re, w_im, qr[:, 1:2], qi[:, 1:2])
        else:
            o_re, o_im = _cmul(ct_re, ct_im, _select_cols(qr, sel_out), _select_cols(qi, sel_out))
        e_parts += [o_re, -o_im]
        rsel = rsel_rev if d == 0 else rsel_fwd
        f_re, f_im = _cmul(jnp.concatenate([bb_re] * t, axis=0), jnp.concatenate([bb_im] * t, axis=0),
                           _select_rows(rsel, pr), _select_rows(rsel, pi))
        f_parts += [f_re, f_im]
        dec_ref[0, 2 * d:2 * d + 1, :] = jnp.concatenate([pr[t:t + 1], pr[t:t + 1]], axis=1)
        dec_ref[0, 2 * d + 1:2 * d + 2, :] = jnp.concatenate([-pi[t:t + 1], pi[t:t + 1]], axis=1)

    blocks = []
    for s in range(t):
        fwd = strips[0] if s == 0 else jnp.where(lane >= n * s, pltpu.roll(strips[0], n * s, axis=1), 0.0)
        back = t - 1 - s
        bwd = strips[1] if back == 0 else jnp.where(lane < width - n * back,
                                                     pltpu.roll(strips[1], width - n * back, axis=1), 0.0)
        blocks.append(fwd + bwd)
    m = jnp.concatenate(blocks, axis=0)
    w1_ref[0] = jnp.concatenate([m] + f_parts, axis=1).astype(BF16)
    e_ref[0] = jnp.concatenate(e_parts, axis=0).astype(BF16)


def _s5_operators(a_re, a_im, log_dt, b_re, b_im, c_re, c_im):
    g, p, n, t = S5_GROUPS, S5_STATE, S5_GROUP, CHUNK
    spec = lambda shape: pl.BlockSpec((2, 1) + shape, lambda i: (0, i) + (0,) * len(shape))
    return pl.pallas_call(
        _s5_prep_kernel, grid=(g,),
        in_specs=[spec((1, p)), spec((1, p)), spec((p, 1)), spec((p, 1)), spec((1, 1)),
                  spec((p, n)), spec((p, n)), spec((n, p)), spec((n, p))],
        out_specs=[pl.BlockSpec((1, t * n, 2 * t * n), lambda i: (i, 0, 0)),
                   pl.BlockSpec((1, 4 * p, t * n), lambda i: (i, 0, 0)),
                   pl.BlockSpec((1, 4, 2 * p), lambda i: (i, 0, 0))],
        out_shape=[jax.ShapeDtypeStruct((g, t * n, 2 * t * n), BF16),
                   jax.ShapeDtypeStruct((g, 4 * p, t * n), BF16),
                   jax.ShapeDtypeStruct((g, 4, 2 * p), F32)],
        compiler_params=_cparams(("parallel",)),
        name="s5_prep",
    )(a_re.reshape(2, g, 1, p), a_im.reshape(2, g, 1, p), a_re.reshape(2, g, p, 1),
      a_im.reshape(2, g, p, 1), log_dt.reshape(2, g, 1, 1), b_re, b_im, c_re, c_im)


def _block_transpose8(ps):
    ps = list(ps)
    blk = lax.broadcasted_iota(jnp.int32, ps[0].shape, 1) // S5_GROUP
    for k in range(3):
        step = 1 << k
        shift = S5_GROUP * step
        keep = ((blk >> k) & 1) == 0
        for a in range(8):
            if a & step:
                continue
            pa, pb = ps[a], ps[a + step]
            ps[a] = jnp.where(keep, pa, pltpu.roll(pb, shift, axis=1))
            ps[a + step] = jnp.where(keep, pltpu.roll(pa, 128 - shift, axis=1), pb)
    return ps


def _s5a_kernel(u_ref, w1_ref, yin_ref, gf_ref, gb_ref):
    nb = u_ref.shape[1]
    xs = []
    for s in range(CHUNK):
        parts = [u_ref[0, b, pl.ds(s, TILE_CHUNKS, stride=CHUNK), :] for b in range(nb)]
        xs.append(jnp.concatenate(parts, axis=0))
    lo = _block_transpose8(xs[:8])
    hi = _block_transpose8(xs[8:])
    for j in range(GRP_PER_BLK):
        og = jnp.concatenate([lo[j], hi[j]], axis=1)
        r = _dot(og.astype(BF16), w1_ref[j])
        yin_ref[j] = r[:, 0:256]
        gf_ref[j] = r[:, 256:384]
        gb_ref[j] = r[:, 384:512]


def _s5a_call(u, w1):
    n_blk, b, s, _ = u.shape
    nt = s // TILE_TOK
    rows = b * TILE_CHUNKS
    out_spec = lambda n: pl.BlockSpec((GRP_PER_BLK, rows, n), lambda l, j: (l, j, 0))
    return pl.pallas_call(
        _s5a_kernel, grid=(n_blk, nt),
        in_specs=[pl.BlockSpec((1, b, TILE_TOK, LANE_BLK), lambda l, j: (l, 0, j, 0)),
                  pl.BlockSpec((GRP_PER_BLK, 256, 512), lambda l, j: (l, 0, 0))],
        out_specs=[out_spec(256), out_spec(128), out_spec(128)],
        out_shape=[jax.ShapeDtypeStruct((S5_GROUPS, nt * rows, 256), F32),
                   jax.ShapeDtypeStruct((S5_GROUPS, nt * rows, 128), F32),
                   jax.ShapeDtypeStruct((S5_GROUPS, nt * rows, 128), F32)],
        compiler_params=_cparams(("parallel", "parallel")),
        name="s5_chunk_in",
    )(u, w1)


def _s5b_kernel(nb, a_ref, gfc_ref, gfl_ref, gbc_ref, gbl_ref, hfc_ref, hfl_ref, hbc_ref, hbl_ref):
    gb = a_ref.shape[0]
    rows = nb * TILE_CHUNKS
    n_lat = gfl_ref.shape[1] // rows
    a1f = [jnp.broadcast_to(a_ref[g, 0:1, :], (nb, 128)) for g in range(gb)]
    a2f = [jnp.broadcast_to(a_ref[g, 1:2, :], (nb, 128)) for g in range(gb)]
    a1b = [jnp.broadcast_to(a_ref[g, 2:3, :], (nb, 128)) for g in range(gb)]
    a2b = [jnp.broadcast_to(a_ref[g, 3:4, :], (nb, 128)) for g in range(gb)]

    def step(state, a1, a2, g_ref, h_ref, g, row):
        h, hs = state
        h_ref[g, row, :] = h
        inp = g_ref[g, row, :]
        return a1 * h + a2 * hs + inp, a1 * hs - a2 * h + pltpu.roll(inp, 64, axis=1)

    zero = jnp.zeros((nb, 128), F32)
    hf = [(zero, zero) for _ in range(gb)]
    hb = [(zero, zero) for _ in range(gb)]
    for ci in range(TILE_CHUNKS):
        rf = pl.ds(ci, nb, stride=TILE_CHUNKS)
        rb = pl.ds(TILE_CHUNKS - 1 - ci, nb, stride=TILE_CHUNKS)
        for g in range(gb):
            hf[g] = step(hf[g], a1f[g], a2f[g], gfc_ref, hfc_ref, g, rf)
            hb[g] = step(hb[g], a1b[g], a2b[g], gbc_ref, hbc_ref, g, rb)

    def body(j, carry):
        hf, hb = carry
        hf = list(hf)
        hb = list(hb)
        base_f = j * rows
        base_b = (n_lat - 1 - j) * rows
        for ci in range(TILE_CHUNKS):
            rf = pl.ds(base_f + ci, nb, stride=TILE_CHUNKS)
            rb = pl.ds(base_b + (TILE_CHUNKS - 1 - ci), nb, stride=TILE_CHUNKS)
            for g in range(gb):
                hf[g] = step(hf[g], a1f[g], a2f[g], gfl_ref, hfl_ref, g, rf)
                hb[g] = step(hb[g], a1b[g], a2b[g], gbl_ref, hbl_ref, g, rb)
        return tuple(hf), tuple(hb)

    lax.fori_loop(0, n_lat, body, (tuple(hf), tuple(hb)))


def _s5b_call(decay, gf_c, gf_l, gb_c, gb_l, nb):
    gblk = 4
    spec = lambda a: pl.BlockSpec((gblk, a.shape[1], 128), lambda i: (i, 0, 0))
    sds = lambda a: jax.ShapeDtypeStruct(a.shape, F32)
    return pl.pallas_call(
        functools.partial(_s5b_kernel, nb), grid=(S5_GROUPS // gblk,),
        in_specs=[pl.BlockSpec((gblk, 4, 128), lambda i: (i, 0, 0)),
                  spec(gf_c), spec(gf_l), spec(gb_c), spec(gb_l)],
        out_specs=[spec(gf_c), spec(gf_l), spec(gb_c), spec(gb_l)],
        out_shape=[sds(gf_c), sds(gf_l), sds(gb_c), sds(gb_l)],
        compiler_params=_cparams(("parallel",)),
        name="s5_state_scan",
    )(decay, gf_c, gf_l, gb_c, gb_l)


def _s5c_kernel(yin_ref, hf_ref, hb_ref, e_ref, u_ref, d_ref, y_ref):
    nb = u_ref.shape[1]
    ys = []
    for j in range(GRP_PER_BLK):
        h = jnp.concatenate([hf_ref[j], hb_ref[j]], axis=1).astype(BF16)
        ys.append(yin_ref[j] + _dot(h, e_ref[j]))
    at = (_block_transpose8([y[:, :128] for y in ys])
          + _block_transpose8([y[:, 128:] for y in ys]))
    d = d_ref[...]
    for t in range(CHUNK):
        for b in range(nb):
            rows = pl.ds(t, TILE_CHUNKS, stride=CHUNK)
            y_ref[0, b, rows, :] = at[t][b * TILE_CHUNKS:(b + 1) * TILE_CHUNKS] + d * u_ref[0, b, rows, :]


def _s5c_call(yin, hf, hb, e, u, d_skip):
    n_blk, b, s, _ = u.shape
    nt = s // TILE_TOK
    rows = b * TILE_CHUNKS
    gspec = lambda n: pl.BlockSpec((GRP_PER_BLK, rows, n), lambda l, j: (l, j, 0))
    tok_spec = pl.BlockSpec((1, b, TILE_TOK, LANE_BLK), lambda l, j: (l, 0, j, 0))
    return pl.pallas_call(
        _s5c_kernel, grid=(n_blk, nt),
        in_specs=[gspec(256), gspec(128), gspec(128),
                  pl.BlockSpec((GRP_PER_BLK, 256, 256), lambda l, j: (l, 0, 0)),
                  tok_spec,
                  pl.BlockSpec((1, LANE_BLK), lambda l, j: (0, l))],
        out_specs=tok_spec,
        out_shape=_lane_block_shape(b, s),
        compiler_params=_cparams(("parallel", "parallel")),
        name="s5_chunk_out",
    )(yin, hf, hb, e, u, d_skip.reshape(1, D_S5))


def _gelu_tanh(x):
    return 0.5 * x * (1.0 + jnp.tanh(math.sqrt(2.0 / math.pi) * (x + 0.044715 * (x * x * x))))


def _route(logits):
    lane = lax.broadcasted_iota(jnp.int32, logits.shape, 1).astype(F32)
    neg = jnp.float32(-1e30)
    big = jnp.float32(1e9)
    gl = jnp.where(lane < N_GROUPS, logits, neg)
    gmax = jnp.max(gl, axis=1, keepdims=True)
    gidx = jnp.min(jnp.where(gl == gmax, lane, big), axis=1, keepdims=True)
    gsum = jnp.sum(jnp.exp(gl - gmax), axis=1, keepdims=True)
    gw = 1.0 / gsum
    lo = N_GROUPS + EXP_PER_GROUP * gidx
    el = jnp.where((lane >= lo) & (lane < lo + EXP_PER_GROUP), logits, neg)
    v1 = jnp.max(el, axis=1, keepdims=True)
    i1 = jnp.min(jnp.where(el == v1, lane, big), axis=1, keepdims=True)
    el2 = jnp.where(lane == i1, neg, el)
    v2 = jnp.max(el2, axis=1, keepdims=True)
    i2 = jnp.min(jnp.where(el2 == v2, lane, big), axis=1, keepdims=True)
    ex = jnp.exp(v2 - v1)
    p1 = 1.0 / (1.0 + ex)
    p2 = ex * p1
    e1 = i1 - lo
    e2 = i2 - lo
    first = e1 < e2
    ea = jnp.where(first, e1, e2)
    eb = jnp.where(first, e2, e1)
    wa = gw * jnp.where(first, p1, p2)
    wb = gw * jnp.where(first, p2, p1)
    pair = ea * (7.0 - ea) * 0.5 + (eb - ea - 1.0)
    return wa, wb, 6.0 * gidx + pair


def _out_kernel(ypre_ref, ysc_ref, ycf_ref, x_ref, mod_ref, wglu_ref, bglu_ref, wo_ref,
                lng_ref, lnb_ref, wr_ref, br_ref, x1_ref, hx_ref, meta_ref, counts_ref, stage_ref, cnt_ref):
    @pl.when((pl.program_id(0) == 0) & (pl.program_id(1) == 0))
    def _():
        cnt_ref[...] = jnp.zeros_like(cnt_ref)

    ypre = jnp.concatenate([ypre_ref[blk, 0] for blk in range(ypre_ref.shape[0])], axis=1)
    t = _gelu_tanh(ypre)
    gate = _sigmoid(_dot(t.astype(BF16), wglu_ref[...]) + bglu_ref[...])
    ys5 = (t * gate).astype(BF16)
    y = (_dot(ys5, wo_ref[0:D_S5, :]) + _dot(ysc_ref[0], wo_ref[D_S5:D_S5 + D_SC, :])
         + _dot(ycf_ref[0], wo_ref[D_S5 + D_SC:D_MODEL, :]))
    g1 = mod_ref[0, 2:3, :]
    x1 = _layer_norm(DN_ALPHA * x_ref[0] + g1 * y, lng_ref[...], lnb_ref[...])
    x1_ref[0] = x1
    h2 = x1 * (1.0 + mod_ref[0, 4:5, :]) + mod_ref[0, 3:4, :]
    wa, wb, cls = _route(_dot(h2.astype(BF16), wr_ref[...]) + br_ref[...])

    tm = h2.shape[0]
    lane = lax.broadcasted_iota(jnp.int32, (tm, ROUTER_LANES), 1).astype(F32)
    onehot = jnp.where(lane == cls, 1.0, 0.0)
    row_i = lax.broadcasted_iota(jnp.int32, (tm, tm), 0)
    col_i = lax.broadcasted_iota(jnp.int32, (tm, tm), 1)
    earlier = jnp.where(col_i < row_i, 1.0, 0.0).astype(BF16)
    before = _dot(earlier, onehot.astype(BF16)) + cnt_ref[...]
    rank = jnp.sum(before * onehot, axis=1, keepdims=True)
    cnt_ref[...] += jnp.sum(onehot, axis=0, keepdims=True)
    counts_ref[...] = cnt_ref[...]

    meta = (jnp.where(lane == META_WA, wa, 0.0) + jnp.where(lane == META_WB, wb, 0.0)
            + jnp.where(lane == META_CLS, cls, 0.0) + jnp.where(lane == META_RANK, rank, 0.0))
    meta_ref[...] = jnp.transpose(meta)[0:SUBLANES, :]
    hx_ref[0, :, 0:D_MODEL] = h2
    hx_ref[0, :, D_MODEL:HX_LANES] = meta
    stage_ref[...] = jnp.zeros_like(stage_ref)


def _out_call(ypre, ysc, ycf, x, mod, wglu_bf, b_glu, wo_bf, ln_g, ln_b, w_router, b_router, tm):
    b, s, d = x.shape
    nt = s // tm
    stage_octs = _sorted_rows(b * s) // (SUBLANES * b * nt)
    assert stage_octs * SUBLANES * b * nt == _sorted_rows(b * s)
    row_spec = lambda n: pl.BlockSpec((1, tm, n), lambda i, j: (i, j, 0))
    full = lambda shape: pl.BlockSpec(shape, lambda i, j: (0,) * len(shape))
    return pl.pallas_call(
        _out_kernel, grid=(b, nt),
        in_specs=[_lane_block_spec(tm), row_spec(D_SC), row_spec(D_CF), row_spec(d),
                  pl.BlockSpec((1, 6, d), lambda i, j: (i, 0, 0)),
                  full((D_S5, D_S5)), full((1, D_S5)), full((d, d)),
                  full((1, d)), full((1, d)), full((d, ROUTER_LANES)), full((1, ROUTER_LANES))],
        out_specs=[row_spec(d), row_spec(HX_LANES),
                   pl.BlockSpec((SUBLANES, tm), lambda i, j: (0, i * nt + j)),
                   full((1, ROUTER_LANES)),
                   pl.BlockSpec((stage_octs, SUBLANES, HX_LANES), lambda i, j: (i * nt + j, 0, 0))],
        out_shape=[jax.ShapeDtypeStruct((b, s, d), F32), jax.ShapeDtypeStruct((b, s, HX_LANES), F32),
                   jax.ShapeDtypeStruct((SUBLANES, b * s), F32),
                   jax.ShapeDtypeStruct((1, ROUTER_LANES), F32),
                   jax.ShapeDtypeStruct((_sorted_rows(b * s) // SUBLANES, SUBLANES, HX_LANES), F32)],
        scratch_shapes=[pltpu.VMEM((1, ROUTER_LANES), F32)],
        compiler_params=_cparams(("arbitrary", "arbitrary")),
        name="out_proj",
    )(ypre, ysc, ycf, x, mod, wglu_bf, b_glu.reshape(1, -1), wo_bf, ln_g.reshape(1, -1),
      ln_b.reshape(1, -1), w_router, b_router)


def _sorted_rows(n_tok):
    return n_tok + N_CLASSES * MOE_TM


def _moe_plan(meta, counts, n_tok):
    cls = meta[META_CLS].astype(jnp.int32)
    rank = meta[META_RANK].astype(jnp.int32)
    cnt = counts[0, :N_CLASSES].astype(jnp.int32)
    n_tiles = (cnt + (MOE_TM - 1)) // MOE_TM
    ends = jnp.cumsum(n_tiles)
    starts = ends - n_tiles
    slot = starts[cls] * MOE_TM + rank
    t_max = n_tok // MOE_TM + N_CLASSES
    n_used = ends[N_CLASSES - 1]
    tile = jnp.minimum(jnp.arange(t_max, dtype=jnp.int32), n_used - 1)
    tile_cls = jnp.sum((tile[:, None] >= ends[None, :]).astype(jnp.int32), axis=1)
    group = tile_cls // 6
    pair = tile_cls % 6
    first = jnp.array([0, 0, 0, 1, 1, 2], jnp.int32)[pair] + EXP_PER_GROUP * group
    second = jnp.array([1, 2, 3, 2, 3, 3], jnp.int32)[pair] + EXP_PER_GROUP * group
    return slot, tile, first, second, n_used.reshape(1)


def _split_row(row):
    return lax.shift_right_logical(row, 3), lax.bitwise_and(row, SUBLANES - 1)


def _dispatch_kernel(slot_ref, hx_ref, xs_init_ref, xs_ref, sem):
    del xs_init_ref
    n_oct = hx_ref.shape[1]
    base = (pl.program_id(0) * pl.num_programs(1) + pl.program_id(1)) * (n_oct * SUBLANES)

    def body(i, carry):
        for k in range(SUBLANES):
            oct_id, sub = _split_row(slot_ref[base + i * SUBLANES + k])
            pltpu.make_async_copy(hx_ref.at[0, i, pl.ds(k, 1), :], xs_ref.at[oct_id, pl.ds(sub, 1), :],
                                  sem).start(priority=k % 2)
        return carry

    lax.fori_loop(0, n_oct, body, 0)
    pltpu.make_async_copy(hx_ref.at[0], xs_ref.at[pl.ds(0, n_oct)], sem).wait()


def _dispatch_call(slot, hx, xs_init, tm):
    b, s, w = hx.shape
    n_rows = xs_init.shape[0] * SUBLANES
    grid_spec = pltpu.PrefetchScalarGridSpec(
        num_scalar_prefetch=1, grid=(b, s // tm),
        in_specs=[pl.BlockSpec((1, tm // SUBLANES, SUBLANES, w), lambda i, j, slot: (i, j, 0, 0)),
                  pl.BlockSpec(memory_space=pl.ANY)],
        out_specs=pl.BlockSpec(memory_space=pl.ANY),
        scratch_shapes=[pltpu.SemaphoreType.DMA(())])
    xs = pl.pallas_call(
        _dispatch_kernel, grid_spec=grid_spec,
        out_shape=jax.ShapeDtypeStruct(xs_init.shape, F32),
        input_output_aliases={2: 0},
        compiler_params=_cparams(("arbitrary", "arbitrary")),
        name="moe_dispatch",
    )(slot, hx.reshape(b, s // SUBLANES, SUBLANES, w), xs_init)
    return xs.reshape(n_rows, w)


def _moe_kernel(tile_ref, first_ref, second_ref, nused_ref, xs_ref, wga_ref, wgb_ref, wua_ref, wub_ref,
                wda_ref, wdb_ref, ys_ref):
    del tile_ref, first_ref, second_ref
    t = pl.program_id(0)
    n_used = nused_ref[0]

    @pl.when(t < n_used)
    def _():
        x = xs_ref[...]
        xb = x[:, 0:D_MODEL].astype(BF16)

        def expert(wg_ref, wu_ref, wd_ref, w):
            gate = _dot(xb, wg_ref[0])
            up = _dot(xb, wu_ref[0])
            act = gate * _sigmoid(gate) * up * w
            return _dot(act.astype(BF16), wd_ref[0])

        wa = x[:, D_MODEL + META_WA:D_MODEL + META_WA + 1]
        wb = x[:, D_MODEL + META_WB:D_MODEL + META_WB + 1]
        ys_ref[...] = expert(wga_ref, wua_ref, wda_ref, wa) + expert(wgb_ref, wub_ref, wdb_ref, wb)

    @pl.when(t >= n_used)
    def _():
        ys_ref[...] = jnp.zeros_like(ys_ref)


def _moe_call(tile, first, second, n_used, xs, wg_bf, wu_bf, wd_bf):
    n_rows, w = xs.shape
    d = D_MODEL
    t_max = tile.shape[0]
    up_spec = lambda sel: pl.BlockSpec((1, d, D_EXPERT), lambda t, tl, fi, se, nu: ((fi, se)[sel][t], 0, 0))
    down_spec = lambda sel: pl.BlockSpec((1, D_EXPERT, d), lambda t, tl, fi, se, nu: ((fi, se)[sel][t], 0, 0))
    grid_spec = pltpu.PrefetchScalarGridSpec(
        num_scalar_prefetch=4, grid=(t_max,),
        in_specs=[pl.BlockSpec((MOE_TM, w), lambda t, tl, fi, se, nu: (tl[t], 0)),
                  up_spec(0), up_spec(1), up_spec(0), up_spec(1), down_spec(0), down_spec(1)],
        out_specs=pl.BlockSpec((MOE_TM, d), lambda t, tl, fi, se, nu: (t, 0)))
    return pl.pallas_call(
        _moe_kernel, grid_spec=grid_spec,
        out_shape=jax.ShapeDtypeStruct((n_rows, d), F32),
        compiler_params=_cparams(("arbitrary",)),
        name="moe_experts",
    )(tile, first, second, n_used, xs, wg_bf, wg_bf, wu_bf, wu_bf, wd_bf, wd_bf)


def _combine_kernel(slot_ref, x1_ref, mod_ref, lng_ref, lnb_ref, ys_ref, o_ref, f_ref, sem):
    n_oct = f_ref.shape[1]
    tm = n_oct * SUBLANES
    step = pl.program_id(0) * pl.num_programs(1) + pl.program_id(1)
    n_steps = pl.num_programs(0) * pl.num_programs(1)

    def request(which, buf):
        base = which * tm

        def body(i, carry):
            for k in range(SUBLANES):
                oct_id, sub = _split_row(slot_ref[base + i * SUBLANES + k])
                pltpu.make_async_copy(ys_ref.at[oct_id, pl.ds(sub, 1), :], f_ref.at[buf, i, pl.ds(k, 1), :],
                                      sem.at[buf]).start(priority=k % 2)
            return carry

        lax.fori_loop(0, n_oct, body, 0)

    @pl.when(step == 0)
    def _():
        request(0, 0)

    @pl.when(step + 1 < n_steps)
    def _():
        request(step + 1, (step + 1) % 2)

    buf = step % 2
    pltpu.make_async_copy(ys_ref.at[pl.ds(0, n_oct)], f_ref.at[buf], sem.at[buf]).wait()
    f = f_ref[buf].reshape(tm, f_ref.shape[3])
    g2 = mod_ref[0, 5:6, :]
    o_ref[0] = _layer_norm(DN_ALPHA * x1_ref[0] + g2 * f, lng_ref[...], lnb_ref[...])


def _combine_call(slot, x1, mod, ln_g, ln_b, ys, tm):
    b, s, d = x1.shape
    ys = ys.reshape(ys.shape[0] // SUBLANES, SUBLANES, d)
    grid_spec = pltpu.PrefetchScalarGridSpec(
        num_scalar_prefetch=1, grid=(b, s // tm),
        in_specs=[pl.BlockSpec((1, tm, d), lambda i, j, slot: (i, j, 0)),
                  pl.BlockSpec((1, 6, d), lambda i, j, slot: (i, 0, 0)),
                  pl.BlockSpec((1, d), lambda i, j, slot: (0, 0)),
                  pl.BlockSpec((1, d), lambda i, j, slot: (0, 0)),
                  pl.BlockSpec(memory_space=pl.ANY)],
        out_specs=pl.BlockSpec((1, tm, d), lambda i, j, slot: (i, j, 0)),
        scratch_shapes=[pltpu.VMEM((2, tm // SUBLANES, SUBLANES, d), F32), pltpu.SemaphoreType.DMA((2,))])
    return pl.pallas_call(
        _combine_kernel, grid_spec=grid_spec,
        out_shape=jax.ShapeDtypeStruct((b, s, d), F32),
        compiler_params=_cparams(("arbitrary", "arbitrary")),
        name="moe_combine",
    )(slot, x1, mod, ln_g.reshape(1, -1), ln_b.reshape(1, -1), ys)


def _moe_sublayer(hx, meta, counts, xs_init, x1, mod, wg_bf, wu_bf, wd_bf, ln_g, ln_b, tm):
    b, s, _ = x1.shape
    n_tok = b * s
    slot, tile, first, second, n_used = _moe_plan(meta, counts, n_tok)
    xs = _dispatch_call(slot, hx, xs_init, tm)
    ys = _moe_call(tile, first, second, n_used, xs, wg_bf, wu_bf, wd_bf)
    return _combine_call(slot, x1, mod, ln_g, ln_b, ys, tm)


def kernel(x, c, ctx, c_ctx, w_mod, b_mod, w_in, s5_a_re, s5_a_im, s5_log_dt, s5_b_re, s5_b_im, s5_c_re, s5_c_im, s5_d, w_glu, b_glu, w_sc, w_dw, b_dw, ln_cf_g, ln_cf_b, w_o, ln1_g, ln1_b, w_rg, b_rg, w_rexp, b_rexp, w_gate, w_up, w_down, ln2_g, ln2_b):
    nb, seq, d = x.shape
    n_ctx = ctx.shape[1]
    n_layers = w_mod.shape[0]
    assert seq % TILE_TOK == 0 and n_ctx % TILE_TOK == 0 and seq % GRID_W == 0

    mod_rows = 16
    assert nb + 1 <= mod_rows
    c_all = jnp.concatenate([c, c_ctx[None, :], jnp.zeros((mod_rows - nb - 1, d), F32)], axis=0)
    mod_all = _mod_call(c_all, w_mod, b_mod)

    pad_r = ROUTER_LANES - N_GROUPS - N_EXPERTS
    x_lat, x_ctx = x, ctx
    for l in range(n_layers):
        last = l == n_layers - 1
        mod_lat = mod_all[l, :nb].reshape(nb, 6, d)
        mod_ctx = jnp.broadcast_to(mod_all[l, nb].reshape(1, 6, d), (nb, 6, d))
        w_in_bf = w_in[l].astype(BF16)
        wglu_bf = w_glu[l].astype(BF16)
        wo_bf = w_o[l].astype(BF16)
        wg_bf = w_gate[l].astype(BF16)
        wu_bf = w_up[l].astype(BF16)
        wd_bf = w_down[l].astype(BF16)
        w_router = jnp.concatenate([w_rg[l], w_rexp[l], jnp.zeros((d, pad_r), F32)], axis=1).astype(BF16)
        b_router = jnp.concatenate([b_rg[l], b_rexp[l], jnp.zeros((pad_r,), F32)]).reshape(1, -1)
        w1, e_op, decay = _s5_operators(s5_a_re[l], s5_a_im[l], s5_log_dt[l], s5_b_re[l], s5_b_im[l],
                                        s5_c_re[l], s5_c_im[l])

        u_l, bg_l, cv_l, gl_l = _in_call(x_lat, mod_lat, w_in_bf, 512, False)
        if last:
            u_c = _in_call(x_ctx, mod_ctx, w_in_bf[:, :D_S5], TILE_TOK, True)
        else:
            u_c, bg_c, cv_c, gl_c = _in_call(x_ctx, mod_ctx, w_in_bf, TILE_TOK, False)

        yin_l, gf_l, gb_l = _s5a_call(u_l, w1)
        yin_c, gf_c, gb_c = _s5a_call(u_c, w1)
        hf_c, hf_l, hb_c, hb_l = _s5b_call(decay, gf_c, gf_l, gb_c, gb_l, nb)
        ypre_l = _s5c_call(yin_l, hf_l, hb_l, e_op, u_l, s5_d[l])

        ysc_l, ycf_l = _conv_call(bg_l, cv_l, gl_l, w_sc[l], w_dw[l], b_dw[l], ln_cf_g[l], ln_cf_b[l], True)
        x1_l, hx_l, meta_l, cnt_l, stage_l = _out_call(ypre_l, ysc_l, ycf_l, x_lat, mod_lat, wglu_bf, b_glu[l],
                                                       wo_bf, ln1_g[l], ln1_b[l], w_router, b_router, 512)
        if not last:
            ypre_c = _s5c_call(yin_c, hf_c, hb_c, e_op, u_c, s5_d[l])
            ysc_c, ycf_c = _conv_call(bg_c, cv_c, gl_c, w_sc[l], w_dw[l], b_dw[l], ln_cf_g[l], ln_cf_b[l], False)
            x1_c, hx_c, meta_c, cnt_c, stage_c = _out_call(ypre_c, ysc_c, ycf_c, x_ctx, mod_ctx, wglu_bf, b_glu[l],
                                                           wo_bf, ln1_g[l], ln1_b[l], w_router, b_router, TILE_TOK)
            x_ctx = _moe_sublayer(hx_c, meta_c, cnt_c, stage_c, x1_c, mod_ctx, wg_bf, wu_bf, wd_bf,
                                  ln2_g[l], ln2_b[l], TILE_TOK)
        x_lat = _moe_sublayer(hx_l, meta_l, cnt_l, stage_l, x1_l, mod_lat, wg_bf, wu_bf, wd_bf,
                              ln2_g[l], ln2_b[l], 512)
    return x_lat
```

```python
import functools
import math

import jax
import jax.numpy as jnp
from jax import lax
from jax.experimental import pallas as pl
from jax.experimental.pallas import tpu as pltpu

F32 = jnp.float32
BF16 = jnp.bfloat16

D_MODEL = 1024
DEPTH = 2
GRID_W = 64
D_S5 = 512
S5_GROUP = 16
S5_GROUPS = 32
S5_STATE = 64
D_SC = 256
D_CF = 256
CF_WIDTH = 31
CF_HALF = 15
D_IN = 1792
N_GROUPS = 4
EXP_PER_GROUP = 4
N_EXPERTS = 16
D_EXPERT = 256
DN_ALPHA = (2 * DEPTH) ** 0.25
LN_EPS = 1e-5

CHUNK = 16
TILE_CHUNKS = 16
TILE_TOK = CHUNK * TILE_CHUNKS
LANE_BLK = 128
GRP_PER_BLK = LANE_BLK // S5_GROUP
ROUTER_LANES = 128
HX_LANES = D_MODEL + ROUTER_LANES
META_WA, META_WB, META_CLS, META_RANK = 0, 1, 2, 3
N_CLASSES = N_GROUPS * 6
MOE_TM = 256
SUBLANES = 8
VMEM_LIMIT = 56 * 1024 * 1024


def _cparams(sem):
    return pltpu.CompilerParams(dimension_semantics=sem, vmem_limit_bytes=VMEM_LIMIT)


def _split_bf16(a):
    hi = a.astype(BF16)
    lo = (a - hi.astype(F32)).astype(BF16)
    return hi, lo


def _dot(a, b):
    return jnp.dot(a, b, preferred_element_type=F32)


def _dot3(a, b):
    ah, al = _split_bf16(a)
    bh, bl = _split_bf16(b)
    return _dot(ah, bh) + (_dot(al, bh) + _dot(ah, bl))


def _sigmoid(x):
    return 1.0 / (1.0 + jnp.exp(-x))


def _layer_norm(x, g, b):
    mu = jnp.mean(x, axis=-1, keepdims=True)
    xc = x - mu
    var = jnp.mean(xc * xc, axis=-1, keepdims=True)
    return xc * lax.rsqrt(var + LN_EPS) * g + b


def _mod_kernel(c_ref, w_ref, b_ref, o_ref):
    c = c_ref[...]
    s = c * _sigmoid(c)
    o_ref[0] = _dot3(s, w_ref[0]) + b_ref[0]


def _mod_call(c_all, w_mod, b_mod):
    n_layers, d, n_out = w_mod.shape
    tn = 1536
    rows = c_all.shape[0]
    return pl.pallas_call(
        _mod_kernel,
        grid=(n_layers, n_out // tn),
        in_specs=[
            pl.BlockSpec((rows, d), lambda l, j: (0, 0)),
            pl.BlockSpec((1, d, tn), lambda l, j: (l, 0, j)),
            pl.BlockSpec((1, 1, tn), lambda l, j: (l, 0, j)),
        ],
        out_specs=pl.BlockSpec((1, rows, tn), lambda l, j: (l, 0, j)),
        out_shape=jax.ShapeDtypeStruct((n_layers, rows, n_out), F32),
        compiler_params=_cparams(("parallel", "parallel")),
        name="mod",
    )(c_all, w_mod, b_mod.reshape(n_layers, 1, n_out))


def _in_kernel(x_ref, mod_ref, w_ref, u_ref, bg_ref, cv_ref, gl_ref):
    x = x_ref[0]
    sh = mod_ref[0, 0:1, :]
    sc = mod_ref[0, 1:2, :]
    h = (x * (1.0 + sc) + sh).astype(BF16)
    z = _dot(h, w_ref[...])
    _store_lane_blocks(u_ref, z[:, 0:512])
    bg_ref[0] = z[:, 512:768]
    cv_ref[0] = z[:, 768:1024] * z[:, 1024:1280]
    gl_ref[0] = z[:, 1280:1536] * _sigmoid(z[:, 1536:1792])


def _in_u_kernel(x_ref, mod_ref, w_ref, u_ref):
    x = x_ref[0]
    sh = mod_ref[0, 0:1, :]
    sc = mod_ref[0, 1:2, :]
    h = (x * (1.0 + sc) + sh).astype(BF16)
    _store_lane_blocks(u_ref, _dot(h, w_ref[...]))


def _store_lane_blocks(ref, val):
    for blk in range(ref.shape[0]):
        ref[blk, 0] = val[:, blk * LANE_BLK:(blk + 1) * LANE_BLK]


def _lane_block_spec(tm):
    return pl.BlockSpec((D_S5 // LANE_BLK, 1, tm, LANE_BLK), lambda i, j: (0, i, j, 0))


def _lane_block_shape(b, s):
    return jax.ShapeDtypeStruct((D_S5 // LANE_BLK, b, s, LANE_BLK), F32)


def _in_call(x, mod, w_in_bf, tm, u_only):
    b, s, d = x.shape
    grid = (b, s // tm)
    row_spec = lambda n: pl.BlockSpec((1, tm, n), lambda i, j: (i, j, 0))
    in_specs = [
        row_spec(d),
        pl.BlockSpec((1, 6, d), lambda i, j: (i, 0, 0)),
    ]
    if u_only:
        in_specs.append(pl.BlockSpec((d, D_S5), lambda i, j: (0, 0)))
        return pl.pallas_call(
            _in_u_kernel, grid=grid, in_specs=in_specs,
            out_specs=_lane_block_spec(tm),
            out_shape=_lane_block_shape(b, s),
            compiler_params=_cparams(("parallel", "parallel")),
            name="in_proj_u",
        )(x, mod, w_in_bf)
    in_specs.append(pl.BlockSpec((d, D_IN), lambda i, j: (0, 0)))
    return pl.pallas_call(
        _in_kernel, grid=grid, in_specs=in_specs,
        out_specs=[_lane_block_spec(tm), row_spec(D_SC), row_spec(D_SC), row_spec(D_CF)],
        out_shape=[_lane_block_shape(b, s),
                   jax.ShapeDtypeStruct((b, s, D_SC), F32),
                   jax.ShapeDtypeStruct((b, s, D_SC), F32),
                   jax.ShapeDtypeStruct((b, s, D_CF), F32)],
        compiler_params=_cparams(("parallel", "parallel")),
        name="in_proj",
    )(x, mod, w_in_bf)


def _conv_tail(t, bdw_ref, lng_ref, lnb_ref):
    t = t + bdw_ref[...]
    t = _layer_norm(t, lng_ref[...], lnb_ref[...])
    return t * _sigmoid(t)


def _conv_grid_kernel(bg_ref, cv_ref, gl_ref, wsc_ref, wdw_ref, bdw_ref, lng_ref, lnb_ref,
                      ysc_ref, ycf_ref, pad_ref, t_ref):
    s = cv_ref.shape[1]
    rows = s // GRID_W
    cv = cv_ref[0]
    col = lax.broadcasted_iota(jnp.int32, (s, D_SC), 0) % GRID_W
    prev = jnp.where(col == 0, 0.0, pltpu.roll(cv, 1, axis=0))
    nxt = jnp.where(col == GRID_W - 1, 0.0, pltpu.roll(cv, s - 1, axis=0))
    conv = prev * wsc_ref[0:1, :] + cv * wsc_ref[1:2, :] + nxt * wsc_ref[2:3, :]
    ysc_ref[0] = (bg_ref[0] * conv).astype(ysc_ref.dtype)

    zero = jnp.zeros((CF_HALF, GRID_W, D_CF), F32)
    pad_ref[0:CF_HALF] = zero
    pad_ref[CF_HALF + rows:CF_HALF + rows + CF_HALF] = zero
    pad_ref[CF_HALF:CF_HALF + rows] = gl_ref[0].reshape(rows, GRID_W, D_CF)

    def body(i, carry):
        w0 = pl.multiple_of(i * 8, 8)
        for half in range(D_CF // 128):
            lanes = slice(half * 128, (half + 1) * 128)
            acc = jnp.zeros((rows, 8, 128), F32)
            for k in range(CF_WIDTH):
                acc = acc + pad_ref[k:k + rows, pl.ds(w0, 8), lanes] * wdw_ref[k:k + 1, lanes]
            t_ref[:, pl.ds(w0, 8), lanes] = acc
        return carry

    lax.fori_loop(0, GRID_W // 8, body, 0)
    t = t_ref[...].reshape(s, D_CF)
    ycf_ref[0] = _conv_tail(t, bdw_ref, lng_ref, lnb_ref).astype(ycf_ref.dtype)


def _conv_seq_kernel(bg_ref, cv_ref, gl_ref, wsc_ref, wdw_ref, bdw_ref, lng_ref, lnb_ref,
                     ysc_ref, ycf_ref, pad_ref):
    s = cv_ref.shape[1]
    cv = cv_ref[0]
    pos = lax.broadcasted_iota(jnp.int32, (s, D_SC), 0)
    prev = jnp.where(pos == 0, 0.0, pltpu.roll(cv, 1, axis=0))
    nxt = jnp.where(pos == s - 1, 0.0, pltpu.roll(cv, s - 1, axis=0))
    conv = prev * wsc_ref[0:1, :] + cv * wsc_ref[1:2, :] + nxt * wsc_ref[2:3, :]
    ysc_ref[0] = (bg_ref[0] * conv).astype(ysc_ref.dtype)

    off = 16
    pad_ref[0:off] = jnp.zeros((off, D_CF), F32)
    pad_ref[off + s:off + s + 16] = jnp.zeros((16, D_CF), F32)
    pad_ref[off:off + s] = gl_ref[0]
    acc = jnp.zeros((s, D_CF), F32)
    for k in range(CF_WIDTH):
        acc = acc + pad_ref[pl.ds(off - CF_HALF + k, s), :] * wdw_ref[k:k + 1, :]
    ycf_ref[0] = _conv_tail(acc, bdw_ref, lng_ref, lnb_ref).astype(ycf_ref.dtype)


def _conv_call(bg, cv, gl, w_sc, w_dw, b_dw, ln_g, ln_b, grid_mode):
    b, s, _ = bg.shape
    row_spec = pl.BlockSpec((1, s, D_SC), lambda i: (i, 0, 0))
    full = lambda shape: pl.BlockSpec(shape, lambda i: (0,) * len(shape))
    if grid_mode:
        rows = s // GRID_W
        kern = _conv_grid_kernel
        scratch = [pltpu.VMEM((rows + 2 * CF_HALF, GRID_W, D_CF), F32),
                   pltpu.VMEM((rows, GRID_W, D_CF), F32)]
        name = "conv_grid"
    else:
        kern = _conv_seq_kernel
        scratch = [pltpu.VMEM((s + 32, D_CF), F32)]
        name = "conv_seq"
    return pl.pallas_call(
        kern, grid=(b,),
        in_specs=[row_spec, row_spec, row_spec, full((3, D_SC)), full((CF_WIDTH, D_CF)),
                  full((1, D_CF)), full((1, D_CF)), full((1, D_CF))],
        out_specs=[row_spec, row_spec],
        out_shape=[jax.ShapeDtypeStruct((b, s, D_SC), BF16), jax.ShapeDtypeStruct((b, s, D_CF), BF16)],
        scratch_shapes=scratch,
        compiler_params=_cparams(("parallel",)),
        name=name,
    )(bg, cv, gl, w_sc, w_dw, b_dw.reshape(1, -1), ln_g.reshape(1, -1), ln_b.reshape(1, -1))


def _split3(a):
    hi = a.astype(BF16)
    r = a - hi.astype(F32)
    mid = r.astype(BF16)
    lo = (r - mid.astype(F32)).astype(BF16)
    return hi, mid, lo


def _select_cols(a, sel):
    hi, mid, lo = _split3(a)
    return _dot(hi, sel) + (_dot(mid, sel) + _dot(lo, sel))


def _select_rows(sel, a):
    hi, mid, lo = _split3(a)
    return _dot(sel, hi) + (_dot(sel, mid) + _dot(sel, lo))


def _cmul(ar, ai, br, bi):
    return ar * br - ai * bi, ar * bi + ai * br


def _s5_prep_kernel(are_r, aim_r, are_c, aim_c, ldt, bre_ref, bim_ref, cre_ref, cim_ref,
                    w1_ref, e_ref, dec_ref):
    t, n, p = CHUNK, S5_GROUP, S5_STATE
    width = t * n
    lane_tok = lax.broadcasted_iota(jnp.int32, (128, width), 1) // n
    pow_id = lax.broadcasted_iota(jnp.int32, (128, width), 0)
    onehot = lambda cond: jnp.where(cond, 1.0, 0.0).astype(BF16)
    sel_fwd = onehot(pow_id == lane_tok)
    sel_rev = onehot(pow_id == t - 1 - lane_tok)
    sel_out = onehot(pow_id == t - lane_tok)
    row_tok = lax.broadcasted_iota(jnp.int32, (width, 32), 0) // n
    row_pow = lax.broadcasted_iota(jnp.int32, (width, 32), 1)
    rsel_rev = onehot(row_pow == t - 1 - row_tok)
    rsel_fwd = onehot(row_pow == row_tok)
    lane = lax.broadcasted_iota(jnp.int32, (n, width), 1)

    jc = jnp.minimum(lax.broadcasted_iota(jnp.int32, (p, 128), 1), t).astype(F32)
    jr = jnp.minimum(lax.broadcasted_iota(jnp.int32, (32, p), 0), t).astype(F32)

    strips = []
    f_parts = []
    e_parts = []
    for d in range(2):
        dt = jnp.exp(ldt[d, 0])
        mag_c = are_c[d, 0] * dt
        th_c = aim_c[d, 0] * dt
        ec = jnp.exp(jc * mag_c)
        qr = ec * jnp.cos(jc * th_c)
        qi = ec * jnp.sin(jc * th_c)
        a_re = are_r[d, 0]
        a_im = aim_r[d, 0]
        er = jnp.exp(jr * (a_re * dt))
        pr = er * jnp.cos(jr * (a_im * dt))
        pi = er * jnp.sin(jr * (a_im * dt))
        nr = pr[1:2] - 1.0
        ni = pi[1:2]
        den = a_re * a_re + a_im * a_im
        fre = (nr * a_re + ni * a_im) / den
        fim = (ni * a_re - nr * a_im) / den
        bt_re = bre_ref[d, 0].T
        bt_im = bim_ref[d, 0].T
        bb_re, bb_im = _cmul(fre, fim, bt_re, bt_im)
        ct_re = jnp.concatenate([cre_ref[d, 0].T] * t, axis=1)
        ct_im = jnp.concatenate([cim_ref[d, 0].T] * t, axis=1)
        sel = sel_fwd if d == 0 else sel_rev
        w_re, w_im = _cmul(ct_re, ct_im, _select_cols(qr, sel), _select_cols(qi, sel))
        strips.append(_dot3(bb_re, w_re) - _dot3(bb_im, w_im))
        if d == 0:
            o_re, o_im = _cmul(w_re, w_im, qr[:, 1:2], qi[:, 1:2])
        else:
            o_re, o_im = _cmul(ct_re, ct_im, _select_cols(qr, sel_out), _select_cols(qi, sel_out))
        e_parts += [o_re, -o_im]
        rsel = rsel_rev if d == 0 else rsel_fwd
        f_re, f_im = _cmul(jnp.concatenate([bb_re] * t, axis=0), jnp.concatenate([bb_im] * t, axis=0),
                           _select_rows(rsel, pr), _select_rows(rsel, pi))
        f_parts += [f_re, f_im]
        dec_ref[0, 2 * d:2 * d + 1, :] = jnp.concatenate([pr[t:t + 1], pr[t:t + 1]], axis=1)
        dec_ref[0, 2 * d + 1:2 * d + 2, :] = jnp.concatenate([-pi[t:t + 1], pi[t:t + 1]], axis=1)

    blocks = []
    for s in range(t):
        fwd = strips[0] if s == 0 else jnp.where(lane >= n * s, pltpu.roll(strips[0], n * s, axis=1), 0.0)
        back = t - 1 - s
        bwd = strips[1] if back == 0 else jnp.where(lane < width - n * back,
                                                     pltpu.roll(strips[1], width - n * back, axis=1), 0.0)
        blocks.append(fwd + bwd)
    m = jnp.concatenate(blocks, axis=0)
    w1_ref[0] = jnp.concatenate([m] + f_parts, axis=1).astype(BF16)
    e_ref[0] = jnp.concatenate(e_parts, axis=0).astype(BF16)


def _s5_operators(a_re, a_im, log_dt, b_re, b_im, c_re, c_im):
    g, p, n, t = S5_GROUPS, S5_STATE, S5_GROUP, CHUNK
    spec = lambda shape: pl.BlockSpec((2, 1) + shape, lambda i: (0, i) + (0,) * len(shape))
    return pl.pallas_call(
        _s5_prep_kernel, grid=(g,),
        in_specs=[spec((1, p)), spec((1, p)), spec((p, 1)), spec((p, 1)), spec((1, 1)),
                  spec((p, n)), spec((p, n)), spec((n, p)), spec((n, p))],
        out_specs=[pl.BlockSpec((1, t * n, 2 * t * n), lambda i: (i, 0, 0)),
                   pl.BlockSpec((1, 4 * p, t * n), lambda i: (i, 0, 0)),
                   pl.BlockSpec((1, 4, 2 * p), lambda i: (i, 0, 0))],
        out_shape=[jax.ShapeDtypeStruct((g, t * n, 2 * t * n), BF16),
                   jax.ShapeDtypeStruct((g, 4 * p, t * n), BF16),
                   jax.ShapeDtypeStruct((g, 4, 2 * p), F32)],
        compiler_params=_cparams(("parallel",)),
        name="s5_prep",
    )(a_re.reshape(2, g, 1, p), a_im.reshape(2, g, 1, p), a_re.reshape(2, g, p, 1),
      a_im.reshape(2, g, p, 1), log_dt.reshape(2, g, 1, 1), b_re, b_im, c_re, c_im)


def _block_transpose8(ps):
    ps = list(ps)
    blk = lax.broadcasted_iota(jnp.int32, ps[0].shape, 1) // S5_GROUP
    for k in range(3):
        step = 1 << k
        shift = S5_GROUP * step
        keep = ((blk >> k) & 1) == 0
        for a in range(8):
            if a & step:
                continue
            pa, pb = ps[a], ps[a + step]
            ps[a] = jnp.where(keep, pa, pltpu.roll(pb, shift, axis=1))
            ps[a + step] = jnp.where(keep, pltpu.roll(pa, 128 - shift, axis=1), pb)
    return ps


def _s5a_kernel(u_ref, w1_ref, yin_ref, gf_ref, gb_ref):
    nb = u_ref.shape[1]
    xs = []
    for s in range(CHUNK):
        parts = [u_ref[0, b, pl.ds(s, TILE_CHUNKS, stride=CHUNK), :] for b in range(nb)]
        xs.append(jnp.concatenate(parts, axis=0))
    lo = _block_transpose8(xs[:8])
    hi = _block_transpose8(xs[8:])
    for j in range(GRP_PER_BLK):
        og = jnp.concatenate([lo[j], hi[j]], axis=1)
        r = _dot(og.astype(BF16), w1_ref[j])
        yin_ref[j] = r[:, 0:256]
        gf_ref[j] = r[:, 256:384]
        gb_ref[j] = r[:, 384:512]


def _s5a_call(u, w1):
    n_blk, b, s, _ = u.shape
    nt = s // TILE_TOK
    rows = b * TILE_CHUNKS
    out_spec = lambda n: pl.BlockSpec((GRP_PER_BLK, rows, n), lambda l, j: (l, j, 0))
    return pl.pallas_call(
        _s5a_kernel, grid=(n_blk, nt),
        in_specs=[pl.BlockSpec((1, b, TILE_TOK, LANE_BLK), lambda l, j: (l, 0, j, 0)),
                  pl.BlockSpec((GRP_PER_BLK, 256, 512), lambda l, j: (l, 0, 0))],
        out_specs=[out_spec(256), out_spec(128), out_spec(128)],
        out_shape=[jax.ShapeDtypeStruct((S5_GROUPS, nt * rows, 256), F32),
                   jax.ShapeDtypeStruct((S5_GROUPS, nt * rows, 128), F32),
                   jax.ShapeDtypeStruct((S5_GROUPS, nt * rows, 128), F32)],
        compiler_params=_cparams(("parallel", "parallel")),
        name="s5_chunk_in",
    )(u, w1)


def _s5b_kernel(nb, a_ref, gfc_ref, gfl_ref, gbc_ref, gbl_ref, hfc_ref, hfl_ref, hbc_ref, hbl_ref):
    gb = a_ref.shape[0]
    rows = nb * TILE_CHUNKS
    n_lat = gfl_ref.shape[1] // rows
    a1f = [jnp.broadcast_to(a_ref[g, 0:1, :], (nb, 128)) for g in range(gb)]
    a2f = [jnp.broadcast_to(a_ref[g, 1:2, :], (nb, 128)) for g in range(gb)]
    a1b = [jnp.broadcast_to(a_ref[g, 2:3, :], (nb, 128)) for g in range(gb)]
    a2b = [jnp.broadcast_to(a_ref[g, 3:4, :], (nb, 128)) for g in range(gb)]

    def step(state, a1, a2, g_ref, h_ref, g, row):
        h, hs = state
        h_ref[g, row, :] = h
        inp = g_ref[g, row, :]
        return a1 * h + a2 * hs + inp, a1 * hs - a2 * h + pltpu.roll(inp, 64, axis=1)

    zero = jnp.zeros((nb, 128), F32)
    hf = [(zero, zero) for _ in range(gb)]
    hb = [(zero, zero) for _ in range(gb)]
    for ci in range(TILE_CHUNKS):
        rf = pl.ds(ci, nb, stride=TILE_CHUNKS)
        rb = pl.ds(TILE_CHUNKS - 1 - ci, nb, stride=TILE_CHUNKS)
        for g in range(gb):
            hf[g] = step(hf[g], a1f[g], a2f[g], gfc_ref, hfc_ref, g, rf)
            hb[g] = step(hb[g], a1b[g], a2b[g], gbc_ref, hbc_ref, g, rb)

    def body(j, carry):
        hf, hb = carry
        hf = list(hf)
        hb = list(hb)
        base_f = j * rows
        base_b = (n_lat - 1 - j) * rows
        for ci in range(TILE_CHUNKS):
            rf = pl.ds(base_f + ci, nb, stride=TILE_CHUNKS)
            rb = pl.ds(base_b + (TILE_CHUNKS - 1 - ci), nb, stride=TILE_CHUNKS)
            for g in range(gb):
                hf[g] = step(hf[g], a1f[g], a2f[g], gfl_ref, hfl_ref, g, rf)
                hb[g] = step(hb[g], a1b[g], a2b[g], gbl_ref, hbl_ref, g, rb)
        return tuple(hf), tuple(hb)

    lax.fori_loop(0, n_lat, body, (tuple(hf), tuple(hb)))


def _s5b_call(decay, gf_c, gf_l, gb_c, gb_l, nb):
    gblk = 4
    spec = lambda a: pl.BlockSpec((gblk, a.shape[1], 128), lambda i: (i, 0, 0))
    sds = lambda a: jax.ShapeDtypeStruct(a.shape, F32)
    return pl.pallas_call(
        functools.partial(_s5b_kernel, nb), grid=(S5_GROUPS // gblk,),
        in_specs=[pl.BlockSpec((gblk, 4, 128), lambda i: (i, 0, 0)),
                  spec(gf_c), spec(gf_l), spec(gb_c), spec(gb_l)],
        out_specs=[spec(gf_c), spec(gf_l), spec(gb_c), spec(gb_l)],
        out_shape=[sds(gf_c), sds(gf_l), sds(gb_c), sds(gb_l)],
        compiler_params=_cparams(("parallel",)),
        name="s5_state_scan",
    )(decay, gf_c, gf_l, gb_c, gb_l)


def _s5c_kernel(yin_ref, hf_ref, hb_ref, e_ref, u_ref, d_ref, y_ref):
    nb = u_ref.shape[1]
    ys = []
    for j in range(GRP_PER_BLK):
        h = jnp.concatenate([hf_ref[j], hb_ref[j]], axis=1).astype(BF16)
        ys.append(yin_ref[j] + _dot(h, e_ref[j]))
    at = (_block_transpose8([y[:, :128] for y in ys])
          + _block_transpose8([y[:, 128:] for y in ys]))
    d = d_ref[...]
    for t in range(CHUNK):
        for b in range(nb):
            rows = pl.ds(t, TILE_CHUNKS, stride=CHUNK)
            y_ref[0, b, rows, :] = at[t][b * TILE_CHUNKS:(b + 1) * TILE_CHUNKS] + d * u_ref[0, b, rows, :]


def _s5c_call(yin, hf, hb, e, u, d_skip):
    n_blk, b, s, _ = u.shape
    nt = s // TILE_TOK
    rows = b * TILE_CHUNKS
    gspec = lambda n: pl.BlockSpec((GRP_PER_BLK, rows, n), lambda l, j: (l, j, 0))
    tok_spec = pl.BlockSpec((1, b, TILE_TOK, LANE_BLK), lambda l, j: (l, 0, j, 0))
    return pl.pallas_call(
        _s5c_kernel, grid=(n_blk, nt),
        in_specs=[gspec(256), gspec(128), gspec(128),
                  pl.BlockSpec((GRP_PER_BLK, 256, 256), lambda l, j: (l, 0, 0)),
                  tok_spec,
                  pl.BlockSpec((1, LANE_BLK), lambda l, j: (0, l))],
        out_specs=tok_spec,
        out_shape=_lane_block_shape(b, s),
        compiler_params=_cparams(("parallel", "parallel")),
        name="s5_chunk_out",
    )(yin, hf, hb, e, u, d_skip.reshape(1, D_S5))


def _gelu_tanh(x):
    return 0.5 * x * (1.0 + jnp.tanh(math.sqrt(2.0 / math.pi) * (x + 0.044715 * (x * x * x))))


def _route(logits):
    lane = lax.broadcasted_iota(jnp.int32, logits.shape, 1).astype(F32)
    neg = jnp.float32(-1e30)
    big = jnp.float32(1e9)
    gl = jnp.where(lane < N_GROUPS, logits, neg)
    gmax = jnp.max(gl, axis=1, keepdims=True)
    gidx = jnp.min(jnp.where(gl == gmax, lane, big), axis=1, keepdims=True)
    gsum = jnp.sum(jnp.exp(gl - gmax), axis=1, keepdims=True)
    gw = 1.0 / gsum
    lo = N_GROUPS + EXP_PER_GROUP * gidx
    el = jnp.where((lane >= lo) & (lane < lo + EXP_PER_GROUP), logits, neg)
    v1 = jnp.max(el, axis=1, keepdims=True)
    i1 = jnp.min(jnp.where(el == v1, lane, big), axis=1, keepdims=True)
    el2 = jnp.where(lane == i1, neg, el)
    v2 = jnp.max(el2, axis=1, keepdims=True)
    i2 = jnp.min(jnp.where(el2 == v2, lane, big), axis=1, keepdims=True)
    ex = jnp.exp(v2 - v1)
    p1 = 1.0 / (1.0 + ex)
    p2 = ex * p1
    e1 = i1 - lo
    e2 = i2 - lo
    first = e1 < e2
    ea = jnp.where(first, e1, e2)
    eb = jnp.where(first, e2, e1)
    wa = gw * jnp.where(first, p1, p2)
    wb = gw * jnp.where(first, p2, p1)
    pair = ea * (7.0 - ea) * 0.5 + (eb - ea - 1.0)
    return wa, wb, 6.0 * gidx + pair


def _out_kernel(ypre_ref, ysc_ref, ycf_ref, x_ref, mod_ref, wglu_ref, bglu_ref, wo_ref,
                lng_ref, lnb_ref, wr_ref, br_ref, cnt0_ref, x1_ref, hx_ref, meta_ref, counts_ref, *rest):
    stage_ref = rest[0] if len(rest) == 2 else None
    cnt_ref = rest[-1]

    @pl.when((pl.program_id(0) == 0) & (pl.program_id(1) == 0))
    def _():
        cnt_ref[...] = cnt0_ref[...]

    ypre = jnp.concatenate([ypre_ref[blk, 0] for blk in range(ypre_ref.shape[0])], axis=1)
    t = _gelu_tanh(ypre)
    gate = _sigmoid(_dot(t.astype(BF16), wglu_ref[...]) + bglu_ref[...])
    ys5 = (t * gate).astype(BF16)
    y = (_dot(ys5, wo_ref[0:D_S5, :]) + _dot(ysc_ref[0], wo_ref[D_S5:D_S5 + D_SC, :])
         + _dot(ycf_ref[0], wo_ref[D_S5 + D_SC:D_MODEL, :]))
    g1 = mod_ref[0, 2:3, :]
    x1 = _layer_norm(DN_ALPHA * x_ref[0] + g1 * y, lng_ref[...], lnb_ref[...])
    x1_ref[0] = x1
    h2 = x1 * (1.0 + mod_ref[0, 4:5, :]) + mod_ref[0, 3:4, :]
    wa, wb, cls = _route(_dot(h2.astype(BF16), wr_ref[...]) + br_ref[...])

    tm = h2.shape[0]
    lane = lax.broadcasted_iota(jnp.int32, (tm, ROUTER_LANES), 1).astype(F32)
    onehot = jnp.where(lane == cls, 1.0, 0.0)
    row_i = lax.broadcasted_iota(jnp.int32, (tm, tm), 0)
    col_i = lax.broadcasted_iota(jnp.int32, (tm, tm), 1)
    earlier = jnp.where(col_i < row_i, 1.0, 0.0).astype(BF16)
    before = _dot(earlier, onehot.astype(BF16)) + cnt_ref[...]
    rank = jnp.sum(before * onehot, axis=1, keepdims=True)
    cnt_ref[...] += jnp.sum(onehot, axis=0, keepdims=True)
    counts_ref[...] = cnt_ref[...]

    meta = (jnp.where(lane == META_WA, wa, 0.0) + jnp.where(lane == META_WB, wb, 0.0)
            + jnp.where(lane == META_CLS, cls, 0.0) + jnp.where(lane == META_RANK, rank, 0.0))
    meta_ref[...] = jnp.transpose(meta)[0:SUBLANES, :]
    hx_ref[0, :, 0:D_MODEL] = h2
    hx_ref[0, :, D_MODEL:HX_LANES] = meta
    if stage_ref is not None:
        stage_ref[...] = jnp.zeros_like(stage_ref)


def _out_call(ypre, ysc, ycf, x, mod, wglu_bf, b_glu, wo_bf, ln_g, ln_b, w_router, b_router, counts0, tm,
              stage_rows):
    b, s, d = x.shape
    nt = s // tm
    row_spec = lambda n: pl.BlockSpec((1, tm, n), lambda i, j: (i, j, 0))
    full = lambda shape: pl.BlockSpec(shape, lambda i, j: (0,) * len(shape))
    out_specs = [row_spec(d), row_spec(HX_LANES),
                 pl.BlockSpec((SUBLANES, tm), lambda i, j: (0, i * nt + j)),
                 full((1, ROUTER_LANES))]
    out_shape = [jax.ShapeDtypeStruct((b, s, d), F32), jax.ShapeDtypeStruct((b, s, HX_LANES), F32),
                 jax.ShapeDtypeStruct((SUBLANES, b * s), F32),
                 jax.ShapeDtypeStruct((1, ROUTER_LANES), F32)]
    if stage_rows:
        stage_octs = stage_rows // (SUBLANES * b * nt)
        assert stage_octs * SUBLANES * b * nt == stage_rows
        out_specs.append(pl.BlockSpec((stage_octs, SUBLANES, HX_LANES), lambda i, j: (i * nt + j, 0, 0)))
        out_shape.append(jax.ShapeDtypeStruct((stage_rows // SUBLANES, SUBLANES, HX_LANES), F32))
    return pl.pallas_call(
        _out_kernel, grid=(b, nt),
        in_specs=[_lane_block_spec(tm), row_spec(D_SC), row_spec(D_CF), row_spec(d),
                  pl.BlockSpec((1, 6, d), lambda i, j: (i, 0, 0)),
                  full((D_S5, D_S5)), full((1, D_S5)), full((d, d)),
                  full((1, d)), full((1, d)), full((d, ROUTER_LANES)), full((1, ROUTER_LANES)),
                  full((1, ROUTER_LANES))],
        out_specs=out_specs, out_shape=out_shape,
        scratch_shapes=[pltpu.VMEM((1, ROUTER_LANES), F32)],
        compiler_params=_cparams(("arbitrary", "arbitrary")),
        name="out_proj",
    )(ypre, ysc, ycf, x, mod, wglu_bf, b_glu.reshape(1, -1), wo_bf, ln_g.reshape(1, -1),
      ln_b.reshape(1, -1), w_router, b_router, counts0)


def _sorted_rows(n_tok):
    return n_tok + N_CLASSES * MOE_TM


def _moe_plan(meta, counts, n_tok):
    cls = meta[META_CLS].astype(jnp.int32)
    rank = meta[META_RANK].astype(jnp.int32)
    cnt = counts[0, :N_CLASSES].astype(jnp.int32)
    n_tiles = (cnt + (MOE_TM - 1)) // MOE_TM
    ends = jnp.cumsum(n_tiles)
    starts = ends - n_tiles
    slot = starts[cls] * MOE_TM + rank
    t_max = n_tok // MOE_TM + N_CLASSES
    n_used = ends[N_CLASSES - 1]
    tile = jnp.minimum(jnp.arange(t_max, dtype=jnp.int32), n_used - 1)
    tile_cls = jnp.sum((tile[:, None] >= ends[None, :]).astype(jnp.int32), axis=1)
    group = tile_cls // 6
    pair = tile_cls % 6
    first = jnp.array([0, 0, 0, 1, 1, 2], jnp.int32)[pair] + EXP_PER_GROUP * group
    second = jnp.array([1, 2, 3, 2, 3, 3], jnp.int32)[pair] + EXP_PER_GROUP * group
    return slot, tile, first, second, n_used.reshape(1)


def _split_row(row):
    return lax.shift_right_logical(row, 3), lax.bitwise_and(row, SUBLANES - 1)


def _dispatch_kernel(slot_ref, hx_ref, xs_init_ref, xs_ref, sem):
    del xs_init_ref
    n_oct = hx_ref.shape[1]
    base = (pl.program_id(0) * pl.num_programs(1) + pl.program_id(1)) * (n_oct * SUBLANES)

    def body(i, carry):
        for k in range(SUBLANES):
            oct_id, sub = _split_row(slot_ref[base + i * SUBLANES + k])
            pltpu.make_async_copy(hx_ref.at[0, i, pl.ds(k, 1), :], xs_ref.at[oct_id, pl.ds(sub, 1), :],
                                  sem).start(priority=k % 2)
        return carry

    lax.fori_loop(0, n_oct, body, 0)
    pltpu.make_async_copy(hx_ref.at[0], xs_ref.at[pl.ds(0, n_oct)], sem).wait()


def _dispatch_call(slot, hx, xs_init, tm):
    b, s, w = hx.shape
    n_rows = xs_init.shape[0] * SUBLANES
    grid_spec = pltpu.PrefetchScalarGridSpec(
        num_scalar_prefetch=1, grid=(b, s // tm),
        in_specs=[pl.BlockSpec((1, tm // SUBLANES, SUBLANES, w), lambda i, j, slot: (i, j, 0, 0)),
                  pl.BlockSpec(memory_space=pl.ANY)],
        out_specs=pl.BlockSpec(memory_space=pl.ANY),
        scratch_shapes=[pltpu.SemaphoreType.DMA(())])
    xs = pl.pallas_call(
        _dispatch_kernel, grid_spec=grid_spec,
        out_shape=jax.ShapeDtypeStruct(xs_init.shape, F32),
        input_output_aliases={2: 0},
        compiler_params=_cparams(("arbitrary", "arbitrary")),
        name="moe_dispatch",
    )(slot, hx.reshape(b, s // SUBLANES, SUBLANES, w), xs_init)
    return xs.reshape(n_rows, w)


def _moe_kernel(tile_ref, first_ref, second_ref, nused_ref, xs_ref, wga_ref, wgb_ref, wua_ref, wub_ref,
                wda_ref, wdb_ref, ys_ref):
    del tile_ref, first_ref, second_ref
    t = pl.program_id(0)
    n_used = nused_ref[0]

    @pl.when(t < n_used)
    def _():
        x = xs_ref[...]
        xb = x[:, 0:D_MODEL].astype(BF16)

        def expert(wg_ref, wu_ref, wd_ref, w):
            gate = _dot(xb, wg_ref[0])
            up = _dot(xb, wu_ref[0])
            act = gate * _sigmoid(gate) * up * w
            return _dot(act.astype(BF16), wd_ref[0])

        wa = x[:, D_MODEL + META_WA:D_MODEL + META_WA + 1]
        wb = x[:, D_MODEL + META_WB:D_MODEL + META_WB + 1]
        ys_ref[...] = expert(wga_ref, wua_ref, wda_ref, wa) + expert(wgb_ref, wub_ref, wdb_ref, wb)

    @pl.when(t >= n_used)
    def _():
        ys_ref[...] = jnp.zeros_like(ys_ref)


def _moe_call(tile, first, second, n_used, xs, wg_bf, wu_bf, wd_bf):
    n_rows, w = xs.shape
    d = D_MODEL
    t_max = tile.shape[0]
    up_spec = lambda sel: pl.BlockSpec((1, d, D_EXPERT), lambda t, tl, fi, se, nu: ((fi, se)[sel][t], 0, 0))
    down_spec = lambda sel: pl.BlockSpec((1, D_EXPERT, d), lambda t, tl, fi, se, nu: ((fi, se)[sel][t], 0, 0))
    grid_spec = pltpu.PrefetchScalarGridSpec(
        num_scalar_prefetch=4, grid=(t_max,),
        in_specs=[pl.BlockSpec((MOE_TM, w), lambda t, tl, fi, se, nu: (tl[t], 0)),
                  up_spec(0), up_spec(1), up_spec(0), up_spec(1), down_spec(0), down_spec(1)],
        out_specs=pl.BlockSpec((MOE_TM, d), lambda t, tl, fi, se, nu: (t, 0)))
    return pl.pallas_call(
        _moe_kernel, grid_spec=grid_spec,
        out_shape=jax.ShapeDtypeStruct((n_rows, d), F32),
        compiler_params=_cparams(("arbitrary",)),
        name="moe_experts",
    )(tile, first, second, n_used, xs, wg_bf, wg_bf, wu_bf, wu_bf, wd_bf, wd_bf)


def _combine_kernel(slot_ref, x1_ref, mod_ref, lng_ref, lnb_ref, ys_ref, o_ref, f_ref, sem):
    n_oct = f_ref.shape[1]
    tm = n_oct * SUBLANES
    step = pl.program_id(0) * pl.num_programs(1) + pl.program_id(1)
    n_steps = pl.num_programs(0) * pl.num_programs(1)

    def request(which, buf):
        base = which * tm

        def body(i, carry):
            for k in range(SUBLANES):
                oct_id, sub = _split_row(slot_ref[base + i * SUBLANES + k])
                pltpu.make_async_copy(ys_ref.at[oct_id, pl.ds(sub, 1), :], f_ref.at[buf, i, pl.ds(k, 1), :],
                                      sem.at[buf]).start(priority=k % 2)
            return carry

        lax.fori_loop(0, n_oct, body, 0)

    @pl.when(step == 0)
    def _():
        request(0, 0)

    @pl.when(step + 1 < n_steps)
    def _():
        request(step + 1, (step + 1) % 2)

    buf = step % 2
    pltpu.make_async_copy(ys_ref.at[pl.ds(0, n_oct)], f_ref.at[buf], sem.at[buf]).wait()
    f = f_ref[buf].reshape(tm, f_ref.shape[3])
    g2 = mod_ref[0, 5:6, :]
    o_ref[0] = _layer_norm(DN_ALPHA * x1_ref[0] + g2 * f, lng_ref[...], lnb_ref[...])


def _combine_call(slot, x1, mod, ln_g, ln_b, ys, tm):
    b, s, d = x1.shape
    ys = ys.reshape(ys.shape[0] // SUBLANES, SUBLANES, d)
    grid_spec = pltpu.PrefetchScalarGridSpec(
        num_scalar_prefetch=1, grid=(b, s // tm),
        in_specs=[pl.BlockSpec((1, tm, d), lambda i, j, slot: (i, j, 0)),
                  pl.BlockSpec((1, 6, d), lambda i, j, slot: (i, 0, 0)),
                  pl.BlockSpec((1, d), lambda i, j, slot: (0, 0)),
                  pl.BlockSpec((1, d), lambda i, j, slot: (0, 0)),
                  pl.BlockSpec(memory_space=pl.ANY)],
        out_specs=pl.BlockSpec((1, tm, d), lambda i, j, slot: (i, j, 0)),
        scratch_shapes=[pltpu.VMEM((2, tm // SUBLANES, SUBLANES, d), F32), pltpu.SemaphoreType.DMA((2,))])
    return pl.pallas_call(
        _combine_kernel, grid_spec=grid_spec,
        out_shape=jax.ShapeDtypeStruct((b, s, d), F32),
        compiler_params=_cparams(("arbitrary", "arbitrary")),
        name="moe_combine",
    )(slot, x1, mod, ln_g.reshape(1, -1), ln_b.reshape(1, -1), ys)


def _moe_sublayer(parts, counts, xs_init, wg_bf, wu_bf, wd_bf, ln_g, ln_b):
    sizes = [p[2].shape[0] * p[2].shape[1] for p in parts]
    meta = jnp.concatenate([p[1] for p in parts], axis=1)
    slot, tile, first, second, n_used = _moe_plan(meta, counts, sum(sizes))
    slots, start = [], 0
    for n in sizes:
        slots.append(slot[start:start + n])
        start += n
    xs = xs_init
    for (hx, _, _, _, tm), sl in zip(parts, slots):
        xs = _dispatch_call(sl, hx, xs.reshape(xs_init.shape), tm)
    ys = _moe_call(tile, first, second, n_used, xs, wg_bf, wu_bf, wd_bf)
    return [_combine_call(sl, x1, mod, ln_g, ln_b, ys, tm) for (_, _, x1, mod, tm), sl in zip(parts, slots)]


def kernel(x, c, ctx, c_ctx, w_mod, b_mod, w_in, s5_a_re, s5_a_im, s5_log_dt, s5_b_re, s5_b_im, s5_c_re, s5_c_im, s5_d, w_glu, b_glu, w_sc, w_dw, b_dw, ln_cf_g, ln_cf_b, w_o, ln1_g, ln1_b, w_rg, b_rg, w_rexp, b_rexp, w_gate, w_up, w_down, ln2_g, ln2_b):
    nb, seq, d = x.shape
    n_ctx = ctx.shape[1]
    n_layers = w_mod.shape[0]
    assert seq % TILE_TOK == 0 and n_ctx % TILE_TOK == 0 and seq % GRID_W == 0

    mod_rows = 16
    assert nb + 1 <= mod_rows
    c_all = jnp.concatenate([c, c_ctx[None, :], jnp.zeros((mod_rows - nb - 1, d), F32)], axis=0)
    mod_all = _mod_call(c_all, w_mod, b_mod)

    pad_r = ROUTER_LANES - N_GROUPS - N_EXPERTS
    x_lat, x_ctx = x, ctx
    for l in range(n_layers):
        last = l == n_layers - 1
        mod_lat = mod_all[l, :nb].reshape(nb, 6, d)
        mod_ctx = jnp.broadcast_to(mod_all[l, nb].reshape(1, 6, d), (nb, 6, d))
        w_in_bf = w_in[l].astype(BF16)
        wglu_bf = w_glu[l].astype(BF16)
        wo_bf = w_o[l].astype(BF16)
        wg_bf = w_gate[l].astype(BF16)
        wu_bf = w_up[l].astype(BF16)
        wd_bf = w_down[l].astype(BF16)
        w_router = jnp.concatenate([w_rg[l], w_rexp[l], jnp.zeros((d, pad_r), F32)], axis=1).astype(BF16)
        b_router = jnp.concatenate([b_rg[l], b_rexp[l], jnp.zeros((pad_r,), F32)]).reshape(1, -1)
        w1, e_op, decay = _s5_operators(s5_a_re[l], s5_a_im[l], s5_log_dt[l], s5_b_re[l], s5_b_im[l],
                                        s5_c_re[l], s5_c_im[l])

        u_l, bg_l, cv_l, gl_l = _in_call(x_lat, mod_lat, w_in_bf, 512, False)
        if last:
            u_c = _in_call(x_ctx, mod_ctx, w_in_bf[:, :D_S5], TILE_TOK, True)
        else:
            u_c, bg_c, cv_c, gl_c = _in_call(x_ctx, mod_ctx, w_in_bf, TILE_TOK, False)

        yin_l, gf_l, gb_l = _s5a_call(u_l, w1)
        yin_c, gf_c, gb_c = _s5a_call(u_c, w1)
        hf_c, hf_l, hb_c, hb_l = _s5b_call(decay, gf_c, gf_l, gb_c, gb_l, nb)
        ypre_l = _s5c_call(yin_l, hf_l, hb_l, e_op, u_l, s5_d[l])

        ysc_l, ycf_l = _conv_call(bg_l, cv_l, gl_l, w_sc[l], w_dw[l], b_dw[l], ln_cf_g[l], ln_cf_b[l], True)
        parts = []
        counts = jnp.zeros((1, ROUTER_LANES), F32)
        n_moe = nb * seq
        if not last:
            ypre_c = _s5c_call(yin_c, hf_c, hb_c, e_op, u_c, s5_d[l])
            ysc_c, ycf_c = _conv_call(bg_c, cv_c, gl_c, w_sc[l], w_dw[l], b_dw[l], ln_cf_g[l], ln_cf_b[l], False)
            x1_c, hx_c, meta_c, counts = _out_call(ypre_c, ysc_c, ycf_c, x_ctx, mod_ctx, wglu_bf, b_glu[l], wo_bf,
                                                   ln1_g[l], ln1_b[l], w_router, b_router, counts, TILE_TOK, 0)
            parts.append((hx_c, meta_c, x1_c, mod_ctx, TILE_TOK))
            n_moe += nb * n_ctx
        x1_l, hx_l, meta_l, counts, stage = _out_call(ypre_l, ysc_l, ycf_l, x_lat, mod_lat, wglu_bf, b_glu[l], wo_bf,
                                                      ln1_g[l], ln1_b[l], w_router, b_router, counts, 512,
                                                      _sorted_rows(n_moe))
        parts.append((hx_l, meta_l, x1_l, mod_lat, 512))
        outs = _moe_sublayer(parts, counts, stage, wg_bf, wu_bf, wd_bf, ln2_g[l], ln2_b[l])
        x_lat = outs[-1]
        if not last:
            x_ctx = outs[0]
    return x_lat
```

```python
import functools
import math

import jax
import jax.numpy as jnp
from jax import lax
from jax.experimental import pallas as pl
from jax.experimental.pallas import tpu as pltpu

F32 = jnp.float32
BF16 = jnp.bfloat16

D_MODEL = 1024
DEPTH = 2
GRID_W = 64
D_S5 = 512
S5_GROUP = 16
S5_GROUPS = 32
S5_STATE = 64
D_SC = 256
D_CF = 256
CF_WIDTH = 31
CF_HALF = 15
D_IN = 1792
N_GROUPS = 4
EXP_PER_GROUP = 4
N_EXPERTS = 16
D_EXPERT = 256
DN_ALPHA = (2 * DEPTH) ** 0.25
LN_EPS = 1e-5

CHUNK = 16
TILE_CHUNKS = 16
TILE_TOK = CHUNK * TILE_CHUNKS
LANE_BLK = 128
GRP_PER_BLK = LANE_BLK // S5_GROUP
ROUTER_LANES = 128
HX_LANES = D_MODEL + ROUTER_LANES
META_WA, META_WB, META_CLS, META_RANK = 0, 1, 2, 3
N_CLASSES = N_GROUPS * 6
MOE_TM = 512
OUT_SUBTILES = 2
SUBLANES = 8
VMEM_LIMIT = 56 * 1024 * 1024


def _cparams(sem):
    return pltpu.CompilerParams(dimension_semantics=sem, vmem_limit_bytes=VMEM_LIMIT)


def _split_bf16(a):
    hi = a.astype(BF16)
    lo = (a - hi.astype(F32)).astype(BF16)
    return hi, lo


def _dot(a, b):
    return jnp.dot(a, b, preferred_element_type=F32)


def _dot3(a, b):
    ah, al = _split_bf16(a)
    bh, bl = _split_bf16(b)
    return _dot(ah, bh) + (_dot(al, bh) + _dot(ah, bl))


def _sigmoid(x):
    return 1.0 / (1.0 + jnp.exp(-x))


def _layer_norm(x, g, b):
    mu = jnp.mean(x, axis=-1, keepdims=True)
    xc = x - mu
    var = jnp.mean(xc * xc, axis=-1, keepdims=True)
    return xc * lax.rsqrt(var + LN_EPS) * g + b


def _mod_kernel(c_ref, w_ref, b_ref, o_ref):
    c = c_ref[...]
    s = c * _sigmoid(c)
    o_ref[0] = _dot3(s, w_ref[0]) + b_ref[0]


def _mod_call(c_all, w_mod, b_mod):
    n_layers, d, n_out = w_mod.shape
    tn = 1536
    rows = c_all.shape[0]
    return pl.pallas_call(
        _mod_kernel,
        grid=(n_layers, n_out // tn),
        in_specs=[
            pl.BlockSpec((rows, d), lambda l, j: (0, 0)),
            pl.BlockSpec((1, d, tn), lambda l, j: (l, 0, j)),
            pl.BlockSpec((1, 1, tn), lambda l, j: (l, 0, j)),
        ],
        out_specs=pl.BlockSpec((1, rows, tn), lambda l, j: (l, 0, j)),
        out_shape=jax.ShapeDtypeStruct((n_layers, rows, n_out), F32),
        compiler_params=_cparams(("parallel", "parallel")),
        name="mod",
    )(c_all, w_mod, b_mod.reshape(n_layers, 1, n_out))


def _in_kernel(x_ref, mod_ref, w_ref, u_ref, bg_ref, cv_ref, gl_ref):
    x = x_ref[0]
    sh = mod_ref[0, 0:1, :]
    sc = mod_ref[0, 1:2, :]
    h = (x * (1.0 + sc) + sh).astype(BF16)
    z = _dot(h, w_ref[...])
    _store_lane_blocks(u_ref, z[:, 0:512])
    bg_ref[0] = z[:, 512:768]
    cv_ref[0] = z[:, 768:1024] * z[:, 1024:1280]
    gl_ref[0] = z[:, 1280:1536] * _sigmoid(z[:, 1536:1792])


def _in_u_kernel(x_ref, mod_ref, w_ref, u_ref):
    x = x_ref[0]
    sh = mod_ref[0, 0:1, :]
    sc = mod_ref[0, 1:2, :]
    h = (x * (1.0 + sc) + sh).astype(BF16)
    _store_lane_blocks(u_ref, _dot(h, w_ref[...]))


def _store_lane_blocks(ref, val):
    for blk in range(ref.shape[0]):
        ref[blk, 0] = val[:, blk * LANE_BLK:(blk + 1) * LANE_BLK]


def _lane_block_spec(tm):
    return pl.BlockSpec((D_S5 // LANE_BLK, 1, tm, LANE_BLK), lambda i, j: (0, i, j, 0))


def _lane_block_shape(b, s):
    return jax.ShapeDtypeStruct((D_S5 // LANE_BLK, b, s, LANE_BLK), F32)


def _in_call(x, mod, w_in_bf, tm, u_only):
    b, s, d = x.shape
    grid = (b, s // tm)
    row_spec = lambda n: pl.BlockSpec((1, tm, n), lambda i, j: (i, j, 0))
    in_specs = [
        row_spec(d),
        pl.BlockSpec((1, 6, d), lambda i, j: (i, 0, 0)),
    ]
    if u_only:
        in_specs.append(pl.BlockSpec((d, D_S5), lambda i, j: (0, 0)))
        return pl.pallas_call(
            _in_u_kernel, grid=grid, in_specs=in_specs,
            out_specs=_lane_block_spec(tm),
            out_shape=_lane_block_shape(b, s),
            compiler_params=_cparams(("parallel", "parallel")),
            name="in_proj_u",
        )(x, mod, w_in_bf)
    in_specs.append(pl.BlockSpec((d, D_IN), lambda i, j: (0, 0)))
    return pl.pallas_call(
        _in_kernel, grid=grid, in_specs=in_specs,
        out_specs=[_lane_block_spec(tm), row_spec(D_SC), row_spec(D_SC), row_spec(D_CF)],
        out_shape=[_lane_block_shape(b, s),
                   jax.ShapeDtypeStruct((b, s, D_SC), F32),
                   jax.ShapeDtypeStruct((b, s, D_SC), F32),
                   jax.ShapeDtypeStruct((b, s, D_CF), F32)],
        compiler_params=_cparams(("parallel", "parallel")),
        name="in_proj",
    )(x, mod, w_in_bf)


def _conv_tail(t, bdw_ref, lng_ref, lnb_ref):
    t = t + bdw_ref[...]
    t = _layer_norm(t, lng_ref[...], lnb_ref[...])
    return t * _sigmoid(t)


def _conv_grid_kernel(bg_ref, cv_ref, gl_ref, wsc_ref, wdw_ref, bdw_ref, lng_ref, lnb_ref,
                      ysc_ref, ycf_ref, pad_ref, t_ref):
    s = cv_ref.shape[1]
    rows = s // GRID_W
    cv = cv_ref[0]
    col = lax.broadcasted_iota(jnp.int32, (s, D_SC), 0) % GRID_W
    prev = jnp.where(col == 0, 0.0, pltpu.roll(cv, 1, axis=0))
    nxt = jnp.where(col == GRID_W - 1, 0.0, pltpu.roll(cv, s - 1, axis=0))
    conv = prev * wsc_ref[0:1, :] + cv * wsc_ref[1:2, :] + nxt * wsc_ref[2:3, :]
    ysc_ref[0] = (bg_ref[0] * conv).astype(ysc_ref.dtype)

    zero = jnp.zeros((CF_HALF, GRID_W, D_CF), F32)
    pad_ref[0:CF_HALF] = zero
    pad_ref[CF_HALF + rows:CF_HALF + rows + CF_HALF] = zero
    pad_ref[CF_HALF:CF_HALF + rows] = gl_ref[0].reshape(rows, GRID_W, D_CF)

    def body(i, carry):
        w0 = pl.multiple_of(i * 8, 8)
        for half in range(D_CF // 128):
            lanes = slice(half * 128, (half + 1) * 128)
            acc = jnp.zeros((rows, 8, 128), F32)
            for k in range(CF_WIDTH):
                acc = acc + pad_ref[k:k + rows, pl.ds(w0, 8), lanes] * wdw_ref[k:k + 1, lanes]
            t_ref[:, pl.ds(w0, 8), lanes] = acc
        return carry

    lax.fori_loop(0, GRID_W // 8, body, 0)
    t = t_ref[...].reshape(s, D_CF)
    ycf_ref[0] = _conv_tail(t, bdw_ref, lng_ref, lnb_ref).astype(ycf_ref.dtype)


def _conv_seq_kernel(bg_ref, cv_ref, gl_ref, wsc_ref, wdw_ref, bdw_ref, lng_ref, lnb_ref,
                     ysc_ref, ycf_ref, pad_ref):
    s = cv_ref.shape[1]
    cv = cv_ref[0]
    pos = lax.broadcasted_iota(jnp.int32, (s, D_SC), 0)
    prev = jnp.where(pos == 0, 0.0, pltpu.roll(cv, 1, axis=0))
    nxt = jnp.where(pos == s - 1, 0.0, pltpu.roll(cv, s - 1, axis=0))
    conv = prev * wsc_ref[0:1, :] + cv * wsc_ref[1:2, :] + nxt * wsc_ref[2:3, :]
    ysc_ref[0] = (bg_ref[0] * conv).astype(ysc_ref.dtype)

    off = 16
    pad_ref[0:off] = jnp.zeros((off, D_CF), F32)
    pad_ref[off + s:off + s + 16] = jnp.zeros((16, D_CF), F32)
    pad_ref[off:off + s] = gl_ref[0]
    acc = jnp.zeros((s, D_CF), F32)
    for k in range(CF_WIDTH):
        acc = acc + pad_ref[pl.ds(off - CF_HALF + k, s), :] * wdw_ref[k:k + 1, :]
    ycf_ref[0] = _conv_tail(acc, bdw_ref, lng_ref, lnb_ref).astype(ycf_ref.dtype)


def _conv_call(bg, cv, gl, w_sc, w_dw, b_dw, ln_g, ln_b, grid_mode):
    b, s, _ = bg.shape
    row_spec = pl.BlockSpec((1, s, D_SC), lambda i: (i, 0, 0))
    full = lambda shape: pl.BlockSpec(shape, lambda i: (0,) * len(shape))
    if grid_mode:
        rows = s // GRID_W
        kern = _conv_grid_kernel
        scratch = [pltpu.VMEM((rows + 2 * CF_HALF, GRID_W, D_CF), F32),
                   pltpu.VMEM((rows, GRID_W, D_CF), F32)]
        name = "conv_grid"
    else:
        kern = _conv_seq_kernel
        scratch = [pltpu.VMEM((s + 32, D_CF), F32)]
        name = "conv_seq"
    return pl.pallas_call(
        kern, grid=(b,),
        in_specs=[row_spec, row_spec, row_spec, full((3, D_SC)), full((CF_WIDTH, D_CF)),
                  full((1, D_CF)), full((1, D_CF)), full((1, D_CF))],
        out_specs=[row_spec, row_spec],
        out_shape=[jax.ShapeDtypeStruct((b, s, D_SC), BF16), jax.ShapeDtypeStruct((b, s, D_CF), BF16)],
        scratch_shapes=scratch,
        compiler_params=_cparams(("parallel",)),
        name=name,
    )(bg, cv, gl, w_sc, w_dw, b_dw.reshape(1, -1), ln_g.reshape(1, -1), ln_b.reshape(1, -1))


def _split3(a):
    hi = a.astype(BF16)
    r = a - hi.astype(F32)
    mid = r.astype(BF16)
    lo = (r - mid.astype(F32)).astype(BF16)
    return hi, mid, lo


def _select_cols(a, sel):
    hi, mid, lo = _split3(a)
    return _dot(hi, sel) + (_dot(mid, sel) + _dot(lo, sel))


def _select_rows(sel, a):
    hi, mid, lo = _split3(a)
    return _dot(sel, hi) + (_dot(sel, mid) + _dot(sel, lo))


def _cmul(ar, ai, br, bi):
    return ar * br - ai * bi, ar * bi + ai * br


def _s5_prep_kernel(are_r, aim_r, are_c, aim_c, ldt, bre_ref, bim_ref, cre_ref, cim_ref,
                    w1_ref, e_ref, dec_ref):
    t, n, p = CHUNK, S5_GROUP, S5_STATE
    width = t * n
    lane_tok = lax.broadcasted_iota(jnp.int32, (128, width), 1) // n
    pow_id = lax.broadcasted_iota(jnp.int32, (128, width), 0)
    onehot = lambda cond: jnp.where(cond, 1.0, 0.0).astype(BF16)
    sel_fwd = onehot(pow_id == lane_tok)
    sel_rev = onehot(pow_id == t - 1 - lane_tok)
    sel_out = onehot(pow_id == t - lane_tok)
    row_tok = lax.broadcasted_iota(jnp.int32, (width, 32), 0) // n
    row_pow = lax.broadcasted_iota(jnp.int32, (width, 32), 1)
    rsel_rev = onehot(row_pow == t - 1 - row_tok)
    rsel_fwd = onehot(row_pow == row_tok)
    lane = lax.broadcasted_iota(jnp.int32, (n, width), 1)

    jc = jnp.minimum(lax.broadcasted_iota(jnp.int32, (p, 128), 1), t).astype(F32)
    jr = jnp.minimum(lax.broadcasted_iota(jnp.int32, (32, p), 0), t).astype(F32)

    strips = []
    f_parts = []
    e_parts = []
    for d in range(2):
        dt = jnp.exp(ldt[d, 0])
        mag_c = are_c[d, 0] * dt
        th_c = aim_c[d, 0] * dt
        ec = jnp.exp(jc * mag_c)
        qr = ec * jnp.cos(jc * th_c)
        qi = ec * jnp.sin(jc * th_c)
        a_re = are_r[d, 0]
        a_im = aim_r[d, 0]
        er = jnp.exp(jr * (a_re * dt))
        pr = er * jnp.cos(jr * (a_im * dt))
        pi = er * jnp.sin(jr * (a_im * dt))
        nr = pr[1:2] - 1.0
        ni = pi[1:2]
        den = a_re * a_re + a_im * a_im
        fre = (nr * a_re + ni * a_im) / den
        fim = (ni * a_re - nr * a_im) / den
        bt_re = bre_ref[d, 0].T
        bt_im = bim_ref[d, 0].T
        bb_re, bb_im = _cmul(fre, fim, bt_re, bt_im)
        ct_re = jnp.concatenate([cre_ref[d, 0].T] * t, axis=1)
        ct_im = jnp.concatenate([cim_ref[d, 0].T] * t, axis=1)
        sel = sel_fwd if d == 0 else sel_rev
        w_re, w_im = _cmul(ct_re, ct_im, _select_cols(qr, sel), _select_cols(qi, sel))
        strips.append(_dot3(bb_re, w_re) - _dot3(bb_im, w_im))
        if d == 0:
            o_re, o_im = _cmul(w_re, w_im, qr[:, 1:2], qi[:, 1:2])
        else:
            o_re, o_im = _cmul(ct_re, ct_im, _select_cols(qr, sel_out), _select_cols(qi, sel_out))
        e_parts += [o_re, -o_im]
        rsel = rsel_rev if d == 0 else rsel_fwd
        f_re, f_im = _cmul(jnp.concatenate([bb_re] * t, axis=0), jnp.concatenate([bb_im] * t, axis=0),
                           _select_rows(rsel, pr), _select_rows(rsel, pi))
        f_parts += [f_re, f_im]
        dec_ref[0, 2 * d:2 * d + 1, :] = jnp.concatenate([pr[t:t + 1], pr[t:t + 1]], axis=1)
        dec_ref[0, 2 * d + 1:2 * d + 2, :] = jnp.concatenate([-pi[t:t + 1], pi[t:t + 1]], axis=1)

    blocks = []
    for s in range(t):
        fwd = strips[0] if s == 0 else jnp.where(lane >= n * s, pltpu.roll(strips[0], n * s, axis=1), 0.0)
        back = t - 1 - s
        bwd = strips[1] if back == 0 else jnp.where(lane < width - n * back,
                                                     pltpu.roll(strips[1], width - n * back, axis=1), 0.0)
        blocks.append(fwd + bwd)
    m = jnp.concatenate(blocks, axis=0)
    w1_ref[0] = jnp.concatenate([m] + f_parts, axis=1).astype(BF16)
    e_ref[0] = jnp.concatenate(e_parts, axis=0).astype(BF16)


def _s5_operators(a_re, a_im, log_dt, b_re, b_im, c_re, c_im):
    g, p, n, t = S5_GROUPS, S5_STATE, S5_GROUP, CHUNK
    spec = lambda shape: pl.BlockSpec((2, 1) + shape, lambda i: (0, i) + (0,) * len(shape))
    return pl.pallas_call(
        _s5_prep_kernel, grid=(g,),
        in_specs=[spec((1, p)), spec((1, p)), spec((p, 1)), spec((p, 1)), spec((1, 1)),
                  spec((p, n)), spec((p, n)), spec((n, p)), spec((n, p))],
        out_specs=[pl.BlockSpec((1, t * n, 2 * t * n), lambda i: (i, 0, 0)),
                   pl.BlockSpec((1, 4 * p, t * n), lambda i: (i, 0, 0)),
                   pl.BlockSpec((1, 4, 2 * p), lambda i: (i, 0, 0))],
        out_shape=[jax.ShapeDtypeStruct((g, t * n, 2 * t * n), BF16),
                   jax.ShapeDtypeStruct((g, 4 * p, t * n), BF16),
                   jax.ShapeDtypeStruct((g, 4, 2 * p), F32)],
        compiler_params=_cparams(("parallel",)),
        name="s5_prep",
    )(a_re.reshape(2, g, 1, p), a_im.reshape(2, g, 1, p), a_re.reshape(2, g, p, 1),
      a_im.reshape(2, g, p, 1), log_dt.reshape(2, g, 1, 1), b_re, b_im, c_re, c_im)


def _block_transpose8(ps):
    ps = list(ps)
    blk = lax.broadcasted_iota(jnp.int32, ps[0].shape, 1) // S5_GROUP
    for k in range(3):
        step = 1 << k
        shift = S5_GROUP * step
        keep = ((blk >> k) & 1) == 0
        for a in range(8):
            if a & step:
                continue
            pa, pb = ps[a], ps[a + step]
            ps[a] = jnp.where(keep, pa, pltpu.roll(pb, shift, axis=1))
            ps[a + step] = jnp.where(keep, pltpu.roll(pa, 128 - shift, axis=1), pb)
    return ps


def _s5a_kernel(u_ref, w1_ref, yin_ref, gf_ref, gb_ref):
    nb = u_ref.shape[1]
    xs = []
    for s in range(CHUNK):
        parts = [u_ref[0, b, pl.ds(s, TILE_CHUNKS, stride=CHUNK), :] for b in range(nb)]
        xs.append(jnp.concatenate(parts, axis=0))
    lo = _block_transpose8(xs[:8])
    hi = _block_transpose8(xs[8:])
    for j in range(GRP_PER_BLK):
        og = jnp.concatenate([lo[j], hi[j]], axis=1)
        r = _dot(og.astype(BF16), w1_ref[j])
        yin_ref[j] = r[:, 0:256]
        gf_ref[j] = r[:, 256:384]
        gb_ref[j] = r[:, 384:512]


def _s5a_call(u, w1):
    n_blk, b, s, _ = u.shape
    nt = s // TILE_TOK
    rows = b * TILE_CHUNKS
    out_spec = lambda n: pl.BlockSpec((GRP_PER_BLK, rows, n), lambda l, j: (l, j, 0))
    return pl.pallas_call(
        _s5a_kernel, grid=(n_blk, nt),
        in_specs=[pl.BlockSpec((1, b, TILE_TOK, LANE_BLK), lambda l, j: (l, 0, j, 0)),
                  pl.BlockSpec((GRP_PER_BLK, 256, 512), lambda l, j: (l, 0, 0))],
        out_specs=[out_spec(256), out_spec(128), out_spec(128)],
        out_shape=[jax.ShapeDtypeStruct((S5_GROUPS, nt * rows, 256), F32),
                   jax.ShapeDtypeStruct((S5_GROUPS, nt * rows, 128), F32),
                   jax.ShapeDtypeStruct((S5_GROUPS, nt * rows, 128), F32)],
        compiler_params=_cparams(("parallel", "parallel")),
        name="s5_chunk_in",
    )(u, w1)


def _s5b_kernel(nb, a_ref, gfc_ref, gfl_ref, gbc_ref, gbl_ref, hfc_ref, hfl_ref, hbc_ref, hbl_ref):
    gb = a_ref.shape[0]
    rows = nb * TILE_CHUNKS
    n_lat = gfl_ref.shape[1] // rows
    a1f = [jnp.broadcast_to(a_ref[g, 0:1, :], (nb, 128)) for g in range(gb)]
    a2f = [jnp.broadcast_to(a_ref[g, 1:2, :], (nb, 128)) for g in range(gb)]
    a1b = [jnp.broadcast_to(a_ref[g, 2:3, :], (nb, 128)) for g in range(gb)]
    a2b = [jnp.broadcast_to(a_ref[g, 3:4, :], (nb, 128)) for g in range(gb)]

    def step(state, a1, a2, g_ref, h_ref, g, row):
        h, hs = state
        h_ref[g, row, :] = h
        inp = g_ref[g, row, :]
        return a1 * h + a2 * hs + inp, a1 * hs - a2 * h + pltpu.roll(inp, 64, axis=1)

    zero = jnp.zeros((nb, 128), F32)
    hf = [(zero, zero) for _ in range(gb)]
    hb = [(zero, zero) for _ in range(gb)]
    for ci in range(TILE_CHUNKS):
        rf = pl.ds(ci, nb, stride=TILE_CHUNKS)
        rb = pl.ds(TILE_CHUNKS - 1 - ci, nb, stride=TILE_CHUNKS)
        for g in range(gb):
            hf[g] = step(hf[g], a1f[g], a2f[g], gfc_ref, hfc_ref, g, rf)
            hb[g] = step(hb[g], a1b[g], a2b[g], gbc_ref, hbc_ref, g, rb)

    def body(j, carry):
        hf, hb = carry
        hf = list(hf)
        hb = list(hb)
        base_f = j * rows
        base_b = (n_lat - 1 - j) * rows
        for ci in range(TILE_CHUNKS):
            rf = pl.ds(base_f + ci, nb, stride=TILE_CHUNKS)
            rb = pl.ds(base_b + (TILE_CHUNKS - 1 - ci), nb, stride=TILE_CHUNKS)
            for g in range(gb):
                hf[g] = step(hf[g], a1f[g], a2f[g], gfl_ref, hfl_ref, g, rf)
                hb[g] = step(hb[g], a1b[g], a2b[g], gbl_ref, hbl_ref, g, rb)
        return tuple(hf), tuple(hb)

    lax.fori_loop(0, n_lat, body, (tuple(hf), tuple(hb)))


def _s5b_call(decay, gf_c, gf_l, gb_c, gb_l, nb):
    gblk = 4
    spec = lambda a: pl.BlockSpec((gblk, a.shape[1], 128), lambda i: (i, 0, 0))
    sds = lambda a: jax.ShapeDtypeStruct(a.shape, F32)
    return pl.pallas_call(
        functools.partial(_s5b_kernel, nb), grid=(S5_GROUPS // gblk,),
        in_specs=[pl.BlockSpec((gblk, 4, 128), lambda i: (i, 0, 0)),
                  spec(gf_c), spec(gf_l), spec(gb_c), spec(gb_l)],
        out_specs=[spec(gf_c), spec(gf_l), spec(gb_c), spec(gb_l)],
        out_shape=[sds(gf_c), sds(gf_l), sds(gb_c), sds(gb_l)],
        compiler_params=_cparams(("parallel",)),
        name="s5_state_scan",
    )(decay, gf_c, gf_l, gb_c, gb_l)


def _s5c_kernel(yin_ref, hf_ref, hb_ref, e_ref, u_ref, d_ref, y_ref):
    nb = u_ref.shape[1]
    ys = []
    for j in range(GRP_PER_BLK):
        h = jnp.concatenate([hf_ref[j], hb_ref[j]], axis=1).astype(BF16)
        ys.append(yin_ref[j] + _dot(h, e_ref[j]))
    at = (_block_transpose8([y[:, :128] for y in ys])
          + _block_transpose8([y[:, 128:] for y in ys]))
    d = d_ref[...]
    for t in range(CHUNK):
        for b in range(nb):
            rows = pl.ds(t, TILE_CHUNKS, stride=CHUNK)
            y_ref[0, b, rows, :] = at[t][b * TILE_CHUNKS:(b + 1) * TILE_CHUNKS] + d * u_ref[0, b, rows, :]


def _s5c_call(yin, hf, hb, e, u, d_skip):
    n_blk, b, s, _ = u.shape
    nt = s // TILE_TOK
    rows = b * TILE_CHUNKS
    gspec = lambda n: pl.BlockSpec((GRP_PER_BLK, rows, n), lambda l, j: (l, j, 0))
    tok_spec = pl.BlockSpec((1, b, TILE_TOK, LANE_BLK), lambda l, j: (l, 0, j, 0))
    return pl.pallas_call(
        _s5c_kernel, grid=(n_blk, nt),
        in_specs=[gspec(256), gspec(128), gspec(128),
                  pl.BlockSpec((GRP_PER_BLK, 256, 256), lambda l, j: (l, 0, 0)),
                  tok_spec,
                  pl.BlockSpec((1, LANE_BLK), lambda l, j: (0, l))],
        out_specs=tok_spec,
        out_shape=_lane_block_shape(b, s),
        compiler_params=_cparams(("parallel", "parallel")),
        name="s5_chunk_out",
    )(yin, hf, hb, e, u, d_skip.reshape(1, D_S5))


def _gelu_tanh(x):
    return 0.5 * x * (1.0 + jnp.tanh(math.sqrt(2.0 / math.pi) * (x + 0.044715 * (x * x * x))))


def _route(logits):
    lane = lax.broadcasted_iota(jnp.int32, logits.shape, 1).astype(F32)
    neg = jnp.float32(-1e30)
    big = jnp.float32(1e9)
    gl = jnp.where(lane < N_GROUPS, logits, neg)
    gmax = jnp.max(gl, axis=1, keepdims=True)
    gidx = jnp.min(jnp.where(gl == gmax, lane, big), axis=1, keepdims=True)
    gsum = jnp.sum(jnp.exp(gl - gmax), axis=1, keepdims=True)
    gw = 1.0 / gsum
    lo = N_GROUPS + EXP_PER_GROUP * gidx
    el = jnp.where((lane >= lo) & (lane < lo + EXP_PER_GROUP), logits, neg)
    v1 = jnp.max(el, axis=1, keepdims=True)
    i1 = jnp.min(jnp.where(el == v1, lane, big), axis=1, keepdims=True)
    el2 = jnp.where(lane == i1, neg, el)
    v2 = jnp.max(el2, axis=1, keepdims=True)
    i2 = jnp.min(jnp.where(el2 == v2, lane, big), axis=1, keepdims=True)
    ex = jnp.exp(v2 - v1)
    p1 = 1.0 / (1.0 + ex)
    p2 = ex * p1
    e1 = i1 - lo
    e2 = i2 - lo
    first = e1 < e2
    ea = jnp.where(first, e1, e2)
    eb = jnp.where(first, e2, e1)
    wa = gw * jnp.where(first, p1, p2)
    wb = gw * jnp.where(first, p2, p1)
    pair = ea * (7.0 - ea) * 0.5 + (eb - ea - 1.0)
    return wa, wb, 6.0 * gidx + pair


def _out_kernel(ypre_ref, ysc_ref, ycf_ref, x_ref, mod_ref, wglu_ref, bglu_ref, wo_ref,
                lng_ref, lnb_ref, wr_ref, br_ref, cnt0_ref, x1_ref, hx_ref, meta_ref, counts_ref, *rest):
    stage_ref = rest[0] if len(rest) == 2 else None
    cnt_ref = rest[-1]

    @pl.when((pl.program_id(0) == 0) & (pl.program_id(1) == 0))
    def _():
        cnt_ref[...] = cnt0_ref[...]

    tm = x_ref.shape[1]
    sub = tm // OUT_SUBTILES

    def row_chain(r0):
        rows = slice(r0, r0 + sub)
        ypre = jnp.concatenate([ypre_ref[blk, 0, rows, :] for blk in range(ypre_ref.shape[0])], axis=1)
        t = _gelu_tanh(ypre)
        gate = _sigmoid(_dot(t.astype(BF16), wglu_ref[...]) + bglu_ref[...])
        ys5 = (t * gate).astype(BF16)
        y = (_dot(ys5, wo_ref[0:D_S5, :]) + _dot(ysc_ref[0, rows, :], wo_ref[D_S5:D_S5 + D_SC, :])
             + _dot(ycf_ref[0, rows, :], wo_ref[D_S5 + D_SC:D_MODEL, :]))
        g1 = mod_ref[0, 2:3, :]
        x1 = _layer_norm(DN_ALPHA * x_ref[0, rows, :] + g1 * y, lng_ref[...], lnb_ref[...])
        x1_ref[0, rows, :] = x1
        h2 = x1 * (1.0 + mod_ref[0, 4:5, :]) + mod_ref[0, 3:4, :]
        hx_ref[0, rows, 0:D_MODEL] = h2
        return _route(_dot(h2.astype(BF16), wr_ref[...]) + br_ref[...])

    routed = [row_chain(r0) for r0 in range(0, tm, sub)]
    wa, wb, cls = (jnp.concatenate([r[i] for r in routed], axis=0) for i in range(3))

    lane = lax.broadcasted_iota(jnp.int32, (tm, ROUTER_LANES), 1).astype(F32)
    onehot = jnp.where(lane == cls, 1.0, 0.0)
    row_i = lax.broadcasted_iota(jnp.int32, (tm, tm), 0)
    col_i = lax.broadcasted_iota(jnp.int32, (tm, tm), 1)
    earlier = jnp.where(col_i < row_i, 1.0, 0.0).astype(BF16)
    before = _dot(earlier, onehot.astype(BF16)) + cnt_ref[...]
    rank = jnp.sum(before * onehot, axis=1, keepdims=True)
    cnt_ref[...] += jnp.sum(onehot, axis=0, keepdims=True)
    counts_ref[...] = cnt_ref[...]

    meta = (jnp.where(lane == META_WA, wa, 0.0) + jnp.where(lane == META_WB, wb, 0.0)
            + jnp.where(lane == META_CLS, cls, 0.0) + jnp.where(lane == META_RANK, rank, 0.0))
    meta_ref[...] = jnp.transpose(meta)[0:SUBLANES, :]
    hx_ref[0, :, D_MODEL:HX_LANES] = meta
    if stage_ref is not None:
        stage_ref[...] = jnp.zeros_like(stage_ref)


def _out_call(ypre, ysc, ycf, x, mod, wglu_bf, b_glu, wo_bf, ln_g, ln_b, w_router, b_router, counts0, tm,
              stage_rows):
    b, s, d = x.shape
    nt = s // tm
    row_spec = lambda n: pl.BlockSpec((1, tm, n), lambda i, j: (i, j, 0))
    full = lambda shape: pl.BlockSpec(shape, lambda i, j: (0,) * len(shape))
    out_specs = [row_spec(d), row_spec(HX_LANES),
                 pl.BlockSpec((SUBLANES, tm), lambda i, j: (0, i * nt + j)),
                 full((1, ROUTER_LANES))]
    out_shape = [jax.ShapeDtypeStruct((b, s, d), F32), jax.ShapeDtypeStruct((b, s, HX_LANES), F32),
                 jax.ShapeDtypeStruct((SUBLANES, b * s), F32),
                 jax.ShapeDtypeStruct((1, ROUTER_LANES), F32)]
    if stage_rows:
        stage_octs = stage_rows // (SUBLANES * b * nt)
        assert stage_octs * SUBLANES * b * nt == stage_rows
        out_specs.append(pl.BlockSpec((stage_octs, SUBLANES, HX_LANES), lambda i, j: (i * nt + j, 0, 0)))
        out_shape.append(jax.ShapeDtypeStruct((stage_rows // SUBLANES, SUBLANES, HX_LANES), F32))
    return pl.pallas_call(
        _out_kernel, grid=(b, nt),
        in_specs=[_lane_block_spec(tm), row_spec(D_SC), row_spec(D_CF), row_spec(d),
                  pl.BlockSpec((1, 6, d), lambda i, j: (i, 0, 0)),
                  full((D_S5, D_S5)), full((1, D_S5)), full((d, d)),
                  full((1, d)), full((1, d)), full((d, ROUTER_LANES)), full((1, ROUTER_LANES)),
                  full((1, ROUTER_LANES))],
        out_specs=out_specs, out_shape=out_shape,
        scratch_shapes=[pltpu.VMEM((1, ROUTER_LANES), F32)],
        compiler_params=_cparams(("arbitrary", "arbitrary")),
        name="out_proj",
    )(ypre, ysc, ycf, x, mod, wglu_bf, b_glu.reshape(1, -1), wo_bf, ln_g.reshape(1, -1),
      ln_b.reshape(1, -1), w_router, b_router, counts0)


def _sorted_rows(n_tok):
    return n_tok + N_CLASSES * MOE_TM


def _moe_plan(meta, counts, n_tok):
    cls = meta[META_CLS].astype(jnp.int32)
    rank = meta[META_RANK].astype(jnp.int32)
    cnt = counts[0, :N_CLASSES].astype(jnp.int32)
    n_tiles = (cnt + (MOE_TM - 1)) // MOE_TM
    ends = jnp.cumsum(n_tiles)
    starts = ends - n_tiles
    slot = starts[cls] * MOE_TM + rank
    t_max = n_tok // MOE_TM + N_CLASSES
    n_used = ends[N_CLASSES - 1]
    tile = jnp.minimum(jnp.arange(t_max, dtype=jnp.int32), n_used - 1)
    tile_cls = jnp.sum((tile[:, None] >= ends[None, :]).astype(jnp.int32), axis=1)
    group = tile_cls // 6
    pair = tile_cls % 6
    first = jnp.array([0, 0, 0, 1, 1, 2], jnp.int32)[pair] + EXP_PER_GROUP * group
    second = jnp.array([1, 2, 3, 2, 3, 3], jnp.int32)[pair] + EXP_PER_GROUP * group
    return slot, tile, first, second, n_used.reshape(1)


def _split_row(row):
    return lax.shift_right_logical(row, 3), lax.bitwise_and(row, SUBLANES - 1)


def _dispatch_kernel(slot_ref, hx_ref, xs_init_ref, xs_ref, sem):
    del xs_init_ref
    n_oct = hx_ref.shape[1]
    base = (pl.program_id(0) * pl.num_programs(1) + pl.program_id(1)) * (n_oct * SUBLANES)

    def body(i, carry):
        for k in range(SUBLANES):
            oct_id, sub = _split_row(slot_ref[base + i * SUBLANES + k])
            pltpu.make_async_copy(hx_ref.at[0, i, pl.ds(k, 1), :], xs_ref.at[oct_id, pl.ds(sub, 1), :],
                                  sem).start(priority=k % 2)
        return carry

    lax.fori_loop(0, n_oct, body, 0)
    pltpu.make_async_copy(hx_ref.at[0], xs_ref.at[pl.ds(0, n_oct)], sem).wait()


def _dispatch_call(slot, hx, xs_init, tm):
    b, s, w = hx.shape
    n_rows = xs_init.shape[0] * SUBLANES
    grid_spec = pltpu.PrefetchScalarGridSpec(
        num_scalar_prefetch=1, grid=(b, s // tm),
        in_specs=[pl.BlockSpec((1, tm // SUBLANES, SUBLANES, w), lambda i, j, slot: (i, j, 0, 0)),
                  pl.BlockSpec(memory_space=pl.ANY)],
        out_specs=pl.BlockSpec(memory_space=pl.ANY),
        scratch_shapes=[pltpu.SemaphoreType.DMA(())])
    xs = pl.pallas_call(
        _dispatch_kernel, grid_spec=grid_spec,
        out_shape=jax.ShapeDtypeStruct(xs_init.shape, F32),
        input_output_aliases={2: 0},
        compiler_params=_cparams(("arbitrary", "arbitrary")),
        name="moe_dispatch",
    )(slot, hx.reshape(b, s // SUBLANES, SUBLANES, w), xs_init)
    return xs.reshape(n_rows, w)


def _moe_kernel(tile_ref, first_ref, second_ref, nused_ref, xs_ref, wga_ref, wgb_ref, wua_ref, wub_ref,
                wda_ref, wdb_ref, ys_ref):
    del tile_ref, first_ref, second_ref
    t = pl.program_id(0)
    n_used = nused_ref[0]

    @pl.when(t < n_used)
    def _():
        x = xs_ref[...]
        xb = x[:, 0:D_MODEL].astype(BF16)

        def expert(wg_ref, wu_ref, wd_ref, w):
            gate = _dot(xb, wg_ref[0])
            up = _dot(xb, wu_ref[0])
            act = gate * _sigmoid(gate) * up * w
            return _dot(act.astype(BF16), wd_ref[0])

        wa = x[:, D_MODEL + META_WA:D_MODEL + META_WA + 1]
        wb = x[:, D_MODEL + META_WB:D_MODEL + META_WB + 1]
        ys_ref[...] = expert(wga_ref, wua_ref, wda_ref, wa) + expert(wgb_ref, wub_ref, wdb_ref, wb)

    @pl.when(t >= n_used)
    def _():
        ys_ref[...] = jnp.zeros_like(ys_ref)


def _moe_call(tile, first, second, n_used, xs, wg_bf, wu_bf, wd_bf):
    n_rows, w = xs.shape
    d = D_MODEL
    t_max = tile.shape[0]
    up_spec = lambda sel: pl.BlockSpec((1, d, D_EXPERT), lambda t, tl, fi, se, nu: ((fi, se)[sel][t], 0, 0))
    down_spec = lambda sel: pl.BlockSpec((1, D_EXPERT, d), lambda t, tl, fi, se, nu: ((fi, se)[sel][t], 0, 0))
    grid_spec = pltpu.PrefetchScalarGridSpec(
        num_scalar_prefetch=4, grid=(t_max,),
        in_specs=[pl.BlockSpec((MOE_TM, w), lambda t, tl, fi, se, nu: (tl[t], 0)),
                  up_spec(0), up_spec(1), up_spec(0), up_spec(1), down_spec(0), down_spec(1)],
        out_specs=pl.BlockSpec((MOE_TM, d), lambda t, tl, fi, se, nu: (t, 0)))
    return pl.pallas_call(
        _moe_kernel, grid_spec=grid_spec,
        out_shape=jax.ShapeDtypeStruct((n_rows, d), F32),
        compiler_params=_cparams(("arbitrary",)),
        name="moe_experts",
    )(tile, first, second, n_used, xs, wg_bf, wg_bf, wu_bf, wu_bf, wd_bf, wd_bf)


def _combine_kernel(slot_ref, x1_ref, mod_ref, lng_ref, lnb_ref, ys_ref, o_ref, f_ref, sem):
    n_oct = f_ref.shape[1]
    tm = n_oct * SUBLANES
    step = pl.program_id(0) * pl.num_programs(1) + pl.program_id(1)
    n_steps = pl.num_programs(0) * pl.num_programs(1)

    def request(which, buf):
        base = which * tm

        def body(i, carry):
            for k in range(SUBLANES):
                oct_id, sub = _split_row(slot_ref[base + i * SUBLANES + k])
                pltpu.make_async_copy(ys_ref.at[oct_id, pl.ds(sub, 1), :], f_ref.at[buf, i, pl.ds(k, 1), :],
                                      sem.at[buf]).start(priority=k % 2)
            return carry

        lax.fori_loop(0, n_oct, body, 0)

    @pl.when(step == 0)
    def _():
        request(0, 0)

    @pl.when(step + 1 < n_steps)
    def _():
        request(step + 1, (step + 1) % 2)

    buf = step % 2
    pltpu.make_async_copy(ys_ref.at[pl.ds(0, n_oct)], f_ref.at[buf], sem.at[buf]).wait()
    f = f_ref[buf].reshape(tm, f_ref.shape[3])
    g2 = mod_ref[0, 5:6, :]
    o_ref[0] = _layer_norm(DN_ALPHA * x1_ref[0] + g2 * f, lng_ref[...], lnb_ref[...])


def _combine_call(slot, x1, mod, ln_g, ln_b, ys, tm):
    b, s, d = x1.shape
    ys = ys.reshape(ys.shape[0] // SUBLANES, SUBLANES, d)
    grid_spec = pltpu.PrefetchScalarGridSpec(
        num_scalar_prefetch=1, grid=(b, s // tm),
        in_specs=[pl.BlockSpec((1, tm, d), lambda i, j, slot: (i, j, 0)),
                  pl.BlockSpec((1, 6, d), lambda i, j, slot: (i, 0, 0)),
                  pl.BlockSpec((1, d), lambda i, j, slot: (0, 0)),
                  pl.BlockSpec((1, d), lambda i, j, slot: (0, 0)),
                  pl.BlockSpec(memory_space=pl.ANY)],
        out_specs=pl.BlockSpec((1, tm, d), lambda i, j, slot: (i, j, 0)),
        scratch_shapes=[pltpu.VMEM((2, tm // SUBLANES, SUBLANES, d), F32), pltpu.SemaphoreType.DMA((2,))])
    return pl.pallas_call(
        _combine_kernel, grid_spec=grid_spec,
        out_shape=jax.ShapeDtypeStruct((b, s, d), F32),
        compiler_params=_cparams(("arbitrary", "arbitrary")),
        name="moe_combine",
    )(slot, x1, mod, ln_g.reshape(1, -1), ln_b.reshape(1, -1), ys)


def _moe_sublayer(parts, counts, xs_init, wg_bf, wu_bf, wd_bf, ln_g, ln_b):
    sizes = [p[2].shape[0] * p[2].shape[1] for p in parts]
    meta = jnp.concatenate([p[1] for p in parts], axis=1)
    slot, tile, first, second, n_used = _moe_plan(meta, counts, sum(sizes))
    slots, start = [], 0
    for n in sizes:
        slots.append(slot[start:start + n])
        start += n
    xs = xs_init
    for (hx, _, _, _, tm), sl in zip(parts, slots):
        xs = _dispatch_call(sl, hx, xs.reshape(xs_init.shape), tm)
    ys = _moe_call(tile, first, second, n_used, xs, wg_bf, wu_bf, wd_bf)
    return [_combine_call(sl, x1, mod, ln_g, ln_b, ys, tm) for (_, _, x1, mod, tm), sl in zip(parts, slots)]


def kernel(x, c, ctx, c_ctx, w_mod, b_mod, w_in, s5_a_re, s5_a_im, s5_log_dt, s5_b_re, s5_b_im, s5_c_re, s5_c_im, s5_d, w_glu, b_glu, w_sc, w_dw, b_dw, ln_cf_g, ln_cf_b, w_o, ln1_g, ln1_b, w_rg, b_rg, w_rexp, b_rexp, w_gate, w_up, w_down, ln2_g, ln2_b):
    nb, seq, d = x.shape
    n_ctx = ctx.shape[1]
    n_layers = w_mod.shape[0]
    assert seq % TILE_TOK == 0 and n_ctx % TILE_TOK == 0 and seq % GRID_W == 0

    mod_rows = 16
    assert nb + 1 <= mod_rows
    c_all = jnp.concatenate([c, c_ctx[None, :], jnp.zeros((mod_rows - nb - 1, d), F32)], axis=0)
    mod_all = _mod_call(c_all, w_mod, b_mod)

    pad_r = ROUTER_LANES - N_GROUPS - N_EXPERTS
    x_lat, x_ctx = x, ctx
    for l in range(n_layers):
        last = l == n_layers - 1
        mod_lat = mod_all[l, :nb].reshape(nb, 6, d)
        mod_ctx = jnp.broadcast_to(mod_all[l, nb].reshape(1, 6, d), (nb, 6, d))
        w_in_bf = w_in[l].astype(BF16)
        wglu_bf = w_glu[l].astype(BF16)
        wo_bf = w_o[l].astype(BF16)
        wg_bf = w_gate[l].astype(BF16)
        wu_bf = w_up[l].astype(BF16)
        wd_bf = w_down[l].astype(BF16)
        w_router = jnp.concatenate([w_rg[l], w_rexp[l], jnp.zeros((d, pad_r), F32)], axis=1).astype(BF16)
        b_router = jnp.concatenate([b_rg[l], b_rexp[l], jnp.zeros((pad_r,), F32)]).reshape(1, -1)
        w1, e_op, decay = _s5_operators(s5_a_re[l], s5_a_im[l], s5_log_dt[l], s5_b_re[l], s5_b_im[l],
                                        s5_c_re[l], s5_c_im[l])

        u_l, bg_l, cv_l, gl_l = _in_call(x_lat, mod_lat, w_in_bf, 512, False)
        if last:
            u_c = _in_call(x_ctx, mod_ctx, w_in_bf[:, :D_S5], TILE_TOK, True)
        else:
            u_c, bg_c, cv_c, gl_c = _in_call(x_ctx, mod_ctx, w_in_bf, TILE_TOK, False)

        yin_l, gf_l, gb_l = _s5a_call(u_l, w1)
        yin_c, gf_c, gb_c = _s5a_call(u_c, w1)
        hf_c, hf_l, hb_c, hb_l = _s5b_call(decay, gf_c, gf_l, gb_c, gb_l, nb)
        ypre_l = _s5c_call(yin_l, hf_l, hb_l, e_op, u_l, s5_d[l])

        ysc_l, ycf_l = _conv_call(bg_l, cv_l, gl_l, w_sc[l], w_dw[l], b_dw[l], ln_cf_g[l], ln_cf_b[l], True)
        parts = []
        counts = jnp.zeros((1, ROUTER_LANES), F32)
        n_moe = nb * seq
        if not last:
            ypre_c = _s5c_call(yin_c, hf_c, hb_c, e_op, u_c, s5_d[l])
            ysc_c, ycf_c = _conv_call(bg_c, cv_c, gl_c, w_sc[l], w_dw[l], b_dw[l], ln_cf_g[l], ln_cf_b[l], False)
            x1_c, hx_c, meta_c, counts = _out_call(ypre_c, ysc_c, ycf_c, x_ctx, mod_ctx, wglu_bf, b_glu[l], wo_bf,
                                                   ln1_g[l], ln1_b[l], w_router, b_router, counts, TILE_TOK, 0)
            parts.append((hx_c, meta_c, x1_c, mod_ctx, TILE_TOK))
            n_moe += nb * n_ctx
        x1_l, hx_l, meta_l, counts, stage = _out_call(ypre_l, ysc_l, ycf_l, x_lat, mod_lat, wglu_bf, b_glu[l], wo_bf,
                                                      ln1_g[l], ln1_b[l], w_router, b_router, counts, 512,
                                                      _sorted_rows(n_moe))
        parts.append((hx_l, meta_l, x1_l, mod_lat, 512))
        outs = _moe_sublayer(parts, counts, stage, wg_bf, wu_bf, wd_bf, ln2_g[l], ln2_b[l])
        x_lat = outs[-1]
        if not last:
            x_ctx = outs[0]
    return x_lat
```

```python
import functools
import math

import jax
import jax.numpy as jnp
from jax import lax
from jax.experimental import pallas as pl
from jax.experimental.pallas import tpu as pltpu

F32 = jnp.float32
BF16 = jnp.bfloat16

D_MODEL = 1024
DEPTH = 2
GRID_W = 64
D_S5 = 512
S5_GROUP = 16
S5_GROUPS = 32
S5_STATE = 64
D_SC = 256
D_CF = 256
CF_WIDTH = 31
CF_HALF = 15
D_IN = 1792
N_GROUPS = 4
EXP_PER_GROUP = 4
N_EXPERTS = 16
D_EXPERT = 256
DN_ALPHA = (2 * DEPTH) ** 0.25
LN_EPS = 1e-5

CHUNK = 16
N_POW = 32
PREP_GROUPS = 8
TILE_CHUNKS = 16
TILE_TOK = CHUNK * TILE_CHUNKS
LANE_BLK = 128
GRP_PER_BLK = LANE_BLK // S5_GROUP
ROUTER_LANES = 128
HX_LANES = D_MODEL + ROUTER_LANES
META_WA, META_WB, META_CLS, META_RANK = 0, 1, 2, 3
N_CLASSES = N_GROUPS * 6
MOE_TM = 512
OUT_SUBTILES = 2
SUBLANES = 8
VMEM_LIMIT = 56 * 1024 * 1024


def _cparams(sem):
    return pltpu.CompilerParams(dimension_semantics=sem, vmem_limit_bytes=VMEM_LIMIT)


def _split_bf16(a):
    hi = a.astype(BF16)
    lo = (a - hi.astype(F32)).astype(BF16)
    return hi, lo


def _dot(a, b):
    return jnp.dot(a, b, preferred_element_type=F32)


def _dot3(a, b):
    ah, al = _split_bf16(a)
    bh, bl = _split_bf16(b)
    return _dot(ah, bh) + (_dot(al, bh) + _dot(ah, bl))


def _sigmoid(x):
    return 1.0 / (1.0 + jnp.exp(-x))


def _layer_norm(x, g, b):
    mu = jnp.mean(x, axis=-1, keepdims=True)
    xc = x - mu
    var = jnp.mean(xc * xc, axis=-1, keepdims=True)
    return xc * lax.rsqrt(var + LN_EPS) * g + b


def _mod_kernel(c_ref, w_ref, b_ref, o_ref):
    c = c_ref[...]
    s = c * _sigmoid(c)
    o_ref[0] = _dot3(s, w_ref[0]) + b_ref[0]


def _mod_call(c_all, w_mod, b_mod):
    n_layers, d, n_out = w_mod.shape
    tn = 1536
    rows = c_all.shape[0]
    return pl.pallas_call(
        _mod_kernel,
        grid=(n_layers, n_out // tn),
        in_specs=[
            pl.BlockSpec((rows, d), lambda l, j: (0, 0)),
            pl.BlockSpec((1, d, tn), lambda l, j: (l, 0, j)),
            pl.BlockSpec((1, 1, tn), lambda l, j: (l, 0, j)),
        ],
        out_specs=pl.BlockSpec((1, rows, tn), lambda l, j: (l, 0, j)),
        out_shape=jax.ShapeDtypeStruct((n_layers, rows, n_out), F32),
        compiler_params=_cparams(("parallel", "parallel")),
        name="mod",
    )(c_all, w_mod, b_mod.reshape(n_layers, 1, n_out))


def _in_kernel(x_ref, mod_ref, w_ref, u_ref, bg_ref, cv_ref, gl_ref):
    x = x_ref[0]
    sh = mod_ref[0, 0:1, :]
    sc = mod_ref[0, 1:2, :]
    h = (x * (1.0 + sc) + sh).astype(BF16)
    z = _dot(h, w_ref[...])
    _store_lane_blocks(u_ref, z[:, 0:512])
    bg_ref[0] = z[:, 512:768]
    cv_ref[0] = z[:, 768:1024] * z[:, 1024:1280]
    gl_ref[0] = z[:, 1280:1536] * _sigmoid(z[:, 1536:1792])


def _in_u_kernel(x_ref, mod_ref, w_ref, u_ref):
    x = x_ref[0]
    sh = mod_ref[0, 0:1, :]
    sc = mod_ref[0, 1:2, :]
    h = (x * (1.0 + sc) + sh).astype(BF16)
    _store_lane_blocks(u_ref, _dot(h, w_ref[...]))


def _store_lane_blocks(ref, val):
    for blk in range(ref.shape[0]):
        ref[blk, 0] = val[:, blk * LANE_BLK:(blk + 1) * LANE_BLK]


def _lane_block_spec(tm):
    return pl.BlockSpec((D_S5 // LANE_BLK, 1, tm, LANE_BLK), lambda i, j: (0, i, j, 0))


def _lane_block_shape(b, s):
    return jax.ShapeDtypeStruct((D_S5 // LANE_BLK, b, s, LANE_BLK), F32)


def _in_call(x, mod, w_in_bf, tm, u_only):
    b, s, d = x.shape
    grid = (b, s // tm)
    row_spec = lambda n: pl.BlockSpec((1, tm, n), lambda i, j: (i, j, 0))
    in_specs = [
        row_spec(d),
        pl.BlockSpec((1, 6, d), lambda i, j: (i, 0, 0)),
    ]
    if u_only:
        in_specs.append(pl.BlockSpec((d, D_S5), lambda i, j: (0, 0)))
        return pl.pallas_call(
            _in_u_kernel, grid=grid, in_specs=in_specs,
            out_specs=_lane_block_spec(tm),
            out_shape=_lane_block_shape(b, s),
            compiler_params=_cparams(("parallel", "parallel")),
            name="in_proj_u",
        )(x, mod, w_in_bf)
    in_specs.append(pl.BlockSpec((d, D_IN), lambda i, j: (0, 0)))
    return pl.pallas_call(
        _in_kernel, grid=grid, in_specs=in_specs,
        out_specs=[_lane_block_spec(tm), row_spec(D_SC), row_spec(D_SC), row_spec(D_CF)],
        out_shape=[_lane_block_shape(b, s),
                   jax.ShapeDtypeStruct((b, s, D_SC), F32),
                   jax.ShapeDtypeStruct((b, s, D_SC), F32),
                   jax.ShapeDtypeStruct((b, s, D_CF), F32)],
        compiler_params=_cparams(("parallel", "parallel")),
        name="in_proj",
    )(x, mod, w_in_bf)


def _conv_tail(t, bdw_ref, lng_ref, lnb_ref):
    t = t + bdw_ref[...]
    t = _layer_norm(t, lng_ref[...], lnb_ref[...])
    return t * _sigmoid(t)


def _conv_grid_kernel(bg_ref, cv_ref, gl_ref, wsc_ref, wdw_ref, bdw_ref, lng_ref, lnb_ref,
                      ysc_ref, ycf_ref, pad_ref, t_ref):
    s = cv_ref.shape[1]
    rows = s // GRID_W
    cv = cv_ref[0]
    col = lax.broadcasted_iota(jnp.int32, (s, D_SC), 0) % GRID_W
    prev = jnp.where(col == 0, 0.0, pltpu.roll(cv, 1, axis=0))
    nxt = jnp.where(col == GRID_W - 1, 0.0, pltpu.roll(cv, s - 1, axis=0))
    conv = prev * wsc_ref[0:1, :] + cv * wsc_ref[1:2, :] + nxt * wsc_ref[2:3, :]
    ysc_ref[0] = (bg_ref[0] * conv).astype(ysc_ref.dtype)

    zero = jnp.zeros((CF_HALF, GRID_W, D_CF), F32)
    pad_ref[0:CF_HALF] = zero
    pad_ref[CF_HALF + rows:CF_HALF + rows + CF_HALF] = zero
    pad_ref[CF_HALF:CF_HALF + rows] = gl_ref[0].reshape(rows, GRID_W, D_CF)

    def body(i, carry):
        w0 = pl.multiple_of(i * 8, 8)
        for half in range(D_CF // 128):
            lanes = slice(half * 128, (half + 1) * 128)
            acc = jnp.zeros((rows, 8, 128), F32)
            for k in range(CF_WIDTH):
                acc = acc + pad_ref[k:k + rows, pl.ds(w0, 8), lanes] * wdw_ref[k:k + 1, lanes]
            t_ref[:, pl.ds(w0, 8), lanes] = acc
        return carry

    lax.fori_loop(0, GRID_W // 8, body, 0)
    t = t_ref[...].reshape(s, D_CF)
    ycf_ref[0] = _conv_tail(t, bdw_ref, lng_ref, lnb_ref).astype(ycf_ref.dtype)


def _conv_seq_kernel(bg_ref, cv_ref, gl_ref, wsc_ref, wdw_ref, bdw_ref, lng_ref, lnb_ref,
                     ysc_ref, ycf_ref, pad_ref):
    s = cv_ref.shape[1]
    cv = cv_ref[0]
    pos = lax.broadcasted_iota(jnp.int32, (s, D_SC), 0)
    prev = jnp.where(pos == 0, 0.0, pltpu.roll(cv, 1, axis=0))
    nxt = jnp.where(pos == s - 1, 0.0, pltpu.roll(cv, s - 1, axis=0))
    conv = prev * wsc_ref[0:1, :] + cv * wsc_ref[1:2, :] + nxt * wsc_ref[2:3, :]
    ysc_ref[0] = (bg_ref[0] * conv).astype(ysc_ref.dtype)

    off = 16
    pad_ref[0:off] = jnp.zeros((off, D_CF), F32)
    pad_ref[off + s:off + s + 16] = jnp.zeros((16, D_CF), F32)
    pad_ref[off:off + s] = gl_ref[0]
    acc = jnp.zeros((s, D_CF), F32)
    for k in range(CF_WIDTH):
        acc = acc + pad_ref[pl.ds(off - CF_HALF + k, s), :] * wdw_ref[k:k + 1, :]
    ycf_ref[0] = _conv_tail(acc, bdw_ref, lng_ref, lnb_ref).astype(ycf_ref.dtype)


def _conv_call(bg, cv, gl, w_sc, w_dw, b_dw, ln_g, ln_b, grid_mode):
    b, s, _ = bg.shape
    row_spec = pl.BlockSpec((1, s, D_SC), lambda i: (i, 0, 0))
    full = lambda shape: pl.BlockSpec(shape, lambda i: (0,) * len(shape))
    if grid_mode:
        rows = s // GRID_W
        kern = _conv_grid_kernel
        scratch = [pltpu.VMEM((rows + 2 * CF_HALF, GRID_W, D_CF), F32),
                   pltpu.VMEM((rows, GRID_W, D_CF), F32)]
        name = "conv_grid"
    else:
        kern = _conv_seq_kernel
        scratch = [pltpu.VMEM((s + 32, D_CF), F32)]
        name = "conv_seq"
    return pl.pallas_call(
        kern, grid=(b,),
        in_specs=[row_spec, row_spec, row_spec, full((3, D_SC)), full((CF_WIDTH, D_CF)),
                  full((1, D_CF)), full((1, D_CF)), full((1, D_CF))],
        out_specs=[row_spec, row_spec],
        out_shape=[jax.ShapeDtypeStruct((b, s, D_SC), BF16), jax.ShapeDtypeStruct((b, s, D_CF), BF16)],
        scratch_shapes=scratch,
        compiler_params=_cparams(("parallel",)),
        name=name,
    )(bg, cv, gl, w_sc, w_dw, b_dw.reshape(1, -1), ln_g.reshape(1, -1), ln_b.reshape(1, -1))


def _split3(a):
    hi = a.astype(BF16)
    r = a - hi.astype(F32)
    mid = r.astype(BF16)
    lo = (r - mid.astype(F32)).astype(BF16)
    return hi, mid, lo


def _select_cols(a, sel):
    hi, mid, lo = _split3(a)
    return _dot(hi, sel) + (_dot(mid, sel) + _dot(lo, sel))


def _select_rows(sel, a):
    hi, mid, lo = _split3(a)
    return _dot(sel, hi) + (_dot(sel, mid) + _dot(sel, lo))


def _cmul(ar, ai, br, bi):
    return ar * br - ai * bi, ar * bi + ai * br


def _s5_pow_kernel(are_ref, aim_ref, ldt_ref, pr_ref, pi_ref):
    j = jnp.minimum(lax.broadcasted_iota(jnp.int32, pr_ref.shape[1:], 0), CHUNK).astype(F32)
    for d in range(2):
        dt = jnp.exp(ldt_ref[d])
        e = jnp.exp(j * (are_ref[d] * dt))
        pr_ref[d] = e * jnp.cos(j * (aim_ref[d] * dt))
        pi_ref[d] = e * jnp.sin(j * (aim_ref[d] * dt))


def _s5_pow_call(a_re, a_im, log_dt):
    g, p = S5_GROUPS, S5_STATE
    flat = lambda a: a.reshape(2, 1, g * p)
    sds = jax.ShapeDtypeStruct((2, N_POW, g * p), F32)
    pr, pi = pl.pallas_call(_s5_pow_kernel, out_shape=[sds, sds], name="s5_powers")(
        flat(a_re), flat(a_im), flat(jnp.repeat(log_dt, p, axis=1)))
    by_group = lambda a: a.reshape(2, N_POW, g, p).transpose(0, 2, 1, 3)
    return by_group(pr), by_group(pi)


def _s5_prep_kernel(are_r, aim_r, pr_ref, pi_ref, qr_ref, qi_ref, bre_ref, bim_ref, cre_ref, cim_ref,
                    w1_ref, e_ref, dec_ref):
    t, n, p = CHUNK, S5_GROUP, S5_STATE
    width = t * n
    lane_tok = lax.broadcasted_iota(jnp.int32, (N_POW, width), 1) // n
    pow_id = lax.broadcasted_iota(jnp.int32, (N_POW, width), 0)
    onehot = lambda cond: jnp.where(cond, 1.0, 0.0).astype(BF16)
    sel_fwd = onehot(pow_id == lane_tok)
    sel_rev = onehot(pow_id == t - 1 - lane_tok)
    sel_out = onehot(pow_id == t - lane_tok)
    row_tok = lax.broadcasted_iota(jnp.int32, (width, N_POW), 0) // n
    row_pow = lax.broadcasted_iota(jnp.int32, (width, N_POW), 1)
    rsel_rev = onehot(row_pow == t - 1 - row_tok)
    rsel_fwd = onehot(row_pow == row_tok)
    lane = lax.broadcasted_iota(jnp.int32, (n, width), 1)
    for gi in range(w1_ref.shape[0]):
        _s5_prep_group(gi, are_r, aim_r, pr_ref, pi_ref, qr_ref, qi_ref, bre_ref, bim_ref, cre_ref, cim_ref,
                       w1_ref, e_ref, dec_ref, (sel_fwd, sel_rev, sel_out, rsel_rev, rsel_fwd, lane))


def _s5_prep_group(gi, are_r, aim_r, pr_ref, pi_ref, qr_ref, qi_ref, bre_ref, bim_ref, cre_ref, cim_ref,
                   w1_ref, e_ref, dec_ref, selectors):
    sel_fwd, sel_rev, sel_out, rsel_rev, rsel_fwd, lane = selectors
    t, n = CHUNK, S5_GROUP
    width = t * n
    strips = []
    f_parts = []
    e_parts = []
    for d in range(2):
        qr = qr_ref[d, gi]
        qi = qi_ref[d, gi]
        pr = pr_ref[d, gi]
        pi = pi_ref[d, gi]
        a_re = are_r[d, gi]
        a_im = aim_r[d, gi]
        nr = pr[1:2] - 1.0
        ni = pi[1:2]
        den = a_re * a_re + a_im * a_im
        fre = (nr * a_re + ni * a_im) / den
        fim = (ni * a_re - nr * a_im) / den
        bt_re = bre_ref[d, gi].T
        bt_im = bim_ref[d, gi].T
        bb_re, bb_im = _cmul(fre, fim, bt_re, bt_im)
        ct_re = jnp.concatenate([cre_ref[d, gi].T] * t, axis=1)
        ct_im = jnp.concatenate([cim_ref[d, gi].T] * t, axis=1)
        sel = sel_fwd if d == 0 else sel_rev
        w_re, w_im = _cmul(ct_re, ct_im, _select_cols(qr, sel), _select_cols(qi, sel))
        strips.append(_dot3(bb_re, w_re) - _dot3(bb_im, w_im))
        if d == 0:
            o_re, o_im = _cmul(w_re, w_im, qr[:, 1:2], qi[:, 1:2])
        else:
            o_re, o_im = _cmul(ct_re, ct_im, _select_cols(qr, sel_out), _select_cols(qi, sel_out))
        e_parts += [o_re, -o_im]
        rsel = rsel_rev if d == 0 else rsel_fwd
        f_re, f_im = _cmul(jnp.concatenate([bb_re] * t, axis=0), jnp.concatenate([bb_im] * t, axis=0),
                           _select_rows(rsel, pr), _select_rows(rsel, pi))
        f_parts += [f_re, f_im]
        dec_ref[gi, 2 * d:2 * d + 1, :] = jnp.concatenate([pr[t:t + 1], pr[t:t + 1]], axis=1)
        dec_ref[gi, 2 * d + 1:2 * d + 2, :] = jnp.concatenate([-pi[t:t + 1], pi[t:t + 1]], axis=1)

    blocks = []
    for s in range(t):
        fwd = strips[0] if s == 0 else jnp.where(lane >= n * s, pltpu.roll(strips[0], n * s, axis=1), 0.0)
        back = t - 1 - s
        bwd = strips[1] if back == 0 else jnp.where(lane < width - n * back,
                                                     pltpu.roll(strips[1], width - n * back, axis=1), 0.0)
        blocks.append(fwd + bwd)
    m = jnp.concatenate(blocks, axis=0)
    w1_ref[gi] = jnp.concatenate([m] + f_parts, axis=1).astype(BF16)
    e_ref[gi] = jnp.concatenate(e_parts, axis=0).astype(BF16)


def _s5_operators(a_re, a_im, log_dt, b_re, b_im, c_re, c_im):
    g, p, n, t = S5_GROUPS, S5_STATE, S5_GROUP, CHUNK
    pr, pi = _s5_pow_call(a_re, a_im, log_dt)
    gb = PREP_GROUPS
    spec = lambda shape: pl.BlockSpec((2, gb) + shape, lambda i: (0, i) + (0,) * len(shape))
    return pl.pallas_call(
        _s5_prep_kernel, grid=(g // gb,),
        in_specs=[spec((1, p)), spec((1, p)), spec((N_POW, p)), spec((N_POW, p)), spec((p, N_POW)),
                  spec((p, N_POW)), spec((p, n)), spec((p, n)), spec((n, p)), spec((n, p))],
        out_specs=[pl.BlockSpec((gb, t * n, 2 * t * n), lambda i: (i, 0, 0)),
                   pl.BlockSpec((gb, 4 * p, t * n), lambda i: (i, 0, 0)),
                   pl.BlockSpec((gb, 4, 2 * p), lambda i: (i, 0, 0))],
        out_shape=[jax.ShapeDtypeStruct((g, t * n, 2 * t * n), BF16),
                   jax.ShapeDtypeStruct((g, 4 * p, t * n), BF16),
                   jax.ShapeDtypeStruct((g, 4, 2 * p), F32)],
        compiler_params=_cparams(("parallel",)),
        name="s5_prep",
    )(a_re.reshape(2, g, 1, p), a_im.reshape(2, g, 1, p), pr, pi, pr.transpose(0, 1, 3, 2),
      pi.transpose(0, 1, 3, 2), b_re, b_im, c_re, c_im)


def _block_transpose8(ps):
    ps = list(ps)
    blk = lax.broadcasted_iota(jnp.int32, ps[0].shape, 1) // S5_GROUP
    for k in range(3):
        step = 1 << k
        shift = S5_GROUP * step
        keep = ((blk >> k) & 1) == 0
        for a in range(8):
            if a & step:
                continue
            pa, pb = ps[a], ps[a + step]
            ps[a] = jnp.where(keep, pa, pltpu.roll(pb, shift, axis=1))
            ps[a + step] = jnp.where(keep, pltpu.roll(pa, 128 - shift, axis=1), pb)
    return ps


def _s5a_kernel(u_ref, w1_ref, yin_ref, gf_ref, gb_ref):
    nb = u_ref.shape[1]
    xs = []
    for s in range(CHUNK):
        parts = [u_ref[0, b, pl.ds(s, TILE_CHUNKS, stride=CHUNK), :] for b in range(nb)]
        xs.append(jnp.concatenate(parts, axis=0))
    lo = _block_transpose8(xs[:8])
    hi = _block_transpose8(xs[8:])
    for j in range(GRP_PER_BLK):
        og = jnp.concatenate([lo[j], hi[j]], axis=1)
        r = _dot(og.astype(BF16), w1_ref[j])
        yin_ref[j] = r[:, 0:256]
        gf_ref[j] = r[:, 256:384]
        gb_ref[j] = r[:, 384:512]


def _s5a_call(u, w1):
    n_blk, b, s, _ = u.shape
    nt = s // TILE_TOK
    rows = b * TILE_CHUNKS
    out_spec = lambda n: pl.BlockSpec((GRP_PER_BLK, rows, n), lambda l, j: (l, j, 0))
    return pl.pallas_call(
        _s5a_kernel, grid=(n_blk, nt),
        in_specs=[pl.BlockSpec((1, b, TILE_TOK, LANE_BLK), lambda l, j: (l, 0, j, 0)),
                  pl.BlockSpec((GRP_PER_BLK, 256, 512), lambda l, j: (l, 0, 0))],
        out_specs=[out_spec(256), out_spec(128), out_spec(128)],
        out_shape=[jax.ShapeDtypeStruct((S5_GROUPS, nt * rows, 256), F32),
                   jax.ShapeDtypeStruct((S5_GROUPS, nt * rows, 128), F32),
                   jax.ShapeDtypeStruct((S5_GROUPS, nt * rows, 128), F32)],
        compiler_params=_cparams(("parallel", "parallel")),
        name="s5_chunk_in",
    )(u, w1)


def _s5b_kernel(nb, a_ref, gfc_ref, gfl_ref, gbc_ref, gbl_ref, hfc_ref, hfl_ref, hbc_ref, hbl_ref):
    gb = a_ref.shape[0]
    rows = nb * TILE_CHUNKS
    n_lat = gfl_ref.shape[1] // rows
    a1f = [jnp.broadcast_to(a_ref[g, 0:1, :], (nb, 128)) for g in range(gb)]
    a2f = [jnp.broadcast_to(a_ref[g, 1:2, :], (nb, 128)) for g in range(gb)]
    a1b = [jnp.broadcast_to(a_ref[g, 2:3, :], (nb, 128)) for g in range(gb)]
    a2b = [jnp.broadcast_to(a_ref[g, 3:4, :], (nb, 128)) for g in range(gb)]

    def step(state, a1, a2, g_ref, h_ref, g, row):
        h, hs = state
        h_ref[g, row, :] = h
        inp = g_ref[g, row, :]
        return a1 * h + a2 * hs + inp, a1 * hs - a2 * h + pltpu.roll(inp, 64, axis=1)

    zero = jnp.zeros((nb, 128), F32)
    hf = [(zero, zero) for _ in range(gb)]
    hb = [(zero, zero) for _ in range(gb)]
    for ci in range(TILE_CHUNKS):
        rf = pl.ds(ci, nb, stride=TILE_CHUNKS)
        rb = pl.ds(TILE_CHUNKS - 1 - ci, nb, stride=TILE_CHUNKS)
        for g in range(gb):
            hf[g] = step(hf[g], a1f[g], a2f[g], gfc_ref, hfc_ref, g, rf)
            hb[g] = step(hb[g], a1b[g], a2b[g], gbc_ref, hbc_ref, g, rb)

    def body(j, carry):
        hf, hb = carry
        hf = list(hf)
        hb = list(hb)
        base_f = j * rows
        base_b = (n_lat - 1 - j) * rows
        for ci in range(TILE_CHUNKS):
            rf = pl.ds(base_f + ci, nb, stride=TILE_CHUNKS)
            rb = pl.ds(base_b + (TILE_CHUNKS - 1 - ci), nb, stride=TILE_CHUNKS)
            for g in range(gb):
                hf[g] = step(hf[g], a1f[g], a2f[g], gfl_ref, hfl_ref, g, rf)
                hb[g] = step(hb[g], a1b[g], a2b[g], gbl_ref, hbl_ref, g, rb)
        return tuple(hf), tuple(hb)

    lax.fori_loop(0, n_lat, body, (tuple(hf), tuple(hb)))


def _s5b_call(decay, gf_c, gf_l, gb_c, gb_l, nb):
    gblk = 4
    spec = lambda a: pl.BlockSpec((gblk, a.shape[1], 128), lambda i: (i, 0, 0))
    sds = lambda a: jax.ShapeDtypeStruct(a.shape, F32)
    return pl.pallas_call(
        functools.partial(_s5b_kernel, nb), grid=(S5_GROUPS // gblk,),
        in_specs=[pl.BlockSpec((gblk, 4, 128), lambda i: (i, 0, 0)),
                  spec(gf_c), spec(gf_l), spec(gb_c), spec(gb_l)],
        out_specs=[spec(gf_c), spec(gf_l), spec(gb_c), spec(gb_l)],
        out_shape=[sds(gf_c), sds(gf_l), sds(gb_c), sds(gb_l)],
        compiler_params=_cparams(("parallel",)),
        name="s5_state_scan",
    )(decay, gf_c, gf_l, gb_c, gb_l)


def _s5c_kernel(yin_ref, hf_ref, hb_ref, e_ref, u_ref, d_ref, y_ref):
    nb = u_ref.shape[1]
    ys = []
    for j in range(GRP_PER_BLK):
        h = jnp.concatenate([hf_ref[j], hb_ref[j]], axis=1).astype(BF16)
        ys.append(yin_ref[j] + _dot(h, e_ref[j]))
    at = (_block_transpose8([y[:, :128] for y in ys])
          + _block_transpose8([y[:, 128:] for y in ys]))
    d = d_ref[...]
    for t in range(CHUNK):
        for b in range(nb):
            rows = pl.ds(t, TILE_CHUNKS, stride=CHUNK)
            y_ref[0, b, rows, :] = at[t][b * TILE_CHUNKS:(b + 1) * TILE_CHUNKS] + d * u_ref[0, b, rows, :]


def _s5c_call(yin, hf, hb, e, u, d_skip):
    n_blk, b, s, _ = u.shape
    nt = s // TILE_TOK
    rows = b * TILE_CHUNKS
    gspec = lambda n: pl.BlockSpec((GRP_PER_BLK, rows, n), lambda l, j: (l, j, 0))
    tok_spec = pl.BlockSpec((1, b, TILE_TOK, LANE_BLK), lambda l, j: (l, 0, j, 0))
    return pl.pallas_call(
        _s5c_kernel, grid=(n_blk, nt),
        in_specs=[gspec(256), gspec(128), gspec(128),
                  pl.BlockSpec((GRP_PER_BLK, 256, 256), lambda l, j: (l, 0, 0)),
                  tok_spec,
                  pl.BlockSpec((1, LANE_BLK), lambda l, j: (0, l))],
        out_specs=tok_spec,
        out_shape=_lane_block_shape(b, s),
        compiler_params=_cparams(("parallel", "parallel")),
        name="s5_chunk_out",
    )(yin, hf, hb, e, u, d_skip.reshape(1, D_S5))


def _gelu_tanh(x):
    return 0.5 * x * (1.0 + jnp.tanh(math.sqrt(2.0 / math.pi) * (x + 0.044715 * (x * x * x))))


def _route(logits):
    lane = lax.broadcasted_iota(jnp.int32, logits.shape, 1).astype(F32)
    neg = jnp.float32(-1e30)
    big = jnp.float32(1e9)
    gl = jnp.where(lane < N_GROUPS, logits, neg)
    gmax = jnp.max(gl, axis=1, keepdims=True)
    gidx = jnp.min(jnp.where(gl == gmax, lane, big), axis=1, keepdims=True)
    gsum = jnp.sum(jnp.exp(gl - gmax), axis=1, keepdims=True)
    gw = 1.0 / gsum
    lo = N_GROUPS + EXP_PER_GROUP * gidx
    el = jnp.where((lane >= lo) & (lane < lo + EXP_PER_GROUP), logits, neg)
    v1 = jnp.max(el, axis=1, keepdims=True)
    i1 = jnp.min(jnp.where(el == v1, lane, big), axis=1, keepdims=True)
    el2 = jnp.where(lane == i1, neg, el)
    v2 = jnp.max(el2, axis=1, keepdims=True)
    i2 = jnp.min(jnp.where(el2 == v2, lane, big), axis=1, keepdims=True)
    ex = jnp.exp(v2 - v1)
    p1 = 1.0 / (1.0 + ex)
    p2 = ex * p1
    e1 = i1 - lo
    e2 = i2 - lo
    first = e1 < e2
    ea = jnp.where(first, e1, e2)
    eb = jnp.where(first, e2, e1)
    wa = gw * jnp.where(first, p1, p2)
    wb = gw * jnp.where(first, p2, p1)
    pair = ea * (7.0 - ea) * 0.5 + (eb - ea - 1.0)
    return wa, wb, 6.0 * gidx + pair


def _out_kernel(ypre_ref, ysc_ref, ycf_ref, x_ref, mod_ref, wglu_ref, bglu_ref, wo_ref,
                lng_ref, lnb_ref, wr_ref, br_ref, cnt0_ref, x1_ref, hx_ref, meta_ref, counts_ref, *rest):
    stage_ref = rest[0] if len(rest) == 2 else None
    cnt_ref = rest[-1]

    @pl.when((pl.program_id(0) == 0) & (pl.program_id(1) == 0))
    def _():
        cnt_ref[...] = cnt0_ref[...]

    tm = x_ref.shape[1]
    sub = tm // OUT_SUBTILES

    def row_chain(r0):
        rows = slice(r0, r0 + sub)
        ypre = jnp.concatenate([ypre_ref[blk, 0, rows, :] for blk in range(ypre_ref.shape[0])], axis=1)
        t = _gelu_tanh(ypre)
        gate = _sigmoid(_dot(t.astype(BF16), wglu_ref[...]) + bglu_ref[...])
        ys5 = (t * gate).astype(BF16)
        y = (_dot(ys5, wo_ref[0:D_S5, :]) + _dot(ysc_ref[0, rows, :], wo_ref[D_S5:D_S5 + D_SC, :])
             + _dot(ycf_ref[0, rows, :], wo_ref[D_S5 + D_SC:D_MODEL, :]))
        g1 = mod_ref[0, 2:3, :]
        x1 = _layer_norm(DN_ALPHA * x_ref[0, rows, :] + g1 * y, lng_ref[...], lnb_ref[...])
        x1_ref[0, rows, :] = x1
        h2 = x1 * (1.0 + mod_ref[0, 4:5, :]) + mod_ref[0, 3:4, :]
        hx_ref[0, rows, 0:D_MODEL] = h2
        return _route(_dot(h2.astype(BF16), wr_ref[...]) + br_ref[...])

    routed = [row_chain(r0) for r0 in range(0, tm, sub)]
    wa, wb, cls = (jnp.concatenate([r[i] for r in routed], axis=0) for i in range(3))

    lane = lax.broadcasted_iota(jnp.int32, (tm, ROUTER_LANES), 1).astype(F32)
    onehot = jnp.where(lane == cls, 1.0, 0.0)
    row_i = lax.broadcasted_iota(jnp.int32, (tm, tm), 0)
    col_i = lax.broadcasted_iota(jnp.int32, (tm, tm), 1)
    earlier = jnp.where(col_i < row_i, 1.0, 0.0).astype(BF16)
    before = _dot(earlier, onehot.astype(BF16)) + cnt_ref[...]
    rank = jnp.sum(before * onehot, axis=1, keepdims=True)
    cnt_ref[...] += jnp.sum(onehot, axis=0, keepdims=True)
    counts_ref[...] = cnt_ref[...]

    meta = (jnp.where(lane == META_WA, wa, 0.0) + jnp.where(lane == META_WB, wb, 0.0)
            + jnp.where(lane == META_CLS, cls, 0.0) + jnp.where(lane == META_RANK, rank, 0.0))
    meta_ref[...] = jnp.transpose(meta)[0:SUBLANES, :]
    hx_ref[0, :, D_MODEL:HX_LANES] = meta
    if stage_ref is not None:
        stage_ref[...] = jnp.zeros_like(stage_ref)


def _out_call(ypre, ysc, ycf, x, mod, wglu_bf, b_glu, wo_bf, ln_g, ln_b, w_router, b_router, counts0, tm,
              stage_rows):
    b, s, d = x.shape
    nt = s // tm
    row_spec = lambda n: pl.BlockSpec((1, tm, n), lambda i, j: (i, j, 0))
    full = lambda shape: pl.BlockSpec(shape, lambda i, j: (0,) * len(shape))
    out_specs = [row_spec(d), row_spec(HX_LANES),
                 pl.BlockSpec((SUBLANES, tm), lambda i, j: (0, i * nt + j)),
                 full((1, ROUTER_LANES))]
    out_shape = [jax.ShapeDtypeStruct((b, s, d), F32), jax.ShapeDtypeStruct((b, s, HX_LANES), F32),
                 jax.ShapeDtypeStruct((SUBLANES, b * s), F32),
                 jax.ShapeDtypeStruct((1, ROUTER_LANES), F32)]
    if stage_rows:
        stage_octs = stage_rows // (SUBLANES * b * nt)
        assert stage_octs * SUBLANES * b * nt == stage_rows
        out_specs.append(pl.BlockSpec((stage_octs, SUBLANES, HX_LANES), lambda i, j: (i * nt + j, 0, 0)))
        out_shape.append(jax.ShapeDtypeStruct((stage_rows // SUBLANES, SUBLANES, HX_LANES), F32))
    return pl.pallas_call(
        _out_kernel, grid=(b, nt),
        in_specs=[_lane_block_spec(tm), row_spec(D_SC), row_spec(D_CF), row_spec(d),
                  pl.BlockSpec((1, 6, d), lambda i, j: (i, 0, 0)),
                  full((D_S5, D_S5)), full((1, D_S5)), full((d, d)),
                  full((1, d)), full((1, d)), full((d, ROUTER_LANES)), full((1, ROUTER_LANES)),
                  full((1, ROUTER_LANES))],
        out_specs=out_specs, out_shape=out_shape,
        scratch_shapes=[pltpu.VMEM((1, ROUTER_LANES), F32)],
        compiler_params=_cparams(("arbitrary", "arbitrary")),
        name="out_proj",
    )(ypre, ysc, ycf, x, mod, wglu_bf, b_glu.reshape(1, -1), wo_bf, ln_g.reshape(1, -1),
      ln_b.reshape(1, -1), w_router, b_router, counts0)


def _sorted_rows(n_tok):
    return n_tok + N_CLASSES * MOE_TM


def _moe_plan(meta, counts, n_tok):
    cls = meta[META_CLS].astype(jnp.int32)
    rank = meta[META_RANK].astype(jnp.int32)
    cnt = counts[0, :N_CLASSES].astype(jnp.int32)
    n_tiles = (cnt + (MOE_TM - 1)) // MOE_TM
    ends = jnp.cumsum(n_tiles)
    starts = ends - n_tiles
    slot = starts[cls] * MOE_TM + rank
    t_max = n_tok // MOE_TM + N_CLASSES
    n_used = ends[N_CLASSES - 1]
    tile = jnp.minimum(jnp.arange(t_max, dtype=jnp.int32), n_used - 1)
    tile_cls = jnp.sum((tile[:, None] >= ends[None, :]).astype(jnp.int32), axis=1)
    group = tile_cls // 6
    pair = tile_cls % 6
    first = jnp.array([0, 0, 0, 1, 1, 2], jnp.int32)[pair] + EXP_PER_GROUP * group
    second = jnp.array([1, 2, 3, 2, 3, 3], jnp.int32)[pair] + EXP_PER_GROUP * group
    return slot, tile, first, second, n_used.reshape(1)


def _split_row(row):
    return lax.shift_right_logical(row, 3), lax.bitwise_and(row, SUBLANES - 1)


def _dispatch_kernel(slot_ref, hx_ref, xs_init_ref, xs_ref, sem):
    del xs_init_ref
    n_oct = hx_ref.shape[1]
    base = (pl.program_id(0) * pl.num_programs(1) + pl.program_id(1)) * (n_oct * SUBLANES)

    def body(i, carry):
        for k in range(SUBLANES):
            oct_id, sub = _split_row(slot_ref[base + i * SUBLANES + k])
            pltpu.make_async_copy(hx_ref.at[0, i, pl.ds(k, 1), :], xs_ref.at[oct_id, pl.ds(sub, 1), :],
                                  sem).start(priority=k % 2)
        return carry

    lax.fori_loop(0, n_oct, body, 0)
    pltpu.make_async_copy(hx_ref.at[0], xs_ref.at[pl.ds(0, n_oct)], sem).wait()


def _dispatch_call(slot, hx, xs_init, tm):
    b, s, w = hx.shape
    n_rows = xs_init.shape[0] * SUBLANES
    grid_spec = pltpu.PrefetchScalarGridSpec(
        num_scalar_prefetch=1, grid=(b, s // tm),
        in_specs=[pl.BlockSpec((1, tm // SUBLANES, SUBLANES, w), lambda i, j, slot: (i, j, 0, 0)),
                  pl.BlockSpec(memory_space=pl.ANY)],
        out_specs=pl.BlockSpec(memory_space=pl.ANY),
        scratch_shapes=[pltpu.SemaphoreType.DMA(())])
    xs = pl.pallas_call(
        _dispatch_kernel, grid_spec=grid_spec,
        out_shape=jax.ShapeDtypeStruct(xs_init.shape, F32),
        input_output_aliases={2: 0},
        compiler_params=_cparams(("arbitrary", "arbitrary")),
        name="moe_dispatch",
    )(slot, hx.reshape(b, s // SUBLANES, SUBLANES, w), xs_init)
    return xs.reshape(n_rows, w)


def _moe_kernel(tile_ref, first_ref, second_ref, nused_ref, xs_ref, wga_ref, wgb_ref, wua_ref, wub_ref,
                wda_ref, wdb_ref, ys_ref):
    del tile_ref, first_ref, second_ref
    t = pl.program_id(0)
    n_used = nused_ref[0]

    @pl.when(t < n_used)
    def _():
        x = xs_ref[...]
        xb = x[:, 0:D_MODEL].astype(BF16)

        def expert(wg_ref, wu_ref, wd_ref, w):
            gate = _dot(xb, wg_ref[0])
            up = _dot(xb, wu_ref[0])
            act = gate * _sigmoid(gate) * up * w
            return _dot(act.astype(BF16), wd_ref[0])

        wa = x[:, D_MODEL + META_WA:D_MODEL + META_WA + 1]
        wb = x[:, D_MODEL + META_WB:D_MODEL + META_WB + 1]
        ys_ref[...] = expert(wga_ref, wua_ref, wda_ref, wa) + expert(wgb_ref, wub_ref, wdb_ref, wb)

    @pl.when(t >= n_used)
    def _():
        ys_ref[...] = jnp.zeros_like(ys_ref)


def _moe_call(tile, first, second, n_used, xs, wg_bf, wu_bf, wd_bf):
    n_rows, w = xs.shape
    d = D_MODEL
    t_max = tile.shape[0]
    up_spec = lambda sel: pl.BlockSpec((1, d, D_EXPERT), lambda t, tl, fi, se, nu: ((fi, se)[sel][t], 0, 0))
    down_spec = lambda sel: pl.BlockSpec((1, D_EXPERT, d), lambda t, tl, fi, se, nu: ((fi, se)[sel][t], 0, 0))
    grid_spec = pltpu.PrefetchScalarGridSpec(
        num_scalar_prefetch=4, grid=(t_max,),
        in_specs=[pl.BlockSpec((MOE_TM, w), lambda t, tl, fi, se, nu: (tl[t], 0)),
                  up_spec(0), up_spec(1), up_spec(0), up_spec(1), down_spec(0), down_spec(1)],
        out_specs=pl.BlockSpec((MOE_TM, d), lambda t, tl, fi, se, nu: (t, 0)))
    return pl.pallas_call(
        _moe_kernel, grid_spec=grid_spec,
        out_shape=jax.ShapeDtypeStruct((n_rows, d), F32),
        compiler_params=_cparams(("arbitrary",)),
        name="moe_experts",
    )(tile, first, second, n_used, xs, wg_bf, wg_bf, wu_bf, wu_bf, wd_bf, wd_bf)


def _combine_kernel(slot_ref, x1_ref, mod_ref, lng_ref, lnb_ref, ys_ref, o_ref, f_ref, sem):
    n_oct = f_ref.shape[1]
    tm = n_oct * SUBLANES
    step = pl.program_id(0) * pl.num_programs(1) + pl.program_id(1)
    n_steps = pl.num_programs(0) * pl.num_programs(1)

    def request(which, buf):
        base = which * tm

        def body(i, carry):
            for k in range(SUBLANES):
                oct_id, sub = _split_row(slot_ref[base + i * SUBLANES + k])
                pltpu.make_async_copy(ys_ref.at[oct_id, pl.ds(sub, 1), :], f_ref.at[buf, i, pl.ds(k, 1), :],
                                      sem.at[buf]).start(priority=k % 2)
            return carry

        lax.fori_loop(0, n_oct, body, 0)

    @pl.when(step == 0)
    def _():
        request(0, 0)

    @pl.when(step + 1 < n_steps)
    def _():
        request(step + 1, (step + 1) % 2)

    buf = step % 2
    pltpu.make_async_copy(ys_ref.at[pl.ds(0, n_oct)], f_ref.at[buf], sem.at[buf]).wait()
    f = f_ref[buf].reshape(tm, f_ref.shape[3])
    g2 = mod_ref[0, 5:6, :]
    o_ref[0] = _layer_norm(DN_ALPHA * x1_ref[0] + g2 * f, lng_ref[...], lnb_ref[...])


def _combine_call(slot, x1, mod, ln_g, ln_b, ys, tm):
    b, s, d = x1.shape
    ys = ys.reshape(ys.shape[0] // SUBLANES, SUBLANES, d)
    grid_spec = pltpu.PrefetchScalarGridSpec(
        num_scalar_prefetch=1, grid=(b, s // tm),
        in_specs=[pl.BlockSpec((1, tm, d), lambda i, j, slot: (i, j, 0)),
                  pl.BlockSpec((1, 6, d), lambda i, j, slot: (i, 0, 0)),
                  pl.BlockSpec((1, d), lambda i, j, slot: (0, 0)),
                  pl.BlockSpec((1, d), lambda i, j, slot: (0, 0)),
                  pl.BlockSpec(memory_space=pl.ANY)],
        out_specs=pl.BlockSpec((1, tm, d), lambda i, j, slot: (i, j, 0)),
        scratch_shapes=[pltpu.VMEM((2, tm // SUBLANES, SUBLANES, d), F32), pltpu.SemaphoreType.DMA((2,))])
    return pl.pallas_call(
        _combine_kernel, grid_spec=grid_spec,
        out_shape=jax.ShapeDtypeStruct((b, s, d), F32),
        compiler_params=_cparams(("arbitrary", "arbitrary")),
        name="moe_combine",
    )(slot, x1, mod, ln_g.reshape(1, -1), ln_b.reshape(1, -1), ys)


def _moe_sublayer(parts, counts, xs_init, wg_bf, wu_bf, wd_bf, ln_g, ln_b):
    sizes = [p[2].shape[0] * p[2].shape[1] for p in parts]
    meta = jnp.concatenate([p[1] for p in parts], axis=1)
    slot, tile, first, second, n_used = _moe_plan(meta, counts, sum(sizes))
    slots, start = [], 0
    for n in sizes:
        slots.append(slot[start:start + n])
        start += n
    xs = xs_init
    for (hx, _, _, _, tm), sl in zip(parts, slots):
        xs = _dispatch_call(sl, hx, xs.reshape(xs_init.shape), tm)
    ys = _moe_call(tile, first, second, n_used, xs, wg_bf, wu_bf, wd_bf)
    return [_combine_call(sl, x1, mod, ln_g, ln_b, ys, tm) for (_, _, x1, mod, tm), sl in zip(parts, slots)]


def kernel(x, c, ctx, c_ctx, w_mod, b_mod, w_in, s5_a_re, s5_a_im, s5_log_dt, s5_b_re, s5_b_im, s5_c_re, s5_c_im, s5_d, w_glu, b_glu, w_sc, w_dw, b_dw, ln_cf_g, ln_cf_b, w_o, ln1_g, ln1_b, w_rg, b_rg, w_rexp, b_rexp, w_gate, w_up, w_down, ln2_g, ln2_b):
    nb, seq, d = x.shape
    n_ctx = ctx.shape[1]
    n_layers = w_mod.shape[0]
    assert seq % TILE_TOK == 0 and n_ctx % TILE_TOK == 0 and seq % GRID_W == 0

    mod_rows = 16
    assert nb + 1 <= mod_rows
    c_all = jnp.concatenate([c, c_ctx[None, :], jnp.zeros((mod_rows - nb - 1, d), F32)], axis=0)
    mod_all = _mod_call(c_all, w_mod, b_mod)

    pad_r = ROUTER_LANES - N_GROUPS - N_EXPERTS
    x_lat, x_ctx = x, ctx
    for l in range(n_layers):
        last = l == n_layers - 1
        mod_lat = mod_all[l, :nb].reshape(nb, 6, d)
        mod_ctx = jnp.broadcast_to(mod_all[l, nb].reshape(1, 6, d), (nb, 6, d))
        w_in_bf = w_in[l].astype(BF16)
        wglu_bf = w_glu[l].astype(BF16)
        wo_bf = w_o[l].astype(BF16)
        wg_bf = w_gate[l].astype(BF16)
        wu_bf = w_up[l].astype(BF16)
        wd_bf = w_down[l].astype(BF16)
        w_router = jnp.concatenate([w_rg[l], w_rexp[l], jnp.zeros((d, pad_r), F32)], axis=1).astype(BF16)
        b_router = jnp.concatenate([b_rg[l], b_rexp[l], jnp.zeros((pad_r,), F32)]).reshape(1, -1)
        w1, e_op, decay = _s5_operators(s5_a_re[l], s5_a_im[l], s5_log_dt[l], s5_b_re[l], s5_b_im[l],
                                        s5_c_re[l], s5_c_im[l])

        u_l, bg_l, cv_l, gl_l = _in_call(x_lat, mod_lat, w_in_bf, 512, False)
        if last:
            u_c = _in_call(x_ctx, mod_ctx, w_in_bf[:, :D_S5], TILE_TOK, True)
        else:
            u_c, bg_c, cv_c, gl_c = _in_call(x_ctx, mod_ctx, w_in_bf, TILE_TOK, False)

        yin_l, gf_l, gb_l = _s5a_call(u_l, w1)
        yin_c, gf_c, gb_c = _s5a_call(u_c, w1)
        hf_c, hf_l, hb_c, hb_l = _s5b_call(decay, gf_c, gf_l, gb_c, gb_l, nb)
        ypre_l = _s5c_call(yin_l, hf_l, hb_l, e_op, u_l, s5_d[l])

        ysc_l, ycf_l = _conv_call(bg_l, cv_l, gl_l, w_sc[l], w_dw[l], b_dw[l], ln_cf_g[l], ln_cf_b[l], True)
        parts = []
        counts = jnp.zeros((1, ROUTER_LANES), F32)
        n_moe = nb * seq
        if not last:
            ypre_c = _s5c_call(yin_c, hf_c, hb_c, e_op, u_c, s5_d[l])
            ysc_c, ycf_c = _conv_call(bg_c, cv_c, gl_c, w_sc[l], w_dw[l], b_dw[l], ln_cf_g[l], ln_cf_b[l], False)
            x1_c, hx_c, meta_c, counts = _out_call(ypre_c, ysc_c, ycf_c, x_ctx, mod_ctx, wglu_bf, b_glu[l], wo_bf,
                                                   ln1_g[l], ln1_b[l], w_router, b_router, counts, TILE_TOK, 0)
            parts.append((hx_c, meta_c, x1_c, mod_ctx, TILE_TOK))
            n_moe += nb * n_ctx
        x1_l, hx_l, meta_l, counts, stage = _out_call(ypre_l, ysc_l, ycf_l, x_lat, mod_lat, wglu_bf, b_glu[l], wo_bf,
                                                      ln1_g[l], ln1_b[l], w_router, b_router, counts, 512,
                                                      _sorted_rows(n_moe))
        parts.append((hx_l, meta_l, x1_l, mod_lat, 512))
        outs = _moe_sublayer(parts, counts, stage, wg_bf, wu_bf, wd_bf, ln2_g[l], ln2_b[l])
        x_lat = outs[-1]
        if not last:
            x_ctx = outs[0]
    return x_lat
```

```python
import functools
import math

import jax
import jax.numpy as jnp
from jax import lax
from jax.experimental import pallas as pl
from jax.experimental.pallas import tpu as pltpu

F32 = jnp.float32
BF16 = jnp.bfloat16

D_MODEL = 1024
DEPTH = 2
GRID_W = 64
D_S5 = 512
S5_GROUP = 16
S5_GROUPS = 32
S5_STATE = 64
D_SC = 256
D_CF = 256
CF_WIDTH = 31
CF_HALF = 15
D_IN = 1792
N_GROUPS = 4
EXP_PER_GROUP = 4
N_EXPERTS = 16
D_EXPERT = 256
DN_ALPHA = (2 * DEPTH) ** 0.25
LN_EPS = 1e-5

CHUNK = 16
N_POW = 32
PREP_GROUPS = 8
TILE_CHUNKS = 16
TILE_TOK = CHUNK * TILE_CHUNKS
LANE_BLK = 128
GRP_PER_BLK = LANE_BLK // S5_GROUP
ROUTER_LANES = 128
HX_LANES = D_MODEL + ROUTER_LANES
META_WA, META_WB, META_CLS, META_RANK = 0, 1, 2, 3
N_CLASSES = N_GROUPS * 6
MOE_TM = 512
OUT_SUBTILES = 2
SUBLANES = 8
VMEM_LIMIT = 56 * 1024 * 1024


def _cparams(sem):
    return pltpu.CompilerParams(dimension_semantics=sem, vmem_limit_bytes=VMEM_LIMIT)


def _split_bf16(a):
    hi = a.astype(BF16)
    lo = (a - hi.astype(F32)).astype(BF16)
    return hi, lo


def _dot(a, b):
    return jnp.dot(a, b, preferred_element_type=F32)


def _dot3(a, b):
    ah, al = _split_bf16(a)
    bh, bl = _split_bf16(b)
    return _dot(ah, bh) + (_dot(al, bh) + _dot(ah, bl))


def _sigmoid(x):
    return 1.0 / (1.0 + jnp.exp(-x))


def _layer_norm(x, g, b):
    mu = jnp.mean(x, axis=-1, keepdims=True)
    xc = x - mu
    var = jnp.mean(xc * xc, axis=-1, keepdims=True)
    return xc * lax.rsqrt(var + LN_EPS) * g + b


def _mod_kernel(c_ref, w_ref, b_ref, o_ref):
    c = c_ref[...]
    s = c * _sigmoid(c)
    o_ref[0] = _dot3(s, w_ref[0]) + b_ref[0]


def _mod_call(c_all, w_mod, b_mod):
    n_layers, d, n_out = w_mod.shape
    tn = 1536
    rows = c_all.shape[0]
    return pl.pallas_call(
        _mod_kernel,
        grid=(n_layers, n_out // tn),
        in_specs=[
            pl.BlockSpec((rows, d), lambda l, j: (0, 0)),
            pl.BlockSpec((1, d, tn), lambda l, j: (l, 0, j)),
            pl.BlockSpec((1, 1, tn), lambda l, j: (l, 0, j)),
        ],
        out_specs=pl.BlockSpec((1, rows, tn), lambda l, j: (l, 0, j)),
        out_shape=jax.ShapeDtypeStruct((n_layers, rows, n_out), F32),
        compiler_params=_cparams(("parallel", "parallel")),
        name="mod",
    )(c_all, w_mod, b_mod.reshape(n_layers, 1, n_out))


def _in_body(x, mod_ref, w_ref, u_ref, bg_ref, cv_ref, gl_ref):
    sh = mod_ref[0, 0:1, :]
    sc = mod_ref[0, 1:2, :]
    h = (x * (1.0 + sc) + sh).astype(BF16)
    z = _dot(h, w_ref[...])
    _store_lane_blocks(u_ref, z[:, 0:512])
    bg_ref[0] = z[:, 512:768]
    cv_ref[0] = z[:, 768:1024] * z[:, 1024:1280]
    gl_ref[0] = z[:, 1280:1536] * _sigmoid(z[:, 1536:1792])


def _in_kernel(x_ref, mod_ref, w_ref, u_ref, bg_ref, cv_ref, gl_ref):
    _in_body(x_ref[0], mod_ref, w_ref, u_ref, bg_ref, cv_ref, gl_ref)


def _in_u_kernel(x_ref, mod_ref, w_ref, u_ref):
    x = x_ref[0]
    sh = mod_ref[0, 0:1, :]
    sc = mod_ref[0, 1:2, :]
    h = (x * (1.0 + sc) + sh).astype(BF16)
    _store_lane_blocks(u_ref, _dot(h, w_ref[...]))


def _store_lane_blocks(ref, val):
    for blk in range(ref.shape[0]):
        ref[blk, 0] = val[:, blk * LANE_BLK:(blk + 1) * LANE_BLK]


def _lane_block_spec(tm):
    return pl.BlockSpec((D_S5 // LANE_BLK, 1, tm, LANE_BLK), lambda i, j: (0, i, j, 0))


def _lane_block_shape(b, s):
    return jax.ShapeDtypeStruct((D_S5 // LANE_BLK, b, s, LANE_BLK), F32)


def _in_call(x, mod, w_in_bf, tm, u_only):
    b, s, d = x.shape
    grid = (b, s // tm)
    row_spec = lambda n: pl.BlockSpec((1, tm, n), lambda i, j: (i, j, 0))
    in_specs = [
        row_spec(d),
        pl.BlockSpec((1, 6, d), lambda i, j: (i, 0, 0)),
    ]
    if u_only:
        in_specs.append(pl.BlockSpec((d, D_S5), lambda i, j: (0, 0)))
        return pl.pallas_call(
            _in_u_kernel, grid=grid, in_specs=in_specs,
            out_specs=_lane_block_spec(tm),
            out_shape=_lane_block_shape(b, s),
            compiler_params=_cparams(("parallel", "parallel")),
            name="in_proj_u",
        )(x, mod, w_in_bf)
    in_specs.append(pl.BlockSpec((d, D_IN), lambda i, j: (0, 0)))
    return pl.pallas_call(
        _in_kernel, grid=grid, in_specs=in_specs,
        out_specs=[_lane_block_spec(tm), row_spec(D_SC), row_spec(D_SC), row_spec(D_CF)],
        out_shape=[_lane_block_shape(b, s),
                   jax.ShapeDtypeStruct((b, s, D_SC), F32),
                   jax.ShapeDtypeStruct((b, s, D_SC), F32),
                   jax.ShapeDtypeStruct((b, s, D_CF), F32)],
        compiler_params=_cparams(("parallel", "parallel")),
        name="in_proj",
    )(x, mod, w_in_bf)


def _conv_tail(t, bdw_ref, lng_ref, lnb_ref):
    t = t + bdw_ref[...]
    t = _layer_norm(t, lng_ref[...], lnb_ref[...])
    return t * _sigmoid(t)


def _conv_grid_kernel(bg_ref, cv_ref, gl_ref, wsc_ref, wdw_ref, bdw_ref, lng_ref, lnb_ref,
                      ysc_ref, ycf_ref, pad_ref, t_ref):
    s = cv_ref.shape[1]
    rows = s // GRID_W
    cv = cv_ref[0]
    col = lax.broadcasted_iota(jnp.int32, (s, D_SC), 0) % GRID_W
    prev = jnp.where(col == 0, 0.0, pltpu.roll(cv, 1, axis=0))
    nxt = jnp.where(col == GRID_W - 1, 0.0, pltpu.roll(cv, s - 1, axis=0))
    conv = prev * wsc_ref[0:1, :] + cv * wsc_ref[1:2, :] + nxt * wsc_ref[2:3, :]
    ysc_ref[0] = (bg_ref[0] * conv).astype(ysc_ref.dtype)

    zero = jnp.zeros((CF_HALF, GRID_W, D_CF), F32)
    pad_ref[0:CF_HALF] = zero
    pad_ref[CF_HALF + rows:CF_HALF + rows + CF_HALF] = zero
    pad_ref[CF_HALF:CF_HALF + rows] = gl_ref[0].reshape(rows, GRID_W, D_CF)

    def body(i, carry):
        w0 = pl.multiple_of(i * 8, 8)
        for half in range(D_CF // 128):
            lanes = slice(half * 128, (half + 1) * 128)
            acc = jnp.zeros((rows, 8, 128), F32)
            for k in range(CF_WIDTH):
                acc = acc + pad_ref[k:k + rows, pl.ds(w0, 8), lanes] * wdw_ref[k:k + 1, lanes]
            t_ref[:, pl.ds(w0, 8), lanes] = acc
        return carry

    lax.fori_loop(0, GRID_W // 8, body, 0)
    t = t_ref[...].reshape(s, D_CF)
    ycf_ref[0] = _conv_tail(t, bdw_ref, lng_ref, lnb_ref).astype(ycf_ref.dtype)


def _conv_seq_kernel(bg_ref, cv_ref, gl_ref, wsc_ref, wdw_ref, bdw_ref, lng_ref, lnb_ref,
                     ysc_ref, ycf_ref, pad_ref):
    s = cv_ref.shape[1]
    cv = cv_ref[0]
    pos = lax.broadcasted_iota(jnp.int32, (s, D_SC), 0)
    prev = jnp.where(pos == 0, 0.0, pltpu.roll(cv, 1, axis=0))
    nxt = jnp.where(pos == s - 1, 0.0, pltpu.roll(cv, s - 1, axis=0))
    conv = prev * wsc_ref[0:1, :] + cv * wsc_ref[1:2, :] + nxt * wsc_ref[2:3, :]
    ysc_ref[0] = (bg_ref[0] * conv).astype(ysc_ref.dtype)

    off = 16
    pad_ref[0:off] = jnp.zeros((off, D_CF), F32)
    pad_ref[off + s:off + s + 16] = jnp.zeros((16, D_CF), F32)
    pad_ref[off:off + s] = gl_ref[0]
    acc = jnp.zeros((s, D_CF), F32)
    for k in range(CF_WIDTH):
        acc = acc + pad_ref[pl.ds(off - CF_HALF + k, s), :] * wdw_ref[k:k + 1, :]
    ycf_ref[0] = _conv_tail(acc, bdw_ref, lng_ref, lnb_ref).astype(ycf_ref.dtype)


def _conv_call(bg, cv, gl, w_sc, w_dw, b_dw, ln_g, ln_b, grid_mode):
    b, s, _ = bg.shape
    row_spec = pl.BlockSpec((1, s, D_SC), lambda i: (i, 0, 0))
    full = lambda shape: pl.BlockSpec(shape, lambda i: (0,) * len(shape))
    if grid_mode:
        rows = s // GRID_W
        kern = _conv_grid_kernel
        scratch = [pltpu.VMEM((rows + 2 * CF_HALF, GRID_W, D_CF), F32),
                   pltpu.VMEM((rows, GRID_W, D_CF), F32)]
        name = "conv_grid"
    else:
        kern = _conv_seq_kernel
        scratch = [pltpu.VMEM((s + 32, D_CF), F32)]
        name = "conv_seq"
    return pl.pallas_call(
        kern, grid=(b,),
        in_specs=[row_spec, row_spec, row_spec, full((3, D_SC)), full((CF_WIDTH, D_CF)),
                  full((1, D_CF)), full((1, D_CF)), full((1, D_CF))],
        out_specs=[row_spec, row_spec],
        out_shape=[jax.ShapeDtypeStruct((b, s, D_SC), BF16), jax.ShapeDtypeStruct((b, s, D_CF), BF16)],
        scratch_shapes=scratch,
        compiler_params=_cparams(("parallel",)),
        name=name,
    )(bg, cv, gl, w_sc, w_dw, b_dw.reshape(1, -1), ln_g.reshape(1, -1), ln_b.reshape(1, -1))


def _split3(a):
    hi = a.astype(BF16)
    r = a - hi.astype(F32)
    mid = r.astype(BF16)
    lo = (r - mid.astype(F32)).astype(BF16)
    return hi, mid, lo


def _select_cols(a, sel):
    hi, mid, lo = _split3(a)
    return _dot(hi, sel) + (_dot(mid, sel) + _dot(lo, sel))


def _select_rows(sel, a):
    hi, mid, lo = _split3(a)
    return _dot(sel, hi) + (_dot(sel, mid) + _dot(sel, lo))


def _cmul(ar, ai, br, bi):
    return ar * br - ai * bi, ar * bi + ai * br


def _s5_pow_kernel(are_ref, aim_ref, ldt_ref, pr_ref, pi_ref):
    j = jnp.minimum(lax.broadcasted_iota(jnp.int32, pr_ref.shape[1:], 0), CHUNK).astype(F32)
    for d in range(2):
        dt = jnp.exp(ldt_ref[d])
        e = jnp.exp(j * (are_ref[d] * dt))
        pr_ref[d] = e * jnp.cos(j * (aim_ref[d] * dt))
        pi_ref[d] = e * jnp.sin(j * (aim_ref[d] * dt))


def _s5_pow_call(a_re, a_im, log_dt):
    g, p = S5_GROUPS, S5_STATE
    flat = lambda a: a.reshape(2, 1, g * p)
    sds = jax.ShapeDtypeStruct((2, N_POW, g * p), F32)
    pr, pi = pl.pallas_call(_s5_pow_kernel, out_shape=[sds, sds], name="s5_powers")(
        flat(a_re), flat(a_im), flat(jnp.repeat(log_dt, p, axis=1)))
    by_group = lambda a: a.reshape(2, N_POW, g, p).transpose(0, 2, 1, 3)
    return by_group(pr), by_group(pi)


def _s5_prep_kernel(are_r, aim_r, pr_ref, pi_ref, qr_ref, qi_ref, bre_ref, bim_ref, cre_ref, cim_ref,
                    w1_ref, e_ref, dec_ref):
    t, n, p = CHUNK, S5_GROUP, S5_STATE
    width = t * n
    lane_tok = lax.broadcasted_iota(jnp.int32, (N_POW, width), 1) // n
    pow_id = lax.broadcasted_iota(jnp.int32, (N_POW, width), 0)
    onehot = lambda cond: jnp.where(cond, 1.0, 0.0).astype(BF16)
    sel_fwd = onehot(pow_id == lane_tok)
    sel_rev = onehot(pow_id == t - 1 - lane_tok)
    sel_out = onehot(pow_id == t - lane_tok)
    row_tok = lax.broadcasted_iota(jnp.int32, (width, N_POW), 0) // n
    row_pow = lax.broadcasted_iota(jnp.int32, (width, N_POW), 1)
    rsel_rev = onehot(row_pow == t - 1 - row_tok)
    rsel_fwd = onehot(row_pow == row_tok)
    lane = lax.broadcasted_iota(jnp.int32, (n, width), 1)
    for gi in range(w1_ref.shape[0]):
        _s5_prep_group(gi, are_r, aim_r, pr_ref, pi_ref, qr_ref, qi_ref, bre_ref, bim_ref, cre_ref, cim_ref,
                       w1_ref, e_ref, dec_ref, (sel_fwd, sel_rev, sel_out, rsel_rev, rsel_fwd, lane))


def _s5_prep_group(gi, are_r, aim_r, pr_ref, pi_ref, qr_ref, qi_ref, bre_ref, bim_ref, cre_ref, cim_ref,
                   w1_ref, e_ref, dec_ref, selectors):
    sel_fwd, sel_rev, sel_out, rsel_rev, rsel_fwd, lane = selectors
    t, n = CHUNK, S5_GROUP
    width = t * n
    strips = []
    f_parts = []
    e_parts = []
    for d in range(2):
        qr = qr_ref[d, gi]
        qi = qi_ref[d, gi]
        pr = pr_ref[d, gi]
        pi = pi_ref[d, gi]
        a_re = are_r[d, gi]
        a_im = aim_r[d, gi]
        nr = pr[1:2] - 1.0
        ni = pi[1:2]
        den = a_re * a_re + a_im * a_im
        fre = (nr * a_re + ni * a_im) / den
        fim = (ni * a_re - nr * a_im) / den
        bt_re = bre_ref[d, gi].T
        bt_im = bim_ref[d, gi].T
        bb_re, bb_im = _cmul(fre, fim, bt_re, bt_im)
        ct_re = jnp.concatenate([cre_ref[d, gi].T] * t, axis=1)
        ct_im = jnp.concatenate([cim_ref[d, gi].T] * t, axis=1)
        sel = sel_fwd if d == 0 else sel_rev
        w_re, w_im = _cmul(ct_re, ct_im, _select_cols(qr, sel), _select_cols(qi, sel))
        strips.append(_dot3(bb_re, w_re) - _dot3(bb_im, w_im))
        if d == 0:
            o_re, o_im = _cmul(w_re, w_im, qr[:, 1:2], qi[:, 1:2])
        else:
            o_re, o_im = _cmul(ct_re, ct_im, _select_cols(qr, sel_out), _select_cols(qi, sel_out))
        e_parts += [o_re, -o_im]
        rsel = rsel_rev if d == 0 else rsel_fwd
        f_re, f_im = _cmul(jnp.concatenate([bb_re] * t, axis=0), jnp.concatenate([bb_im] * t, axis=0),
                           _select_rows(rsel, pr), _select_rows(rsel, pi))
        f_parts += [f_re, f_im]
        dec_ref[gi, 2 * d:2 * d + 1, :] = jnp.concatenate([pr[t:t + 1], pr[t:t + 1]], axis=1)
        dec_ref[gi, 2 * d + 1:2 * d + 2, :] = jnp.concatenate([-pi[t:t + 1], pi[t:t + 1]], axis=1)

    blocks = []
    for s in range(t):
        fwd = strips[0] if s == 0 else jnp.where(lane >= n * s, pltpu.roll(strips[0], n * s, axis=1), 0.0)
        back = t - 1 - s
        bwd = strips[1] if back == 0 else jnp.where(lane < width - n * back,
                                                     pltpu.roll(strips[1], width - n * back, axis=1), 0.0)
        blocks.append(fwd + bwd)
    m = jnp.concatenate(blocks, axis=0)
    w1_ref[gi] = jnp.concatenate([m] + f_parts, axis=1).astype(BF16)
    e_ref[gi] = jnp.concatenate(e_parts, axis=0).astype(BF16)


def _s5_operators(a_re, a_im, log_dt, b_re, b_im, c_re, c_im):
    g, p, n, t = S5_GROUPS, S5_STATE, S5_GROUP, CHUNK
    pr, pi = _s5_pow_call(a_re, a_im, log_dt)
    gb = PREP_GROUPS
    spec = lambda shape: pl.BlockSpec((2, gb) + shape, lambda i: (0, i) + (0,) * len(shape))
    return pl.pallas_call(
        _s5_prep_kernel, grid=(g // gb,),
        in_specs=[spec((1, p)), spec((1, p)), spec((N_POW, p)), spec((N_POW, p)), spec((p, N_POW)),
                  spec((p, N_POW)), spec((p, n)), spec((p, n)), spec((n, p)), spec((n, p))],
        out_specs=[pl.BlockSpec((gb, t * n, 2 * t * n), lambda i: (i, 0, 0)),
                   pl.BlockSpec((gb, 4 * p, t * n), lambda i: (i, 0, 0)),
                   pl.BlockSpec((gb, 4, 2 * p), lambda i: (i, 0, 0))],
        out_shape=[jax.ShapeDtypeStruct((g, t * n, 2 * t * n), BF16),
                   jax.ShapeDtypeStruct((g, 4 * p, t * n), BF16),
                   jax.ShapeDtypeStruct((g, 4, 2 * p), F32)],
        compiler_params=_cparams(("parallel",)),
        name="s5_prep",
    )(a_re.reshape(2, g, 1, p), a_im.reshape(2, g, 1, p), pr, pi, pr.transpose(0, 1, 3, 2),
      pi.transpose(0, 1, 3, 2), b_re, b_im, c_re, c_im)


def _block_transpose8(ps):
    ps = list(ps)
    blk = lax.broadcasted_iota(jnp.int32, ps[0].shape, 1) // S5_GROUP
    for k in range(3):
        step = 1 << k
        shift = S5_GROUP * step
        keep = ((blk >> k) & 1) == 0
        for a in range(8):
            if a & step:
                continue
            pa, pb = ps[a], ps[a + step]
            ps[a] = jnp.where(keep, pa, pltpu.roll(pb, shift, axis=1))
            ps[a + step] = jnp.where(keep, pltpu.roll(pa, 128 - shift, axis=1), pb)
    return ps


def _s5a_kernel(u_ref, w1_ref, yin_ref, gf_ref, gb_ref):
    nb = u_ref.shape[1]
    xs = []
    for s in range(CHUNK):
        parts = [u_ref[0, b, pl.ds(s, TILE_CHUNKS, stride=CHUNK), :] for b in range(nb)]
        xs.append(jnp.concatenate(parts, axis=0))
    lo = _block_transpose8(xs[:8])
    hi = _block_transpose8(xs[8:])
    for j in range(GRP_PER_BLK):
        og = jnp.concatenate([lo[j], hi[j]], axis=1)
        r = _dot(og.astype(BF16), w1_ref[j])
        yin_ref[j] = r[:, 0:256]
        gf_ref[j] = r[:, 256:384]
        gb_ref[j] = r[:, 384:512]


def _s5a_call(u, w1):
    n_blk, b, s, _ = u.shape
    nt = s // TILE_TOK
    rows = b * TILE_CHUNKS
    out_spec = lambda n: pl.BlockSpec((GRP_PER_BLK, rows, n), lambda l, j: (l, j, 0))
    return pl.pallas_call(
        _s5a_kernel, grid=(n_blk, nt),
        in_specs=[pl.BlockSpec((1, b, TILE_TOK, LANE_BLK), lambda l, j: (l, 0, j, 0)),
                  pl.BlockSpec((GRP_PER_BLK, 256, 512), lambda l, j: (l, 0, 0))],
        out_specs=[out_spec(256), out_spec(128), out_spec(128)],
        out_shape=[jax.ShapeDtypeStruct((S5_GROUPS, nt * rows, 256), F32),
                   jax.ShapeDtypeStruct((S5_GROUPS, nt * rows, 128), F32),
                   jax.ShapeDtypeStruct((S5_GROUPS, nt * rows, 128), F32)],
        compiler_params=_cparams(("parallel", "parallel")),
        name="s5_chunk_in",
    )(u, w1)


def _s5b_kernel(nb, a_ref, gfc_ref, gfl_ref, gbc_ref, gbl_ref, hfc_ref, hfl_ref, hbc_ref, hbl_ref):
    gb = a_ref.shape[0]
    rows = nb * TILE_CHUNKS
    n_lat = gfl_ref.shape[1] // rows
    a1f = [jnp.broadcast_to(a_ref[g, 0:1, :], (nb, 128)) for g in range(gb)]
    a2f = [jnp.broadcast_to(a_ref[g, 1:2, :], (nb, 128)) for g in range(gb)]
    a1b = [jnp.broadcast_to(a_ref[g, 2:3, :], (nb, 128)) for g in range(gb)]
    a2b = [jnp.broadcast_to(a_ref[g, 3:4, :], (nb, 128)) for g in range(gb)]

    def step(state, a1, a2, g_ref, h_ref, g, row):
        h, hs = state
        h_ref[g, row, :] = h
        inp = g_ref[g, row, :]
        return a1 * h + a2 * hs + inp, a1 * hs - a2 * h + pltpu.roll(inp, 64, axis=1)

    zero = jnp.zeros((nb, 128), F32)
    hf = [(zero, zero) for _ in range(gb)]
    hb = [(zero, zero) for _ in range(gb)]
    for ci in range(TILE_CHUNKS):
        rf = pl.ds(ci, nb, stride=TILE_CHUNKS)
        rb = pl.ds(TILE_CHUNKS - 1 - ci, nb, stride=TILE_CHUNKS)
        for g in range(gb):
            hf[g] = step(hf[g], a1f[g], a2f[g], gfc_ref, hfc_ref, g, rf)
            hb[g] = step(hb[g], a1b[g], a2b[g], gbc_ref, hbc_ref, g, rb)

    def body(j, carry):
        hf, hb = carry
        hf = list(hf)
        hb = list(hb)
        base_f = j * rows
        base_b = (n_lat - 1 - j) * rows
        for ci in range(TILE_CHUNKS):
            rf = pl.ds(base_f + ci, nb, stride=TILE_CHUNKS)
            rb = pl.ds(base_b + (TILE_CHUNKS - 1 - ci), nb, stride=TILE_CHUNKS)
            for g in range(gb):
                hf[g] = step(hf[g], a1f[g], a2f[g], gfl_ref, hfl_ref, g, rf)
                hb[g] = step(hb[g], a1b[g], a2b[g], gbl_ref, hbl_ref, g, rb)
        return tuple(hf), tuple(hb)

    lax.fori_loop(0, n_lat, body, (tuple(hf), tuple(hb)))


def _s5b_call(decay, gf_c, gf_l, gb_c, gb_l, nb):
    gblk = 4
    spec = lambda a: pl.BlockSpec((gblk, a.shape[1], 128), lambda i: (i, 0, 0))
    sds = lambda a: jax.ShapeDtypeStruct(a.shape, F32)
    return pl.pallas_call(
        functools.partial(_s5b_kernel, nb), grid=(S5_GROUPS // gblk,),
        in_specs=[pl.BlockSpec((gblk, 4, 128), lambda i: (i, 0, 0)),
                  spec(gf_c), spec(gf_l), spec(gb_c), spec(gb_l)],
        out_specs=[spec(gf_c), spec(gf_l), spec(gb_c), spec(gb_l)],
        out_shape=[sds(gf_c), sds(gf_l), sds(gb_c), sds(gb_l)],
        compiler_params=_cparams(("parallel",)),
        name="s5_state_scan",
    )(decay, gf_c, gf_l, gb_c, gb_l)


def _s5c_kernel(yin_ref, hf_ref, hb_ref, e_ref, u_ref, d_ref, y_ref):
    nb = u_ref.shape[1]
    ys = []
    for j in range(GRP_PER_BLK):
        h = jnp.concatenate([hf_ref[j], hb_ref[j]], axis=1).astype(BF16)
        ys.append(yin_ref[j] + _dot(h, e_ref[j]))
    at = (_block_transpose8([y[:, :128] for y in ys])
          + _block_transpose8([y[:, 128:] for y in ys]))
    d = d_ref[...]
    for t in range(CHUNK):
        for b in range(nb):
            rows = pl.ds(t, TILE_CHUNKS, stride=CHUNK)
            y_ref[0, b, rows, :] = at[t][b * TILE_CHUNKS:(b + 1) * TILE_CHUNKS] + d * u_ref[0, b, rows, :]


def _s5c_call(yin, hf, hb, e, u, d_skip):
    n_blk, b, s, _ = u.shape
    nt = s // TILE_TOK
    rows = b * TILE_CHUNKS
    gspec = lambda n: pl.BlockSpec((GRP_PER_BLK, rows, n), lambda l, j: (l, j, 0))
    tok_spec = pl.BlockSpec((1, b, TILE_TOK, LANE_BLK), lambda l, j: (l, 0, j, 0))
    return pl.pallas_call(
        _s5c_kernel, grid=(n_blk, nt),
        in_specs=[gspec(256), gspec(128), gspec(128),
                  pl.BlockSpec((GRP_PER_BLK, 256, 256), lambda l, j: (l, 0, 0)),
                  tok_spec,
                  pl.BlockSpec((1, LANE_BLK), lambda l, j: (0, l))],
        out_specs=tok_spec,
        out_shape=_lane_block_shape(b, s),
        compiler_params=_cparams(("parallel", "parallel")),
        name="s5_chunk_out",
    )(yin, hf, hb, e, u, d_skip.reshape(1, D_S5))


def _gelu_tanh(x):
    return 0.5 * x * (1.0 + jnp.tanh(math.sqrt(2.0 / math.pi) * (x + 0.044715 * (x * x * x))))


def _route(logits):
    lane = lax.broadcasted_iota(jnp.int32, logits.shape, 1).astype(F32)
    neg = jnp.float32(-1e30)
    big = jnp.float32(1e9)
    gl = jnp.where(lane < N_GROUPS, logits, neg)
    gmax = jnp.max(gl, axis=1, keepdims=True)
    gidx = jnp.min(jnp.where(gl == gmax, lane, big), axis=1, keepdims=True)
    gsum = jnp.sum(jnp.exp(gl - gmax), axis=1, keepdims=True)
    gw = 1.0 / gsum
    lo = N_GROUPS + EXP_PER_GROUP * gidx
    el = jnp.where((lane >= lo) & (lane < lo + EXP_PER_GROUP), logits, neg)
    v1 = jnp.max(el, axis=1, keepdims=True)
    i1 = jnp.min(jnp.where(el == v1, lane, big), axis=1, keepdims=True)
    el2 = jnp.where(lane == i1, neg, el)
    v2 = jnp.max(el2, axis=1, keepdims=True)
    i2 = jnp.min(jnp.where(el2 == v2, lane, big), axis=1, keepdims=True)
    ex = jnp.exp(v2 - v1)
    p1 = 1.0 / (1.0 + ex)
    p2 = ex * p1
    e1 = i1 - lo
    e2 = i2 - lo
    first = e1 < e2
    ea = jnp.where(first, e1, e2)
    eb = jnp.where(first, e2, e1)
    wa = gw * jnp.where(first, p1, p2)
    wb = gw * jnp.where(first, p2, p1)
    pair = ea * (7.0 - ea) * 0.5 + (eb - ea - 1.0)
    return wa, wb, 6.0 * gidx + pair


def _out_kernel(ypre_ref, ysc_ref, ycf_ref, x_ref, mod_ref, wglu_ref, bglu_ref, wo_ref,
                lng_ref, lnb_ref, wr_ref, br_ref, cnt0_ref, x1_ref, hx_ref, meta_ref, counts_ref, *rest):
    stage_ref = rest[0] if len(rest) == 2 else None
    cnt_ref = rest[-1]

    @pl.when((pl.program_id(0) == 0) & (pl.program_id(1) == 0))
    def _():
        cnt_ref[...] = cnt0_ref[...]

    tm = x_ref.shape[1]
    sub = tm // OUT_SUBTILES

    def row_chain(r0):
        rows = slice(r0, r0 + sub)
        ypre = jnp.concatenate([ypre_ref[blk, 0, rows, :] for blk in range(ypre_ref.shape[0])], axis=1)
        t = _gelu_tanh(ypre)
        gate = _sigmoid(_dot(t.astype(BF16), wglu_ref[...]) + bglu_ref[...])
        ys5 = (t * gate).astype(BF16)
        y = (_dot(ys5, wo_ref[0:D_S5, :]) + _dot(ysc_ref[0, rows, :], wo_ref[D_S5:D_S5 + D_SC, :])
             + _dot(ycf_ref[0, rows, :], wo_ref[D_S5 + D_SC:D_MODEL, :]))
        g1 = mod_ref[0, 2:3, :]
        x1 = _layer_norm(DN_ALPHA * x_ref[0, rows, :] + g1 * y, lng_ref[...], lnb_ref[...])
        x1_ref[0, rows, :] = x1
        h2 = x1 * (1.0 + mod_ref[0, 4:5, :]) + mod_ref[0, 3:4, :]
        hx_ref[0, rows, 0:D_MODEL] = h2
        return _route(_dot(h2.astype(BF16), wr_ref[...]) + br_ref[...])

    routed = [row_chain(r0) for r0 in range(0, tm, sub)]
    wa, wb, cls = (jnp.concatenate([r[i] for r in routed], axis=0) for i in range(3))

    lane = lax.broadcasted_iota(jnp.int32, (tm, ROUTER_LANES), 1).astype(F32)
    onehot = jnp.where(lane == cls, 1.0, 0.0)
    row_i = lax.broadcasted_iota(jnp.int32, (tm, tm), 0)
    col_i = lax.broadcasted_iota(jnp.int32, (tm, tm), 1)
    earlier = jnp.where(col_i < row_i, 1.0, 0.0).astype(BF16)
    before = _dot(earlier, onehot.astype(BF16)) + cnt_ref[...]
    rank = jnp.sum(before * onehot, axis=1, keepdims=True)
    cnt_ref[...] += jnp.sum(onehot, axis=0, keepdims=True)
    counts_ref[...] = cnt_ref[...]

    meta = (jnp.where(lane == META_WA, wa, 0.0) + jnp.where(lane == META_WB, wb, 0.0)
            + jnp.where(lane == META_CLS, cls, 0.0) + jnp.where(lane == META_RANK, rank, 0.0))
    meta_ref[...] = jnp.transpose(meta)[0:SUBLANES, :]
    hx_ref[0, :, D_MODEL:HX_LANES] = meta
    if stage_ref is not None:
        stage_ref[...] = jnp.zeros_like(stage_ref)


def _out_call(ypre, ysc, ycf, x, mod, wglu_bf, b_glu, wo_bf, ln_g, ln_b, w_router, b_router, counts0, tm,
              stage_rows):
    b, s, d = x.shape
    nt = s // tm
    row_spec = lambda n: pl.BlockSpec((1, tm, n), lambda i, j: (i, j, 0))
    full = lambda shape: pl.BlockSpec(shape, lambda i, j: (0,) * len(shape))
    out_specs = [row_spec(d), row_spec(HX_LANES),
                 pl.BlockSpec((SUBLANES, tm), lambda i, j: (0, i * nt + j)),
                 full((1, ROUTER_LANES))]
    out_shape = [jax.ShapeDtypeStruct((b, s, d), F32), jax.ShapeDtypeStruct((b, s, HX_LANES), F32),
                 jax.ShapeDtypeStruct((SUBLANES, b * s), F32),
                 jax.ShapeDtypeStruct((1, ROUTER_LANES), F32)]
    if stage_rows:
        stage_octs = stage_rows // (SUBLANES * b * nt)
        assert stage_octs * SUBLANES * b * nt == stage_rows
        out_specs.append(pl.BlockSpec((stage_octs, SUBLANES, HX_LANES), lambda i, j: (i * nt + j, 0, 0)))
        out_shape.append(jax.ShapeDtypeStruct((stage_rows // SUBLANES, SUBLANES, HX_LANES), F32))
    return pl.pallas_call(
        _out_kernel, grid=(b, nt),
        in_specs=[_lane_block_spec(tm), row_spec(D_SC), row_spec(D_CF), row_spec(d),
                  pl.BlockSpec((1, 6, d), lambda i, j: (i, 0, 0)),
                  full((D_S5, D_S5)), full((1, D_S5)), full((d, d)),
                  full((1, d)), full((1, d)), full((d, ROUTER_LANES)), full((1, ROUTER_LANES)),
                  full((1, ROUTER_LANES))],
        out_specs=out_specs, out_shape=out_shape,
        scratch_shapes=[pltpu.VMEM((1, ROUTER_LANES), F32)],
        compiler_params=_cparams(("arbitrary", "arbitrary")),
        name="out_proj",
    )(ypre, ysc, ycf, x, mod, wglu_bf, b_glu.reshape(1, -1), wo_bf, ln_g.reshape(1, -1),
      ln_b.reshape(1, -1), w_router, b_router, counts0)


def _sorted_rows(n_tok):
    return n_tok + N_CLASSES * MOE_TM


def _moe_plan(meta, counts, n_tok):
    cls = meta[META_CLS].astype(jnp.int32)
    rank = meta[META_RANK].astype(jnp.int32)
    cnt = counts[0, :N_CLASSES].astype(jnp.int32)
    n_tiles = (cnt + (MOE_TM - 1)) // MOE_TM
    ends = jnp.cumsum(n_tiles)
    starts = ends - n_tiles
    slot = starts[cls] * MOE_TM + rank
    t_max = n_tok // MOE_TM + N_CLASSES
    n_used = ends[N_CLASSES - 1]
    tile = jnp.minimum(jnp.arange(t_max, dtype=jnp.int32), n_used - 1)
    tile_cls = jnp.sum((tile[:, None] >= ends[None, :]).astype(jnp.int32), axis=1)
    group = tile_cls // 6
    pair = tile_cls % 6
    first = jnp.array([0, 0, 0, 1, 1, 2], jnp.int32)[pair] + EXP_PER_GROUP * group
    second = jnp.array([1, 2, 3, 2, 3, 3], jnp.int32)[pair] + EXP_PER_GROUP * group
    return slot, tile, first, second, n_used.reshape(1)


def _split_row(row):
    return lax.shift_right_logical(row, 3), lax.bitwise_and(row, SUBLANES - 1)


def _dispatch_kernel(slot_ref, hx_ref, xs_init_ref, xs_ref, sem):
    del xs_init_ref
    n_oct = hx_ref.shape[1]
    base = (pl.program_id(0) * pl.num_programs(1) + pl.program_id(1)) * (n_oct * SUBLANES)

    def body(i, carry):
        for k in range(SUBLANES):
            oct_id, sub = _split_row(slot_ref[base + i * SUBLANES + k])
            pltpu.make_async_copy(hx_ref.at[0, i, pl.ds(k, 1), :], xs_ref.at[oct_id, pl.ds(sub, 1), :],
                                  sem).start(priority=k % 2)
        return carry

    lax.fori_loop(0, n_oct, body, 0)
    pltpu.make_async_copy(hx_ref.at[0], xs_ref.at[pl.ds(0, n_oct)], sem).wait()


def _dispatch_call(slot, hx, xs_init, tm):
    b, s, w = hx.shape
    n_rows = xs_init.shape[0] * SUBLANES
    grid_spec = pltpu.PrefetchScalarGridSpec(
        num_scalar_prefetch=1, grid=(b, s // tm),
        in_specs=[pl.BlockSpec((1, tm // SUBLANES, SUBLANES, w), lambda i, j, slot: (i, j, 0, 0)),
                  pl.BlockSpec(memory_space=pl.ANY)],
        out_specs=pl.BlockSpec(memory_space=pl.ANY),
        scratch_shapes=[pltpu.SemaphoreType.DMA(())])
    xs = pl.pallas_call(
        _dispatch_kernel, grid_spec=grid_spec,
        out_shape=jax.ShapeDtypeStruct(xs_init.shape, F32),
        input_output_aliases={2: 0},
        compiler_params=_cparams(("arbitrary", "arbitrary")),
        name="moe_dispatch",
    )(slot, hx.reshape(b, s // SUBLANES, SUBLANES, w), xs_init)
    return xs.reshape(n_rows, w)


def _moe_kernel(tile_ref, first_ref, second_ref, nused_ref, xs_ref, wga_ref, wgb_ref, wua_ref, wub_ref,
                wda_ref, wdb_ref, ys_ref):
    del tile_ref, first_ref, second_ref
    t = pl.program_id(0)
    n_used = nused_ref[0]

    @pl.when(t < n_used)
    def _():
        x = xs_ref[...]
        xb = x[:, 0:D_MODEL].astype(BF16)

        def expert(wg_ref, wu_ref, wd_ref, w):
            gate = _dot(xb, wg_ref[0])
            up = _dot(xb, wu_ref[0])
            act = gate * _sigmoid(gate) * up * w
            return _dot(act.astype(BF16), wd_ref[0])

        wa = x[:, D_MODEL + META_WA:D_MODEL + META_WA + 1]
        wb = x[:, D_MODEL + META_WB:D_MODEL + META_WB + 1]
        ys_ref[...] = expert(wga_ref, wua_ref, wda_ref, wa) + expert(wgb_ref, wub_ref, wdb_ref, wb)

    @pl.when(t >= n_used)
    def _():
        ys_ref[...] = jnp.zeros_like(ys_ref)


def _moe_call(tile, first, second, n_used, xs, wg_bf, wu_bf, wd_bf):
    n_rows, w = xs.shape
    d = D_MODEL
    t_max = tile.shape[0]
    up_spec = lambda sel: pl.BlockSpec((1, d, D_EXPERT), lambda t, tl, fi, se, nu: ((fi, se)[sel][t], 0, 0))
    down_spec = lambda sel: pl.BlockSpec((1, D_EXPERT, d), lambda t, tl, fi, se, nu: ((fi, se)[sel][t], 0, 0))
    grid_spec = pltpu.PrefetchScalarGridSpec(
        num_scalar_prefetch=4, grid=(t_max,),
        in_specs=[pl.BlockSpec((MOE_TM, w), lambda t, tl, fi, se, nu: (tl[t], 0)),
                  up_spec(0), up_spec(1), up_spec(0), up_spec(1), down_spec(0), down_spec(1)],
        out_specs=pl.BlockSpec((MOE_TM, d), lambda t, tl, fi, se, nu: (t, 0)))
    return pl.pallas_call(
        _moe_kernel, grid_spec=grid_spec,
        out_shape=jax.ShapeDtypeStruct((n_rows, d), F32),
        compiler_params=_cparams(("arbitrary",)),
        name="moe_experts",
    )(tile, first, second, n_used, xs, wg_bf, wg_bf, wu_bf, wu_bf, wd_bf, wd_bf)


def _combine_kernel(slot_ref, x1_ref, mod_ref, lng_ref, lnb_ref, ys_ref, o_ref, f_ref, sem):
    o_ref[0] = _combine_rows(slot_ref, x1_ref, mod_ref, lng_ref, lnb_ref, ys_ref, f_ref, sem)


def _combine_in_kernel(slot_ref, x1_ref, mod_ref, lng_ref, lnb_ref, ys_ref, mod_next_ref, w_ref,
                       o_ref, u_ref, bg_ref, cv_ref, gl_ref, f_ref, sem):
    x2 = _combine_rows(slot_ref, x1_ref, mod_ref, lng_ref, lnb_ref, ys_ref, f_ref, sem)
    o_ref[0] = x2
    _in_body(x2, mod_next_ref, w_ref, u_ref, bg_ref, cv_ref, gl_ref)


def _combine_rows(slot_ref, x1_ref, mod_ref, lng_ref, lnb_ref, ys_ref, f_ref, sem):
    n_oct = f_ref.shape[1]
    tm = n_oct * SUBLANES
    step = pl.program_id(0) * pl.num_programs(1) + pl.program_id(1)
    n_steps = pl.num_programs(0) * pl.num_programs(1)

    def request(which, buf):
        base = which * tm

        def body(i, carry):
            for k in range(SUBLANES):
                oct_id, sub = _split_row(slot_ref[base + i * SUBLANES + k])
                pltpu.make_async_copy(ys_ref.at[oct_id, pl.ds(sub, 1), :], f_ref.at[buf, i, pl.ds(k, 1), :],
                                      sem.at[buf]).start(priority=k % 2)
            return carry

        lax.fori_loop(0, n_oct, body, 0)

    @pl.when(step == 0)
    def _():
        request(0, 0)

    @pl.when(step + 1 < n_steps)
    def _():
        request(step + 1, (step + 1) % 2)

    buf = step % 2
    pltpu.make_async_copy(ys_ref.at[pl.ds(0, n_oct)], f_ref.at[buf], sem.at[buf]).wait()
    f = f_ref[buf].reshape(tm, f_ref.shape[3])
    g2 = mod_ref[0, 5:6, :]
    return _layer_norm(DN_ALPHA * x1_ref[0] + g2 * f, lng_ref[...], lnb_ref[...])


def _combine_call(slot, x1, mod, ln_g, ln_b, ys, tm, next_in=None):
    b, s, d = x1.shape
    ys = ys.reshape(ys.shape[0] // SUBLANES, SUBLANES, d)
    row_spec = lambda n: pl.BlockSpec((1, tm, n), lambda i, j, slot: (i, j, 0))
    mod_spec = pl.BlockSpec((1, 6, d), lambda i, j, slot: (i, 0, 0))
    full = lambda shape: pl.BlockSpec(shape, lambda i, j, slot: (0,) * len(shape))
    in_specs = [row_spec(d), mod_spec, full((1, d)), full((1, d)), pl.BlockSpec(memory_space=pl.ANY)]
    out_specs = [row_spec(d)]
    out_shape = [jax.ShapeDtypeStruct((b, s, d), F32)]
    args = [slot, x1, mod, ln_g.reshape(1, -1), ln_b.reshape(1, -1), ys]
    kern, name = _combine_kernel, "moe_combine"
    if next_in is not None:
        kern, name = _combine_in_kernel, "moe_combine_in_proj"
        in_specs += [mod_spec, full((d, D_IN))]
        args += list(next_in)
        out_specs += [pl.BlockSpec((D_S5 // LANE_BLK, 1, tm, LANE_BLK), lambda i, j, slot: (0, i, j, 0)),
                      row_spec(D_SC), row_spec(D_SC), row_spec(D_CF)]
        out_shape += [_lane_block_shape(b, s)] + [jax.ShapeDtypeStruct((b, s, D_SC), F32)] * 3
    grid_spec = pltpu.PrefetchScalarGridSpec(
        num_scalar_prefetch=1, grid=(b, s // tm), in_specs=in_specs, out_specs=out_specs,
        scratch_shapes=[pltpu.VMEM((2, tm // SUBLANES, SUBLANES, d), F32), pltpu.SemaphoreType.DMA((2,))])
    outs = pl.pallas_call(
        kern, grid_spec=grid_spec, out_shape=out_shape,
        compiler_params=_cparams(("arbitrary", "arbitrary")),
        name=name,
    )(*args)
    return outs[0] if next_in is None else outs


def _moe_sublayer(parts, counts, xs_init, wg_bf, wu_bf, wd_bf, ln_g, ln_b, next_in=None):
    sizes = [p[2].shape[0] * p[2].shape[1] for p in parts]
    meta = jnp.concatenate([p[1] for p in parts], axis=1)
    slot, tile, first, second, n_used = _moe_plan(meta, counts, sum(sizes))
    slots, start = [], 0
    for n in sizes:
        slots.append(slot[start:start + n])
        start += n
    xs = xs_init
    for (hx, _, _, _, tm), sl in zip(parts, slots):
        xs = _dispatch_call(sl, hx, xs.reshape(xs_init.shape), tm)
    ys = _moe_call(tile, first, second, n_used, xs, wg_bf, wu_bf, wd_bf)
    outs = []
    for idx, ((_, _, x1, mod, tm), sl) in enumerate(zip(parts, slots)):
        fuse = next_in if idx == len(parts) - 1 else None
        outs.append(_combine_call(sl, x1, mod, ln_g, ln_b, ys, tm, fuse))
    return outs


def kernel(x, c, ctx, c_ctx, w_mod, b_mod, w_in, s5_a_re, s5_a_im, s5_log_dt, s5_b_re, s5_b_im, s5_c_re, s5_c_im, s5_d, w_glu, b_glu, w_sc, w_dw, b_dw, ln_cf_g, ln_cf_b, w_o, ln1_g, ln1_b, w_rg, b_rg, w_rexp, b_rexp, w_gate, w_up, w_down, ln2_g, ln2_b):
    nb, seq, d = x.shape
    n_ctx = ctx.shape[1]
    n_layers = w_mod.shape[0]
    assert seq % TILE_TOK == 0 and n_ctx % TILE_TOK == 0 and seq % GRID_W == 0

    mod_rows = 16
    assert nb + 1 <= mod_rows
    c_all = jnp.concatenate([c, c_ctx[None, :], jnp.zeros((mod_rows - nb - 1, d), F32)], axis=0)
    mod_all = _mod_call(c_all, w_mod, b_mod)

    pad_r = ROUTER_LANES - N_GROUPS - N_EXPERTS
    x_lat, x_ctx = x, ctx
    mods_lat = [mod_all[l, :nb].reshape(nb, 6, d) for l in range(n_layers)]
    w_in_bfs = [w_in[l].astype(BF16) for l in range(n_layers)]
    lat_proj = None
    for l in range(n_layers):
        last = l == n_layers - 1
        mod_lat = mods_lat[l]
        mod_ctx = jnp.broadcast_to(mod_all[l, nb].reshape(1, 6, d), (nb, 6, d))
        w_in_bf = w_in_bfs[l]
        wglu_bf = w_glu[l].astype(BF16)
        wo_bf = w_o[l].astype(BF16)
        wg_bf = w_gate[l].astype(BF16)
        wu_bf = w_up[l].astype(BF16)
        wd_bf = w_down[l].astype(BF16)
        w_router = jnp.concatenate([w_rg[l], w_rexp[l], jnp.zeros((d, pad_r), F32)], axis=1).astype(BF16)
        b_router = jnp.concatenate([b_rg[l], b_rexp[l], jnp.zeros((pad_r,), F32)]).reshape(1, -1)
        w1, e_op, decay = _s5_operators(s5_a_re[l], s5_a_im[l], s5_log_dt[l], s5_b_re[l], s5_b_im[l],
                                        s5_c_re[l], s5_c_im[l])

        if lat_proj is None:
            u_l, bg_l, cv_l, gl_l = _in_call(x_lat, mod_lat, w_in_bf, 512, False)
        else:
            u_l, bg_l, cv_l, gl_l = lat_proj
        if last:
            u_c = _in_call(x_ctx, mod_ctx, w_in_bf[:, :D_S5], TILE_TOK, True)
        else:
            u_c, bg_c, cv_c, gl_c = _in_call(x_ctx, mod_ctx, w_in_bf, TILE_TOK, False)

        yin_l, gf_l, gb_l = _s5a_call(u_l, w1)
        yin_c, gf_c, gb_c = _s5a_call(u_c, w1)
        hf_c, hf_l, hb_c, hb_l = _s5b_call(decay, gf_c, gf_l, gb_c, gb_l, nb)
        ypre_l = _s5c_call(yin_l, hf_l, hb_l, e_op, u_l, s5_d[l])

        ysc_l, ycf_l = _conv_call(bg_l, cv_l, gl_l, w_sc[l], w_dw[l], b_dw[l], ln_cf_g[l], ln_cf_b[l], True)
        parts = []
        counts = jnp.zeros((1, ROUTER_LANES), F32)
        n_moe = nb * seq
        if not last:
            ypre_c = _s5c_call(yin_c, hf_c, hb_c, e_op, u_c, s5_d[l])
            ysc_c, ycf_c = _conv_call(bg_c, cv_c, gl_c, w_sc[l], w_dw[l], b_dw[l], ln_cf_g[l], ln_cf_b[l], False)
            x1_c, hx_c, meta_c, counts = _out_call(ypre_c, ysc_c, ycf_c, x_ctx, mod_ctx, wglu_bf, b_glu[l], wo_bf,
                                                   ln1_g[l], ln1_b[l], w_router, b_router, counts, TILE_TOK, 0)
            parts.append((hx_c, meta_c, x1_c, mod_ctx, TILE_TOK))
            n_moe += nb * n_ctx
        x1_l, hx_l, meta_l, counts, stage = _out_call(ypre_l, ysc_l, ycf_l, x_lat, mod_lat, wglu_bf, b_glu[l], wo_bf,
                                                      ln1_g[l], ln1_b[l], w_router, b_router, counts, 512,
                                                      _sorted_rows(n_moe))
        parts.append((hx_l, meta_l, x1_l, mod_lat, 512))
        next_in = None if last else (mods_lat[l + 1], w_in_bfs[l + 1])
        outs = _moe_sublayer(parts, counts, stage, wg_bf, wu_bf, wd_bf, ln2_g[l], ln2_b[l], next_in)
        if last:
            x_lat = outs[-1]
        else:
            x_ctx = outs[0]
            x_lat, lat_proj = outs[-1][0], outs[-1][1:]
    return x_lat
```

```python
import functools
import math

import jax
import jax.numpy as jnp
from jax import lax
from jax.experimental import pallas as pl
from jax.experimental.pallas import tpu as pltpu

F32 = jnp.float32
BF16 = jnp.bfloat16

D_MODEL = 1024
DEPTH = 2
GRID_W = 64
D_S5 = 512
S5_GROUP = 16
S5_GROUPS = 32
S5_STATE = 64
D_SC = 256
D_CF = 256
CF_WIDTH = 31
CF_HALF = 15
D_IN = 1792
N_GROUPS = 4
EXP_PER_GROUP = 4
N_EXPERTS = 16
D_EXPERT = 256
DN_ALPHA = (2 * DEPTH) ** 0.25
LN_EPS = 1e-5

CHUNK = 16
N_POW = 32
PREP_GROUPS = 8
TILE_CHUNKS = 16
TILE_TOK = CHUNK * TILE_CHUNKS
LANE_BLK = 128
GRP_PER_BLK = LANE_BLK // S5_GROUP
ROUTER_LANES = 128
HX_LANES = D_MODEL + ROUTER_LANES
META_WA, META_WB, META_CLS, META_RANK = 0, 1, 2, 3
N_CLASSES = N_GROUPS * 6
MOE_TM = 512
OUT_SUBTILES = 2
SUBLANES = 8
VMEM_LIMIT = 56 * 1024 * 1024


def _cparams(sem):
    return pltpu.CompilerParams(dimension_semantics=sem, vmem_limit_bytes=VMEM_LIMIT)


def _split_bf16(a):
    hi = a.astype(BF16)
    lo = (a - hi.astype(F32)).astype(BF16)
    return hi, lo


def _dot(a, b):
    return jnp.dot(a, b, preferred_element_type=F32)


def _dot3(a, b):
    ah, al = _split_bf16(a)
    bh, bl = _split_bf16(b)
    return _dot(ah, bh) + (_dot(al, bh) + _dot(ah, bl))


def _sigmoid(x):
    return 1.0 / (1.0 + jnp.exp(-x))


def _layer_norm(x, g, b):
    mu = jnp.mean(x, axis=-1, keepdims=True)
    xc = x - mu
    var = jnp.mean(xc * xc, axis=-1, keepdims=True)
    return xc * lax.rsqrt(var + LN_EPS) * g + b


def _mod_kernel(c_ref, w_ref, b_ref, o_ref):
    c = c_ref[...]
    s = c * _sigmoid(c)
    o_ref[0] = _dot3(s, w_ref[0]) + b_ref[0]


def _mod_call(c_all, w_mod, b_mod):
    n_layers, d, n_out = w_mod.shape
    tn = 1536
    rows = c_all.shape[0]
    return pl.pallas_call(
        _mod_kernel,
        grid=(n_layers, n_out // tn),
        in_specs=[
            pl.BlockSpec((rows, d), lambda l, j: (0, 0)),
            pl.BlockSpec((1, d, tn), lambda l, j: (l, 0, j)),
            pl.BlockSpec((1, 1, tn), lambda l, j: (l, 0, j)),
        ],
        out_specs=pl.BlockSpec((1, rows, tn), lambda l, j: (l, 0, j)),
        out_shape=jax.ShapeDtypeStruct((n_layers, rows, n_out), F32),
        compiler_params=_cparams(("parallel", "parallel")),
        name="mod",
    )(c_all, w_mod, b_mod.reshape(n_layers, 1, n_out))


def _in_body(x, mod_ref, w_ref, u_ref, bg_ref, cv_ref, gl_ref):
    sh = mod_ref[0, 0:1, :]
    sc = mod_ref[0, 1:2, :]
    h = (x * (1.0 + sc) + sh).astype(BF16)
    z = _dot(h, w_ref[...])
    _store_lane_blocks(u_ref, z[:, 0:512])
    bg_ref[0] = z[:, 512:768]
    cv_ref[0] = z[:, 768:1024] * z[:, 1024:1280]
    gl_ref[0] = z[:, 1280:1536] * _sigmoid(z[:, 1536:1792])


def _in_kernel(x_ref, mod_ref, w_ref, u_ref, bg_ref, cv_ref, gl_ref):
    _in_body(x_ref[0], mod_ref, w_ref, u_ref, bg_ref, cv_ref, gl_ref)


def _in_u_kernel(x_ref, mod_ref, w_ref, u_ref):
    x = x_ref[0]
    sh = mod_ref[0, 0:1, :]
    sc = mod_ref[0, 1:2, :]
    h = (x * (1.0 + sc) + sh).astype(BF16)
    _store_lane_blocks(u_ref, _dot(h, w_ref[...]))


def _store_lane_blocks(ref, val):
    for blk in range(ref.shape[0]):
        ref[blk, 0] = val[:, blk * LANE_BLK:(blk + 1) * LANE_BLK]


def _lane_block_spec(tm):
    return pl.BlockSpec((D_S5 // LANE_BLK, 1, tm, LANE_BLK), lambda i, j: (0, i, j, 0))


def _lane_block_shape(b, s):
    return jax.ShapeDtypeStruct((D_S5 // LANE_BLK, b, s, LANE_BLK), F32)


def _in_call(x, mod, w_in_bf, tm, u_only):
    b, s, d = x.shape
    grid = (b, s // tm)
    row_spec = lambda n: pl.BlockSpec((1, tm, n), lambda i, j: (i, j, 0))
    in_specs = [
        row_spec(d),
        pl.BlockSpec((1, 6, d), lambda i, j: (i, 0, 0)),
    ]
    if u_only:
        in_specs.append(pl.BlockSpec((d, D_S5), lambda i, j: (0, 0)))
        return pl.pallas_call(
            _in_u_kernel, grid=grid, in_specs=in_specs,
            out_specs=_lane_block_spec(tm),
            out_shape=_lane_block_shape(b, s),
            compiler_params=_cparams(("parallel", "parallel")),
            name="in_proj_u",
        )(x, mod, w_in_bf)
    in_specs.append(pl.BlockSpec((d, D_IN), lambda i, j: (0, 0)))
    return pl.pallas_call(
        _in_kernel, grid=grid, in_specs=in_specs,
        out_specs=[_lane_block_spec(tm), row_spec(D_SC), row_spec(D_SC), row_spec(D_CF)],
        out_shape=[_lane_block_shape(b, s),
                   jax.ShapeDtypeStruct((b, s, D_SC), F32),
                   jax.ShapeDtypeStruct((b, s, D_SC), F32),
                   jax.ShapeDtypeStruct((b, s, D_CF), F32)],
        compiler_params=_cparams(("parallel", "parallel")),
        name="in_proj",
    )(x, mod, w_in_bf)


def _conv_tail(t, bdw_ref, lng_ref, lnb_ref):
    t = t + bdw_ref[...]
    t = _layer_norm(t, lng_ref[...], lnb_ref[...])
    return t * _sigmoid(t)


def _conv_grid_kernel(bg_ref, cv_ref, gl_ref, wsc_ref, wdw_ref, bdw_ref, lng_ref, lnb_ref,
                      ysc_ref, ycf_ref, grid_ref, t_ref):
    s = cv_ref.shape[1]
    rows = s // GRID_W
    cv = cv_ref[0]
    col = lax.broadcasted_iota(jnp.int32, (s, D_SC), 0) % GRID_W
    prev = jnp.where(col == 0, 0.0, pltpu.roll(cv, 1, axis=0))
    nxt = jnp.where(col == GRID_W - 1, 0.0, pltpu.roll(cv, s - 1, axis=0))
    conv = prev * wsc_ref[0:1, :] + cv * wsc_ref[1:2, :] + nxt * wsc_ref[2:3, :]
    ysc_ref[0] = (bg_ref[0] * conv).astype(ysc_ref.dtype)

    grid_ref[...] = gl_ref[0].reshape(rows, GRID_W, D_CF)

    def body(i, carry):
        w0 = pl.multiple_of(i * 8, 8)
        for half in range(D_CF // 128):
            lanes = slice(half * 128, (half + 1) * 128)
            acc = jnp.zeros((rows, 8, 128), F32)
            for k in range(CF_WIDTH):
                lo = max(0, CF_HALF - k)
                hi = min(rows, rows + CF_HALF - k)
                if hi <= lo:
                    continue
                term = grid_ref[lo + k - CF_HALF:hi + k - CF_HALF, pl.ds(w0, 8), lanes] * wdw_ref[k:k + 1, lanes]
                pieces = [acc[:lo]] * (lo > 0) + [acc[lo:hi] + term] + [acc[hi:]] * (hi < rows)
                acc = jnp.concatenate(pieces, axis=0) if len(pieces) > 1 else pieces[0]
            t_ref[:, pl.ds(w0, 8), lanes] = acc
        return carry

    lax.fori_loop(0, GRID_W // 8, body, 0)
    t = t_ref[...].reshape(s, D_CF)
    ycf_ref[0] = _conv_tail(t, bdw_ref, lng_ref, lnb_ref).astype(ycf_ref.dtype)


def _conv_seq_kernel(bg_ref, cv_ref, gl_ref, wsc_ref, wdw_ref, bdw_ref, lng_ref, lnb_ref,
                     ysc_ref, ycf_ref, pad_ref):
    s = cv_ref.shape[1]
    cv = cv_ref[0]
    pos = lax.broadcasted_iota(jnp.int32, (s, D_SC), 0)
    prev = jnp.where(pos == 0, 0.0, pltpu.roll(cv, 1, axis=0))
    nxt = jnp.where(pos == s - 1, 0.0, pltpu.roll(cv, s - 1, axis=0))
    conv = prev * wsc_ref[0:1, :] + cv * wsc_ref[1:2, :] + nxt * wsc_ref[2:3, :]
    ysc_ref[0] = (bg_ref[0] * conv).astype(ysc_ref.dtype)

    off = 16
    pad_ref[0:off] = jnp.zeros((off, D_CF), F32)
    pad_ref[off + s:off + s + 16] = jnp.zeros((16, D_CF), F32)
    pad_ref[off:off + s] = gl_ref[0]
    acc = jnp.zeros((s, D_CF), F32)
    for k in range(CF_WIDTH):
        acc = acc + pad_ref[pl.ds(off - CF_HALF + k, s), :] * wdw_ref[k:k + 1, :]
    ycf_ref[0] = _conv_tail(acc, bdw_ref, lng_ref, lnb_ref).astype(ycf_ref.dtype)


def _conv_call(bg, cv, gl, w_sc, w_dw, b_dw, ln_g, ln_b, grid_mode):
    b, s, _ = bg.shape
    row_spec = pl.BlockSpec((1, s, D_SC), lambda i: (i, 0, 0))
    full = lambda shape: pl.BlockSpec(shape, lambda i: (0,) * len(shape))
    if grid_mode:
        rows = s // GRID_W
        kern = _conv_grid_kernel
        scratch = [pltpu.VMEM((rows, GRID_W, D_CF), F32), pltpu.VMEM((rows, GRID_W, D_CF), F32)]
        name = "conv_grid"
    else:
        kern = _conv_seq_kernel
        scratch = [pltpu.VMEM((s + 32, D_CF), F32)]
        name = "conv_seq"
    return pl.pallas_call(
        kern, grid=(b,),
        in_specs=[row_spec, row_spec, row_spec, full((3, D_SC)), full((CF_WIDTH, D_CF)),
                  full((1, D_CF)), full((1, D_CF)), full((1, D_CF))],
        out_specs=[row_spec, row_spec],
        out_shape=[jax.ShapeDtypeStruct((b, s, D_SC), BF16), jax.ShapeDtypeStruct((b, s, D_CF), BF16)],
        scratch_shapes=scratch,
        compiler_params=_cparams(("parallel",)),
        name=name,
    )(bg, cv, gl, w_sc, w_dw, b_dw.reshape(1, -1), ln_g.reshape(1, -1), ln_b.reshape(1, -1))


def _split3(a):
    hi = a.astype(BF16)
    r = a - hi.astype(F32)
    mid = r.astype(BF16)
    lo = (r - mid.astype(F32)).astype(BF16)
    return hi, mid, lo


def _select_cols(a, sel):
    hi, mid, lo = _split3(a)
    return _dot(hi, sel) + (_dot(mid, sel) + _dot(lo, sel))


def _select_rows(sel, a):
    hi, mid, lo = _split3(a)
    return _dot(sel, hi) + (_dot(sel, mid) + _dot(sel, lo))


def _cmul(ar, ai, br, bi):
    return ar * br - ai * bi, ar * bi + ai * br


def _s5_pow_kernel(are_ref, aim_ref, ldt_ref, pr_ref, pi_ref):
    j = jnp.minimum(lax.broadcasted_iota(jnp.int32, pr_ref.shape[1:], 0), CHUNK).astype(F32)
    for d in range(2):
        dt = jnp.exp(ldt_ref[d])
        e = jnp.exp(j * (are_ref[d] * dt))
        pr_ref[d] = e * jnp.cos(j * (aim_ref[d] * dt))
        pi_ref[d] = e * jnp.sin(j * (aim_ref[d] * dt))


def _s5_pow_call(a_re, a_im, log_dt):
    g, p = S5_GROUPS, S5_STATE
    flat = lambda a: a.reshape(2, 1, g * p)
    sds = jax.ShapeDtypeStruct((2, N_POW, g * p), F32)
    pr, pi = pl.pallas_call(_s5_pow_kernel, out_shape=[sds, sds], name="s5_powers")(
        flat(a_re), flat(a_im), flat(jnp.repeat(log_dt, p, axis=1)))
    by_group = lambda a: a.reshape(2, N_POW, g, p).transpose(0, 2, 1, 3)
    return by_group(pr), by_group(pi)


def _s5_prep_kernel(are_r, aim_r, pr_ref, pi_ref, qr_ref, qi_ref, bre_ref, bim_ref, cre_ref, cim_ref,
                    w1_ref, e_ref, dec_ref):
    t, n, p = CHUNK, S5_GROUP, S5_STATE
    width = t * n
    lane_tok = lax.broadcasted_iota(jnp.int32, (N_POW, width), 1) // n
    pow_id = lax.broadcasted_iota(jnp.int32, (N_POW, width), 0)
    onehot = lambda cond: jnp.where(cond, 1.0, 0.0).astype(BF16)
    sel_fwd = onehot(pow_id == lane_tok)
    sel_rev = onehot(pow_id == t - 1 - lane_tok)
    sel_out = onehot(pow_id == t - lane_tok)
    row_tok = lax.broadcasted_iota(jnp.int32, (width, N_POW), 0) // n
    row_pow = lax.broadcasted_iota(jnp.int32, (width, N_POW), 1)
    rsel_rev = onehot(row_pow == t - 1 - row_tok)
    rsel_fwd = onehot(row_pow == row_tok)
    lane = lax.broadcasted_iota(jnp.int32, (n, width), 1)
    for gi in range(w1_ref.shape[0]):
        _s5_prep_group(gi, are_r, aim_r, pr_ref, pi_ref, qr_ref, qi_ref, bre_ref, bim_ref, cre_ref, cim_ref,
                       w1_ref, e_ref, dec_ref, (sel_fwd, sel_rev, sel_out, rsel_rev, rsel_fwd, lane))


def _s5_prep_group(gi, are_r, aim_r, pr_ref, pi_ref, qr_ref, qi_ref, bre_ref, bim_ref, cre_ref, cim_ref,
                   w1_ref, e_ref, dec_ref, selectors):
    sel_fwd, sel_rev, sel_out, rsel_rev, rsel_fwd, lane = selectors
    t, n = CHUNK, S5_GROUP
    width = t * n
    strips = []
    f_parts = []
    e_parts = []
    for d in range(2):
        qr = qr_ref[d, gi]
        qi = qi_ref[d, gi]
        pr = pr_ref[d, gi]
        pi = pi_ref[d, gi]
        a_re = are_r[d, gi]
        a_im = aim_r[d, gi]
        nr = pr[1:2] - 1.0
        ni = pi[1:2]
        den = a_re * a_re + a_im * a_im
        fre = (nr * a_re + ni * a_im) / den
        fim = (ni * a_re - nr * a_im) / den
        bt_re = bre_ref[d, gi].T
        bt_im = bim_ref[d, gi].T
        bb_re, bb_im = _cmul(fre, fim, bt_re, bt_im)
        ct_re = jnp.concatenate([cre_ref[d, gi].T] * t, axis=1)
        ct_im = jnp.concatenate([cim_ref[d, gi].T] * t, axis=1)
        sel = sel_fwd if d == 0 else sel_rev
        w_re, w_im = _cmul(ct_re, ct_im, _select_cols(qr, sel), _select_cols(qi, sel))
        strips.append(_dot3(bb_re, w_re) - _dot3(bb_im, w_im))
        if d == 0:
            o_re, o_im = _cmul(w_re, w_im, qr[:, 1:2], qi[:, 1:2])
        else:
            o_re, o_im = _cmul(ct_re, ct_im, _select_cols(qr, sel_out), _select_cols(qi, sel_out))
        e_parts += [o_re, -o_im]
        rsel = rsel_rev if d == 0 else rsel_fwd
        f_re, f_im = _cmul(jnp.concatenate([bb_re] * t, axis=0), jnp.concatenate([bb_im] * t, axis=0),
                           _select_rows(rsel, pr), _select_rows(rsel, pi))
        f_parts += [f_re, f_im]
        dec_ref[gi, 2 * d:2 * d + 1, :] = jnp.concatenate([pr[t:t + 1], pr[t:t + 1]], axis=1)
        dec_ref[gi, 2 * d + 1:2 * d + 2, :] = jnp.concatenate([-pi[t:t + 1], pi[t:t + 1]], axis=1)

    blocks = []
    for s in range(t):
        fwd = strips[0] if s == 0 else jnp.where(lane >= n * s, pltpu.roll(strips[0], n * s, axis=1), 0.0)
        back = t - 1 - s
        bwd = strips[1] if back == 0 else jnp.where(lane < width - n * back,
                                                     pltpu.roll(strips[1], width - n * back, axis=1), 0.0)
        blocks.append(fwd + bwd)
    m = jnp.concatenate(blocks, axis=0)
    w1_ref[gi] = jnp.concatenate([m] + f_parts, axis=1).astype(BF16)
    e_ref[gi] = jnp.concatenate(e_parts, axis=0).astype(BF16)


def _s5_operators(a_re, a_im, log_dt, b_re, b_im, c_re, c_im):
    g, p, n, t = S5_GROUPS, S5_STATE, S5_GROUP, CHUNK
    pr, pi = _s5_pow_call(a_re, a_im, log_dt)
    gb = PREP_GROUPS
    spec = lambda shape: pl.BlockSpec((2, gb) + shape, lambda i: (0, i) + (0,) * len(shape))
    return pl.pallas_call(
        _s5_prep_kernel, grid=(g // gb,),
        in_specs=[spec((1, p)), spec((1, p)), spec((N_POW, p)), spec((N_POW, p)), spec((p, N_POW)),
                  spec((p, N_POW)), spec((p, n)), spec((p, n)), spec((n, p)), spec((n, p))],
        out_specs=[pl.BlockSpec((gb, t * n, 2 * t * n), lambda i: (i, 0, 0)),
                   pl.BlockSpec((gb, 4 * p, t * n), lambda i: (i, 0, 0)),
                   pl.BlockSpec((gb, 4, 2 * p), lambda i: (i, 0, 0))],
        out_shape=[jax.ShapeDtypeStruct((g, t * n, 2 * t * n), BF16),
                   jax.ShapeDtypeStruct((g, 4 * p, t * n), BF16),
                   jax.ShapeDtypeStruct((g, 4, 2 * p), F32)],
        compiler_params=_cparams(("parallel",)),
        name="s5_prep",
    )(a_re.reshape(2, g, 1, p), a_im.reshape(2, g, 1, p), pr, pi, pr.transpose(0, 1, 3, 2),
      pi.transpose(0, 1, 3, 2), b_re, b_im, c_re, c_im)


def _block_transpose8(ps):
    ps = list(ps)
    blk = lax.broadcasted_iota(jnp.int32, ps[0].shape, 1) // S5_GROUP
    for k in range(3):
        step = 1 << k
        shift = S5_GROUP * step
        keep = ((blk >> k) & 1) == 0
        for a in range(8):
            if a & step:
                continue
            pa, pb = ps[a], ps[a + step]
            ps[a] = jnp.where(keep, pa, pltpu.roll(pb, shift, axis=1))
            ps[a + step] = jnp.where(keep, pltpu.roll(pa, 128 - shift, axis=1), pb)
    return ps


def _s5a_kernel(u_ref, w1_ref, yin_ref, gf_ref, gb_ref):
    nb = u_ref.shape[1]
    xs = []
    for s in range(CHUNK):
        parts = [u_ref[0, b, pl.ds(s, TILE_CHUNKS, stride=CHUNK), :] for b in range(nb)]
        xs.append(jnp.concatenate(parts, axis=0))
    lo = _block_transpose8(xs[:8])
    hi = _block_transpose8(xs[8:])
    for j in range(GRP_PER_BLK):
        og = jnp.concatenate([lo[j], hi[j]], axis=1)
        r = _dot(og.astype(BF16), w1_ref[j])
        yin_ref[j] = r[:, 0:256]
        gf_ref[j] = r[:, 256:384]
        gb_ref[j] = r[:, 384:512]


def _s5a_call(u, w1):
    n_blk, b, s, _ = u.shape
    nt = s // TILE_TOK
    rows = b * TILE_CHUNKS
    out_spec = lambda n: pl.BlockSpec((GRP_PER_BLK, rows, n), lambda l, j: (l, j, 0))
    return pl.pallas_call(
        _s5a_kernel, grid=(n_blk, nt),
        in_specs=[pl.BlockSpec((1, b, TILE_TOK, LANE_BLK), lambda l, j: (l, 0, j, 0)),
                  pl.BlockSpec((GRP_PER_BLK, 256, 512), lambda l, j: (l, 0, 0))],
        out_specs=[out_spec(256), out_spec(128), out_spec(128)],
        out_shape=[jax.ShapeDtypeStruct((S5_GROUPS, nt * rows, 256), F32),
                   jax.ShapeDtypeStruct((S5_GROUPS, nt * rows, 128), F32),
                   jax.ShapeDtypeStruct((S5_GROUPS, nt * rows, 128), F32)],
        compiler_params=_cparams(("parallel", "parallel")),
        name="s5_chunk_in",
    )(u, w1)


def _s5b_kernel(nb, a_ref, gfc_ref, gfl_ref, gbc_ref, gbl_ref, hfc_ref, hfl_ref, hbc_ref, hbl_ref):
    gb = a_ref.shape[0]
    rows = nb * TILE_CHUNKS
    n_lat = gfl_ref.shape[1] // rows
    a1f = [jnp.broadcast_to(a_ref[g, 0:1, :], (nb, 128)) for g in range(gb)]
    a2f = [jnp.broadcast_to(a_ref[g, 1:2, :], (nb, 128)) for g in range(gb)]
    a1b = [jnp.broadcast_to(a_ref[g, 2:3, :], (nb, 128)) for g in range(gb)]
    a2b = [jnp.broadcast_to(a_ref[g, 3:4, :], (nb, 128)) for g in range(gb)]

    def step(state, a1, a2, g_ref, h_ref, g, row):
        h, hs = state
        h_ref[g, row, :] = h
        inp = g_ref[g, row, :]
        return a1 * h + a2 * hs + inp, a1 * hs - a2 * h + pltpu.roll(inp, 64, axis=1)

    zero = jnp.zeros((nb, 128), F32)
    hf = [(zero, zero) for _ in range(gb)]
    hb = [(zero, zero) for _ in range(gb)]
    for ci in range(TILE_CHUNKS):
        rf = pl.ds(ci, nb, stride=TILE_CHUNKS)
        rb = pl.ds(TILE_CHUNKS - 1 - ci, nb, stride=TILE_CHUNKS)
        for g in range(gb):
            hf[g] = step(hf[g], a1f[g], a2f[g], gfc_ref, hfc_ref, g, rf)
            hb[g] = step(hb[g], a1b[g], a2b[g], gbc_ref, hbc_ref, g, rb)

    def body(j, carry):
        hf, hb = carry
        hf = list(hf)
        hb = list(hb)
        base_f = j * rows
        base_b = (n_lat - 1 - j) * rows
        for ci in range(TILE_CHUNKS):
            rf = pl.ds(base_f + ci, nb, stride=TILE_CHUNKS)
            rb = pl.ds(base_b + (TILE_CHUNKS - 1 - ci), nb, stride=TILE_CHUNKS)
            for g in range(gb):
                hf[g] = step(hf[g], a1f[g], a2f[g], gfl_ref, hfl_ref, g, rf)
                hb[g] = step(hb[g], a1b[g], a2b[g], gbl_ref, hbl_ref, g, rb)
        return tuple(hf), tuple(hb)

    lax.fori_loop(0, n_lat, body, (tuple(hf), tuple(hb)))


def _s5b_call(decay, gf_c, gf_l, gb_c, gb_l, nb):
    gblk = 4
    spec = lambda a: pl.BlockSpec((gblk, a.shape[1], 128), lambda i: (i, 0, 0))
    sds = lambda a: jax.ShapeDtypeStruct(a.shape, F32)
    return pl.pallas_call(
        functools.partial(_s5b_kernel, nb), grid=(S5_GROUPS // gblk,),
        in_specs=[pl.BlockSpec((gblk, 4, 128), lambda i: (i, 0, 0)),
                  spec(gf_c), spec(gf_l), spec(gb_c), spec(gb_l)],
        out_specs=[spec(gf_c), spec(gf_l), spec(gb_c), spec(gb_l)],
        out_shape=[sds(gf_c), sds(gf_l), sds(gb_c), sds(gb_l)],
        compiler_params=_cparams(("parallel",)),
        name="s5_state_scan",
    )(decay, gf_c, gf_l, gb_c, gb_l)


def _s5c_kernel(yin_ref, hf_ref, hb_ref, e_ref, u_ref, d_ref, y_ref):
    nb = u_ref.shape[1]
    ys = []
    for j in range(GRP_PER_BLK):
        h = jnp.concatenate([hf_ref[j], hb_ref[j]], axis=1).astype(BF16)
        ys.append(yin_ref[j] + _dot(h, e_ref[j]))
    at = (_block_transpose8([y[:, :128] for y in ys])
          + _block_transpose8([y[:, 128:] for y in ys]))
    d = d_ref[...]
    for t in range(CHUNK):
        for b in range(nb):
            rows = pl.ds(t, TILE_CHUNKS, stride=CHUNK)
            y_ref[0, b, rows, :] = at[t][b * TILE_CHUNKS:(b + 1) * TILE_CHUNKS] + d * u_ref[0, b, rows, :]


def _s5c_call(yin, hf, hb, e, u, d_skip):
    n_blk, b, s, _ = u.shape
    nt = s // TILE_TOK
    rows = b * TILE_CHUNKS
    gspec = lambda n: pl.BlockSpec((GRP_PER_BLK, rows, n), lambda l, j: (l, j, 0))
    tok_spec = pl.BlockSpec((1, b, TILE_TOK, LANE_BLK), lambda l, j: (l, 0, j, 0))
    return pl.pallas_call(
        _s5c_kernel, grid=(n_blk, nt),
        in_specs=[gspec(256), gspec(128), gspec(128),
                  pl.BlockSpec((GRP_PER_BLK, 256, 256), lambda l, j: (l, 0, 0)),
                  tok_spec,
                  pl.BlockSpec((1, LANE_BLK), lambda l, j: (0, l))],
        out_specs=tok_spec,
        out_shape=_lane_block_shape(b, s),
        compiler_params=_cparams(("parallel", "parallel")),
        name="s5_chunk_out",
    )(yin, hf, hb, e, u, d_skip.reshape(1, D_S5))


def _gelu_tanh(x):
    return 0.5 * x * (1.0 + jnp.tanh(math.sqrt(2.0 / math.pi) * (x + 0.044715 * (x * x * x))))


def _route(logits):
    lane = lax.broadcasted_iota(jnp.int32, logits.shape, 1).astype(F32)
    neg = jnp.float32(-1e30)
    big = jnp.float32(1e9)
    gl = jnp.where(lane < N_GROUPS, logits, neg)
    gmax = jnp.max(gl, axis=1, keepdims=True)
    gidx = jnp.min(jnp.where(gl == gmax, lane, big), axis=1, keepdims=True)
    gsum = jnp.sum(jnp.exp(gl - gmax), axis=1, keepdims=True)
    gw = 1.0 / gsum
    lo = N_GROUPS + EXP_PER_GROUP * gidx
    el = jnp.where((lane >= lo) & (lane < lo + EXP_PER_GROUP), logits, neg)
    v1 = jnp.max(el, axis=1, keepdims=True)
    i1 = jnp.min(jnp.where(el == v1, lane, big), axis=1, keepdims=True)
    el2 = jnp.where(lane == i1, neg, el)
    v2 = jnp.max(el2, axis=1, keepdims=True)
    i2 = jnp.min(jnp.where(el2 == v2, lane, big), axis=1, keepdims=True)
    ex = jnp.exp(v2 - v1)
    p1 = 1.0 / (1.0 + ex)
    p2 = ex * p1
    e1 = i1 - lo
    e2 = i2 - lo
    first = e1 < e2
    ea = jnp.where(first, e1, e2)
    eb = jnp.where(first, e2, e1)
    wa = gw * jnp.where(first, p1, p2)
    wb = gw * jnp.where(first, p2, p1)
    pair = ea * (7.0 - ea) * 0.5 + (eb - ea - 1.0)
    return wa, wb, 6.0 * gidx + pair


def _out_kernel(ypre_ref, ysc_ref, ycf_ref, x_ref, mod_ref, wglu_ref, bglu_ref, wo_ref,
                lng_ref, lnb_ref, wr_ref, br_ref, cnt0_ref, x1_ref, hx_ref, meta_ref, counts_ref, *rest):
    stage_ref = rest[0] if len(rest) == 2 else None
    cnt_ref = rest[-1]

    @pl.when((pl.program_id(0) == 0) & (pl.program_id(1) == 0))
    def _():
        cnt_ref[...] = cnt0_ref[...]

    tm = x_ref.shape[1]
    sub = tm // OUT_SUBTILES

    def row_chain(r0):
        rows = slice(r0, r0 + sub)
        ypre = jnp.concatenate([ypre_ref[blk, 0, rows, :] for blk in range(ypre_ref.shape[0])], axis=1)
        t = _gelu_tanh(ypre)
        gate = _sigmoid(_dot(t.astype(BF16), wglu_ref[...]) + bglu_ref[...])
        ys5 = (t * gate).astype(BF16)
        y = (_dot(ys5, wo_ref[0:D_S5, :]) + _dot(ysc_ref[0, rows, :], wo_ref[D_S5:D_S5 + D_SC, :])
             + _dot(ycf_ref[0, rows, :], wo_ref[D_S5 + D_SC:D_MODEL, :]))
        g1 = mod_ref[0, 2:3, :]
        x1 = _layer_norm(DN_ALPHA * x_ref[0, rows, :] + g1 * y, lng_ref[...], lnb_ref[...])
        x1_ref[0, rows, :] = x1
        h2 = x1 * (1.0 + mod_ref[0, 4:5, :]) + mod_ref[0, 3:4, :]
        hx_ref[0, rows, 0:D_MODEL] = h2
        return _route(_dot(h2.astype(BF16), wr_ref[...]) + br_ref[...])

    routed = [row_chain(r0) for r0 in range(0, tm, sub)]
    wa, wb, cls = (jnp.concatenate([r[i] for r in routed], axis=0) for i in range(3))

    lane = lax.broadcasted_iota(jnp.int32, (tm, ROUTER_LANES), 1).astype(F32)
    onehot = jnp.where(lane == cls, 1.0, 0.0)
    row_i = lax.broadcasted_iota(jnp.int32, (tm, tm), 0)
    col_i = lax.broadcasted_iota(jnp.int32, (tm, tm), 1)
    earlier = jnp.where(col_i < row_i, 1.0, 0.0).astype(BF16)
    before = _dot(earlier, onehot.astype(BF16)) + cnt_ref[...]
    rank = jnp.sum(before * onehot, axis=1, keepdims=True)
    cnt_ref[...] += jnp.sum(onehot, axis=0, keepdims=True)
    counts_ref[...] = cnt_ref[...]

    meta = (jnp.where(lane == META_WA, wa, 0.0) + jnp.where(lane == META_WB, wb, 0.0)
            + jnp.where(lane == META_CLS, cls, 0.0) + jnp.where(lane == META_RANK, rank, 0.0))
    meta_ref[...] = jnp.transpose(meta)[0:SUBLANES, :]
    hx_ref[0, :, D_MODEL:HX_LANES] = meta
    if stage_ref is not None:
        stage_ref[...] = jnp.zeros_like(stage_ref)


def _out_call(ypre, ysc, ycf, x, mod, wglu_bf, b_glu, wo_bf, ln_g, ln_b, w_router, b_router, counts0, tm,
              stage_rows):
    b, s, d = x.shape
    nt = s // tm
    row_spec = lambda n: pl.BlockSpec((1, tm, n), lambda i, j: (i, j, 0))
    full = lambda shape: pl.BlockSpec(shape, lambda i, j: (0,) * len(shape))
    out_specs = [row_spec(d), row_spec(HX_LANES),
                 pl.BlockSpec((SUBLANES, tm), lambda i, j: (0, i * nt + j)),
                 full((1, ROUTER_LANES))]
    out_shape = [jax.ShapeDtypeStruct((b, s, d), F32), jax.ShapeDtypeStruct((b, s, HX_LANES), F32),
                 jax.ShapeDtypeStruct((SUBLANES, b * s), F32),
                 jax.ShapeDtypeStruct((1, ROUTER_LANES), F32)]
    if stage_rows:
        stage_octs = stage_rows // (SUBLANES * b * nt)
        assert stage_octs * SUBLANES * b * nt == stage_rows
        out_specs.append(pl.BlockSpec((stage_octs, SUBLANES, HX_LANES), lambda i, j: (i * nt + j, 0, 0)))
        out_shape.append(jax.ShapeDtypeStruct((stage_rows // SUBLANES, SUBLANES, HX_LANES), F32))
    return pl.pallas_call(
        _out_kernel, grid=(b, nt),
        in_specs=[_lane_block_spec(tm), row_spec(D_SC), row_spec(D_CF), row_spec(d),
                  pl.BlockSpec((1, 6, d), lambda i, j: (i, 0, 0)),
                  full((D_S5, D_S5)), full((1, D_S5)), full((d, d)),
                  full((1, d)), full((1, d)), full((d, ROUTER_LANES)), full((1, ROUTER_LANES)),
                  full((1, ROUTER_LANES))],
        out_specs=out_specs, out_shape=out_shape,
        scratch_shapes=[pltpu.VMEM((1, ROUTER_LANES), F32)],
        compiler_params=_cparams(("arbitrary", "arbitrary")),
        name="out_proj",
    )(ypre, ysc, ycf, x, mod, wglu_bf, b_glu.reshape(1, -1), wo_bf, ln_g.reshape(1, -1),
      ln_b.reshape(1, -1), w_router, b_router, counts0)


def _sorted_rows(n_tok):
    return n_tok + N_CLASSES * MOE_TM


def _moe_plan(meta, counts, n_tok):
    cls = meta[META_CLS].astype(jnp.int32)
    rank = meta[META_RANK].astype(jnp.int32)
    cnt = counts[0, :N_CLASSES].astype(jnp.int32)
    n_tiles = (cnt + (MOE_TM - 1)) // MOE_TM
    ends = jnp.cumsum(n_tiles)
    starts = ends - n_tiles
    slot = starts[cls] * MOE_TM + rank
    t_max = n_tok // MOE_TM + N_CLASSES
    n_used = ends[N_CLASSES - 1]
    tile = jnp.minimum(jnp.arange(t_max, dtype=jnp.int32), n_used - 1)
    tile_cls = jnp.sum((tile[:, None] >= ends[None, :]).astype(jnp.int32), axis=1)
    group = tile_cls // 6
    pair = tile_cls % 6
    first = jnp.array([0, 0, 0, 1, 1, 2], jnp.int32)[pair] + EXP_PER_GROUP * group
    second = jnp.array([1, 2, 3, 2, 3, 3], jnp.int32)[pair] + EXP_PER_GROUP * group
    return slot, tile, first, second, n_used.reshape(1)


def _split_row(row):
    return lax.shift_right_logical(row, 3), lax.bitwise_and(row, SUBLANES - 1)


def _dispatch_kernel(slot_ref, hx_ref, xs_init_ref, xs_ref, sem):
    del xs_init_ref
    n_oct = hx_ref.shape[1]
    base = (pl.program_id(0) * pl.num_programs(1) + pl.program_id(1)) * (n_oct * SUBLANES)

    def body(i, carry):
        for k in range(SUBLANES):
            oct_id, sub = _split_row(slot_ref[base + i * SUBLANES + k])
            pltpu.make_async_copy(hx_ref.at[0, i, pl.ds(k, 1), :], xs_ref.at[oct_id, pl.ds(sub, 1), :],
                                  sem).start(priority=k % 2)
        return carry

    lax.fori_loop(0, n_oct, body, 0)
    pltpu.make_async_copy(hx_ref.at[0], xs_ref.at[pl.ds(0, n_oct)], sem).wait()


def _dispatch_call(slot, hx, xs_init, tm):
    b, s, w = hx.shape
    n_rows = xs_init.shape[0] * SUBLANES
    grid_spec = pltpu.PrefetchScalarGridSpec(
        num_scalar_prefetch=1, grid=(b, s // tm),
        in_specs=[pl.BlockSpec((1, tm // SUBLANES, SUBLANES, w), lambda i, j, slot: (i, j, 0, 0)),
                  pl.BlockSpec(memory_space=pl.ANY)],
        out_specs=pl.BlockSpec(memory_space=pl.ANY),
        scratch_shapes=[pltpu.SemaphoreType.DMA(())])
    xs = pl.pallas_call(
        _dispatch_kernel, grid_spec=grid_spec,
        out_shape=jax.ShapeDtypeStruct(xs_init.shape, F32),
        input_output_aliases={2: 0},
        compiler_params=_cparams(("arbitrary", "arbitrary")),
        name="moe_dispatch",
    )(slot, hx.reshape(b, s // SUBLANES, SUBLANES, w), xs_init)
    return xs.reshape(n_rows, w)


def _moe_kernel(tile_ref, first_ref, second_ref, nused_ref, xs_ref, wga_ref, wgb_ref, wua_ref, wub_ref,
                wda_ref, wdb_ref, ys_ref):
    del tile_ref, first_ref, second_ref
    t = pl.program_id(0)
    n_used = nused_ref[0]

    @pl.when(t < n_used)
    def _():
        x = xs_ref[...]
        xb = x[:, 0:D_MODEL].astype(BF16)

        def expert(wg_ref, wu_ref, wd_ref, w):
            gate = _dot(xb, wg_ref[0])
            up = _dot(xb, wu_ref[0])
            act = gate * _sigmoid(gate) * up * w
            return _dot(act.astype(BF16), wd_ref[0])

        wa = x[:, D_MODEL + META_WA:D_MODEL + META_WA + 1]
        wb = x[:, D_MODEL + META_WB:D_MODEL + META_WB + 1]
        ys_ref[...] = expert(wga_ref, wua_ref, wda_ref, wa) + expert(wgb_ref, wub_ref, wdb_ref, wb)

    @pl.when(t >= n_used)
    def _():
        ys_ref[...] = jnp.zeros_like(ys_ref)


def _moe_call(tile, first, second, n_used, xs, wg_bf, wu_bf, wd_bf):
    n_rows, w = xs.shape
    d = D_MODEL
    t_max = tile.shape[0]
    up_spec = lambda sel: pl.BlockSpec((1, d, D_EXPERT), lambda t, tl, fi, se, nu: ((fi, se)[sel][t], 0, 0))
    down_spec = lambda sel: pl.BlockSpec((1, D_EXPERT, d), lambda t, tl, fi, se, nu: ((fi, se)[sel][t], 0, 0))
    grid_spec = pltpu.PrefetchScalarGridSpec(
        num_scalar_prefetch=4, grid=(t_max,),
        in_specs=[pl.BlockSpec((MOE_TM, w), lambda t, tl, fi, se, nu: (tl[t], 0)),
                  up_spec(0), up_spec(1), up_spec(0), up_spec(1), down_spec(0), down_spec(1)],
        out_specs=pl.BlockSpec((MOE_TM, d), lambda t, tl, fi, se, nu: (t, 0)))
    return pl.pallas_call(
        _moe_kernel, grid_spec=grid_spec,
        out_shape=jax.ShapeDtypeStruct((n_rows, d), F32),
        compiler_params=_cparams(("arbitrary",)),
        name="moe_experts",
    )(tile, first, second, n_used, xs, wg_bf, wg_bf, wu_bf, wu_bf, wd_bf, wd_bf)


def _combine_kernel(slot_ref, x1_ref, mod_ref, lng_ref, lnb_ref, ys_ref, o_ref, f_ref, sem):
    o_ref[0] = _combine_rows(slot_ref, x1_ref, mod_ref, lng_ref, lnb_ref, ys_ref, f_ref, sem)


def _combine_in_kernel(slot_ref, x1_ref, mod_ref, lng_ref, lnb_ref, ys_ref, mod_next_ref, w_ref,
                       o_ref, u_ref, bg_ref, cv_ref, gl_ref, f_ref, sem):
    x2 = _combine_rows(slot_ref, x1_ref, mod_ref, lng_ref, lnb_ref, ys_ref, f_ref, sem)
    o_ref[0] = x2
    _in_body(x2, mod_next_ref, w_ref, u_ref, bg_ref, cv_ref, gl_ref)


def _combine_rows(slot_ref, x1_ref, mod_ref, lng_ref, lnb_ref, ys_ref, f_ref, sem):
    n_oct = f_ref.shape[1]
    tm = n_oct * SUBLANES
    step = pl.program_id(0) * pl.num_programs(1) + pl.program_id(1)
    n_steps = pl.num_programs(0) * pl.num_programs(1)

    def request(which, buf):
        base = which * tm

        def body(i, carry):
            for k in range(SUBLANES):
                oct_id, sub = _split_row(slot_ref[base + i * SUBLANES + k])
                pltpu.make_async_copy(ys_ref.at[oct_id, pl.ds(sub, 1), :], f_ref.at[buf, i, pl.ds(k, 1), :],
                                      sem.at[buf]).start(priority=k % 2)
            return carry

        lax.fori_loop(0, n_oct, body, 0)

    @pl.when(step == 0)
    def _():
        request(0, 0)

    @pl.when(step + 1 < n_steps)
    def _():
        request(step + 1, (step + 1) % 2)

    buf = step % 2
    pltpu.make_async_copy(ys_ref.at[pl.ds(0, n_oct)], f_ref.at[buf], sem.at[buf]).wait()
    f = f_ref[buf].reshape(tm, f_ref.shape[3])
    g2 = mod_ref[0, 5:6, :]
    return _layer_norm(DN_ALPHA * x1_ref[0] + g2 * f, lng_ref[...], lnb_ref[...])


def _combine_call(slot, x1, mod, ln_g, ln_b, ys, tm, next_in=None):
    b, s, d = x1.shape
    ys = ys.reshape(ys.shape[0] // SUBLANES, SUBLANES, d)
    row_spec = lambda n: pl.BlockSpec((1, tm, n), lambda i, j, slot: (i, j, 0))
    mod_spec = pl.BlockSpec((1, 6, d), lambda i, j, slot: (i, 0, 0))
    full = lambda shape: pl.BlockSpec(shape, lambda i, j, slot: (0,) * len(shape))
    in_specs = [row_spec(d), mod_spec, full((1, d)), full((1, d)), pl.BlockSpec(memory_space=pl.ANY)]
    out_specs = [row_spec(d)]
    out_shape = [jax.ShapeDtypeStruct((b, s, d), F32)]
    args = [slot, x1, mod, ln_g.reshape(1, -1), ln_b.reshape(1, -1), ys]
    kern, name = _combine_kernel, "moe_combine"
    if next_in is not None:
        kern, name = _combine_in_kernel, "moe_combine_in_proj"
        in_specs += [mod_spec, full((d, D_IN))]
        args += list(next_in)
        out_specs += [pl.BlockSpec((D_S5 // LANE_BLK, 1, tm, LANE_BLK), lambda i, j, slot: (0, i, j, 0)),
                      row_spec(D_SC), row_spec(D_SC), row_spec(D_CF)]
        out_shape += [_lane_block_shape(b, s)] + [jax.ShapeDtypeStruct((b, s, D_SC), F32)] * 3
    grid_spec = pltpu.PrefetchScalarGridSpec(
        num_scalar_prefetch=1, grid=(b, s // tm), in_specs=in_specs, out_specs=out_specs,
        scratch_shapes=[pltpu.VMEM((2, tm // SUBLANES, SUBLANES, d), F32), pltpu.SemaphoreType.DMA((2,))])
    outs = pl.pallas_call(
        kern, grid_spec=grid_spec, out_shape=out_shape,
        compiler_params=_cparams(("arbitrary", "arbitrary")),
        name=name,
    )(*args)
    return outs[0] if next_in is None else outs


def _moe_sublayer(parts, counts, xs_init, wg_bf, wu_bf, wd_bf, ln_g, ln_b, next_in=None):
    sizes = [p[2].shape[0] * p[2].shape[1] for p in parts]
    meta = jnp.concatenate([p[1] for p in parts], axis=1)
    slot, tile, first, second, n_used = _moe_plan(meta, counts, sum(sizes))
    slots, start = [], 0
    for n in sizes:
        slots.append(slot[start:start + n])
        start += n
    xs = xs_init
    for (hx, _, _, _, tm), sl in zip(parts, slots):
        xs = _dispatch_call(sl, hx, xs.reshape(xs_init.shape), tm)
    ys = _moe_call(tile, first, second, n_used, xs, wg_bf, wu_bf, wd_bf)
    outs = []
    for idx, ((_, _, x1, mod, tm), sl) in enumerate(zip(parts, slots)):
        fuse = next_in if idx == len(parts) - 1 else None
        outs.append(_combine_call(sl, x1, mod, ln_g, ln_b, ys, tm, fuse))
    return outs


def kernel(x, c, ctx, c_ctx, w_mod, b_mod, w_in, s5_a_re, s5_a_im, s5_log_dt, s5_b_re, s5_b_im, s5_c_re, s5_c_im, s5_d, w_glu, b_glu, w_sc, w_dw, b_dw, ln_cf_g, ln_cf_b, w_o, ln1_g, ln1_b, w_rg, b_rg, w_rexp, b_rexp, w_gate, w_up, w_down, ln2_g, ln2_b):
    nb, seq, d = x.shape
    n_ctx = ctx.shape[1]
    n_layers = w_mod.shape[0]
    assert seq % TILE_TOK == 0 and n_ctx % TILE_TOK == 0 and seq % GRID_W == 0

    mod_rows = 16
    assert nb + 1 <= mod_rows
    c_all = jnp.concatenate([c, c_ctx[None, :], jnp.zeros((mod_rows - nb - 1, d), F32)], axis=0)
    mod_all = _mod_call(c_all, w_mod, b_mod)

    pad_r = ROUTER_LANES - N_GROUPS - N_EXPERTS
    x_lat, x_ctx = x, ctx
    mods_lat = [mod_all[l, :nb].reshape(nb, 6, d) for l in range(n_layers)]
    w_in_bfs = [w_in[l].astype(BF16) for l in range(n_layers)]
    lat_proj = None
    for l in range(n_layers):
        last = l == n_layers - 1
        mod_lat = mods_lat[l]
        mod_ctx = jnp.broadcast_to(mod_all[l, nb].reshape(1, 6, d), (nb, 6, d))
        w_in_bf = w_in_bfs[l]
        wglu_bf = w_glu[l].astype(BF16)
        wo_bf = w_o[l].astype(BF16)
        wg_bf = w_gate[l].astype(BF16)
        wu_bf = w_up[l].astype(BF16)
        wd_bf = w_down[l].astype(BF16)
        w_router = jnp.concatenate([w_rg[l], w_rexp[l], jnp.zeros((d, pad_r), F32)], axis=1).astype(BF16)
        b_router = jnp.concatenate([b_rg[l], b_rexp[l], jnp.zeros((pad_r,), F32)]).reshape(1, -1)
        w1, e_op, decay = _s5_operators(s5_a_re[l], s5_a_im[l], s5_log_dt[l], s5_b_re[l], s5_b_im[l],
                                        s5_c_re[l], s5_c_im[l])

        if lat_proj is None:
            u_l, bg_l, cv_l, gl_l = _in_call(x_lat, mod_lat, w_in_bf, 512, False)
        else:
            u_l, bg_l, cv_l, gl_l = lat_proj
        if last:
            u_c = _in_call(x_ctx, mod_ctx, w_in_bf[:, :D_S5], TILE_TOK, True)
        else:
            u_c, bg_c, cv_c, gl_c = _in_call(x_ctx, mod_ctx, w_in_bf, TILE_TOK, False)

        yin_l, gf_l, gb_l = _s5a_call(u_l, w1)
        yin_c, gf_c, gb_c = _s5a_call(u_c, w1)
        hf_c, hf_l, hb_c, hb_l = _s5b_call(decay, gf_c, gf_l, gb_c, gb_l, nb)
        ypre_l = _s5c_call(yin_l, hf_l, hb_l, e_op, u_l, s5_d[l])

        ysc_l, ycf_l = _conv_call(bg_l, cv_l, gl_l, w_sc[l], w_dw[l], b_dw[l], ln_cf_g[l], ln_cf_b[l], True)
        parts = []
        counts = jnp.zeros((1, ROUTER_LANES), F32)
        n_moe = nb * seq
        if not last:
            ypre_c = _s5c_call(yin_c, hf_c, hb_c, e_op, u_c, s5_d[l])
            ysc_c, ycf_c = _conv_call(bg_c, cv_c, gl_c, w_sc[l], w_dw[l], b_dw[l], ln_cf_g[l], ln_cf_b[l], False)
            x1_c, hx_c, meta_c, counts = _out_call(ypre_c, ysc_c, ycf_c, x_ctx, mod_ctx, wglu_bf, b_glu[l], wo_bf,
                                                   ln1_g[l], ln1_b[l], w_router, b_router, counts, TILE_TOK, 0)
            parts.append((hx_c, meta_c, x1_c, mod_ctx, TILE_TOK))
            n_moe += nb * n_ctx
        x1_l, hx_l, meta_l, counts, stage = _out_call(ypre_l, ysc_l, ycf_l, x_lat, mod_lat, wglu_bf, b_glu[l], wo_bf,
                                                      ln1_g[l], ln1_b[l], w_router, b_router, counts, 512,
                                                      _sorted_rows(n_moe))
        parts.append((hx_l, meta_l, x1_l, mod_lat, 512))
        next_in = None if last else (mods_lat[l + 1], w_in_bfs[l + 1])
        outs = _moe_sublayer(parts, counts, stage, wg_bf, wu_bf, wd_bf, ln2_g[l], ln2_b[l], next_in)
        if last:
            x_lat = outs[-1]
        else:
            x_ctx = outs[0]
            x_lat, lat_proj = outs[-1][0], outs[-1][1:]
    return x_lat
```

```python
import functools
import math

import jax
import jax.numpy as jnp
from jax import lax
from jax.experimental import pallas as pl
from jax.experimental.pallas import tpu as pltpu

F32 = jnp.float32
BF16 = jnp.bfloat16

D_MODEL = 1024
DEPTH = 2
GRID_W = 64
D_S5 = 512
S5_GROUP = 16
S5_GROUPS = 32
S5_STATE = 64
D_SC = 256
D_CF = 256
CF_WIDTH = 31
CF_HALF = 15
D_IN = 1792
N_GROUPS = 4
EXP_PER_GROUP = 4
N_EXPERTS = 16
D_EXPERT = 256
DN_ALPHA = (2 * DEPTH) ** 0.25
LN_EPS = 1e-5

CHUNK = 16
N_POW = 32
PREP_GROUPS = 8
TILE_CHUNKS = 16
TILE_TOK = CHUNK * TILE_CHUNKS
LANE_BLK = 128
GRP_PER_BLK = LANE_BLK // S5_GROUP
ROUTER_LANES = 128
HX_LANES = D_MODEL + ROUTER_LANES
META_WA, META_WB, META_CLS, META_RANK = 0, 1, 2, 3
N_CLASSES = N_GROUPS * 6
MOE_TM = 512
OUT_SUBTILES = 2
SUBLANES = 8
VMEM_LIMIT = 56 * 1024 * 1024


def _cparams(sem):
    return pltpu.CompilerParams(dimension_semantics=sem, vmem_limit_bytes=VMEM_LIMIT)


def _split_bf16(a):
    hi = a.astype(BF16)
    lo = (a - hi.astype(F32)).astype(BF16)
    return hi, lo


def _dot(a, b):
    return jnp.dot(a, b, preferred_element_type=F32)


def _dot3(a, b):
    ah, al = _split_bf16(a)
    bh, bl = _split_bf16(b)
    return _dot(ah, bh) + (_dot(al, bh) + _dot(ah, bl))


def _sigmoid(x):
    return 1.0 / (1.0 + jnp.exp(-x))


def _layer_norm(x, g, b):
    mu = jnp.mean(x, axis=-1, keepdims=True)
    xc = x - mu
    var = jnp.mean(xc * xc, axis=-1, keepdims=True)
    return xc * lax.rsqrt(var + LN_EPS) * g + b


def _mod_kernel(c_ref, w_ref, b_ref, o_ref):
    c = c_ref[...]
    s = c * _sigmoid(c)
    o_ref[0] = _dot3(s, w_ref[0]) + b_ref[0]


def _mod_call(c_all, w_mod, b_mod):
    n_layers, d, n_out = w_mod.shape
    tn = 1536
    rows = c_all.shape[0]
    return pl.pallas_call(
        _mod_kernel,
        grid=(n_layers, n_out // tn),
        in_specs=[
            pl.BlockSpec((rows, d), lambda l, j: (0, 0)),
            pl.BlockSpec((1, d, tn), lambda l, j: (l, 0, j)),
            pl.BlockSpec((1, 1, tn), lambda l, j: (l, 0, j)),
        ],
        out_specs=pl.BlockSpec((1, rows, tn), lambda l, j: (l, 0, j)),
        out_shape=jax.ShapeDtypeStruct((n_layers, rows, n_out), F32),
        compiler_params=_cparams(("parallel", "parallel")),
        name="mod",
    )(c_all, w_mod, b_mod.reshape(n_layers, 1, n_out))


def _in_body(x, mod_ref, w_ref, u_ref, bg_ref, cv_ref, gl_ref):
    sh = mod_ref[0, 0:1, :]
    sc = mod_ref[0, 1:2, :]
    h = (x * (1.0 + sc) + sh).astype(BF16)
    z = _dot(h, w_ref[...])
    _store_lane_blocks(u_ref, z[:, 0:512])
    bg_ref[0] = z[:, 512:768]
    cv_ref[0] = z[:, 768:1024] * z[:, 1024:1280]
    gl_ref[0] = z[:, 1280:1536] * _sigmoid(z[:, 1536:1792])


def _in_kernel(x_ref, mod_ref, w_ref, u_ref, bg_ref, cv_ref, gl_ref):
    _in_body(x_ref[0], mod_ref, w_ref, u_ref, bg_ref, cv_ref, gl_ref)


def _in_u_kernel(x_ref, mod_ref, w_ref, u_ref):
    x = x_ref[0]
    sh = mod_ref[0, 0:1, :]
    sc = mod_ref[0, 1:2, :]
    h = (x * (1.0 + sc) + sh).astype(BF16)
    _store_lane_blocks(u_ref, _dot(h, w_ref[...]))


def _store_lane_blocks(ref, val):
    for blk in range(ref.shape[0]):
        ref[blk, 0] = val[:, blk * LANE_BLK:(blk + 1) * LANE_BLK]


def _lane_block_spec(tm):
    return pl.BlockSpec((D_S5 // LANE_BLK, 1, tm, LANE_BLK), lambda i, j: (0, i, j, 0))


def _lane_block_shape(b, s):
    return jax.ShapeDtypeStruct((D_S5 // LANE_BLK, b, s, LANE_BLK), F32)


def _in_call(x, mod, w_in_bf, tm, u_only):
    b, s, d = x.shape
    grid = (b, s // tm)
    row_spec = lambda n: pl.BlockSpec((1, tm, n), lambda i, j: (i, j, 0))
    in_specs = [
        row_spec(d),
        pl.BlockSpec((1, 6, d), lambda i, j: (i, 0, 0)),
    ]
    if u_only:
        in_specs.append(pl.BlockSpec((d, D_S5), lambda i, j: (0, 0)))
        return pl.pallas_call(
            _in_u_kernel, grid=grid, in_specs=in_specs,
            out_specs=_lane_block_spec(tm),
            out_shape=_lane_block_shape(b, s),
            compiler_params=_cparams(("parallel", "parallel")),
            name="in_proj_u",
        )(x, mod, w_in_bf)
    in_specs.append(pl.BlockSpec((d, D_IN), lambda i, j: (0, 0)))
    return pl.pallas_call(
        _in_kernel, grid=grid, in_specs=in_specs,
        out_specs=[_lane_block_spec(tm), row_spec(D_SC), row_spec(D_SC), row_spec(D_CF)],
        out_shape=[_lane_block_shape(b, s),
                   jax.ShapeDtypeStruct((b, s, D_SC), F32),
                   jax.ShapeDtypeStruct((b, s, D_SC), F32),
                   jax.ShapeDtypeStruct((b, s, D_CF), F32)],
        compiler_params=_cparams(("parallel", "parallel")),
        name="in_proj",
    )(x, mod, w_in_bf)


def _conv_tail(t, bdw_ref, lng_ref, lnb_ref):
    t = t + bdw_ref[...]
    t = _layer_norm(t, lng_ref[...], lnb_ref[...])
    return t * _sigmoid(t)


def _conv_grid_kernel(bg_ref, cv_ref, gl_ref, wsc_ref, wdw_ref, bdw_ref, lng_ref, lnb_ref,
                      ysc_ref, ycf_ref, grid_ref, t_ref):
    s = cv_ref.shape[1]
    rows = s // GRID_W
    cv = cv_ref[0]
    col = lax.broadcasted_iota(jnp.int32, (s, D_SC), 0) % GRID_W
    prev = jnp.where(col == 0, 0.0, pltpu.roll(cv, 1, axis=0))
    nxt = jnp.where(col == GRID_W - 1, 0.0, pltpu.roll(cv, s - 1, axis=0))
    conv = prev * wsc_ref[0:1, :] + cv * wsc_ref[1:2, :] + nxt * wsc_ref[2:3, :]
    ysc_ref[0] = (bg_ref[0] * conv).astype(ysc_ref.dtype)

    grid_ref[...] = gl_ref[0].reshape(rows, GRID_W, D_CF)

    def body(i, carry):
        w0 = pl.multiple_of(i * 8, 8)
        for half in range(D_CF // 128):
            lanes = slice(half * 128, (half + 1) * 128)
            acc = jnp.zeros((rows, 8, 128), F32)
            for k in range(CF_WIDTH):
                lo = max(0, CF_HALF - k)
                hi = min(rows, rows + CF_HALF - k)
                if hi <= lo:
                    continue
                term = grid_ref[lo + k - CF_HALF:hi + k - CF_HALF, pl.ds(w0, 8), lanes] * wdw_ref[k:k + 1, lanes]
                pieces = [acc[:lo]] * (lo > 0) + [acc[lo:hi] + term] + [acc[hi:]] * (hi < rows)
                acc = jnp.concatenate(pieces, axis=0) if len(pieces) > 1 else pieces[0]
            t_ref[:, pl.ds(w0, 8), lanes] = acc
        return carry

    lax.fori_loop(0, GRID_W // 8, body, 0)
    t = t_ref[...].reshape(s, D_CF)
    ycf_ref[0] = _conv_tail(t, bdw_ref, lng_ref, lnb_ref).astype(ycf_ref.dtype)


def _conv_seq_kernel(bg_ref, cv_ref, gl_ref, wsc_ref, wdw_ref, bdw_ref, lng_ref, lnb_ref,
                     ysc_ref, ycf_ref, pad_ref):
    s = cv_ref.shape[1]
    cv = cv_ref[0]
    pos = lax.broadcasted_iota(jnp.int32, (s, D_SC), 0)
    prev = jnp.where(pos == 0, 0.0, pltpu.roll(cv, 1, axis=0))
    nxt = jnp.where(pos == s - 1, 0.0, pltpu.roll(cv, s - 1, axis=0))
    conv = prev * wsc_ref[0:1, :] + cv * wsc_ref[1:2, :] + nxt * wsc_ref[2:3, :]
    ysc_ref[0] = (bg_ref[0] * conv).astype(ysc_ref.dtype)

    off = 16
    pad_ref[0:off] = jnp.zeros((off, D_CF), F32)
    pad_ref[off + s:off + s + 16] = jnp.zeros((16, D_CF), F32)
    pad_ref[off:off + s] = gl_ref[0]
    acc = jnp.zeros((s, D_CF), F32)
    for k in range(CF_WIDTH):
        acc = acc + pad_ref[pl.ds(off - CF_HALF + k, s), :] * wdw_ref[k:k + 1, :]
    ycf_ref[0] = _conv_tail(acc, bdw_ref, lng_ref, lnb_ref).astype(ycf_ref.dtype)


def _conv_call(bg, cv, gl, w_sc, w_dw, b_dw, ln_g, ln_b, grid_mode):
    b, s, _ = bg.shape
    row_spec = pl.BlockSpec((1, s, D_SC), lambda i: (i, 0, 0))
    full = lambda shape: pl.BlockSpec(shape, lambda i: (0,) * len(shape))
    if grid_mode:
        rows = s // GRID_W
        kern = _conv_grid_kernel
        scratch = [pltpu.VMEM((rows, GRID_W, D_CF), F32), pltpu.VMEM((rows, GRID_W, D_CF), F32)]
        name = "conv_grid"
    else:
        kern = _conv_seq_kernel
        scratch = [pltpu.VMEM((s + 32, D_CF), F32)]
        name = "conv_seq"
    return pl.pallas_call(
        kern, grid=(b,),
        in_specs=[row_spec, row_spec, row_spec, full((3, D_SC)), full((CF_WIDTH, D_CF)),
                  full((1, D_CF)), full((1, D_CF)), full((1, D_CF))],
        out_specs=[row_spec, row_spec],
        out_shape=[jax.ShapeDtypeStruct((b, s, D_SC), BF16), jax.ShapeDtypeStruct((b, s, D_CF), BF16)],
        scratch_shapes=scratch,
        compiler_params=_cparams(("parallel",)),
        name=name,
    )(bg, cv, gl, w_sc, w_dw, b_dw.reshape(1, -1), ln_g.reshape(1, -1), ln_b.reshape(1, -1))


def _split3(a):
    hi = a.astype(BF16)
    r = a - hi.astype(F32)
    mid = r.astype(BF16)
    lo = (r - mid.astype(F32)).astype(BF16)
    return hi, mid, lo


def _select_cols(a, sel):
    hi, mid, lo = _split3(a)
    return _dot(hi, sel) + (_dot(mid, sel) + _dot(lo, sel))


def _select_rows(sel, a):
    hi, mid, lo = _split3(a)
    return _dot(sel, hi) + (_dot(sel, mid) + _dot(sel, lo))


def _cmul(ar, ai, br, bi):
    return ar * br - ai * bi, ar * bi + ai * br


def _s5_pow_kernel(are_ref, aim_ref, ldt_ref, pr_ref, pi_ref):
    j = jnp.minimum(lax.broadcasted_iota(jnp.int32, pr_ref.shape[1:], 0), CHUNK).astype(F32)
    for d in range(2):
        dt = jnp.exp(ldt_ref[d])
        e = jnp.exp(j * (are_ref[d] * dt))
        pr_ref[d] = e * jnp.cos(j * (aim_ref[d] * dt))
        pi_ref[d] = e * jnp.sin(j * (aim_ref[d] * dt))


def _s5_pow_call(a_re, a_im, log_dt):
    g, p = S5_GROUPS, S5_STATE
    flat = lambda a: a.reshape(2, 1, g * p)
    sds = jax.ShapeDtypeStruct((2, N_POW, g * p), F32)
    pr, pi = pl.pallas_call(_s5_pow_kernel, out_shape=[sds, sds], name="s5_powers")(
        flat(a_re), flat(a_im), flat(jnp.repeat(log_dt, p, axis=1)))
    by_group = lambda a: a.reshape(2, N_POW, g, p).transpose(0, 2, 1, 3)
    return by_group(pr), by_group(pi)


def _s5_prep_kernel(are_r, aim_r, pr_ref, pi_ref, qr_ref, qi_ref, bre_ref, bim_ref, cre_ref, cim_ref,
                    w1_ref, e_ref, dec_ref):
    t, n, p = CHUNK, S5_GROUP, S5_STATE
    width = t * n
    lane_tok = lax.broadcasted_iota(jnp.int32, (N_POW, width), 1) // n
    pow_id = lax.broadcasted_iota(jnp.int32, (N_POW, width), 0)
    onehot = lambda cond: jnp.where(cond, 1.0, 0.0).astype(BF16)
    sel_fwd = onehot(pow_id == lane_tok)
    sel_rev = onehot(pow_id == t - 1 - lane_tok)
    sel_out = onehot(pow_id == t - lane_tok)
    row_tok = lax.broadcasted_iota(jnp.int32, (width, N_POW), 0) // n
    row_pow = lax.broadcasted_iota(jnp.int32, (width, N_POW), 1)
    rsel_rev = onehot(row_pow == t - 1 - row_tok)
    rsel_fwd = onehot(row_pow == row_tok)
    lane = lax.broadcasted_iota(jnp.int32, (n, width), 1)
    for gi in range(w1_ref.shape[0]):
        _s5_prep_group(gi, are_r, aim_r, pr_ref, pi_ref, qr_ref, qi_ref, bre_ref, bim_ref, cre_ref, cim_ref,
                       w1_ref, e_ref, dec_ref, (sel_fwd, sel_rev, sel_out, rsel_rev, rsel_fwd, lane))


def _s5_prep_group(gi, are_r, aim_r, pr_ref, pi_ref, qr_ref, qi_ref, bre_ref, bim_ref, cre_ref, cim_ref,
                   w1_ref, e_ref, dec_ref, selectors):
    sel_fwd, sel_rev, sel_out, rsel_rev, rsel_fwd, lane = selectors
    t, n = CHUNK, S5_GROUP
    width = t * n
    strips = []
    f_parts = []
    e_parts = []
    for d in range(2):
        qr = qr_ref[d, gi]
        qi = qi_ref[d, gi]
        pr = pr_ref[d, gi]
        pi = pi_ref[d, gi]
        a_re = are_r[d, gi]
        a_im = aim_r[d, gi]
        nr = pr[1:2] - 1.0
        ni = pi[1:2]
        den = a_re * a_re + a_im * a_im
        fre = (nr * a_re + ni * a_im) / den
        fim = (ni * a_re - nr * a_im) / den
        bt_re = bre_ref[d, gi].T
        bt_im = bim_ref[d, gi].T
        bb_re, bb_im = _cmul(fre, fim, bt_re, bt_im)
        ct_re = jnp.concatenate([cre_ref[d, gi].T] * t, axis=1)
        ct_im = jnp.concatenate([cim_ref[d, gi].T] * t, axis=1)
        sel = sel_fwd if d == 0 else sel_rev
        w_re, w_im = _cmul(ct_re, ct_im, _select_cols(qr, sel), _select_cols(qi, sel))
        strips.append(_dot3(bb_re, w_re) - _dot3(bb_im, w_im))
        if d == 0:
            o_re, o_im = _cmul(w_re, w_im, qr[:, 1:2], qi[:, 1:2])
        else:
            o_re, o_im = _cmul(ct_re, ct_im, _select_cols(qr, sel_out), _select_cols(qi, sel_out))
        e_parts += [o_re, -o_im]
        rsel = rsel_rev if d == 0 else rsel_fwd
        f_re, f_im = _cmul(jnp.concatenate([bb_re] * t, axis=0), jnp.concatenate([bb_im] * t, axis=0),
                           _select_rows(rsel, pr), _select_rows(rsel, pi))
        f_parts += [f_re, f_im]
        dec_ref[gi, 2 * d:2 * d + 1, :] = jnp.concatenate([pr[t:t + 1], pr[t:t + 1]], axis=1)
        dec_ref[gi, 2 * d + 1:2 * d + 2, :] = jnp.concatenate([-pi[t:t + 1], pi[t:t + 1]], axis=1)

    blocks = []
    for s in range(t):
        fwd = strips[0] if s == 0 else jnp.where(lane >= n * s, pltpu.roll(strips[0], n * s, axis=1), 0.0)
        back = t - 1 - s
        bwd = strips[1] if back == 0 else jnp.where(lane < width - n * back,
                                                     pltpu.roll(strips[1], width - n * back, axis=1), 0.0)
        blocks.append(fwd + bwd)
    m = jnp.concatenate(blocks, axis=0)
    w1_ref[gi] = jnp.concatenate([m] + f_parts, axis=1).astype(BF16)
    e_ref[gi] = jnp.concatenate(e_parts, axis=0).astype(BF16)


def _s5_operators(a_re, a_im, log_dt, b_re, b_im, c_re, c_im):
    g, p, n, t = S5_GROUPS, S5_STATE, S5_GROUP, CHUNK
    pr, pi = _s5_pow_call(a_re, a_im, log_dt)
    gb = PREP_GROUPS
    spec = lambda shape: pl.BlockSpec((2, gb) + shape, lambda i: (0, i) + (0,) * len(shape))
    return pl.pallas_call(
        _s5_prep_kernel, grid=(g // gb,),
        in_specs=[spec((1, p)), spec((1, p)), spec((N_POW, p)), spec((N_POW, p)), spec((p, N_POW)),
                  spec((p, N_POW)), spec((p, n)), spec((p, n)), spec((n, p)), spec((n, p))],
        out_specs=[pl.BlockSpec((gb, t * n, 2 * t * n), lambda i: (i, 0, 0)),
                   pl.BlockSpec((gb, 4 * p, t * n), lambda i: (i, 0, 0)),
                   pl.BlockSpec((gb, 4, 2 * p), lambda i: (i, 0, 0))],
        out_shape=[jax.ShapeDtypeStruct((g, t * n, 2 * t * n), BF16),
                   jax.ShapeDtypeStruct((g, 4 * p, t * n), BF16),
                   jax.ShapeDtypeStruct((g, 4, 2 * p), F32)],
        compiler_params=_cparams(("parallel",)),
        name="s5_prep",
    )(a_re.reshape(2, g, 1, p), a_im.reshape(2, g, 1, p), pr, pi, pr.transpose(0, 1, 3, 2),
      pi.transpose(0, 1, 3, 2), b_re, b_im, c_re, c_im)


def _block_transpose8(ps):
    ps = list(ps)
    blk = lax.broadcasted_iota(jnp.int32, ps[0].shape, 1) // S5_GROUP
    for k in range(3):
        step = 1 << k
        shift = S5_GROUP * step
        keep = ((blk >> k) & 1) == 0
        for a in range(8):
            if a & step:
                continue
            pa, pb = ps[a], ps[a + step]
            ps[a] = jnp.where(keep, pa, pltpu.roll(pb, shift, axis=1))
            ps[a + step] = jnp.where(keep, pltpu.roll(pa, 128 - shift, axis=1), pb)
    return ps


def _s5a_kernel(u_ref, w1_ref, yin_ref, gf_ref, gb_ref):
    nb = u_ref.shape[1]
    xs = []
    for s in range(CHUNK):
        parts = [u_ref[0, b, pl.ds(s, TILE_CHUNKS, stride=CHUNK), :] for b in range(nb)]
        xs.append(jnp.concatenate(parts, axis=0))
    lo = _block_transpose8(xs[:8])
    hi = _block_transpose8(xs[8:])
    for j in range(GRP_PER_BLK):
        og = jnp.concatenate([lo[j], hi[j]], axis=1)
        r = _dot(og.astype(BF16), w1_ref[j])
        yin_ref[j] = r[:, 0:256].astype(yin_ref.dtype)
        gf_ref[j] = r[:, 256:384]
        gb_ref[j] = r[:, 384:512]


def _s5a_call(u, w1):
    n_blk, b, s, _ = u.shape
    nt = s // TILE_TOK
    rows = b * TILE_CHUNKS
    out_spec = lambda n: pl.BlockSpec((GRP_PER_BLK, rows, n), lambda l, j: (l, j, 0))
    return pl.pallas_call(
        _s5a_kernel, grid=(n_blk, nt),
        in_specs=[pl.BlockSpec((1, b, TILE_TOK, LANE_BLK), lambda l, j: (l, 0, j, 0)),
                  pl.BlockSpec((GRP_PER_BLK, 256, 512), lambda l, j: (l, 0, 0))],
        out_specs=[out_spec(256), out_spec(128), out_spec(128)],
        out_shape=[jax.ShapeDtypeStruct((S5_GROUPS, nt * rows, 256), BF16),
                   jax.ShapeDtypeStruct((S5_GROUPS, nt * rows, 128), F32),
                   jax.ShapeDtypeStruct((S5_GROUPS, nt * rows, 128), F32)],
        compiler_params=_cparams(("parallel", "parallel")),
        name="s5_chunk_in",
    )(u, w1)


def _s5b_kernel(nb, a_ref, gfc_ref, gfl_ref, gbc_ref, gbl_ref, hfc_ref, hfl_ref, hbc_ref, hbl_ref):
    gb = a_ref.shape[0]
    rows = nb * TILE_CHUNKS
    n_lat = gfl_ref.shape[1] // rows
    a1f = [jnp.broadcast_to(a_ref[g, 0:1, :], (nb, 128)) for g in range(gb)]
    a2f = [jnp.broadcast_to(a_ref[g, 1:2, :], (nb, 128)) for g in range(gb)]
    a1b = [jnp.broadcast_to(a_ref[g, 2:3, :], (nb, 128)) for g in range(gb)]
    a2b = [jnp.broadcast_to(a_ref[g, 3:4, :], (nb, 128)) for g in range(gb)]

    def step(state, a1, a2, g_ref, h_ref, g, row):
        h, hs = state
        h_ref[g, row, :] = h
        inp = g_ref[g, row, :]
        return a1 * h + a2 * hs + inp, a1 * hs - a2 * h + pltpu.roll(inp, 64, axis=1)

    zero = jnp.zeros((nb, 128), F32)
    hf = [(zero, zero) for _ in range(gb)]
    hb = [(zero, zero) for _ in range(gb)]
    for ci in range(TILE_CHUNKS):
        rf = pl.ds(ci, nb, stride=TILE_CHUNKS)
        rb = pl.ds(TILE_CHUNKS - 1 - ci, nb, stride=TILE_CHUNKS)
        for g in range(gb):
            hf[g] = step(hf[g], a1f[g], a2f[g], gfc_ref, hfc_ref, g, rf)
            hb[g] = step(hb[g], a1b[g], a2b[g], gbc_ref, hbc_ref, g, rb)

    def body(j, carry):
        hf, hb = carry
        hf = list(hf)
        hb = list(hb)
        base_f = j * rows
        base_b = (n_lat - 1 - j) * rows
        for ci in range(TILE_CHUNKS):
            rf = pl.ds(base_f + ci, nb, stride=TILE_CHUNKS)
            rb = pl.ds(base_b + (TILE_CHUNKS - 1 - ci), nb, stride=TILE_CHUNKS)
            for g in range(gb):
                hf[g] = step(hf[g], a1f[g], a2f[g], gfl_ref, hfl_ref, g, rf)
                hb[g] = step(hb[g], a1b[g], a2b[g], gbl_ref, hbl_ref, g, rb)
        return tuple(hf), tuple(hb)

    lax.fori_loop(0, n_lat, body, (tuple(hf), tuple(hb)))


def _s5b_call(decay, gf_c, gf_l, gb_c, gb_l, nb):
    gblk = 4
    spec = lambda a: pl.BlockSpec((gblk, a.shape[1], 128), lambda i: (i, 0, 0))
    sds = lambda a: jax.ShapeDtypeStruct(a.shape, F32)
    return pl.pallas_call(
        functools.partial(_s5b_kernel, nb), grid=(S5_GROUPS // gblk,),
        in_specs=[pl.BlockSpec((gblk, 4, 128), lambda i: (i, 0, 0)),
                  spec(gf_c), spec(gf_l), spec(gb_c), spec(gb_l)],
        out_specs=[spec(gf_c), spec(gf_l), spec(gb_c), spec(gb_l)],
        out_shape=[sds(gf_c), sds(gf_l), sds(gb_c), sds(gb_l)],
        compiler_params=_cparams(("parallel",)),
        name="s5_state_scan",
    )(decay, gf_c, gf_l, gb_c, gb_l)


def _s5c_kernel(yin_ref, hf_ref, hb_ref, e_ref, u_ref, d_ref, y_ref):
    nb = u_ref.shape[1]
    ys = []
    for j in range(GRP_PER_BLK):
        h = jnp.concatenate([hf_ref[j], hb_ref[j]], axis=1).astype(BF16)
        ys.append(yin_ref[j].astype(F32) + _dot(h, e_ref[j]))
    at = (_block_transpose8([y[:, :128] for y in ys])
          + _block_transpose8([y[:, 128:] for y in ys]))
    d = d_ref[...]
    for t in range(CHUNK):
        for b in range(nb):
            rows = pl.ds(t, TILE_CHUNKS, stride=CHUNK)
            y_ref[0, b, rows, :] = at[t][b * TILE_CHUNKS:(b + 1) * TILE_CHUNKS] + d * u_ref[0, b, rows, :]


def _s5c_call(yin, hf, hb, e, u, d_skip):
    n_blk, b, s, _ = u.shape
    nt = s // TILE_TOK
    rows = b * TILE_CHUNKS
    gspec = lambda n: pl.BlockSpec((GRP_PER_BLK, rows, n), lambda l, j: (l, j, 0))
    tok_spec = pl.BlockSpec((1, b, TILE_TOK, LANE_BLK), lambda l, j: (l, 0, j, 0))
    return pl.pallas_call(
        _s5c_kernel, grid=(n_blk, nt),
        in_specs=[gspec(256), gspec(128), gspec(128),
                  pl.BlockSpec((GRP_PER_BLK, 256, 256), lambda l, j: (l, 0, 0)),
                  tok_spec,
                  pl.BlockSpec((1, LANE_BLK), lambda l, j: (0, l))],
        out_specs=tok_spec,
        out_shape=_lane_block_shape(b, s),
        compiler_params=_cparams(("parallel", "parallel")),
        name="s5_chunk_out",
    )(yin, hf, hb, e, u, d_skip.reshape(1, D_S5))


def _gelu_tanh(x):
    return 0.5 * x * (1.0 + jnp.tanh(math.sqrt(2.0 / math.pi) * (x + 0.044715 * (x * x * x))))


def _route(logits):
    lane = lax.broadcasted_iota(jnp.int32, logits.shape, 1).astype(F32)
    neg = jnp.float32(-1e30)
    big = jnp.float32(1e9)
    gl = jnp.where(lane < N_GROUPS, logits, neg)
    gmax = jnp.max(gl, axis=1, keepdims=True)
    gidx = jnp.min(jnp.where(gl == gmax, lane, big), axis=1, keepdims=True)
    gsum = jnp.sum(jnp.exp(gl - gmax), axis=1, keepdims=True)
    gw = 1.0 / gsum
    lo = N_GROUPS + EXP_PER_GROUP * gidx
    el = jnp.where((lane >= lo) & (lane < lo + EXP_PER_GROUP), logits, neg)
    v1 = jnp.max(el, axis=1, keepdims=True)
    i1 = jnp.min(jnp.where(el == v1, lane, big), axis=1, keepdims=True)
    el2 = jnp.where(lane == i1, neg, el)
    v2 = jnp.max(el2, axis=1, keepdims=True)
    i2 = jnp.min(jnp.where(el2 == v2, lane, big), axis=1, keepdims=True)
    ex = jnp.exp(v2 - v1)
    p1 = 1.0 / (1.0 + ex)
    p2 = ex * p1
    e1 = i1 - lo
    e2 = i2 - lo
    first = e1 < e2
    ea = jnp.where(first, e1, e2)
    eb = jnp.where(first, e2, e1)
    wa = gw * jnp.where(first, p1, p2)
    wb = gw * jnp.where(first, p2, p1)
    pair = ea * (7.0 - ea) * 0.5 + (eb - ea - 1.0)
    return wa, wb, 6.0 * gidx + pair


def _out_kernel(ypre_ref, ysc_ref, ycf_ref, x_ref, mod_ref, wglu_ref, bglu_ref, wo_ref,
                lng_ref, lnb_ref, wr_ref, br_ref, cnt0_ref, x1_ref, hx_ref, meta_ref, counts_ref, *rest):
    stage_ref = rest[0] if len(rest) == 2 else None
    cnt_ref = rest[-1]

    @pl.when((pl.program_id(0) == 0) & (pl.program_id(1) == 0))
    def _():
        cnt_ref[...] = cnt0_ref[...]

    tm = x_ref.shape[1]
    sub = tm // OUT_SUBTILES

    def row_chain(r0):
        rows = slice(r0, r0 + sub)
        ypre = jnp.concatenate([ypre_ref[blk, 0, rows, :] for blk in range(ypre_ref.shape[0])], axis=1)
        t = _gelu_tanh(ypre)
        gate = _sigmoid(_dot(t.astype(BF16), wglu_ref[...]) + bglu_ref[...])
        ys5 = (t * gate).astype(BF16)
        y = (_dot(ys5, wo_ref[0:D_S5, :]) + _dot(ysc_ref[0, rows, :], wo_ref[D_S5:D_S5 + D_SC, :])
             + _dot(ycf_ref[0, rows, :], wo_ref[D_S5 + D_SC:D_MODEL, :]))
        g1 = mod_ref[0, 2:3, :]
        x1 = _layer_norm(DN_ALPHA * x_ref[0, rows, :] + g1 * y, lng_ref[...], lnb_ref[...])
        x1_ref[0, rows, :] = x1
        h2 = x1 * (1.0 + mod_ref[0, 4:5, :]) + mod_ref[0, 3:4, :]
        hx_ref[0, rows, 0:D_MODEL] = h2
        return _route(_dot(h2.astype(BF16), wr_ref[...]) + br_ref[...])

    routed = [row_chain(r0) for r0 in range(0, tm, sub)]
    wa, wb, cls = (jnp.concatenate([r[i] for r in routed], axis=0) for i in range(3))

    lane = lax.broadcasted_iota(jnp.int32, (tm, ROUTER_LANES), 1).astype(F32)
    onehot = jnp.where(lane == cls, 1.0, 0.0)
    row_i = lax.broadcasted_iota(jnp.int32, (tm, tm), 0)
    col_i = lax.broadcasted_iota(jnp.int32, (tm, tm), 1)
    earlier = jnp.where(col_i < row_i, 1.0, 0.0).astype(BF16)
    before = _dot(earlier, onehot.astype(BF16)) + cnt_ref[...]
    rank = jnp.sum(before * onehot, axis=1, keepdims=True)
    cnt_ref[...] += jnp.sum(onehot, axis=0, keepdims=True)
    counts_ref[...] = cnt_ref[...]

    meta = (jnp.where(lane == META_WA, wa, 0.0) + jnp.where(lane == META_WB, wb, 0.0)
            + jnp.where(lane == META_CLS, cls, 0.0) + jnp.where(lane == META_RANK, rank, 0.0))
    meta_ref[...] = jnp.transpose(meta)[0:SUBLANES, :]
    hx_ref[0, :, D_MODEL:HX_LANES] = meta
    if stage_ref is not None:
        stage_ref[...] = jnp.zeros_like(stage_ref)


def _out_call(ypre, ysc, ycf, x, mod, wglu_bf, b_glu, wo_bf, ln_g, ln_b, w_router, b_router, counts0, tm,
              stage_rows):
    b, s, d = x.shape
    nt = s // tm
    row_spec = lambda n: pl.BlockSpec((1, tm, n), lambda i, j: (i, j, 0))
    full = lambda shape: pl.BlockSpec(shape, lambda i, j: (0,) * len(shape))
    out_specs = [row_spec(d), row_spec(HX_LANES),
                 pl.BlockSpec((SUBLANES, tm), lambda i, j: (0, i * nt + j)),
                 full((1, ROUTER_LANES))]
    out_shape = [jax.ShapeDtypeStruct((b, s, d), F32), jax.ShapeDtypeStruct((b, s, HX_LANES), F32),
                 jax.ShapeDtypeStruct((SUBLANES, b * s), F32),
                 jax.ShapeDtypeStruct((1, ROUTER_LANES), F32)]
    if stage_rows:
        stage_octs = stage_rows // (SUBLANES * b * nt)
        assert stage_octs * SUBLANES * b * nt == stage_rows
        out_specs.append(pl.BlockSpec((stage_octs, SUBLANES, HX_LANES), lambda i, j: (i * nt + j, 0, 0)))
        out_shape.append(jax.ShapeDtypeStruct((stage_rows // SUBLANES, SUBLANES, HX_LANES), F32))
    return pl.pallas_call(
        _out_kernel, grid=(b, nt),
        in_specs=[_lane_block_spec(tm), row_spec(D_SC), row_spec(D_CF), row_spec(d),
                  pl.BlockSpec((1, 6, d), lambda i, j: (i, 0, 0)),
                  full((D_S5, D_S5)), full((1, D_S5)), full((d, d)),
                  full((1, d)), full((1, d)), full((d, ROUTER_LANES)), full((1, ROUTER_LANES)),
                  full((1, ROUTER_LANES))],
        out_specs=out_specs, out_shape=out_shape,
        scratch_shapes=[pltpu.VMEM((1, ROUTER_LANES), F32)],
        compiler_params=_cparams(("arbitrary", "arbitrary")),
        name="out_proj",
    )(ypre, ysc, ycf, x, mod, wglu_bf, b_glu.reshape(1, -1), wo_bf, ln_g.reshape(1, -1),
      ln_b.reshape(1, -1), w_router, b_router, counts0)


def _sorted_rows(n_tok):
    return n_tok + N_CLASSES * MOE_TM


def _moe_plan(meta, counts, n_tok):
    cls = meta[META_CLS].astype(jnp.int32)
    rank = meta[META_RANK].astype(jnp.int32)
    cnt = counts[0, :N_CLASSES].astype(jnp.int32)
    n_tiles = (cnt + (MOE_TM - 1)) // MOE_TM
    ends = jnp.cumsum(n_tiles)
    starts = ends - n_tiles
    slot = starts[cls] * MOE_TM + rank
    t_max = n_tok // MOE_TM + N_CLASSES
    n_used = ends[N_CLASSES - 1]
    tile = jnp.minimum(jnp.arange(t_max, dtype=jnp.int32), n_used - 1)
    tile_cls = jnp.sum((tile[:, None] >= ends[None, :]).astype(jnp.int32), axis=1)
    group = tile_cls // 6
    pair = tile_cls % 6
    first = jnp.array([0, 0, 0, 1, 1, 2], jnp.int32)[pair] + EXP_PER_GROUP * group
    second = jnp.array([1, 2, 3, 2, 3, 3], jnp.int32)[pair] + EXP_PER_GROUP * group
    new_cls = jnp.concatenate([jnp.ones((1,), jnp.int32), (tile_cls[1:] != tile_cls[:-1]).astype(jnp.int32)])
    ctl = jnp.concatenate([n_used.reshape(1), new_cls])
    return slot, tile, first, second, ctl


def _split_row(row):
    return lax.shift_right_logical(row, 3), lax.bitwise_and(row, SUBLANES - 1)


def _dispatch_kernel(slot_ref, hx_ref, xs_init_ref, xs_ref, sem):
    del xs_init_ref
    n_oct = hx_ref.shape[1]
    base = (pl.program_id(0) * pl.num_programs(1) + pl.program_id(1)) * (n_oct * SUBLANES)

    def body(i, carry):
        for k in range(SUBLANES):
            oct_id, sub = _split_row(slot_ref[base + i * SUBLANES + k])
            pltpu.make_async_copy(hx_ref.at[0, i, pl.ds(k, 1), :], xs_ref.at[oct_id, pl.ds(sub, 1), :],
                                  sem).start(priority=k % 2)
        return carry

    lax.fori_loop(0, n_oct, body, 0)
    pltpu.make_async_copy(hx_ref.at[0], xs_ref.at[pl.ds(0, n_oct)], sem).wait()


def _dispatch_call(slot, hx, xs_init, tm):
    b, s, w = hx.shape
    n_rows = xs_init.shape[0] * SUBLANES
    grid_spec = pltpu.PrefetchScalarGridSpec(
        num_scalar_prefetch=1, grid=(b, s // tm),
        in_specs=[pl.BlockSpec((1, tm // SUBLANES, SUBLANES, w), lambda i, j, slot: (i, j, 0, 0)),
                  pl.BlockSpec(memory_space=pl.ANY)],
        out_specs=pl.BlockSpec(memory_space=pl.ANY),
        scratch_shapes=[pltpu.SemaphoreType.DMA(())])
    xs = pl.pallas_call(
        _dispatch_kernel, grid_spec=grid_spec,
        out_shape=jax.ShapeDtypeStruct(xs_init.shape, F32),
        input_output_aliases={2: 0},
        compiler_params=_cparams(("arbitrary", "arbitrary")),
        name="moe_dispatch",
    )(slot, hx.reshape(b, s // SUBLANES, SUBLANES, w), xs_init)
    return xs.reshape(n_rows, w)


def _moe_kernel(tile_ref, first_ref, second_ref, ctl_ref, xs_ref, wga_ref, wgb_ref, wua_ref, wub_ref,
                wda_ref, wdb_ref, ys_ref, *wbf):
    del tile_ref, first_ref, second_ref
    t = pl.program_id(0)
    n_used = ctl_ref[0]

    @pl.when((t < n_used) & (ctl_ref[1 + t] == 1))
    def _():
        for src, dst in zip((wga_ref, wgb_ref, wua_ref, wub_ref, wda_ref, wdb_ref), wbf):
            dst[...] = src[0].astype(BF16)

    @pl.when(t < n_used)
    def _():
        x = xs_ref[...]
        xb = x[:, 0:D_MODEL].astype(BF16)

        def expert(wg_ref, wu_ref, wd_ref, w):
            gate = _dot(xb, wg_ref[...])
            up = _dot(xb, wu_ref[...])
            act = gate * _sigmoid(gate) * up * w
            return _dot(act.astype(BF16), wd_ref[...])

        wa = x[:, D_MODEL + META_WA:D_MODEL + META_WA + 1]
        wb = x[:, D_MODEL + META_WB:D_MODEL + META_WB + 1]
        ys_ref[...] = expert(wbf[0], wbf[2], wbf[4], wa) + expert(wbf[1], wbf[3], wbf[5], wb)

    @pl.when(t >= n_used)
    def _():
        ys_ref[...] = jnp.zeros_like(ys_ref)


def _moe_call(tile, first, second, ctl, xs, w_gate, w_up, w_down):
    n_rows, w = xs.shape
    d = D_MODEL
    t_max = tile.shape[0]
    up_spec = lambda sel: pl.BlockSpec((1, d, D_EXPERT), lambda t, tl, fi, se, nu: ((fi, se)[sel][t], 0, 0))
    down_spec = lambda sel: pl.BlockSpec((1, D_EXPERT, d), lambda t, tl, fi, se, nu: ((fi, se)[sel][t], 0, 0))
    grid_spec = pltpu.PrefetchScalarGridSpec(
        num_scalar_prefetch=4, grid=(t_max,),
        in_specs=[pl.BlockSpec((MOE_TM, w), lambda t, tl, fi, se, nu: (tl[t], 0)),
                  up_spec(0), up_spec(1), up_spec(0), up_spec(1), down_spec(0), down_spec(1)],
        out_specs=pl.BlockSpec((MOE_TM, d), lambda t, tl, fi, se, nu: (t, 0)),
        scratch_shapes=[pltpu.VMEM((d, D_EXPERT), BF16)] * 4 + [pltpu.VMEM((D_EXPERT, d), BF16)] * 2)
    return pl.pallas_call(
        _moe_kernel, grid_spec=grid_spec,
        out_shape=jax.ShapeDtypeStruct((n_rows, d), F32),
        compiler_params=_cparams(("arbitrary",)),
        name="moe_experts",
    )(tile, first, second, ctl, xs, w_gate, w_gate, w_up, w_up, w_down, w_down)


def _combine_kernel(slot_ref, x1_ref, mod_ref, lng_ref, lnb_ref, ys_ref, o_ref, f_ref, sem):
    o_ref[0] = _combine_rows(slot_ref, x1_ref, mod_ref, lng_ref, lnb_ref, ys_ref, f_ref, sem)


def _combine_in_kernel(slot_ref, x1_ref, mod_ref, lng_ref, lnb_ref, ys_ref, mod_next_ref, w_ref,
                       o_ref, u_ref, bg_ref, cv_ref, gl_ref, f_ref, sem):
    x2 = _combine_rows(slot_ref, x1_ref, mod_ref, lng_ref, lnb_ref, ys_ref, f_ref, sem)
    o_ref[0] = x2
    _in_body(x2, mod_next_ref, w_ref, u_ref, bg_ref, cv_ref, gl_ref)


def _combine_rows(slot_ref, x1_ref, mod_ref, lng_ref, lnb_ref, ys_ref, f_ref, sem):
    n_oct = f_ref.shape[1]
    tm = n_oct * SUBLANES
    step = pl.program_id(0) * pl.num_programs(1) + pl.program_id(1)
    n_steps = pl.num_programs(0) * pl.num_programs(1)

    def request(which, buf):
        base = which * tm

        def body(i, carry):
            for k in range(SUBLANES):
                oct_id, sub = _split_row(slot_ref[base + i * SUBLANES + k])
                pltpu.make_async_copy(ys_ref.at[oct_id, pl.ds(sub, 1), :], f_ref.at[buf, i, pl.ds(k, 1), :],
                                      sem.at[buf]).start(priority=k % 2)
            return carry

        lax.fori_loop(0, n_oct, body, 0)

    @pl.when(step == 0)
    def _():
        request(0, 0)

    @pl.when(step + 1 < n_steps)
    def _():
        request(step + 1, (step + 1) % 2)

    buf = step % 2
    pltpu.make_async_copy(ys_ref.at[pl.ds(0, n_oct)], f_ref.at[buf], sem.at[buf]).wait()
    f = f_ref[buf].reshape(tm, f_ref.shape[3])
    g2 = mod_ref[0, 5:6, :]
    return _layer_norm(DN_ALPHA * x1_ref[0] + g2 * f, lng_ref[...], lnb_ref[...])


def _combine_call(slot, x1, mod, ln_g, ln_b, ys, tm, next_in=None):
    b, s, d = x1.shape
    ys = ys.reshape(ys.shape[0] // SUBLANES, SUBLANES, d)
    row_spec = lambda n: pl.BlockSpec((1, tm, n), lambda i, j, slot: (i, j, 0))
    mod_spec = pl.BlockSpec((1, 6, d), lambda i, j, slot: (i, 0, 0))
    full = lambda shape: pl.BlockSpec(shape, lambda i, j, slot: (0,) * len(shape))
    in_specs = [row_spec(d), mod_spec, full((1, d)), full((1, d)), pl.BlockSpec(memory_space=pl.ANY)]
    out_specs = [row_spec(d)]
    out_shape = [jax.ShapeDtypeStruct((b, s, d), F32)]
    args = [slot, x1, mod, ln_g.reshape(1, -1), ln_b.reshape(1, -1), ys]
    kern, name = _combine_kernel, "moe_combine"
    if next_in is not None:
        kern, name = _combine_in_kernel, "moe_combine_in_proj"
        in_specs += [mod_spec, full((d, D_IN))]
        args += list(next_in)
        out_specs += [pl.BlockSpec((D_S5 // LANE_BLK, 1, tm, LANE_BLK), lambda i, j, slot: (0, i, j, 0)),
                      row_spec(D_SC), row_spec(D_SC), row_spec(D_CF)]
        out_shape += [_lane_block_shape(b, s)] + [jax.ShapeDtypeStruct((b, s, D_SC), F32)] * 3
    grid_spec = pltpu.PrefetchScalarGridSpec(
        num_scalar_prefetch=1, grid=(b, s // tm), in_specs=in_specs, out_specs=out_specs,
        scratch_shapes=[pltpu.VMEM((2, tm // SUBLANES, SUBLANES, d), F32), pltpu.SemaphoreType.DMA((2,))])
    outs = pl.pallas_call(
        kern, grid_spec=grid_spec, out_shape=out_shape,
        compiler_params=_cparams(("arbitrary", "arbitrary")),
        name=name,
    )(*args)
    return outs[0] if next_in is None else outs


def _moe_sublayer(parts, counts, xs_init, w_gate, w_up, w_down, ln_g, ln_b, next_in=None):
    sizes = [p[2].shape[0] * p[2].shape[1] for p in parts]
    meta = jnp.concatenate([p[1] for p in parts], axis=1)
    slot, tile, first, second, ctl = _moe_plan(meta, counts, sum(sizes))
    slots, start = [], 0
    for n in sizes:
        slots.append(slot[start:start + n])
        start += n
    xs = xs_init
    for (hx, _, _, _, tm), sl in zip(parts, slots):
        xs = _dispatch_call(sl, hx, xs.reshape(xs_init.shape), tm)
    ys = _moe_call(tile, first, second, ctl, xs, w_gate, w_up, w_down)
    outs = []
    for idx, ((_, _, x1, mod, tm), sl) in enumerate(zip(parts, slots)):
        fuse = next_in if idx == len(parts) - 1 else None
        outs.append(_combine_call(sl, x1, mod, ln_g, ln_b, ys, tm, fuse))
    return outs


def kernel(x, c, ctx, c_ctx, w_mod, b_mod, w_in, s5_a_re, s5_a_im, s5_log_dt, s5_b_re, s5_b_im, s5_c_re, s5_c_im, s5_d, w_glu, b_glu, w_sc, w_dw, b_dw, ln_cf_g, ln_cf_b, w_o, ln1_g, ln1_b, w_rg, b_rg, w_rexp, b_rexp, w_gate, w_up, w_down, ln2_g, ln2_b):
    nb, seq, d = x.shape
    n_ctx = ctx.shape[1]
    n_layers = w_mod.shape[0]
    assert seq % TILE_TOK == 0 and n_ctx % TILE_TOK == 0 and seq % GRID_W == 0

    mod_rows = 16
    assert nb + 1 <= mod_rows
    c_all = jnp.concatenate([c, c_ctx[None, :], jnp.zeros((mod_rows - nb - 1, d), F32)], axis=0)
    mod_all = _mod_call(c_all, w_mod, b_mod)

    pad_r = ROUTER_LANES - N_GROUPS - N_EXPERTS
    x_lat, x_ctx = x, ctx
    mods_lat = [mod_all[l, :nb].reshape(nb, 6, d) for l in range(n_layers)]
    w_in_bfs = [w_in[l].astype(BF16) for l in range(n_layers)]
    lat_proj = None
    for l in range(n_layers):
        last = l == n_layers - 1
        mod_lat = mods_lat[l]
        mod_ctx = jnp.broadcast_to(mod_all[l, nb].reshape(1, 6, d), (nb, 6, d))
        w_in_bf = w_in_bfs[l]
        wglu_bf = w_glu[l].astype(BF16)
        wo_bf = w_o[l].astype(BF16)
        w_router = jnp.concatenate([w_rg[l], w_rexp[l], jnp.zeros((d, pad_r), F32)], axis=1).astype(BF16)
        b_router = jnp.concatenate([b_rg[l], b_rexp[l], jnp.zeros((pad_r,), F32)]).reshape(1, -1)
        w1, e_op, decay = _s5_operators(s5_a_re[l], s5_a_im[l], s5_log_dt[l], s5_b_re[l], s5_b_im[l],
                                        s5_c_re[l], s5_c_im[l])

        if lat_proj is None:
            u_l, bg_l, cv_l, gl_l = _in_call(x_lat, mod_lat, w_in_bf, 512, False)
        else:
            u_l, bg_l, cv_l, gl_l = lat_proj
        if last:
            u_c = _in_call(x_ctx, mod_ctx, w_in_bf[:, :D_S5], TILE_TOK, True)
        else:
            u_c, bg_c, cv_c, gl_c = _in_call(x_ctx, mod_ctx, w_in_bf, TILE_TOK, False)

        yin_l, gf_l, gb_l = _s5a_call(u_l, w1)
        yin_c, gf_c, gb_c = _s5a_call(u_c, w1)
        hf_c, hf_l, hb_c, hb_l = _s5b_call(decay, gf_c, gf_l, gb_c, gb_l, nb)
        ypre_l = _s5c_call(yin_l, hf_l, hb_l, e_op, u_l, s5_d[l])

        ysc_l, ycf_l = _conv_call(bg_l, cv_l, gl_l, w_sc[l], w_dw[l], b_dw[l], ln_cf_g[l], ln_cf_b[l], True)
        parts = []
        counts = jnp.zeros((1, ROUTER_LANES), F32)
        n_moe = nb * seq
        if not last:
            ypre_c = _s5c_call(yin_c, hf_c, hb_c, e_op, u_c, s5_d[l])
            ysc_c, ycf_c = _conv_call(bg_c, cv_c, gl_c, w_sc[l], w_dw[l], b_dw[l], ln_cf_g[l], ln_cf_b[l], False)
            x1_c, hx_c, meta_c, counts = _out_call(ypre_c, ysc_c, ycf_c, x_ctx, mod_ctx, wglu_bf, b_glu[l], wo_bf,
                                                   ln1_g[l], ln1_b[l], w_router, b_router, counts, TILE_TOK, 0)
            parts.append((hx_c, meta_c, x1_c, mod_ctx, TILE_TOK))
            n_moe += nb * n_ctx
        x1_l, hx_l, meta_l, counts, stage = _out_call(ypre_l, ysc_l, ycf_l, x_lat, mod_lat, wglu_bf, b_glu[l], wo_bf,
                                                      ln1_g[l], ln1_b[l], w_router, b_router, counts, 512,
                                                      _sorted_rows(n_moe))
        parts.append((hx_l, meta_l, x1_l, mod_lat, 512))
        next_in = None if last else (mods_lat[l + 1], w_in_bfs[l + 1])
        outs = _moe_sublayer(parts, counts, stage, w_gate[l], w_up[l], w_down[l], ln2_g[l], ln2_b[l], next_in)
        if last:
            x_lat = outs[-1]
        else:
            x_ctx = outs[0]
            x_lat, lat_proj = outs[-1][0], outs[-1][1:]
    return x_lat
```

```python
import functools
import math

import jax
import jax.numpy as jnp
from jax import lax
from jax.experimental import pallas as pl
from jax.experimental.pallas import tpu as pltpu

F32 = jnp.float32
BF16 = jnp.bfloat16

D_MODEL = 1024
DEPTH = 2
GRID_W = 64
D_S5 = 512
S5_GROUP = 16
S5_GROUPS = 32
S5_STATE = 64
D_SC = 256
D_CF = 256
CF_WIDTH = 31
CF_HALF = 15
D_IN = 1792
N_GROUPS = 4
EXP_PER_GROUP = 4
N_EXPERTS = 16
D_EXPERT = 256
DN_ALPHA = (2 * DEPTH) ** 0.25
LN_EPS = 1e-5

CHUNK = 16
N_POW = 32
PREP_GROUPS = 8
TILE_CHUNKS = 16
TILE_TOK = CHUNK * TILE_CHUNKS
LANE_BLK = 128
GRP_PER_BLK = LANE_BLK // S5_GROUP
ROUTER_LANES = 128
HX_LANES = D_MODEL + ROUTER_LANES
META_WA, META_WB, META_CLS, META_RANK = 0, 1, 2, 3
N_CLASSES = N_GROUPS * 6
MOE_TM = 512
OUT_SUBTILES = 2
SUBLANES = 8
VMEM_LIMIT = 56 * 1024 * 1024


def _cparams(sem):
    return pltpu.CompilerParams(dimension_semantics=sem, vmem_limit_bytes=VMEM_LIMIT)


def _split_bf16(a):
    hi = a.astype(BF16)
    lo = (a - hi.astype(F32)).astype(BF16)
    return hi, lo


def _dot(a, b):
    return jnp.dot(a, b, preferred_element_type=F32)


def _dot3(a, b):
    ah, al = _split_bf16(a)
    bh, bl = _split_bf16(b)
    return _dot(ah, bh) + (_dot(al, bh) + _dot(ah, bl))


def _sigmoid(x):
    return 1.0 / (1.0 + jnp.exp(-x))


def _layer_norm(x, g, b):
    mu = jnp.mean(x, axis=-1, keepdims=True)
    xc = x - mu
    var = jnp.mean(xc * xc, axis=-1, keepdims=True)
    return xc * lax.rsqrt(var + LN_EPS) * g + b


def _mod_kernel(c_ref, w_ref, b_ref, o_ref):
    c = c_ref[...]
    s = c * _sigmoid(c)
    o_ref[0] = _dot3(s, w_ref[0]) + b_ref[0]


def _mod_call(c_all, w_mod, b_mod):
    n_layers, d, n_out = w_mod.shape
    tn = 1536
    rows = c_all.shape[0]
    return pl.pallas_call(
        _mod_kernel,
        grid=(n_layers, n_out // tn),
        in_specs=[
            pl.BlockSpec((rows, d), lambda l, j: (0, 0)),
            pl.BlockSpec((1, d, tn), lambda l, j: (l, 0, j)),
            pl.BlockSpec((1, 1, tn), lambda l, j: (l, 0, j)),
        ],
        out_specs=pl.BlockSpec((1, rows, tn), lambda l, j: (l, 0, j)),
        out_shape=jax.ShapeDtypeStruct((n_layers, rows, n_out), F32),
        compiler_params=_cparams(("parallel", "parallel")),
        name="mod",
    )(c_all, w_mod, b_mod.reshape(n_layers, 1, n_out))


def _in_body(x, mod_ref, w_ref, u_ref, bg_ref, cv_ref, gl_ref):
    sh = mod_ref[0, 0:1, :]
    sc = mod_ref[0, 1:2, :]
    h = (x * (1.0 + sc) + sh).astype(BF16)
    z = _dot(h, w_ref[...])
    _store_lane_blocks(u_ref, z[:, 0:512])
    bg_ref[0] = z[:, 512:768]
    cv_ref[0] = z[:, 768:1024] * z[:, 1024:1280]
    gl_ref[0] = z[:, 1280:1536] * _sigmoid(z[:, 1536:1792])


def _in_kernel(x_ref, mod_ref, w_ref, u_ref, bg_ref, cv_ref, gl_ref):
    _in_body(x_ref[0], mod_ref, w_ref, u_ref, bg_ref, cv_ref, gl_ref)


def _in_u_kernel(x_ref, mod_ref, w_ref, u_ref):
    x = x_ref[0]
    sh = mod_ref[0, 0:1, :]
    sc = mod_ref[0, 1:2, :]
    h = (x * (1.0 + sc) + sh).astype(BF16)
    _store_lane_blocks(u_ref, _dot(h, w_ref[...]))


def _store_lane_blocks(ref, val):
    for blk in range(ref.shape[0]):
        ref[blk, 0] = val[:, blk * LANE_BLK:(blk + 1) * LANE_BLK]


def _lane_block_spec(tm):
    return pl.BlockSpec((D_S5 // LANE_BLK, 1, tm, LANE_BLK), lambda i, j: (0, i, j, 0))


def _lane_block_shape(b, s):
    return jax.ShapeDtypeStruct((D_S5 // LANE_BLK, b, s, LANE_BLK), F32)


def _in_call(x, mod, w_in_bf, tm, u_only):
    b, s, d = x.shape
    grid = (b, s // tm)
    row_spec = lambda n: pl.BlockSpec((1, tm, n), lambda i, j: (i, j, 0))
    in_specs = [
        row_spec(d),
        pl.BlockSpec((1, 6, d), lambda i, j: (i, 0, 0)),
    ]
    if u_only:
        in_specs.append(pl.BlockSpec((d, D_S5), lambda i, j: (0, 0)))
        return pl.pallas_call(
            _in_u_kernel, grid=grid, in_specs=in_specs,
            out_specs=_lane_block_spec(tm),
            out_shape=_lane_block_shape(b, s),
            compiler_params=_cparams(("parallel", "parallel")),
            name="in_proj_u",
        )(x, mod, w_in_bf)
    in_specs.append(pl.BlockSpec((d, D_IN), lambda i, j: (0, 0)))
    return pl.pallas_call(
        _in_kernel, grid=grid, in_specs=in_specs,
        out_specs=[_lane_block_spec(tm), row_spec(D_SC), row_spec(D_SC), row_spec(D_CF)],
        out_shape=[_lane_block_shape(b, s),
                   jax.ShapeDtypeStruct((b, s, D_SC), F32),
                   jax.ShapeDtypeStruct((b, s, D_SC), F32),
                   jax.ShapeDtypeStruct((b, s, D_CF), F32)],
        compiler_params=_cparams(("parallel", "parallel")),
        name="in_proj",
    )(x, mod, w_in_bf)


def _conv_tail(t, bdw_ref, lng_ref, lnb_ref):
    t = t + bdw_ref[...]
    t = _layer_norm(t, lng_ref[...], lnb_ref[...])
    return t * _sigmoid(t)


def _conv_grid_kernel(bg_ref, cv_ref, gl_ref, wsc_ref, wdw_ref, bdw_ref, lng_ref, lnb_ref,
                      ysc_ref, ycf_ref, grid_ref, t_ref):
    s = cv_ref.shape[1]
    rows = s // GRID_W
    cv = cv_ref[0]
    col = lax.broadcasted_iota(jnp.int32, (s, D_SC), 0) % GRID_W
    prev = jnp.where(col == 0, 0.0, pltpu.roll(cv, 1, axis=0))
    nxt = jnp.where(col == GRID_W - 1, 0.0, pltpu.roll(cv, s - 1, axis=0))
    conv = prev * wsc_ref[0:1, :] + cv * wsc_ref[1:2, :] + nxt * wsc_ref[2:3, :]
    ysc_ref[0] = (bg_ref[0] * conv).astype(ysc_ref.dtype)

    grid_ref[...] = gl_ref[0].reshape(rows, GRID_W, D_CF)

    def body(i, carry):
        w0 = pl.multiple_of(i * 8, 8)
        for half in range(D_CF // 128):
            lanes = slice(half * 128, (half + 1) * 128)
            acc = jnp.zeros((rows, 8, 128), F32)
            for k in range(CF_WIDTH):
                lo = max(0, CF_HALF - k)
                hi = min(rows, rows + CF_HALF - k)
                if hi <= lo:
                    continue
                term = grid_ref[lo + k - CF_HALF:hi + k - CF_HALF, pl.ds(w0, 8), lanes] * wdw_ref[k:k + 1, lanes]
                pieces = [acc[:lo]] * (lo > 0) + [acc[lo:hi] + term] + [acc[hi:]] * (hi < rows)
                acc = jnp.concatenate(pieces, axis=0) if len(pieces) > 1 else pieces[0]
            t_ref[:, pl.ds(w0, 8), lanes] = acc
        return carry

    lax.fori_loop(0, GRID_W // 8, body, 0)
    t = t_ref[...].reshape(s, D_CF)
    ycf_ref[0] = _conv_tail(t, bdw_ref, lng_ref, lnb_ref).astype(ycf_ref.dtype)


def _conv_seq_kernel(bg_ref, cv_ref, gl_ref, wsc_ref, wdw_ref, bdw_ref, lng_ref, lnb_ref,
                     ysc_ref, ycf_ref, pad_ref):
    s = cv_ref.shape[1]
    cv = cv_ref[0]
    pos = lax.broadcasted_iota(jnp.int32, (s, D_SC), 0)
    prev = jnp.where(pos == 0, 0.0, pltpu.roll(cv, 1, axis=0))
    nxt = jnp.where(pos == s - 1, 0.0, pltpu.roll(cv, s - 1, axis=0))
    conv = prev * wsc_ref[0:1, :] + cv * wsc_ref[1:2, :] + nxt * wsc_ref[2:3, :]
    ysc_ref[0] = (bg_ref[0] * conv).astype(ysc_ref.dtype)

    off = 16
    pad_ref[0:off] = jnp.zeros((off, D_CF), F32)
    pad_ref[off + s:off + s + 16] = jnp.zeros((16, D_CF), F32)
    pad_ref[off:off + s] = gl_ref[0]
    acc = jnp.zeros((s, D_CF), F32)
    for k in range(CF_WIDTH):
        acc = acc + pad_ref[pl.ds(off - CF_HALF + k, s), :] * wdw_ref[k:k + 1, :]
    ycf_ref[0] = _conv_tail(acc, bdw_ref, lng_ref, lnb_ref).astype(ycf_ref.dtype)


def _conv_call(bg, cv, gl, w_sc, w_dw, b_dw, ln_g, ln_b, grid_mode):
    b, s, _ = bg.shape
    row_spec = pl.BlockSpec((1, s, D_SC), lambda i: (i, 0, 0))
    full = lambda shape: pl.BlockSpec(shape, lambda i: (0,) * len(shape))
    if grid_mode:
        rows = s // GRID_W
        kern = _conv_grid_kernel
        scratch = [pltpu.VMEM((rows, GRID_W, D_CF), F32), pltpu.VMEM((rows, GRID_W, D_CF), F32)]
        name = "conv_grid"
    else:
        kern = _conv_seq_kernel
        scratch = [pltpu.VMEM((s + 32, D_CF), F32)]
        name = "conv_seq"
    return pl.pallas_call(
        kern, grid=(b,),
        in_specs=[row_spec, row_spec, row_spec, full((3, D_SC)), full((CF_WIDTH, D_CF)),
                  full((1, D_CF)), full((1, D_CF)), full((1, D_CF))],
        out_specs=[row_spec, row_spec],
        out_shape=[jax.ShapeDtypeStruct((b, s, D_SC), BF16), jax.ShapeDtypeStruct((b, s, D_CF), BF16)],
        scratch_shapes=scratch,
        compiler_params=_cparams(("parallel",)),
        name=name,
    )(bg, cv, gl, w_sc, w_dw, b_dw.reshape(1, -1), ln_g.reshape(1, -1), ln_b.reshape(1, -1))


def _split3(a):
    hi = a.astype(BF16)
    r = a - hi.astype(F32)
    mid = r.astype(BF16)
    lo = (r - mid.astype(F32)).astype(BF16)
    return hi, mid, lo


def _select_cols(a, sel):
    hi, mid, lo = _split3(a)
    return _dot(hi, sel) + (_dot(mid, sel) + _dot(lo, sel))


def _select_rows(sel, a):
    hi, mid, lo = _split3(a)
    return _dot(sel, hi) + (_dot(sel, mid) + _dot(sel, lo))


def _cmul(ar, ai, br, bi):
    return ar * br - ai * bi, ar * bi + ai * br


def _s5_pow_kernel(are_ref, aim_ref, ldt_ref, pr_ref, pi_ref):
    j = jnp.minimum(lax.broadcasted_iota(jnp.int32, pr_ref.shape[1:], 0), CHUNK).astype(F32)
    for d in range(2):
        dt = jnp.exp(ldt_ref[d])
        e = jnp.exp(j * (are_ref[d] * dt))
        pr_ref[d] = e * jnp.cos(j * (aim_ref[d] * dt))
        pi_ref[d] = e * jnp.sin(j * (aim_ref[d] * dt))


def _s5_pow_call(a_re, a_im, log_dt):
    g, p = S5_GROUPS, S5_STATE
    flat = lambda a: a.reshape(2, 1, g * p)
    sds = jax.ShapeDtypeStruct((2, N_POW, g * p), F32)
    pr, pi = pl.pallas_call(_s5_pow_kernel, out_shape=[sds, sds], name="s5_powers")(
        flat(a_re), flat(a_im), flat(jnp.repeat(log_dt, p, axis=1)))
    by_group = lambda a: a.reshape(2, N_POW, g, p).transpose(0, 2, 1, 3)
    return by_group(pr), by_group(pi)


def _s5_prep_kernel(are_r, aim_r, pr_ref, pi_ref, qr_ref, qi_ref, bre_ref, bim_ref, cre_ref, cim_ref,
                    w1_ref, e_ref, dec_ref):
    t, n, p = CHUNK, S5_GROUP, S5_STATE
    width = t * n
    lane_tok = lax.broadcasted_iota(jnp.int32, (N_POW, width), 1) // n
    pow_id = lax.broadcasted_iota(jnp.int32, (N_POW, width), 0)
    onehot = lambda cond: jnp.where(cond, 1.0, 0.0).astype(BF16)
    sel_fwd = onehot(pow_id == lane_tok)
    sel_rev = onehot(pow_id == t - 1 - lane_tok)
    sel_out = onehot(pow_id == t - lane_tok)
    row_tok = lax.broadcasted_iota(jnp.int32, (width, N_POW), 0) // n
    row_pow = lax.broadcasted_iota(jnp.int32, (width, N_POW), 1)
    rsel_rev = onehot(row_pow == t - 1 - row_tok)
    rsel_fwd = onehot(row_pow == row_tok)
    lane = lax.broadcasted_iota(jnp.int32, (n, width), 1)
    for gi in range(w1_ref.shape[0]):
        _s5_prep_group(gi, are_r, aim_r, pr_ref, pi_ref, qr_ref, qi_ref, bre_ref, bim_ref, cre_ref, cim_ref,
                       w1_ref, e_ref, dec_ref, (sel_fwd, sel_rev, sel_out, rsel_rev, rsel_fwd, lane))


def _s5_prep_group(gi, are_r, aim_r, pr_ref, pi_ref, qr_ref, qi_ref, bre_ref, bim_ref, cre_ref, cim_ref,
                   w1_ref, e_ref, dec_ref, selectors):
    sel_fwd, sel_rev, sel_out, rsel_rev, rsel_fwd, lane = selectors
    t, n = CHUNK, S5_GROUP
    width = t * n
    strips = []
    f_parts = []
    e_parts = []
    for d in range(2):
        qr = qr_ref[d, gi]
        qi = qi_ref[d, gi]
        pr = pr_ref[d, gi]
        pi = pi_ref[d, gi]
        a_re = are_r[d, gi]
        a_im = aim_r[d, gi]
        nr = pr[1:2] - 1.0
        ni = pi[1:2]
        den = a_re * a_re + a_im * a_im
        fre = (nr * a_re + ni * a_im) / den
        fim = (ni * a_re - nr * a_im) / den
        bt_re = bre_ref[d, gi].T
        bt_im = bim_ref[d, gi].T
        bb_re, bb_im = _cmul(fre, fim, bt_re, bt_im)
        ct_re = jnp.concatenate([cre_ref[d, gi].T] * t, axis=1)
        ct_im = jnp.concatenate([cim_ref[d, gi].T] * t, axis=1)
        sel = sel_fwd if d == 0 else sel_rev
        w_re, w_im = _cmul(ct_re, ct_im, _select_cols(qr, sel), _select_cols(qi, sel))
        strips.append(_dot3(bb_re, w_re) - _dot3(bb_im, w_im))
        if d == 0:
            o_re, o_im = _cmul(w_re, w_im, qr[:, 1:2], qi[:, 1:2])
        else:
            o_re, o_im = _cmul(ct_re, ct_im, _select_cols(qr, sel_out), _select_cols(qi, sel_out))
        e_parts += [o_re, -o_im]
        rsel = rsel_rev if d == 0 else rsel_fwd
        f_re, f_im = _cmul(jnp.concatenate([bb_re] * t, axis=0), jnp.concatenate([bb_im] * t, axis=0),
                           _select_rows(rsel, pr), _select_rows(rsel, pi))
        f_parts += [f_re, f_im]
        dec_ref[gi, 2 * d:2 * d + 1, :] = jnp.concatenate([pr[t:t + 1], pr[t:t + 1]], axis=1)
        dec_ref[gi, 2 * d + 1:2 * d + 2, :] = jnp.concatenate([-pi[t:t + 1], pi[t:t + 1]], axis=1)

    blocks = []
    for s in range(t):
        fwd = strips[0] if s == 0 else jnp.where(lane >= n * s, pltpu.roll(strips[0], n * s, axis=1), 0.0)
        back = t - 1 - s
        bwd = strips[1] if back == 0 else jnp.where(lane < width - n * back,
                                                     pltpu.roll(strips[1], width - n * back, axis=1), 0.0)
        blocks.append(fwd + bwd)
    m = jnp.concatenate(blocks, axis=0)
    w1_ref[gi] = jnp.concatenate([m] + f_parts, axis=1).astype(BF16)
    e_ref[gi] = jnp.concatenate(e_parts, axis=0).astype(BF16)


def _s5_operators(a_re, a_im, log_dt, b_re, b_im, c_re, c_im):
    g, p, n, t = S5_GROUPS, S5_STATE, S5_GROUP, CHUNK
    pr, pi = _s5_pow_call(a_re, a_im, log_dt)
    gb = PREP_GROUPS
    spec = lambda shape: pl.BlockSpec((2, gb) + shape, lambda i: (0, i) + (0,) * len(shape))
    return pl.pallas_call(
        _s5_prep_kernel, grid=(g // gb,),
        in_specs=[spec((1, p)), spec((1, p)), spec((N_POW, p)), spec((N_POW, p)), spec((p, N_POW)),
                  spec((p, N_POW)), spec((p, n)), spec((p, n)), spec((n, p)), spec((n, p))],
        out_specs=[pl.BlockSpec((gb, t * n, 2 * t * n), lambda i: (i, 0, 0)),
                   pl.BlockSpec((gb, 4 * p, t * n), lambda i: (i, 0, 0)),
                   pl.BlockSpec((gb, 4, 2 * p), lambda i: (i, 0, 0))],
        out_shape=[jax.ShapeDtypeStruct((g, t * n, 2 * t * n), BF16),
                   jax.ShapeDtypeStruct((g, 4 * p, t * n), BF16),
                   jax.ShapeDtypeStruct((g, 4, 2 * p), F32)],
        compiler_params=_cparams(("parallel",)),
        name="s5_prep",
    )(a_re.reshape(2, g, 1, p), a_im.reshape(2, g, 1, p), pr, pi, pr.transpose(0, 1, 3, 2),
      pi.transpose(0, 1, 3, 2), b_re, b_im, c_re, c_im)


def _block_transpose8(ps):
    ps = list(ps)
    blk = lax.broadcasted_iota(jnp.int32, ps[0].shape, 1) // S5_GROUP
    for k in range(3):
        step = 1 << k
        shift = S5_GROUP * step
        keep = ((blk >> k) & 1) == 0
        for a in range(8):
            if a & step:
                continue
            pa, pb = ps[a], ps[a + step]
            ps[a] = jnp.where(keep, pa, pltpu.roll(pb, shift, axis=1))
            ps[a + step] = jnp.where(keep, pltpu.roll(pa, 128 - shift, axis=1), pb)
    return ps


def _s5a_kernel(u_ref, w1_ref, yin_ref, gf_ref, gb_ref):
    nb = u_ref.shape[1]
    xs = []
    for s in range(CHUNK):
        parts = [u_ref[0, b, pl.ds(s, TILE_CHUNKS, stride=CHUNK), :] for b in range(nb)]
        xs.append(jnp.concatenate(parts, axis=0))
    lo = _block_transpose8(xs[:8])
    hi = _block_transpose8(xs[8:])
    for j in range(GRP_PER_BLK):
        og = jnp.concatenate([lo[j], hi[j]], axis=1)
        r = _dot(og.astype(BF16), w1_ref[j])
        yin_ref[j] = r[:, 0:256].astype(yin_ref.dtype)
        gf_ref[j] = r[:, 256:384]
        gb_ref[j] = r[:, 384:512]


def _s5a_call(u, w1):
    n_blk, b, s, _ = u.shape
    nt = s // TILE_TOK
    rows = b * TILE_CHUNKS
    out_spec = lambda n: pl.BlockSpec((GRP_PER_BLK, rows, n), lambda l, j: (l, j, 0))
    return pl.pallas_call(
        _s5a_kernel, grid=(n_blk, nt),
        in_specs=[pl.BlockSpec((1, b, TILE_TOK, LANE_BLK), lambda l, j: (l, 0, j, 0)),
                  pl.BlockSpec((GRP_PER_BLK, 256, 512), lambda l, j: (l, 0, 0))],
        out_specs=[out_spec(256), out_spec(128), out_spec(128)],
        out_shape=[jax.ShapeDtypeStruct((S5_GROUPS, nt * rows, 256), BF16),
                   jax.ShapeDtypeStruct((S5_GROUPS, nt * rows, 128), F32),
                   jax.ShapeDtypeStruct((S5_GROUPS, nt * rows, 128), F32)],
        compiler_params=_cparams(("parallel", "parallel")),
        name="s5_chunk_in",
    )(u, w1)


def _s5b_kernel(nb, a_ref, gfc_ref, gfl_ref, gbc_ref, gbl_ref, hfc_ref, hfl_ref, hbc_ref, hbl_ref):
    gb = a_ref.shape[0]
    rows = nb * TILE_CHUNKS
    n_lat = gfl_ref.shape[1] // rows
    a1f = [jnp.broadcast_to(a_ref[g, 0:1, :], (nb, 128)) for g in range(gb)]
    a2f = [jnp.broadcast_to(a_ref[g, 1:2, :], (nb, 128)) for g in range(gb)]
    a1b = [jnp.broadcast_to(a_ref[g, 2:3, :], (nb, 128)) for g in range(gb)]
    a2b = [jnp.broadcast_to(a_ref[g, 3:4, :], (nb, 128)) for g in range(gb)]

    def step(state, a1, a2, g_ref, h_ref, g, row):
        h, hs = state
        h_ref[g, row, :] = h
        inp = g_ref[g, row, :]
        return a1 * h + a2 * hs + inp, a1 * hs - a2 * h + pltpu.roll(inp, 64, axis=1)

    zero = jnp.zeros((nb, 128), F32)
    hf = [(zero, zero) for _ in range(gb)]
    hb = [(zero, zero) for _ in range(gb)]
    for ci in range(TILE_CHUNKS):
        rf = pl.ds(ci, nb, stride=TILE_CHUNKS)
        rb = pl.ds(TILE_CHUNKS - 1 - ci, nb, stride=TILE_CHUNKS)
        for g in range(gb):
            hf[g] = step(hf[g], a1f[g], a2f[g], gfc_ref, hfc_ref, g, rf)
            hb[g] = step(hb[g], a1b[g], a2b[g], gbc_ref, hbc_ref, g, rb)

    def body(j, carry):
        hf, hb = carry
        hf = list(hf)
        hb = list(hb)
        base_f = j * rows
        base_b = (n_lat - 1 - j) * rows
        for ci in range(TILE_CHUNKS):
            rf = pl.ds(base_f + ci, nb, stride=TILE_CHUNKS)
            rb = pl.ds(base_b + (TILE_CHUNKS - 1 - ci), nb, stride=TILE_CHUNKS)
            for g in range(gb):
                hf[g] = step(hf[g], a1f[g], a2f[g], gfl_ref, hfl_ref, g, rf)
                hb[g] = step(hb[g], a1b[g], a2b[g], gbl_ref, hbl_ref, g, rb)
        return tuple(hf), tuple(hb)

    lax.fori_loop(0, n_lat, body, (tuple(hf), tuple(hb)))


def _s5b_call(decay, gf_c, gf_l, gb_c, gb_l, nb):
    gblk = 4
    spec = lambda a: pl.BlockSpec((gblk, a.shape[1], 128), lambda i: (i, 0, 0))
    sds = lambda a: jax.ShapeDtypeStruct(a.shape, F32)
    return pl.pallas_call(
        functools.partial(_s5b_kernel, nb), grid=(S5_GROUPS // gblk,),
        in_specs=[pl.BlockSpec((gblk, 4, 128), lambda i: (i, 0, 0)),
                  spec(gf_c), spec(gf_l), spec(gb_c), spec(gb_l)],
        out_specs=[spec(gf_c), spec(gf_l), spec(gb_c), spec(gb_l)],
        out_shape=[sds(gf_c), sds(gf_l), sds(gb_c), sds(gb_l)],
        compiler_params=_cparams(("parallel",)),
        name="s5_state_scan",
    )(decay, gf_c, gf_l, gb_c, gb_l)


def _s5c_kernel(yin_ref, hf_ref, hb_ref, e_ref, u_ref, d_ref, y_ref):
    nb = u_ref.shape[1]
    ys = []
    for j in range(GRP_PER_BLK):
        h = jnp.concatenate([hf_ref[j], hb_ref[j]], axis=1).astype(BF16)
        ys.append(yin_ref[j].astype(F32) + _dot(h, e_ref[j]))
    at = (_block_transpose8([y[:, :128] for y in ys])
          + _block_transpose8([y[:, 128:] for y in ys]))
    d = d_ref[...]
    for t in range(CHUNK):
        for b in range(nb):
            rows = pl.ds(t, TILE_CHUNKS, stride=CHUNK)
            y_ref[0, b, rows, :] = at[t][b * TILE_CHUNKS:(b + 1) * TILE_CHUNKS] + d * u_ref[0, b, rows, :]


def _s5c_call(yin, hf, hb, e, u, d_skip):
    n_blk, b, s, _ = u.shape
    nt = s // TILE_TOK
    rows = b * TILE_CHUNKS
    gspec = lambda n: pl.BlockSpec((GRP_PER_BLK, rows, n), lambda l, j: (l, j, 0))
    tok_spec = pl.BlockSpec((1, b, TILE_TOK, LANE_BLK), lambda l, j: (l, 0, j, 0))
    return pl.pallas_call(
        _s5c_kernel, grid=(n_blk, nt),
        in_specs=[gspec(256), gspec(128), gspec(128),
                  pl.BlockSpec((GRP_PER_BLK, 256, 256), lambda l, j: (l, 0, 0)),
                  tok_spec,
                  pl.BlockSpec((1, LANE_BLK), lambda l, j: (0, l))],
        out_specs=tok_spec,
        out_shape=_lane_block_shape(b, s),
        compiler_params=_cparams(("parallel", "parallel")),
        name="s5_chunk_out",
    )(yin, hf, hb, e, u, d_skip.reshape(1, D_S5))


def _gelu_tanh(x):
    return 0.5 * x * (1.0 + jnp.tanh(math.sqrt(2.0 / math.pi) * (x + 0.044715 * (x * x * x))))


def _route(logits):
    lane = lax.broadcasted_iota(jnp.int32, logits.shape, 1).astype(F32)
    neg = jnp.float32(-1e30)
    big = jnp.float32(1e9)
    gl = jnp.where(lane < N_GROUPS, logits, neg)
    gmax = jnp.max(gl, axis=1, keepdims=True)
    gidx = jnp.min(jnp.where(gl == gmax, lane, big), axis=1, keepdims=True)
    gsum = jnp.sum(jnp.exp(gl - gmax), axis=1, keepdims=True)
    gw = 1.0 / gsum
    lo = N_GROUPS + EXP_PER_GROUP * gidx
    el = jnp.where((lane >= lo) & (lane < lo + EXP_PER_GROUP), logits, neg)
    v1 = jnp.max(el, axis=1, keepdims=True)
    i1 = jnp.min(jnp.where(el == v1, lane, big), axis=1, keepdims=True)
    el2 = jnp.where(lane == i1, neg, el)
    v2 = jnp.max(el2, axis=1, keepdims=True)
    i2 = jnp.min(jnp.where(el2 == v2, lane, big), axis=1, keepdims=True)
    ex = jnp.exp(v2 - v1)
    p1 = 1.0 / (1.0 + ex)
    p2 = ex * p1
    e1 = i1 - lo
    e2 = i2 - lo
    first = e1 < e2
    ea = jnp.where(first, e1, e2)
    eb = jnp.where(first, e2, e1)
    wa = gw * jnp.where(first, p1, p2)
    wb = gw * jnp.where(first, p2, p1)
    pair = ea * (7.0 - ea) * 0.5 + (eb - ea - 1.0)
    return wa, wb, 6.0 * gidx + pair


def _out_kernel(ypre_ref, ysc_ref, ycf_ref, x_ref, mod_ref, wglu_ref, bglu_ref, wo_ref,
                lng_ref, lnb_ref, wr_ref, br_ref, cnt0_ref, x1_ref, hx_ref, meta_ref, counts_ref, *rest):
    stage_ref = rest[0] if len(rest) == 2 else None
    cnt_ref = rest[-1]

    @pl.when((pl.program_id(0) == 0) & (pl.program_id(1) == 0))
    def _():
        cnt_ref[...] = cnt0_ref[...]

    tm = x_ref.shape[1]
    sub = tm // OUT_SUBTILES

    def row_chain(r0):
        rows = slice(r0, r0 + sub)
        ypre = jnp.concatenate([ypre_ref[blk, 0, rows, :] for blk in range(ypre_ref.shape[0])], axis=1)
        t = _gelu_tanh(ypre)
        gate = _sigmoid(_dot(t.astype(BF16), wglu_ref[...]) + bglu_ref[...])
        ys5 = (t * gate).astype(BF16)
        y = (_dot(ys5, wo_ref[0:D_S5, :]) + _dot(ysc_ref[0, rows, :], wo_ref[D_S5:D_S5 + D_SC, :])
             + _dot(ycf_ref[0, rows, :], wo_ref[D_S5 + D_SC:D_MODEL, :]))
        g1 = mod_ref[0, 2:3, :]
        x1 = _layer_norm(DN_ALPHA * x_ref[0, rows, :] + g1 * y, lng_ref[...], lnb_ref[...])
        x1_ref[0, rows, :] = x1
        h2 = x1 * (1.0 + mod_ref[0, 4:5, :]) + mod_ref[0, 3:4, :]
        hx_ref[0, rows, 0:D_MODEL] = h2
        return _route(_dot(h2.astype(BF16), wr_ref[...]) + br_ref[...])

    routed = [row_chain(r0) for r0 in range(0, tm, sub)]
    wa, wb, cls = (jnp.concatenate([r[i] for r in routed], axis=0) for i in range(3))

    lane = lax.broadcasted_iota(jnp.int32, (tm, ROUTER_LANES), 1).astype(F32)
    onehot = jnp.where(lane == cls, 1.0, 0.0)
    row_i = lax.broadcasted_iota(jnp.int32, (tm, tm), 0)
    col_i = lax.broadcasted_iota(jnp.int32, (tm, tm), 1)
    earlier = jnp.where(col_i < row_i, 1.0, 0.0).astype(BF16)
    before = _dot(earlier, onehot.astype(BF16)) + cnt_ref[...]
    rank = jnp.sum(before * onehot, axis=1, keepdims=True)
    cnt_ref[...] += jnp.sum(onehot, axis=0, keepdims=True)
    counts_ref[...] = cnt_ref[...]

    meta = (jnp.where(lane == META_WA, wa, 0.0) + jnp.where(lane == META_WB, wb, 0.0)
            + jnp.where(lane == META_CLS, cls, 0.0) + jnp.where(lane == META_RANK, rank, 0.0))
    meta_ref[...] = jnp.transpose(meta)[0:SUBLANES, :]
    hx_ref[0, :, D_MODEL:HX_LANES] = meta
    if stage_ref is not None:
        stage_ref[...] = jnp.zeros_like(stage_ref)


def _out_call(ypre, ysc, ycf, x, mod, wglu_bf, b_glu, wo_bf, ln_g, ln_b, w_router, b_router, counts0, tm,
              stage_rows):
    b, s, d = x.shape
    nt = s // tm
    row_spec = lambda n: pl.BlockSpec((1, tm, n), lambda i, j: (i, j, 0))
    full = lambda shape: pl.BlockSpec(shape, lambda i, j: (0,) * len(shape))
    out_specs = [row_spec(d), row_spec(HX_LANES),
                 pl.BlockSpec((SUBLANES, tm), lambda i, j: (0, i * nt + j)),
                 full((1, ROUTER_LANES))]
    out_shape = [jax.ShapeDtypeStruct((b, s, d), F32), jax.ShapeDtypeStruct((b, s, HX_LANES), F32),
                 jax.ShapeDtypeStruct((SUBLANES, b * s), F32),
                 jax.ShapeDtypeStruct((1, ROUTER_LANES), F32)]
    if stage_rows:
        stage_octs = stage_rows // (SUBLANES * b * nt)
        assert stage_octs * SUBLANES * b * nt == stage_rows
        out_specs.append(pl.BlockSpec((stage_octs, SUBLANES, HX_LANES), lambda i, j: (i * nt + j, 0, 0)))
        out_shape.append(jax.ShapeDtypeStruct((stage_rows // SUBLANES, SUBLANES, HX_LANES), F32))
    return pl.pallas_call(
        _out_kernel, grid=(b, nt),
        in_specs=[_lane_block_spec(tm), row_spec(D_SC), row_spec(D_CF), row_spec(d),
                  pl.BlockSpec((1, 6, d), lambda i, j: (i, 0, 0)),
                  full((D_S5, D_S5)), full((1, D_S5)), full((d, d)),
                  full((1, d)), full((1, d)), full((d, ROUTER_LANES)), full((1, ROUTER_LANES)),
                  full((1, ROUTER_LANES))],
        out_specs=out_specs, out_shape=out_shape,
        scratch_shapes=[pltpu.VMEM((1, ROUTER_LANES), F32)],
        compiler_params=_cparams(("arbitrary", "arbitrary")),
        name="out_proj",
    )(ypre, ysc, ycf, x, mod, wglu_bf, b_glu.reshape(1, -1), wo_bf, ln_g.reshape(1, -1),
      ln_b.reshape(1, -1), w_router, b_router, counts0)


def _sorted_rows(n_tok):
    return n_tok + N_CLASSES * MOE_TM


def _moe_plan(meta, counts, n_tok):
    cls = meta[META_CLS].astype(jnp.int32)
    rank = meta[META_RANK].astype(jnp.int32)
    cnt = counts[0, :N_CLASSES].astype(jnp.int32)
    n_tiles = (cnt + (MOE_TM - 1)) // MOE_TM
    ends = jnp.cumsum(n_tiles)
    starts = ends - n_tiles
    slot = starts[cls] * MOE_TM + rank
    t_max = n_tok // MOE_TM + N_CLASSES
    n_used = ends[N_CLASSES - 1]
    tile = jnp.minimum(jnp.arange(t_max, dtype=jnp.int32), n_used - 1)
    tile_cls = jnp.sum((tile[:, None] >= ends[None, :]).astype(jnp.int32), axis=1)
    group = tile_cls // 6
    pair = tile_cls % 6
    first = jnp.array([0, 0, 0, 1, 1, 2], jnp.int32)[pair] + EXP_PER_GROUP * group
    second = jnp.array([1, 2, 3, 2, 3, 3], jnp.int32)[pair] + EXP_PER_GROUP * group
    return slot, tile, first, second, n_used.reshape(1)


def _split_row(row):
    return lax.shift_right_logical(row, 3), lax.bitwise_and(row, SUBLANES - 1)


def _dispatch_kernel(slot_ref, hx_ref, xs_init_ref, xs_ref, sem):
    del xs_init_ref
    n_oct = hx_ref.shape[1]
    base = (pl.program_id(0) * pl.num_programs(1) + pl.program_id(1)) * (n_oct * SUBLANES)

    def body(i, carry):
        for k in range(SUBLANES):
            oct_id, sub = _split_row(slot_ref[base + i * SUBLANES + k])
            pltpu.make_async_copy(hx_ref.at[0, i, pl.ds(k, 1), :], xs_ref.at[oct_id, pl.ds(sub, 1), :],
                                  sem).start(priority=k % 2)
        return carry

    lax.fori_loop(0, n_oct, body, 0)
    pltpu.make_async_copy(hx_ref.at[0], xs_ref.at[pl.ds(0, n_oct)], sem).wait()


def _dispatch_call(slot, hx, xs_init, tm):
    b, s, w = hx.shape
    n_rows = xs_init.shape[0] * SUBLANES
    grid_spec = pltpu.PrefetchScalarGridSpec(
        num_scalar_prefetch=1, grid=(b, s // tm),
        in_specs=[pl.BlockSpec((1, tm // SUBLANES, SUBLANES, w), lambda i, j, slot: (i, j, 0, 0)),
                  pl.BlockSpec(memory_space=pl.ANY)],
        out_specs=pl.BlockSpec(memory_space=pl.ANY),
        scratch_shapes=[pltpu.SemaphoreType.DMA(())])
    xs = pl.pallas_call(
        _dispatch_kernel, grid_spec=grid_spec,
        out_shape=jax.ShapeDtypeStruct(xs_init.shape, F32),
        input_output_aliases={2: 0},
        compiler_params=_cparams(("arbitrary", "arbitrary")),
        name="moe_dispatch",
    )(slot, hx.reshape(b, s // SUBLANES, SUBLANES, w), xs_init)
    return xs.reshape(n_rows, w)


def _moe_kernel(tile_ref, first_ref, second_ref, nused_ref, xs_ref, wga_ref, wgb_ref, wua_ref, wub_ref,
                wda_ref, wdb_ref, ys_ref):
    del tile_ref, first_ref, second_ref
    t = pl.program_id(0)
    n_used = nused_ref[0]

    @pl.when(t < n_used)
    def _():
        x = xs_ref[...]
        xb = x[:, 0:D_MODEL].astype(BF16)

        def expert(wg_ref, wu_ref, wd_ref, w):
            gate = _dot(xb, wg_ref[0])
            up = _dot(xb, wu_ref[0])
            act = gate * _sigmoid(gate) * up * w
            return _dot(act.astype(BF16), wd_ref[0])

        wa = x[:, D_MODEL + META_WA:D_MODEL + META_WA + 1]
        wb = x[:, D_MODEL + META_WB:D_MODEL + META_WB + 1]
        ys_ref[...] = expert(wga_ref, wua_ref, wda_ref, wa) + expert(wgb_ref, wub_ref, wdb_ref, wb)

    @pl.when(t >= n_used)
    def _():
        ys_ref[...] = jnp.zeros_like(ys_ref)


def _moe_call(tile, first, second, n_used, xs, wg_bf, wu_bf, wd_bf):
    n_rows, w = xs.shape
    d = D_MODEL
    t_max = tile.shape[0]
    up_spec = lambda sel: pl.BlockSpec((1, d, D_EXPERT), lambda t, tl, fi, se, nu: ((fi, se)[sel][t], 0, 0))
    down_spec = lambda sel: pl.BlockSpec((1, D_EXPERT, d), lambda t, tl, fi, se, nu: ((fi, se)[sel][t], 0, 0))
    grid_spec = pltpu.PrefetchScalarGridSpec(
        num_scalar_prefetch=4, grid=(t_max,),
        in_specs=[pl.BlockSpec((MOE_TM, w), lambda t, tl, fi, se, nu: (tl[t], 0)),
                  up_spec(0), up_spec(1), up_spec(0), up_spec(1), down_spec(0), down_spec(1)],
        out_specs=pl.BlockSpec((MOE_TM, d), lambda t, tl, fi, se, nu: (t, 0)))
    return pl.pallas_call(
        _moe_kernel, grid_spec=grid_spec,
        out_shape=jax.ShapeDtypeStruct((n_rows, d), F32),
        compiler_params=_cparams(("arbitrary",)),
        name="moe_experts",
    )(tile, first, second, n_used, xs, wg_bf, wg_bf, wu_bf, wu_bf, wd_bf, wd_bf)


def _combine_kernel(slot_ref, x1_ref, mod_ref, lng_ref, lnb_ref, ys_ref, o_ref, f_ref, sem):
    o_ref[0] = _combine_rows(slot_ref, x1_ref, mod_ref, lng_ref, lnb_ref, ys_ref, f_ref, sem)


def _combine_in_kernel(slot_ref, x1_ref, mod_ref, lng_ref, lnb_ref, ys_ref, mod_next_ref, w_ref,
                       o_ref, u_ref, bg_ref, cv_ref, gl_ref, f_ref, sem):
    x2 = _combine_rows(slot_ref, x1_ref, mod_ref, lng_ref, lnb_ref, ys_ref, f_ref, sem)
    o_ref[0] = x2
    _in_body(x2, mod_next_ref, w_ref, u_ref, bg_ref, cv_ref, gl_ref)


def _combine_rows(slot_ref, x1_ref, mod_ref, lng_ref, lnb_ref, ys_ref, f_ref, sem):
    n_oct = f_ref.shape[1]
    tm = n_oct * SUBLANES
    step = pl.program_id(0) * pl.num_programs(1) + pl.program_id(1)
    n_steps = pl.num_programs(0) * pl.num_programs(1)

    def request(which, buf):
        base = which * tm

        def body(i, carry):
            for k in range(SUBLANES):
                oct_id, sub = _split_row(slot_ref[base + i * SUBLANES + k])
                pltpu.make_async_copy(ys_ref.at[oct_id, pl.ds(sub, 1), :], f_ref.at[buf, i, pl.ds(k, 1), :],
                                      sem.at[buf]).start(priority=k % 2)
            return carry

        lax.fori_loop(0, n_oct, body, 0)

    @pl.when(step == 0)
    def _():
        request(0, 0)

    @pl.when(step + 1 < n_steps)
    def _():
        request(step + 1, (step + 1) % 2)

    buf = step % 2
    pltpu.make_async_copy(ys_ref.at[pl.ds(0, n_oct)], f_ref.at[buf], sem.at[buf]).wait()
    f = f_ref[buf].reshape(tm, f_ref.shape[3])
    g2 = mod_ref[0, 5:6, :]
    return _layer_norm(DN_ALPHA * x1_ref[0] + g2 * f, lng_ref[...], lnb_ref[...])


def _combine_call(slot, x1, mod, ln_g, ln_b, ys, tm, next_in=None):
    b, s, d = x1.shape
    ys = ys.reshape(ys.shape[0] // SUBLANES, SUBLANES, d)
    row_spec = lambda n: pl.BlockSpec((1, tm, n), lambda i, j, slot: (i, j, 0))
    mod_spec = pl.BlockSpec((1, 6, d), lambda i, j, slot: (i, 0, 0))
    full = lambda shape: pl.BlockSpec(shape, lambda i, j, slot: (0,) * len(shape))
    in_specs = [row_spec(d), mod_spec, full((1, d)), full((1, d)), pl.BlockSpec(memory_space=pl.ANY)]
    out_specs = [row_spec(d)]
    out_shape = [jax.ShapeDtypeStruct((b, s, d), F32)]
    args = [slot, x1, mod, ln_g.reshape(1, -1), ln_b.reshape(1, -1), ys]
    kern, name = _combine_kernel, "moe_combine"
    if next_in is not None:
        kern, name = _combine_in_kernel, "moe_combine_in_proj"
        in_specs += [mod_spec, full((d, D_IN))]
        args += list(next_in)
        out_specs += [pl.BlockSpec((D_S5 // LANE_BLK, 1, tm, LANE_BLK), lambda i, j, slot: (0, i, j, 0)),
                      row_spec(D_SC), row_spec(D_SC), row_spec(D_CF)]
        out_shape += [_lane_block_shape(b, s)] + [jax.ShapeDtypeStruct((b, s, D_SC), F32)] * 3
    grid_spec = pltpu.PrefetchScalarGridSpec(
        num_scalar_prefetch=1, grid=(b, s // tm), in_specs=in_specs, out_specs=out_specs,
        scratch_shapes=[pltpu.VMEM((2, tm // SUBLANES, SUBLANES, d), F32), pltpu.SemaphoreType.DMA((2,))])
    outs = pl.pallas_call(
        kern, grid_spec=grid_spec, out_shape=out_shape,
        compiler_params=_cparams(("arbitrary", "arbitrary")),
        name=name,
    )(*args)
    return outs[0] if next_in is None else outs


def _moe_sublayer(parts, counts, xs_init, wg_bf, wu_bf, wd_bf, ln_g, ln_b, next_in=None):
    sizes = [p[2].shape[0] * p[2].shape[1] for p in parts]
    meta = jnp.concatenate([p[1] for p in parts], axis=1)
    slot, tile, first, second, n_used = _moe_plan(meta, counts, sum(sizes))
    slots, start = [], 0
    for n in sizes:
        slots.append(slot[start:start + n])
        start += n
    xs = xs_init
    for (hx, _, _, _, tm), sl in zip(parts, slots):
        xs = _dispatch_call(sl, hx, xs.reshape(xs_init.shape), tm)
    ys = _moe_call(tile, first, second, n_used, xs, wg_bf, wu_bf, wd_bf)
    outs = []
    for idx, ((_, _, x1, mod, tm), sl) in enumerate(zip(parts, slots)):
        fuse = next_in if idx == len(parts) - 1 else None
        outs.append(_combine_call(sl, x1, mod, ln_g, ln_b, ys, tm, fuse))
    return outs


def kernel(x, c, ctx, c_ctx, w_mod, b_mod, w_in, s5_a_re, s5_a_im, s5_log_dt, s5_b_re, s5_b_im, s5_c_re, s5_c_im, s5_d, w_glu, b_glu, w_sc, w_dw, b_dw, ln_cf_g, ln_cf_b, w_o, ln1_g, ln1_b, w_rg, b_rg, w_rexp, b_rexp, w_gate, w_up, w_down, ln2_g, ln2_b):
    nb, seq, d = x.shape
    n_ctx = ctx.shape[1]
    n_layers = w_mod.shape[0]
    assert seq % TILE_TOK == 0 and n_ctx % TILE_TOK == 0 and seq % GRID_W == 0

    mod_rows = 16
    assert nb + 1 <= mod_rows
    c_all = jnp.concatenate([c, c_ctx[None, :], jnp.zeros((mod_rows - nb - 1, d), F32)], axis=0)
    mod_all = _mod_call(c_all, w_mod, b_mod)

    pad_r = ROUTER_LANES - N_GROUPS - N_EXPERTS
    x_lat, x_ctx = x, ctx
    mods_lat = [mod_all[l, :nb].reshape(nb, 6, d) for l in range(n_layers)]
    w_in_bfs = [w_in[l].astype(BF16) for l in range(n_layers)]
    lat_proj = None
    for l in range(n_layers):
        last = l == n_layers - 1
        mod_lat = mods_lat[l]
        mod_ctx = jnp.broadcast_to(mod_all[l, nb].reshape(1, 6, d), (nb, 6, d))
        w_in_bf = w_in_bfs[l]
        wglu_bf = w_glu[l].astype(BF16)
        wo_bf = w_o[l].astype(BF16)
        wg_bf = w_gate[l].astype(BF16)
        wu_bf = w_up[l].astype(BF16)
        wd_bf = w_down[l].astype(BF16)
        w_router = jnp.concatenate([w_rg[l], w_rexp[l], jnp.zeros((d, pad_r), F32)], axis=1).astype(BF16)
        b_router = jnp.concatenate([b_rg[l], b_rexp[l], jnp.zeros((pad_r,), F32)]).reshape(1, -1)
        w1, e_op, decay = _s5_operators(s5_a_re[l], s5_a_im[l], s5_log_dt[l], s5_b_re[l], s5_b_im[l],
                                        s5_c_re[l], s5_c_im[l])

        if lat_proj is None:
            u_l, bg_l, cv_l, gl_l = _in_call(x_lat, mod_lat, w_in_bf, 512, False)
        else:
            u_l, bg_l, cv_l, gl_l = lat_proj
        if last:
            u_c = _in_call(x_ctx, mod_ctx, w_in_bf[:, :D_S5], TILE_TOK, True)
        else:
            u_c, bg_c, cv_c, gl_c = _in_call(x_ctx, mod_ctx, w_in_bf, TILE_TOK, False)

        yin_l, gf_l, gb_l = _s5a_call(u_l, w1)
        yin_c, gf_c, gb_c = _s5a_call(u_c, w1)
        hf_c, hf_l, hb_c, hb_l = _s5b_call(decay, gf_c, gf_l, gb_c, gb_l, nb)
        ypre_l = _s5c_call(yin_l, hf_l, hb_l, e_op, u_l, s5_d[l])

        ysc_l, ycf_l = _conv_call(bg_l, cv_l, gl_l, w_sc[l], w_dw[l], b_dw[l], ln_cf_g[l], ln_cf_b[l], True)
        parts = []
        counts = jnp.zeros((1, ROUTER_LANES), F32)
        n_moe = nb * seq
        if not last:
            ypre_c = _s5c_call(yin_c, hf_c, hb_c, e_op, u_c, s5_d[l])
            ysc_c, ycf_c = _conv_call(bg_c, cv_c, gl_c, w_sc[l], w_dw[l], b_dw[l], ln_cf_g[l], ln_cf_b[l], False)
            x1_c, hx_c, meta_c, counts = _out_call(ypre_c, ysc_c, ycf_c, x_ctx, mod_ctx, wglu_bf, b_glu[l], wo_bf,
                                                   ln1_g[l], ln1_b[l], w_router, b_router, counts, TILE_TOK, 0)
            parts.append((hx_c, meta_c, x1_c, mod_ctx, TILE_TOK))
            n_moe += nb * n_ctx
        x1_l, hx_l, meta_l, counts, stage = _out_call(ypre_l, ysc_l, ycf_l, x_lat, mod_lat, wglu_bf, b_glu[l], wo_bf,
                                                      ln1_g[l], ln1_b[l], w_router, b_router, counts, 512,
                                                      _sorted_rows(n_moe))
        parts.append((hx_l, meta_l, x1_l, mod_lat, 512))
        next_in = None if last else (mods_lat[l + 1], w_in_bfs[l + 1])
        outs = _moe_sublayer(parts, counts, stage, wg_bf, wu_bf, wd_bf, ln2_g[l], ln2_b[l], next_in)
        if last:
            x_lat = outs[-1]
        else:
            x_ctx = outs[0]
            x_lat, lat_proj = outs[-1][0], outs[-1][1:]
    return x_lat
```

```python
import functools
import math

import jax
import jax.numpy as jnp
from jax import lax
from jax.experimental import pallas as pl
from jax.experimental.pallas import tpu as pltpu

F32 = jnp.float32
BF16 = jnp.bfloat16

D_MODEL = 1024
DEPTH = 2
GRID_W = 64
D_S5 = 512
S5_GROUP = 16
S5_GROUPS = 32
S5_STATE = 64
D_SC = 256
D_CF = 256
CF_WIDTH = 31
CF_HALF = 15
D_IN = 1792
N_GROUPS = 4
EXP_PER_GROUP = 4
N_EXPERTS = 16
D_EXPERT = 256
DN_ALPHA = (2 * DEPTH) ** 0.25
LN_EPS = 1e-5

CHUNK = 16
N_POW = 32
PREP_GROUPS = 8
TILE_CHUNKS = 16
TILE_TOK = CHUNK * TILE_CHUNKS
LANE_BLK = 128
GRP_PER_BLK = LANE_BLK // S5_GROUP
ROUTER_LANES = 128
HX_LANES = D_MODEL + ROUTER_LANES
META_WA, META_WB, META_CLS, META_RANK = 0, 1, 2, 3
N_CLASSES = N_GROUPS * 6
MOE_TM = 512
OUT_SUBTILES = 2
ROW_TILE_BIG = 1024
SUBLANES = 8
VMEM_LIMIT = 56 * 1024 * 1024


def _cparams(sem):
    return pltpu.CompilerParams(dimension_semantics=sem, vmem_limit_bytes=VMEM_LIMIT)


def _split_bf16(a):
    hi = a.astype(BF16)
    lo = (a - hi.astype(F32)).astype(BF16)
    return hi, lo


def _dot(a, b):
    return jnp.dot(a, b, preferred_element_type=F32)


def _dot3(a, b):
    ah, al = _split_bf16(a)
    bh, bl = _split_bf16(b)
    return _dot(ah, bh) + (_dot(al, bh) + _dot(ah, bl))


def _sigmoid(x):
    return 1.0 / (1.0 + jnp.exp(-x))


def _layer_norm(x, g, b):
    mu = jnp.mean(x, axis=-1, keepdims=True)
    xc = x - mu
    var = jnp.mean(xc * xc, axis=-1, keepdims=True)
    return xc * lax.rsqrt(var + LN_EPS) * g + b


def _mod_kernel(c_ref, w_ref, b_ref, o_ref):
    c = c_ref[...]
    s = c * _sigmoid(c)
    o_ref[0] = _dot3(s, w_ref[0]) + b_ref[0]


def _mod_call(c_all, w_mod, b_mod):
    n_layers, d, n_out = w_mod.shape
    tn = 1536
    rows = c_all.shape[0]
    return pl.pallas_call(
        _mod_kernel,
        grid=(n_layers, n_out // tn),
        in_specs=[
            pl.BlockSpec((rows, d), lambda l, j: (0, 0)),
            pl.BlockSpec((1, d, tn), lambda l, j: (l, 0, j)),
            pl.BlockSpec((1, 1, tn), lambda l, j: (l, 0, j)),
        ],
        out_specs=pl.BlockSpec((1, rows, tn), lambda l, j: (l, 0, j)),
        out_shape=jax.ShapeDtypeStruct((n_layers, rows, n_out), F32),
        compiler_params=_cparams(("parallel", "parallel")),
        name="mod",
    )(c_all, w_mod, b_mod.reshape(n_layers, 1, n_out))


def _in_body(x, mod_ref, w_ref, u_ref, bg_ref, cv_ref, gl_ref):
    sh = mod_ref[0, 0:1, :]
    sc = mod_ref[0, 1:2, :]
    h = (x * (1.0 + sc) + sh).astype(BF16)
    z = _dot(h, w_ref[...])
    _store_lane_blocks(u_ref, z[:, 0:512])
    bg_ref[0] = z[:, 512:768]
    cv_ref[0] = z[:, 768:1024] * z[:, 1024:1280]
    gl_ref[0] = z[:, 1280:1536] * _sigmoid(z[:, 1536:1792])


def _in_kernel(x_ref, mod_ref, w_ref, u_ref, bg_ref, cv_ref, gl_ref):
    _in_body(x_ref[0], mod_ref, w_ref, u_ref, bg_ref, cv_ref, gl_ref)


def _in_u_kernel(x_ref, mod_ref, w_ref, u_ref):
    x = x_ref[0]
    sh = mod_ref[0, 0:1, :]
    sc = mod_ref[0, 1:2, :]
    h = (x * (1.0 + sc) + sh).astype(BF16)
    _store_lane_blocks(u_ref, _dot(h, w_ref[...]))


def _store_lane_blocks(ref, val):
    for blk in range(ref.shape[0]):
        ref[blk, 0] = val[:, blk * LANE_BLK:(blk + 1) * LANE_BLK]


def _lane_block_spec(tm):
    return pl.BlockSpec((D_S5 // LANE_BLK, 1, tm, LANE_BLK), lambda i, j: (0, i, j, 0))


def _lane_block_shape(b, s):
    return jax.ShapeDtypeStruct((D_S5 // LANE_BLK, b, s, LANE_BLK), F32)


def _in_call(x, mod, w_in_bf, tm, u_only):
    b, s, d = x.shape
    grid = (b, s // tm)
    row_spec = lambda n: pl.BlockSpec((1, tm, n), lambda i, j: (i, j, 0))
    in_specs = [
        row_spec(d),
        pl.BlockSpec((1, 6, d), lambda i, j: (i, 0, 0)),
    ]
    if u_only:
        in_specs.append(pl.BlockSpec((d, D_S5), lambda i, j: (0, 0)))
        return pl.pallas_call(
            _in_u_kernel, grid=grid, in_specs=in_specs,
            out_specs=_lane_block_spec(tm),
            out_shape=_lane_block_shape(b, s),
            compiler_params=_cparams(("parallel", "parallel")),
            name="in_proj_u",
        )(x, mod, w_in_bf)
    in_specs.append(pl.BlockSpec((d, D_IN), lambda i, j: (0, 0)))
    return pl.pallas_call(
        _in_kernel, grid=grid, in_specs=in_specs,
        out_specs=[_lane_block_spec(tm), row_spec(D_SC), row_spec(D_SC), row_spec(D_CF)],
        out_shape=[_lane_block_shape(b, s),
                   jax.ShapeDtypeStruct((b, s, D_SC), F32),
                   jax.ShapeDtypeStruct((b, s, D_SC), F32),
                   jax.ShapeDtypeStruct((b, s, D_CF), F32)],
        compiler_params=_cparams(("parallel", "parallel")),
        name="in_proj",
    )(x, mod, w_in_bf)


def _conv_tail(t, bdw_ref, lng_ref, lnb_ref):
    t = t + bdw_ref[...]
    t = _layer_norm(t, lng_ref[...], lnb_ref[...])
    return t * _sigmoid(t)


def _conv_grid_kernel(bg_ref, cv_ref, gl_ref, wsc_ref, wdw_ref, bdw_ref, lng_ref, lnb_ref,
                      ysc_ref, ycf_ref, grid_ref, t_ref):
    s = cv_ref.shape[1]
    rows = s // GRID_W
    cv = cv_ref[0]
    col = lax.broadcasted_iota(jnp.int32, (s, D_SC), 0) % GRID_W
    prev = jnp.where(col == 0, 0.0, pltpu.roll(cv, 1, axis=0))
    nxt = jnp.where(col == GRID_W - 1, 0.0, pltpu.roll(cv, s - 1, axis=0))
    conv = prev * wsc_ref[0:1, :] + cv * wsc_ref[1:2, :] + nxt * wsc_ref[2:3, :]
    ysc_ref[0] = (bg_ref[0] * conv).astype(ysc_ref.dtype)

    grid_ref[...] = gl_ref[0].reshape(rows, GRID_W, D_CF)

    def body(i, carry):
        w0 = pl.multiple_of(i * 8, 8)
        for half in range(D_CF // 128):
            lanes = slice(half * 128, (half + 1) * 128)
            acc = jnp.zeros((rows, 8, 128), F32)
            for k in range(CF_WIDTH):
                lo = max(0, CF_HALF - k)
                hi = min(rows, rows + CF_HALF - k)
                if hi <= lo:
                    continue
                term = grid_ref[lo + k - CF_HALF:hi + k - CF_HALF, pl.ds(w0, 8), lanes] * wdw_ref[k:k + 1, lanes]
                pieces = [acc[:lo]] * (lo > 0) + [acc[lo:hi] + term] + [acc[hi:]] * (hi < rows)
                acc = jnp.concatenate(pieces, axis=0) if len(pieces) > 1 else pieces[0]
            t_ref[:, pl.ds(w0, 8), lanes] = acc
        return carry

    lax.fori_loop(0, GRID_W // 8, body, 0)
    t = t_ref[...].reshape(s, D_CF)
    ycf_ref[0] = _conv_tail(t, bdw_ref, lng_ref, lnb_ref).astype(ycf_ref.dtype)


def _conv_seq_kernel(bg_ref, cv_ref, gl_ref, wsc_ref, wdw_ref, bdw_ref, lng_ref, lnb_ref,
                     ysc_ref, ycf_ref, pad_ref):
    s = cv_ref.shape[1]
    cv = cv_ref[0]
    pos = lax.broadcasted_iota(jnp.int32, (s, D_SC), 0)
    prev = jnp.where(pos == 0, 0.0, pltpu.roll(cv, 1, axis=0))
    nxt = jnp.where(pos == s - 1, 0.0, pltpu.roll(cv, s - 1, axis=0))
    conv = prev * wsc_ref[0:1, :] + cv * wsc_ref[1:2, :] + nxt * wsc_ref[2:3, :]
    ysc_ref[0] = (bg_ref[0] * conv).astype(ysc_ref.dtype)

    off = 16
    pad_ref[0:off] = jnp.zeros((off, D_CF), F32)
    pad_ref[off + s:off + s + 16] = jnp.zeros((16, D_CF), F32)
    pad_ref[off:off + s] = gl_ref[0]
    acc = jnp.zeros((s, D_CF), F32)
    for k in range(CF_WIDTH):
        acc = acc + pad_ref[pl.ds(off - CF_HALF + k, s), :] * wdw_ref[k:k + 1, :]
    ycf_ref[0] = _conv_tail(acc, bdw_ref, lng_ref, lnb_ref).astype(ycf_ref.dtype)


def _conv_call(bg, cv, gl, w_sc, w_dw, b_dw, ln_g, ln_b, grid_mode):
    b, s, _ = bg.shape
    row_spec = pl.BlockSpec((1, s, D_SC), lambda i: (i, 0, 0))
    full = lambda shape: pl.BlockSpec(shape, lambda i: (0,) * len(shape))
    if grid_mode:
        rows = s // GRID_W
        kern = _conv_grid_kernel
        scratch = [pltpu.VMEM((rows, GRID_W, D_CF), F32), pltpu.VMEM((rows, GRID_W, D_CF), F32)]
        name = "conv_grid"
    else:
        kern = _conv_seq_kernel
        scratch = [pltpu.VMEM((s + 32, D_CF), F32)]
        name = "conv_seq"
    return pl.pallas_call(
        kern, grid=(b,),
        in_specs=[row_spec, row_spec, row_spec, full((3, D_SC)), full((CF_WIDTH, D_CF)),
                  full((1, D_CF)), full((1, D_CF)), full((1, D_CF))],
        out_specs=[row_spec, row_spec],
        out_shape=[jax.ShapeDtypeStruct((b, s, D_SC), BF16), jax.ShapeDtypeStruct((b, s, D_CF), BF16)],
        scratch_shapes=scratch,
        compiler_params=_cparams(("parallel",)),
        name=name,
    )(bg, cv, gl, w_sc, w_dw, b_dw.reshape(1, -1), ln_g.reshape(1, -1), ln_b.reshape(1, -1))


def _split3(a):
    hi = a.astype(BF16)
    r = a - hi.astype(F32)
    mid = r.astype(BF16)
    lo = (r - mid.astype(F32)).astype(BF16)
    return hi, mid, lo


def _select_cols(a, sel):
    hi, mid, lo = _split3(a)
    return _dot(hi, sel) + (_dot(mid, sel) + _dot(lo, sel))


def _select_rows(sel, a):
    hi, mid, lo = _split3(a)
    return _dot(sel, hi) + (_dot(sel, mid) + _dot(sel, lo))


def _cmul(ar, ai, br, bi):
    return ar * br - ai * bi, ar * bi + ai * br


def _s5_pow_kernel(are_ref, aim_ref, ldt_ref, pr_ref, pi_ref):
    j = jnp.minimum(lax.broadcasted_iota(jnp.int32, pr_ref.shape[1:], 0), CHUNK).astype(F32)
    for d in range(2):
        dt = jnp.exp(ldt_ref[d])
        e = jnp.exp(j * (are_ref[d] * dt))
        pr_ref[d] = e * jnp.cos(j * (aim_ref[d] * dt))
        pi_ref[d] = e * jnp.sin(j * (aim_ref[d] * dt))


def _s5_pow_call(a_re, a_im, log_dt):
    g, p = S5_GROUPS, S5_STATE
    flat = lambda a: a.reshape(2, 1, g * p)
    sds = jax.ShapeDtypeStruct((2, N_POW, g * p), F32)
    pr, pi = pl.pallas_call(_s5_pow_kernel, out_shape=[sds, sds], name="s5_powers")(
        flat(a_re), flat(a_im), flat(jnp.repeat(log_dt, p, axis=1)))
    by_group = lambda a: a.reshape(2, N_POW, g, p).transpose(0, 2, 1, 3)
    return by_group(pr), by_group(pi)


def _s5_prep_kernel(are_r, aim_r, pr_ref, pi_ref, qr_ref, qi_ref, bre_ref, bim_ref, cre_ref, cim_ref,
                    w1_ref, e_ref, dec_ref):
    t, n, p = CHUNK, S5_GROUP, S5_STATE
    width = t * n
    lane_tok = lax.broadcasted_iota(jnp.int32, (N_POW, width), 1) // n
    pow_id = lax.broadcasted_iota(jnp.int32, (N_POW, width), 0)
    onehot = lambda cond: jnp.where(cond, 1.0, 0.0).astype(BF16)
    sel_fwd = onehot(pow_id == lane_tok)
    sel_rev = onehot(pow_id == t - 1 - lane_tok)
    sel_out = onehot(pow_id == t - lane_tok)
    row_tok = lax.broadcasted_iota(jnp.int32, (width, N_POW), 0) // n
    row_pow = lax.broadcasted_iota(jnp.int32, (width, N_POW), 1)
    rsel_rev = onehot(row_pow == t - 1 - row_tok)
    rsel_fwd = onehot(row_pow == row_tok)
    lane = lax.broadcasted_iota(jnp.int32, (n, width), 1)
    for gi in range(w1_ref.shape[0]):
        _s5_prep_group(gi, are_r, aim_r, pr_ref, pi_ref, qr_ref, qi_ref, bre_ref, bim_ref, cre_ref, cim_ref,
                       w1_ref, e_ref, dec_ref, (sel_fwd, sel_rev, sel_out, rsel_rev, rsel_fwd, lane))


def _s5_prep_group(gi, are_r, aim_r, pr_ref, pi_ref, qr_ref, qi_ref, bre_ref, bim_ref, cre_ref, cim_ref,
                   w1_ref, e_ref, dec_ref, selectors):
    sel_fwd, sel_rev, sel_out, rsel_rev, rsel_fwd, lane = selectors
    t, n = CHUNK, S5_GROUP
    width = t * n
    strips = []
    f_parts = []
    e_parts = []
    for d in range(2):
        qr = qr_ref[d, gi]
        qi = qi_ref[d, gi]
        pr = pr_ref[d, gi]
        pi = pi_ref[d, gi]
        a_re = are_r[d, gi]
        a_im = aim_r[d, gi]
        nr = pr[1:2] - 1.0
        ni = pi[1:2]
        den = a_re * a_re + a_im * a_im
        fre = (nr * a_re + ni * a_im) / den
        fim = (ni * a_re - nr * a_im) / den
        bt_re = bre_ref[d, gi].T
        bt_im = bim_ref[d, gi].T
        bb_re, bb_im = _cmul(fre, fim, bt_re, bt_im)
        ct_re = jnp.concatenate([cre_ref[d, gi].T] * t, axis=1)
        ct_im = jnp.concatenate([cim_ref[d, gi].T] * t, axis=1)
        sel = sel_fwd if d == 0 else sel_rev
        w_re, w_im = _cmul(ct_re, ct_im, _select_cols(qr, sel), _select_cols(qi, sel))
        strips.append(_dot3(bb_re, w_re) - _dot3(bb_im, w_im))
        if d == 0:
            o_re, o_im = _cmul(w_re, w_im, qr[:, 1:2], qi[:, 1:2])
        else:
            o_re, o_im = _cmul(ct_re, ct_im, _select_cols(qr, sel_out), _select_cols(qi, sel_out))
        e_parts += [o_re, -o_im]
        rsel = rsel_rev if d == 0 else rsel_fwd
        f_re, f_im = _cmul(jnp.concatenate([bb_re] * t, axis=0), jnp.concatenate([bb_im] * t, axis=0),
                           _select_rows(rsel, pr), _select_rows(rsel, pi))
        f_parts += [f_re, f_im]
        dec_ref[gi, 2 * d:2 * d + 1, :] = jnp.concatenate([pr[t:t + 1], pr[t:t + 1]], axis=1)
        dec_ref[gi, 2 * d + 1:2 * d + 2, :] = jnp.concatenate([-pi[t:t + 1], pi[t:t + 1]], axis=1)

    blocks = []
    for s in range(t):
        fwd = strips[0] if s == 0 else jnp.where(lane >= n * s, pltpu.roll(strips[0], n * s, axis=1), 0.0)
        back = t - 1 - s
        bwd = strips[1] if back == 0 else jnp.where(lane < width - n * back,
                                                     pltpu.roll(strips[1], width - n * back, axis=1), 0.0)
        blocks.append(fwd + bwd)
    m = jnp.concatenate(blocks, axis=0)
    w1_ref[gi] = jnp.concatenate([m] + f_parts, axis=1).astype(BF16)
    e_ref[gi] = jnp.concatenate(e_parts, axis=0).astype(BF16)


def _s5_operators(a_re, a_im, log_dt, b_re, b_im, c_re, c_im):
    g, p, n, t = S5_GROUPS, S5_STATE, S5_GROUP, CHUNK
    pr, pi = _s5_pow_call(a_re, a_im, log_dt)
    gb = PREP_GROUPS
    spec = lambda shape: pl.BlockSpec((2, gb) + shape, lambda i: (0, i) + (0,) * len(shape))
    return pl.pallas_call(
        _s5_prep_kernel, grid=(g // gb,),
        in_specs=[spec((1, p)), spec((1, p)), spec((N_POW, p)), spec((N_POW, p)), spec((p, N_POW)),
                  spec((p, N_POW)), spec((p, n)), spec((p, n)), spec((n, p)), spec((n, p))],
        out_specs=[pl.BlockSpec((gb, t * n, 2 * t * n), lambda i: (i, 0, 0)),
                   pl.BlockSpec((gb, 4 * p, t * n), lambda i: (i, 0, 0)),
                   pl.BlockSpec((gb, 4, 2 * p), lambda i: (i, 0, 0))],
        out_shape=[jax.ShapeDtypeStruct((g, t * n, 2 * t * n), BF16),
                   jax.ShapeDtypeStruct((g, 4 * p, t * n), BF16),
                   jax.ShapeDtypeStruct((g, 4, 2 * p), F32)],
        compiler_params=_cparams(("parallel",)),
        name="s5_prep",
    )(a_re.reshape(2, g, 1, p), a_im.reshape(2, g, 1, p), pr, pi, pr.transpose(0, 1, 3, 2),
      pi.transpose(0, 1, 3, 2), b_re, b_im, c_re, c_im)


def _block_transpose8(ps):
    ps = list(ps)
    blk = lax.broadcasted_iota(jnp.int32, ps[0].shape, 1) // S5_GROUP
    for k in range(3):
        step = 1 << k
        shift = S5_GROUP * step
        keep = ((blk >> k) & 1) == 0
        for a in range(8):
            if a & step:
                continue
            pa, pb = ps[a], ps[a + step]
            ps[a] = jnp.where(keep, pa, pltpu.roll(pb, shift, axis=1))
            ps[a + step] = jnp.where(keep, pltpu.roll(pa, 128 - shift, axis=1), pb)
    return ps


def _s5a_kernel(u_ref, w1_ref, yin_ref, gf_ref, gb_ref):
    nb = u_ref.shape[1]
    xs = []
    for s in range(CHUNK):
        parts = [u_ref[0, b, pl.ds(s, TILE_CHUNKS, stride=CHUNK), :] for b in range(nb)]
        xs.append(jnp.concatenate(parts, axis=0))
    lo = _block_transpose8(xs[:8])
    hi = _block_transpose8(xs[8:])
    for j in range(GRP_PER_BLK):
        og = jnp.concatenate([lo[j], hi[j]], axis=1)
        r = _dot(og.astype(BF16), w1_ref[j])
        yin_ref[j] = r[:, 0:256].astype(yin_ref.dtype)
        gf_ref[j] = r[:, 256:384]
        gb_ref[j] = r[:, 384:512]


def _s5a_call(u, w1):
    n_blk, b, s, _ = u.shape
    nt = s // TILE_TOK
    rows = b * TILE_CHUNKS
    out_spec = lambda n: pl.BlockSpec((GRP_PER_BLK, rows, n), lambda l, j: (l, j, 0))
    return pl.pallas_call(
        _s5a_kernel, grid=(n_blk, nt),
        in_specs=[pl.BlockSpec((1, b, TILE_TOK, LANE_BLK), lambda l, j: (l, 0, j, 0)),
                  pl.BlockSpec((GRP_PER_BLK, 256, 512), lambda l, j: (l, 0, 0))],
        out_specs=[out_spec(256), out_spec(128), out_spec(128)],
        out_shape=[jax.ShapeDtypeStruct((S5_GROUPS, nt * rows, 256), BF16),
                   jax.ShapeDtypeStruct((S5_GROUPS, nt * rows, 128), F32),
                   jax.ShapeDtypeStruct((S5_GROUPS, nt * rows, 128), F32)],
        compiler_params=_cparams(("parallel", "parallel")),
        name="s5_chunk_in",
    )(u, w1)


def _s5b_kernel(nb, a_ref, gfc_ref, gfl_ref, gbc_ref, gbl_ref, hfc_ref, hfl_ref, hbc_ref, hbl_ref):
    gb = a_ref.shape[0]
    rows = nb * TILE_CHUNKS
    n_lat = gfl_ref.shape[1] // rows
    a1f = [jnp.broadcast_to(a_ref[g, 0:1, :], (nb, 128)) for g in range(gb)]
    a2f = [jnp.broadcast_to(a_ref[g, 1:2, :], (nb, 128)) for g in range(gb)]
    a1b = [jnp.broadcast_to(a_ref[g, 2:3, :], (nb, 128)) for g in range(gb)]
    a2b = [jnp.broadcast_to(a_ref[g, 3:4, :], (nb, 128)) for g in range(gb)]

    def step(state, a1, a2, g_ref, h_ref, g, row):
        h, hs = state
        h_ref[g, row, :] = h
        inp = g_ref[g, row, :]
        return a1 * h + a2 * hs + inp, a1 * hs - a2 * h + pltpu.roll(inp, 64, axis=1)

    zero = jnp.zeros((nb, 128), F32)
    hf = [(zero, zero) for _ in range(gb)]
    hb = [(zero, zero) for _ in range(gb)]
    for ci in range(TILE_CHUNKS):
        rf = pl.ds(ci, nb, stride=TILE_CHUNKS)
        rb = pl.ds(TILE_CHUNKS - 1 - ci, nb, stride=TILE_CHUNKS)
        for g in range(gb):
            hf[g] = step(hf[g], a1f[g], a2f[g], gfc_ref, hfc_ref, g, rf)
            hb[g] = step(hb[g], a1b[g], a2b[g], gbc_ref, hbc_ref, g, rb)

    def body(j, carry):
        hf, hb = carry
        hf = list(hf)
        hb = list(hb)
        base_f = j * rows
        base_b = (n_lat - 1 - j) * rows
        for ci in range(TILE_CHUNKS):
            rf = pl.ds(base_f + ci, nb, stride=TILE_CHUNKS)
            rb = pl.ds(base_b + (TILE_CHUNKS - 1 - ci), nb, stride=TILE_CHUNKS)
            for g in range(gb):
                hf[g] = step(hf[g], a1f[g], a2f[g], gfl_ref, hfl_ref, g, rf)
                hb[g] = step(hb[g], a1b[g], a2b[g], gbl_ref, hbl_ref, g, rb)
        return tuple(hf), tuple(hb)

    lax.fori_loop(0, n_lat, body, (tuple(hf), tuple(hb)))


def _s5b_call(decay, gf_c, gf_l, gb_c, gb_l, nb):
    gblk = 4
    spec = lambda a: pl.BlockSpec((gblk, a.shape[1], 128), lambda i: (i, 0, 0))
    sds = lambda a: jax.ShapeDtypeStruct(a.shape, F32)
    return pl.pallas_call(
        functools.partial(_s5b_kernel, nb), grid=(S5_GROUPS // gblk,),
        in_specs=[pl.BlockSpec((gblk, 4, 128), lambda i: (i, 0, 0)),
                  spec(gf_c), spec(gf_l), spec(gb_c), spec(gb_l)],
        out_specs=[spec(gf_c), spec(gf_l), spec(gb_c), spec(gb_l)],
        out_shape=[sds(gf_c), sds(gf_l), sds(gb_c), sds(gb_l)],
        compiler_params=_cparams(("parallel",)),
        name="s5_state_scan",
    )(decay, gf_c, gf_l, gb_c, gb_l)


def _s5c_kernel(yin_ref, hf_ref, hb_ref, e_ref, u_ref, d_ref, y_ref):
    nb = u_ref.shape[1]
    ys = []
    for j in range(GRP_PER_BLK):
        h = jnp.concatenate([hf_ref[j], hb_ref[j]], axis=1).astype(BF16)
        ys.append(yin_ref[j].astype(F32) + _dot(h, e_ref[j]))
    at = (_block_transpose8([y[:, :128] for y in ys])
          + _block_transpose8([y[:, 128:] for y in ys]))
    d = d_ref[...]
    for t in range(CHUNK):
        for b in range(nb):
            rows = pl.ds(t, TILE_CHUNKS, stride=CHUNK)
            y_ref[0, b, rows, :] = at[t][b * TILE_CHUNKS:(b + 1) * TILE_CHUNKS] + d * u_ref[0, b, rows, :]


def _s5c_call(yin, hf, hb, e, u, d_skip):
    n_blk, b, s, _ = u.shape
    nt = s // TILE_TOK
    rows = b * TILE_CHUNKS
    gspec = lambda n: pl.BlockSpec((GRP_PER_BLK, rows, n), lambda l, j: (l, j, 0))
    tok_spec = pl.BlockSpec((1, b, TILE_TOK, LANE_BLK), lambda l, j: (l, 0, j, 0))
    return pl.pallas_call(
        _s5c_kernel, grid=(n_blk, nt),
        in_specs=[gspec(256), gspec(128), gspec(128),
                  pl.BlockSpec((GRP_PER_BLK, 256, 256), lambda l, j: (l, 0, 0)),
                  tok_spec,
                  pl.BlockSpec((1, LANE_BLK), lambda l, j: (0, l))],
        out_specs=tok_spec,
        out_shape=_lane_block_shape(b, s),
        compiler_params=_cparams(("parallel", "parallel")),
        name="s5_chunk_out",
    )(yin, hf, hb, e, u, d_skip.reshape(1, D_S5))


def _gelu_tanh(x):
    return 0.5 * x * (1.0 + jnp.tanh(math.sqrt(2.0 / math.pi) * (x + 0.044715 * (x * x * x))))


def _route(logits):
    lane = lax.broadcasted_iota(jnp.int32, logits.shape, 1).astype(F32)
    neg = jnp.float32(-1e30)
    big = jnp.float32(1e9)
    gl = jnp.where(lane < N_GROUPS, logits, neg)
    gmax = jnp.max(gl, axis=1, keepdims=True)
    gidx = jnp.min(jnp.where(gl == gmax, lane, big), axis=1, keepdims=True)
    gsum = jnp.sum(jnp.exp(gl - gmax), axis=1, keepdims=True)
    gw = 1.0 / gsum
    lo = N_GROUPS + EXP_PER_GROUP * gidx
    el = jnp.where((lane >= lo) & (lane < lo + EXP_PER_GROUP), logits, neg)
    v1 = jnp.max(el, axis=1, keepdims=True)
    i1 = jnp.min(jnp.where(el == v1, lane, big), axis=1, keepdims=True)
    el2 = jnp.where(lane == i1, neg, el)
    v2 = jnp.max(el2, axis=1, keepdims=True)
    i2 = jnp.min(jnp.where(el2 == v2, lane, big), axis=1, keepdims=True)
    ex = jnp.exp(v2 - v1)
    p1 = 1.0 / (1.0 + ex)
    p2 = ex * p1
    e1 = i1 - lo
    e2 = i2 - lo
    first = e1 < e2
    ea = jnp.where(first, e1, e2)
    eb = jnp.where(first, e2, e1)
    wa = gw * jnp.where(first, p1, p2)
    wb = gw * jnp.where(first, p2, p1)
    pair = ea * (7.0 - ea) * 0.5 + (eb - ea - 1.0)
    return wa, wb, 6.0 * gidx + pair


def _out_kernel(ypre_ref, ysc_ref, ycf_ref, x_ref, mod_ref, wglu_ref, bglu_ref, wo_ref,
                lng_ref, lnb_ref, wr_ref, br_ref, cnt0_ref, x1_ref, hx_ref, meta_ref, counts_ref, *rest):
    stage_ref = rest[0] if len(rest) == 2 else None
    cnt_ref = rest[-1]

    @pl.when((pl.program_id(0) == 0) & (pl.program_id(1) == 0))
    def _():
        cnt_ref[...] = cnt0_ref[...]

    tm = x_ref.shape[1]
    sub = tm // OUT_SUBTILES

    def row_chain(r0):
        rows = slice(r0, r0 + sub)
        ypre = jnp.concatenate([ypre_ref[blk, 0, rows, :] for blk in range(ypre_ref.shape[0])], axis=1)
        t = _gelu_tanh(ypre)
        gate = _sigmoid(_dot(t.astype(BF16), wglu_ref[...]) + bglu_ref[...])
        ys5 = (t * gate).astype(BF16)
        y = (_dot(ys5, wo_ref[0:D_S5, :]) + _dot(ysc_ref[0, rows, :], wo_ref[D_S5:D_S5 + D_SC, :])
             + _dot(ycf_ref[0, rows, :], wo_ref[D_S5 + D_SC:D_MODEL, :]))
        g1 = mod_ref[0, 2:3, :]
        x1 = _layer_norm(DN_ALPHA * x_ref[0, rows, :] + g1 * y, lng_ref[...], lnb_ref[...])
        x1_ref[0, rows, :] = x1
        h2 = x1 * (1.0 + mod_ref[0, 4:5, :]) + mod_ref[0, 3:4, :]
        hx_ref[0, rows, 0:D_MODEL] = h2
        return _route(_dot(h2.astype(BF16), wr_ref[...]) + br_ref[...])

    routed = [row_chain(r0) for r0 in range(0, tm, sub)]
    wa, wb, cls = (jnp.concatenate([r[i] for r in routed], axis=0) for i in range(3))

    lane = lax.broadcasted_iota(jnp.int32, (tm, ROUTER_LANES), 1).astype(F32)
    onehot = jnp.where(lane == cls, 1.0, 0.0)
    row_i = lax.broadcasted_iota(jnp.int32, (tm, tm), 0)
    col_i = lax.broadcasted_iota(jnp.int32, (tm, tm), 1)
    earlier = jnp.where(col_i < row_i, 1.0, 0.0).astype(BF16)
    before = _dot(earlier, onehot.astype(BF16)) + cnt_ref[...]
    rank = jnp.sum(before * onehot, axis=1, keepdims=True)
    cnt_ref[...] += jnp.sum(onehot, axis=0, keepdims=True)
    counts_ref[...] = cnt_ref[...]

    meta = (jnp.where(lane == META_WA, wa, 0.0) + jnp.where(lane == META_WB, wb, 0.0)
            + jnp.where(lane == META_CLS, cls, 0.0) + jnp.where(lane == META_RANK, rank, 0.0))
    meta_ref[...] = jnp.transpose(meta)[0:SUBLANES, :]
    hx_ref[0, :, D_MODEL:HX_LANES] = meta
    if stage_ref is not None:
        stage_ref[...] = jnp.zeros_like(stage_ref)


def _out_call(ypre, ysc, ycf, x, mod, wglu_bf, b_glu, wo_bf, ln_g, ln_b, w_router, b_router, counts0, tm,
              stage_rows):
    b, s, d = x.shape
    nt = s // tm
    row_spec = lambda n: pl.BlockSpec((1, tm, n), lambda i, j: (i, j, 0))
    full = lambda shape: pl.BlockSpec(shape, lambda i, j: (0,) * len(shape))
    out_specs = [row_spec(d), row_spec(HX_LANES),
                 pl.BlockSpec((SUBLANES, tm), lambda i, j: (0, i * nt + j)),
                 full((1, ROUTER_LANES))]
    out_shape = [jax.ShapeDtypeStruct((b, s, d), F32), jax.ShapeDtypeStruct((b, s, HX_LANES), F32),
                 jax.ShapeDtypeStruct((SUBLANES, b * s), F32),
                 jax.ShapeDtypeStruct((1, ROUTER_LANES), F32)]
    if stage_rows:
        stage_octs = stage_rows // (SUBLANES * b * nt)
        assert stage_octs * SUBLANES * b * nt == stage_rows
        out_specs.append(pl.BlockSpec((stage_octs, SUBLANES, HX_LANES), lambda i, j: (i * nt + j, 0, 0)))
        out_shape.append(jax.ShapeDtypeStruct((stage_rows // SUBLANES, SUBLANES, HX_LANES), F32))
    return pl.pallas_call(
        _out_kernel, grid=(b, nt),
        in_specs=[_lane_block_spec(tm), row_spec(D_SC), row_spec(D_CF), row_spec(d),
                  pl.BlockSpec((1, 6, d), lambda i, j: (i, 0, 0)),
                  full((D_S5, D_S5)), full((1, D_S5)), full((d, d)),
                  full((1, d)), full((1, d)), full((d, ROUTER_LANES)), full((1, ROUTER_LANES)),
                  full((1, ROUTER_LANES))],
        out_specs=out_specs, out_shape=out_shape,
        scratch_shapes=[pltpu.VMEM((1, ROUTER_LANES), F32)],
        compiler_params=_cparams(("arbitrary", "arbitrary")),
        name="out_proj",
    )(ypre, ysc, ycf, x, mod, wglu_bf, b_glu.reshape(1, -1), wo_bf, ln_g.reshape(1, -1),
      ln_b.reshape(1, -1), w_router, b_router, counts0)


def _sorted_rows(n_tok):
    return n_tok + N_CLASSES * MOE_TM


def _moe_plan(meta, counts, n_tok):
    cls = meta[META_CLS].astype(jnp.int32)
    rank = meta[META_RANK].astype(jnp.int32)
    cnt = counts[0, :N_CLASSES].astype(jnp.int32)
    n_tiles = (cnt + (MOE_TM - 1)) // MOE_TM
    ends = jnp.cumsum(n_tiles)
    starts = ends - n_tiles
    slot = starts[cls] * MOE_TM + rank
    t_max = n_tok // MOE_TM + N_CLASSES
    n_used = ends[N_CLASSES - 1]
    tile = jnp.minimum(jnp.arange(t_max, dtype=jnp.int32), n_used - 1)
    tile_cls = jnp.sum((tile[:, None] >= ends[None, :]).astype(jnp.int32), axis=1)
    group = tile_cls // 6
    pair = tile_cls % 6
    first = jnp.array([0, 0, 0, 1, 1, 2], jnp.int32)[pair] + EXP_PER_GROUP * group
    second = jnp.array([1, 2, 3, 2, 3, 3], jnp.int32)[pair] + EXP_PER_GROUP * group
    return slot, tile, first, second, n_used.reshape(1)


def _split_row(row):
    return lax.shift_right_logical(row, 3), lax.bitwise_and(row, SUBLANES - 1)


def _dispatch_kernel(slot_ref, hx_ref, xs_init_ref, xs_ref, sem):
    del xs_init_ref
    n_oct = hx_ref.shape[1]
    base = (pl.program_id(0) * pl.num_programs(1) + pl.program_id(1)) * (n_oct * SUBLANES)

    def body(i, carry):
        for k in range(SUBLANES):
            oct_id, sub = _split_row(slot_ref[base + i * SUBLANES + k])
            pltpu.make_async_copy(hx_ref.at[0, i, pl.ds(k, 1), :], xs_ref.at[oct_id, pl.ds(sub, 1), :],
                                  sem).start(priority=k % 2)
        return carry

    lax.fori_loop(0, n_oct, body, 0)
    pltpu.make_async_copy(hx_ref.at[0], xs_ref.at[pl.ds(0, n_oct)], sem).wait()


def _dispatch_call(slot, hx, xs_init, tm):
    b, s, w = hx.shape
    n_rows = xs_init.shape[0] * SUBLANES
    grid_spec = pltpu.PrefetchScalarGridSpec(
        num_scalar_prefetch=1, grid=(b, s // tm),
        in_specs=[pl.BlockSpec((1, tm // SUBLANES, SUBLANES, w), lambda i, j, slot: (i, j, 0, 0)),
                  pl.BlockSpec(memory_space=pl.ANY)],
        out_specs=pl.BlockSpec(memory_space=pl.ANY),
        scratch_shapes=[pltpu.SemaphoreType.DMA(())])
    xs = pl.pallas_call(
        _dispatch_kernel, grid_spec=grid_spec,
        out_shape=jax.ShapeDtypeStruct(xs_init.shape, F32),
        input_output_aliases={2: 0},
        compiler_params=_cparams(("arbitrary", "arbitrary")),
        name="moe_dispatch",
    )(slot, hx.reshape(b, s // SUBLANES, SUBLANES, w), xs_init)
    return xs.reshape(n_rows, w)


def _moe_kernel(tile_ref, first_ref, second_ref, nused_ref, xs_ref, wga_ref, wgb_ref, wua_ref, wub_ref,
                wda_ref, wdb_ref, ys_ref):
    del tile_ref, first_ref, second_ref
    t = pl.program_id(0)
    n_used = nused_ref[0]

    @pl.when(t < n_used)
    def _():
        x = xs_ref[...]
        xb = x[:, 0:D_MODEL].astype(BF16)

        def expert(wg_ref, wu_ref, wd_ref, w):
            gate = _dot(xb, wg_ref[0])
            up = _dot(xb, wu_ref[0])
            act = gate * _sigmoid(gate) * up * w
            return _dot(act.astype(BF16), wd_ref[0])

        wa = x[:, D_MODEL + META_WA:D_MODEL + META_WA + 1]
        wb = x[:, D_MODEL + META_WB:D_MODEL + META_WB + 1]
        ys_ref[...] = expert(wga_ref, wua_ref, wda_ref, wa) + expert(wgb_ref, wub_ref, wdb_ref, wb)

    @pl.when(t >= n_used)
    def _():
        ys_ref[...] = jnp.zeros_like(ys_ref)


def _moe_call(tile, first, second, n_used, xs, wg_bf, wu_bf, wd_bf):
    n_rows, w = xs.shape
    d = D_MODEL
    t_max = tile.shape[0]
    up_spec = lambda sel: pl.BlockSpec((1, d, D_EXPERT), lambda t, tl, fi, se, nu: ((fi, se)[sel][t], 0, 0))
    down_spec = lambda sel: pl.BlockSpec((1, D_EXPERT, d), lambda t, tl, fi, se, nu: ((fi, se)[sel][t], 0, 0))
    grid_spec = pltpu.PrefetchScalarGridSpec(
        num_scalar_prefetch=4, grid=(t_max,),
        in_specs=[pl.BlockSpec((MOE_TM, w), lambda t, tl, fi, se, nu: (tl[t], 0)),
                  up_spec(0), up_spec(1), up_spec(0), up_spec(1), down_spec(0), down_spec(1)],
        out_specs=pl.BlockSpec((MOE_TM, d), lambda t, tl, fi, se, nu: (t, 0)))
    return pl.pallas_call(
        _moe_kernel, grid_spec=grid_spec,
        out_shape=jax.ShapeDtypeStruct((n_rows, d), F32),
        compiler_params=_cparams(("arbitrary",)),
        name="moe_experts",
    )(tile, first, second, n_used, xs, wg_bf, wg_bf, wu_bf, wu_bf, wd_bf, wd_bf)


def _combine_kernel(slot_ref, x1_ref, mod_ref, lng_ref, lnb_ref, ys_ref, o_ref, f_ref, sem):
    o_ref[0] = _combine_rows(slot_ref, x1_ref, mod_ref, lng_ref, lnb_ref, ys_ref, f_ref, sem)


def _combine_in_kernel(slot_ref, x1_ref, mod_ref, lng_ref, lnb_ref, ys_ref, mod_next_ref, w_ref,
                       o_ref, u_ref, bg_ref, cv_ref, gl_ref, f_ref, sem):
    x2 = _combine_rows(slot_ref, x1_ref, mod_ref, lng_ref, lnb_ref, ys_ref, f_ref, sem)
    o_ref[0] = x2
    _in_body(x2, mod_next_ref, w_ref, u_ref, bg_ref, cv_ref, gl_ref)


def _combine_rows(slot_ref, x1_ref, mod_ref, lng_ref, lnb_ref, ys_ref, f_ref, sem):
    n_oct = f_ref.shape[1]
    tm = n_oct * SUBLANES
    step = pl.program_id(0) * pl.num_programs(1) + pl.program_id(1)
    n_steps = pl.num_programs(0) * pl.num_programs(1)

    def request(which, buf):
        base = which * tm

        def body(i, carry):
            for k in range(SUBLANES):
                oct_id, sub = _split_row(slot_ref[base + i * SUBLANES + k])
                pltpu.make_async_copy(ys_ref.at[oct_id, pl.ds(sub, 1), :], f_ref.at[buf, i, pl.ds(k, 1), :],
                                      sem.at[buf]).start(priority=k % 2)
            return carry

        lax.fori_loop(0, n_oct, body, 0)

    @pl.when(step == 0)
    def _():
        request(0, 0)

    @pl.when(step + 1 < n_steps)
    def _():
        request(step + 1, (step + 1) % 2)

    buf = step % 2
    pltpu.make_async_copy(ys_ref.at[pl.ds(0, n_oct)], f_ref.at[buf], sem.at[buf]).wait()
    f = f_ref[buf].reshape(tm, f_ref.shape[3])
    g2 = mod_ref[0, 5:6, :]
    return _layer_norm(DN_ALPHA * x1_ref[0] + g2 * f, lng_ref[...], lnb_ref[...])


def _combine_call(slot, x1, mod, ln_g, ln_b, ys, tm, next_in=None):
    b, s, d = x1.shape
    ys = ys.reshape(ys.shape[0] // SUBLANES, SUBLANES, d)
    row_spec = lambda n: pl.BlockSpec((1, tm, n), lambda i, j, slot: (i, j, 0))
    mod_spec = pl.BlockSpec((1, 6, d), lambda i, j, slot: (i, 0, 0))
    full = lambda shape: pl.BlockSpec(shape, lambda i, j, slot: (0,) * len(shape))
    in_specs = [row_spec(d), mod_spec, full((1, d)), full((1, d)), pl.BlockSpec(memory_space=pl.ANY)]
    out_specs = [row_spec(d)]
    out_shape = [jax.ShapeDtypeStruct((b, s, d), F32)]
    args = [slot, x1, mod, ln_g.reshape(1, -1), ln_b.reshape(1, -1), ys]
    kern, name = _combine_kernel, "moe_combine"
    if next_in is not None:
        kern, name = _combine_in_kernel, "moe_combine_in_proj"
        in_specs += [mod_spec, full((d, D_IN))]
        args += list(next_in)
        out_specs += [pl.BlockSpec((D_S5 // LANE_BLK, 1, tm, LANE_BLK), lambda i, j, slot: (0, i, j, 0)),
                      row_spec(D_SC), row_spec(D_SC), row_spec(D_CF)]
        out_shape += [_lane_block_shape(b, s)] + [jax.ShapeDtypeStruct((b, s, D_SC), F32)] * 3
    grid_spec = pltpu.PrefetchScalarGridSpec(
        num_scalar_prefetch=1, grid=(b, s // tm), in_specs=in_specs, out_specs=out_specs,
        scratch_shapes=[pltpu.VMEM((2, tm // SUBLANES, SUBLANES, d), F32), pltpu.SemaphoreType.DMA((2,))])
    outs = pl.pallas_call(
        kern, grid_spec=grid_spec, out_shape=out_shape,
        compiler_params=_cparams(("arbitrary", "arbitrary")),
        name=name,
    )(*args)
    return outs[0] if next_in is None else outs


def _moe_sublayer(parts, counts, xs_init, wg_bf, wu_bf, wd_bf, ln_g, ln_b, next_in=None):
    sizes = [p[2].shape[0] * p[2].shape[1] for p in parts]
    meta = jnp.concatenate([p[1] for p in parts], axis=1)
    slot, tile, first, second, n_used = _moe_plan(meta, counts, sum(sizes))
    slots, start = [], 0
    for n in sizes:
        slots.append(slot[start:start + n])
        start += n
    xs = xs_init
    for (hx, _, _, _, tm), sl in zip(parts, slots):
        xs = _dispatch_call(sl, hx, xs.reshape(xs_init.shape), tm)
    ys = _moe_call(tile, first, second, n_used, xs, wg_bf, wu_bf, wd_bf)
    outs = []
    for idx, ((_, _, x1, mod, tm), sl) in enumerate(zip(parts, slots)):
        fuse = next_in if idx == len(parts) - 1 else None
        outs.append(_combine_call(sl, x1, mod, ln_g, ln_b, ys, tm, fuse))
    return outs


def kernel(x, c, ctx, c_ctx, w_mod, b_mod, w_in, s5_a_re, s5_a_im, s5_log_dt, s5_b_re, s5_b_im, s5_c_re, s5_c_im, s5_d, w_glu, b_glu, w_sc, w_dw, b_dw, ln_cf_g, ln_cf_b, w_o, ln1_g, ln1_b, w_rg, b_rg, w_rexp, b_rexp, w_gate, w_up, w_down, ln2_g, ln2_b):
    nb, seq, d = x.shape
    n_ctx = ctx.shape[1]
    n_layers = w_mod.shape[0]
    assert seq % TILE_TOK == 0 and n_ctx % TILE_TOK == 0 and seq % GRID_W == 0

    mod_rows = 16
    assert nb + 1 <= mod_rows
    c_all = jnp.concatenate([c, c_ctx[None, :], jnp.zeros((mod_rows - nb - 1, d), F32)], axis=0)
    mod_all = _mod_call(c_all, w_mod, b_mod)

    pad_r = ROUTER_LANES - N_GROUPS - N_EXPERTS
    x_lat, x_ctx = x, ctx
    mods_lat = [mod_all[l, :nb].reshape(nb, 6, d) for l in range(n_layers)]
    w_in_bfs = [w_in[l].astype(BF16) for l in range(n_layers)]
    lat_proj = None
    for l in range(n_layers):
        last = l == n_layers - 1
        mod_lat = mods_lat[l]
        mod_ctx = jnp.broadcast_to(mod_all[l, nb].reshape(1, 6, d), (nb, 6, d))
        w_in_bf = w_in_bfs[l]
        wglu_bf = w_glu[l].astype(BF16)
        wo_bf = w_o[l].astype(BF16)
        wg_bf = w_gate[l].astype(BF16)
        wu_bf = w_up[l].astype(BF16)
        wd_bf = w_down[l].astype(BF16)
        w_router = jnp.concatenate([w_rg[l], w_rexp[l], jnp.zeros((d, pad_r), F32)], axis=1).astype(BF16)
        b_router = jnp.concatenate([b_rg[l], b_rexp[l], jnp.zeros((pad_r,), F32)]).reshape(1, -1)
        w1, e_op, decay = _s5_operators(s5_a_re[l], s5_a_im[l], s5_log_dt[l], s5_b_re[l], s5_b_im[l],
                                        s5_c_re[l], s5_c_im[l])

        if lat_proj is None:
            u_l, bg_l, cv_l, gl_l = _in_call(x_lat, mod_lat, w_in_bf, min(ROW_TILE_BIG, seq), False)
        else:
            u_l, bg_l, cv_l, gl_l = lat_proj
        if last:
            u_c = _in_call(x_ctx, mod_ctx, w_in_bf[:, :D_S5], TILE_TOK, True)
        else:
            u_c, bg_c, cv_c, gl_c = _in_call(x_ctx, mod_ctx, w_in_bf, TILE_TOK, False)

        yin_l, gf_l, gb_l = _s5a_call(u_l, w1)
        yin_c, gf_c, gb_c = _s5a_call(u_c, w1)
        hf_c, hf_l, hb_c, hb_l = _s5b_call(decay, gf_c, gf_l, gb_c, gb_l, nb)
        ypre_l = _s5c_call(yin_l, hf_l, hb_l, e_op, u_l, s5_d[l])

        ysc_l, ycf_l = _conv_call(bg_l, cv_l, gl_l, w_sc[l], w_dw[l], b_dw[l], ln_cf_g[l], ln_cf_b[l], True)
        parts = []
        counts = jnp.zeros((1, ROUTER_LANES), F32)
        n_moe = nb * seq
        if not last:
            ypre_c = _s5c_call(yin_c, hf_c, hb_c, e_op, u_c, s5_d[l])
            ysc_c, ycf_c = _conv_call(bg_c, cv_c, gl_c, w_sc[l], w_dw[l], b_dw[l], ln_cf_g[l], ln_cf_b[l], False)
            x1_c, hx_c, meta_c, counts = _out_call(ypre_c, ysc_c, ycf_c, x_ctx, mod_ctx, wglu_bf, b_glu[l], wo_bf,
                                                   ln1_g[l], ln1_b[l], w_router, b_router, counts, TILE_TOK, 0)
            parts.append((hx_c, meta_c, x1_c, mod_ctx, TILE_TOK))
            n_moe += nb * n_ctx
        x1_l, hx_l, meta_l, counts, stage = _out_call(ypre_l, ysc_l, ycf_l, x_lat, mod_lat, wglu_bf, b_glu[l], wo_bf,
                                                      ln1_g[l], ln1_b[l], w_router, b_router, counts, 512,
                                                      _sorted_rows(n_moe))
        parts.append((hx_l, meta_l, x1_l, mod_lat, min(ROW_TILE_BIG, seq)))
        next_in = None if last else (mods_lat[l + 1], w_in_bfs[l + 1])
        outs = _moe_sublayer(parts, counts, stage, wg_bf, wu_bf, wd_bf, ln2_g[l], ln2_b[l], next_in)
        if last:
            x_lat = outs[-1]
        else:
            x_ctx = outs[0]
            x_lat, lat_proj = outs[-1][0], outs[-1][1:]
    return x_lat
```

```python
import functools
import math

import jax
import jax.numpy as jnp
from jax import lax
from jax.experimental import pallas as pl
from jax.experimental.pallas import tpu as pltpu

F32 = jnp.float32
BF16 = jnp.bfloat16

D_MODEL = 1024
DEPTH = 2
GRID_W = 64
D_S5 = 512
S5_GROUP = 16
S5_GROUPS = 32
S5_STATE = 64
D_SC = 256
D_CF = 256
CF_WIDTH = 31
CF_HALF = 15
D_IN = 1792
N_GROUPS = 4
EXP_PER_GROUP = 4
N_EXPERTS = 16
D_EXPERT = 256
DN_ALPHA = (2 * DEPTH) ** 0.25
LN_EPS = 1e-5

CHUNK = 16
N_POW = 32
PREP_GROUPS = 8
TILE_CHUNKS = 16
TILE_TOK = CHUNK * TILE_CHUNKS
LANE_BLK = 128
GRP_PER_BLK = LANE_BLK // S5_GROUP
ROUTER_LANES = 128
HX_LANES = D_MODEL + ROUTER_LANES
META_WA, META_WB, META_CLS, META_RANK = 0, 1, 2, 3
N_CLASSES = N_GROUPS * 6
MOE_TM = 512
OUT_SUBTILES = 2
ROW_TILE_BIG = 1024
SUBLANES = 8
VMEM_LIMIT = 56 * 1024 * 1024


def _cparams(sem):
    return pltpu.CompilerParams(dimension_semantics=sem, vmem_limit_bytes=VMEM_LIMIT)


def _split_bf16(a):
    hi = a.astype(BF16)
    lo = (a - hi.astype(F32)).astype(BF16)
    return hi, lo


def _dot(a, b):
    return jnp.dot(a, b, preferred_element_type=F32)


def _dot3(a, b):
    ah, al = _split_bf16(a)
    bh, bl = _split_bf16(b)
    return _dot(ah, bh) + (_dot(al, bh) + _dot(ah, bl))


def _sigmoid(x):
    return 1.0 / (1.0 + jnp.exp(-x))


def _layer_norm(x, g, b):
    mu = jnp.mean(x, axis=-1, keepdims=True)
    xc = x - mu
    var = jnp.mean(xc * xc, axis=-1, keepdims=True)
    return xc * lax.rsqrt(var + LN_EPS) * g + b


def _mod_kernel(c_ref, w_ref, b_ref, o_ref):
    c = c_ref[...]
    s = c * _sigmoid(c)
    o_ref[0] = _dot3(s, w_ref[0]) + b_ref[0]


def _mod_call(c_all, w_mod, b_mod):
    n_layers, d, n_out = w_mod.shape
    tn = 1536
    rows = c_all.shape[0]
    return pl.pallas_call(
        _mod_kernel,
        grid=(n_layers, n_out // tn),
        in_specs=[
            pl.BlockSpec((rows, d), lambda l, j: (0, 0)),
            pl.BlockSpec((1, d, tn), lambda l, j: (l, 0, j)),
            pl.BlockSpec((1, 1, tn), lambda l, j: (l, 0, j)),
        ],
        out_specs=pl.BlockSpec((1, rows, tn), lambda l, j: (l, 0, j)),
        out_shape=jax.ShapeDtypeStruct((n_layers, rows, n_out), F32),
        compiler_params=_cparams(("parallel", "parallel")),
        name="mod",
    )(c_all, w_mod, b_mod.reshape(n_layers, 1, n_out))


def _in_body(x, mod_ref, w_ref, u_ref, bg_ref, cv_ref, gl_ref):
    sh = mod_ref[0, 0:1, :]
    sc = mod_ref[0, 1:2, :]
    h = (x * (1.0 + sc) + sh).astype(BF16)
    z = _dot(h, w_ref[...])
    _store_lane_blocks(u_ref, z[:, 0:512])
    bg_ref[0] = z[:, 512:768]
    cv_ref[0] = z[:, 768:1024] * z[:, 1024:1280]
    gl_ref[0] = z[:, 1280:1536] * _sigmoid(z[:, 1536:1792])


def _in_kernel(x_ref, mod_ref, w_ref, u_ref, bg_ref, cv_ref, gl_ref):
    _in_body(x_ref[0], mod_ref, w_ref, u_ref, bg_ref, cv_ref, gl_ref)


def _in_u_kernel(x_ref, mod_ref, w_ref, u_ref):
    x = x_ref[0]
    sh = mod_ref[0, 0:1, :]
    sc = mod_ref[0, 1:2, :]
    h = (x * (1.0 + sc) + sh).astype(BF16)
    _store_lane_blocks(u_ref, _dot(h, w_ref[...]))


def _store_lane_blocks(ref, val):
    for blk in range(ref.shape[0]):
        ref[blk, 0] = val[:, blk * LANE_BLK:(blk + 1) * LANE_BLK]


def _lane_block_spec(tm):
    return pl.BlockSpec((D_S5 // LANE_BLK, 1, tm, LANE_BLK), lambda i, j: (0, i, j, 0))


def _lane_block_shape(b, s):
    return jax.ShapeDtypeStruct((D_S5 // LANE_BLK, b, s, LANE_BLK), F32)


def _in_call(x, mod, w_in_bf, tm, u_only):
    b, s, d = x.shape
    grid = (b, s // tm)
    row_spec = lambda n: pl.BlockSpec((1, tm, n), lambda i, j: (i, j, 0))
    in_specs = [
        row_spec(d),
        pl.BlockSpec((1, 6, d), lambda i, j: (i, 0, 0)),
    ]
    if u_only:
        in_specs.append(pl.BlockSpec((d, D_S5), lambda i, j: (0, 0)))
        return pl.pallas_call(
            _in_u_kernel, grid=grid, in_specs=in_specs,
            out_specs=_lane_block_spec(tm),
            out_shape=_lane_block_shape(b, s),
            compiler_params=_cparams(("parallel", "parallel")),
            name="in_proj_u",
        )(x, mod, w_in_bf)
    in_specs.append(pl.BlockSpec((d, D_IN), lambda i, j: (0, 0)))
    return pl.pallas_call(
        _in_kernel, grid=grid, in_specs=in_specs,
        out_specs=[_lane_block_spec(tm), row_spec(D_SC), row_spec(D_SC), row_spec(D_CF)],
        out_shape=[_lane_block_shape(b, s),
                   jax.ShapeDtypeStruct((b, s, D_SC), F32),
                   jax.ShapeDtypeStruct((b, s, D_SC), F32),
                   jax.ShapeDtypeStruct((b, s, D_CF), F32)],
        compiler_params=_cparams(("parallel", "parallel")),
        name="in_proj",
    )(x, mod, w_in_bf)


def _conv_tail(t, bdw_ref, lng_ref, lnb_ref):
    t = t + bdw_ref[...]
    t = _layer_norm(t, lng_ref[...], lnb_ref[...])
    return t * _sigmoid(t)


def _conv_grid_kernel(bg_ref, cv_ref, gl_ref, wsc_ref, wdw_ref, bdw_ref, lng_ref, lnb_ref,
                      ysc_ref, ycf_ref, grid_ref, t_ref):
    s = cv_ref.shape[1]
    rows = s // GRID_W
    cv = cv_ref[0]
    col = lax.broadcasted_iota(jnp.int32, (s, D_SC), 0) % GRID_W
    prev = jnp.where(col == 0, 0.0, pltpu.roll(cv, 1, axis=0))
    nxt = jnp.where(col == GRID_W - 1, 0.0, pltpu.roll(cv, s - 1, axis=0))
    conv = prev * wsc_ref[0:1, :] + cv * wsc_ref[1:2, :] + nxt * wsc_ref[2:3, :]
    ysc_ref[0] = (bg_ref[0] * conv).astype(ysc_ref.dtype)

    grid_ref[...] = gl_ref[0].reshape(rows, GRID_W, D_CF)

    def body(i, carry):
        w0 = pl.multiple_of(i * 8, 8)
        for half in range(D_CF // 128):
            lanes = slice(half * 128, (half + 1) * 128)
            acc = jnp.zeros((rows, 8, 128), F32)
            for k in range(CF_WIDTH):
                lo = max(0, CF_HALF - k)
                hi = min(rows, rows + CF_HALF - k)
                if hi <= lo:
                    continue
                term = grid_ref[lo + k - CF_HALF:hi + k - CF_HALF, pl.ds(w0, 8), lanes] * wdw_ref[k:k + 1, lanes]
                pieces = [acc[:lo]] * (lo > 0) + [acc[lo:hi] + term] + [acc[hi:]] * (hi < rows)
                acc = jnp.concatenate(pieces, axis=0) if len(pieces) > 1 else pieces[0]
            t_ref[:, pl.ds(w0, 8), lanes] = acc
        return carry

    lax.fori_loop(0, GRID_W // 8, body, 0)
    t = t_ref[...].reshape(s, D_CF)
    ycf_ref[0] = _conv_tail(t, bdw_ref, lng_ref, lnb_ref).astype(ycf_ref.dtype)


def _conv_seq_kernel(bg_ref, cv_ref, gl_ref, wsc_ref, wdw_ref, bdw_ref, lng_ref, lnb_ref,
                     ysc_ref, ycf_ref, pad_ref):
    s = cv_ref.shape[1]
    cv = cv_ref[0]
    pos = lax.broadcasted_iota(jnp.int32, (s, D_SC), 0)
    prev = jnp.where(pos == 0, 0.0, pltpu.roll(cv, 1, axis=0))
    nxt = jnp.where(pos == s - 1, 0.0, pltpu.roll(cv, s - 1, axis=0))
    conv = prev * wsc_ref[0:1, :] + cv * wsc_ref[1:2, :] + nxt * wsc_ref[2:3, :]
    ysc_ref[0] = (bg_ref[0] * conv).astype(ysc_ref.dtype)

    off = 16
    pad_ref[0:off] = jnp.zeros((off, D_CF), F32)
    pad_ref[off + s:off + s + 16] = jnp.zeros((16, D_CF), F32)
    pad_ref[off:off + s] = gl_ref[0]
    acc = jnp.zeros((s, D_CF), F32)
    for k in range(CF_WIDTH):
        acc = acc + pad_ref[pl.ds(off - CF_HALF + k, s), :] * wdw_ref[k:k + 1, :]
    ycf_ref[0] = _conv_tail(acc, bdw_ref, lng_ref, lnb_ref).astype(ycf_ref.dtype)


def _conv_call(bg, cv, gl, w_sc, w_dw, b_dw, ln_g, ln_b, grid_mode):
    b, s, _ = bg.shape
    row_spec = pl.BlockSpec((1, s, D_SC), lambda i: (i, 0, 0))
    full = lambda shape: pl.BlockSpec(shape, lambda i: (0,) * len(shape))
    if grid_mode:
        rows = s // GRID_W
        kern = _conv_grid_kernel
        scratch = [pltpu.VMEM((rows, GRID_W, D_CF), F32), pltpu.VMEM((rows, GRID_W, D_CF), F32)]
        name = "conv_grid"
    else:
        kern = _conv_seq_kernel
        scratch = [pltpu.VMEM((s + 32, D_CF), F32)]
        name = "conv_seq"
    return pl.pallas_call(
        kern, grid=(b,),
        in_specs=[row_spec, row_spec, row_spec, full((3, D_SC)), full((CF_WIDTH, D_CF)),
                  full((1, D_CF)), full((1, D_CF)), full((1, D_CF))],
        out_specs=[row_spec, row_spec],
        out_shape=[jax.ShapeDtypeStruct((b, s, D_SC), BF16), jax.ShapeDtypeStruct((b, s, D_CF), BF16)],
        scratch_shapes=scratch,
        compiler_params=_cparams(("parallel",)),
        name=name,
    )(bg, cv, gl, w_sc, w_dw, b_dw.reshape(1, -1), ln_g.reshape(1, -1), ln_b.reshape(1, -1))


def _split3(a):
    hi = a.astype(BF16)
    r = a - hi.astype(F32)
    mid = r.astype(BF16)
    lo = (r - mid.astype(F32)).astype(BF16)
    return hi, mid, lo


def _select_cols(a, sel):
    hi, mid, lo = _split3(a)
    return _dot(hi, sel) + (_dot(mid, sel) + _dot(lo, sel))


def _select_rows(sel, a):
    hi, mid, lo = _split3(a)
    return _dot(sel, hi) + (_dot(sel, mid) + _dot(sel, lo))


def _cmul(ar, ai, br, bi):
    return ar * br - ai * bi, ar * bi + ai * br


def _s5_pow_kernel(are_ref, aim_ref, ldt_ref, pr_ref, pi_ref):
    j = jnp.minimum(lax.broadcasted_iota(jnp.int32, pr_ref.shape[1:], 0), CHUNK).astype(F32)
    for d in range(2):
        dt = jnp.exp(ldt_ref[d])
        e = jnp.exp(j * (are_ref[d] * dt))
        pr_ref[d] = e * jnp.cos(j * (aim_ref[d] * dt))
        pi_ref[d] = e * jnp.sin(j * (aim_ref[d] * dt))


def _s5_pow_call(a_re, a_im, log_dt):
    g, p = S5_GROUPS, S5_STATE
    flat = lambda a: a.reshape(2, 1, g * p)
    sds = jax.ShapeDtypeStruct((2, N_POW, g * p), F32)
    pr, pi = pl.pallas_call(_s5_pow_kernel, out_shape=[sds, sds], name="s5_powers")(
        flat(a_re), flat(a_im), flat(jnp.repeat(log_dt, p, axis=1)))
    by_group = lambda a: a.reshape(2, N_POW, g, p).transpose(0, 2, 1, 3)
    return by_group(pr), by_group(pi)


def _s5_prep_kernel(are_r, aim_r, pr_ref, pi_ref, qr_ref, qi_ref, bre_ref, bim_ref, cre_ref, cim_ref,
                    w1_ref, e_ref, dec_ref):
    t, n, p = CHUNK, S5_GROUP, S5_STATE
    width = t * n
    lane_tok = lax.broadcasted_iota(jnp.int32, (N_POW, width), 1) // n
    pow_id = lax.broadcasted_iota(jnp.int32, (N_POW, width), 0)
    onehot = lambda cond: jnp.where(cond, 1.0, 0.0).astype(BF16)
    sel_fwd = onehot(pow_id == lane_tok)
    sel_rev = onehot(pow_id == t - 1 - lane_tok)
    sel_out = onehot(pow_id == t - lane_tok)
    row_tok = lax.broadcasted_iota(jnp.int32, (width, N_POW), 0) // n
    row_pow = lax.broadcasted_iota(jnp.int32, (width, N_POW), 1)
    rsel_rev = onehot(row_pow == t - 1 - row_tok)
    rsel_fwd = onehot(row_pow == row_tok)
    lane = lax.broadcasted_iota(jnp.int32, (n, width), 1)
    for gi in range(w1_ref.shape[0]):
        _s5_prep_group(gi, are_r, aim_r, pr_ref, pi_ref, qr_ref, qi_ref, bre_ref, bim_ref, cre_ref, cim_ref,
                       w1_ref, e_ref, dec_ref, (sel_fwd, sel_rev, sel_out, rsel_rev, rsel_fwd, lane))


def _s5_prep_group(gi, are_r, aim_r, pr_ref, pi_ref, qr_ref, qi_ref, bre_ref, bim_ref, cre_ref, cim_ref,
                   w1_ref, e_ref, dec_ref, selectors):
    sel_fwd, sel_rev, sel_out, rsel_rev, rsel_fwd, lane = selectors
    t, n = CHUNK, S5_GROUP
    width = t * n
    strips = []
    f_parts = []
    e_parts = []
    for d in range(2):
        qr = qr_ref[d, gi]
        qi = qi_ref[d, gi]
        pr = pr_ref[d, gi]
        pi = pi_ref[d, gi]
        a_re = are_r[d, gi]
        a_im = aim_r[d, gi]
        nr = pr[1:2] - 1.0
        ni = pi[1:2]
        den = a_re * a_re + a_im * a_im
        fre = (nr * a_re + ni * a_im) / den
        fim = (ni * a_re - nr * a_im) / den
        bt_re = bre_ref[d, gi].T
        bt_im = bim_ref[d, gi].T
        bb_re, bb_im = _cmul(fre, fim, bt_re, bt_im)
        ct_re = jnp.concatenate([cre_ref[d, gi].T] * t, axis=1)
        ct_im = jnp.concatenate([cim_ref[d, gi].T] * t, axis=1)
        sel = sel_fwd if d == 0 else sel_rev
        w_re, w_im = _cmul(ct_re, ct_im, _select_cols(qr, sel), _select_cols(qi, sel))
        strips.append(_dot3(bb_re, w_re) - _dot3(bb_im, w_im))
        if d == 0:
            o_re, o_im = _cmul(w_re, w_im, qr[:, 1:2], qi[:, 1:2])
        else:
            o_re, o_im = _cmul(ct_re, ct_im, _select_cols(qr, sel_out), _select_cols(qi, sel_out))
        e_parts += [o_re, -o_im]
        rsel = rsel_rev if d == 0 else rsel_fwd
        f_re, f_im = _cmul(jnp.concatenate([bb_re] * t, axis=0), jnp.concatenate([bb_im] * t, axis=0),
                           _select_rows(rsel, pr), _select_rows(rsel, pi))
        f_parts += [f_re, f_im]
        dec_ref[gi, 2 * d:2 * d + 1, :] = jnp.concatenate([pr[t:t + 1], pr[t:t + 1]], axis=1)
        dec_ref[gi, 2 * d + 1:2 * d + 2, :] = jnp.concatenate([-pi[t:t + 1], pi[t:t + 1]], axis=1)

    blocks = []
    for s in range(t):
        fwd = strips[0] if s == 0 else jnp.where(lane >= n * s, pltpu.roll(strips[0], n * s, axis=1), 0.0)
        back = t - 1 - s
        bwd = strips[1] if back == 0 else jnp.where(lane < width - n * back,
                                                     pltpu.roll(strips[1], width - n * back, axis=1), 0.0)
        blocks.append(fwd + bwd)
    m = jnp.concatenate(blocks, axis=0)
    w1_ref[gi] = jnp.concatenate([m] + f_parts, axis=1).astype(BF16)
    e_ref[gi] = jnp.concatenate(e_parts, axis=0).astype(BF16)


def _s5_operators(a_re, a_im, log_dt, b_re, b_im, c_re, c_im):
    g, p, n, t = S5_GROUPS, S5_STATE, S5_GROUP, CHUNK
    pr, pi = _s5_pow_call(a_re, a_im, log_dt)
    gb = PREP_GROUPS
    spec = lambda shape: pl.BlockSpec((2, gb) + shape, lambda i: (0, i) + (0,) * len(shape))
    return pl.pallas_call(
        _s5_prep_kernel, grid=(g // gb,),
        in_specs=[spec((1, p)), spec((1, p)), spec((N_POW, p)), spec((N_POW, p)), spec((p, N_POW)),
                  spec((p, N_POW)), spec((p, n)), spec((p, n)), spec((n, p)), spec((n, p))],
        out_specs=[pl.BlockSpec((gb, t * n, 2 * t * n), lambda i: (i, 0, 0)),
                   pl.BlockSpec((gb, 4 * p, t * n), lambda i: (i, 0, 0)),
                   pl.BlockSpec((gb, 4, 2 * p), lambda i: (i, 0, 0))],
        out_shape=[jax.ShapeDtypeStruct((g, t * n, 2 * t * n), BF16),
                   jax.ShapeDtypeStruct((g, 4 * p, t * n), BF16),
                   jax.ShapeDtypeStruct((g, 4, 2 * p), F32)],
        compiler_params=_cparams(("parallel",)),
        name="s5_prep",
    )(a_re.reshape(2, g, 1, p), a_im.reshape(2, g, 1, p), pr, pi, pr.transpose(0, 1, 3, 2),
      pi.transpose(0, 1, 3, 2), b_re, b_im, c_re, c_im)


def _block_transpose8(ps):
    ps = list(ps)
    blk = lax.broadcasted_iota(jnp.int32, ps[0].shape, 1) // S5_GROUP
    for k in range(3):
        step = 1 << k
        shift = S5_GROUP * step
        keep = ((blk >> k) & 1) == 0
        for a in range(8):
            if a & step:
                continue
            pa, pb = ps[a], ps[a + step]
            ps[a] = jnp.where(keep, pa, pltpu.roll(pb, shift, axis=1))
            ps[a + step] = jnp.where(keep, pltpu.roll(pa, 128 - shift, axis=1), pb)
    return ps


def _s5a_kernel(u_ref, w1_ref, yin_ref, gf_ref, gb_ref):
    nb = u_ref.shape[1]
    xs = []
    for s in range(CHUNK):
        parts = [u_ref[0, b, pl.ds(s, TILE_CHUNKS, stride=CHUNK), :] for b in range(nb)]
        xs.append(jnp.concatenate(parts, axis=0))
    lo = _block_transpose8(xs[:8])
    hi = _block_transpose8(xs[8:])
    for j in range(GRP_PER_BLK):
        og = jnp.concatenate([lo[j], hi[j]], axis=1)
        r = _dot(og.astype(BF16), w1_ref[j])
        yin_ref[j] = r[:, 0:256].astype(yin_ref.dtype)
        gf_ref[j] = r[:, 256:384]
        gb_ref[j] = r[:, 384:512]


def _s5a_call(u, w1):
    n_blk, b, s, _ = u.shape
    nt = s // TILE_TOK
    rows = b * TILE_CHUNKS
    out_spec = lambda n: pl.BlockSpec((GRP_PER_BLK, rows, n), lambda l, j: (l, j, 0))
    return pl.pallas_call(
        _s5a_kernel, grid=(n_blk, nt),
        in_specs=[pl.BlockSpec((1, b, TILE_TOK, LANE_BLK), lambda l, j: (l, 0, j, 0)),
                  pl.BlockSpec((GRP_PER_BLK, 256, 512), lambda l, j: (l, 0, 0))],
        out_specs=[out_spec(256), out_spec(128), out_spec(128)],
        out_shape=[jax.ShapeDtypeStruct((S5_GROUPS, nt * rows, 256), BF16),
                   jax.ShapeDtypeStruct((S5_GROUPS, nt * rows, 128), F32),
                   jax.ShapeDtypeStruct((S5_GROUPS, nt * rows, 128), F32)],
        compiler_params=_cparams(("parallel", "parallel")),
        name="s5_chunk_in",
    )(u, w1)


def _s5b_kernel(nb, a_ref, gfc_ref, gfl_ref, gbc_ref, gbl_ref, hfc_ref, hfl_ref, hbc_ref, hbl_ref):
    gb = a_ref.shape[0]
    rows = nb * TILE_CHUNKS
    n_lat = gfl_ref.shape[1] // rows
    a1f = [jnp.broadcast_to(a_ref[g, 0:1, :], (nb, 128)) for g in range(gb)]
    a2f = [jnp.broadcast_to(a_ref[g, 1:2, :], (nb, 128)) for g in range(gb)]
    a1b = [jnp.broadcast_to(a_ref[g, 2:3, :], (nb, 128)) for g in range(gb)]
    a2b = [jnp.broadcast_to(a_ref[g, 3:4, :], (nb, 128)) for g in range(gb)]

    def step(state, a1, a2, g_ref, h_ref, g, row):
        h, hs = state
        h_ref[g, row, :] = h
        inp = g_ref[g, row, :]
        return a1 * h + a2 * hs + inp, a1 * hs - a2 * h + pltpu.roll(inp, 64, axis=1)

    zero = jnp.zeros((nb, 128), F32)
    hf = [(zero, zero) for _ in range(gb)]
    hb = [(zero, zero) for _ in range(gb)]
    for ci in range(TILE_CHUNKS):
        rf = pl.ds(ci, nb, stride=TILE_CHUNKS)
        rb = pl.ds(TILE_CHUNKS - 1 - ci, nb, stride=TILE_CHUNKS)
        for g in range(gb):
            hf[g] = step(hf[g], a1f[g], a2f[g], gfc_ref, hfc_ref, g, rf)
            hb[g] = step(hb[g], a1b[g], a2b[g], gbc_ref, hbc_ref, g, rb)

    def body(j, carry):
        hf, hb = carry
        hf = list(hf)
        hb = list(hb)
        base_f = j * rows
        base_b = (n_lat - 1 - j) * rows
        for ci in range(TILE_CHUNKS):
            rf = pl.ds(base_f + ci, nb, stride=TILE_CHUNKS)
            rb = pl.ds(base_b + (TILE_CHUNKS - 1 - ci), nb, stride=TILE_CHUNKS)
            for g in range(gb):
                hf[g] = step(hf[g], a1f[g], a2f[g], gfl_ref, hfl_ref, g, rf)
                hb[g] = step(hb[g], a1b[g], a2b[g], gbl_ref, hbl_ref, g, rb)
        return tuple(hf), tuple(hb)

    lax.fori_loop(0, n_lat, body, (tuple(hf), tuple(hb)))


def _s5b_call(decay, gf_c, gf_l, gb_c, gb_l, nb):
    gblk = 4
    spec = lambda a: pl.BlockSpec((gblk, a.shape[1], 128), lambda i: (i, 0, 0))
    sds = lambda a: jax.ShapeDtypeStruct(a.shape, F32)
    return pl.pallas_call(
        functools.partial(_s5b_kernel, nb), grid=(S5_GROUPS // gblk,),
        in_specs=[pl.BlockSpec((gblk, 4, 128), lambda i: (i, 0, 0)),
                  spec(gf_c), spec(gf_l), spec(gb_c), spec(gb_l)],
        out_specs=[spec(gf_c), spec(gf_l), spec(gb_c), spec(gb_l)],
        out_shape=[sds(gf_c), sds(gf_l), sds(gb_c), sds(gb_l)],
        compiler_params=_cparams(("parallel",)),
        name="s5_state_scan",
    )(decay, gf_c, gf_l, gb_c, gb_l)


def _s5c_kernel(yin_ref, hf_ref, hb_ref, e_ref, u_ref, d_ref, y_ref):
    nb = u_ref.shape[1]
    ys = []
    for j in range(GRP_PER_BLK):
        h = jnp.concatenate([hf_ref[j], hb_ref[j]], axis=1).astype(BF16)
        ys.append(yin_ref[j].astype(F32) + _dot(h, e_ref[j]))
    at = (_block_transpose8([y[:, :128] for y in ys])
          + _block_transpose8([y[:, 128:] for y in ys]))
    d = d_ref[...]
    for t in range(CHUNK):
        for b in range(nb):
            rows = pl.ds(t, TILE_CHUNKS, stride=CHUNK)
            y_ref[0, b, rows, :] = at[t][b * TILE_CHUNKS:(b + 1) * TILE_CHUNKS] + d * u_ref[0, b, rows, :]


def _s5c_call(yin, hf, hb, e, u, d_skip):
    n_blk, b, s, _ = u.shape
    nt = s // TILE_TOK
    rows = b * TILE_CHUNKS
    gspec = lambda n: pl.BlockSpec((GRP_PER_BLK, rows, n), lambda l, j: (l, j, 0))
    tok_spec = pl.BlockSpec((1, b, TILE_TOK, LANE_BLK), lambda l, j: (l, 0, j, 0))
    return pl.pallas_call(
        _s5c_kernel, grid=(n_blk, nt),
        in_specs=[gspec(256), gspec(128), gspec(128),
                  pl.BlockSpec((GRP_PER_BLK, 256, 256), lambda l, j: (l, 0, 0)),
                  tok_spec,
                  pl.BlockSpec((1, LANE_BLK), lambda l, j: (0, l))],
        out_specs=tok_spec,
        out_shape=_lane_block_shape(b, s),
        compiler_params=_cparams(("parallel", "parallel")),
        name="s5_chunk_out",
    )(yin, hf, hb, e, u, d_skip.reshape(1, D_S5))


def _gelu_tanh(x):
    return 0.5 * x * (1.0 + jnp.tanh(math.sqrt(2.0 / math.pi) * (x + 0.044715 * (x * x * x))))


def _route(logits):
    lane = lax.broadcasted_iota(jnp.int32, logits.shape, 1).astype(F32)
    neg = jnp.float32(-1e30)
    big = jnp.float32(1e9)
    gl = jnp.where(lane < N_GROUPS, logits, neg)
    gmax = jnp.max(gl, axis=1, keepdims=True)
    gidx = jnp.min(jnp.where(gl == gmax, lane, big), axis=1, keepdims=True)
    gsum = jnp.sum(jnp.exp(gl - gmax), axis=1, keepdims=True)
    gw = 1.0 / gsum
    lo = N_GROUPS + EXP_PER_GROUP * gidx
    el = jnp.where((lane >= lo) & (lane < lo + EXP_PER_GROUP), logits, neg)
    v1 = jnp.max(el, axis=1, keepdims=True)
    i1 = jnp.min(jnp.where(el == v1, lane, big), axis=1, keepdims=True)
    el2 = jnp.where(lane == i1, neg, el)
    v2 = jnp.max(el2, axis=1, keepdims=True)
    i2 = jnp.min(jnp.where(el2 == v2, lane, big), axis=1, keepdims=True)
    ex = jnp.exp(v2 - v1)
    p1 = 1.0 / (1.0 + ex)
    p2 = ex * p1
    e1 = i1 - lo
    e2 = i2 - lo
    first = e1 < e2
    ea = jnp.where(first, e1, e2)
    eb = jnp.where(first, e2, e1)
    wa = gw * jnp.where(first, p1, p2)
    wb = gw * jnp.where(first, p2, p1)
    pair = ea * (7.0 - ea) * 0.5 + (eb - ea - 1.0)
    return wa, wb, 6.0 * gidx + pair


def _out_kernel(ypre_ref, ysc_ref, ycf_ref, x_ref, mod_ref, wglu_ref, bglu_ref, wo_ref,
                lng_ref, lnb_ref, wr_ref, br_ref, cnt0_ref, x1_ref, hx_ref, meta_ref, counts_ref, *rest):
    stage_ref = rest[0] if len(rest) == 2 else None
    cnt_ref = rest[-1]

    @pl.when((pl.program_id(0) == 0) & (pl.program_id(1) == 0))
    def _():
        cnt_ref[...] = cnt0_ref[...]

    tm = x_ref.shape[1]
    sub = tm // OUT_SUBTILES

    def row_chain(r0):
        rows = slice(r0, r0 + sub)
        ypre = jnp.concatenate([ypre_ref[blk, 0, rows, :] for blk in range(ypre_ref.shape[0])], axis=1)
        t = _gelu_tanh(ypre)
        gate = _sigmoid(_dot(t.astype(BF16), wglu_ref[...]) + bglu_ref[...])
        ys5 = (t * gate).astype(BF16)
        y = (_dot(ys5, wo_ref[0:D_S5, :]) + _dot(ysc_ref[0, rows, :], wo_ref[D_S5:D_S5 + D_SC, :])
             + _dot(ycf_ref[0, rows, :], wo_ref[D_S5 + D_SC:D_MODEL, :]))
        g1 = mod_ref[0, 2:3, :]
        x1 = _layer_norm(DN_ALPHA * x_ref[0, rows, :] + g1 * y, lng_ref[...], lnb_ref[...])
        x1_ref[0, rows, :] = x1
        h2 = x1 * (1.0 + mod_ref[0, 4:5, :]) + mod_ref[0, 3:4, :]
        hx_ref[0, rows, 0:D_MODEL] = h2
        return _route(_dot(h2.astype(BF16), wr_ref[...]) + br_ref[...])

    routed = [row_chain(r0) for r0 in range(0, tm, sub)]
    wa, wb, cls = (jnp.concatenate([r[i] for r in routed], axis=0) for i in range(3))

    lane = lax.broadcasted_iota(jnp.int32, (tm, ROUTER_LANES), 1).astype(F32)
    onehot = jnp.where(lane == cls, 1.0, 0.0)
    row_i = lax.broadcasted_iota(jnp.int32, (tm, tm), 0)
    col_i = lax.broadcasted_iota(jnp.int32, (tm, tm), 1)
    earlier = jnp.where(col_i < row_i, 1.0, 0.0).astype(BF16)
    before = _dot(earlier, onehot.astype(BF16)) + cnt_ref[...]
    rank = jnp.sum(before * onehot, axis=1, keepdims=True)
    cnt_ref[...] += jnp.sum(onehot, axis=0, keepdims=True)
    counts_ref[...] = cnt_ref[...]

    meta = (jnp.where(lane == META_WA, wa, 0.0) + jnp.where(lane == META_WB, wb, 0.0)
            + jnp.where(lane == META_CLS, cls, 0.0) + jnp.where(lane == META_RANK, rank, 0.0))
    meta_ref[...] = jnp.transpose(meta)[0:SUBLANES, :]
    hx_ref[0, :, D_MODEL:HX_LANES] = meta
    if stage_ref is not None:
        stage_ref[...] = jnp.zeros_like(stage_ref)


def _out_call(ypre, ysc, ycf, x, mod, wglu_bf, b_glu, wo_bf, ln_g, ln_b, w_router, b_router, counts0, tm,
              stage_rows):
    b, s, d = x.shape
    nt = s // tm
    row_spec = lambda n: pl.BlockSpec((1, tm, n), lambda i, j: (i, j, 0))
    full = lambda shape: pl.BlockSpec(shape, lambda i, j: (0,) * len(shape))
    out_specs = [row_spec(d), row_spec(HX_LANES),
                 pl.BlockSpec((SUBLANES, tm), lambda i, j: (0, i * nt + j)),
                 full((1, ROUTER_LANES))]
    out_shape = [jax.ShapeDtypeStruct((b, s, d), F32), jax.ShapeDtypeStruct((b, s, HX_LANES), F32),
                 jax.ShapeDtypeStruct((SUBLANES, b * s), F32),
                 jax.ShapeDtypeStruct((1, ROUTER_LANES), F32)]
    if stage_rows:
        stage_octs = stage_rows // (SUBLANES * b * nt)
        assert stage_octs * SUBLANES * b * nt == stage_rows
        out_specs.append(pl.BlockSpec((stage_octs, SUBLANES, HX_LANES), lambda i, j: (i * nt + j, 0, 0)))
        out_shape.append(jax.ShapeDtypeStruct((stage_rows // SUBLANES, SUBLANES, HX_LANES), F32))
    return pl.pallas_call(
        _out_kernel, grid=(b, nt),
        in_specs=[_lane_block_spec(tm), row_spec(D_SC), row_spec(D_CF), row_spec(d),
                  pl.BlockSpec((1, 6, d), lambda i, j: (i, 0, 0)),
                  full((D_S5, D_S5)), full((1, D_S5)), full((d, d)),
                  full((1, d)), full((1, d)), full((d, ROUTER_LANES)), full((1, ROUTER_LANES)),
                  full((1, ROUTER_LANES))],
        out_specs=out_specs, out_shape=out_shape,
        scratch_shapes=[pltpu.VMEM((1, ROUTER_LANES), F32)],
        compiler_params=_cparams(("arbitrary", "arbitrary")),
        name="out_proj",
    )(ypre, ysc, ycf, x, mod, wglu_bf, b_glu.reshape(1, -1), wo_bf, ln_g.reshape(1, -1),
      ln_b.reshape(1, -1), w_router, b_router, counts0)


def _sorted_rows(n_tok):
    return n_tok + N_CLASSES * MOE_TM


def _moe_plan(meta, counts, n_tok, expert_base):
    cls = meta[META_CLS].astype(jnp.int32)
    rank = meta[META_RANK].astype(jnp.int32)
    cnt = counts[0, :N_CLASSES].astype(jnp.int32)
    n_tiles = (cnt + (MOE_TM - 1)) // MOE_TM
    ends = jnp.cumsum(n_tiles)
    starts = ends - n_tiles
    class_ids = jnp.arange(N_CLASSES, dtype=jnp.int32)
    first_row = jnp.sum(jnp.where(cls[:, None] == class_ids[None, :], starts[None, :] * MOE_TM, 0), axis=1)
    slot = first_row + rank
    t_max = n_tok // MOE_TM + N_CLASSES
    n_used = ends[N_CLASSES - 1]
    tile = jnp.minimum(jnp.arange(t_max, dtype=jnp.int32), n_used - 1)
    tile_cls = jnp.sum((tile[:, None] >= ends[None, :]).astype(jnp.int32), axis=1)
    group = tile_cls // 6
    pair = tile_cls % 6
    first = jnp.array([0, 0, 0, 1, 1, 2], jnp.int32)[pair] + EXP_PER_GROUP * group + expert_base
    second = jnp.array([1, 2, 3, 2, 3, 3], jnp.int32)[pair] + EXP_PER_GROUP * group + expert_base
    return slot, tile, first, second, n_used.reshape(1)


def _split_row(row):
    return lax.shift_right_logical(row, 3), lax.bitwise_and(row, SUBLANES - 1)


def _dispatch_kernel(slot_ref, hx_ref, xs_init_ref, xs_ref, sem):
    del xs_init_ref
    n_oct = hx_ref.shape[1]
    base = (pl.program_id(0) * pl.num_programs(1) + pl.program_id(1)) * (n_oct * SUBLANES)

    def body(i, carry):
        for k in range(SUBLANES):
            oct_id, sub = _split_row(slot_ref[base + i * SUBLANES + k])
            pltpu.make_async_copy(hx_ref.at[0, i, pl.ds(k, 1), :], xs_ref.at[oct_id, pl.ds(sub, 1), :],
                                  sem).start(priority=k % 2)
        return carry

    lax.fori_loop(0, n_oct, body, 0)
    pltpu.make_async_copy(hx_ref.at[0], xs_ref.at[pl.ds(0, n_oct)], sem).wait()


def _dispatch_call(slot, hx, xs_init, tm):
    b, s, w = hx.shape
    n_rows = xs_init.shape[0] * SUBLANES
    grid_spec = pltpu.PrefetchScalarGridSpec(
        num_scalar_prefetch=1, grid=(b, s // tm),
        in_specs=[pl.BlockSpec((1, tm // SUBLANES, SUBLANES, w), lambda i, j, slot: (i, j, 0, 0)),
                  pl.BlockSpec(memory_space=pl.ANY)],
        out_specs=pl.BlockSpec(memory_space=pl.ANY),
        scratch_shapes=[pltpu.SemaphoreType.DMA(())])
    xs = pl.pallas_call(
        _dispatch_kernel, grid_spec=grid_spec,
        out_shape=jax.ShapeDtypeStruct(xs_init.shape, F32),
        input_output_aliases={2: 0},
        compiler_params=_cparams(("arbitrary", "arbitrary")),
        name="moe_dispatch",
    )(slot, hx.reshape(b, s // SUBLANES, SUBLANES, w), xs_init)
    return xs.reshape(n_rows, w)


def _moe_kernel(tile_ref, first_ref, second_ref, nused_ref, xs_ref, wga_ref, wgb_ref, wua_ref, wub_ref,
                wda_ref, wdb_ref, ys_ref):
    del tile_ref, first_ref, second_ref
    t = pl.program_id(0)
    n_used = nused_ref[0]

    @pl.when(t < n_used)
    def _():
        x = xs_ref[...]
        xb = x[:, 0:D_MODEL].astype(BF16)

        def expert(wg_ref, wu_ref, wd_ref, w):
            gate = _dot(xb, wg_ref[0])
            up = _dot(xb, wu_ref[0])
            act = gate * _sigmoid(gate) * up * w
            return _dot(act.astype(BF16), wd_ref[0])

        wa = x[:, D_MODEL + META_WA:D_MODEL + META_WA + 1]
        wb = x[:, D_MODEL + META_WB:D_MODEL + META_WB + 1]
        ys_ref[...] = expert(wga_ref, wua_ref, wda_ref, wa) + expert(wgb_ref, wub_ref, wdb_ref, wb)

    @pl.when(t >= n_used)
    def _():
        ys_ref[...] = jnp.zeros_like(ys_ref)


def _moe_call(tile, first, second, n_used, xs, wg_bf, wu_bf, wd_bf):
    n_rows, w = xs.shape
    d = D_MODEL
    t_max = tile.shape[0]
    up_spec = lambda sel: pl.BlockSpec((1, d, D_EXPERT), lambda t, tl, fi, se, nu: ((fi, se)[sel][t], 0, 0))
    down_spec = lambda sel: pl.BlockSpec((1, D_EXPERT, d), lambda t, tl, fi, se, nu: ((fi, se)[sel][t], 0, 0))
    grid_spec = pltpu.PrefetchScalarGridSpec(
        num_scalar_prefetch=4, grid=(t_max,),
        in_specs=[pl.BlockSpec((MOE_TM, w), lambda t, tl, fi, se, nu: (tl[t], 0)),
                  up_spec(0), up_spec(1), up_spec(0), up_spec(1), down_spec(0), down_spec(1)],
        out_specs=pl.BlockSpec((MOE_TM, d), lambda t, tl, fi, se, nu: (t, 0)))
    return pl.pallas_call(
        _moe_kernel, grid_spec=grid_spec,
        out_shape=jax.ShapeDtypeStruct((n_rows, d), F32),
        compiler_params=_cparams(("arbitrary",)),
        name="moe_experts",
    )(tile, first, second, n_used, xs, wg_bf, wg_bf, wu_bf, wu_bf, wd_bf, wd_bf)


def _combine_kernel(slot_ref, x1_ref, mod_ref, lng_ref, lnb_ref, ys_ref, o_ref, f_ref, sem):
    o_ref[0] = _combine_rows(slot_ref, x1_ref, mod_ref, lng_ref, lnb_ref, ys_ref, f_ref, sem)


def _combine_in_kernel(slot_ref, x1_ref, mod_ref, lng_ref, lnb_ref, ys_ref, mod_next_ref, w_ref,
                       o_ref, u_ref, bg_ref, cv_ref, gl_ref, f_ref, sem):
    x2 = _combine_rows(slot_ref, x1_ref, mod_ref, lng_ref, lnb_ref, ys_ref, f_ref, sem)
    o_ref[0] = x2
    _in_body(x2, mod_next_ref, w_ref, u_ref, bg_ref, cv_ref, gl_ref)


def _combine_rows(slot_ref, x1_ref, mod_ref, lng_ref, lnb_ref, ys_ref, f_ref, sem):
    n_oct = f_ref.shape[1]
    tm = n_oct * SUBLANES
    step = pl.program_id(0) * pl.num_programs(1) + pl.program_id(1)
    n_steps = pl.num_programs(0) * pl.num_programs(1)

    def request(which, buf):
        base = which * tm

        def body(i, carry):
            for k in range(SUBLANES):
                oct_id, sub = _split_row(slot_ref[base + i * SUBLANES + k])
                pltpu.make_async_copy(ys_ref.at[oct_id, pl.ds(sub, 1), :], f_ref.at[buf, i, pl.ds(k, 1), :],
                                      sem.at[buf]).start(priority=k % 2)
            return carry

        lax.fori_loop(0, n_oct, body, 0)

    @pl.when(step == 0)
    def _():
        request(0, 0)

    @pl.when(step + 1 < n_steps)
    def _():
        request(step + 1, (step + 1) % 2)

    buf = step % 2
    pltpu.make_async_copy(ys_ref.at[pl.ds(0, n_oct)], f_ref.at[buf], sem.at[buf]).wait()
    f = f_ref[buf].reshape(tm, f_ref.shape[3])
    g2 = mod_ref[0, 5:6, :]
    return _layer_norm(DN_ALPHA * x1_ref[0] + g2 * f, lng_ref[...], lnb_ref[...])


def _combine_call(slot, x1, mod, ln_g, ln_b, ys, tm, next_in=None):
    b, s, d = x1.shape
    ys = ys.reshape(ys.shape[0] // SUBLANES, SUBLANES, d)
    row_spec = lambda n: pl.BlockSpec((1, tm, n), lambda i, j, slot: (i, j, 0))
    mod_spec = pl.BlockSpec((1, 6, d), lambda i, j, slot: (i, 0, 0))
    full = lambda shape: pl.BlockSpec(shape, lambda i, j, slot: (0,) * len(shape))
    in_specs = [row_spec(d), mod_spec, full((1, d)), full((1, d)), pl.BlockSpec(memory_space=pl.ANY)]
    out_specs = [row_spec(d)]
    out_shape = [jax.ShapeDtypeStruct((b, s, d), F32)]
    args = [slot, x1, mod, ln_g.reshape(1, -1), ln_b.reshape(1, -1), ys]
    kern, name = _combine_kernel, "moe_combine"
    if next_in is not None:
        kern, name = _combine_in_kernel, "moe_combine_in_proj"
        in_specs += [mod_spec, full((d, D_IN))]
        args += list(next_in)
        out_specs += [pl.BlockSpec((D_S5 // LANE_BLK, 1, tm, LANE_BLK), lambda i, j, slot: (0, i, j, 0)),
                      row_spec(D_SC), row_spec(D_SC), row_spec(D_CF)]
        out_shape += [_lane_block_shape(b, s)] + [jax.ShapeDtypeStruct((b, s, D_SC), F32)] * 3
    grid_spec = pltpu.PrefetchScalarGridSpec(
        num_scalar_prefetch=1, grid=(b, s // tm), in_specs=in_specs, out_specs=out_specs,
        scratch_shapes=[pltpu.VMEM((2, tm // SUBLANES, SUBLANES, d), F32), pltpu.SemaphoreType.DMA((2,))])
    outs = pl.pallas_call(
        kern, grid_spec=grid_spec, out_shape=out_shape,
        compiler_params=_cparams(("arbitrary", "arbitrary")),
        name=name,
    )(*args)
    return outs[0] if next_in is None else outs


def _moe_sublayer(parts, counts, xs_init, expert_base, wg_bf, wu_bf, wd_bf, ln_g, ln_b, next_in=None):
    sizes = [p[2].shape[0] * p[2].shape[1] for p in parts]
    meta = jnp.concatenate([p[1] for p in parts], axis=1)
    slot, tile, first, second, n_used = _moe_plan(meta, counts, sum(sizes), expert_base)
    slots, start = [], 0
    for n in sizes:
        slots.append(slot[start:start + n])
        start += n
    xs = xs_init
    for (hx, _, _, _, tm), sl in zip(parts, slots):
        xs = _dispatch_call(sl, hx, xs.reshape(xs_init.shape), tm)
    ys = _moe_call(tile, first, second, n_used, xs, wg_bf, wu_bf, wd_bf)
    outs = []
    for idx, ((_, _, x1, mod, tm), sl) in enumerate(zip(parts, slots)):
        fuse = next_in if idx == len(parts) - 1 else None
        outs.append(_combine_call(sl, x1, mod, ln_g, ln_b, ys, tm, fuse))
    return outs


def kernel(x, c, ctx, c_ctx, w_mod, b_mod, w_in, s5_a_re, s5_a_im, s5_log_dt, s5_b_re, s5_b_im, s5_c_re, s5_c_im, s5_d, w_glu, b_glu, w_sc, w_dw, b_dw, ln_cf_g, ln_cf_b, w_o, ln1_g, ln1_b, w_rg, b_rg, w_rexp, b_rexp, w_gate, w_up, w_down, ln2_g, ln2_b):
    nb, seq, d = x.shape
    n_ctx = ctx.shape[1]
    n_layers = w_mod.shape[0]
    assert seq % TILE_TOK == 0 and n_ctx % TILE_TOK == 0 and seq % GRID_W == 0

    mod_rows = 16
    assert nb + 1 <= mod_rows
    c_all = jnp.concatenate([c, c_ctx[None, :], jnp.zeros((mod_rows - nb - 1, d), F32)], axis=0)
    mod_all = _mod_call(c_all, w_mod, b_mod)

    pad_r = ROUTER_LANES - N_GROUPS - N_EXPERTS
    x_lat, x_ctx = x, ctx
    mods_lat = [mod_all[l, :nb].reshape(nb, 6, d) for l in range(n_layers)]
    w_in_bfs = [w_in[l].astype(BF16) for l in range(n_layers)]
    stack = lambda w: w.astype(BF16).reshape((n_layers * N_EXPERTS,) + w.shape[2:])
    wg_bf, wu_bf, wd_bf = stack(w_gate), stack(w_up), stack(w_down)
    lat_proj = None
    for l in range(n_layers):
        last = l == n_layers - 1
        mod_lat = mods_lat[l]
        mod_ctx = jnp.broadcast_to(mod_all[l, nb].reshape(1, 6, d), (nb, 6, d))
        w_in_bf = w_in_bfs[l]
        wglu_bf = w_glu[l].astype(BF16)
        wo_bf = w_o[l].astype(BF16)
        w_router = jnp.concatenate([w_rg[l], w_rexp[l], jnp.zeros((d, pad_r), F32)], axis=1).astype(BF16)
        b_router = jnp.concatenate([b_rg[l], b_rexp[l], jnp.zeros((pad_r,), F32)]).reshape(1, -1)
        w1, e_op, decay = _s5_operators(s5_a_re[l], s5_a_im[l], s5_log_dt[l], s5_b_re[l], s5_b_im[l],
                                        s5_c_re[l], s5_c_im[l])

        if lat_proj is None:
            u_l, bg_l, cv_l, gl_l = _in_call(x_lat, mod_lat, w_in_bf, min(ROW_TILE_BIG, seq), False)
        else:
            u_l, bg_l, cv_l, gl_l = lat_proj
        if last:
            u_c = _in_call(x_ctx, mod_ctx, w_in_bf[:, :D_S5], TILE_TOK, True)
        else:
            u_c, bg_c, cv_c, gl_c = _in_call(x_ctx, mod_ctx, w_in_bf, TILE_TOK, False)

        yin_l, gf_l, gb_l = _s5a_call(u_l, w1)
        yin_c, gf_c, gb_c = _s5a_call(u_c, w1)
        hf_c, hf_l, hb_c, hb_l = _s5b_call(decay, gf_c, gf_l, gb_c, gb_l, nb)
        ypre_l = _s5c_call(yin_l, hf_l, hb_l, e_op, u_l, s5_d[l])

        ysc_l, ycf_l = _conv_call(bg_l, cv_l, gl_l, w_sc[l], w_dw[l], b_dw[l], ln_cf_g[l], ln_cf_b[l], True)
        parts = []
        counts = jnp.zeros((1, ROUTER_LANES), F32)
        n_moe = nb * seq
        if not last:
            ypre_c = _s5c_call(yin_c, hf_c, hb_c, e_op, u_c, s5_d[l])
            ysc_c, ycf_c = _conv_call(bg_c, cv_c, gl_c, w_sc[l], w_dw[l], b_dw[l], ln_cf_g[l], ln_cf_b[l], False)
            x1_c, hx_c, meta_c, counts = _out_call(ypre_c, ysc_c, ycf_c, x_ctx, mod_ctx, wglu_bf, b_glu[l], wo_bf,
                                                   ln1_g[l], ln1_b[l], w_router, b_router, counts, TILE_TOK, 0)
            parts.append((hx_c, meta_c, x1_c, mod_ctx, TILE_TOK))
            n_moe += nb * n_ctx
        x1_l, hx_l, meta_l, counts, stage = _out_call(ypre_l, ysc_l, ycf_l, x_lat, mod_lat, wglu_bf, b_glu[l], wo_bf,
                                                      ln1_g[l], ln1_b[l], w_router, b_router, counts, 512,
                                                      _sorted_rows(n_moe))
        parts.append((hx_l, meta_l, x1_l, mod_lat, min(ROW_TILE_BIG, seq)))
        next_in = None if last else (mods_lat[l + 1], w_in_bfs[l + 1])
        outs = _moe_sublayer(parts, counts, stage, l * N_EXPERTS, wg_bf, wu_bf, wd_bf, ln2_g[l], ln2_b[l], next_in)
        if last:
            x_lat = outs[-1]
        else:
            x_ctx = outs[0]
            x_lat, lat_proj = outs[-1][0], outs[-1][1:]
    return x_lat
```

```python
import functools
import math

import jax
import jax.numpy as jnp
from jax import lax
from jax.experimental import pallas as pl
from jax.experimental.pallas import tpu as pltpu

F32 = jnp.float32
BF16 = jnp.bfloat16

D_MODEL = 1024
DEPTH = 2
GRID_W = 64
D_S5 = 512
S5_GROUP = 16
S5_GROUPS = 32
S5_STATE = 64
D_SC = 256
D_CF = 256
CF_WIDTH = 31
CF_HALF = 15
D_IN = 1792
N_GROUPS = 4
EXP_PER_GROUP = 4
N_EXPERTS = 16
D_EXPERT = 256
DN_ALPHA = (2 * DEPTH) ** 0.25
LN_EPS = 1e-5

CHUNK = 16
N_POW = 32
PREP_GROUPS = 8
TILE_CHUNKS = 16
TILE_TOK = CHUNK * TILE_CHUNKS
LANE_BLK = 128
GRP_PER_BLK = LANE_BLK // S5_GROUP
ROUTER_LANES = 128
HX_LANES = D_MODEL + ROUTER_LANES
META_WA, META_WB, META_CLS, META_RANK = 0, 1, 2, 3
N_CLASSES = N_GROUPS * 6
MOE_TM = 512
OUT_SUBTILES = 2
ROW_TILE_BIG = 1024
SUBLANES = 8
VMEM_LIMIT = 56 * 1024 * 1024


def _cparams(sem):
    return pltpu.CompilerParams(dimension_semantics=sem, vmem_limit_bytes=VMEM_LIMIT)


def _split_bf16(a):
    hi = a.astype(BF16)
    lo = (a - hi.astype(F32)).astype(BF16)
    return hi, lo


def _dot(a, b):
    return jnp.dot(a, b, preferred_element_type=F32)


def _dot3(a, b):
    ah, al = _split_bf16(a)
    bh, bl = _split_bf16(b)
    return _dot(ah, bh) + (_dot(al, bh) + _dot(ah, bl))


def _sigmoid(x):
    return 1.0 / (1.0 + jnp.exp(-x))


def _layer_norm(x, g, b):
    mu = jnp.mean(x, axis=-1, keepdims=True)
    xc = x - mu
    var = jnp.mean(xc * xc, axis=-1, keepdims=True)
    return xc * lax.rsqrt(var + LN_EPS) * g + b


def _mod_kernel(c_ref, w_ref, b_ref, o_ref):
    c = c_ref[...]
    s = c * _sigmoid(c)
    o_ref[0] = _dot3(s, w_ref[0]) + b_ref[0]


def _mod_call(c_all, w_mod, b_mod):
    n_layers, d, n_out = w_mod.shape
    tn = 1536
    rows = c_all.shape[0]
    return pl.pallas_call(
        _mod_kernel,
        grid=(n_layers, n_out // tn),
        in_specs=[
            pl.BlockSpec((rows, d), lambda l, j: (0, 0)),
            pl.BlockSpec((1, d, tn), lambda l, j: (l, 0, j)),
            pl.BlockSpec((1, 1, tn), lambda l, j: (l, 0, j)),
        ],
        out_specs=pl.BlockSpec((1, rows, tn), lambda l, j: (l, 0, j)),
        out_shape=jax.ShapeDtypeStruct((n_layers, rows, n_out), F32),
        compiler_params=_cparams(("parallel", "parallel")),
        name="mod",
    )(c_all, w_mod, b_mod.reshape(n_layers, 1, n_out))


def _in_body(x, mod_ref, w_ref, u_ref, bg_ref, cv_ref, gl_ref):
    sh = mod_ref[0, 0:1, :]
    sc = mod_ref[0, 1:2, :]
    h = (x * (1.0 + sc) + sh).astype(BF16)
    z = _dot(h, w_ref[...])
    _store_lane_blocks(u_ref, z[:, 0:512])
    bg_ref[0] = z[:, 512:768]
    cv_ref[0] = z[:, 768:1024] * z[:, 1024:1280]
    gl_ref[0] = z[:, 1280:1536] * _sigmoid(z[:, 1536:1792])


def _in_kernel(x_ref, mod_ref, w_ref, u_ref, bg_ref, cv_ref, gl_ref):
    _in_body(x_ref[0], mod_ref, w_ref, u_ref, bg_ref, cv_ref, gl_ref)


def _in_u_kernel(x_ref, mod_ref, w_ref, u_ref):
    x = x_ref[0]
    sh = mod_ref[0, 0:1, :]
    sc = mod_ref[0, 1:2, :]
    h = (x * (1.0 + sc) + sh).astype(BF16)
    _store_lane_blocks(u_ref, _dot(h, w_ref[...]))


def _store_lane_blocks(ref, val):
    for blk in range(ref.shape[0]):
        ref[blk, 0] = val[:, blk * LANE_BLK:(blk + 1) * LANE_BLK]


def _lane_block_spec(tm):
    return pl.BlockSpec((D_S5 // LANE_BLK, 1, tm, LANE_BLK), lambda i, j: (0, i, j, 0))


def _lane_block_shape(b, s):
    return jax.ShapeDtypeStruct((D_S5 // LANE_BLK, b, s, LANE_BLK), F32)


def _in_call(x, mod, w_in_bf, tm, u_only):
    b, s, d = x.shape
    grid = (b, s // tm)
    row_spec = lambda n: pl.BlockSpec((1, tm, n), lambda i, j: (i, j, 0))
    in_specs = [
        row_spec(d),
        pl.BlockSpec((1, 6, d), lambda i, j: (i, 0, 0)),
    ]
    if u_only:
        in_specs.append(pl.BlockSpec((d, D_S5), lambda i, j: (0, 0)))
        return pl.pallas_call(
            _in_u_kernel, grid=grid, in_specs=in_specs,
            out_specs=_lane_block_spec(tm),
            out_shape=_lane_block_shape(b, s),
            compiler_params=_cparams(("parallel", "parallel")),
            name="in_proj_u",
        )(x, mod, w_in_bf)
    in_specs.append(pl.BlockSpec((d, D_IN), lambda i, j: (0, 0)))
    return pl.pallas_call(
        _in_kernel, grid=grid, in_specs=in_specs,
        out_specs=[_lane_block_spec(tm), row_spec(D_SC), row_spec(D_SC), row_spec(D_CF)],
        out_shape=[_lane_block_shape(b, s),
                   jax.ShapeDtypeStruct((b, s, D_SC), F32),
                   jax.ShapeDtypeStruct((b, s, D_SC), F32),
                   jax.ShapeDtypeStruct((b, s, D_CF), F32)],
        compiler_params=_cparams(("parallel", "parallel")),
        name="in_proj",
    )(x, mod, w_in_bf)


def _conv_tail(t, bdw_ref, lng_ref, lnb_ref):
    t = t + bdw_ref[...]
    t = _layer_norm(t, lng_ref[...], lnb_ref[...])
    return t * _sigmoid(t)


def _conv_grid_kernel(bg_ref, cv_ref, gl_ref, wsc_ref, wdw_ref, bdw_ref, lng_ref, lnb_ref,
                      ysc_ref, ycf_ref, grid_ref, t_ref):
    s = cv_ref.shape[1]
    rows = s // GRID_W
    cv = cv_ref[0]
    col = lax.broadcasted_iota(jnp.int32, (s, D_SC), 0) % GRID_W
    prev = jnp.where(col == 0, 0.0, pltpu.roll(cv, 1, axis=0))
    nxt = jnp.where(col == GRID_W - 1, 0.0, pltpu.roll(cv, s - 1, axis=0))
    conv = prev * wsc_ref[0:1, :] + cv * wsc_ref[1:2, :] + nxt * wsc_ref[2:3, :]
    ysc_ref[0] = (bg_ref[0] * conv).astype(ysc_ref.dtype)

    grid_ref[...] = gl_ref[0].reshape(rows, GRID_W, D_CF)

    def body(i, carry):
        w0 = pl.multiple_of(i * 8, 8)
        for half in range(D_CF // 128):
            lanes = slice(half * 128, (half + 1) * 128)
            acc = jnp.zeros((rows, 8, 128), F32)
            for k in range(CF_WIDTH):
                lo = max(0, CF_HALF - k)
                hi = min(rows, rows + CF_HALF - k)
                if hi <= lo:
                    continue
                term = grid_ref[lo + k - CF_HALF:hi + k - CF_HALF, pl.ds(w0, 8), lanes] * wdw_ref[k:k + 1, lanes]
                pieces = [acc[:lo]] * (lo > 0) + [acc[lo:hi] + term] + [acc[hi:]] * (hi < rows)
                acc = jnp.concatenate(pieces, axis=0) if len(pieces) > 1 else pieces[0]
            t_ref[:, pl.ds(w0, 8), lanes] = acc
        return carry

    lax.fori_loop(0, GRID_W // 8, body, 0)
    t = t_ref[...].reshape(s, D_CF)
    ycf_ref[0] = _conv_tail(t, bdw_ref, lng_ref, lnb_ref).astype(ycf_ref.dtype)


def _conv_seq_kernel(bg_ref, cv_ref, gl_ref, wsc_ref, wdw_ref, bdw_ref, lng_ref, lnb_ref,
                     ysc_ref, ycf_ref, pad_ref):
    s = cv_ref.shape[1]
    cv = cv_ref[0]
    pos = lax.broadcasted_iota(jnp.int32, (s, D_SC), 0)
    prev = jnp.where(pos == 0, 0.0, pltpu.roll(cv, 1, axis=0))
    nxt = jnp.where(pos == s - 1, 0.0, pltpu.roll(cv, s - 1, axis=0))
    conv = prev * wsc_ref[0:1, :] + cv * wsc_ref[1:2, :] + nxt * wsc_ref[2:3, :]
    ysc_ref[0] = (bg_ref[0] * conv).astype(ysc_ref.dtype)

    off = 16
    pad_ref[0:off] = jnp.zeros((off, D_CF), F32)
    pad_ref[off + s:off + s + 16] = jnp.zeros((16, D_CF), F32)
    pad_ref[off:off + s] = gl_ref[0]
    acc = jnp.zeros((s, D_CF), F32)
    for k in range(CF_WIDTH):
        acc = acc + pad_ref[pl.ds(off - CF_HALF + k, s), :] * wdw_ref[k:k + 1, :]
    ycf_ref[0] = _conv_tail(acc, bdw_ref, lng_ref, lnb_ref).astype(ycf_ref.dtype)


def _conv_call(bg, cv, gl, w_sc, w_dw, b_dw, ln_g, ln_b, grid_mode):
    b, s, _ = bg.shape
    row_spec = pl.BlockSpec((1, s, D_SC), lambda i: (i, 0, 0))
    full = lambda shape: pl.BlockSpec(shape, lambda i: (0,) * len(shape))
    if grid_mode:
        rows = s // GRID_W
        kern = _conv_grid_kernel
        scratch = [pltpu.VMEM((rows, GRID_W, D_CF), F32), pltpu.VMEM((rows, GRID_W, D_CF), F32)]
        name = "conv_grid"
    else:
        kern = _conv_seq_kernel
        scratch = [pltpu.VMEM((s + 32, D_CF), F32)]
        name = "conv_seq"
    return pl.pallas_call(
        kern, grid=(b,),
        in_specs=[row_spec, row_spec, row_spec, full((3, D_SC)), full((CF_WIDTH, D_CF)),
                  full((1, D_CF)), full((1, D_CF)), full((1, D_CF))],
        out_specs=[row_spec, row_spec],
        out_shape=[jax.ShapeDtypeStruct((b, s, D_SC), BF16), jax.ShapeDtypeStruct((b, s, D_CF), BF16)],
        scratch_shapes=scratch,
        compiler_params=_cparams(("parallel",)),
        name=name,
    )(bg, cv, gl, w_sc, w_dw, b_dw.reshape(1, -1), ln_g.reshape(1, -1), ln_b.reshape(1, -1))


def _split3(a):
    hi = a.astype(BF16)
    r = a - hi.astype(F32)
    mid = r.astype(BF16)
    lo = (r - mid.astype(F32)).astype(BF16)
    return hi, mid, lo


def _select_cols(a, sel):
    hi, mid, lo = _split3(a)
    return _dot(hi, sel) + (_dot(mid, sel) + _dot(lo, sel))


def _select_rows(sel, a):
    hi, mid, lo = _split3(a)
    return _dot(sel, hi) + (_dot(sel, mid) + _dot(sel, lo))


def _cmul(ar, ai, br, bi):
    return ar * br - ai * bi, ar * bi + ai * br


def _s5_pow_kernel(are_ref, aim_ref, ldt_ref, pr_ref, pi_ref):
    j = jnp.minimum(lax.broadcasted_iota(jnp.int32, pr_ref.shape[1:], 0), CHUNK).astype(F32)
    for d in range(2):
        dt = jnp.exp(ldt_ref[d])
        e = jnp.exp(j * (are_ref[d] * dt))
        pr_ref[d] = e * jnp.cos(j * (aim_ref[d] * dt))
        pi_ref[d] = e * jnp.sin(j * (aim_ref[d] * dt))


def _s5_pow_call(a_re, a_im, log_dt):
    g, p = S5_GROUPS, S5_STATE
    flat = lambda a: a.reshape(2, 1, g * p)
    sds = jax.ShapeDtypeStruct((2, N_POW, g * p), F32)
    pr, pi = pl.pallas_call(_s5_pow_kernel, out_shape=[sds, sds], name="s5_powers")(
        flat(a_re), flat(a_im), flat(jnp.repeat(log_dt, p, axis=1)))
    by_group = lambda a: a.reshape(2, N_POW, g, p).transpose(0, 2, 1, 3)
    return by_group(pr), by_group(pi)


def _s5_prep_kernel(are_r, aim_r, pr_ref, pi_ref, bre_ref, bim_ref, cre_ref, cim_ref,
                    w1_ref, e_ref, dec_ref):
    t, n, p = CHUNK, S5_GROUP, S5_STATE
    width = t * n
    lane_tok = lax.broadcasted_iota(jnp.int32, (N_POW, width), 1) // n
    pow_id = lax.broadcasted_iota(jnp.int32, (N_POW, width), 0)
    onehot = lambda cond: jnp.where(cond, 1.0, 0.0).astype(BF16)
    sel_fwd = onehot(pow_id == lane_tok)
    sel_rev = onehot(pow_id == t - 1 - lane_tok)
    sel_out = onehot(pow_id == t - lane_tok)
    row_tok = lax.broadcasted_iota(jnp.int32, (width, N_POW), 0) // n
    row_pow = lax.broadcasted_iota(jnp.int32, (width, N_POW), 1)
    rsel_rev = onehot(row_pow == t - 1 - row_tok)
    rsel_fwd = onehot(row_pow == row_tok)
    lane = lax.broadcasted_iota(jnp.int32, (n, width), 1)
    for gi in range(w1_ref.shape[0]):
        _s5_prep_group(gi, are_r, aim_r, pr_ref, pi_ref, bre_ref, bim_ref, cre_ref, cim_ref,
                       w1_ref, e_ref, dec_ref, (sel_fwd, sel_rev, sel_out, rsel_rev, rsel_fwd, lane))


def _s5_prep_group(gi, are_r, aim_r, pr_ref, pi_ref, bre_ref, bim_ref, cre_ref, cim_ref,
                   w1_ref, e_ref, dec_ref, selectors):
    sel_fwd, sel_rev, sel_out, rsel_rev, rsel_fwd, lane = selectors
    t, n = CHUNK, S5_GROUP
    width = t * n
    strips = []
    f_parts = []
    e_parts = []
    for d in range(2):
        pr = pr_ref[d, gi]
        pi = pi_ref[d, gi]
        qr = pr.T
        qi = pi.T
        a_re = are_r[d, gi]
        a_im = aim_r[d, gi]
        nr = pr[1:2] - 1.0
        ni = pi[1:2]
        den = a_re * a_re + a_im * a_im
        fre = (nr * a_re + ni * a_im) / den
        fim = (ni * a_re - nr * a_im) / den
        bt_re = bre_ref[d, gi].T
        bt_im = bim_ref[d, gi].T
        bb_re, bb_im = _cmul(fre, fim, bt_re, bt_im)
        ct_re = jnp.concatenate([cre_ref[d, gi].T] * t, axis=1)
        ct_im = jnp.concatenate([cim_ref[d, gi].T] * t, axis=1)
        sel = sel_fwd if d == 0 else sel_rev
        w_re, w_im = _cmul(ct_re, ct_im, _select_cols(qr, sel), _select_cols(qi, sel))
        strips.append(_dot3(bb_re, w_re) - _dot3(bb_im, w_im))
        if d == 0:
            o_re, o_im = _cmul(w_re, w_im, qr[:, 1:2], qi[:, 1:2])
        else:
            o_re, o_im = _cmul(ct_re, ct_im, _select_cols(qr, sel_out), _select_cols(qi, sel_out))
        e_parts += [o_re, -o_im]
        rsel = rsel_rev if d == 0 else rsel_fwd
        f_re, f_im = _cmul(jnp.concatenate([bb_re] * t, axis=0), jnp.concatenate([bb_im] * t, axis=0),
                           _select_rows(rsel, pr), _select_rows(rsel, pi))
        f_parts += [f_re, f_im]
        dec_ref[gi, 2 * d:2 * d + 1, :] = jnp.concatenate([pr[t:t + 1], pr[t:t + 1]], axis=1)
        dec_ref[gi, 2 * d + 1:2 * d + 2, :] = jnp.concatenate([-pi[t:t + 1], pi[t:t + 1]], axis=1)

    blocks = []
    for s in range(t):
        fwd = strips[0] if s == 0 else jnp.where(lane >= n * s, pltpu.roll(strips[0], n * s, axis=1), 0.0)
        back = t - 1 - s
        bwd = strips[1] if back == 0 else jnp.where(lane < width - n * back,
                                                     pltpu.roll(strips[1], width - n * back, axis=1), 0.0)
        blocks.append(fwd + bwd)
    m = jnp.concatenate(blocks, axis=0)
    w1_ref[gi] = jnp.concatenate([m] + f_parts, axis=1).astype(BF16)
    e_ref[gi] = jnp.concatenate(e_parts, axis=0).astype(BF16)


def _s5_operators(a_re, a_im, log_dt, b_re, b_im, c_re, c_im):
    g, p, n, t = S5_GROUPS, S5_STATE, S5_GROUP, CHUNK
    pr, pi = _s5_pow_call(a_re, a_im, log_dt)
    gb = PREP_GROUPS
    spec = lambda shape: pl.BlockSpec((2, gb) + shape, lambda i: (0, i) + (0,) * len(shape))
    return pl.pallas_call(
        _s5_prep_kernel, grid=(g // gb,),
        in_specs=[spec((1, p)), spec((1, p)), spec((N_POW, p)), spec((N_POW, p)),
                  spec((p, n)), spec((p, n)), spec((n, p)), spec((n, p))],
        out_specs=[pl.BlockSpec((gb, t * n, 2 * t * n), lambda i: (i, 0, 0)),
                   pl.BlockSpec((gb, 4 * p, t * n), lambda i: (i, 0, 0)),
                   pl.BlockSpec((gb, 4, 2 * p), lambda i: (i, 0, 0))],
        out_shape=[jax.ShapeDtypeStruct((g, t * n, 2 * t * n), BF16),
                   jax.ShapeDtypeStruct((g, 4 * p, t * n), BF16),
                   jax.ShapeDtypeStruct((g, 4, 2 * p), F32)],
        compiler_params=_cparams(("parallel",)),
        name="s5_prep",
    )(a_re.reshape(2, g, 1, p), a_im.reshape(2, g, 1, p), pr, pi, b_re, b_im, c_re, c_im)


def _block_transpose8(ps):
    ps = list(ps)
    blk = lax.broadcasted_iota(jnp.int32, ps[0].shape, 1) // S5_GROUP
    for k in range(3):
        step = 1 << k
        shift = S5_GROUP * step
        keep = ((blk >> k) & 1) == 0
        for a in range(8):
            if a & step:
                continue
            pa, pb = ps[a], ps[a + step]
            ps[a] = jnp.where(keep, pa, pltpu.roll(pb, shift, axis=1))
            ps[a + step] = jnp.where(keep, pltpu.roll(pa, 128 - shift, axis=1), pb)
    return ps


def _s5a_kernel(u_ref, w1_ref, yin_ref, gf_ref, gb_ref):
    nb = u_ref.shape[1]
    xs = []
    for s in range(CHUNK):
        parts = [u_ref[0, b, pl.ds(s, TILE_CHUNKS, stride=CHUNK), :] for b in range(nb)]
        xs.append(jnp.concatenate(parts, axis=0))
    lo = _block_transpose8(xs[:8])
    hi = _block_transpose8(xs[8:])
    for j in range(GRP_PER_BLK):
        og = jnp.concatenate([lo[j], hi[j]], axis=1)
        r = _dot(og.astype(BF16), w1_ref[j])
        yin_ref[j] = r[:, 0:256].astype(yin_ref.dtype)
        gf_ref[j] = r[:, 256:384]
        gb_ref[j] = r[:, 384:512]


def _s5a_call(u, w1):
    n_blk, b, s, _ = u.shape
    nt = s // TILE_TOK
    rows = b * TILE_CHUNKS
    out_spec = lambda n: pl.BlockSpec((GRP_PER_BLK, rows, n), lambda l, j: (l, j, 0))
    return pl.pallas_call(
        _s5a_kernel, grid=(n_blk, nt),
        in_specs=[pl.BlockSpec((1, b, TILE_TOK, LANE_BLK), lambda l, j: (l, 0, j, 0)),
                  pl.BlockSpec((GRP_PER_BLK, 256, 512), lambda l, j: (l, 0, 0))],
        out_specs=[out_spec(256), out_spec(128), out_spec(128)],
        out_shape=[jax.ShapeDtypeStruct((S5_GROUPS, nt * rows, 256), BF16),
                   jax.ShapeDtypeStruct((S5_GROUPS, nt * rows, 128), F32),
                   jax.ShapeDtypeStruct((S5_GROUPS, nt * rows, 128), F32)],
        compiler_params=_cparams(("parallel", "parallel")),
        name="s5_chunk_in",
    )(u, w1)


def _s5b_kernel(nb, a_ref, gfc_ref, gfl_ref, gbc_ref, gbl_ref, hfc_ref, hfl_ref, hbc_ref, hbl_ref):
    gb = a_ref.shape[0]
    rows = nb * TILE_CHUNKS
    n_lat = gfl_ref.shape[1] // rows
    a1f = [jnp.broadcast_to(a_ref[g, 0:1, :], (nb, 128)) for g in range(gb)]
    a2f = [jnp.broadcast_to(a_ref[g, 1:2, :], (nb, 128)) for g in range(gb)]
    a1b = [jnp.broadcast_to(a_ref[g, 2:3, :], (nb, 128)) for g in range(gb)]
    a2b = [jnp.broadcast_to(a_ref[g, 3:4, :], (nb, 128)) for g in range(gb)]

    def step(state, a1, a2, g_ref, h_ref, g, row):
        h, hs = state
        h_ref[g, row, :] = h
        inp = g_ref[g, row, :]
        return a1 * h + a2 * hs + inp, a1 * hs - a2 * h + pltpu.roll(inp, 64, axis=1)

    zero = jnp.zeros((nb, 128), F32)
    hf = [(zero, zero) for _ in range(gb)]
    hb = [(zero, zero) for _ in range(gb)]
    for ci in range(TILE_CHUNKS):
        rf = pl.ds(ci, nb, stride=TILE_CHUNKS)
        rb = pl.ds(TILE_CHUNKS - 1 - ci, nb, stride=TILE_CHUNKS)
        for g in range(gb):
            hf[g] = step(hf[g], a1f[g], a2f[g], gfc_ref, hfc_ref, g, rf)
            hb[g] = step(hb[g], a1b[g], a2b[g], gbc_ref, hbc_ref, g, rb)

    def body(j, carry):
        hf, hb = carry
        hf = list(hf)
        hb = list(hb)
        base_f = j * rows
        base_b = (n_lat - 1 - j) * rows
        for ci in range(TILE_CHUNKS):
            rf = pl.ds(base_f + ci, nb, stride=TILE_CHUNKS)
            rb = pl.ds(base_b + (TILE_CHUNKS - 1 - ci), nb, stride=TILE_CHUNKS)
            for g in range(gb):
                hf[g] = step(hf[g], a1f[g], a2f[g], gfl_ref, hfl_ref, g, rf)
                hb[g] = step(hb[g], a1b[g], a2b[g], gbl_ref, hbl_ref, g, rb)
        return tuple(hf), tuple(hb)

    lax.fori_loop(0, n_lat, body, (tuple(hf), tuple(hb)))


def _s5b_call(decay, gf_c, gf_l, gb_c, gb_l, nb):
    gblk = 4
    spec = lambda a: pl.BlockSpec((gblk, a.shape[1], 128), lambda i: (i, 0, 0))
    sds = lambda a: jax.ShapeDtypeStruct(a.shape, F32)
    return pl.pallas_call(
        functools.partial(_s5b_kernel, nb), grid=(S5_GROUPS // gblk,),
        in_specs=[pl.BlockSpec((gblk, 4, 128), lambda i: (i, 0, 0)),
                  spec(gf_c), spec(gf_l), spec(gb_c), spec(gb_l)],
        out_specs=[spec(gf_c), spec(gf_l), spec(gb_c), spec(gb_l)],
        out_shape=[sds(gf_c), sds(gf_l), sds(gb_c), sds(gb_l)],
        compiler_params=_cparams(("parallel",)),
        name="s5_state_scan",
    )(decay, gf_c, gf_l, gb_c, gb_l)


def _s5c_kernel(yin_ref, hf_ref, hb_ref, e_ref, u_ref, d_ref, y_ref):
    nb = u_ref.shape[1]
    ys = []
    for j in range(GRP_PER_BLK):
        h = jnp.concatenate([hf_ref[j], hb_ref[j]], axis=1).astype(BF16)
        ys.append(yin_ref[j].astype(F32) + _dot(h, e_ref[j]))
    at = (_block_transpose8([y[:, :128] for y in ys])
          + _block_transpose8([y[:, 128:] for y in ys]))
    d = d_ref[...]
    for t in range(CHUNK):
        for b in range(nb):
            rows = pl.ds(t, TILE_CHUNKS, stride=CHUNK)
            y_ref[0, b, rows, :] = at[t][b * TILE_CHUNKS:(b + 1) * TILE_CHUNKS] + d * u_ref[0, b, rows, :]


def _s5c_call(yin, hf, hb, e, u, d_skip):
    n_blk, b, s, _ = u.shape
    nt = s // TILE_TOK
    rows = b * TILE_CHUNKS
    gspec = lambda n: pl.BlockSpec((GRP_PER_BLK, rows, n), lambda l, j: (l, j, 0))
    tok_spec = pl.BlockSpec((1, b, TILE_TOK, LANE_BLK), lambda l, j: (l, 0, j, 0))
    return pl.pallas_call(
        _s5c_kernel, grid=(n_blk, nt),
        in_specs=[gspec(256), gspec(128), gspec(128),
                  pl.BlockSpec((GRP_PER_BLK, 256, 256), lambda l, j: (l, 0, 0)),
                  tok_spec,
                  pl.BlockSpec((1, LANE_BLK), lambda l, j: (0, l))],
        out_specs=tok_spec,
        out_shape=_lane_block_shape(b, s),
        compiler_params=_cparams(("parallel", "parallel")),
        name="s5_chunk_out",
    )(yin, hf, hb, e, u, d_skip.reshape(1, D_S5))


def _gelu_tanh(x):
    return 0.5 * x * (1.0 + jnp.tanh(math.sqrt(2.0 / math.pi) * (x + 0.044715 * (x * x * x))))


def _route(logits):
    lane = lax.broadcasted_iota(jnp.int32, logits.shape, 1).astype(F32)
    neg = jnp.float32(-1e30)
    big = jnp.float32(1e9)
    gl = jnp.where(lane < N_GROUPS, logits, neg)
    gmax = jnp.max(gl, axis=1, keepdims=True)
    gidx = jnp.min(jnp.where(gl == gmax, lane, big), axis=1, keepdims=True)
    gsum = jnp.sum(jnp.exp(gl - gmax), axis=1, keepdims=True)
    gw = 1.0 / gsum
    lo = N_GROUPS + EXP_PER_GROUP * gidx
    el = jnp.where((lane >= lo) & (lane < lo + EXP_PER_GROUP), logits, neg)
    v1 = jnp.max(el, axis=1, keepdims=True)
    i1 = jnp.min(jnp.where(el == v1, lane, big), axis=1, keepdims=True)
    el2 = jnp.where(lane == i1, neg, el)
    v2 = jnp.max(el2, axis=1, keepdims=True)
    i2 = jnp.min(jnp.where(el2 == v2, lane, big), axis=1, keepdims=True)
    ex = jnp.exp(v2 - v1)
    p1 = 1.0 / (1.0 + ex)
    p2 = ex * p1
    e1 = i1 - lo
    e2 = i2 - lo
    first = e1 < e2
    ea = jnp.where(first, e1, e2)
    eb = jnp.where(first, e2, e1)
    wa = gw * jnp.where(first, p1, p2)
    wb = gw * jnp.where(first, p2, p1)
    pair = ea * (7.0 - ea) * 0.5 + (eb - ea - 1.0)
    return wa, wb, 6.0 * gidx + pair


def _out_kernel(ypre_ref, ysc_ref, ycf_ref, x_ref, mod_ref, wglu_ref, bglu_ref, wo_ref,
                lng_ref, lnb_ref, wr_ref, br_ref, cnt0_ref, x1_ref, hx_ref, meta_ref, counts_ref, *rest):
    stage_ref = rest[0] if len(rest) == 2 else None
    cnt_ref = rest[-1]

    @pl.when((pl.program_id(0) == 0) & (pl.program_id(1) == 0))
    def _():
        cnt_ref[...] = cnt0_ref[...]

    tm = x_ref.shape[1]
    sub = tm // OUT_SUBTILES

    def row_chain(r0):
        rows = slice(r0, r0 + sub)
        ypre = jnp.concatenate([ypre_ref[blk, 0, rows, :] for blk in range(ypre_ref.shape[0])], axis=1)
        t = _gelu_tanh(ypre)
        gate = _sigmoid(_dot(t.astype(BF16), wglu_ref[...]) + bglu_ref[...])
        ys5 = (t * gate).astype(BF16)
        y = (_dot(ys5, wo_ref[0:D_S5, :]) + _dot(ysc_ref[0, rows, :], wo_ref[D_S5:D_S5 + D_SC, :])
             + _dot(ycf_ref[0, rows, :], wo_ref[D_S5 + D_SC:D_MODEL, :]))
        g1 = mod_ref[0, 2:3, :]
        x1 = _layer_norm(DN_ALPHA * x_ref[0, rows, :] + g1 * y, lng_ref[...], lnb_ref[...])
        x1_ref[0, rows, :] = x1
        h2 = x1 * (1.0 + mod_ref[0, 4:5, :]) + mod_ref[0, 3:4, :]
        hx_ref[0, rows, 0:D_MODEL] = h2
        return _route(_dot(h2.astype(BF16), wr_ref[...]) + br_ref[...])

    routed = [row_chain(r0) for r0 in range(0, tm, sub)]
    wa, wb, cls = (jnp.concatenate([r[i] for r in routed], axis=0) for i in range(3))

    lane = lax.broadcasted_iota(jnp.int32, (tm, ROUTER_LANES), 1).astype(F32)
    onehot = jnp.where(lane == cls, 1.0, 0.0)
    row_i = lax.broadcasted_iota(jnp.int32, (tm, tm), 0)
    col_i = lax.broadcasted_iota(jnp.int32, (tm, tm), 1)
    earlier = jnp.where(col_i < row_i, 1.0, 0.0).astype(BF16)
    before = _dot(earlier, onehot.astype(BF16)) + cnt_ref[...]
    rank = jnp.sum(before * onehot, axis=1, keepdims=True)
    cnt_ref[...] += jnp.sum(onehot, axis=0, keepdims=True)
    counts_ref[...] = cnt_ref[...]

    meta = (jnp.where(lane == META_WA, wa, 0.0) + jnp.where(lane == META_WB, wb, 0.0)
            + jnp.where(lane == META_CLS, cls, 0.0) + jnp.where(lane == META_RANK, rank, 0.0))
    meta_ref[...] = jnp.transpose(meta)[0:SUBLANES, :]
    hx_ref[0, :, D_MODEL:HX_LANES] = meta
    if stage_ref is not None:
        stage_ref[...] = jnp.zeros_like(stage_ref)


def _out_call(ypre, ysc, ycf, x, mod, wglu_bf, b_glu, wo_bf, ln_g, ln_b, w_router, b_router, counts0, tm,
              stage_rows):
    b, s, d = x.shape
    nt = s // tm
    row_spec = lambda n: pl.BlockSpec((1, tm, n), lambda i, j: (i, j, 0))
    full = lambda shape: pl.BlockSpec(shape, lambda i, j: (0,) * len(shape))
    out_specs = [row_spec(d), row_spec(HX_LANES),
                 pl.BlockSpec((SUBLANES, tm), lambda i, j: (0, i * nt + j)),
                 full((1, ROUTER_LANES))]
    out_shape = [jax.ShapeDtypeStruct((b, s, d), F32), jax.ShapeDtypeStruct((b, s, HX_LANES), F32),
                 jax.ShapeDtypeStruct((SUBLANES, b * s), F32),
                 jax.ShapeDtypeStruct((1, ROUTER_LANES), F32)]
    if stage_rows:
        stage_octs = stage_rows // (SUBLANES * b * nt)
        assert stage_octs * SUBLANES * b * nt == stage_rows
        out_specs.append(pl.BlockSpec((stage_octs, SUBLANES, HX_LANES), lambda i, j: (i * nt + j, 0, 0)))
        out_shape.append(jax.ShapeDtypeStruct((stage_rows // SUBLANES, SUBLANES, HX_LANES), F32))
    return pl.pallas_call(
        _out_kernel, grid=(b, nt),
        in_specs=[_lane_block_spec(tm), row_spec(D_SC), row_spec(D_CF), row_spec(d),
                  pl.BlockSpec((1, 6, d), lambda i, j: (i, 0, 0)),
                  full((D_S5, D_S5)), full((1, D_S5)), full((d, d)),
                  full((1, d)), full((1, d)), full((d, ROUTER_LANES)), full((1, ROUTER_LANES)),
                  full((1, ROUTER_LANES))],
        out_specs=out_specs, out_shape=out_shape,
        scratch_shapes=[pltpu.VMEM((1, ROUTER_LANES), F32)],
        compiler_params=_cparams(("arbitrary", "arbitrary")),
        name="out_proj",
    )(ypre, ysc, ycf, x, mod, wglu_bf, b_glu.reshape(1, -1), wo_bf, ln_g.reshape(1, -1),
      ln_b.reshape(1, -1), w_router, b_router, counts0)


def _sorted_rows(n_tok):
    return n_tok + N_CLASSES * MOE_TM


def _moe_plan(meta, counts, n_tok, expert_base):
    cls = meta[META_CLS].astype(jnp.int32)
    rank = meta[META_RANK].astype(jnp.int32)
    cnt = counts[0, :N_CLASSES].astype(jnp.int32)
    n_tiles = (cnt + (MOE_TM - 1)) // MOE_TM
    ends = jnp.cumsum(n_tiles)
    starts = ends - n_tiles
    class_ids = jnp.arange(N_CLASSES, dtype=jnp.int32)
    first_row = jnp.sum(jnp.where(cls[:, None] == class_ids[None, :], starts[None, :] * MOE_TM, 0), axis=1)
    slot = first_row + rank
    t_max = n_tok // MOE_TM + N_CLASSES
    n_used = ends[N_CLASSES - 1]
    tile = jnp.minimum(jnp.arange(t_max, dtype=jnp.int32), n_used - 1)
    tile_cls = jnp.sum((tile[:, None] >= ends[None, :]).astype(jnp.int32), axis=1)
    group = tile_cls // 6
    pair = tile_cls % 6
    first = jnp.array([0, 0, 0, 1, 1, 2], jnp.int32)[pair] + EXP_PER_GROUP * group + expert_base
    second = jnp.array([1, 2, 3, 2, 3, 3], jnp.int32)[pair] + EXP_PER_GROUP * group + expert_base
    return slot, tile, first, second, n_used.reshape(1)


def _split_row(row):
    return lax.shift_right_logical(row, 3), lax.bitwise_and(row, SUBLANES - 1)


def _dispatch_kernel(slot_ref, hx_ref, xs_init_ref, xs_ref, sem):
    del xs_init_ref
    n_oct = hx_ref.shape[1]
    base = (pl.program_id(0) * pl.num_programs(1) + pl.program_id(1)) * (n_oct * SUBLANES)

    def body(i, carry):
        for k in range(SUBLANES):
            oct_id, sub = _split_row(slot_ref[base + i * SUBLANES + k])
            pltpu.make_async_copy(hx_ref.at[0, i, pl.ds(k, 1), :], xs_ref.at[oct_id, pl.ds(sub, 1), :],
                                  sem).start(priority=k % 2)
        return carry

    lax.fori_loop(0, n_oct, body, 0)
    pltpu.make_async_copy(hx_ref.at[0], xs_ref.at[pl.ds(0, n_oct)], sem).wait()


def _dispatch_call(slot, hx, xs_init, tm):
    b, s, w = hx.shape
    n_rows = xs_init.shape[0] * SUBLANES
    grid_spec = pltpu.PrefetchScalarGridSpec(
        num_scalar_prefetch=1, grid=(b, s // tm),
        in_specs=[pl.BlockSpec((1, tm // SUBLANES, SUBLANES, w), lambda i, j, slot: (i, j, 0, 0)),
                  pl.BlockSpec(memory_space=pl.ANY)],
        out_specs=pl.BlockSpec(memory_space=pl.ANY),
        scratch_shapes=[pltpu.SemaphoreType.DMA(())])
    xs = pl.pallas_call(
        _dispatch_kernel, grid_spec=grid_spec,
        out_shape=jax.ShapeDtypeStruct(xs_init.shape, F32),
        input_output_aliases={2: 0},
        compiler_params=_cparams(("arbitrary", "arbitrary")),
        name="moe_dispatch",
    )(slot, hx.reshape(b, s // SUBLANES, SUBLANES, w), xs_init)
    return xs.reshape(n_rows, w)


def _moe_kernel(tile_ref, first_ref, second_ref, nused_ref, xs_ref, wga_ref, wgb_ref, wua_ref, wub_ref,
                wda_ref, wdb_ref, ys_ref):
    del tile_ref, first_ref, second_ref
    t = pl.program_id(0)
    n_used = nused_ref[0]

    @pl.when(t < n_used)
    def _():
        x = xs_ref[...]
        xb = x[:, 0:D_MODEL].astype(BF16)

        def expert(wg_ref, wu_ref, wd_ref, w):
            gate = _dot(xb, wg_ref[0])
            up = _dot(xb, wu_ref[0])
            act = gate * _sigmoid(gate) * up * w
            return _dot(act.astype(BF16), wd_ref[0])

        wa = x[:, D_MODEL + META_WA:D_MODEL + META_WA + 1]
        wb = x[:, D_MODEL + META_WB:D_MODEL + META_WB + 1]
        ys_ref[...] = expert(wga_ref, wua_ref, wda_ref, wa) + expert(wgb_ref, wub_ref, wdb_ref, wb)

    @pl.when(t >= n_used)
    def _():
        ys_ref[...] = jnp.zeros_like(ys_ref)


def _moe_call(tile, first, second, n_used, xs, wg_bf, wu_bf, wd_bf):
    n_rows, w = xs.shape
    d = D_MODEL
    t_max = tile.shape[0]
    up_spec = lambda sel: pl.BlockSpec((1, d, D_EXPERT), lambda t, tl, fi, se, nu: ((fi, se)[sel][t], 0, 0))
    down_spec = lambda sel: pl.BlockSpec((1, D_EXPERT, d), lambda t, tl, fi, se, nu: ((fi, se)[sel][t], 0, 0))
    grid_spec = pltpu.PrefetchScalarGridSpec(
        num_scalar_prefetch=4, grid=(t_max,),
        in_specs=[pl.BlockSpec((MOE_TM, w), lambda t, tl, fi, se, nu: (tl[t], 0)),
                  up_spec(0), up_spec(1), up_spec(0), up_spec(1), down_spec(0), down_spec(1)],
        out_specs=pl.BlockSpec((MOE_TM, d), lambda t, tl, fi, se, nu: (t, 0)))
    return pl.pallas_call(
        _moe_kernel, grid_spec=grid_spec,
        out_shape=jax.ShapeDtypeStruct((n_rows, d), F32),
        compiler_params=_cparams(("arbitrary",)),
        name="moe_experts",
    )(tile, first, second, n_used, xs, wg_bf, wg_bf, wu_bf, wu_bf, wd_bf, wd_bf)


def _combine_kernel(slot_ref, x1_ref, mod_ref, lng_ref, lnb_ref, ys_ref, o_ref, f_ref, sem):
    o_ref[0] = _combine_rows(slot_ref, x1_ref, mod_ref, lng_ref, lnb_ref, ys_ref, f_ref, sem)


def _combine_in_kernel(slot_ref, x1_ref, mod_ref, lng_ref, lnb_ref, ys_ref, mod_next_ref, w_ref,
                       o_ref, u_ref, bg_ref, cv_ref, gl_ref, f_ref, sem):
    x2 = _combine_rows(slot_ref, x1_ref, mod_ref, lng_ref, lnb_ref, ys_ref, f_ref, sem)
    o_ref[0] = x2
    _in_body(x2, mod_next_ref, w_ref, u_ref, bg_ref, cv_ref, gl_ref)


def _combine_rows(slot_ref, x1_ref, mod_ref, lng_ref, lnb_ref, ys_ref, f_ref, sem):
    n_oct = f_ref.shape[1]
    tm = n_oct * SUBLANES
    step = pl.program_id(0) * pl.num_programs(1) + pl.program_id(1)
    n_steps = pl.num_programs(0) * pl.num_programs(1)

    def request(which, buf):
        base = which * tm

        def body(i, carry):
            for k in range(SUBLANES):
                oct_id, sub = _split_row(slot_ref[base + i * SUBLANES + k])
                pltpu.make_async_copy(ys_ref.at[oct_id, pl.ds(sub, 1), :], f_ref.at[buf, i, pl.ds(k, 1), :],
                                      sem.at[buf]).start(priority=k % 2)
            return carry

        lax.fori_loop(0, n_oct, body, 0)

    @pl.when(step == 0)
    def _():
        request(0, 0)

    @pl.when(step + 1 < n_steps)
    def _():
        request(step + 1, (step + 1) % 2)

    buf = step % 2
    pltpu.make_async_copy(ys_ref.at[pl.ds(0, n_oct)], f_ref.at[buf], sem.at[buf]).wait()
    f = f_ref[buf].reshape(tm, f_ref.shape[3])
    g2 = mod_ref[0, 5:6, :]
    return _layer_norm(DN_ALPHA * x1_ref[0] + g2 * f, lng_ref[...], lnb_ref[...])


def _combine_call(slot, x1, mod, ln_g, ln_b, ys, tm, next_in=None):
    b, s, d = x1.shape
    ys = ys.reshape(ys.shape[0] // SUBLANES, SUBLANES, d)
    row_spec = lambda n: pl.BlockSpec((1, tm, n), lambda i, j, slot: (i, j, 0))
    mod_spec = pl.BlockSpec((1, 6, d), lambda i, j, slot: (i, 0, 0))
    full = lambda shape: pl.BlockSpec(shape, lambda i, j, slot: (0,) * len(shape))
    in_specs = [row_spec(d), mod_spec, full((1, d)), full((1, d)), pl.BlockSpec(memory_space=pl.ANY)]
    out_specs = [row_spec(d)]
    out_shape = [jax.ShapeDtypeStruct((b, s, d), F32)]
    args = [slot, x1, mod, ln_g.reshape(1, -1), ln_b.reshape(1, -1), ys]
    kern, name = _combine_kernel, "moe_combine"
    if next_in is not None:
        kern, name = _combine_in_kernel, "moe_combine_in_proj"
        in_specs += [mod_spec, full((d, D_IN))]
        args += list(next_in)
        out_specs += [pl.BlockSpec((D_S5 // LANE_BLK, 1, tm, LANE_BLK), lambda i, j, slot: (0, i, j, 0)),
                      row_spec(D_SC), row_spec(D_SC), row_spec(D_CF)]
        out_shape += [_lane_block_shape(b, s)] + [jax.ShapeDtypeStruct((b, s, D_SC), F32)] * 3
    grid_spec = pltpu.PrefetchScalarGridSpec(
        num_scalar_prefetch=1, grid=(b, s // tm), in_specs=in_specs, out_specs=out_specs,
        scratch_shapes=[pltpu.VMEM((2, tm // SUBLANES, SUBLANES, d), F32), pltpu.SemaphoreType.DMA((2,))])
    outs = pl.pallas_call(
        kern, grid_spec=grid_spec, out_shape=out_shape,
        compiler_params=_cparams(("arbitrary", "arbitrary")),
        name=name,
    )(*args)
    return outs[0] if next_in is None else outs


def _moe_sublayer(parts, counts, xs_init, expert_base, wg_bf, wu_bf, wd_bf, ln_g, ln_b, next_in=None):
    sizes = [p[2].shape[0] * p[2].shape[1] for p in parts]
    meta = jnp.concatenate([p[1] for p in parts], axis=1)
    slot, tile, first, second, n_used = _moe_plan(meta, counts, sum(sizes), expert_base)
    slots, start = [], 0
    for n in sizes:
        slots.append(slot[start:start + n])
        start += n
    xs = xs_init
    for (hx, _, _, _, tm), sl in zip(parts, slots):
        xs = _dispatch_call(sl, hx, xs.reshape(xs_init.shape), tm)
    ys = _moe_call(tile, first, second, n_used, xs, wg_bf, wu_bf, wd_bf)
    outs = []
    for idx, ((_, _, x1, mod, tm), sl) in enumerate(zip(parts, slots)):
        fuse = next_in if idx == len(parts) - 1 else None
        outs.append(_combine_call(sl, x1, mod, ln_g, ln_b, ys, tm, fuse))
    return outs


def kernel(x, c, ctx, c_ctx, w_mod, b_mod, w_in, s5_a_re, s5_a_im, s5_log_dt, s5_b_re, s5_b_im, s5_c_re, s5_c_im, s5_d, w_glu, b_glu, w_sc, w_dw, b_dw, ln_cf_g, ln_cf_b, w_o, ln1_g, ln1_b, w_rg, b_rg, w_rexp, b_rexp, w_gate, w_up, w_down, ln2_g, ln2_b):
    nb, seq, d = x.shape
    n_ctx = ctx.shape[1]
    n_layers = w_mod.shape[0]
    assert seq % TILE_TOK == 0 and n_ctx % TILE_TOK == 0 and seq % GRID_W == 0

    mod_rows = 16
    assert nb + 1 <= mod_rows
    c_all = jnp.concatenate([c, c_ctx[None, :], jnp.zeros((mod_rows - nb - 1, d), F32)], axis=0)
    mod_all = _mod_call(c_all, w_mod, b_mod)

    pad_r = ROUTER_LANES - N_GROUPS - N_EXPERTS
    x_lat, x_ctx = x, ctx
    mods_lat = [mod_all[l, :nb].reshape(nb, 6, d) for l in range(n_layers)]
    w_in_bfs = [w_in[l].astype(BF16) for l in range(n_layers)]
    stack = lambda w: w.astype(BF16).reshape((n_layers * N_EXPERTS,) + w.shape[2:])
    wg_bf, wu_bf, wd_bf = stack(w_gate), stack(w_up), stack(w_down)
    lat_proj = None
    for l in range(n_layers):
        last = l == n_layers - 1
        mod_lat = mods_lat[l]
        mod_ctx = jnp.broadcast_to(mod_all[l, nb].reshape(1, 6, d), (nb, 6, d))
        w_in_bf = w_in_bfs[l]
        wglu_bf = w_glu[l].astype(BF16)
        wo_bf = w_o[l].astype(BF16)
        w_router = jnp.concatenate([w_rg[l], w_rexp[l], jnp.zeros((d, pad_r), F32)], axis=1).astype(BF16)
        b_router = jnp.concatenate([b_rg[l], b_rexp[l], jnp.zeros((pad_r,), F32)]).reshape(1, -1)
        w1, e_op, decay = _s5_operators(s5_a_re[l], s5_a_im[l], s5_log_dt[l], s5_b_re[l], s5_b_im[l],
                                        s5_c_re[l], s5_c_im[l])

        if lat_proj is None:
            u_l, bg_l, cv_l, gl_l = _in_call(x_lat, mod_lat, w_in_bf, min(ROW_TILE_BIG, seq), False)
        else:
            u_l, bg_l, cv_l, gl_l = lat_proj
        if last:
            u_c = _in_call(x_ctx, mod_ctx, w_in_bf[:, :D_S5], TILE_TOK, True)
        else:
            u_c, bg_c, cv_c, gl_c = _in_call(x_ctx, mod_ctx, w_in_bf, TILE_TOK, False)

        yin_l, gf_l, gb_l = _s5a_call(u_l, w1)
        yin_c, gf_c, gb_c = _s5a_call(u_c, w1)
        hf_c, hf_l, hb_c, hb_l = _s5b_call(decay, gf_c, gf_l, gb_c, gb_l, nb)
        ypre_l = _s5c_call(yin_l, hf_l, hb_l, e_op, u_l, s5_d[l])

        ysc_l, ycf_l = _conv_call(bg_l, cv_l, gl_l, w_sc[l], w_dw[l], b_dw[l], ln_cf_g[l], ln_cf_b[l], True)
        parts = []
        counts = jnp.zeros((1, ROUTER_LANES), F32)
        n_moe = nb * seq
        if not last:
            ypre_c = _s5c_call(yin_c, hf_c, hb_c, e_op, u_c, s5_d[l])
            ysc_c, ycf_c = _conv_call(bg_c, cv_c, gl_c, w_sc[l], w_dw[l], b_dw[l], ln_cf_g[l], ln_cf_b[l], False)
            x1_c, hx_c, meta_c, counts = _out_call(ypre_c, ysc_c, ycf_c, x_ctx, mod_ctx, wglu_bf, b_glu[l], wo_bf,
                                                   ln1_g[l], ln1_b[l], w_router, b_router, counts, TILE_TOK, 0)
            parts.append((hx_c, meta_c, x1_c, mod_ctx, TILE_TOK))
            n_moe += nb * n_ctx
        x1_l, hx_l, meta_l, counts, stage = _out_call(ypre_l, ysc_l, ycf_l, x_lat, mod_lat, wglu_bf, b_glu[l], wo_bf,
                                                      ln1_g[l], ln1_b[l], w_router, b_router, counts, 512,
                                                      _sorted_rows(n_moe))
        parts.append((hx_l, meta_l, x1_l, mod_lat, min(ROW_TILE_BIG, seq)))
        next_in = None if last else (mods_lat[l + 1], w_in_bfs[l + 1])
        outs = _moe_sublayer(parts, counts, stage, l * N_EXPERTS, wg_bf, wu_bf, wd_bf, ln2_g[l], ln2_b[l], next_in)
        if last:
            x_lat = outs[-1]
        else:
            x_ctx = outs[0]
            x_lat, lat_proj = outs[-1][0], outs[-1][1:]
    return x_lat
```

```python
import functools
import math

import jax
import jax.numpy as jnp
from jax import lax
from jax.experimental import pallas as pl
from jax.experimental.pallas import tpu as pltpu

F32 = jnp.float32
BF16 = jnp.bfloat16

D_MODEL = 1024
DEPTH = 2
GRID_W = 64
D_S5 = 512
S5_GROUP = 16
S5_GROUPS = 32
S5_STATE = 64
D_SC = 256
D_CF = 256
CF_WIDTH = 31
CF_HALF = 15
D_IN = 1792
N_GROUPS = 4
EXP_PER_GROUP = 4
N_EXPERTS = 16
D_EXPERT = 256
DN_ALPHA = (2 * DEPTH) ** 0.25
LN_EPS = 1e-5

CHUNK = 16
N_POW = 32
PREP_GROUPS = 8
TILE_CHUNKS = 16
TILE_TOK = CHUNK * TILE_CHUNKS
LANE_BLK = 128
GRP_PER_BLK = LANE_BLK // S5_GROUP
ROUTER_LANES = 128
HX_LANES = D_MODEL + ROUTER_LANES
META_WA, META_WB, META_CLS, META_RANK = 0, 1, 2, 3
N_CLASSES = N_GROUPS * 6
MOE_TM = 512
OUT_SUBTILES = 2
ROW_TILE_BIG = 1024
SUBLANES = 8
VMEM_LIMIT = 56 * 1024 * 1024


def _cparams(sem):
    return pltpu.CompilerParams(dimension_semantics=sem, vmem_limit_bytes=VMEM_LIMIT)


def _split_bf16(a):
    hi = a.astype(BF16)
    lo = (a - hi.astype(F32)).astype(BF16)
    return hi, lo


def _dot(a, b):
    return jnp.dot(a, b, preferred_element_type=F32)


def _dot3(a, b):
    ah, al = _split_bf16(a)
    bh, bl = _split_bf16(b)
    return _dot(ah, bh) + (_dot(al, bh) + _dot(ah, bl))


def _sigmoid(x):
    return 1.0 / (1.0 + jnp.exp(-x))


def _layer_norm(x, g, b):
    mu = jnp.mean(x, axis=-1, keepdims=True)
    xc = x - mu
    var = jnp.mean(xc * xc, axis=-1, keepdims=True)
    return xc * lax.rsqrt(var + LN_EPS) * g + b


def _mod_kernel(c_ref, w_ref, b_ref, o_ref):
    c = c_ref[...]
    s = c * _sigmoid(c)
    o_ref[0] = _dot3(s, w_ref[0]) + b_ref[0]


def _mod_call(c_all, w_mod, b_mod):
    n_layers, d, n_out = w_mod.shape
    tn = 1536
    rows = c_all.shape[0]
    return pl.pallas_call(
        _mod_kernel,
        grid=(n_layers, n_out // tn),
        in_specs=[
            pl.BlockSpec((rows, d), lambda l, j: (0, 0)),
            pl.BlockSpec((1, d, tn), lambda l, j: (l, 0, j)),
            pl.BlockSpec((1, 1, tn), lambda l, j: (l, 0, j)),
        ],
        out_specs=pl.BlockSpec((1, rows, tn), lambda l, j: (l, 0, j)),
        out_shape=jax.ShapeDtypeStruct((n_layers, rows, n_out), F32),
        compiler_params=_cparams(("parallel", "parallel")),
        name="mod",
    )(c_all, w_mod, b_mod.reshape(n_layers, 1, n_out))


def _in_body(x, mod_ref, w_ref, u_ref, bg_ref, cv_ref, gl_ref):
    sh = mod_ref[0, 0:1, :]
    sc = mod_ref[0, 1:2, :]
    h = (x * (1.0 + sc) + sh).astype(BF16)
    z = _dot(h, w_ref[0])
    _store_lane_blocks(u_ref, z[:, 0:512])
    bg_ref[0] = z[:, 512:768]
    cv_ref[0] = z[:, 768:1024] * z[:, 1024:1280]
    gl_ref[0] = z[:, 1280:1536] * _sigmoid(z[:, 1536:1792])


def _in_kernel(x_ref, mod_ref, w_ref, u_ref, bg_ref, cv_ref, gl_ref):
    _in_body(x_ref[0], mod_ref, w_ref, u_ref, bg_ref, cv_ref, gl_ref)


def _in_u_kernel(x_ref, mod_ref, w_ref, u_ref):
    x = x_ref[0]
    sh = mod_ref[0, 0:1, :]
    sc = mod_ref[0, 1:2, :]
    h = (x * (1.0 + sc) + sh).astype(BF16)
    _store_lane_blocks(u_ref, _dot(h, w_ref[0]))


def _store_lane_blocks(ref, val):
    for blk in range(ref.shape[0]):
        ref[blk, 0] = val[:, blk * LANE_BLK:(blk + 1) * LANE_BLK]


def _lane_block_spec(tm):
    return pl.BlockSpec((D_S5 // LANE_BLK, 1, tm, LANE_BLK), lambda i, j: (0, i, j, 0))


def _lane_block_shape(b, s):
    return jax.ShapeDtypeStruct((D_S5 // LANE_BLK, b, s, LANE_BLK), F32)


def _in_call(x, mod, w_in_all, layer, tm, u_only):
    b, s, d = x.shape
    grid = (b, s // tm)
    row_spec = lambda n: pl.BlockSpec((1, tm, n), lambda i, j: (i, j, 0))
    in_specs = [
        row_spec(d),
        pl.BlockSpec((1, 6, d), lambda i, j: (i, 0, 0)),
    ]
    if u_only:
        in_specs.append(pl.BlockSpec((1, d, D_S5), lambda i, j: (layer, 0, 0)))
        return pl.pallas_call(
            _in_u_kernel, grid=grid, in_specs=in_specs,
            out_specs=_lane_block_spec(tm),
            out_shape=_lane_block_shape(b, s),
            compiler_params=_cparams(("parallel", "parallel")),
            name="in_proj_u",
        )(x, mod, w_in_all)
    in_specs.append(pl.BlockSpec((1, d, D_IN), lambda i, j: (layer, 0, 0)))
    return pl.pallas_call(
        _in_kernel, grid=grid, in_specs=in_specs,
        out_specs=[_lane_block_spec(tm), row_spec(D_SC), row_spec(D_SC), row_spec(D_CF)],
        out_shape=[_lane_block_shape(b, s),
                   jax.ShapeDtypeStruct((b, s, D_SC), F32),
                   jax.ShapeDtypeStruct((b, s, D_SC), F32),
                   jax.ShapeDtypeStruct((b, s, D_CF), F32)],
        compiler_params=_cparams(("parallel", "parallel")),
        name="in_proj",
    )(x, mod, w_in_all)


def _conv_tail(t, bdw_ref, lng_ref, lnb_ref):
    t = t + bdw_ref[...]
    t = _layer_norm(t, lng_ref[...], lnb_ref[...])
    return t * _sigmoid(t)


def _conv_grid_kernel(bg_ref, cv_ref, gl_ref, wsc_ref, wdw_ref, bdw_ref, lng_ref, lnb_ref,
                      ysc_ref, ycf_ref, grid_ref, t_ref):
    s = cv_ref.shape[1]
    rows = s // GRID_W
    cv = cv_ref[0]
    col = lax.broadcasted_iota(jnp.int32, (s, D_SC), 0) % GRID_W
    prev = jnp.where(col == 0, 0.0, pltpu.roll(cv, 1, axis=0))
    nxt = jnp.where(col == GRID_W - 1, 0.0, pltpu.roll(cv, s - 1, axis=0))
    conv = prev * wsc_ref[0:1, :] + cv * wsc_ref[1:2, :] + nxt * wsc_ref[2:3, :]
    ysc_ref[0] = (bg_ref[0] * conv).astype(ysc_ref.dtype)

    grid_ref[...] = gl_ref[0].reshape(rows, GRID_W, D_CF)

    def body(i, carry):
        w0 = pl.multiple_of(i * 8, 8)
        for half in range(D_CF // 128):
            lanes = slice(half * 128, (half + 1) * 128)
            acc = jnp.zeros((rows, 8, 128), F32)
            for k in range(CF_WIDTH):
                lo = max(0, CF_HALF - k)
                hi = min(rows, rows + CF_HALF - k)
                if hi <= lo:
                    continue
                term = grid_ref[lo + k - CF_HALF:hi + k - CF_HALF, pl.ds(w0, 8), lanes] * wdw_ref[k:k + 1, lanes]
                pieces = [acc[:lo]] * (lo > 0) + [acc[lo:hi] + term] + [acc[hi:]] * (hi < rows)
                acc = jnp.concatenate(pieces, axis=0) if len(pieces) > 1 else pieces[0]
            t_ref[:, pl.ds(w0, 8), lanes] = acc
        return carry

    lax.fori_loop(0, GRID_W // 8, body, 0)
    t = t_ref[...].reshape(s, D_CF)
    ycf_ref[0] = _conv_tail(t, bdw_ref, lng_ref, lnb_ref).astype(ycf_ref.dtype)


def _conv_seq_kernel(bg_ref, cv_ref, gl_ref, wsc_ref, wdw_ref, bdw_ref, lng_ref, lnb_ref,
                     ysc_ref, ycf_ref, pad_ref):
    s = cv_ref.shape[1]
    cv = cv_ref[0]
    pos = lax.broadcasted_iota(jnp.int32, (s, D_SC), 0)
    prev = jnp.where(pos == 0, 0.0, pltpu.roll(cv, 1, axis=0))
    nxt = jnp.where(pos == s - 1, 0.0, pltpu.roll(cv, s - 1, axis=0))
    conv = prev * wsc_ref[0:1, :] + cv * wsc_ref[1:2, :] + nxt * wsc_ref[2:3, :]
    ysc_ref[0] = (bg_ref[0] * conv).astype(ysc_ref.dtype)

    off = 16
    pad_ref[0:off] = jnp.zeros((off, D_CF), F32)
    pad_ref[off + s:off + s + 16] = jnp.zeros((16, D_CF), F32)
    pad_ref[off:off + s] = gl_ref[0]
    acc = jnp.zeros((s, D_CF), F32)
    for k in range(CF_WIDTH):
        acc = acc + pad_ref[pl.ds(off - CF_HALF + k, s), :] * wdw_ref[k:k + 1, :]
    ycf_ref[0] = _conv_tail(acc, bdw_ref, lng_ref, lnb_ref).astype(ycf_ref.dtype)


def _conv_call(bg, cv, gl, w_sc, w_dw, b_dw, ln_g, ln_b, grid_mode):
    b, s, _ = bg.shape
    row_spec = pl.BlockSpec((1, s, D_SC), lambda i: (i, 0, 0))
    full = lambda shape: pl.BlockSpec(shape, lambda i: (0,) * len(shape))
    if grid_mode:
        rows = s // GRID_W
        kern = _conv_grid_kernel
        scratch = [pltpu.VMEM((rows, GRID_W, D_CF), F32), pltpu.VMEM((rows, GRID_W, D_CF), F32)]
        name = "conv_grid"
    else:
        kern = _conv_seq_kernel
        scratch = [pltpu.VMEM((s + 32, D_CF), F32)]
        name = "conv_seq"
    return pl.pallas_call(
        kern, grid=(b,),
        in_specs=[row_spec, row_spec, row_spec, full((3, D_SC)), full((CF_WIDTH, D_CF)),
                  full((1, D_CF)), full((1, D_CF)), full((1, D_CF))],
        out_specs=[row_spec, row_spec],
        out_shape=[jax.ShapeDtypeStruct((b, s, D_SC), BF16), jax.ShapeDtypeStruct((b, s, D_CF), BF16)],
        scratch_shapes=scratch,
        compiler_params=_cparams(("parallel",)),
        name=name,
    )(bg, cv, gl, w_sc, w_dw, b_dw.reshape(1, -1), ln_g.reshape(1, -1), ln_b.reshape(1, -1))


def _split3(a):
    hi = a.astype(BF16)
    r = a - hi.astype(F32)
    mid = r.astype(BF16)
    lo = (r - mid.astype(F32)).astype(BF16)
    return hi, mid, lo


def _select_cols(a, sel):
    hi, mid, lo = _split3(a)
    return _dot(hi, sel) + (_dot(mid, sel) + _dot(lo, sel))


def _select_rows(sel, a):
    hi, mid, lo = _split3(a)
    return _dot(sel, hi) + (_dot(sel, mid) + _dot(sel, lo))


def _cmul(ar, ai, br, bi):
    return ar * br - ai * bi, ar * bi + ai * br


def _s5_pow_kernel(are_ref, aim_ref, ldt_ref, pr_ref, pi_ref):
    j = jnp.minimum(lax.broadcasted_iota(jnp.int32, pr_ref.shape[1:], 0), CHUNK).astype(F32)
    for d in range(2):
        dt = jnp.exp(ldt_ref[d])
        e = jnp.exp(j * (are_ref[d] * dt))
        pr_ref[d] = e * jnp.cos(j * (aim_ref[d] * dt))
        pi_ref[d] = e * jnp.sin(j * (aim_ref[d] * dt))


def _s5_pow_call(a_re, a_im, log_dt):
    g, p = S5_GROUPS, S5_STATE
    flat = lambda a: a.reshape(2, 1, g * p)
    sds = jax.ShapeDtypeStruct((2, N_POW, g * p), F32)
    pr, pi = pl.pallas_call(_s5_pow_kernel, out_shape=[sds, sds], name="s5_powers")(
        flat(a_re), flat(a_im), flat(jnp.repeat(log_dt, p, axis=1)))
    by_group = lambda a: a.reshape(2, N_POW, g, p).transpose(0, 2, 1, 3)
    return by_group(pr), by_group(pi)


def _s5_prep_kernel(are_r, aim_r, pr_ref, pi_ref, bre_ref, bim_ref, cre_ref, cim_ref,
                    w1_ref, e_ref, dec_ref):
    t, n, p = CHUNK, S5_GROUP, S5_STATE
    width = t * n
    lane_tok = lax.broadcasted_iota(jnp.int32, (N_POW, width), 1) // n
    pow_id = lax.broadcasted_iota(jnp.int32, (N_POW, width), 0)
    onehot = lambda cond: jnp.where(cond, 1.0, 0.0).astype(BF16)
    sel_fwd = onehot(pow_id == lane_tok)
    sel_rev = onehot(pow_id == t - 1 - lane_tok)
    sel_out = onehot(pow_id == t - lane_tok)
    row_tok = lax.broadcasted_iota(jnp.int32, (width, N_POW), 0) // n
    row_pow = lax.broadcasted_iota(jnp.int32, (width, N_POW), 1)
    rsel_rev = onehot(row_pow == t - 1 - row_tok)
    rsel_fwd = onehot(row_pow == row_tok)
    lane = lax.broadcasted_iota(jnp.int32, (n, width), 1)
    for gi in range(w1_ref.shape[0]):
        _s5_prep_group(gi, are_r, aim_r, pr_ref, pi_ref, bre_ref, bim_ref, cre_ref, cim_ref,
                       w1_ref, e_ref, dec_ref, (sel_fwd, sel_rev, sel_out, rsel_rev, rsel_fwd, lane))


def _s5_prep_group(gi, are_r, aim_r, pr_ref, pi_ref, bre_ref, bim_ref, cre_ref, cim_ref,
                   w1_ref, e_ref, dec_ref, selectors):
    sel_fwd, sel_rev, sel_out, rsel_rev, rsel_fwd, lane = selectors
    t, n = CHUNK, S5_GROUP
    width = t * n
    strips = []
    f_parts = []
    e_parts = []
    for d in range(2):
        pr = pr_ref[d, gi]
        pi = pi_ref[d, gi]
        qr = pr.T
        qi = pi.T
        a_re = are_r[d, gi]
        a_im = aim_r[d, gi]
        nr = pr[1:2] - 1.0
        ni = pi[1:2]
        den = a_re * a_re + a_im * a_im
        fre = (nr * a_re + ni * a_im) / den
        fim = (ni * a_re - nr * a_im) / den
        bt_re = bre_ref[d, gi].T
        bt_im = bim_ref[d, gi].T
        bb_re, bb_im = _cmul(fre, fim, bt_re, bt_im)
        ct_re = jnp.concatenate([cre_ref[d, gi].T] * t, axis=1)
        ct_im = jnp.concatenate([cim_ref[d, gi].T] * t, axis=1)
        sel = sel_fwd if d == 0 else sel_rev
        w_re, w_im = _cmul(ct_re, ct_im, _select_cols(qr, sel), _select_cols(qi, sel))
        strips.append(_dot3(bb_re, w_re) - _dot3(bb_im, w_im))
        if d == 0:
            o_re, o_im = _cmul(w_re, w_im, qr[:, 1:2], qi[:, 1:2])
        else:
            o_re, o_im = _cmul(ct_re, ct_im, _select_cols(qr, sel_out), _select_cols(qi, sel_out))
        e_parts += [o_re, -o_im]
        rsel = rsel_rev if d == 0 else rsel_fwd
        f_re, f_im = _cmul(jnp.concatenate([bb_re] * t, axis=0), jnp.concatenate([bb_im] * t, axis=0),
                           _select_rows(rsel, pr), _select_rows(rsel, pi))
        f_parts += [f_re, f_im]
        dec_ref[gi, 2 * d:2 * d + 1, :] = jnp.concatenate([pr[t:t + 1], pr[t:t + 1]], axis=1)
        dec_ref[gi, 2 * d + 1:2 * d + 2, :] = jnp.concatenate([-pi[t:t + 1], pi[t:t + 1]], axis=1)

    blocks = []
    for s in range(t):
        fwd = strips[0] if s == 0 else jnp.where(lane >= n * s, pltpu.roll(strips[0], n * s, axis=1), 0.0)
        back = t - 1 - s
        bwd = strips[1] if back == 0 else jnp.where(lane < width - n * back,
                                                     pltpu.roll(strips[1], width - n * back, axis=1), 0.0)
        blocks.append(fwd + bwd)
    m = jnp.concatenate(blocks, axis=0)
    w1_ref[gi] = jnp.concatenate([m] + f_parts, axis=1).astype(BF16)
    e_ref[gi] = jnp.concatenate(e_parts, axis=0).astype(BF16)


def _s5_operators(a_re, a_im, log_dt, b_re, b_im, c_re, c_im):
    g, p, n, t = S5_GROUPS, S5_STATE, S5_GROUP, CHUNK
    pr, pi = _s5_pow_call(a_re, a_im, log_dt)
    gb = PREP_GROUPS
    spec = lambda shape: pl.BlockSpec((2, gb) + shape, lambda i: (0, i) + (0,) * len(shape))
    return pl.pallas_call(
        _s5_prep_kernel, grid=(g // gb,),
        in_specs=[spec((1, p)), spec((1, p)), spec((N_POW, p)), spec((N_POW, p)),
                  spec((p, n)), spec((p, n)), spec((n, p)), spec((n, p))],
        out_specs=[pl.BlockSpec((gb, t * n, 2 * t * n), lambda i: (i, 0, 0)),
                   pl.BlockSpec((gb, 4 * p, t * n), lambda i: (i, 0, 0)),
                   pl.BlockSpec((gb, 4, 2 * p), lambda i: (i, 0, 0))],
        out_shape=[jax.ShapeDtypeStruct((g, t * n, 2 * t * n), BF16),
                   jax.ShapeDtypeStruct((g, 4 * p, t * n), BF16),
                   jax.ShapeDtypeStruct((g, 4, 2 * p), F32)],
        compiler_params=_cparams(("parallel",)),
        name="s5_prep",
    )(a_re.reshape(2, g, 1, p), a_im.reshape(2, g, 1, p), pr, pi, b_re, b_im, c_re, c_im)


def _block_transpose8(ps):
    ps = list(ps)
    blk = lax.broadcasted_iota(jnp.int32, ps[0].shape, 1) // S5_GROUP
    for k in range(3):
        step = 1 << k
        shift = S5_GROUP * step
        keep = ((blk >> k) & 1) == 0
        for a in range(8):
            if a & step:
                continue
            pa, pb = ps[a], ps[a + step]
            ps[a] = jnp.where(keep, pa, pltpu.roll(pb, shift, axis=1))
            ps[a + step] = jnp.where(keep, pltpu.roll(pa, 128 - shift, axis=1), pb)
    return ps


def _s5a_kernel(u_ref, w1_ref, yin_ref, gf_ref, gb_ref):
    nb = u_ref.shape[1]
    xs = []
    for s in range(CHUNK):
        parts = [u_ref[0, b, pl.ds(s, TILE_CHUNKS, stride=CHUNK), :] for b in range(nb)]
        xs.append(jnp.concatenate(parts, axis=0))
    lo = _block_transpose8(xs[:8])
    hi = _block_transpose8(xs[8:])
    for j in range(GRP_PER_BLK):
        og = jnp.concatenate([lo[j], hi[j]], axis=1)
        r = _dot(og.astype(BF16), w1_ref[j])
        yin_ref[j] = r[:, 0:256].astype(yin_ref.dtype)
        gf_ref[j] = r[:, 256:384]
        gb_ref[j] = r[:, 384:512]


def _s5a_call(u, w1):
    n_blk, b, s, _ = u.shape
    nt = s // TILE_TOK
    rows = b * TILE_CHUNKS
    out_spec = lambda n: pl.BlockSpec((GRP_PER_BLK, rows, n), lambda l, j: (l, j, 0))
    return pl.pallas_call(
        _s5a_kernel, grid=(n_blk, nt),
        in_specs=[pl.BlockSpec((1, b, TILE_TOK, LANE_BLK), lambda l, j: (l, 0, j, 0)),
                  pl.BlockSpec((GRP_PER_BLK, 256, 512), lambda l, j: (l, 0, 0))],
        out_specs=[out_spec(256), out_spec(128), out_spec(128)],
        out_shape=[jax.ShapeDtypeStruct((S5_GROUPS, nt * rows, 256), BF16),
                   jax.ShapeDtypeStruct((S5_GROUPS, nt * rows, 128), F32),
                   jax.ShapeDtypeStruct((S5_GROUPS, nt * rows, 128), F32)],
        compiler_params=_cparams(("parallel", "parallel")),
        name="s5_chunk_in",
    )(u, w1)


def _s5b_kernel(nb, a_ref, gfc_ref, gfl_ref, gbc_ref, gbl_ref, hfc_ref, hfl_ref, hbc_ref, hbl_ref):
    gb = a_ref.shape[0]
    rows = nb * TILE_CHUNKS
    n_lat = gfl_ref.shape[1] // rows
    a1f = [jnp.broadcast_to(a_ref[g, 0:1, :], (nb, 128)) for g in range(gb)]
    a2f = [jnp.broadcast_to(a_ref[g, 1:2, :], (nb, 128)) for g in range(gb)]
    a1b = [jnp.broadcast_to(a_ref[g, 2:3, :], (nb, 128)) for g in range(gb)]
    a2b = [jnp.broadcast_to(a_ref[g, 3:4, :], (nb, 128)) for g in range(gb)]

    def step(state, a1, a2, g_ref, h_ref, g, row):
        h, hs = state
        h_ref[g, row, :] = h
        inp = g_ref[g, row, :]
        return a1 * h + a2 * hs + inp, a1 * hs - a2 * h + pltpu.roll(inp, 64, axis=1)

    zero = jnp.zeros((nb, 128), F32)
    hf = [(zero, zero) for _ in range(gb)]
    hb = [(zero, zero) for _ in range(gb)]
    for ci in range(TILE_CHUNKS):
        rf = pl.ds(ci, nb, stride=TILE_CHUNKS)
        rb = pl.ds(TILE_CHUNKS - 1 - ci, nb, stride=TILE_CHUNKS)
        for g in range(gb):
            hf[g] = step(hf[g], a1f[g], a2f[g], gfc_ref, hfc_ref, g, rf)
            hb[g] = step(hb[g], a1b[g], a2b[g], gbc_ref, hbc_ref, g, rb)

    def body(j, carry):
        hf, hb = carry
        hf = list(hf)
        hb = list(hb)
        base_f = j * rows
        base_b = (n_lat - 1 - j) * rows
        for ci in range(TILE_CHUNKS):
            rf = pl.ds(base_f + ci, nb, stride=TILE_CHUNKS)
            rb = pl.ds(base_b + (TILE_CHUNKS - 1 - ci), nb, stride=TILE_CHUNKS)
            for g in range(gb):
                hf[g] = step(hf[g], a1f[g], a2f[g], gfl_ref, hfl_ref, g, rf)
                hb[g] = step(hb[g], a1b[g], a2b[g], gbl_ref, hbl_ref, g, rb)
        return tuple(hf), tuple(hb)

    lax.fori_loop(0, n_lat, body, (tuple(hf), tuple(hb)))


def _s5b_call(decay, gf_c, gf_l, gb_c, gb_l, nb):
    gblk = 4
    spec = lambda a: pl.BlockSpec((gblk, a.shape[1], 128), lambda i: (i, 0, 0))
    sds = lambda a: jax.ShapeDtypeStruct(a.shape, F32)
    return pl.pallas_call(
        functools.partial(_s5b_kernel, nb), grid=(S5_GROUPS // gblk,),
        in_specs=[pl.BlockSpec((gblk, 4, 128), lambda i: (i, 0, 0)),
                  spec(gf_c), spec(gf_l), spec(gb_c), spec(gb_l)],
        out_specs=[spec(gf_c), spec(gf_l), spec(gb_c), spec(gb_l)],
        out_shape=[sds(gf_c), sds(gf_l), sds(gb_c), sds(gb_l)],
        compiler_params=_cparams(("parallel",)),
        name="s5_state_scan",
    )(decay, gf_c, gf_l, gb_c, gb_l)


def _s5c_kernel(yin_ref, hf_ref, hb_ref, e_ref, u_ref, d_ref, y_ref):
    nb = u_ref.shape[1]
    ys = []
    for j in range(GRP_PER_BLK):
        h = jnp.concatenate([hf_ref[j], hb_ref[j]], axis=1).astype(BF16)
        ys.append(yin_ref[j].astype(F32) + _dot(h, e_ref[j]))
    at = (_block_transpose8([y[:, :128] for y in ys])
          + _block_transpose8([y[:, 128:] for y in ys]))
    d = d_ref[...]
    for t in range(CHUNK):
        for b in range(nb):
            rows = pl.ds(t, TILE_CHUNKS, stride=CHUNK)
            y_ref[0, b, rows, :] = at[t][b * TILE_CHUNKS:(b + 1) * TILE_CHUNKS] + d * u_ref[0, b, rows, :]


def _s5c_call(yin, hf, hb, e, u, d_skip):
    n_blk, b, s, _ = u.shape
    nt = s // TILE_TOK
    rows = b * TILE_CHUNKS
    gspec = lambda n: pl.BlockSpec((GRP_PER_BLK, rows, n), lambda l, j: (l, j, 0))
    tok_spec = pl.BlockSpec((1, b, TILE_TOK, LANE_BLK), lambda l, j: (l, 0, j, 0))
    return pl.pallas_call(
        _s5c_kernel, grid=(n_blk, nt),
        in_specs=[gspec(256), gspec(128), gspec(128),
                  pl.BlockSpec((GRP_PER_BLK, 256, 256), lambda l, j: (l, 0, 0)),
                  tok_spec,
                  pl.BlockSpec((1, LANE_BLK), lambda l, j: (0, l))],
        out_specs=tok_spec,
        out_shape=_lane_block_shape(b, s),
        compiler_params=_cparams(("parallel", "parallel")),
        name="s5_chunk_out",
    )(yin, hf, hb, e, u, d_skip.reshape(1, D_S5))


def _gelu_tanh(x):
    return 0.5 * x * (1.0 + jnp.tanh(math.sqrt(2.0 / math.pi) * (x + 0.044715 * (x * x * x))))


def _route(logits):
    lane = lax.broadcasted_iota(jnp.int32, logits.shape, 1).astype(F32)
    neg = jnp.float32(-1e30)
    big = jnp.float32(1e9)
    gl = jnp.where(lane < N_GROUPS, logits, neg)
    gmax = jnp.max(gl, axis=1, keepdims=True)
    gidx = jnp.min(jnp.where(gl == gmax, lane, big), axis=1, keepdims=True)
    gsum = jnp.sum(jnp.exp(gl - gmax), axis=1, keepdims=True)
    gw = 1.0 / gsum
    lo = N_GROUPS + EXP_PER_GROUP * gidx
    el = jnp.where((lane >= lo) & (lane < lo + EXP_PER_GROUP), logits, neg)
    v1 = jnp.max(el, axis=1, keepdims=True)
    i1 = jnp.min(jnp.where(el == v1, lane, big), axis=1, keepdims=True)
    el2 = jnp.where(lane == i1, neg, el)
    v2 = jnp.max(el2, axis=1, keepdims=True)
    i2 = jnp.min(jnp.where(el2 == v2, lane, big), axis=1, keepdims=True)
    ex = jnp.exp(v2 - v1)
    p1 = 1.0 / (1.0 + ex)
    p2 = ex * p1
    e1 = i1 - lo
    e2 = i2 - lo
    first = e1 < e2
    ea = jnp.where(first, e1, e2)
    eb = jnp.where(first, e2, e1)
    wa = gw * jnp.where(first, p1, p2)
    wb = gw * jnp.where(first, p2, p1)
    pair = ea * (7.0 - ea) * 0.5 + (eb - ea - 1.0)
    return wa, wb, 6.0 * gidx + pair


def _out_kernel(ypre_ref, ysc_ref, ycf_ref, x_ref, mod_ref, wglu_ref, bglu_ref, wo_ref,
                lng_ref, lnb_ref, wr_ref, br_ref, cnt0_ref, x1_ref, hx_ref, meta_ref, counts_ref, *rest):
    stage_ref = rest[0] if len(rest) == 2 else None
    cnt_ref = rest[-1]

    @pl.when((pl.program_id(0) == 0) & (pl.program_id(1) == 0))
    def _():
        cnt_ref[...] = cnt0_ref[...]

    tm = x_ref.shape[1]
    sub = tm // OUT_SUBTILES

    def row_chain(r0):
        rows = slice(r0, r0 + sub)
        ypre = jnp.concatenate([ypre_ref[blk, 0, rows, :] for blk in range(ypre_ref.shape[0])], axis=1)
        t = _gelu_tanh(ypre)
        gate = _sigmoid(_dot(t.astype(BF16), wglu_ref[...]) + bglu_ref[...])
        ys5 = (t * gate).astype(BF16)
        y = (_dot(ys5, wo_ref[0:D_S5, :]) + _dot(ysc_ref[0, rows, :], wo_ref[D_S5:D_S5 + D_SC, :])
             + _dot(ycf_ref[0, rows, :], wo_ref[D_S5 + D_SC:D_MODEL, :]))
        g1 = mod_ref[0, 2:3, :]
        x1 = _layer_norm(DN_ALPHA * x_ref[0, rows, :] + g1 * y, lng_ref[...], lnb_ref[...])
        x1_ref[0, rows, :] = x1
        h2 = x1 * (1.0 + mod_ref[0, 4:5, :]) + mod_ref[0, 3:4, :]
        hx_ref[0, rows, 0:D_MODEL] = h2
        return _route(_dot(h2.astype(BF16), wr_ref[...]) + br_ref[...])

    routed = [row_chain(r0) for r0 in range(0, tm, sub)]
    wa, wb, cls = (jnp.concatenate([r[i] for r in routed], axis=0) for i in range(3))

    lane = lax.broadcasted_iota(jnp.int32, (tm, ROUTER_LANES), 1).astype(F32)
    onehot = jnp.where(lane == cls, 1.0, 0.0)
    row_i = lax.broadcasted_iota(jnp.int32, (tm, tm), 0)
    col_i = lax.broadcasted_iota(jnp.int32, (tm, tm), 1)
    earlier = jnp.where(col_i < row_i, 1.0, 0.0).astype(BF16)
    before = _dot(earlier, onehot.astype(BF16)) + cnt_ref[...]
    rank = jnp.sum(before * onehot, axis=1, keepdims=True)
    cnt_ref[...] += jnp.sum(onehot, axis=0, keepdims=True)
    counts_ref[...] = cnt_ref[...]

    meta = (jnp.where(lane == META_WA, wa, 0.0) + jnp.where(lane == META_WB, wb, 0.0)
            + jnp.where(lane == META_CLS, cls, 0.0) + jnp.where(lane == META_RANK, rank, 0.0))
    meta_ref[...] = jnp.transpose(meta)[0:SUBLANES, :]
    hx_ref[0, :, D_MODEL:HX_LANES] = meta
    if stage_ref is not None:
        stage_ref[...] = jnp.zeros_like(stage_ref)


def _out_call(ypre, ysc, ycf, x, mod, wglu_bf, b_glu, wo_bf, ln_g, ln_b, w_router, b_router, counts0, tm,
              stage_rows):
    b, s, d = x.shape
    nt = s // tm
    row_spec = lambda n: pl.BlockSpec((1, tm, n), lambda i, j: (i, j, 0))
    full = lambda shape: pl.BlockSpec(shape, lambda i, j: (0,) * len(shape))
    out_specs = [row_spec(d), row_spec(HX_LANES),
                 pl.BlockSpec((SUBLANES, tm), lambda i, j: (0, i * nt + j)),
                 full((1, ROUTER_LANES))]
    out_shape = [jax.ShapeDtypeStruct((b, s, d), F32), jax.ShapeDtypeStruct((b, s, HX_LANES), F32),
                 jax.ShapeDtypeStruct((SUBLANES, b * s), F32),
                 jax.ShapeDtypeStruct((1, ROUTER_LANES), F32)]
    if stage_rows:
        stage_octs = stage_rows // (SUBLANES * b * nt)
        assert stage_octs * SUBLANES * b * nt == stage_rows
        out_specs.append(pl.BlockSpec((stage_octs, SUBLANES, HX_LANES), lambda i, j: (i * nt + j, 0, 0)))
        out_shape.append(jax.ShapeDtypeStruct((stage_rows // SUBLANES, SUBLANES, HX_LANES), F32))
    return pl.pallas_call(
        _out_kernel, grid=(b, nt),
        in_specs=[_lane_block_spec(tm), row_spec(D_SC), row_spec(D_CF), row_spec(d),
                  pl.BlockSpec((1, 6, d), lambda i, j: (i, 0, 0)),
                  full((D_S5, D_S5)), full((1, D_S5)), full((d, d)),
                  full((1, d)), full((1, d)), full((d, ROUTER_LANES)), full((1, ROUTER_LANES)),
                  full((1, ROUTER_LANES))],
        out_specs=out_specs, out_shape=out_shape,
        scratch_shapes=[pltpu.VMEM((1, ROUTER_LANES), F32)],
        compiler_params=_cparams(("arbitrary", "arbitrary")),
        name="out_proj",
    )(ypre, ysc, ycf, x, mod, wglu_bf, b_glu.reshape(1, -1), wo_bf, ln_g.reshape(1, -1),
      ln_b.reshape(1, -1), w_router, b_router, counts0)


def _sorted_rows(n_tok):
    return n_tok + N_CLASSES * MOE_TM


def _moe_plan(meta, counts, n_tok, expert_base):
    cls = meta[META_CLS].astype(jnp.int32)
    rank = meta[META_RANK].astype(jnp.int32)
    cnt = counts[0, :N_CLASSES].astype(jnp.int32)
    n_tiles = (cnt + (MOE_TM - 1)) // MOE_TM
    ends = jnp.cumsum(n_tiles)
    starts = ends - n_tiles
    class_ids = jnp.arange(N_CLASSES, dtype=jnp.int32)
    first_row = jnp.sum(jnp.where(cls[:, None] == class_ids[None, :], starts[None, :] * MOE_TM, 0), axis=1)
    slot = first_row + rank
    t_max = n_tok // MOE_TM + N_CLASSES
    n_used = ends[N_CLASSES - 1]
    tile = jnp.minimum(jnp.arange(t_max, dtype=jnp.int32), n_used - 1)
    tile_cls = jnp.sum((tile[:, None] >= ends[None, :]).astype(jnp.int32), axis=1)
    group = tile_cls // 6
    pair = tile_cls % 6
    first = jnp.array([0, 0, 0, 1, 1, 2], jnp.int32)[pair] + EXP_PER_GROUP * group + expert_base
    second = jnp.array([1, 2, 3, 2, 3, 3], jnp.int32)[pair] + EXP_PER_GROUP * group + expert_base
    return slot, tile, first, second, n_used.reshape(1)


def _split_row(row):
    return lax.shift_right_logical(row, 3), lax.bitwise_and(row, SUBLANES - 1)


def _dispatch_kernel(slot_ref, hx_ref, xs_init_ref, xs_ref, sem):
    del xs_init_ref
    n_oct = hx_ref.shape[1]
    base = (pl.program_id(0) * pl.num_programs(1) + pl.program_id(1)) * (n_oct * SUBLANES)

    def body(i, carry):
        for k in range(SUBLANES):
            oct_id, sub = _split_row(slot_ref[base + i * SUBLANES + k])
            pltpu.make_async_copy(hx_ref.at[0, i, pl.ds(k, 1), :], xs_ref.at[oct_id, pl.ds(sub, 1), :],
                                  sem).start(priority=k % 2)
        return carry

    lax.fori_loop(0, n_oct, body, 0)
    pltpu.make_async_copy(hx_ref.at[0], xs_ref.at[pl.ds(0, n_oct)], sem).wait()


def _dispatch_call(slot, hx, xs_init, tm):
    b, s, w = hx.shape
    n_rows = xs_init.shape[0] * SUBLANES
    grid_spec = pltpu.PrefetchScalarGridSpec(
        num_scalar_prefetch=1, grid=(b, s // tm),
        in_specs=[pl.BlockSpec((1, tm // SUBLANES, SUBLANES, w), lambda i, j, slot: (i, j, 0, 0)),
                  pl.BlockSpec(memory_space=pl.ANY)],
        out_specs=pl.BlockSpec(memory_space=pl.ANY),
        scratch_shapes=[pltpu.SemaphoreType.DMA(())])
    xs = pl.pallas_call(
        _dispatch_kernel, grid_spec=grid_spec,
        out_shape=jax.ShapeDtypeStruct(xs_init.shape, F32),
        input_output_aliases={2: 0},
        compiler_params=_cparams(("arbitrary", "arbitrary")),
        name="moe_dispatch",
    )(slot, hx.reshape(b, s // SUBLANES, SUBLANES, w), xs_init)
    return xs.reshape(n_rows, w)


def _moe_kernel(tile_ref, first_ref, second_ref, nused_ref, xs_ref, wga_ref, wgb_ref, wua_ref, wub_ref,
                wda_ref, wdb_ref, ys_ref):
    del tile_ref, first_ref, second_ref
    t = pl.program_id(0)
    n_used = nused_ref[0]

    @pl.when(t < n_used)
    def _():
        x = xs_ref[...]
        xb = x[:, 0:D_MODEL].astype(BF16)

        def expert(wg_ref, wu_ref, wd_ref, w):
            gate = _dot(xb, wg_ref[0])
            up = _dot(xb, wu_ref[0])
            act = gate * _sigmoid(gate) * up * w
            return _dot(act.astype(BF16), wd_ref[0])

        wa = x[:, D_MODEL + META_WA:D_MODEL + META_WA + 1]
        wb = x[:, D_MODEL + META_WB:D_MODEL + META_WB + 1]
        ys_ref[...] = expert(wga_ref, wua_ref, wda_ref, wa) + expert(wgb_ref, wub_ref, wdb_ref, wb)

    @pl.when(t >= n_used)
    def _():
        ys_ref[...] = jnp.zeros_like(ys_ref)


def _moe_call(tile, first, second, n_used, xs, wg_bf, wu_bf, wd_bf):
    n_rows, w = xs.shape
    d = D_MODEL
    t_max = tile.shape[0]
    up_spec = lambda sel: pl.BlockSpec((1, d, D_EXPERT), lambda t, tl, fi, se, nu: ((fi, se)[sel][t], 0, 0))
    down_spec = lambda sel: pl.BlockSpec((1, D_EXPERT, d), lambda t, tl, fi, se, nu: ((fi, se)[sel][t], 0, 0))
    grid_spec = pltpu.PrefetchScalarGridSpec(
        num_scalar_prefetch=4, grid=(t_max,),
        in_specs=[pl.BlockSpec((MOE_TM, w), lambda t, tl, fi, se, nu: (tl[t], 0)),
                  up_spec(0), up_spec(1), up_spec(0), up_spec(1), down_spec(0), down_spec(1)],
        out_specs=pl.BlockSpec((MOE_TM, d), lambda t, tl, fi, se, nu: (t, 0)))
    return pl.pallas_call(
        _moe_kernel, grid_spec=grid_spec,
        out_shape=jax.ShapeDtypeStruct((n_rows, d), F32),
        compiler_params=_cparams(("arbitrary",)),
        name="moe_experts",
    )(tile, first, second, n_used, xs, wg_bf, wg_bf, wu_bf, wu_bf, wd_bf, wd_bf)


def _combine_kernel(slot_ref, x1_ref, mod_ref, lng_ref, lnb_ref, ys_ref, o_ref, f_ref, sem):
    o_ref[0] = _combine_rows(slot_ref, x1_ref, mod_ref, lng_ref, lnb_ref, ys_ref, f_ref, sem)


def _combine_in_kernel(slot_ref, x1_ref, mod_ref, lng_ref, lnb_ref, ys_ref, mod_next_ref, w_ref,
                       o_ref, u_ref, bg_ref, cv_ref, gl_ref, f_ref, sem):
    x2 = _combine_rows(slot_ref, x1_ref, mod_ref, lng_ref, lnb_ref, ys_ref, f_ref, sem)
    o_ref[0] = x2
    _in_body(x2, mod_next_ref, w_ref, u_ref, bg_ref, cv_ref, gl_ref)


def _combine_rows(slot_ref, x1_ref, mod_ref, lng_ref, lnb_ref, ys_ref, f_ref, sem):
    n_oct = f_ref.shape[1]
    tm = n_oct * SUBLANES
    step = pl.program_id(0) * pl.num_programs(1) + pl.program_id(1)
    n_steps = pl.num_programs(0) * pl.num_programs(1)

    def request(which, buf):
        base = which * tm

        def body(i, carry):
            for k in range(SUBLANES):
                oct_id, sub = _split_row(slot_ref[base + i * SUBLANES + k])
                pltpu.make_async_copy(ys_ref.at[oct_id, pl.ds(sub, 1), :], f_ref.at[buf, i, pl.ds(k, 1), :],
                                      sem.at[buf]).start(priority=k % 2)
            return carry

        lax.fori_loop(0, n_oct, body, 0)

    @pl.when(step == 0)
    def _():
        request(0, 0)

    @pl.when(step + 1 < n_steps)
    def _():
        request(step + 1, (step + 1) % 2)

    buf = step % 2
    pltpu.make_async_copy(ys_ref.at[pl.ds(0, n_oct)], f_ref.at[buf], sem.at[buf]).wait()
    f = f_ref[buf].reshape(tm, f_ref.shape[3])
    g2 = mod_ref[0, 5:6, :]
    return _layer_norm(DN_ALPHA * x1_ref[0] + g2 * f, lng_ref[...], lnb_ref[...])


def _combine_call(slot, x1, mod, ln_g, ln_b, ys, tm, next_in=None):
    b, s, d = x1.shape
    ys = ys.reshape(ys.shape[0] // SUBLANES, SUBLANES, d)
    row_spec = lambda n: pl.BlockSpec((1, tm, n), lambda i, j, slot: (i, j, 0))
    mod_spec = pl.BlockSpec((1, 6, d), lambda i, j, slot: (i, 0, 0))
    full = lambda shape: pl.BlockSpec(shape, lambda i, j, slot: (0,) * len(shape))
    in_specs = [row_spec(d), mod_spec, full((1, d)), full((1, d)), pl.BlockSpec(memory_space=pl.ANY)]
    out_specs = [row_spec(d)]
    out_shape = [jax.ShapeDtypeStruct((b, s, d), F32)]
    args = [slot, x1, mod, ln_g.reshape(1, -1), ln_b.reshape(1, -1), ys]
    kern, name = _combine_kernel, "moe_combine"
    if next_in is not None:
        kern, name = _combine_in_kernel, "moe_combine_in_proj"
        mod_next, w_in_all, layer = next_in
        in_specs += [mod_spec, pl.BlockSpec((1, d, D_IN), lambda i, j, slot: (layer, 0, 0))]
        args += [mod_next, w_in_all]
        out_specs += [pl.BlockSpec((D_S5 // LANE_BLK, 1, tm, LANE_BLK), lambda i, j, slot: (0, i, j, 0)),
                      row_spec(D_SC), row_spec(D_SC), row_spec(D_CF)]
        out_shape += [_lane_block_shape(b, s)] + [jax.ShapeDtypeStruct((b, s, D_SC), F32)] * 3
    grid_spec = pltpu.PrefetchScalarGridSpec(
        num_scalar_prefetch=1, grid=(b, s // tm), in_specs=in_specs, out_specs=out_specs,
        scratch_shapes=[pltpu.VMEM((2, tm // SUBLANES, SUBLANES, d), F32), pltpu.SemaphoreType.DMA((2,))])
    outs = pl.pallas_call(
        kern, grid_spec=grid_spec, out_shape=out_shape,
        compiler_params=_cparams(("arbitrary", "arbitrary")),
        name=name,
    )(*args)
    return outs[0] if next_in is None else outs


def _moe_sublayer(parts, counts, xs_init, expert_base, wg_bf, wu_bf, wd_bf, ln_g, ln_b, next_in=None):
    sizes = [p[2].shape[0] * p[2].shape[1] for p in parts]
    meta = jnp.concatenate([p[1] for p in parts], axis=1)
    slot, tile, first, second, n_used = _moe_plan(meta, counts, sum(sizes), expert_base)
    slots, start = [], 0
    for n in sizes:
        slots.append(slot[start:start + n])
        start += n
    xs = xs_init
    for (hx, _, _, _, tm), sl in zip(parts, slots):
        xs = _dispatch_call(sl, hx, xs.reshape(xs_init.shape), tm)
    ys = _moe_call(tile, first, second, n_used, xs, wg_bf, wu_bf, wd_bf)
    outs = []
    for idx, ((_, _, x1, mod, tm), sl) in enumerate(zip(parts, slots)):
        fuse = next_in if idx == len(parts) - 1 else None
        outs.append(_combine_call(sl, x1, mod, ln_g, ln_b, ys, tm, fuse))
    return outs


def kernel(x, c, ctx, c_ctx, w_mod, b_mod, w_in, s5_a_re, s5_a_im, s5_log_dt, s5_b_re, s5_b_im, s5_c_re, s5_c_im, s5_d, w_glu, b_glu, w_sc, w_dw, b_dw, ln_cf_g, ln_cf_b, w_o, ln1_g, ln1_b, w_rg, b_rg, w_rexp, b_rexp, w_gate, w_up, w_down, ln2_g, ln2_b):
    nb, seq, d = x.shape
    n_ctx = ctx.shape[1]
    n_layers = w_mod.shape[0]
    assert seq % TILE_TOK == 0 and n_ctx % TILE_TOK == 0 and seq % GRID_W == 0

    mod_rows = 16
    assert nb + 1 <= mod_rows
    c_all = jnp.concatenate([c, c_ctx[None, :], jnp.zeros((mod_rows - nb - 1, d), F32)], axis=0)
    mod_all = _mod_call(c_all, w_mod, b_mod)

    pad_r = ROUTER_LANES - N_GROUPS - N_EXPERTS
    x_lat, x_ctx = x, ctx
    mods_lat = [mod_all[l, :nb].reshape(nb, 6, d) for l in range(n_layers)]
    w_in_all = w_in.astype(BF16)
    stack = lambda w: w.astype(BF16).reshape((n_layers * N_EXPERTS,) + w.shape[2:])
    wg_bf, wu_bf, wd_bf = stack(w_gate), stack(w_up), stack(w_down)
    lat_proj = None
    for l in range(n_layers):
        last = l == n_layers - 1
        mod_lat = mods_lat[l]
        mod_ctx = jnp.broadcast_to(mod_all[l, nb].reshape(1, 6, d), (nb, 6, d))
        wglu_bf = w_glu[l].astype(BF16)
        wo_bf = w_o[l].astype(BF16)
        w_router = jnp.concatenate([w_rg[l], w_rexp[l], jnp.zeros((d, pad_r), F32)], axis=1).astype(BF16)
        b_router = jnp.concatenate([b_rg[l], b_rexp[l], jnp.zeros((pad_r,), F32)]).reshape(1, -1)
        w1, e_op, decay = _s5_operators(s5_a_re[l], s5_a_im[l], s5_log_dt[l], s5_b_re[l], s5_b_im[l],
                                        s5_c_re[l], s5_c_im[l])

        if lat_proj is None:
            u_l, bg_l, cv_l, gl_l = _in_call(x_lat, mod_lat, w_in_all, l, min(ROW_TILE_BIG, seq), False)
        else:
            u_l, bg_l, cv_l, gl_l = lat_proj
        if last:
            u_c = _in_call(x_ctx, mod_ctx, w_in_all, l, TILE_TOK, True)
        else:
            u_c, bg_c, cv_c, gl_c = _in_call(x_ctx, mod_ctx, w_in_all, l, TILE_TOK, False)

        yin_l, gf_l, gb_l = _s5a_call(u_l, w1)
        yin_c, gf_c, gb_c = _s5a_call(u_c, w1)
        hf_c, hf_l, hb_c, hb_l = _s5b_call(decay, gf_c, gf_l, gb_c, gb_l, nb)
        ypre_l = _s5c_call(yin_l, hf_l, hb_l, e_op, u_l, s5_d[l])

        ysc_l, ycf_l = _conv_call(bg_l, cv_l, gl_l, w_sc[l], w_dw[l], b_dw[l], ln_cf_g[l], ln_cf_b[l], True)
        parts = []
        counts = jnp.zeros((1, ROUTER_LANES), F32)
        n_moe = nb * seq
        if not last:
            ypre_c = _s5c_call(yin_c, hf_c, hb_c, e_op, u_c, s5_d[l])
            ysc_c, ycf_c = _conv_call(bg_c, cv_c, gl_c, w_sc[l], w_dw[l], b_dw[l], ln_cf_g[l], ln_cf_b[l], False)
            x1_c, hx_c, meta_c, counts = _out_call(ypre_c, ysc_c, ycf_c, x_ctx, mod_ctx, wglu_bf, b_glu[l], wo_bf,
                                                   ln1_g[l], ln1_b[l], w_router, b_router, counts, TILE_TOK, 0)
            parts.append((hx_c, meta_c, x1_c, mod_ctx, TILE_TOK))
            n_moe += nb * n_ctx
        x1_l, hx_l, meta_l, counts, stage = _out_call(ypre_l, ysc_l, ycf_l, x_lat, mod_lat, wglu_bf, b_glu[l], wo_bf,
                                                      ln1_g[l], ln1_b[l], w_router, b_router, counts, 512,
                                                      _sorted_rows(n_moe))
        parts.append((hx_l, meta_l, x1_l, mod_lat, min(ROW_TILE_BIG, seq)))
        next_in = None if last else (mods_lat[l + 1], w_in_all, l + 1)
        outs = _moe_sublayer(parts, counts, stage, l * N_EXPERTS, wg_bf, wu_bf, wd_bf, ln2_g[l], ln2_b[l], next_in)
        if last:
            x_lat = outs[-1]
        else:
            x_ctx = outs[0]
            x_lat, lat_proj = outs[-1][0], outs[-1][1:]
    return x_lat
```

```python
import functools
import math

import jax
import jax.numpy as jnp
from jax import lax
from jax.experimental import pallas as pl
from jax.experimental.pallas import tpu as pltpu

F32 = jnp.float32
BF16 = jnp.bfloat16

D_MODEL = 1024
DEPTH = 2
GRID_W = 64
D_S5 = 512
S5_GROUP = 16
S5_GROUPS = 32
S5_STATE = 64
D_SC = 256
D_CF = 256
CF_WIDTH = 31
CF_HALF = 15
D_IN = 1792
N_GROUPS = 4
EXP_PER_GROUP = 4
N_EXPERTS = 16
D_EXPERT = 256
DN_ALPHA = (2 * DEPTH) ** 0.25
LN_EPS = 1e-5

CHUNK = 16
N_POW = 32
PREP_GROUPS = 16
TILE_CHUNKS = 16
TILE_TOK = CHUNK * TILE_CHUNKS
LANE_BLK = 128
GRP_PER_BLK = LANE_BLK // S5_GROUP
ROUTER_LANES = 128
HX_LANES = D_MODEL + ROUTER_LANES
META_WA, META_WB, META_CLS, META_RANK = 0, 1, 2, 3
N_CLASSES = N_GROUPS * 6
MOE_TM = 512
OUT_SUBTILES = 2
ROW_TILE_BIG = 1024
SUBLANES = 8
VMEM_LIMIT = 56 * 1024 * 1024


def _cparams(sem):
    return pltpu.CompilerParams(dimension_semantics=sem, vmem_limit_bytes=VMEM_LIMIT)


def _split_bf16(a):
    hi = a.astype(BF16)
    lo = (a - hi.astype(F32)).astype(BF16)
    return hi, lo


def _dot(a, b):
    return jnp.dot(a, b, preferred_element_type=F32)


def _dot3(a, b):
    ah, al = _split_bf16(a)
    bh, bl = _split_bf16(b)
    return _dot(ah, bh) + (_dot(al, bh) + _dot(ah, bl))


def _sigmoid(x):
    return 1.0 / (1.0 + jnp.exp(-x))


def _layer_norm(x, g, b):
    mu = jnp.mean(x, axis=-1, keepdims=True)
    xc = x - mu
    var = jnp.mean(xc * xc, axis=-1, keepdims=True)
    return xc * lax.rsqrt(var + LN_EPS) * g + b


def _mod_kernel(c_ref, w_ref, b_ref, o_ref):
    c = c_ref[...]
    s = c * _sigmoid(c)
    o_ref[0] = _dot3(s, w_ref[0]) + b_ref[0]


def _mod_call(c_all, w_mod, b_mod):
    n_layers, d, n_out = w_mod.shape
    tn = 1536
    rows = c_all.shape[0]
    return pl.pallas_call(
        _mod_kernel,
        grid=(n_layers, n_out // tn),
        in_specs=[
            pl.BlockSpec((rows, d), lambda l, j: (0, 0)),
            pl.BlockSpec((1, d, tn), lambda l, j: (l, 0, j)),
            pl.BlockSpec((1, 1, tn), lambda l, j: (l, 0, j)),
        ],
        out_specs=pl.BlockSpec((1, rows, tn), lambda l, j: (l, 0, j)),
        out_shape=jax.ShapeDtypeStruct((n_layers, rows, n_out), F32),
        compiler_params=_cparams(("parallel", "parallel")),
        name="mod",
    )(c_all, w_mod, b_mod.reshape(n_layers, 1, n_out))


def _in_body(x, mod_ref, w_ref, u_ref, bg_ref, cv_ref, gl_ref):
    sh = mod_ref[0, 0:1, :]
    sc = mod_ref[0, 1:2, :]
    h = (x * (1.0 + sc) + sh).astype(BF16)
    z = _dot(h, w_ref[0])
    _store_lane_blocks(u_ref, z[:, 0:512])
    bg_ref[0] = z[:, 512:768]
    cv_ref[0] = z[:, 768:1024] * z[:, 1024:1280]
    gl_ref[0] = z[:, 1280:1536] * _sigmoid(z[:, 1536:1792])


def _in_kernel(x_ref, mod_ref, w_ref, u_ref, bg_ref, cv_ref, gl_ref):
    _in_body(x_ref[0], mod_ref, w_ref, u_ref, bg_ref, cv_ref, gl_ref)


def _in_u_kernel(x_ref, mod_ref, w_ref, u_ref):
    x = x_ref[0]
    sh = mod_ref[0, 0:1, :]
    sc = mod_ref[0, 1:2, :]
    h = (x * (1.0 + sc) + sh).astype(BF16)
    _store_lane_blocks(u_ref, _dot(h, w_ref[0]))


def _store_lane_blocks(ref, val):
    for blk in range(ref.shape[0]):
        ref[blk, 0] = val[:, blk * LANE_BLK:(blk + 1) * LANE_BLK]


def _lane_block_spec(tm):
    return pl.BlockSpec((D_S5 // LANE_BLK, 1, tm, LANE_BLK), lambda i, j: (0, i, j, 0))


def _lane_block_shape(b, s):
    return jax.ShapeDtypeStruct((D_S5 // LANE_BLK, b, s, LANE_BLK), F32)


def _in_call(x, mod, w_in_all, layer, tm, u_only):
    b, s, d = x.shape
    grid = (b, s // tm)
    row_spec = lambda n: pl.BlockSpec((1, tm, n), lambda i, j: (i, j, 0))
    in_specs = [
        row_spec(d),
        pl.BlockSpec((1, 6, d), lambda i, j: (i, 0, 0)),
    ]
    if u_only:
        in_specs.append(pl.BlockSpec((1, d, D_S5), lambda i, j: (layer, 0, 0)))
        return pl.pallas_call(
            _in_u_kernel, grid=grid, in_specs=in_specs,
            out_specs=_lane_block_spec(tm),
            out_shape=_lane_block_shape(b, s),
            compiler_params=_cparams(("parallel", "parallel")),
            name="in_proj_u",
        )(x, mod, w_in_all)
    in_specs.append(pl.BlockSpec((1, d, D_IN), lambda i, j: (layer, 0, 0)))
    return pl.pallas_call(
        _in_kernel, grid=grid, in_specs=in_specs,
        out_specs=[_lane_block_spec(tm), row_spec(D_SC), row_spec(D_SC), row_spec(D_CF)],
        out_shape=[_lane_block_shape(b, s),
                   jax.ShapeDtypeStruct((b, s, D_SC), F32),
                   jax.ShapeDtypeStruct((b, s, D_SC), F32),
                   jax.ShapeDtypeStruct((b, s, D_CF), F32)],
        compiler_params=_cparams(("parallel", "parallel")),
        name="in_proj",
    )(x, mod, w_in_all)


def _conv_tail(t, bdw_ref, lng_ref, lnb_ref):
    t = t + bdw_ref[...]
    t = _layer_norm(t, lng_ref[...], lnb_ref[...])
    return t * _sigmoid(t)


def _conv_grid_kernel(bg_ref, cv_ref, gl_ref, wsc_ref, wdw_ref, bdw_ref, lng_ref, lnb_ref,
                      ysc_ref, ycf_ref, grid_ref, t_ref):
    s = cv_ref.shape[1]
    rows = s // GRID_W
    cv = cv_ref[0]
    col = lax.broadcasted_iota(jnp.int32, (s, D_SC), 0) % GRID_W
    prev = jnp.where(col == 0, 0.0, pltpu.roll(cv, 1, axis=0))
    nxt = jnp.where(col == GRID_W - 1, 0.0, pltpu.roll(cv, s - 1, axis=0))
    conv = prev * wsc_ref[0:1, :] + cv * wsc_ref[1:2, :] + nxt * wsc_ref[2:3, :]
    ysc_ref[0] = (bg_ref[0] * conv).astype(ysc_ref.dtype)

    grid_ref[...] = gl_ref[0].reshape(rows, GRID_W, D_CF)

    def body(i, carry):
        w0 = pl.multiple_of(i * 8, 8)
        for half in range(D_CF // 128):
            lanes = slice(half * 128, (half + 1) * 128)
            acc = jnp.zeros((rows, 8, 128), F32)
            for k in range(CF_WIDTH):
                lo = max(0, CF_HALF - k)
                hi = min(rows, rows + CF_HALF - k)
                if hi <= lo:
                    continue
                term = grid_ref[lo + k - CF_HALF:hi + k - CF_HALF, pl.ds(w0, 8), lanes] * wdw_ref[k:k + 1, lanes]
                pieces = [acc[:lo]] * (lo > 0) + [acc[lo:hi] + term] + [acc[hi:]] * (hi < rows)
                acc = jnp.concatenate(pieces, axis=0) if len(pieces) > 1 else pieces[0]
            t_ref[:, pl.ds(w0, 8), lanes] = acc
        return carry

    lax.fori_loop(0, GRID_W // 8, body, 0)
    t = t_ref[...].reshape(s, D_CF)
    ycf_ref[0] = _conv_tail(t, bdw_ref, lng_ref, lnb_ref).astype(ycf_ref.dtype)


def _conv_seq_kernel(bg_ref, cv_ref, gl_ref, wsc_ref, wdw_ref, bdw_ref, lng_ref, lnb_ref,
                     ysc_ref, ycf_ref, pad_ref):
    s = cv_ref.shape[1]
    cv = cv_ref[0]
    pos = lax.broadcasted_iota(jnp.int32, (s, D_SC), 0)
    prev = jnp.where(pos == 0, 0.0, pltpu.roll(cv, 1, axis=0))
    nxt = jnp.where(pos == s - 1, 0.0, pltpu.roll(cv, s - 1, axis=0))
    conv = prev * wsc_ref[0:1, :] + cv * wsc_ref[1:2, :] + nxt * wsc_ref[2:3, :]
    ysc_ref[0] = (bg_ref[0] * conv).astype(ysc_ref.dtype)

    off = 16
    pad_ref[0:off] = jnp.zeros((off, D_CF), F32)
    pad_ref[off + s:off + s + 16] = jnp.zeros((16, D_CF), F32)
    pad_ref[off:off + s] = gl_ref[0]
    acc = jnp.zeros((s, D_CF), F32)
    for k in range(CF_WIDTH):
        acc = acc + pad_ref[pl.ds(off - CF_HALF + k, s), :] * wdw_ref[k:k + 1, :]
    ycf_ref[0] = _conv_tail(acc, bdw_ref, lng_ref, lnb_ref).astype(ycf_ref.dtype)


def _conv_call(bg, cv, gl, w_sc, w_dw, b_dw, ln_g, ln_b, grid_mode):
    b, s, _ = bg.shape
    row_spec = pl.BlockSpec((1, s, D_SC), lambda i: (i, 0, 0))
    full = lambda shape: pl.BlockSpec(shape, lambda i: (0,) * len(shape))
    if grid_mode:
        rows = s // GRID_W
        kern = _conv_grid_kernel
        scratch = [pltpu.VMEM((rows, GRID_W, D_CF), F32), pltpu.VMEM((rows, GRID_W, D_CF), F32)]
        name = "conv_grid"
    else:
        kern = _conv_seq_kernel
        scratch = [pltpu.VMEM((s + 32, D_CF), F32)]
        name = "conv_seq"
    return pl.pallas_call(
        kern, grid=(b,),
        in_specs=[row_spec, row_spec, row_spec, full((3, D_SC)), full((CF_WIDTH, D_CF)),
                  full((1, D_CF)), full((1, D_CF)), full((1, D_CF))],
        out_specs=[row_spec, row_spec],
        out_shape=[jax.ShapeDtypeStruct((b, s, D_SC), BF16), jax.ShapeDtypeStruct((b, s, D_CF), BF16)],
        scratch_shapes=scratch,
        compiler_params=_cparams(("parallel",)),
        name=name,
    )(bg, cv, gl, w_sc, w_dw, b_dw.reshape(1, -1), ln_g.reshape(1, -1), ln_b.reshape(1, -1))


def _split3(a):
    hi = a.astype(BF16)
    r = a - hi.astype(F32)
    mid = r.astype(BF16)
    lo = (r - mid.astype(F32)).astype(BF16)
    return hi, mid, lo


def _select_cols(a, sel):
    hi, mid, lo = _split3(a)
    return _dot(hi, sel) + (_dot(mid, sel) + _dot(lo, sel))


def _select_rows(sel, a):
    hi, mid, lo = _split3(a)
    return _dot(sel, hi) + (_dot(sel, mid) + _dot(sel, lo))


def _cmul(ar, ai, br, bi):
    return ar * br - ai * bi, ar * bi + ai * br


def _s5_pow_kernel(are_ref, aim_ref, ldt_ref, pr_ref, pi_ref):
    j = jnp.minimum(lax.broadcasted_iota(jnp.int32, pr_ref.shape[1:], 0), CHUNK).astype(F32)
    for d in range(2):
        dt = jnp.exp(ldt_ref[d])
        e = jnp.exp(j * (are_ref[d] * dt))
        pr_ref[d] = e * jnp.cos(j * (aim_ref[d] * dt))
        pi_ref[d] = e * jnp.sin(j * (aim_ref[d] * dt))


def _s5_pow_call(a_re, a_im, log_dt):
    g, p = S5_GROUPS, S5_STATE
    flat = lambda a: a.reshape(2, 1, g * p)
    sds = jax.ShapeDtypeStruct((2, N_POW, g * p), F32)
    pr, pi = pl.pallas_call(_s5_pow_kernel, out_shape=[sds, sds], name="s5_powers")(
        flat(a_re), flat(a_im), flat(jnp.repeat(log_dt, p, axis=1)))
    by_group = lambda a: a.reshape(2, N_POW, g, p).transpose(0, 2, 1, 3)
    return by_group(pr), by_group(pi)


def _s5_prep_kernel(are_r, aim_r, pr_ref, pi_ref, bre_ref, bim_ref, cre_ref, cim_ref,
                    w1_ref, e_ref, dec_ref):
    t, n, p = CHUNK, S5_GROUP, S5_STATE
    width = t * n
    lane_tok = lax.broadcasted_iota(jnp.int32, (N_POW, width), 1) // n
    pow_id = lax.broadcasted_iota(jnp.int32, (N_POW, width), 0)
    onehot = lambda cond: jnp.where(cond, 1.0, 0.0).astype(BF16)
    sel_fwd = onehot(pow_id == lane_tok)
    sel_rev = onehot(pow_id == t - 1 - lane_tok)
    sel_out = onehot(pow_id == t - lane_tok)
    row_tok = lax.broadcasted_iota(jnp.int32, (width, N_POW), 0) // n
    row_pow = lax.broadcasted_iota(jnp.int32, (width, N_POW), 1)
    rsel_rev = onehot(row_pow == t - 1 - row_tok)
    rsel_fwd = onehot(row_pow == row_tok)
    lane = lax.broadcasted_iota(jnp.int32, (n, width), 1)
    for gi in range(w1_ref.shape[0]):
        _s5_prep_group(gi, are_r, aim_r, pr_ref, pi_ref, bre_ref, bim_ref, cre_ref, cim_ref,
                       w1_ref, e_ref, dec_ref, (sel_fwd, sel_rev, sel_out, rsel_rev, rsel_fwd, lane))


def _s5_prep_group(gi, are_r, aim_r, pr_ref, pi_ref, bre_ref, bim_ref, cre_ref, cim_ref,
                   w1_ref, e_ref, dec_ref, selectors):
    sel_fwd, sel_rev, sel_out, rsel_rev, rsel_fwd, lane = selectors
    t, n = CHUNK, S5_GROUP
    width = t * n
    strips = []
    f_parts = []
    e_parts = []
    for d in range(2):
        pr = pr_ref[d, gi]
        pi = pi_ref[d, gi]
        qr = pr.T
        qi = pi.T
        a_re = are_r[d, gi]
        a_im = aim_r[d, gi]
        nr = pr[1:2] - 1.0
        ni = pi[1:2]
        den = a_re * a_re + a_im * a_im
        fre = (nr * a_re + ni * a_im) / den
        fim = (ni * a_re - nr * a_im) / den
        bt_re = bre_ref[d, gi].T
        bt_im = bim_ref[d, gi].T
        bb_re, bb_im = _cmul(fre, fim, bt_re, bt_im)
        ct_re = jnp.concatenate([cre_ref[d, gi].T] * t, axis=1)
        ct_im = jnp.concatenate([cim_ref[d, gi].T] * t, axis=1)
        sel = sel_fwd if d == 0 else sel_rev
        w_re, w_im = _cmul(ct_re, ct_im, _select_cols(qr, sel), _select_cols(qi, sel))
        strips.append(_dot3(bb_re, w_re) - _dot3(bb_im, w_im))
        if d == 0:
            o_re, o_im = _cmul(w_re, w_im, qr[:, 1:2], qi[:, 1:2])
        else:
            o_re, o_im = _cmul(ct_re, ct_im, _select_cols(qr, sel_out), _select_cols(qi, sel_out))
        e_parts += [o_re, -o_im]
        rsel = rsel_rev if d == 0 else rsel_fwd
        f_re, f_im = _cmul(jnp.concatenate([bb_re] * t, axis=0), jnp.concatenate([bb_im] * t, axis=0),
                           _select_rows(rsel, pr), _select_rows(rsel, pi))
        f_parts += [f_re, f_im]
        dec_ref[gi, 2 * d:2 * d + 1, :] = jnp.concatenate([pr[t:t + 1], pr[t:t + 1]], axis=1)
        dec_ref[gi, 2 * d + 1:2 * d + 2, :] = jnp.concatenate([-pi[t:t + 1], pi[t:t + 1]], axis=1)

    blocks = []
    for s in range(t):
        fwd = strips[0] if s == 0 else jnp.where(lane >= n * s, pltpu.roll(strips[0], n * s, axis=1), 0.0)
        back = t - 1 - s
        bwd = strips[1] if back == 0 else jnp.where(lane < width - n * back,
                                                     pltpu.roll(strips[1], width - n * back, axis=1), 0.0)
        blocks.append(fwd + bwd)
    m = jnp.concatenate(blocks, axis=0)
    w1_ref[gi] = jnp.concatenate([m] + f_parts, axis=1).astype(BF16)
    e_ref[gi] = jnp.concatenate(e_parts, axis=0).astype(BF16)


def _s5_operators(a_re, a_im, log_dt, b_re, b_im, c_re, c_im):
    g, p, n, t = S5_GROUPS, S5_STATE, S5_GROUP, CHUNK
    pr, pi = _s5_pow_call(a_re, a_im, log_dt)
    gb = PREP_GROUPS
    spec = lambda shape: pl.BlockSpec((2, gb) + shape, lambda i: (0, i) + (0,) * len(shape))
    return pl.pallas_call(
        _s5_prep_kernel, grid=(g // gb,),
        in_specs=[spec((1, p)), spec((1, p)), spec((N_POW, p)), spec((N_POW, p)),
                  spec((p, n)), spec((p, n)), spec((n, p)), spec((n, p))],
        out_specs=[pl.BlockSpec((gb, t * n, 2 * t * n), lambda i: (i, 0, 0)),
                   pl.BlockSpec((gb, 4 * p, t * n), lambda i: (i, 0, 0)),
                   pl.BlockSpec((gb, 4, 2 * p), lambda i: (i, 0, 0))],
        out_shape=[jax.ShapeDtypeStruct((g, t * n, 2 * t * n), BF16),
                   jax.ShapeDtypeStruct((g, 4 * p, t * n), BF16),
                   jax.ShapeDtypeStruct((g, 4, 2 * p), F32)],
        compiler_params=_cparams(("parallel",)),
        name="s5_prep",
    )(a_re.reshape(2, g, 1, p), a_im.reshape(2, g, 1, p), pr, pi, b_re, b_im, c_re, c_im)


def _block_transpose8(ps):
    ps = list(ps)
    blk = lax.broadcasted_iota(jnp.int32, ps[0].shape, 1) // S5_GROUP
    for k in range(3):
        step = 1 << k
        shift = S5_GROUP * step
        keep = ((blk >> k) & 1) == 0
        for a in range(8):
            if a & step:
                continue
            pa, pb = ps[a], ps[a + step]
            ps[a] = jnp.where(keep, pa, pltpu.roll(pb, shift, axis=1))
            ps[a + step] = jnp.where(keep, pltpu.roll(pa, 128 - shift, axis=1), pb)
    return ps


def _s5a_kernel(u_ref, w1_ref, yin_ref, gf_ref, gb_ref):
    nb = u_ref.shape[1]
    xs = []
    for s in range(CHUNK):
        parts = [u_ref[0, b, pl.ds(s, TILE_CHUNKS, stride=CHUNK), :] for b in range(nb)]
        xs.append(jnp.concatenate(parts, axis=0))
    lo = _block_transpose8(xs[:8])
    hi = _block_transpose8(xs[8:])
    for j in range(GRP_PER_BLK):
        og = jnp.concatenate([lo[j], hi[j]], axis=1)
        r = _dot(og.astype(BF16), w1_ref[j])
        yin_ref[j] = r[:, 0:256].astype(yin_ref.dtype)
        gf_ref[j] = r[:, 256:384]
        gb_ref[j] = r[:, 384:512]


def _s5a_call(u, w1):
    n_blk, b, s, _ = u.shape
    nt = s // TILE_TOK
    rows = b * TILE_CHUNKS
    out_spec = lambda n: pl.BlockSpec((GRP_PER_BLK, rows, n), lambda l, j: (l, j, 0))
    return pl.pallas_call(
        _s5a_kernel, grid=(n_blk, nt),
        in_specs=[pl.BlockSpec((1, b, TILE_TOK, LANE_BLK), lambda l, j: (l, 0, j, 0)),
                  pl.BlockSpec((GRP_PER_BLK, 256, 512), lambda l, j: (l, 0, 0))],
        out_specs=[out_spec(256), out_spec(128), out_spec(128)],
        out_shape=[jax.ShapeDtypeStruct((S5_GROUPS, nt * rows, 256), BF16),
                   jax.ShapeDtypeStruct((S5_GROUPS, nt * rows, 128), F32),
                   jax.ShapeDtypeStruct((S5_GROUPS, nt * rows, 128), F32)],
        compiler_params=_cparams(("parallel", "parallel")),
        name="s5_chunk_in",
    )(u, w1)


def _s5b_kernel(nb, a_ref, gfc_ref, gfl_ref, gbc_ref, gbl_ref, hfc_ref, hfl_ref, hbc_ref, hbl_ref):
    gb = a_ref.shape[0]
    rows = nb * TILE_CHUNKS
    n_lat = gfl_ref.shape[1] // rows
    a1f = [jnp.broadcast_to(a_ref[g, 0:1, :], (nb, 128)) for g in range(gb)]
    a2f = [jnp.broadcast_to(a_ref[g, 1:2, :], (nb, 128)) for g in range(gb)]
    a1b = [jnp.broadcast_to(a_ref[g, 2:3, :], (nb, 128)) for g in range(gb)]
    a2b = [jnp.broadcast_to(a_ref[g, 3:4, :], (nb, 128)) for g in range(gb)]

    def step(state, a1, a2, g_ref, h_ref, g, row):
        h, hs = state
        h_ref[g, row, :] = h
        inp = g_ref[g, row, :]
        return a1 * h + a2 * hs + inp, a1 * hs - a2 * h + pltpu.roll(inp, 64, axis=1)

    zero = jnp.zeros((nb, 128), F32)
    hf = [(zero, zero) for _ in range(gb)]
    hb = [(zero, zero) for _ in range(gb)]
    for ci in range(TILE_CHUNKS):
        rf = pl.ds(ci, nb, stride=TILE_CHUNKS)
        rb = pl.ds(TILE_CHUNKS - 1 - ci, nb, stride=TILE_CHUNKS)
        for g in range(gb):
            hf[g] = step(hf[g], a1f[g], a2f[g], gfc_ref, hfc_ref, g, rf)
            hb[g] = step(hb[g], a1b[g], a2b[g], gbc_ref, hbc_ref, g, rb)

    def body(j, carry):
        hf, hb = carry
        hf = list(hf)
        hb = list(hb)
        base_f = j * rows
        base_b = (n_lat - 1 - j) * rows
        for ci in range(TILE_CHUNKS):
            rf = pl.ds(base_f + ci, nb, stride=TILE_CHUNKS)
            rb = pl.ds(base_b + (TILE_CHUNKS - 1 - ci), nb, stride=TILE_CHUNKS)
            for g in range(gb):
                hf[g] = step(hf[g], a1f[g], a2f[g], gfl_ref, hfl_ref, g, rf)
                hb[g] = step(hb[g], a1b[g], a2b[g], gbl_ref, hbl_ref, g, rb)
        return tuple(hf), tuple(hb)

    lax.fori_loop(0, n_lat, body, (tuple(hf), tuple(hb)))


def _s5b_call(decay, gf_c, gf_l, gb_c, gb_l, nb):
    gblk = 8
    spec = lambda a: pl.BlockSpec((gblk, a.shape[1], 128), lambda i: (i, 0, 0))
    sds = lambda a: jax.ShapeDtypeStruct(a.shape, F32)
    return pl.pallas_call(
        functools.partial(_s5b_kernel, nb), grid=(S5_GROUPS // gblk,),
        in_specs=[pl.BlockSpec((gblk, 4, 128), lambda i: (i, 0, 0)),
                  spec(gf_c), spec(gf_l), spec(gb_c), spec(gb_l)],
        out_specs=[spec(gf_c), spec(gf_l), spec(gb_c), spec(gb_l)],
        out_shape=[sds(gf_c), sds(gf_l), sds(gb_c), sds(gb_l)],
        compiler_params=_cparams(("parallel",)),
        name="s5_state_scan",
    )(decay, gf_c, gf_l, gb_c, gb_l)


def _s5c_kernel(yin_ref, hf_ref, hb_ref, e_ref, u_ref, d_ref, y_ref):
    nb = u_ref.shape[1]
    ys = []
    for j in range(GRP_PER_BLK):
        h = jnp.concatenate([hf_ref[j], hb_ref[j]], axis=1).astype(BF16)
        ys.append(yin_ref[j].astype(F32) + _dot(h, e_ref[j]))
    at = (_block_transpose8([y[:, :128] for y in ys])
          + _block_transpose8([y[:, 128:] for y in ys]))
    d = d_ref[...]
    for t in range(CHUNK):
        for b in range(nb):
            rows = pl.ds(t, TILE_CHUNKS, stride=CHUNK)
            y_ref[0, b, rows, :] = at[t][b * TILE_CHUNKS:(b + 1) * TILE_CHUNKS] + d * u_ref[0, b, rows, :]


def _s5c_call(yin, hf, hb, e, u, d_skip):
    n_blk, b, s, _ = u.shape
    nt = s // TILE_TOK
    rows = b * TILE_CHUNKS
    gspec = lambda n: pl.BlockSpec((GRP_PER_BLK, rows, n), lambda l, j: (l, j, 0))
    tok_spec = pl.BlockSpec((1, b, TILE_TOK, LANE_BLK), lambda l, j: (l, 0, j, 0))
    return pl.pallas_call(
        _s5c_kernel, grid=(n_blk, nt),
        in_specs=[gspec(256), gspec(128), gspec(128),
                  pl.BlockSpec((GRP_PER_BLK, 256, 256), lambda l, j: (l, 0, 0)),
                  tok_spec,
                  pl.BlockSpec((1, LANE_BLK), lambda l, j: (0, l))],
        out_specs=tok_spec,
        out_shape=_lane_block_shape(b, s),
        compiler_params=_cparams(("parallel", "parallel")),
        name="s5_chunk_out",
    )(yin, hf, hb, e, u, d_skip.reshape(1, D_S5))


def _gelu_tanh(x):
    return 0.5 * x * (1.0 + jnp.tanh(math.sqrt(2.0 / math.pi) * (x + 0.044715 * (x * x * x))))


def _route(logits):
    lane = lax.broadcasted_iota(jnp.int32, logits.shape, 1).astype(F32)
    neg = jnp.float32(-1e30)
    big = jnp.float32(1e9)
    gl = jnp.where(lane < N_GROUPS, logits, neg)
    gmax = jnp.max(gl, axis=1, keepdims=True)
    gidx = jnp.min(jnp.where(gl == gmax, lane, big), axis=1, keepdims=True)
    gsum = jnp.sum(jnp.exp(gl - gmax), axis=1, keepdims=True)
    gw = 1.0 / gsum
    lo = N_GROUPS + EXP_PER_GROUP * gidx
    el = jnp.where((lane >= lo) & (lane < lo + EXP_PER_GROUP), logits, neg)
    v1 = jnp.max(el, axis=1, keepdims=True)
    i1 = jnp.min(jnp.where(el == v1, lane, big), axis=1, keepdims=True)
    el2 = jnp.where(lane == i1, neg, el)
    v2 = jnp.max(el2, axis=1, keepdims=True)
    i2 = jnp.min(jnp.where(el2 == v2, lane, big), axis=1, keepdims=True)
    ex = jnp.exp(v2 - v1)
    p1 = 1.0 / (1.0 + ex)
    p2 = ex * p1
    e1 = i1 - lo
    e2 = i2 - lo
    first = e1 < e2
    ea = jnp.where(first, e1, e2)
    eb = jnp.where(first, e2, e1)
    wa = gw * jnp.where(first, p1, p2)
    wb = gw * jnp.where(first, p2, p1)
    pair = ea * (7.0 - ea) * 0.5 + (eb - ea - 1.0)
    return wa, wb, 6.0 * gidx + pair


def _out_kernel(ypre_ref, ysc_ref, ycf_ref, x_ref, mod_ref, wglu_ref, bglu_ref, wo_ref,
                lng_ref, lnb_ref, wr_ref, br_ref, cnt0_ref, x1_ref, hx_ref, meta_ref, counts_ref, *rest):
    stage_ref = rest[0] if len(rest) == 2 else None
    cnt_ref = rest[-1]

    @pl.when((pl.program_id(0) == 0) & (pl.program_id(1) == 0))
    def _():
        cnt_ref[...] = cnt0_ref[...]

    tm = x_ref.shape[1]
    sub = tm // OUT_SUBTILES

    def row_chain(r0):
        rows = slice(r0, r0 + sub)
        ypre = jnp.concatenate([ypre_ref[blk, 0, rows, :] for blk in range(ypre_ref.shape[0])], axis=1)
        t = _gelu_tanh(ypre)
        gate = _sigmoid(_dot(t.astype(BF16), wglu_ref[...]) + bglu_ref[...])
        ys5 = (t * gate).astype(BF16)
        y = (_dot(ys5, wo_ref[0:D_S5, :]) + _dot(ysc_ref[0, rows, :], wo_ref[D_S5:D_S5 + D_SC, :])
             + _dot(ycf_ref[0, rows, :], wo_ref[D_S5 + D_SC:D_MODEL, :]))
        g1 = mod_ref[0, 2:3, :]
        x1 = _layer_norm(DN_ALPHA * x_ref[0, rows, :] + g1 * y, lng_ref[...], lnb_ref[...])
        x1_ref[0, rows, :] = x1
        h2 = x1 * (1.0 + mod_ref[0, 4:5, :]) + mod_ref[0, 3:4, :]
        hx_ref[0, rows, 0:D_MODEL] = h2
        return _route(_dot(h2.astype(BF16), wr_ref[...]) + br_ref[...])

    routed = [row_chain(r0) for r0 in range(0, tm, sub)]
    wa, wb, cls = (jnp.concatenate([r[i] for r in routed], axis=0) for i in range(3))

    lane = lax.broadcasted_iota(jnp.int32, (tm, ROUTER_LANES), 1).astype(F32)
    onehot = jnp.where(lane == cls, 1.0, 0.0)
    row_i = lax.broadcasted_iota(jnp.int32, (tm, tm), 0)
    col_i = lax.broadcasted_iota(jnp.int32, (tm, tm), 1)
    earlier = jnp.where(col_i < row_i, 1.0, 0.0).astype(BF16)
    before = _dot(earlier, onehot.astype(BF16)) + cnt_ref[...]
    rank = jnp.sum(before * onehot, axis=1, keepdims=True)
    cnt_ref[...] += jnp.sum(onehot, axis=0, keepdims=True)
    counts_ref[...] = cnt_ref[...]

    meta = (jnp.where(lane == META_WA, wa, 0.0) + jnp.where(lane == META_WB, wb, 0.0)
            + jnp.where(lane == META_CLS, cls, 0.0) + jnp.where(lane == META_RANK, rank, 0.0))
    meta_ref[...] = jnp.transpose(meta)[0:SUBLANES, :]
    hx_ref[0, :, D_MODEL:HX_LANES] = meta
    if stage_ref is not None:
        stage_ref[...] = jnp.zeros_like(stage_ref)


def _out_call(ypre, ysc, ycf, x, mod, wglu_bf, b_glu, wo_bf, ln_g, ln_b, w_router, b_router, counts0, tm,
              stage_rows):
    b, s, d = x.shape
    nt = s // tm
    row_spec = lambda n: pl.BlockSpec((1, tm, n), lambda i, j: (i, j, 0))
    full = lambda shape: pl.BlockSpec(shape, lambda i, j: (0,) * len(shape))
    out_specs = [row_spec(d), row_spec(HX_LANES),
                 pl.BlockSpec((SUBLANES, tm), lambda i, j: (0, i * nt + j)),
                 full((1, ROUTER_LANES))]
    out_shape = [jax.ShapeDtypeStruct((b, s, d), F32), jax.ShapeDtypeStruct((b, s, HX_LANES), F32),
                 jax.ShapeDtypeStruct((SUBLANES, b * s), F32),
                 jax.ShapeDtypeStruct((1, ROUTER_LANES), F32)]
    if stage_rows:
        stage_octs = stage_rows // (SUBLANES * b * nt)
        assert stage_octs * SUBLANES * b * nt == stage_rows
        out_specs.append(pl.BlockSpec((stage_octs, SUBLANES, HX_LANES), lambda i, j: (i * nt + j, 0, 0)))
        out_shape.append(jax.ShapeDtypeStruct((stage_rows // SUBLANES, SUBLANES, HX_LANES), F32))
    return pl.pallas_call(
        _out_kernel, grid=(b, nt),
        in_specs=[_lane_block_spec(tm), row_spec(D_SC), row_spec(D_CF), row_spec(d),
                  pl.BlockSpec((1, 6, d), lambda i, j: (i, 0, 0)),
                  full((D_S5, D_S5)), full((1, D_S5)), full((d, d)),
                  full((1, d)), full((1, d)), full((d, ROUTER_LANES)), full((1, ROUTER_LANES)),
                  full((1, ROUTER_LANES))],
        out_specs=out_specs, out_shape=out_shape,
        scratch_shapes=[pltpu.VMEM((1, ROUTER_LANES), F32)],
        compiler_params=_cparams(("arbitrary", "arbitrary")),
        name="out_proj",
    )(ypre, ysc, ycf, x, mod, wglu_bf, b_glu.reshape(1, -1), wo_bf, ln_g.reshape(1, -1),
      ln_b.reshape(1, -1), w_router, b_router, counts0)


def _sorted_rows(n_tok):
    return n_tok + N_CLASSES * MOE_TM


def _moe_plan(meta, counts, n_tok, expert_base):
    cls = meta[META_CLS].astype(jnp.int32)
    rank = meta[META_RANK].astype(jnp.int32)
    cnt = counts[0, :N_CLASSES].astype(jnp.int32)
    n_tiles = (cnt + (MOE_TM - 1)) // MOE_TM
    ends = jnp.cumsum(n_tiles)
    starts = ends - n_tiles
    class_ids = jnp.arange(N_CLASSES, dtype=jnp.int32)
    first_row = jnp.sum(jnp.where(cls[:, None] == class_ids[None, :], starts[None, :] * MOE_TM, 0), axis=1)
    slot = first_row + rank
    t_max = n_tok // MOE_TM + N_CLASSES
    n_used = ends[N_CLASSES - 1]
    tile = jnp.minimum(jnp.arange(t_max, dtype=jnp.int32), n_used - 1)
    tile_cls = jnp.sum((tile[:, None] >= ends[None, :]).astype(jnp.int32), axis=1)
    group = tile_cls // 6
    pair = tile_cls % 6
    first = jnp.array([0, 0, 0, 1, 1, 2], jnp.int32)[pair] + EXP_PER_GROUP * group + expert_base
    second = jnp.array([1, 2, 3, 2, 3, 3], jnp.int32)[pair] + EXP_PER_GROUP * group + expert_base
    return slot, tile, first, second, n_used.reshape(1)


def _split_row(row):
    return lax.shift_right_logical(row, 3), lax.bitwise_and(row, SUBLANES - 1)


def _dispatch_kernel(slot_ref, hx_ref, xs_init_ref, xs_ref, sem):
    del xs_init_ref
    n_oct = hx_ref.shape[1]
    base = (pl.program_id(0) * pl.num_programs(1) + pl.program_id(1)) * (n_oct * SUBLANES)

    def body(i, carry):
        for k in range(SUBLANES):
            oct_id, sub = _split_row(slot_ref[base + i * SUBLANES + k])
            pltpu.make_async_copy(hx_ref.at[0, i, pl.ds(k, 1), :], xs_ref.at[oct_id, pl.ds(sub, 1), :],
                                  sem).start(priority=k % 2)
        return carry

    lax.fori_loop(0, n_oct, body, 0)
    pltpu.make_async_copy(hx_ref.at[0], xs_ref.at[pl.ds(0, n_oct)], sem).wait()


def _dispatch_call(slot, hx, xs_init, tm):
    b, s, w = hx.shape
    n_rows = xs_init.shape[0] * SUBLANES
    grid_spec = pltpu.PrefetchScalarGridSpec(
        num_scalar_prefetch=1, grid=(b, s // tm),
        in_specs=[pl.BlockSpec((1, tm // SUBLANES, SUBLANES, w), lambda i, j, slot: (i, j, 0, 0)),
                  pl.BlockSpec(memory_space=pl.ANY)],
        out_specs=pl.BlockSpec(memory_space=pl.ANY),
        scratch_shapes=[pltpu.SemaphoreType.DMA(())])
    xs = pl.pallas_call(
        _dispatch_kernel, grid_spec=grid_spec,
        out_shape=jax.ShapeDtypeStruct(xs_init.shape, F32),
        input_output_aliases={2: 0},
        compiler_params=_cparams(("arbitrary", "arbitrary")),
        name="moe_dispatch",
    )(slot, hx.reshape(b, s // SUBLANES, SUBLANES, w), xs_init)
    return xs.reshape(n_rows, w)


def _moe_kernel(tile_ref, first_ref, second_ref, nused_ref, xs_ref, wga_ref, wgb_ref, wua_ref, wub_ref,
                wda_ref, wdb_ref, ys_ref):
    del tile_ref, first_ref, second_ref
    t = pl.program_id(0)
    n_used = nused_ref[0]

    @pl.when(t < n_used)
    def _():
        x = xs_ref[...]
        xb = x[:, 0:D_MODEL].astype(BF16)

        def expert(wg_ref, wu_ref, wd_ref, w):
            gate = _dot(xb, wg_ref[0])
            up = _dot(xb, wu_ref[0])
            act = gate * _sigmoid(gate) * up * w
            return _dot(act.astype(BF16), wd_ref[0])

        wa = x[:, D_MODEL + META_WA:D_MODEL + META_WA + 1]
        wb = x[:, D_MODEL + META_WB:D_MODEL + META_WB + 1]
        ys_ref[...] = expert(wga_ref, wua_ref, wda_ref, wa) + expert(wgb_ref, wub_ref, wdb_ref, wb)

    @pl.when(t >= n_used)
    def _():
        ys_ref[...] = jnp.zeros_like(ys_ref)


def _moe_call(tile, first, second, n_used, xs, wg_bf, wu_bf, wd_bf):
    n_rows, w = xs.shape
    d = D_MODEL
    t_max = tile.shape[0]
    up_spec = lambda sel: pl.BlockSpec((1, d, D_EXPERT), lambda t, tl, fi, se, nu: ((fi, se)[sel][t], 0, 0))
    down_spec = lambda sel: pl.BlockSpec((1, D_EXPERT, d), lambda t, tl, fi, se, nu: ((fi, se)[sel][t], 0, 0))
    grid_spec = pltpu.PrefetchScalarGridSpec(
        num_scalar_prefetch=4, grid=(t_max,),
        in_specs=[pl.BlockSpec((MOE_TM, w), lambda t, tl, fi, se, nu: (tl[t], 0)),
                  up_spec(0), up_spec(1), up_spec(0), up_spec(1), down_spec(0), down_spec(1)],
        out_specs=pl.BlockSpec((MOE_TM, d), lambda t, tl, fi, se, nu: (t, 0)))
    return pl.pallas_call(
        _moe_kernel, grid_spec=grid_spec,
        out_shape=jax.ShapeDtypeStruct((n_rows, d), F32),
        compiler_params=_cparams(("arbitrary",)),
        name="moe_experts",
    )(tile, first, second, n_used, xs, wg_bf, wg_bf, wu_bf, wu_bf, wd_bf, wd_bf)


def _combine_kernel(slot_ref, x1_ref, mod_ref, lng_ref, lnb_ref, ys_ref, o_ref, f_ref, sem):
    o_ref[0] = _combine_rows(slot_ref, x1_ref, mod_ref, lng_ref, lnb_ref, ys_ref, f_ref, sem)


def _combine_in_kernel(slot_ref, x1_ref, mod_ref, lng_ref, lnb_ref, ys_ref, mod_next_ref, w_ref,
                       o_ref, u_ref, bg_ref, cv_ref, gl_ref, f_ref, sem):
    x2 = _combine_rows(slot_ref, x1_ref, mod_ref, lng_ref, lnb_ref, ys_ref, f_ref, sem)
    o_ref[0] = x2
    _in_body(x2, mod_next_ref, w_ref, u_ref, bg_ref, cv_ref, gl_ref)


def _combine_rows(slot_ref, x1_ref, mod_ref, lng_ref, lnb_ref, ys_ref, f_ref, sem):
    n_oct = f_ref.shape[1]
    tm = n_oct * SUBLANES
    step = pl.program_id(0) * pl.num_programs(1) + pl.program_id(1)
    n_steps = pl.num_programs(0) * pl.num_programs(1)

    def request(which, buf):
        base = which * tm

        def body(i, carry):
            for k in range(SUBLANES):
                oct_id, sub = _split_row(slot_ref[base + i * SUBLANES + k])
                pltpu.make_async_copy(ys_ref.at[oct_id, pl.ds(sub, 1), :], f_ref.at[buf, i, pl.ds(k, 1), :],
                                      sem.at[buf]).start(priority=k % 2)
            return carry

        lax.fori_loop(0, n_oct, body, 0)

    @pl.when(step == 0)
    def _():
        request(0, 0)

    @pl.when(step + 1 < n_steps)
    def _():
        request(step + 1, (step + 1) % 2)

    buf = step % 2
    pltpu.make_async_copy(ys_ref.at[pl.ds(0, n_oct)], f_ref.at[buf], sem.at[buf]).wait()
    f = f_ref[buf].reshape(tm, f_ref.shape[3])
    g2 = mod_ref[0, 5:6, :]
    return _layer_norm(DN_ALPHA * x1_ref[0] + g2 * f, lng_ref[...], lnb_ref[...])


def _combine_call(slot, x1, mod, ln_g, ln_b, ys, tm, next_in=None):
    b, s, d = x1.shape
    ys = ys.reshape(ys.shape[0] // SUBLANES, SUBLANES, d)
    row_spec = lambda n: pl.BlockSpec((1, tm, n), lambda i, j, slot: (i, j, 0))
    mod_spec = pl.BlockSpec((1, 6, d), lambda i, j, slot: (i, 0, 0))
    full = lambda shape: pl.BlockSpec(shape, lambda i, j, slot: (0,) * len(shape))
    in_specs = [row_spec(d), mod_spec, full((1, d)), full((1, d)), pl.BlockSpec(memory_space=pl.ANY)]
    out_specs = [row_spec(d)]
    out_shape = [jax.ShapeDtypeStruct((b, s, d), F32)]
    args = [slot, x1, mod, ln_g.reshape(1, -1), ln_b.reshape(1, -1), ys]
    kern, name = _combine_kernel, "moe_combine"
    if next_in is not None:
        kern, name = _combine_in_kernel, "moe_combine_in_proj"
        mod_next, w_in_all, layer = next_in
        in_specs += [mod_spec, pl.BlockSpec((1, d, D_IN), lambda i, j, slot: (layer, 0, 0))]
        args += [mod_next, w_in_all]
        out_specs += [pl.BlockSpec((D_S5 // LANE_BLK, 1, tm, LANE_BLK), lambda i, j, slot: (0, i, j, 0)),
                      row_spec(D_SC), row_spec(D_SC), row_spec(D_CF)]
        out_shape += [_lane_block_shape(b, s)] + [jax.ShapeDtypeStruct((b, s, D_SC), F32)] * 3
    grid_spec = pltpu.PrefetchScalarGridSpec(
        num_scalar_prefetch=1, grid=(b, s // tm), in_specs=in_specs, out_specs=out_specs,
        scratch_shapes=[pltpu.VMEM((2, tm // SUBLANES, SUBLANES, d), F32), pltpu.SemaphoreType.DMA((2,))])
    outs = pl.pallas_call(
        kern, grid_spec=grid_spec, out_shape=out_shape,
        compiler_params=_cparams(("arbitrary", "arbitrary")),
        name=name,
    )(*args)
    return outs[0] if next_in is None else outs


def _moe_sublayer(parts, counts, xs_init, expert_base, wg_bf, wu_bf, wd_bf, ln_g, ln_b, next_in=None):
    sizes = [p[2].shape[0] * p[2].shape[1] for p in parts]
    meta = jnp.concatenate([p[1] for p in parts], axis=1)
    slot, tile, first, second, n_used = _moe_plan(meta, counts, sum(sizes), expert_base)
    slots, start = [], 0
    for n in sizes:
        slots.append(slot[start:start + n])
        start += n
    xs = xs_init
    for (hx, _, _, _, tm), sl in zip(parts, slots):
        xs = _dispatch_call(sl, hx, xs.reshape(xs_init.shape), tm)
    ys = _moe_call(tile, first, second, n_used, xs, wg_bf, wu_bf, wd_bf)
    outs = []
    for idx, ((_, _, x1, mod, tm), sl) in enumerate(zip(parts, slots)):
        fuse = next_in if idx == len(parts) - 1 else None
        outs.append(_combine_call(sl, x1, mod, ln_g, ln_b, ys, tm, fuse))
    return outs


def kernel(x, c, ctx, c_ctx, w_mod, b_mod, w_in, s5_a_re, s5_a_im, s5_log_dt, s5_b_re, s5_b_im, s5_c_re, s5_c_im, s5_d, w_glu, b_glu, w_sc, w_dw, b_dw, ln_cf_g, ln_cf_b, w_o, ln1_g, ln1_b, w_rg, b_rg, w_rexp, b_rexp, w_gate, w_up, w_down, ln2_g, ln2_b):
    nb, seq, d = x.shape
    n_ctx = ctx.shape[1]
    n_layers = w_mod.shape[0]
    assert seq % TILE_TOK == 0 and n_ctx % TILE_TOK == 0 and seq % GRID_W == 0

    mod_rows = 16
    assert nb + 1 <= mod_rows
    c_all = jnp.concatenate([c, c_ctx[None, :], jnp.zeros((mod_rows - nb - 1, d), F32)], axis=0)
    mod_all = _mod_call(c_all, w_mod, b_mod)

    pad_r = ROUTER_LANES - N_GROUPS - N_EXPERTS
    x_lat, x_ctx = x, ctx
    mods_lat = [mod_all[l, :nb].reshape(nb, 6, d) for l in range(n_layers)]
    w_in_all = w_in.astype(BF16)
    stack = lambda w: w.astype(BF16).reshape((n_layers * N_EXPERTS,) + w.shape[2:])
    wg_bf, wu_bf, wd_bf = stack(w_gate), stack(w_up), stack(w_down)
    lat_proj = None
    for l in range(n_layers):
        last = l == n_layers - 1
        mod_lat = mods_lat[l]
        mod_ctx = jnp.broadcast_to(mod_all[l, nb].reshape(1, 6, d), (nb, 6, d))
        wglu_bf = w_glu[l].astype(BF16)
        wo_bf = w_o[l].astype(BF16)
        w_router = jnp.concatenate([w_rg[l], w_rexp[l], jnp.zeros((d, pad_r), F32)], axis=1).astype(BF16)
        b_router = jnp.concatenate([b_rg[l], b_rexp[l], jnp.zeros((pad_r,), F32)]).reshape(1, -1)
        w1, e_op, decay = _s5_operators(s5_a_re[l], s5_a_im[l], s5_log_dt[l], s5_b_re[l], s5_b_im[l],
                                        s5_c_re[l], s5_c_im[l])

        if lat_proj is None:
            u_l, bg_l, cv_l, gl_l = _in_call(x_lat, mod_lat, w_in_all, l, min(ROW_TILE_BIG, seq), False)
        else:
            u_l, bg_l, cv_l, gl_l = lat_proj
        if last:
            u_c = _in_call(x_ctx, mod_ctx, w_in_all, l, TILE_TOK, True)
        else:
            u_c, bg_c, cv_c, gl_c = _in_call(x_ctx, mod_ctx, w_in_all, l, TILE_TOK, False)

        yin_l, gf_l, gb_l = _s5a_call(u_l, w1)
        yin_c, gf_c, gb_c = _s5a_call(u_c, w1)
        hf_c, hf_l, hb_c, hb_l = _s5b_call(decay, gf_c, gf_l, gb_c, gb_l, nb)
        ypre_l = _s5c_call(yin_l, hf_l, hb_l, e_op, u_l, s5_d[l])

        ysc_l, ycf_l = _conv_call(bg_l, cv_l, gl_l, w_sc[l], w_dw[l], b_dw[l], ln_cf_g[l], ln_cf_b[l], True)
        parts = []
        counts = jnp.zeros((1, ROUTER_LANES), F32)
        n_moe = nb * seq
        if not last:
            ypre_c = _s5c_call(yin_c, hf_c, hb_c, e_op, u_c, s5_d[l])
            ysc_c, ycf_c = _conv_call(bg_c, cv_c, gl_c, w_sc[l], w_dw[l], b_dw[l], ln_cf_g[l], ln_cf_b[l], False)
            x1_c, hx_c, meta_c, counts = _out_call(ypre_c, ysc_c, ycf_c, x_ctx, mod_ctx, wglu_bf, b_glu[l], wo_bf,
                                                   ln1_g[l], ln1_b[l], w_router, b_router, counts, TILE_TOK, 0)
            parts.append((hx_c, meta_c, x1_c, mod_ctx, TILE_TOK))
            n_moe += nb * n_ctx
        x1_l, hx_l, meta_l, counts, stage = _out_call(ypre_l, ysc_l, ycf_l, x_lat, mod_lat, wglu_bf, b_glu[l], wo_bf,
                                                      ln1_g[l], ln1_b[l], w_router, b_router, counts, 512,
                                                      _sorted_rows(n_moe))
        parts.append((hx_l, meta_l, x1_l, mod_lat, min(ROW_TILE_BIG, seq)))
        next_in = None if last else (mods_lat[l + 1], w_in_all, l + 1)
        outs = _moe_sublayer(parts, counts, stage, l * N_EXPERTS, wg_bf, wu_bf, wd_bf, ln2_g[l], ln2_b[l], next_in)
        if last:
            x_lat = outs[-1]
        else:
            x_ctx = outs[0]
            x_lat, lat_proj = outs[-1][0], outs[-1][1:]
    return x_lat
```
